```python
import math
import jax, jax.numpy as jnp
from jax import lax
import numpy as np

D_MODEL = 1024
BATCH = 16
SEQ = 4096
DEPTH = 2

CONV_WIDTH = D_MODEL // 2
CONV_K = 31
DN_DK = 128
DN_DV = 128
DN_HEADS = D_MODEL // 128
DN_SHORT_K = 4
DN_CHUNK = 64
SG_WIDTH = D_MODEL // 2
SG_GROUPS = 4
SG_CHUNK = 128
N_BRANCH = 3
NORM_EPS = 1e-6

SPLIT_SIZES = (
    CONV_WIDTH,
    CONV_WIDTH,
    CONV_WIDTH,
    DN_HEADS * (2 * DN_DK + DN_DV),
    DN_HEADS * DN_DV,
    DN_HEADS,
    DN_HEADS,
    SG_WIDTH,
    SG_WIDTH,
    SG_WIDTH,
    N_BRANCH * D_MODEL,
)
N_IN_COLS = sum(SPLIT_SIZES)

kernel_name = "hybrid_conv_deltanet_gmlp_gated_merge"


def _rmsnorm(x, g):
    xf = x.astype(jnp.float32)
    y = xf * lax.rsqrt(jnp.mean(xf * xf, axis=-1, keepdims=True) + NORM_EPS)
    return (y * g.astype(jnp.float32)).astype(x.dtype)


def _layernorm(x, g, b):
    xf = x.astype(jnp.float32)
    mu = jnp.mean(xf, axis=-1, keepdims=True)
    var = jnp.mean(jnp.square(xf - mu), axis=-1, keepdims=True)
    y = (xf - mu) * lax.rsqrt(var + NORM_EPS)
    return (y * g.astype(jnp.float32) + b.astype(jnp.float32)).astype(x.dtype)


def _l2norm(x):
    xf = x.astype(jnp.float32)
    return xf * lax.rsqrt(jnp.sum(xf * xf, axis=-1, keepdims=True) + NORM_EPS)


def _causal_dwconv(x, w):
    k, c = w.shape
    xp = jnp.pad(x, ((0, 0), (k - 1, 0), (0, 0)))
    return lax.conv_general_dilated(
        xp, w.astype(x.dtype)[:, None, :], window_strides=(1,), padding="VALID",
        dimension_numbers=("NWC", "WIO", "NWC"), feature_group_count=c)


def _gated_delta_rule(q, k, v, beta, g):
    bsz, t, h, dk = q.shape
    dv = v.shape[-1]
    c = DN_CHUNK
    n = t // c

    def chunks(a):
        a = a.reshape((bsz, n, c, h) + a.shape[3:])
        return jnp.moveaxis(a, 3, 1)

    q, k, v, beta, g = chunks(q), chunks(k), chunks(v), chunks(beta), chunks(g)
    gc = jnp.cumsum(g, axis=-1)
    diff = gc[..., :, None] - gc[..., None, :]
    incl = jnp.tril(jnp.ones((c, c), dtype=bool))
    strict = jnp.tril(jnp.ones((c, c), dtype=bool), k=-1)
    gamma_incl = jnp.exp(jnp.where(incl, diff, -jnp.inf))
    gamma_strict = jnp.exp(jnp.where(strict, diff, -jnp.inf))

    kk = jnp.einsum("bhnid,bhnjd->bhnij", k, k)
    a_mat = jnp.eye(c, dtype=jnp.float32) + beta[..., :, None] * kk * gamma_strict
    rhs = jnp.concatenate([v * beta[..., None],
                           k * (beta * jnp.exp(gc))[..., None]], axis=-1)
    sol = lax.linalg.triangular_solve(a_mat, rhs, left_side=True, lower=True,
                                      unit_diagonal=True)
    u, w = sol[..., :dv], sol[..., dv:]

    qk = jnp.einsum("bhnid,bhnjd->bhnij", q, k) * gamma_incl
    q_dec = q * jnp.exp(gc)[..., None]
    k_dec = k * jnp.exp(gc[..., -1:] - gc)[..., None]
    d_last = jnp.exp(gc[..., -1])

    def step(s, xs):
        u_c, w_c, qd_c, qk_c, kd_c, dl_c = xs
        v_new = u_c - jnp.einsum("bhcd,bhde->bhce", w_c, s)
        o_c = (jnp.einsum("bhcd,bhde->bhce", qd_c, s)
               + jnp.einsum("bhij,bhje->bhie", qk_c, v_new))
        s = dl_c[..., None, None] * s + jnp.einsum("bhcd,bhce->bhde", kd_c, v_new)
        return s, o_c

    xs = tuple(jnp.moveaxis(a, 2, 0) for a in (u, w, q_dec, qk, k_dec, d_last))
    s0 = jnp.zeros((bsz, h, dk, dv), jnp.float32)
    _, o = lax.scan(step, s0, xs)
    o = jnp.transpose(o, (1, 0, 3, 2, 4))
    return o.reshape(bsz, t, h, dv)


def _layer(x, norm_g, w_in, a_dw, a_dw_b, a_ln_g, a_ln_b, a_proj,
           b_conv, b_a_log, b_dt_bias, b_onorm_g, b_proj,
           c_ln_g, c_ln_b, c_ws, c_bs, c_proj, w_out):
    bsz, t, _ = x.shape
    h = _rmsnorm(x, norm_g)
    proj = h @ w_in
    idx = [int(i) for i in np.cumsum(SPLIT_SIZES)[:-1]]
    (a_val, a_glu, a_z, qkv, b_z, b_beta, b_alpha,
     c_u, c_v, c_z, gate_logits) = jnp.split(proj, idx, axis=-1)

    a = a_val * jax.nn.sigmoid(a_glu)
    a = _causal_dwconv(a, a_dw) + a_dw_b
    a = _layernorm(a, a_ln_g, a_ln_b)
    y_a = jax.nn.silu(a) * jax.nn.silu(a_z)

    qkv = jax.nn.silu(_causal_dwconv(qkv, b_conv))
    q, k, v = jnp.split(qkv, [DN_HEADS * DN_DK, 2 * DN_HEADS * DN_DK], axis=-1)
    q = _l2norm(q.reshape(bsz, t, DN_HEADS, DN_DK)) * (DN_DK ** -0.5)
    k = _l2norm(k.reshape(bsz, t, DN_HEADS, DN_DK))
    v = v.reshape(bsz, t, DN_HEADS, DN_DV).astype(jnp.float32)
    beta = jax.nn.sigmoid(b_beta.astype(jnp.float32))
    g = -jnp.exp(b_a_log.astype(jnp.float32)) * jax.nn.softplus(
        b_alpha.astype(jnp.float32) + b_dt_bias.astype(jnp.float32))
    o = _gated_delta_rule(q, k, v, beta, g)
    o = _rmsnorm(o, b_onorm_g)
    z = jax.nn.silu(b_z.reshape(bsz, t, DN_HEADS, DN_DV).astype(jnp.float32))
    y_b = (o * z).reshape(bsz, t, DN_HEADS * DN_DV).astype(x.dtype)

    u = jax.nn.gelu(c_u)
    vs = _layernorm(jax.nn.gelu(c_v), c_ln_g, c_ln_b)
    vs = vs.reshape(bsz, t // SG_CHUNK, SG_CHUNK, SG_GROUPS, SG_WIDTH // SG_GROUPS)
    ws = jnp.tril(c_ws)
    mixed = jnp.einsum("gij,bnjgc->bnigc", ws, vs) + c_bs.T[:, :, None]
    y_c = u * mixed.reshape(bsz, t, SG_WIDTH) * jax.nn.silu(c_z)

    gates = jax.nn.sigmoid(gate_logits).reshape(bsz, t, N_BRANCH, D_MODEL)
    merged = (gates[..., 0, :] * (y_a @ a_proj)
              + gates[..., 1, :] * (y_b @ b_proj)
              + gates[..., 2, :] * (y_c @ c_proj))
    return x + merged @ w_out


def _fwd_setup_inputs(seed: int = 0) -> dict:
    key = jax.random.key(seed)
    ks = jax.random.split(key, 24)
    f32 = jnp.float32
    L = DEPTH

    def nrm(k, shape, fan_in):
        return jax.random.normal(k, shape, f32) * (fan_in ** -0.5)

    def gain(k, shape):
        return 1.0 + 0.02 * jax.random.normal(k, shape, f32)

    def bias(k, shape):
        return 0.02 * jax.random.normal(k, shape, f32)

    dt = jnp.exp(jax.random.uniform(ks[10], (L, DN_HEADS), f32,
                                    math.log(1e-3), math.log(1e-1)))
    return {
        "x": jax.random.normal(ks[0], (BATCH, SEQ, D_MODEL), f32),
        "norm_g": gain(ks[1], (L, D_MODEL)),
        "w_in": nrm(ks[2], (L, D_MODEL, N_IN_COLS), D_MODEL),
        "a_dw": nrm(ks[3], (L, CONV_K, CONV_WIDTH), CONV_K),
        "a_dw_b": bias(ks[4], (L, CONV_WIDTH)),
        "a_ln_g": gain(ks[5], (L, CONV_WIDTH)),
        "a_ln_b": bias(ks[6], (L, CONV_WIDTH)),
        "a_proj": nrm(ks[7], (L, CONV_WIDTH, D_MODEL), CONV_WIDTH),
        "b_conv": nrm(ks[8], (L, DN_SHORT_K, DN_HEADS * (2 * DN_DK + DN_DV)), DN_SHORT_K),
        "b_a_log": jnp.log(jax.random.uniform(ks[9], (L, DN_HEADS), f32, 1.0, 16.0)),
        "b_dt_bias": dt + jnp.log(-jnp.expm1(-dt)),
        "b_onorm_g": gain(ks[11], (L, DN_DV)),
        "b_proj": nrm(ks[12], (L, DN_HEADS * DN_DV, D_MODEL), DN_HEADS * DN_DV),
        "c_ln_g": gain(ks[13], (L, SG_WIDTH)),
        "c_ln_b": bias(ks[14], (L, SG_WIDTH)),
        "c_ws": nrm(ks[15], (L, SG_GROUPS, SG_CHUNK, SG_CHUNK), SG_CHUNK),
        "c_bs": gain(ks[16], (L, SG_GROUPS, SG_CHUNK)),
        "c_proj": nrm(ks[17], (L, SG_WIDTH, D_MODEL), SG_WIDTH),
        "w_out": nrm(ks[18], (L, D_MODEL, D_MODEL), D_MODEL),
        "final_g": gain(ks[19], (D_MODEL,)),
    }


def _fwd_reference(x, norm_g, w_in, a_dw, a_dw_b, a_ln_g, a_ln_b, a_proj,
              b_conv, b_a_log, b_dt_bias, b_onorm_g, b_proj,
              c_ln_g, c_ln_b, c_ws, c_bs, c_proj, w_out, final_g):
    for l in range(DEPTH):
        x = _layer(x, norm_g[l], w_in[l], a_dw[l], a_dw_b[l], a_ln_g[l], a_ln_b[l],
                   a_proj[l], b_conv[l], b_a_log[l], b_dt_bias[l], b_onorm_g[l],
                   b_proj[l], c_ln_g[l], c_ln_b[l], c_ws[l], c_bs[l], c_proj[l],
                   w_out[l])
    return _rmsnorm(x, final_g)


import jax as _jax
import jax.numpy as _jnp

TWIN_FORMAT = 'train_step'
FWD_PARAMS = ['x', 'norm_g', 'w_in', 'a_dw', 'a_dw_b', 'a_ln_g', 'a_ln_b', 'a_proj', 'b_conv', 'b_a_log', 'b_dt_bias', 'b_onorm_g', 'b_proj', 'c_ln_g', 'c_ln_b', 'c_ws', 'c_bs', 'c_proj', 'w_out', 'final_g']
TWIN_WEIGHTS = ['norm_g', 'w_in', 'a_dw', 'a_dw_b', 'a_ln_g', 'a_ln_b', 'a_proj', 'b_conv', 'b_a_log', 'b_dt_bias', 'b_onorm_g', 'b_proj', 'c_ln_g', 'c_ln_b', 'c_ws', 'c_bs', 'c_proj', 'w_out', 'final_g']
TWIN_DIFF_INPUT = 'x'
TWIN_INPUTS = ['x', 'norm_g', 'w_in', 'a_dw', 'a_dw_b', 'a_ln_g', 'a_ln_b', 'a_proj', 'b_conv', 'b_a_log', 'b_dt_bias', 'b_onorm_g', 'b_proj', 'c_ln_g', 'c_ln_b', 'c_ws', 'c_bs', 'c_proj', 'w_out', 'final_g', 'loss_target', 'm_norm_g', 'm_w_in', 'm_a_dw', 'm_a_dw_b', 'm_a_ln_g', 'm_a_ln_b', 'm_a_proj', 'm_b_conv', 'm_b_a_log', 'm_b_dt_bias', 'm_b_onorm_g', 'm_b_proj', 'm_c_ln_g', 'm_c_ln_b', 'm_c_ws', 'm_c_bs', 'm_c_proj', 'm_w_out', 'm_final_g', 'v_norm_g', 'v_w_in', 'v_a_dw', 'v_a_dw_b', 'v_a_ln_g', 'v_a_ln_b', 'v_a_proj', 'v_b_conv', 'v_b_a_log', 'v_b_dt_bias', 'v_b_onorm_g', 'v_b_proj', 'v_c_ln_g', 'v_c_ln_b', 'v_c_ws', 'v_c_bs', 'v_c_proj', 'v_w_out', 'v_final_g']
TWIN_OUTPUTS = ['loss', 'grad_x', 'grad_norm_g', 'grad_w_in', 'grad_a_dw', 'grad_a_dw_b', 'grad_a_ln_g', 'grad_a_ln_b', 'grad_a_proj', 'grad_b_conv', 'grad_b_a_log', 'grad_b_dt_bias', 'grad_b_onorm_g', 'grad_b_proj', 'grad_c_ln_g', 'grad_c_ln_b', 'grad_c_ws', 'grad_c_bs', 'grad_c_proj', 'grad_w_out', 'grad_final_g', 'delta_norm_g', 'delta_w_in', 'delta_a_dw', 'delta_a_dw_b', 'delta_a_ln_g', 'delta_a_ln_b', 'delta_a_proj', 'delta_b_conv', 'delta_b_a_log', 'delta_b_dt_bias', 'delta_b_onorm_g', 'delta_b_proj', 'delta_c_ln_g', 'delta_c_ln_b', 'delta_c_ws', 'delta_c_bs', 'delta_c_proj', 'delta_w_out', 'delta_final_g', 'new_m_norm_g', 'new_m_w_in', 'new_m_a_dw', 'new_m_a_dw_b', 'new_m_a_ln_g', 'new_m_a_ln_b', 'new_m_a_proj', 'new_m_b_conv', 'new_m_b_a_log', 'new_m_b_dt_bias', 'new_m_b_onorm_g', 'new_m_b_proj', 'new_m_c_ln_g', 'new_m_c_ln_b', 'new_m_c_ws', 'new_m_c_bs', 'new_m_c_proj', 'new_m_w_out', 'new_m_final_g', 'new_v_norm_g', 'new_v_w_in', 'new_v_a_dw', 'new_v_a_dw_b', 'new_v_a_ln_g', 'new_v_a_ln_b', 'new_v_a_proj', 'new_v_b_conv', 'new_v_b_a_log', 'new_v_b_dt_bias', 'new_v_b_onorm_g', 'new_v_b_proj', 'new_v_c_ln_g', 'new_v_c_ln_b', 'new_v_c_ws', 'new_v_c_bs', 'new_v_c_proj', 'new_v_w_out', 'new_v_final_g']
TWIN_LEAF_KINDS = {'loss': 'loss', 'grad_x': 'grad_x', 'grad_norm_g': 'grad_w', 'grad_w_in': 'grad_w', 'grad_a_dw': 'grad_w', 'grad_a_dw_b': 'grad_w', 'grad_a_ln_g': 'grad_w', 'grad_a_ln_b': 'grad_w', 'grad_a_proj': 'grad_w', 'grad_b_conv': 'grad_w', 'grad_b_a_log': 'grad_w', 'grad_b_dt_bias': 'grad_w', 'grad_b_onorm_g': 'grad_w', 'grad_b_proj': 'grad_w', 'grad_c_ln_g': 'grad_w', 'grad_c_ln_b': 'grad_w', 'grad_c_ws': 'grad_w', 'grad_c_bs': 'grad_w', 'grad_c_proj': 'grad_w', 'grad_w_out': 'grad_w', 'grad_final_g': 'grad_w', 'delta_norm_g': 'delta_w', 'delta_w_in': 'delta_w', 'delta_a_dw': 'delta_w', 'delta_a_dw_b': 'delta_w', 'delta_a_ln_g': 'delta_w', 'delta_a_ln_b': 'delta_w', 'delta_a_proj': 'delta_w', 'delta_b_conv': 'delta_w', 'delta_b_a_log': 'delta_w', 'delta_b_dt_bias': 'delta_w', 'delta_b_onorm_g': 'delta_w', 'delta_b_proj': 'delta_w', 'delta_c_ln_g': 'delta_w', 'delta_c_ln_b': 'delta_w', 'delta_c_ws': 'delta_w', 'delta_c_bs': 'delta_w', 'delta_c_proj': 'delta_w', 'delta_w_out': 'delta_w', 'delta_final_g': 'delta_w', 'new_m_norm_g': 'new_m', 'new_m_w_in': 'new_m', 'new_m_a_dw': 'new_m', 'new_m_a_dw_b': 'new_m', 'new_m_a_ln_g': 'new_m', 'new_m_a_ln_b': 'new_m', 'new_m_a_proj': 'new_m', 'new_m_b_conv': 'new_m', 'new_m_b_a_log': 'new_m', 'new_m_b_dt_bias': 'new_m', 'new_m_b_onorm_g': 'new_m', 'new_m_b_proj': 'new_m', 'new_m_c_ln_g': 'new_m', 'new_m_c_ln_b': 'new_m', 'new_m_c_ws': 'new_m', 'new_m_c_bs': 'new_m', 'new_m_c_proj': 'new_m', 'new_m_w_out': 'new_m', 'new_m_final_g': 'new_m', 'new_v_norm_g': 'new_v', 'new_v_w_in': 'new_v', 'new_v_a_dw': 'new_v', 'new_v_a_dw_b': 'new_v', 'new_v_a_ln_g': 'new_v', 'new_v_a_ln_b': 'new_v', 'new_v_a_proj': 'new_v', 'new_v_b_conv': 'new_v', 'new_v_b_a_log': 'new_v', 'new_v_b_dt_bias': 'new_v', 'new_v_b_onorm_g': 'new_v', 'new_v_b_proj': 'new_v', 'new_v_c_ln_g': 'new_v', 'new_v_c_ln_b': 'new_v', 'new_v_c_ws': 'new_v', 'new_v_c_bs': 'new_v', 'new_v_c_proj': 'new_v', 'new_v_w_out': 'new_v', 'new_v_final_g': 'new_v'}


def _forward(args):
    return _fwd_reference(*[args[k] for k in FWD_PARAMS])


def _output_shape():
    out = _jax.eval_shape(lambda: _forward(_fwd_setup_inputs(0)))
    return out.shape, out.dtype

N_MICROBATCH = 1
ADAM_LR = 0.001
ADAM_B1 = 0.9
ADAM_B2 = 0.999
ADAM_EPS = 1e-08
ADAM_WD = 0.01
ADAM_STEP = 10
PER_EXAMPLE_BATCH_AXIS = {'x': 0, 'loss_target': 0}
SHARED_INPUTS = []
_WEIGHT_DTYPES = {'norm_g': _jnp.float32, 'w_in': _jnp.float32, 'a_dw': _jnp.float32, 'a_dw_b': _jnp.float32, 'a_ln_g': _jnp.float32, 'a_ln_b': _jnp.float32, 'a_proj': _jnp.float32, 'b_conv': _jnp.float32, 'b_a_log': _jnp.float32, 'b_dt_bias': _jnp.float32, 'b_onorm_g': _jnp.float32, 'b_proj': _jnp.float32, 'c_ln_g': _jnp.float32, 'c_ln_b': _jnp.float32, 'c_ws': _jnp.float32, 'c_bs': _jnp.float32, 'c_proj': _jnp.float32, 'w_out': _jnp.float32, 'final_g': _jnp.float32}
MOMENT_SCALE = {'norm_g': 1.819327e-01, 'w_in': 5.661789e-02, 'a_dw': 6.693408e-02, 'a_dw_b': 1.415624e-01, 'a_ln_g': 7.757913e-02, 'a_ln_b': 6.587659e-02, 'a_proj': 4.586922e-02, 'b_conv': 5.867728e-02, 'b_a_log': 4.985112e-01, 'b_dt_bias': 5.055141e-01, 'b_onorm_g': 2.251133e-01, 'b_proj': 7.533396e-02, 'c_ln_g': 4.820125e-02, 'c_ln_b': 4.825568e-02, 'c_ws': 5.031645e-02, 'c_bs': 7.129350e-02, 'c_proj': 6.025514e-02, 'w_out': 1.065465e-01, 'final_g': 6.393578e+01}


def _to_microbatches(a, axis):
    t = _jnp.moveaxis(a, axis, 0)
    t = t.reshape((N_MICROBATCH, t.shape[0] // N_MICROBATCH) + t.shape[1:])
    return _jnp.moveaxis(t, 1, axis + 1)


def setup_inputs(seed: int = 0) -> dict:
    inp = _fwd_setup_inputs(seed)
    key = _jax.random.fold_in(_jax.random.key(seed), 7919)
    shape, _ = _output_shape()
    out = dict(inp)
    out["loss_target"] = _jax.random.normal(_jax.random.fold_in(key, 0), shape, _jnp.float32)
    for i, name in enumerate(TWIN_WEIGHTS):
        w = inp[name].astype(_jnp.float32)
        if MOMENT_SCALE is None:
            s = _jnp.sqrt(_jnp.mean(_jnp.square(w)) + 1e-30)
        else:
            s = MOMENT_SCALE[name]
        km, kv = _jax.random.split(_jax.random.fold_in(key, i + 1))
        out[name] = w
        out["m_" + name] = s * _jax.random.normal(km, w.shape, _jnp.float32)
        out["v_" + name] = (s * s) * _jax.random.uniform(kv, w.shape, _jnp.float32, 0.5, 1.5)
    if N_MICROBATCH > 1:
        for name, axis in PER_EXAMPLE_BATCH_AXIS.items():
            out[name] = _to_microbatches(out[name], axis)
    return {'x': out['x'], 'norm_g': out['norm_g'], 'w_in': out['w_in'], 'a_dw': out['a_dw'], 'a_dw_b': out['a_dw_b'], 'a_ln_g': out['a_ln_g'], 'a_ln_b': out['a_ln_b'], 'a_proj': out['a_proj'], 'b_conv': out['b_conv'], 'b_a_log': out['b_a_log'], 'b_dt_bias': out['b_dt_bias'], 'b_onorm_g': out['b_onorm_g'], 'b_proj': out['b_proj'], 'c_ln_g': out['c_ln_g'], 'c_ln_b': out['c_ln_b'], 'c_ws': out['c_ws'], 'c_bs': out['c_bs'], 'c_proj': out['c_proj'], 'w_out': out['w_out'], 'final_g': out['final_g'], 'loss_target': out['loss_target'], 'm_norm_g': out['m_norm_g'], 'm_w_in': out['m_w_in'], 'm_a_dw': out['m_a_dw'], 'm_a_dw_b': out['m_a_dw_b'], 'm_a_ln_g': out['m_a_ln_g'], 'm_a_ln_b': out['m_a_ln_b'], 'm_a_proj': out['m_a_proj'], 'm_b_conv': out['m_b_conv'], 'm_b_a_log': out['m_b_a_log'], 'm_b_dt_bias': out['m_b_dt_bias'], 'm_b_onorm_g': out['m_b_onorm_g'], 'm_b_proj': out['m_b_proj'], 'm_c_ln_g': out['m_c_ln_g'], 'm_c_ln_b': out['m_c_ln_b'], 'm_c_ws': out['m_c_ws'], 'm_c_bs': out['m_c_bs'], 'm_c_proj': out['m_c_proj'], 'm_w_out': out['m_w_out'], 'm_final_g': out['m_final_g'], 'v_norm_g': out['v_norm_g'], 'v_w_in': out['v_w_in'], 'v_a_dw': out['v_a_dw'], 'v_a_dw_b': out['v_a_dw_b'], 'v_a_ln_g': out['v_a_ln_g'], 'v_a_ln_b': out['v_a_ln_b'], 'v_a_proj': out['v_a_proj'], 'v_b_conv': out['v_b_conv'], 'v_b_a_log': out['v_b_a_log'], 'v_b_dt_bias': out['v_b_dt_bias'], 'v_b_onorm_g': out['v_b_onorm_g'], 'v_b_proj': out['v_b_proj'], 'v_c_ln_g': out['v_c_ln_g'], 'v_c_ln_b': out['v_c_ln_b'], 'v_c_ws': out['v_c_ws'], 'v_c_bs': out['v_c_bs'], 'v_c_proj': out['v_c_proj'], 'v_w_out': out['v_w_out'], 'v_final_g': out['v_final_g']}


def _loss(weights, diff, rest, loss_target):
    with _jax.named_scope("forward"):
        args = {**rest, TWIN_DIFF_INPUT: diff, **{k: w.astype(_WEIGHT_DTYPES[k]) for k, w in weights.items()}}
        y = _forward(args)
    with _jax.named_scope("loss_head"):
        err = _jnp.square(y.astype(_jnp.float32) - loss_target)
        return 0.5 * _jnp.sum(_jnp.mean(err, axis=-1)) if err.ndim else 0.5 * err


def _adamw(w, g, m, v):
    m = ADAM_B1 * m + (1.0 - ADAM_B1) * g
    v = ADAM_B2 * v + (1.0 - ADAM_B2) * _jnp.square(g)
    m_hat = m / (1.0 - ADAM_B1 ** ADAM_STEP)
    v_hat = v / (1.0 - ADAM_B2 ** ADAM_STEP)
    delta = -ADAM_LR * (m_hat / (_jnp.sqrt(v_hat) + ADAM_EPS) + ADAM_WD * w)
    return delta, m, v


def reference(x, norm_g, w_in, a_dw, a_dw_b, a_ln_g, a_ln_b, a_proj, b_conv, b_a_log, b_dt_bias, b_onorm_g, b_proj, c_ln_g, c_ln_b, c_ws, c_bs, c_proj, w_out, final_g, loss_target, m_norm_g, m_w_in, m_a_dw, m_a_dw_b, m_a_ln_g, m_a_ln_b, m_a_proj, m_b_conv, m_b_a_log, m_b_dt_bias, m_b_onorm_g, m_b_proj, m_c_ln_g, m_c_ln_b, m_c_ws, m_c_bs, m_c_proj, m_w_out, m_final_g, v_norm_g, v_w_in, v_a_dw, v_a_dw_b, v_a_ln_g, v_a_ln_b, v_a_proj, v_b_conv, v_b_a_log, v_b_dt_bias, v_b_onorm_g, v_b_proj, v_c_ln_g, v_c_ln_b, v_c_ws, v_c_bs, v_c_proj, v_w_out, v_final_g):
    given = dict(x=x, norm_g=norm_g, w_in=w_in, a_dw=a_dw, a_dw_b=a_dw_b, a_ln_g=a_ln_g, a_ln_b=a_ln_b, a_proj=a_proj, b_conv=b_conv, b_a_log=b_a_log, b_dt_bias=b_dt_bias, b_onorm_g=b_onorm_g, b_proj=b_proj, c_ln_g=c_ln_g, c_ln_b=c_ln_b, c_ws=c_ws, c_bs=c_bs, c_proj=c_proj, w_out=w_out, final_g=final_g, loss_target=loss_target, m_norm_g=m_norm_g, m_w_in=m_w_in, m_a_dw=m_a_dw, m_a_dw_b=m_a_dw_b, m_a_ln_g=m_a_ln_g, m_a_ln_b=m_a_ln_b, m_a_proj=m_a_proj, m_b_conv=m_b_conv, m_b_a_log=m_b_a_log, m_b_dt_bias=m_b_dt_bias, m_b_onorm_g=m_b_onorm_g, m_b_proj=m_b_proj, m_c_ln_g=m_c_ln_g, m_c_ln_b=m_c_ln_b, m_c_ws=m_c_ws, m_c_bs=m_c_bs, m_c_proj=m_c_proj, m_w_out=m_w_out, m_final_g=m_final_g, v_norm_g=v_norm_g, v_w_in=v_w_in, v_a_dw=v_a_dw, v_a_dw_b=v_a_dw_b, v_a_ln_g=v_a_ln_g, v_a_ln_b=v_a_ln_b, v_a_proj=v_a_proj, v_b_conv=v_b_conv, v_b_a_log=v_b_a_log, v_b_dt_bias=v_b_dt_bias, v_b_onorm_g=v_b_onorm_g, v_b_proj=v_b_proj, v_c_ln_g=v_c_ln_g, v_c_ln_b=v_c_ln_b, v_c_ws=v_c_ws, v_c_bs=v_c_bs, v_c_proj=v_c_proj, v_w_out=v_w_out, v_final_g=v_final_g)
    weights = {n: given[n] for n in TWIN_WEIGHTS}
    shared = {n: given[n] for n in SHARED_INPUTS}
    per_example = {n: given[n] for n in ['x']}
    grad_fn = _jax.value_and_grad(_loss, argnums=(0, 1))

    def one_microbatch(ex, loss_target):
        ex = dict(ex)
        diff = ex.pop(TWIN_DIFF_INPUT)
        return grad_fn(weights, diff, {**shared, **ex}, loss_target)

    if N_MICROBATCH == 1:
        loss, (grad_w, grad_x) = one_microbatch(per_example, given["loss_target"])
    else:
        def body(carry, xs):
            loss_sum, grad_sum = carry
            l_k, (gw_k, gx_k) = one_microbatch(xs[0], xs[1])
            with _jax.named_scope("update"):
                return (loss_sum + l_k, _jax.tree.map(_jnp.add, grad_sum, gw_k)), gx_k

        init = (_jnp.zeros((), _jnp.float32), _jax.tree.map(_jnp.zeros_like, weights))
        (loss, grad_w), grad_x = _jax.lax.scan(body, init, (per_example, given["loss_target"]))
    with _jax.named_scope("update"):
        delta_w, new_m, new_v = {}, {}, {}
        for n in TWIN_WEIGHTS:
            delta_w[n], new_m[n], new_v[n] = _adamw(weights[n], grad_w[n], given["m_" + n], given["v_" + n])
    return (loss, grad_x, *[grad_w[n] for n in TWIN_WEIGHTS], *[delta_w[n] for n in TWIN_WEIGHTS],
            *[new_m[n] for n in TWIN_WEIGHTS], *[new_v[n] for n in TWIN_WEIGHTS])
```

```python
import functools

import jax
import jax.numpy as jnp
from jax import lax
from jax.experimental import pallas as pl
from jax.experimental.pallas import tpu as pltpu

F32 = jnp.float32
BF16 = jnp.bfloat16
HI = lax.Precision.HIGHEST
SDS = jax.ShapeDtypeStruct
MESH = pl.DeviceIdType.MESH

NDEV = 8
D = 1024
EPS = 1e-6
LANE = 128
PACK_W = 1024

C_Q, C_K, C_V = 0, 1024, 2048
C_G = 3072
C_A = 6144
C_C = 7680
C_BZ = 9216
C_BA = 10240
NP = 10368

A_K, A_H = 31, 32
B_K, B_H = 4, 8
CH = 64
SG = 128
NH = 8

ADAM_LR, ADAM_B1, ADAM_B2, ADAM_EPS, ADAM_WD, ADAM_STEP = 0.001, 0.9, 0.999, 1e-08, 0.01, 10

IN_NAMES = ['x', 'norm_g', 'w_in', 'a_dw', 'a_dw_b', 'a_ln_g', 'a_ln_b', 'a_proj', 'b_conv', 'b_a_log', 'b_dt_bias',
            'b_onorm_g', 'b_proj', 'c_ln_g', 'c_ln_b', 'c_ws', 'c_bs', 'c_proj', 'w_out', 'final_g']
WEIGHTS = IN_NAMES[1:]
SHARDED = ['w_in', 'a_dw', 'a_proj', 'b_conv', 'b_proj', 'c_proj', 'w_out']
REPL = [n for n in WEIGHTS if n not in SHARDED]
BIG = ['w_in', 'a_proj', 'b_proj', 'c_proj', 'w_out']
SMALL = ['a_dw', 'b_conv']
ROW_SHARDED = ('b_proj', 'w_out')


def _tile(n, pref):
    return pref if (n >= pref and n % pref == 0) else n


def _cp(vmem_mb):
    return pltpu.CompilerParams(vmem_limit_bytes=vmem_mb * 2 ** 20)


def _full(shape):
    nd = len(shape)
    return pl.BlockSpec(shape, lambda *_: (0,) * nd)


def _rms(x, g):
    return x * lax.rsqrt(jnp.mean(x * x, axis=-1, keepdims=True) + EPS) * g


def _softplus(x):
    return jnp.maximum(x, 0.0) + jnp.log1p(jnp.exp(-jnp.abs(x)))


def _bdot(a, b):
    return jnp.dot(a.astype(BF16), b.astype(BF16), preferred_element_type=F32)


def _all_gather(xs, name):
    r, c = xs.shape

    def body(x_ref, out_ref, send_sems, recv_sems, local_sem):
        x, y, cc = lax.axis_index("x"), lax.axis_index("y"), lax.axis_index("c")
        me, sibling = (x, y, cc), (x, y, 1 - cc)
        chips = [(1 - x, y), (x, 1 - y), (1 - x, 1 - y)]

        def slot(px, py, pc):
            return out_ref.at[4 * px + 2 * py + pc]

        def copy(k, block, to, src=None):
            return pltpu.make_async_remote_copy(
                src_ref=slot(*block) if src is None else src, dst_ref=slot(*block),
                send_sem=send_sems.at[k], recv_sem=recv_sems.at[k], device_id=to, device_id_type=MESH)

        mine = pltpu.make_async_copy(x_ref, slot(*me), local_sem)
        mine.start()
        first = [copy(0, me, sibling, src=x_ref)]
        first += [copy(1 + j, me, (*chip, cc), src=x_ref) for j, chip in enumerate(chips)]
        for cp in first:
            cp.start()
        passed = [copy(4 + j, (*chip, cc), sibling) for j, chip in enumerate(chips)]
        for j, chip in enumerate(chips):
            copy(1 + j, (*chip, cc), me).wait_recv()
            passed[j].start()
        copy(0, sibling, me).wait_recv()
        for j, chip in enumerate(chips):
            copy(4 + j, (*chip, 1 - cc), me).wait_recv()
        for cp in first + passed:
            cp.wait_send()
        mine.wait()

    return pl.pallas_call(
        body, name=name, out_shape=SDS((NDEV, r, c), xs.dtype),
        in_specs=[pl.BlockSpec(memory_space=pltpu.HBM)], out_specs=pl.BlockSpec(memory_space=pltpu.HBM),
        scratch_shapes=[pltpu.SemaphoreType.DMA((7,)), pltpu.SemaphoreType.DMA((7,)), pltpu.SemaphoreType.DMA],
    )(xs)


def _all_to_all(send, name):
    _, r, c = send.shape

    def body(in_ref, out_ref, send_sems, recv_sems, local_sem):
        x, y, cc = lax.axis_index("x"), lax.axis_index("y"), lax.axis_index("c")
        me = 4 * x + 2 * y + cc
        mine = pltpu.make_async_copy(in_ref.at[me], out_ref.at[me], local_sem)
        mine.start()
        sends, recvs = [], []
        for k in range(1, NDEV):
            px = 1 - x if k & 4 else x
            py = 1 - y if k & 2 else y
            pc = 1 - cc if k & 1 else cc
            peer = 4 * px + 2 * py + pc
            sends.append(pltpu.make_async_remote_copy(
                src_ref=in_ref.at[peer], dst_ref=out_ref.at[me], send_sem=send_sems.at[k - 1],
                recv_sem=recv_sems.at[k - 1], device_id=(px, py, pc), device_id_type=MESH))
            recvs.append(pltpu.make_async_remote_copy(
                src_ref=in_ref.at[me], dst_ref=out_ref.at[peer], send_sem=send_sems.at[k - 1],
                recv_sem=recv_sems.at[k - 1], device_id=(px, py, pc), device_id_type=MESH))
        for cp in sends:
            cp.start()
        for cp in recvs:
            cp.wait_recv()
        for cp in sends:
            cp.wait_send()
        mine.wait()

    return pl.pallas_call(
        body, name=name, out_shape=SDS((NDEV, r, c), send.dtype),
        in_specs=[pl.BlockSpec(memory_space=pltpu.HBM)], out_specs=pl.BlockSpec(memory_space=pltpu.HBM),
        scratch_shapes=[pltpu.SemaphoreType.DMA((7,)), pltpu.SemaphoreType.DMA((7,)), pltpu.SemaphoreType.DMA],
    )(send)


def _inproj(x2d, g_row, wp):
    n = x2d.shape[0]
    tm, tn = _tile(n, 1024), 1152

    def body(x_ref, g_ref, w_ref, proj_ref, h_ref, hs):
        @pl.when(pl.program_id(1) == 0)
        def _():
            h = _rms(x_ref[...], g_ref[...]).astype(BF16)
            hs[...] = h
            h_ref[...] = h

        proj_ref[...] = jnp.dot(hs[...], w_ref[...], preferred_element_type=F32)

    return pl.pallas_call(
        body, name="inproj", grid=(n // tm, NP // tn),
        in_specs=[pl.BlockSpec((tm, D), lambda i, j: (i, 0)), _full((1, D)), pl.BlockSpec((D, tn), lambda i, j: (0, j))],
        out_specs=[pl.BlockSpec((tm, tn), lambda i, j: (i, j)), pl.BlockSpec((tm, D), lambda i, j: (i, 0))],
        out_shape=[SDS((n, NP), F32), SDS((n, D), BF16)],
        scratch_shapes=[pltpu.VMEM((tm, D), BF16)], compiler_params=_cp(48),
    )(x2d, g_row, wp)


def _mm_nn(at, b, name):
    m, nn = at.shape
    k = b.shape[1]
    tk = 1152 if k % 1152 == 0 else _tile(k, 1024)
    tn = _tile(nn, 1024)

    def body(a_ref, b_ref, o_ref):
        p = jnp.dot(a_ref[...], b_ref[...].astype(BF16), preferred_element_type=F32)

        @pl.when(pl.program_id(1) == 0)
        def _():
            o_ref[...] = p

        @pl.when(pl.program_id(1) > 0)
        def _():
            o_ref[...] += p

    return pl.pallas_call(
        body, name=name, grid=(k // tk, nn // tn),
        in_specs=[pl.BlockSpec((m, tn), lambda j, t: (0, t)), pl.BlockSpec((tn, tk), lambda j, t: (t, j))],
        out_specs=pl.BlockSpec((m, tk), lambda j, t: (0, j)),
        out_shape=SDS((m, k), F32), compiler_params=_cp(48),
    )(at, b)


def _inproj_bwd(dproj, wpt, x2d, g_row, dxo):
    n = x2d.shape[0]
    tm, tk = _tile(n, 512), 1152
    nk = NP // tk

    def body(dp_ref, w_ref, x_ref, g_ref, dxo_ref, dx_ref, dg_ref, acc):
        i, k = pl.program_id(0), pl.program_id(1)
        p = jnp.dot(dp_ref[...].astype(BF16), w_ref[...], preferred_element_type=F32)

        @pl.when(k == 0)
        def _():
            acc[...] = p

        @pl.when(k > 0)
        def _():
            acc[...] += p

        @pl.when(k == nk - 1)
        def _():
            _, vjp = jax.vjp(_rms, x_ref[...], g_ref[...])
            dx, dg = vjp(acc[...])
            dx_ref[...] = dxo_ref[...] + dx

            @pl.when(i == 0)
            def _():
                dg_ref[...] = dg

            @pl.when(i > 0)
            def _():
                dg_ref[...] += dg

    return pl.pallas_call(
        body, name="inproj_bwd", grid=(n // tm, nk),
        in_specs=[pl.BlockSpec((tm, tk), lambda i, k: (i, k)), pl.BlockSpec((tk, D), lambda i, k: (k, 0)),
                  pl.BlockSpec((tm, D), lambda i, k: (i, 0)), _full((1, D)), pl.BlockSpec((tm, D), lambda i, k: (i, 0))],
        out_specs=[pl.BlockSpec((tm, D), lambda i, k: (i, 0)), _full((1, D))],
        out_shape=[SDS((n, D), F32), SDS((1, D), F32)],
        scratch_shapes=[pltpu.VMEM((tm, D), F32)], compiler_params=_cp(48),
    )(dproj, wpt, x2d, g_row, dxo)


def _loss_head(x2d, g_row, tgt):
    n = x2d.shape[0]
    tm = _tile(n, 512)

    def body(x_ref, g_ref, t_ref, dx_ref, dg_ref, loss_ref):
        i = pl.program_id(0)
        y, vjp = jax.vjp(_rms, x_ref[...], g_ref[...])
        err = y - t_ref[...]
        part = 0.5 * jnp.sum(jnp.mean(err * err, axis=-1, keepdims=True), axis=0, keepdims=True)
        dx, dg = vjp(err * (1.0 / D))
        dx_ref[...] = dx
        lb = jnp.broadcast_to(part, (8, LANE))

        @pl.when(i == 0)
        def _():
            dg_ref[...] = dg
            loss_ref[...] = lb

        @pl.when(i > 0)
        def _():
            dg_ref[...] += dg
            loss_ref[...] += lb

    return pl.pallas_call(
        body, name="loss_head", grid=(n // tm,),
        in_specs=[pl.BlockSpec((tm, D), lambda i: (i, 0)), _full((1, D)), pl.BlockSpec((tm, D), lambda i: (i, 0))],
        out_specs=[pl.BlockSpec((tm, D), lambda i: (i, 0)), _full((1, D)), _full((8, LANE))],
        out_shape=[SDS((n, D), F32), SDS((1, D), F32), SDS((8, LANE), F32)], compiler_params=_cp(40),
    )(x2d, g_row, tgt)


def _conv_fwd(ext_ref, w_ref, kw, halo, tt):
    acc = None
    for j in range(kw):
        term = w_ref[j:j + 1, :] * ext_ref[pl.ds(halo - (kw - 1) + j, tt), :]
        acc = term if acc is None else acc + term
    return acc


def _conv_bwd_x(dcp_ref, w_ref, kw, halo, tt):
    acc = None
    for j in range(kw):
        term = w_ref[j:j + 1, :] * dcp_ref[pl.ds(kw - 1 - j, tt + halo), :]
        acc = term if acc is None else acc + term
    return acc


def _conv_bwd_w(dc, ext_ref, gw_ref, kw, halo, tt):
    for j in range(kw):
        gw_ref[j:j + 1, :] += jnp.sum(dc * ext_ref[pl.ds(halo - (kw - 1) + j, tt), :], axis=0, keepdims=True)


def _a_post(c, z, g, b):
    mu = jnp.mean(c, axis=-1, keepdims=True)
    var = jnp.mean(jnp.square(c - mu), axis=-1, keepdims=True)
    a = (c - mu) * lax.rsqrt(var + EPS) * g + b
    return jax.nn.silu(a) * jax.nn.silu(z)


def _a_fwd(proj, dw, b_row, lg, lb, bsz, t):
    n = bsz * t
    tt = _tile(t, 256)
    nt = t // tt
    cb = C_A // 512

    def body(v_ref, g_ref, z_ref, vh_ref, gh_ref, w_ref, b_ref, lg_ref, lb_ref, y_ref, ext):
        i = pl.program_id(1)
        ext[0:A_H, :] = jnp.where(i > 0, vh_ref[...] * jax.nn.sigmoid(gh_ref[...]), 0.0)
        ext[A_H:, :] = v_ref[...] * jax.nn.sigmoid(g_ref[...])
        c = _conv_fwd(ext, w_ref, A_K, A_H, tt) + b_ref[...]
        y_ref[...] = _a_post(c, z_ref[...], lg_ref[...], lb_ref[...])

    def row(b, i):
        return b * nt + i

    def halo(b, i):
        return jnp.maximum((b * t + i * tt) // A_H - 1, 0)

    return pl.pallas_call(
        body, name="a_fwd", grid=(bsz, nt),
        in_specs=[pl.BlockSpec((tt, 512), lambda b, i: (row(b, i), cb)),
                  pl.BlockSpec((tt, 512), lambda b, i: (row(b, i), cb + 1)),
                  pl.BlockSpec((tt, 512), lambda b, i: (row(b, i), cb + 2)),
                  pl.BlockSpec((A_H, 512), lambda b, i: (halo(b, i), cb)),
                  pl.BlockSpec((A_H, 512), lambda b, i: (halo(b, i), cb + 1)),
                  _full((32, 512)), _full((1, 512)), _full((1, 512)), _full((1, 512))],
        out_specs=pl.BlockSpec((tt, 512), lambda b, i: (row(b, i), 0)),
        out_shape=SDS((n, 512), F32),
        scratch_shapes=[pltpu.VMEM((tt + A_H, 512), F32)], compiler_params=_cp(40),
    )(proj, proj, proj, proj, proj, dw, b_row, lg, lb)


def _a_bwd(dproj, proj, dy, dw, b_row, lg, lb, bsz, t):
    n = bsz * t
    tt = _tile(t, 256)
    nt = t // tt
    cb = C_A // 512

    def body(dp_any, v_ref, g_ref, z_ref, vh_ref, gh_ref, dy_ref, w_ref, b_ref, lg_ref, lb_ref,
             dp_ref, gw_ref, gb_ref, glg_ref, glb_ref, ext, dcp, dae, carry):
        b, i = pl.program_id(0), pl.program_id(1)
        ti = nt - 1 - i

        @pl.when((b == 0) & (i == 0))
        def _():
            gw_ref[...] = jnp.zeros_like(gw_ref)
            gb_ref[...] = jnp.zeros_like(gb_ref)
            glg_ref[...] = jnp.zeros_like(glg_ref)
            glb_ref[...] = jnp.zeros_like(glb_ref)

        @pl.when(i == 0)
        def _():
            carry[...] = jnp.zeros_like(carry)

        val, glu = v_ref[...], g_ref[...]
        sg = jax.nn.sigmoid(glu)
        ext[0:A_H, :] = jnp.where(ti > 0, vh_ref[...] * jax.nn.sigmoid(gh_ref[...]), 0.0)
        ext[A_H:, :] = val * sg
        c = _conv_fwd(ext, w_ref, A_K, A_H, tt) + b_ref[...]
        _, vjp = jax.vjp(_a_post, c, z_ref[...], lg_ref[...], lb_ref[...])
        dc, dz, dlg, dlb = vjp(dy_ref[...])
        gb_ref[...] += jnp.sum(dc, axis=0, keepdims=True)
        glg_ref[...] += dlg
        glb_ref[...] += dlb
        dcp[0:A_H, :] = jnp.zeros((A_H, 512), F32)
        dcp[A_H:A_H + tt, :] = dc
        dcp[A_H + tt:, :] = jnp.zeros((A_H, 512), F32)
        _conv_bwd_w(dc, ext, gw_ref, A_K, A_H, tt)
        dae[...] = _conv_bwd_x(dcp, w_ref, A_K, A_H, tt)
        dae[tt:tt + A_H, :] += carry[...]
        carry[...] = dae[0:A_H, :]
        da = dae[A_H:, :]
        dp_ref[:, 0:512] = da * sg
        dp_ref[:, 512:1024] = da * val * sg * (1.0 - sg)
        dp_ref[:, 1024:1536] = dz

    def row(b, i):
        return b * nt + (nt - 1 - i)

    def halo(b, i):
        return jnp.maximum((b * t + (nt - 1 - i) * tt) // A_H - 1, 0)

    outs = pl.pallas_call(
        body, name="a_bwd", grid=(bsz, nt),
        in_specs=[pl.BlockSpec(memory_space=pl.ANY),
                  pl.BlockSpec((tt, 512), lambda b, i: (row(b, i), cb)),
                  pl.BlockSpec((tt, 512), lambda b, i: (row(b, i), cb + 1)),
                  pl.BlockSpec((tt, 512), lambda b, i: (row(b, i), cb + 2)),
                  pl.BlockSpec((A_H, 512), lambda b, i: (halo(b, i), cb)),
                  pl.BlockSpec((A_H, 512), lambda b, i: (halo(b, i), cb + 1)),
                  pl.BlockSpec((tt, 512), lambda b, i: (row(b, i), 0)),
                  _full((32, 512)), _full((1, 512)), _full((1, 512)), _full((1, 512))],
        out_specs=[pl.BlockSpec((tt, 1536), lambda b, i: (row(b, i), C_A // 1536)),
                   _full((32, 512)), _full((1, 512)), _full((1, 512)), _full((1, 512))],
        out_shape=[SDS((n, NP), F32), SDS((32, 512), F32), SDS((1, 512), F32), SDS((1, 512), F32), SDS((1, 512), F32)],
        input_output_aliases={0: 0},
        scratch_shapes=[pltpu.VMEM((tt + A_H, 512), F32), pltpu.VMEM((tt + 2 * A_H, 512), F32),
                        pltpu.VMEM((tt + A_H, 512), F32), pltpu.VMEM((A_H, 512), F32)],
        compiler_params=_cp(48),
    )(dproj, proj, proj, proj, proj, proj, dy, dw, b_row, lg, lb)
    return outs


def _b_post(blocks):
    out = []
    for idx, c in enumerate(blocks):
        s = jax.nn.silu(c)
        if idx < 2 * NH:
            s = s * lax.rsqrt(jnp.sum(s * s, axis=-1, keepdims=True) + EPS)
            if idx < NH:
                s = s * (LANE ** -0.5)
        out.append(s)
    return out


def _bprep_fwd(proj, wconv, bsz, t):
    n = bsz * t
    tt = _tile(t, 256)
    nt = t // tt

    def body(x_ref, xh_ref, w_ref, o_ref, ext):
        i = pl.program_id(1)
        ext[0:B_H, :] = jnp.where(i > 0, xh_ref[...], 0.0)
        ext[B_H:, :] = x_ref[...]
        c = _conv_fwd(ext, w_ref, B_K, B_H, tt)
        outs = _b_post([c[:, LANE * j:LANE * (j + 1)] for j in range(3 * NH)])
        for j, o in enumerate(outs):
            o_ref[:, LANE * j:LANE * (j + 1)] = o

    return pl.pallas_call(
        body, name="bprep_fwd", grid=(bsz, nt),
        in_specs=[pl.BlockSpec((tt, 3072), lambda b, i: (b * nt + i, 0)),
                  pl.BlockSpec((B_H, 3072), lambda b, i: (jnp.maximum((b * t + i * tt) // B_H - 1, 0), 0)),
                  _full((8, 3072))],
        out_specs=pl.BlockSpec((tt, 3072), lambda b, i: (b * nt + i, 0)),
        out_shape=SDS((n, 3072), F32),
        scratch_shapes=[pltpu.VMEM((tt + B_H, 3072), F32)], compiler_params=_cp(48),
    )(proj, proj, wconv)


def _bprep_bwd(dproj, proj, dqkvn, wconv, bsz, t):
    n = bsz * t
    tt = _tile(t, 256)
    nt = t // tt

    def body(dp_any, x_ref, xh_ref, dq_ref, w_ref, dp_ref, gw_ref, ext, dcp, dae, carry):
        b, i = pl.program_id(0), pl.program_id(1)
        ti = nt - 1 - i

        @pl.when((b == 0) & (i == 0))
        def _():
            gw_ref[...] = jnp.zeros_like(gw_ref)

        @pl.when(i == 0)
        def _():
            carry[...] = jnp.zeros_like(carry)

        ext[0:B_H, :] = jnp.where(ti > 0, xh_ref[...], 0.0)
        ext[B_H:, :] = x_ref[...]
        c = _conv_fwd(ext, w_ref, B_K, B_H, tt)
        _, vjp = jax.vjp(_b_post, [c[:, LANE * j:LANE * (j + 1)] for j in range(3 * NH)])
        (dcs,) = vjp([dq_ref[:, LANE * j:LANE * (j + 1)] for j in range(3 * NH)])
        dcp[0:B_H, :] = jnp.zeros((B_H, 3072), F32)
        for j, dcj in enumerate(dcs):
            dcp[B_H:B_H + tt, LANE * j:LANE * (j + 1)] = dcj
        dcp[B_H + tt:, :] = jnp.zeros((B_H, 3072), F32)
        _conv_bwd_w(dcp[B_H:B_H + tt, :], ext, gw_ref, B_K, B_H, tt)
        dae[...] = _conv_bwd_x(dcp, w_ref, B_K, B_H, tt)
        dae[tt:tt + B_H, :] += carry[...]
        carry[...] = dae[0:B_H, :]
        dp_ref[...] = dae[B_H:, :]

    def row(b, i):
        return b * nt + (nt - 1 - i)

    return pl.pallas_call(
        body, name="bprep_bwd", grid=(bsz, nt),
        in_specs=[pl.BlockSpec(memory_space=pl.ANY),
                  pl.BlockSpec((tt, 3072), lambda b, i: (row(b, i), 0)),
                  pl.BlockSpec((B_H, 3072), lambda b, i: (jnp.maximum((b * t + (nt - 1 - i) * tt) // B_H - 1, 0), 0)),
                  pl.BlockSpec((tt, 3072), lambda b, i: (row(b, i), 0)),
                  _full((8, 3072))],
        out_specs=[pl.BlockSpec((tt, 3072), lambda b, i: (row(b, i), 0)), _full((8, 3072))],
        out_shape=[SDS((n, NP), F32), SDS((8, 3072), F32)],
        input_output_aliases={0: 0},
        scratch_shapes=[pltpu.VMEM((tt + B_H, 3072), F32), pltpu.VMEM((tt + 2 * B_H, 3072), F32),
                        pltpu.VMEM((tt + B_H, 3072), F32), pltpu.VMEM((B_H, 3072), F32)],
        compiler_params=_cp(56),
    )(dproj, proj, proj, dqkvn, wconv)


def _hdot(a, b):
    return jnp.dot(a, b, precision=HI, preferred_element_type=F32)


def _dot_nt(a, b, precision=None):
    return lax.dot_general(a, b, (((1,), (1,)), ((), ())), precision=precision, preferred_element_type=F32)


def _dot_tn(a, b):
    return lax.dot_general(a, b, (((0,), (0,)), ((), ())), preferred_element_type=F32)


def _delta_chunk(h, s, q, k, v, ba, z, alog, dtb, og):
    lane = lax.broadcasted_iota(jnp.int32, (1, LANE), 1)
    pick_b = lane == h
    pick_a = lane == h + NH
    bl = jnp.sum(jnp.where(pick_b, ba, 0.0), axis=-1, keepdims=True)
    al = jnp.sum(jnp.where(pick_a, ba, 0.0), axis=-1, keepdims=True)
    alog_h = jnp.sum(jnp.where(pick_a, alog, 0.0), axis=-1, keepdims=True)
    dt_h = jnp.sum(jnp.where(pick_a, dtb, 0.0), axis=-1, keepdims=True)
    beta = jax.nn.sigmoid(bl)
    g = -jnp.exp(alog_h) * _softplus(al + dt_h)

    r = lax.broadcasted_iota(jnp.int32, (CH, CH), 0)
    c = lax.broadcasted_iota(jnp.int32, (CH, CH), 1)
    incl, strict, eye = r >= c, r > c, (r == c).astype(F32)
    gca = _hdot(incl.astype(F32), jnp.broadcast_to(g, (CH, CH)))
    gcr = _dot_nt((c == 0).astype(F32), gca, precision=HI)
    gc = jnp.sum(jnp.where(c == 0, gca, 0.0), axis=-1, keepdims=True)
    ri = lax.broadcasted_iota(jnp.int32, (CH, 1), 0)
    gl = jnp.sum(jnp.where(ri == CH - 1, gc, 0.0), axis=0, keepdims=True)
    diff = gca - gcr
    gam_s = jnp.where(strict, jnp.exp(jnp.where(strict, diff, 0.0)), 0.0)
    gam_i = jnp.where(incl, jnp.exp(jnp.where(incl, diff, 0.0)), 0.0)

    lmat = beta * _dot_nt(k, k) * gam_s
    blk = jnp.right_shift(r, 4) == jnp.right_shift(c, 4)
    dm = jnp.where(blk, lmat, 0.0)
    om = lmat - dm
    d2 = _hdot(dm, dm)
    d4 = _hdot(d2, d2)
    d8 = _hdot(d4, d4)
    p = _hdot(_hdot(_hdot(eye - dm, eye + d2), eye + d4), eye + d8)
    m = _hdot(p, om)
    tinv = _hdot(_hdot(eye - m, eye + _hdot(m, m)), p)

    eg = jnp.exp(gc)
    u = jnp.dot(tinv, v * beta, preferred_element_type=F32)
    w = jnp.dot(tinv, k * (beta * eg), preferred_element_type=F32)
    qk = _dot_nt(q, k) * gam_i
    vn = u - jnp.dot(w, s, preferred_element_type=F32)
    o = jnp.dot(q * eg, s, preferred_element_type=F32) + jnp.dot(qk, vn, preferred_element_type=F32)
    sn = jnp.exp(gl) * s + _dot_tn(k * jnp.exp(gl - gc), vn)
    y = _rms(o, og) * jax.nn.silu(z)
    return sn, y


def _delta_fwd(qkvn, proj, alog, dtb, og, bsz, t):
    n = bsz * t
    nc = t // CH

    def body(q_ref, k_ref, v_ref, ba_ref, z_ref, al_ref, dt_ref, og_ref, y_ref, sh_ref, s_scr):
        ci, h = pl.program_id(1), pl.program_id(2)

        @pl.when(ci == 0)
        def _():
            s_scr[h] = jnp.zeros((LANE, LANE), F32)

        s = s_scr[h]
        sh_ref[...] = s
        sn, y = _delta_chunk(h, s, q_ref[...], k_ref[...], v_ref[...], ba_ref[...], z_ref[...],
                             al_ref[...], dt_ref[...], og_ref[...])
        s_scr[h] = sn
        y_ref[...] = y

    def blk(col):
        return pl.BlockSpec((CH, LANE), lambda b, ci, h: (b * nc + ci, col + h))

    return pl.pallas_call(
        body, name="delta_fwd", grid=(bsz, nc, NH),
        in_specs=[blk(0), blk(NH), blk(2 * NH),
                  pl.BlockSpec((CH, LANE), lambda b, ci, h: (b * nc + ci, C_BA // LANE)),
                  blk(C_BZ // LANE), _full((1, LANE)), _full((1, LANE)), _full((1, LANE))],
        out_specs=[blk(0), pl.BlockSpec((None, None, None, LANE, LANE), lambda b, ci, h: (b, ci, h, 0, 0))],
        out_shape=[SDS((n, D), F32), SDS((bsz, nc, NH, LANE, LANE), F32)],
        scratch_shapes=[pltpu.VMEM((NH, LANE, LANE), F32)], compiler_params=_cp(32),
    )(qkvn, qkvn, qkvn, proj, proj, alog, dtb, og)


def _delta_bwd(dproj, qkvn, proj, shist, dyb, alog, dtb, og, bsz, t):
    n = bsz * t
    nc = t // CH

    def body(dp_any, q_ref, k_ref, v_ref, ba_ref, z_ref, sh_ref, dy_ref, al_ref, dt_ref, og_ref,
             dp_ref, dqkv_ref, gal_ref, gdt_ref, gog_ref, ds_scr):
        b, ci, h = pl.program_id(0), pl.program_id(1), pl.program_id(2)

        @pl.when((b == 0) & (ci == 0) & (h == 0))
        def _():
            gal_ref[...] = jnp.zeros_like(gal_ref)
            gdt_ref[...] = jnp.zeros_like(gdt_ref)
            gog_ref[...] = jnp.zeros_like(gog_ref)

        @pl.when(ci == 0)
        def _():
            ds_scr[h] = jnp.zeros((LANE, LANE), F32)

        _, vjp = jax.vjp(functools.partial(_delta_chunk, h), sh_ref[...], q_ref[...], k_ref[...], v_ref[...],
                         ba_ref[...], z_ref[...], al_ref[...], dt_ref[...], og_ref[...])
        ds, dq, dk, dv, dba, dz, dal, ddt, dog = vjp((ds_scr[h], dy_ref[...]))
        ds_scr[h] = ds
        gal_ref[...] += dal
        gdt_ref[...] += ddt
        gog_ref[...] += dog
        for hh in range(NH):
            @pl.when(h == hh)
            def _():
                dp_ref[:, LANE * hh:LANE * (hh + 1)] = dz
                dqkv_ref[:, LANE * hh:LANE * (hh + 1)] = dq
                dqkv_ref[:, D + LANE * hh:D + LANE * (hh + 1)] = dk
                dqkv_ref[:, 2 * D + LANE * hh:2 * D + LANE * (hh + 1)] = dv

        @pl.when(h == 0)
        def _():
            dp_ref[:, D:D + LANE] = dba

        @pl.when(h > 0)
        def _():
            dp_ref[:, D:D + LANE] += dba

    def blk(col):
        return pl.BlockSpec((CH, LANE), lambda b, ci, h: (b * nc + (nc - 1 - ci), col + h))

    return pl.pallas_call(
        body, name="delta_bwd", grid=(bsz, nc, NH),
        in_specs=[pl.BlockSpec(memory_space=pl.ANY), blk(0), blk(NH), blk(2 * NH),
                  pl.BlockSpec((CH, LANE), lambda b, ci, h: (b * nc + (nc - 1 - ci), C_BA // LANE)),
                  blk(C_BZ // LANE),
                  pl.BlockSpec((None, None, None, LANE, LANE), lambda b, ci, h: (b, nc - 1 - ci, h, 0, 0)),
                  blk(0), _full((1, LANE)), _full((1, LANE)), _full((1, LANE))],
        out_specs=[pl.BlockSpec((CH, D + LANE), lambda b, ci, h: (b * nc + (nc - 1 - ci), C_BZ // (D + LANE))),
                   pl.BlockSpec((CH, 3 * D), lambda b, ci, h: (b * nc + (nc - 1 - ci), 0)),
                   _full((1, LANE)), _full((1, LANE)), _full((1, LANE))],
        out_shape=[SDS((n, NP), F32), SDS((n, 3 * D), F32), SDS((1, LANE), F32), SDS((1, LANE), F32), SDS((1, LANE), F32)],
        input_output_aliases={0: 0},
        scratch_shapes=[pltpu.VMEM((NH, LANE, LANE), F32)], compiler_params=_cp(40),
    )(dproj, qkvn, qkvn, qkvn, proj, proj, shist, dyb, alog, dtb, og)


def _c_chunk(us, vs, zs, lgs, lbs, ws, bsb):
    gv = [jax.nn.gelu(v) for v in vs]
    width = LANE * len(gv)
    mu = sum(jnp.sum(x, axis=-1, keepdims=True) for x in gv) / width
    var = sum(jnp.sum(jnp.square(x - mu), axis=-1, keepdims=True) for x in gv) / width
    rstd = lax.rsqrt(var + EPS)
    r = lax.broadcasted_iota(jnp.int32, (SG, SG), 0)
    c = lax.broadcasted_iota(jnp.int32, (SG, SG), 1)
    out = []
    for j in range(len(gv)):
        nrm = (gv[j] - mu) * rstd * lgs[j] + lbs[j]
        mixed = jnp.dot(jnp.where(r >= c, ws[j], 0.0), nrm, preferred_element_type=F32) + bsb[j]
        out.append(jax.nn.gelu(us[j]) * mixed * jax.nn.silu(zs[j]))
    return out


def _c_args(u_ref, v_ref, z_ref, lg_ref, lb_ref, ws_ref, bs_ref):
    sl = [slice(LANE * j, LANE * (j + 1)) for j in range(4)]
    return ([u_ref[:, s] for s in sl], [v_ref[:, s] for s in sl], [z_ref[:, s] for s in sl],
            [lg_ref[:, s] for s in sl], [lb_ref[:, s] for s in sl],
            [ws_ref[j] for j in range(4)], [bs_ref[j] for j in range(4)])


def _c_fwd(proj, lg, lb, ws, bsb, n):
    cb = C_C // 512

    def body(u_ref, v_ref, z_ref, lg_ref, lb_ref, ws_ref, bs_ref, y_ref):
        outs = _c_chunk(*_c_args(u_ref, v_ref, z_ref, lg_ref, lb_ref, ws_ref, bs_ref))
        for j, o in enumerate(outs):
            y_ref[:, LANE * j:LANE * (j + 1)] = o

    return pl.pallas_call(
        body, name="c_fwd", grid=(n // SG,),
        in_specs=[pl.BlockSpec((SG, 512), lambda i: (i, cb)), pl.BlockSpec((SG, 512), lambda i: (i, cb + 1)),
                  pl.BlockSpec((SG, 512), lambda i: (i, cb + 2)), _full((1, 512)), _full((1, 512)),
                  _full((4, SG, SG)), _full((4, SG, SG))],
        out_specs=pl.BlockSpec((SG, 512), lambda i: (i, 0)),
        out_shape=SDS((n, 512), F32), compiler_params=_cp(32),
    )(proj, proj, proj, lg, lb, ws, bsb)


def _c_bwd(dproj, proj, dy, lg, lb, ws, bsb, n):
    cb = C_C // 512

    def body(dp_any, u_ref, v_ref, z_ref, dy_ref, lg_ref, lb_ref, ws_ref, bs_ref,
             dp_ref, glg_ref, glb_ref, gws_ref, gbs_ref):
        @pl.when(pl.program_id(0) == 0)
        def _():
            glg_ref[...] = jnp.zeros_like(glg_ref)
            glb_ref[...] = jnp.zeros_like(glb_ref)
            gws_ref[...] = jnp.zeros_like(gws_ref)
            gbs_ref[...] = jnp.zeros_like(gbs_ref)

        _, vjp = jax.vjp(_c_chunk, *_c_args(u_ref, v_ref, z_ref, lg_ref, lb_ref, ws_ref, bs_ref))
        dus, dvs, dzs, dlgs, dlbs, dwss, dbss = vjp([dy_ref[:, LANE * j:LANE * (j + 1)] for j in range(4)])
        for j in range(4):
            sl = slice(LANE * j, LANE * (j + 1))
            dp_ref[:, LANE * j:LANE * (j + 1)] = dus[j]
            dp_ref[:, 512 + LANE * j:512 + LANE * (j + 1)] = dvs[j]
            dp_ref[:, 1024 + LANE * j:1024 + LANE * (j + 1)] = dzs[j]
            glg_ref[:, sl] += dlgs[j]
            glb_ref[:, sl] += dlbs[j]
            gws_ref[j] += dwss[j]
            gbs_ref[j] += jnp.broadcast_to(jnp.sum(dbss[j], axis=-1, keepdims=True), (SG, SG))

    return pl.pallas_call(
        body, name="c_bwd", grid=(n // SG,),
        in_specs=[pl.BlockSpec(memory_space=pl.ANY),
                  pl.BlockSpec((SG, 512), lambda i: (i, cb)), pl.BlockSpec((SG, 512), lambda i: (i, cb + 1)),
                  pl.BlockSpec((SG, 512), lambda i: (i, cb + 2)), pl.BlockSpec((SG, 512), lambda i: (i, 0)),
                  _full((1, 512)), _full((1, 512)), _full((4, SG, SG)), _full((4, SG, SG))],
        out_specs=[pl.BlockSpec((SG, 1536), lambda i: (i, C_C // 1536)),
                   _full((1, 512)), _full((1, 512)), _full((4, SG, SG)), _full((4, SG, SG))],
        out_shape=[SDS((n, NP), F32), SDS((1, 512), F32), SDS((1, 512), F32), SDS((4, SG, SG), F32), SDS((4, SG, SG), F32)],
        input_output_aliases={0: 0}, compiler_params=_cp(32),
    )(dproj, proj, proj, proj, dy, lg, lb, ws, bsb)


def _merge_fwd(x2d, ya, yb, yc, proj, ap, bp, cp, wo):
    n = x2d.shape[0]
    tm = _tile(n, 256)
    gb = C_G // D

    def body(x_ref, ya_ref, yb_ref, yc_ref, g0_ref, g1_ref, g2_ref, ap_ref, bp_ref, cp_ref, wo_ref, o_ref):
        merged = (jax.nn.sigmoid(g0_ref[...]) * _bdot(ya_ref[...], ap_ref[...])
                  + jax.nn.sigmoid(g1_ref[...]) * _bdot(yb_ref[...], bp_ref[...])
                  + jax.nn.sigmoid(g2_ref[...]) * _bdot(yc_ref[...], cp_ref[...]))
        o_ref[...] = x_ref[...] + _bdot(merged, wo_ref[...])

    def rows(w):
        return pl.BlockSpec((tm, w), lambda i: (i, 0))

    return pl.pallas_call(
        body, name="merge_fwd", grid=(n // tm,),
        in_specs=[rows(D), rows(512), rows(D), rows(512),
                  pl.BlockSpec((tm, D), lambda i: (i, gb)), pl.BlockSpec((tm, D), lambda i: (i, gb + 1)),
                  pl.BlockSpec((tm, D), lambda i: (i, gb + 2)),
                  _full((512, D)), _full((D, D)), _full((512, D)), _full((D, D))],
        out_specs=rows(D), out_shape=SDS((n, D), F32), compiler_params=_cp(48),
    )(x2d, ya, yb, yc, proj, proj, proj, ap, bp, cp, wo)


def _merge_bwd(dxo, ya, yb, yc, proj, ap, bp, cp, apt, bpt, cpt, wot):
    n = dxo.shape[0]
    tm = _tile(n, 128)
    gb = C_G // D

    def body(d_ref, ya_ref, yb_ref, yc_ref, g0_ref, g1_ref, g2_ref, ap_ref, bp_ref, cp_ref,
             apt_ref, bpt_ref, cpt_ref, wot_ref,
             dp_ref, dya_ref, dyb_ref, dyc_ref, dpa_ref, dpb_ref, dpc_ref, mg_ref):
        dm = _bdot(d_ref[...], wot_ref[...])
        merged = None
        for j, (g_ref, y_ref, w_ref, wt_ref, dy_ref, dpj_ref) in enumerate((
                (g0_ref, ya_ref, ap_ref, apt_ref, dya_ref, dpa_ref),
                (g1_ref, yb_ref, bp_ref, bpt_ref, dyb_ref, dpb_ref),
                (g2_ref, yc_ref, cp_ref, cpt_ref, dyc_ref, dpc_ref))):
            s = jax.nn.sigmoid(g_ref[...])
            pj = _bdot(y_ref[...], w_ref[...])
            merged = s * pj if merged is None else merged + s * pj
            dp_ref[:, D * j:D * (j + 1)] = dm * pj * s * (1.0 - s)
            dpj = (dm * s).astype(BF16)
            dpj_ref[...] = dpj
            dy_ref[...] = jnp.dot(dpj, wt_ref[...], preferred_element_type=F32)
        mg_ref[...] = merged

    def rows(w):
        return pl.BlockSpec((tm, w), lambda i: (i, 0))

    return pl.pallas_call(
        body, name="merge_bwd", grid=(n // tm,),
        in_specs=[rows(D), rows(512), rows(D), rows(512),
                  pl.BlockSpec((tm, D), lambda i: (i, gb)), pl.BlockSpec((tm, D), lambda i: (i, gb + 1)),
                  pl.BlockSpec((tm, D), lambda i: (i, gb + 2)),
                  _full((512, D)), _full((D, D)), _full((512, D)),
                  _full((D, 512)), _full((D, D)), _full((D, 512)), _full((D, D))],
        out_specs=[pl.BlockSpec((tm, 3 * D), lambda i: (i, C_G // (3 * D))), rows(512), rows(D), rows(512),
                   rows(D), rows(D), rows(D), rows(D)],
        out_shape=[SDS((n, NP), F32), SDS((n, 512), F32), SDS((n, D), F32), SDS((n, 512), F32),
                   SDS((n, D), BF16), SDS((n, D), BF16), SDS((n, D), BF16), SDS((n, D), F32)],
        compiler_params=_cp(56),
    )(dxo, ya, yb, yc, proj, proj, proj, ap, bp, cp, apt, bpt, cpt, wot)


def _reduce_adamw(parts, w, m, v, name):
    r = w.shape[0]
    tr = _tile(r, 128)
    c1 = 1.0 - ADAM_B1 ** ADAM_STEP
    c2 = 1.0 - ADAM_B2 ** ADAM_STEP

    def body(p_ref, w_ref, m_ref, v_ref, g_ref, d_ref, nm_ref, nv_ref):
        g = p_ref[0]
        for s in range(1, NDEV):
            g = g + p_ref[s]
        nm = ADAM_B1 * m_ref[...] + (1.0 - ADAM_B1) * g
        nv = ADAM_B2 * v_ref[...] + (1.0 - ADAM_B2) * jnp.square(g)
        g_ref[...] = g
        nm_ref[...] = nm
        nv_ref[...] = nv
        d_ref[...] = -ADAM_LR * ((nm / c1) / (jnp.sqrt(nv / c2) + ADAM_EPS) + ADAM_WD * w_ref[...])

    blk = pl.BlockSpec((tr, PACK_W), lambda i: (i, 0))
    return pl.pallas_call(
        body, name=name, grid=(r // tr,),
        in_specs=[pl.BlockSpec((NDEV, tr, PACK_W), lambda i: (0, i, 0)), blk, blk, blk],
        out_specs=[blk, blk, blk, blk], out_shape=[SDS((r, PACK_W), F32)] * 4, compiler_params=_cp(40),
    )(parts, w, m, v)


def _rows_for(nelem, mult):
    rows = -(-nelem // PACK_W)
    return -(-rows // mult) * mult


def _pack(arrs, mult, dtype):
    flat = jnp.concatenate([a.reshape(-1).astype(dtype) for a in arrs])
    rows = _rows_for(flat.shape[0], mult)
    flat = jnp.pad(flat, (0, rows * PACK_W - flat.shape[0]))
    return flat.reshape(rows, PACK_W)


def _unpack(buf, shapes, lead=()):
    flat = buf.reshape(lead + (-1,))
    out, off = [], 0
    for shp in shapes:
        size = 1
        for s in shp:
            size *= s
        out.append(flat[..., off:off + size].reshape(lead + tuple(shp)))
        off += size
    return out


def _unshard(name, g):
    if name in ROW_SHARDED:
        g = jnp.moveaxis(g, 0, 1)
        return g.reshape(g.shape[0], g.shape[1] * g.shape[2], g.shape[3])
    g = jnp.moveaxis(g, 0, 2)
    return g.reshape(g.shape[0], g.shape[1], g.shape[2] * g.shape[3])


def _reshard(name, full):
    l, r, c = full.shape
    if name in ROW_SHARDED:
        g = jnp.moveaxis(full.reshape(l, NDEV, r // NDEV, c), 1, 0)
    else:
        g = jnp.moveaxis(full.reshape(l, r, NDEV, c // NDEV), 2, 0)
    return g.reshape(NDEV, -1)


def _permute_cols(w):
    pad = jnp.zeros(w.shape[:-1] + (NP - C_BA - 16,), w.dtype)
    return jnp.concatenate([w[..., 1536:4608], w[..., 7184:10256], w[..., 0:1536], w[..., 5648:7184],
                            w[..., 4608:5632], w[..., 5632:5648], pad], axis=-1)


def _unpermute_cols(g):
    return jnp.concatenate([g[..., C_A:C_A + 1536], g[..., 0:3072], g[..., C_BZ:C_BZ + 1024], g[..., C_BA:C_BA + 16],
                            g[..., C_C:C_C + 1536], g[..., C_G:C_G + 3072]], axis=-1)


def _lane_row(vec8, offset):
    return jnp.zeros((1, LANE), F32).at[0, offset:offset + NH].set(vec8)


def kernel(x, norm_g, w_in, a_dw, a_dw_b, a_ln_g, a_ln_b, a_proj, b_conv, b_a_log, b_dt_bias, b_onorm_g, b_proj, c_ln_g, c_ln_b, c_ws, c_bs, c_proj, w_out, final_g, loss_target, m_norm_g, m_w_in, m_a_dw, m_a_dw_b, m_a_ln_g, m_a_ln_b, m_a_proj, m_b_conv, m_b_a_log, m_b_dt_bias, m_b_onorm_g, m_b_proj, m_c_ln_g, m_c_ln_b, m_c_ws, m_c_bs, m_c_proj, m_w_out, m_final_g, v_norm_g, v_w_in, v_a_dw, v_a_dw_b, v_a_ln_g, v_a_ln_b, v_a_proj, v_b_conv, v_b_a_log, v_b_dt_bias, v_b_onorm_g, v_b_proj, v_c_ln_g, v_c_ln_b, v_c_ws, v_c_bs, v_c_proj, v_w_out, v_final_g):
    wts = dict(norm_g=norm_g, w_in=w_in, a_dw=a_dw, a_dw_b=a_dw_b, a_ln_g=a_ln_g, a_ln_b=a_ln_b, a_proj=a_proj,
               b_conv=b_conv, b_a_log=b_a_log, b_dt_bias=b_dt_bias, b_onorm_g=b_onorm_g, b_proj=b_proj,
               c_ln_g=c_ln_g, c_ln_b=c_ln_b, c_ws=c_ws, c_bs=c_bs, c_proj=c_proj, w_out=w_out, final_g=final_g)
    mom = dict(norm_g=m_norm_g, w_in=m_w_in, a_dw=m_a_dw, a_dw_b=m_a_dw_b, a_ln_g=m_a_ln_g, a_ln_b=m_a_ln_b,
               a_proj=m_a_proj, b_conv=m_b_conv, b_a_log=m_b_a_log, b_dt_bias=m_b_dt_bias, b_onorm_g=m_b_onorm_g,
               b_proj=m_b_proj, c_ln_g=m_c_ln_g, c_ln_b=m_c_ln_b, c_ws=m_c_ws, c_bs=m_c_bs, c_proj=m_c_proj,
               w_out=m_w_out, final_g=m_final_g)
    vel = dict(norm_g=v_norm_g, w_in=v_w_in, a_dw=v_a_dw, a_dw_b=v_a_dw_b, a_ln_g=v_a_ln_g, a_ln_b=v_a_ln_b,
               a_proj=v_a_proj, b_conv=v_b_conv, b_a_log=v_b_a_log, b_dt_bias=v_b_dt_bias, b_onorm_g=v_b_onorm_g,
               b_proj=v_b_proj, c_ln_g=v_c_ln_g, c_ln_b=v_c_ln_b, c_ws=v_c_ws, c_bs=v_c_bs, c_proj=v_c_proj,
               w_out=v_w_out, final_g=v_final_g)

    bsz, t, _ = x.shape
    n = bsz * t
    depth = norm_g.shape[0]
    x2d = x.reshape(n, D)
    tgt = loss_target.reshape(n, D)

    big_all = _all_gather(_pack([wts[k] for k in BIG], 16, BF16), "gather_matmul_weights")
    small_all = _all_gather(_pack([wts[k] for k in SMALL], 8, F32), "gather_conv_weights")
    full = {}
    for k, g in zip(BIG, _unpack(big_all, [wts[k].shape for k in BIG], lead=(NDEV,))):
        full[k] = _unshard(k, g)
    for k, g in zip(SMALL, _unpack(small_all, [wts[k].shape for k in SMALL], lead=(NDEV,))):
        full[k] = _unshard(k, g)
    wp = _permute_cols(full['w_in'])
    wpt = jnp.swapaxes(wp, 1, 2)
    a_dw32 = jnp.pad(full['a_dw'], ((0, 0), (0, 32 - A_K), (0, 0)))
    b_conv8 = jnp.pad(full['b_conv'], ((0, 0), (0, 8 - B_K), (0, 0)))
    bsb = jnp.broadcast_to(c_bs[..., None], c_bs.shape + (SG,))

    saved = []
    xl = x2d
    for l in range(depth):
        alog, dtb = _lane_row(b_a_log[l], NH), _lane_row(b_dt_bias[l], NH)
        proj, h = _inproj(xl, norm_g[l][None], wp[l])
        ya = _a_fwd(proj, a_dw32[l], a_dw_b[l][None], a_ln_g[l][None], a_ln_b[l][None], bsz, t)
        qkvn = _bprep_fwd(proj, b_conv8[l], bsz, t)
        yb, shist = _delta_fwd(qkvn, proj, alog, dtb, b_onorm_g[l][None], bsz, t)
        yc = _c_fwd(proj, c_ln_g[l][None], c_ln_b[l][None], c_ws[l], bsb[l], n)
        xn = _merge_fwd(xl, ya, yb, yc, proj, full['a_proj'][l], full['b_proj'][l], full['c_proj'][l], full['w_out'][l])
        saved.append((xl, proj, h, ya, yb, yc, qkvn, shist, alog, dtb))
        xl = xn

    dx, g_final, loss_blk = _loss_head(xl, final_g[None], tgt)
    loss = lax.psum(loss_blk[0, 0], ("x", "y", "c"))

    gfull = {k: [None] * depth for k in WEIGHTS if k != 'final_g'}
    for l in reversed(range(depth)):
        xl, proj, h, ya, yb, yc, qkvn, shist, alog, dtb = saved[l]
        ap, bp, cp, wo = full['a_proj'][l], full['b_proj'][l], full['c_proj'][l], full['w_out'][l]
        dproj, dya, dyb, dyc, dpa, dpb, dpc, merged = _merge_bwd(dx, ya, yb, yc, proj, ap, bp, cp, ap.T, bp.T, cp.T, wo.T)
        gfull['a_proj'][l] = _mm_nn(ya.T.astype(BF16), dpa, "grad_a_proj")
        gfull['b_proj'][l] = _mm_nn(yb.T.astype(BF16), dpb, "grad_b_proj")
        gfull['c_proj'][l] = _mm_nn(yc.T.astype(BF16), dpc, "grad_c_proj")
        gfull['w_out'][l] = _mm_nn(merged.T.astype(BF16), dx, "grad_w_out")
        dproj, g_clg, g_clb, g_cws, g_cbs = _c_bwd(dproj, proj, dyc, c_ln_g[l][None], c_ln_b[l][None], c_ws[l], bsb[l], n)
        dproj, g_adw, g_adb, g_alg, g_alb = _a_bwd(dproj, proj, dya, a_dw32[l], a_dw_b[l][None], a_ln_g[l][None],
                                                   a_ln_b[l][None], bsz, t)
        dproj, dqkvn, g_alog, g_dt, g_og = _delta_bwd(dproj, qkvn, proj, shist, dyb, alog, dtb, b_onorm_g[l][None], bsz, t)
        dproj, g_bconv = _bprep_bwd(dproj, proj, dqkvn, b_conv8[l], bsz, t)
        gfull['w_in'][l] = _unpermute_cols(_mm_nn(h.T, dproj, "grad_w_in"))
        dx, g_ng = _inproj_bwd(dproj, wpt[l], xl, norm_g[l][None], dx)
        gfull['norm_g'][l] = g_ng[0]
        gfull['a_dw'][l] = g_adw[:A_K]
        gfull['a_dw_b'][l], gfull['a_ln_g'][l], gfull['a_ln_b'][l] = g_adb[0], g_alg[0], g_alb[0]
        gfull['b_conv'][l] = g_bconv[:B_K]
        gfull['b_a_log'][l], gfull['b_dt_bias'][l] = g_alog[0, NH:2 * NH], g_dt[0, NH:2 * NH]
        gfull['b_onorm_g'][l] = g_og[0]
        gfull['c_ln_g'][l], gfull['c_ln_b'][l] = g_clg[0], g_clb[0]
        gfull['c_ws'][l], gfull['c_bs'][l] = g_cws, g_cbs[:, :, 0]
    grad_x = dx.reshape(bsz, t, D)
    gfull = {k: jnp.stack(vs) for k, vs in gfull.items()}
    gfull['final_g'] = g_final[0]

    send = jnp.concatenate([_reshard(k, gfull[k]) for k in SHARDED], axis=1)
    rows = _rows_for(send.shape[1], 128)
    send = jnp.pad(send, ((0, 0), (0, rows * PACK_W - send.shape[1]))).reshape(NDEV, rows, PACK_W)
    recv = _all_to_all(send, "exchange_weight_grads")
    outs_s = _reduce_adamw(recv, _pack([wts[k] for k in SHARDED], 128, F32), _pack([mom[k] for k in SHARDED], 128, F32),
                           _pack([vel[k] for k in SHARDED], 128, F32), "adamw_sharded")
    parts_r = _all_gather(_pack([gfull[k] for k in REPL], 8, F32), "gather_replicated_grads")
    outs_r = _reduce_adamw(parts_r, _pack([wts[k] for k in REPL], 8, F32), _pack([mom[k] for k in REPL], 8, F32),
                           _pack([vel[k] for k in REPL], 8, F32), "adamw_replicated")

    res = []
    for o_s, o_r in zip(outs_s, outs_r):
        leaves = dict(zip(SHARDED, _unpack(o_s, [wts[k].shape for k in SHARDED])))
        leaves.update(zip(REPL, _unpack(o_r, [wts[k].shape for k in REPL])))
        res.append([leaves[k] for k in WEIGHTS])
    grads, deltas, new_m, new_v = res
    return (loss, grad_x, *grads, *deltas, *new_m, *new_v)
```

```python
import jax
import jax.numpy as jnp
from jax import lax
from jax.experimental import pallas as pl
from jax.experimental.pallas import tpu as pltpu

F32 = jnp.float32
BF16 = jnp.bfloat16
SDS = jax.ShapeDtypeStruct
MESH = pl.DeviceIdType.MESH

NDEV = 8
D = 1024
EPS = 1e-6
LANE = 128
PACK_W = 1024

C_Q, C_K, C_V = 0, 1024, 2048
C_G = 3072
C_A = 6144
C_C = 7680
C_BZ = 9216
C_BA = 10240
NP = 10368
N_IN = 10256
SEGMENTS = ((0, 1536, C_A), (1536, 4608, C_Q), (4608, 5632, C_BZ), (5632, 5648, C_BA), (5648, 7184, C_C), (7184, 10256, C_G))
W_SHARD = N_IN // NDEV
W_SHARD_PAD = 1408

A_K, A_H = 31, 32
B_K, B_H = 4, 8
CH = 64
SG = 128
NH = 8

ADAM_LR, ADAM_B1, ADAM_B2, ADAM_EPS, ADAM_WD, ADAM_STEP = 0.001, 0.9, 0.999, 1e-08, 0.01, 10

WEIGHTS = ['norm_g', 'w_in', 'a_dw', 'a_dw_b', 'a_ln_g', 'a_ln_b', 'a_proj', 'b_conv', 'b_a_log', 'b_dt_bias',
           'b_onorm_g', 'b_proj', 'c_ln_g', 'c_ln_b', 'c_ws', 'c_bs', 'c_proj', 'w_out', 'final_g']
SHARDED_REST = ['a_dw', 'a_proj', 'b_conv', 'b_proj', 'c_proj', 'w_out']
REPL = [n for n in WEIGHTS if n != 'w_in' and n not in SHARDED_REST]
BIG_REST = ['a_proj', 'b_proj', 'c_proj', 'w_out']
SMALL = ['a_dw', 'b_conv']
ROW_SHARDED = ('b_proj', 'w_out')

NN = ((1,), (0,))
NT = ((1,), (1,))
TN = ((0,), (0,))


def _tile(n, pref):
    return pref if (n >= pref and n % pref == 0) else n


def _cp(vmem_mb):
    return pltpu.CompilerParams(vmem_limit_bytes=vmem_mb * 2 ** 20)


def _full(shape):
    nd = len(shape)
    return pl.BlockSpec(shape, lambda *_: (0,) * nd)


def _rms(x, g):
    return x * lax.rsqrt(jnp.mean(x * x, axis=-1, keepdims=True) + EPS) * g


def _softplus(x):
    return jnp.maximum(x, 0.0) + jnp.log1p(jnp.exp(-jnp.abs(x)))


def _bdot(a, b):
    return jnp.dot(a.astype(BF16), b.astype(BF16), preferred_element_type=F32)


def _mm(a, b, dims):
    return lax.dot_general(a, b, (dims, ((), ())), preferred_element_type=F32)


def _all_gather(xs, name):
    nops = len(xs)

    def body(*refs):
        x_refs, out_refs = refs[:nops], refs[nops:2 * nops]
        send_sems, recv_sems, local_sems = refs[2 * nops:]
        x, y, cc = lax.axis_index("x"), lax.axis_index("y"), lax.axis_index("c")
        me, sibling = (x, y, cc), (x, y, 1 - cc)
        chips = [(1 - x, y), (x, 1 - y), (1 - x, 1 - y)]

        def slot(t, px, py, pc):
            return out_refs[t].at[4 * px + 2 * py + pc]

        def copy(t, k, block, to, src=None):
            return pltpu.make_async_remote_copy(
                src_ref=slot(t, *block) if src is None else src, dst_ref=slot(t, *block),
                send_sem=send_sems.at[7 * t + k], recv_sem=recv_sems.at[7 * t + k], device_id=to, device_id_type=MESH)

        ops = range(nops)
        mine = [pltpu.make_async_copy(x_refs[t], slot(t, *me), local_sems.at[t]) for t in ops]
        for cp in mine:
            cp.start()
        first = [copy(t, 0, me, sibling, src=x_refs[t]) for t in ops]
        first += [copy(t, 1 + j, me, (*chip, cc), src=x_refs[t]) for j, chip in enumerate(chips) for t in ops]
        for cp in first:
            cp.start()
        passed = []
        for j, chip in enumerate(chips):
            for t in ops:
                copy(t, 1 + j, (*chip, cc), me).wait_recv()
                fwd = copy(t, 4 + j, (*chip, cc), sibling)
                fwd.start()
                passed.append(fwd)
        for t in ops:
            copy(t, 0, sibling, me).wait_recv()
        for j, chip in enumerate(chips):
            for t in ops:
                copy(t, 4 + j, (*chip, 1 - cc), me).wait_recv()
        for cp in first + passed:
            cp.wait_send()
        for cp in mine:
            cp.wait()

    hbm = pl.BlockSpec(memory_space=pltpu.HBM)
    return pl.pallas_call(
        body, name=name, out_shape=[SDS((NDEV,) + a.shape, a.dtype) for a in xs],
        in_specs=[hbm] * nops, out_specs=[hbm] * nops,
        scratch_shapes=[pltpu.SemaphoreType.DMA((7 * nops,)), pltpu.SemaphoreType.DMA((7 * nops,)),
                        pltpu.SemaphoreType.DMA((nops,))],
    )(*xs)


def _all_to_all(sends, name):
    nops = len(sends)

    def body(*refs):
        in_refs, out_refs = refs[:nops], refs[nops:2 * nops]
        send_sems, recv_sems, local_sems = refs[2 * nops:]
        x, y, cc = lax.axis_index("x"), lax.axis_index("y"), lax.axis_index("c")
        me = 4 * x + 2 * y + cc
        mine = [pltpu.make_async_copy(in_refs[t].at[me], out_refs[t].at[me], local_sems.at[t]) for t in range(nops)]
        for cp in mine:
            cp.start()
        out, landed = [], []
        for k in range(1, NDEV):
            px = 1 - x if k & 4 else x
            py = 1 - y if k & 2 else y
            pc = 1 - cc if k & 1 else cc
            peer = 4 * px + 2 * py + pc
            for t in range(nops):
                sem = 7 * t + k - 1
                out.append(pltpu.make_async_remote_copy(
                    src_ref=in_refs[t].at[peer], dst_ref=out_refs[t].at[me], send_sem=send_sems.at[sem],
                    recv_sem=recv_sems.at[sem], device_id=(px, py, pc), device_id_type=MESH))
                landed.append(pltpu.make_async_remote_copy(
                    src_ref=in_refs[t].at[me], dst_ref=out_refs[t].at[peer], send_sem=send_sems.at[sem],
                    recv_sem=recv_sems.at[sem], device_id=(px, py, pc), device_id_type=MESH))
        for cp in out:
            cp.start()
        for cp in landed:
            cp.wait_recv()
        for cp in out:
            cp.wait_send()
        for cp in mine:
            cp.wait()

    hbm = pl.BlockSpec(memory_space=pltpu.HBM)
    return pl.pallas_call(
        body, name=name, out_shape=[SDS(a.shape, a.dtype) for a in sends],
        in_specs=[hbm] * nops, out_specs=[hbm] * nops,
        scratch_shapes=[pltpu.SemaphoreType.DMA((7 * nops,)), pltpu.SemaphoreType.DMA((7 * nops,)),
                        pltpu.SemaphoreType.DMA((nops,))],
    )(*sends)


def _inproj(x2d, g_row, wp):
    n = x2d.shape[0]
    tm, tn = _tile(n, 1024), 1152

    def body(x_ref, g_ref, w_ref, proj_ref, h_ref, hs):
        @pl.when(pl.program_id(1) == 0)
        def _():
            h = _rms(x_ref[...], g_ref[...]).astype(BF16)
            hs[...] = h
            h_ref[...] = h

        proj_ref[...] = jnp.dot(hs[...], w_ref[...], preferred_element_type=F32)

    return pl.pallas_call(
        body, name="inproj", grid=(n // tm, NP // tn),
        in_specs=[pl.BlockSpec((tm, D), lambda i, j: (i, 0)), _full((1, D)), pl.BlockSpec((D, tn), lambda i, j: (0, j))],
        out_specs=[pl.BlockSpec((tm, tn), lambda i, j: (i, j)), pl.BlockSpec((tm, D), lambda i, j: (i, 0))],
        out_shape=[SDS((n, NP), F32), SDS((n, D), BF16)],
        scratch_shapes=[pltpu.VMEM((tm, D), BF16)], compiler_params=_cp(48),
    )(x2d, g_row, wp)


def _mm_nn(at, b, name):
    m, nn = at.shape
    k = b.shape[1]
    tk = 1152 if k % 1152 == 0 else _tile(k, 1024)
    tn = _tile(nn, 1024)

    def body(a_ref, b_ref, o_ref):
        p = jnp.dot(a_ref[...], b_ref[...].astype(BF16), preferred_element_type=F32)

        @pl.when(pl.program_id(1) == 0)
        def _():
            o_ref[...] = p

        @pl.when(pl.program_id(1) > 0)
        def _():
            o_ref[...] += p

    return pl.pallas_call(
        body, name=name, grid=(k // tk, nn // tn),
        in_specs=[pl.BlockSpec((m, tn), lambda j, t: (0, t)), pl.BlockSpec((tn, tk), lambda j, t: (t, j))],
        out_specs=pl.BlockSpec((m, tk), lambda j, t: (0, j)),
        out_shape=SDS((m, k), F32), compiler_params=_cp(48),
    )(at, b)


def _inproj_bwd(dproj, wpt, x2d, g_row, dxo):
    n = x2d.shape[0]
    tm, tk = _tile(n, 512), 1152
    nk = NP // tk

    def body(dp_ref, w_ref, x_ref, g_ref, dxo_ref, dx_ref, dg_ref, acc):
        i, k = pl.program_id(0), pl.program_id(1)
        p = jnp.dot(dp_ref[...].astype(BF16), w_ref[...], preferred_element_type=F32)

        @pl.when(k == 0)
        def _():
            acc[...] = p

        @pl.when(k > 0)
        def _():
            acc[...] += p

        @pl.when(k == nk - 1)
        def _():
            _, vjp = jax.vjp(_rms, x_ref[...], g_ref[...])
            dx, dg = vjp(acc[...])
            dx_ref[...] = dxo_ref[...] + dx

            @pl.when(i == 0)
            def _():
                dg_ref[...] = dg

            @pl.when(i > 0)
            def _():
                dg_ref[...] += dg

    return pl.pallas_call(
        body, name="inproj_bwd", grid=(n // tm, nk),
        in_specs=[pl.BlockSpec((tm, tk), lambda i, k: (i, k)), pl.BlockSpec((tk, D), lambda i, k: (k, 0)),
                  pl.BlockSpec((tm, D), lambda i, k: (i, 0)), _full((1, D)), pl.BlockSpec((tm, D), lambda i, k: (i, 0))],
        out_specs=[pl.BlockSpec((tm, D), lambda i, k: (i, 0)), _full((1, D))],
        out_shape=[SDS((n, D), F32), SDS((1, D), F32)],
        scratch_shapes=[pltpu.VMEM((tm, D), F32)], compiler_params=_cp(48),
    )(dproj, wpt, x2d, g_row, dxo)


def _loss_head(x2d, g_row, tgt):
    n = x2d.shape[0]
    tm = _tile(n, 512)

    def body(x_ref, g_ref, t_ref, dx_ref, dg_ref, loss_ref):
        i = pl.program_id(0)
        y, vjp = jax.vjp(_rms, x_ref[...], g_ref[...])
        err = y - t_ref[...]
        part = 0.5 * jnp.sum(jnp.mean(err * err, axis=-1, keepdims=True), axis=0, keepdims=True)
        dx, dg = vjp(err * (1.0 / D))
        dx_ref[...] = dx
        lb = jnp.broadcast_to(part, (8, LANE))

        @pl.when(i == 0)
        def _():
            dg_ref[...] = dg
            loss_ref[...] = lb

        @pl.when(i > 0)
        def _():
            dg_ref[...] += dg
            loss_ref[...] += lb

    return pl.pallas_call(
        body, name="loss_head", grid=(n // tm,),
        in_specs=[pl.BlockSpec((tm, D), lambda i: (i, 0)), _full((1, D)), pl.BlockSpec((tm, D), lambda i: (i, 0))],
        out_specs=[pl.BlockSpec((tm, D), lambda i: (i, 0)), _full((1, D)), _full((8, LANE))],
        out_shape=[SDS((n, D), F32), SDS((1, D), F32), SDS((8, LANE), F32)], compiler_params=_cp(40),
    )(x2d, g_row, tgt)


def _conv_fwd(ext_ref, w_ref, kw, halo, tt):
    acc = None
    for j in range(kw):
        term = w_ref[j:j + 1, :] * ext_ref[pl.ds(halo - (kw - 1) + j, tt), :]
        acc = term if acc is None else acc + term
    return acc


def _conv_bwd_x(dcp_ref, w_ref, kw, halo, tt):
    acc = None
    for j in range(kw):
        term = w_ref[j:j + 1, :] * dcp_ref[pl.ds(kw - 1 - j, tt + halo), :]
        acc = term if acc is None else acc + term
    return acc


def _conv_bwd_w(dc, ext_ref, gw_ref, kw, halo, tt):
    for j in range(kw):
        gw_ref[j:j + 1, :] += jnp.sum(dc * ext_ref[pl.ds(halo - (kw - 1) + j, tt), :], axis=0, keepdims=True)


def _a_post(c, z, g, b):
    mu = jnp.mean(c, axis=-1, keepdims=True)
    var = jnp.mean(jnp.square(c - mu), axis=-1, keepdims=True)
    a = (c - mu) * lax.rsqrt(var + EPS) * g + b
    return jax.nn.silu(a) * jax.nn.silu(z)


def _a_fwd(proj, dw, b_row, lg, lb, bsz, t):
    n = bsz * t
    tt = _tile(t, 256)
    nt = t // tt
    cb = C_A // 512

    def body(v_ref, g_ref, z_ref, vh_ref, gh_ref, w_ref, b_ref, lg_ref, lb_ref, y_ref, ext):
        i = pl.program_id(1)
        ext[0:A_H, :] = jnp.where(i > 0, vh_ref[...] * jax.nn.sigmoid(gh_ref[...]), 0.0)
        ext[A_H:, :] = v_ref[...] * jax.nn.sigmoid(g_ref[...])
        c = _conv_fwd(ext, w_ref, A_K, A_H, tt) + b_ref[...]
        y_ref[...] = _a_post(c, z_ref[...], lg_ref[...], lb_ref[...])

    def row(b, i):
        return b * nt + i

    def halo(b, i):
        return jnp.maximum((b * t + i * tt) // A_H - 1, 0)

    return pl.pallas_call(
        body, name="a_fwd", grid=(bsz, nt),
        in_specs=[pl.BlockSpec((tt, 512), lambda b, i: (row(b, i), cb)),
                  pl.BlockSpec((tt, 512), lambda b, i: (row(b, i), cb + 1)),
                  pl.BlockSpec((tt, 512), lambda b, i: (row(b, i), cb + 2)),
                  pl.BlockSpec((A_H, 512), lambda b, i: (halo(b, i), cb)),
                  pl.BlockSpec((A_H, 512), lambda b, i: (halo(b, i), cb + 1)),
                  _full((32, 512)), _full((1, 512)), _full((1, 512)), _full((1, 512))],
        out_specs=pl.BlockSpec((tt, 512), lambda b, i: (row(b, i), 0)),
        out_shape=SDS((n, 512), F32),
        scratch_shapes=[pltpu.VMEM((tt + A_H, 512), F32)], compiler_params=_cp(40),
    )(proj, proj, proj, proj, proj, dw, b_row, lg, lb)


def _a_bwd(dproj, proj, dy, dw, b_row, lg, lb, bsz, t):
    n = bsz * t
    tt = _tile(t, 256)
    nt = t // tt
    cb = C_A // 512

    def body(dp_any, v_ref, g_ref, z_ref, vh_ref, gh_ref, dy_ref, w_ref, b_ref, lg_ref, lb_ref,
             dp_ref, gw_ref, gb_ref, glg_ref, glb_ref, ext, dcp, dae, carry):
        b, i = pl.program_id(0), pl.program_id(1)
        ti = nt - 1 - i

        @pl.when((b == 0) & (i == 0))
        def _():
            gw_ref[...] = jnp.zeros_like(gw_ref)
            gb_ref[...] = jnp.zeros_like(gb_ref)
            glg_ref[...] = jnp.zeros_like(glg_ref)
            glb_ref[...] = jnp.zeros_like(glb_ref)

        @pl.when(i == 0)
        def _():
            carry[...] = jnp.zeros_like(carry)

        val, glu = v_ref[...], g_ref[...]
        sg = jax.nn.sigmoid(glu)
        ext[0:A_H, :] = jnp.where(ti > 0, vh_ref[...] * jax.nn.sigmoid(gh_ref[...]), 0.0)
        ext[A_H:, :] = val * sg
        c = _conv_fwd(ext, w_ref, A_K, A_H, tt) + b_ref[...]
        _, vjp = jax.vjp(_a_post, c, z_ref[...], lg_ref[...], lb_ref[...])
        dc, dz, dlg, dlb = vjp(dy_ref[...])
        gb_ref[...] += jnp.sum(dc, axis=0, keepdims=True)
        glg_ref[...] += dlg
        glb_ref[...] += dlb
        dcp[0:A_H, :] = jnp.zeros((A_H, 512), F32)
        dcp[A_H:A_H + tt, :] = dc
        dcp[A_H + tt:, :] = jnp.zeros((A_H, 512), F32)
        _conv_bwd_w(dc, ext, gw_ref, A_K, A_H, tt)
        dae[...] = _conv_bwd_x(dcp, w_ref, A_K, A_H, tt)
        dae[tt:tt + A_H, :] += carry[...]
        carry[...] = dae[0:A_H, :]
        da = dae[A_H:, :]
        dp_ref[:, 0:512] = da * sg
        dp_ref[:, 512:1024] = da * val * sg * (1.0 - sg)
        dp_ref[:, 1024:1536] = dz

    def row(b, i):
        return b * nt + (nt - 1 - i)

    def halo(b, i):
        return jnp.maximum((b * t + (nt - 1 - i) * tt) // A_H - 1, 0)

    outs = pl.pallas_call(
        body, name="a_bwd", grid=(bsz, nt),
        in_specs=[pl.BlockSpec(memory_space=pl.ANY),
                  pl.BlockSpec((tt, 512), lambda b, i: (row(b, i), cb)),
                  pl.BlockSpec((tt, 512), lambda b, i: (row(b, i), cb + 1)),
                  pl.BlockSpec((tt, 512), lambda b, i: (row(b, i), cb + 2)),
                  pl.BlockSpec((A_H, 512), lambda b, i: (halo(b, i), cb)),
                  pl.BlockSpec((A_H, 512), lambda b, i: (halo(b, i), cb + 1)),
                  pl.BlockSpec((tt, 512), lambda b, i: (row(b, i), 0)),
                  _full((32, 512)), _full((1, 512)), _full((1, 512)), _full((1, 512))],
        out_specs=[pl.BlockSpec((tt, 1536), lambda b, i: (row(b, i), C_A // 1536)),
                   _full((32, 512)), _full((1, 512)), _full((1, 512)), _full((1, 512))],
        out_shape=[SDS((n, NP), F32), SDS((32, 512), F32), SDS((1, 512), F32), SDS((1, 512), F32), SDS((1, 512), F32)],
        input_output_aliases={0: 0},
        scratch_shapes=[pltpu.VMEM((tt + A_H, 512), F32), pltpu.VMEM((tt + 2 * A_H, 512), F32),
                        pltpu.VMEM((tt + A_H, 512), F32), pltpu.VMEM((A_H, 512), F32)],
        compiler_params=_cp(48),
    )(dproj, proj, proj, proj, proj, proj, dy, dw, b_row, lg, lb)
    return outs


def _b_post(blocks):
    out = []
    for idx, c in enumerate(blocks):
        s = jax.nn.silu(c)
        if idx < 2 * NH:
            s = s * lax.rsqrt(jnp.sum(s * s, axis=-1, keepdims=True) + EPS)
            if idx < NH:
                s = s * (LANE ** -0.5)
        out.append(s)
    return out


def _bprep_fwd(proj, wconv, bsz, t):
    n = bsz * t
    tt = _tile(t, 256)
    nt = t // tt

    def body(x_ref, xh_ref, w_ref, o_ref, ext):
        i = pl.program_id(1)
        ext[0:B_H, :] = jnp.where(i > 0, xh_ref[...], 0.0)
        ext[B_H:, :] = x_ref[...]
        c = _conv_fwd(ext, w_ref, B_K, B_H, tt)
        outs = _b_post([c[:, LANE * j:LANE * (j + 1)] for j in range(3 * NH)])
        for j, o in enumerate(outs):
            o_ref[:, LANE * j:LANE * (j + 1)] = o

    return pl.pallas_call(
        body, name="bprep_fwd", grid=(bsz, nt),
        in_specs=[pl.BlockSpec((tt, 3072), lambda b, i: (b * nt + i, 0)),
                  pl.BlockSpec((B_H, 3072), lambda b, i: (jnp.maximum((b * t + i * tt) // B_H - 1, 0), 0)),
                  _full((8, 3072))],
        out_specs=pl.BlockSpec((tt, 3072), lambda b, i: (b * nt + i, 0)),
        out_shape=SDS((n, 3072), F32),
        scratch_shapes=[pltpu.VMEM((tt + B_H, 3072), F32)], compiler_params=_cp(48),
    )(proj, proj, wconv)


def _bprep_bwd(dproj, proj, dqkvn, wconv, bsz, t):
    n = bsz * t
    tt = _tile(t, 256)
    nt = t // tt

    def body(dp_any, x_ref, xh_ref, dq_ref, w_ref, dp_ref, gw_ref, ext, dcp, dae, carry):
        b, i = pl.program_id(0), pl.program_id(1)
        ti = nt - 1 - i

        @pl.when((b == 0) & (i == 0))
        def _():
            gw_ref[...] = jnp.zeros_like(gw_ref)

        @pl.when(i == 0)
        def _():
            carry[...] = jnp.zeros_like(carry)

        ext[0:B_H, :] = jnp.where(ti > 0, xh_ref[...], 0.0)
        ext[B_H:, :] = x_ref[...]
        c = _conv_fwd(ext, w_ref, B_K, B_H, tt)
        _, vjp = jax.vjp(_b_post, [c[:, LANE * j:LANE * (j + 1)] for j in range(3 * NH)])
        (dcs,) = vjp([dq_ref[:, LANE * j:LANE * (j + 1)] for j in range(3 * NH)])
        dcp[0:B_H, :] = jnp.zeros((B_H, 3072), F32)
        for j, dcj in enumerate(dcs):
            dcp[B_H:B_H + tt, LANE * j:LANE * (j + 1)] = dcj
        dcp[B_H + tt:, :] = jnp.zeros((B_H, 3072), F32)
        _conv_bwd_w(dcp[B_H:B_H + tt, :], ext, gw_ref, B_K, B_H, tt)
        dae[...] = _conv_bwd_x(dcp, w_ref, B_K, B_H, tt)
        dae[tt:tt + B_H, :] += carry[...]
        carry[...] = dae[0:B_H, :]
        dp_ref[...] = dae[B_H:, :]

    def row(b, i):
        return b * nt + (nt - 1 - i)

    return pl.pallas_call(
        body, name="bprep_bwd", grid=(bsz, nt),
        in_specs=[pl.BlockSpec(memory_space=pl.ANY),
                  pl.BlockSpec((tt, 3072), lambda b, i: (row(b, i), 0)),
                  pl.BlockSpec((B_H, 3072), lambda b, i: (jnp.maximum((b * t + (nt - 1 - i) * tt) // B_H - 1, 0), 0)),
                  pl.BlockSpec((tt, 3072), lambda b, i: (row(b, i), 0)),
                  _full((8, 3072))],
        out_specs=[pl.BlockSpec((tt, 3072), lambda b, i: (row(b, i), 0)), _full((8, 3072))],
        out_shape=[SDS((n, NP), F32), SDS((8, 3072), F32)],
        input_output_aliases={0: 0},
        scratch_shapes=[pltpu.VMEM((tt + B_H, 3072), F32), pltpu.VMEM((tt + 2 * B_H, 3072), F32),
                        pltpu.VMEM((tt + B_H, 3072), F32), pltpu.VMEM((B_H, 3072), F32)],
        compiler_params=_cp(56),
    )(dproj, proj, proj, dqkvn, wconv)


def _split2(a):
    hi = a.astype(BF16)
    return hi, (a - hi.astype(F32)).astype(BF16)


def _split3(a):
    p1 = a.astype(BF16)
    r1 = a - p1.astype(F32)
    p2 = r1.astype(BF16)
    return p1, p2, (r1 - p2.astype(F32)).astype(BF16)


def _dot3_raw(a, b, dims):
    a1, a2 = _split2(a)
    b1, b2 = _split2(b)
    return _mm(a1, b1, dims) + (_mm(a1, b2, dims) + _mm(a2, b1, dims))


def _dot6(a, b, dims):
    a1, a2, a3 = _split3(a)
    b1, b2, b3 = _split3(b)
    return (_mm(a1, b1, dims) + (_mm(a1, b2, dims) + _mm(a2, b1, dims))
            + (_mm(a1, b3, dims) + _mm(a2, b2, dims) + _mm(a3, b1, dims)))


def _unit_lower_inverse_raw(lmats):
    r = lax.broadcasted_iota(jnp.int32, (CH, CH), 0)
    c = lax.broadcasted_iota(jnp.int32, (CH, CH), 1)
    eye = (r == c).astype(F32)
    blk = jnp.right_shift(r, 4) == jnp.right_shift(c, 4)
    dm = [jnp.where(blk, x, 0.0) for x in lmats]
    om = [a - b for a, b in zip(lmats, dm)]
    d2 = [_dot3_raw(x, x, NN) for x in dm]
    d4 = [_dot3_raw(x, x, NN) for x in d2]
    d8 = [_dot3_raw(x, x, NN) for x in d4]
    p = [_dot3_raw(eye - a, eye + b, NN) for a, b in zip(dm, d2)]
    p = [_dot3_raw(a, eye + b, NN) for a, b in zip(p, d4)]
    p = [_dot3_raw(a, eye + b, NN) for a, b in zip(p, d8)]
    m = [_dot3_raw(a, b, NN) for a, b in zip(p, om)]
    m2 = [_dot3_raw(x, x, NN) for x in m]
    t = [_dot3_raw(eye - a, eye + b, NN) for a, b in zip(m, m2)]
    return [_dot3_raw(a, b, NN) for a, b in zip(t, p)]


@jax.custom_vjp
def _unit_lower_inverse(lmats):
    return _unit_lower_inverse_raw(lmats)


def _unit_lower_inverse_fwd(lmats):
    tinv = _unit_lower_inverse_raw(lmats)
    return tinv, tinv


def _unit_lower_inverse_bwd(tinv, gs):
    x = [_dot6(t, g, TN) for t, g in zip(tinv, gs)]
    return ([-_dot6(a, t, NT) for a, t in zip(x, tinv)],)


_unit_lower_inverse.defvjp(_unit_lower_inverse_fwd, _unit_lower_inverse_bwd)


def _tri(lower):
    r = lax.broadcasted_iota(jnp.int32, (CH, CH), 0)
    c = lax.broadcasted_iota(jnp.int32, (CH, CH), 1)
    return ((r >= c) if lower else (r <= c)).astype(BF16)


def _tri_dot(x, lower, dims, tri_first):
    p1, p2, p3 = _split3(x)
    tri = _tri(lower)
    if tri_first:
        return _mm(tri, p1, dims) + (_mm(tri, p2, dims) + _mm(tri, p3, dims))
    return _mm(p1, tri, dims) + (_mm(p2, tri, dims) + _mm(p3, tri, dims))


@jax.custom_vjp
def _cumsum_rows(x):
    return _tri_dot(x, True, NN, True)


def _cumsum_rows_fwd(x):
    return _tri_dot(x, True, NN, True), None


def _cumsum_rows_bwd(_, g):
    return (_tri_dot(g, True, TN, True),)


_cumsum_rows.defvjp(_cumsum_rows_fwd, _cumsum_rows_bwd)


@jax.custom_vjp
def _cumsum_rows_t(x):
    return _tri_dot(x, False, TN, False)


def _cumsum_rows_t_fwd(x):
    return _tri_dot(x, False, TN, False), None


def _cumsum_rows_t_bwd(_, g):
    return (_tri_dot(g, False, NT, True),)


_cumsum_rows_t.defvjp(_cumsum_rows_t_fwd, _cumsum_rows_t_bwd)


def _delta_chunk(ss, qs, ks, vs, ba, zs, alog, dtb, og):
    heads = range(NH)
    lane = lax.broadcasted_iota(jnp.int32, (1, LANE), 1)
    r = lax.broadcasted_iota(jnp.int32, (CH, CH), 0)
    c = lax.broadcasted_iota(jnp.int32, (CH, CH), 1)
    ri = lax.broadcasted_iota(jnp.int32, (CH, 1), 0)
    incl, strict = r >= c, r > c

    def pick(x, h):
        return jnp.sum(jnp.where(lane == h, x, 0.0), axis=-1, keepdims=True)

    beta = [jax.nn.sigmoid(pick(ba, h)) for h in heads]
    g = [-jnp.exp(pick(alog, h + NH)) * _softplus(pick(ba, h + NH) + pick(dtb, h + NH)) for h in heads]
    gb = [jnp.broadcast_to(x, (CH, CH)) for x in g]
    gca = [_cumsum_rows(x) for x in gb]
    gcr = [_cumsum_rows_t(x) for x in gb]
    gc = [jnp.sum(jnp.where(c == 0, x, 0.0), axis=-1, keepdims=True) for x in gca]
    gl = [jnp.sum(jnp.where(ri == CH - 1, x, 0.0), axis=0, keepdims=True) for x in gc]
    diff = [a - b for a, b in zip(gca, gcr)]
    gam_s = [jnp.where(strict, jnp.exp(jnp.where(strict, x, 0.0)), 0.0) for x in diff]
    gam_i = [jnp.where(incl, jnp.exp(jnp.where(incl, x, 0.0)), 0.0) for x in diff]

    kk = [_mm(k, k, NT) for k in ks]
    tinv = _unit_lower_inverse([beta[h] * kk[h] * gam_s[h] for h in heads])

    eg = [jnp.exp(x) for x in gc]
    u = [_mm(tinv[h], vs[h] * beta[h], NN) for h in heads]
    w = [_mm(tinv[h], ks[h] * (beta[h] * eg[h]), NN) for h in heads]
    qk = [_mm(qs[h], ks[h], NT) * gam_i[h] for h in heads]
    vn = [u[h] - _mm(w[h], ss[h], NN) for h in heads]
    o = [_mm(qs[h] * eg[h], ss[h], NN) + _mm(qk[h], vn[h], NN) for h in heads]
    sn = [jnp.exp(gl[h]) * ss[h] + _mm(ks[h] * jnp.exp(gl[h] - gc[h]), vn[h], TN) for h in heads]
    y = [_rms(o[h], og) * jax.nn.silu(zs[h]) for h in heads]
    return sn, y


def _head_blocks(ref, base=0):
    return [ref[:, base + LANE * h:base + LANE * (h + 1)] for h in range(NH)]


def _delta_fwd(qkvn, proj, alog, dtb, og, bsz, t):
    n = bsz * t
    nc = t // CH

    def body(q_ref, k_ref, v_ref, ba_ref, z_ref, al_ref, dt_ref, og_ref, y_ref, sh_ref, s_scr):
        @pl.when(pl.program_id(1) == 0)
        def _():
            s_scr[...] = jnp.zeros_like(s_scr)

        ss = [s_scr[h] for h in range(NH)]
        for h in range(NH):
            sh_ref[h] = ss[h]
        sn, y = _delta_chunk(ss, _head_blocks(q_ref), _head_blocks(k_ref), _head_blocks(v_ref), ba_ref[...],
                             _head_blocks(z_ref), al_ref[...], dt_ref[...], og_ref[...])
        for h in range(NH):
            s_scr[h] = sn[h]
            y_ref[:, LANE * h:LANE * (h + 1)] = y[h]

    def blk(width, col):
        return pl.BlockSpec((CH, width), lambda b, ci: (b * nc + ci, col))

    return pl.pallas_call(
        body, name="delta_fwd", grid=(bsz, nc),
        in_specs=[blk(D, 0), blk(D, 1), blk(D, 2), blk(LANE, C_BA // LANE), blk(D, C_BZ // D),
                  _full((1, LANE)), _full((1, LANE)), _full((1, LANE))],
        out_specs=[blk(D, 0), pl.BlockSpec((None, None, NH, LANE, LANE), lambda b, ci: (b, ci, 0, 0, 0))],
        out_shape=[SDS((n, D), F32), SDS((bsz, nc, NH, LANE, LANE), F32)],
        scratch_shapes=[pltpu.VMEM((NH, LANE, LANE), F32)], compiler_params=_cp(40),
    )(qkvn, qkvn, qkvn, proj, proj, alog, dtb, og)


def _delta_bwd(dproj, qkvn, proj, shist, dyb, alog, dtb, og, bsz, t):
    n = bsz * t
    nc = t // CH

    def body(dp_any, q_ref, k_ref, v_ref, ba_ref, z_ref, sh_ref, dy_ref, al_ref, dt_ref, og_ref,
             dp_ref, dqkv_ref, gal_ref, gdt_ref, gog_ref, ds_scr):
        b, ci = pl.program_id(0), pl.program_id(1)

        @pl.when((b == 0) & (ci == 0))
        def _():
            gal_ref[...] = jnp.zeros_like(gal_ref)
            gdt_ref[...] = jnp.zeros_like(gdt_ref)
            gog_ref[...] = jnp.zeros_like(gog_ref)

        @pl.when(ci == 0)
        def _():
            ds_scr[...] = jnp.zeros_like(ds_scr)

        _, vjp = jax.vjp(_delta_chunk, [sh_ref[h] for h in range(NH)], _head_blocks(q_ref), _head_blocks(k_ref),
                         _head_blocks(v_ref), ba_ref[...], _head_blocks(z_ref), al_ref[...], dt_ref[...], og_ref[...])
        ds, dq, dk, dv, dba, dz, dal, ddt, dog = vjp(([ds_scr[h] for h in range(NH)], _head_blocks(dy_ref)))
        gal_ref[...] += dal
        gdt_ref[...] += ddt
        gog_ref[...] += dog
        dp_ref[:, D:D + LANE] = dba
        for h in range(NH):
            ds_scr[h] = ds[h]
            dp_ref[:, LANE * h:LANE * (h + 1)] = dz[h]
            dqkv_ref[:, LANE * h:LANE * (h + 1)] = dq[h]
            dqkv_ref[:, D + LANE * h:D + LANE * (h + 1)] = dk[h]
            dqkv_ref[:, 2 * D + LANE * h:2 * D + LANE * (h + 1)] = dv[h]

    def blk(width, col):
        return pl.BlockSpec((CH, width), lambda b, ci: (b * nc + (nc - 1 - ci), col))

    return pl.pallas_call(
        body, name="delta_bwd", grid=(bsz, nc),
        in_specs=[pl.BlockSpec(memory_space=pl.ANY), blk(D, 0), blk(D, 1), blk(D, 2), blk(LANE, C_BA // LANE),
                  blk(D, C_BZ // D),
                  pl.BlockSpec((None, None, NH, LANE, LANE), lambda b, ci: (b, nc - 1 - ci, 0, 0, 0)),
                  blk(D, 0), _full((1, LANE)), _full((1, LANE)), _full((1, LANE))],
        out_specs=[blk(D + LANE, C_BZ // (D + LANE)), blk(3 * D, 0), _full((1, LANE)), _full((1, LANE)), _full((1, LANE))],
        out_shape=[SDS((n, NP), F32), SDS((n, 3 * D), F32), SDS((1, LANE), F32), SDS((1, LANE), F32), SDS((1, LANE), F32)],
        input_output_aliases={0: 0},
        scratch_shapes=[pltpu.VMEM((NH, LANE, LANE), F32)], compiler_params=_cp(48),
    )(dproj, qkvn, qkvn, qkvn, proj, proj, shist, dyb, alog, dtb, og)


def _c_chunk(us, vs, zs, lgs, lbs, ws, bsb):
    gv = [jax.nn.gelu(v) for v in vs]
    width = LANE * len(gv)
    mu = sum(jnp.sum(x, axis=-1, keepdims=True) for x in gv) / width
    var = sum(jnp.sum(jnp.square(x - mu), axis=-1, keepdims=True) for x in gv) / width
    rstd = lax.rsqrt(var + EPS)
    r = lax.broadcasted_iota(jnp.int32, (SG, SG), 0)
    c = lax.broadcasted_iota(jnp.int32, (SG, SG), 1)
    out = []
    for j in range(len(gv)):
        nrm = (gv[j] - mu) * rstd * lgs[j] + lbs[j]
        mixed = jnp.dot(jnp.where(r >= c, ws[j], 0.0), nrm, preferred_element_type=F32) + bsb[j]
        out.append(jax.nn.gelu(us[j]) * mixed * jax.nn.silu(zs[j]))
    return out


def _c_args(u_ref, v_ref, z_ref, lg_ref, lb_ref, ws_ref, bs_ref):
    sl = [slice(LANE * j, LANE * (j + 1)) for j in range(4)]
    return ([u_ref[:, s] for s in sl], [v_ref[:, s] for s in sl], [z_ref[:, s] for s in sl],
            [lg_ref[:, s] for s in sl], [lb_ref[:, s] for s in sl],
            [ws_ref[j] for j in range(4)], [bs_ref[j] for j in range(4)])


def _c_fwd(proj, lg, lb, ws, bsb, n):
    cb = C_C // 512

    def body(u_ref, v_ref, z_ref, lg_ref, lb_ref, ws_ref, bs_ref, y_ref):
        outs = _c_chunk(*_c_args(u_ref, v_ref, z_ref, lg_ref, lb_ref, ws_ref, bs_ref))
        for j, o in enumerate(outs):
            y_ref[:, LANE * j:LANE * (j + 1)] = o

    return pl.pallas_call(
        body, name="c_fwd", grid=(n // SG,),
        in_specs=[pl.BlockSpec((SG, 512), lambda i: (i, cb)), pl.BlockSpec((SG, 512), lambda i: (i, cb + 1)),
                  pl.BlockSpec((SG, 512), lambda i: (i, cb + 2)), _full((1, 512)), _full((1, 512)),
                  _full((4, SG, SG)), _full((4, SG, SG))],
        out_specs=pl.BlockSpec((SG, 512), lambda i: (i, 0)),
        out_shape=SDS((n, 512), F32), compiler_params=_cp(32),
    )(proj, proj, proj, lg, lb, ws, bsb)


def _c_bwd(dproj, proj, dy, lg, lb, ws, bsb, n):
    cb = C_C // 512

    def body(dp_any, u_ref, v_ref, z_ref, dy_ref, lg_ref, lb_ref, ws_ref, bs_ref,
             dp_ref, glg_ref, glb_ref, gws_ref, gbs_ref):
        @pl.when(pl.program_id(0) == 0)
        def _():
            glg_ref[...] = jnp.zeros_like(glg_ref)
            glb_ref[...] = jnp.zeros_like(glb_ref)
            gws_ref[...] = jnp.zeros_like(gws_ref)
            gbs_ref[...] = jnp.zeros_like(gbs_ref)

        _, vjp = jax.vjp(_c_chunk, *_c_args(u_ref, v_ref, z_ref, lg_ref, lb_ref, ws_ref, bs_ref))
        dus, dvs, dzs, dlgs, dlbs, dwss, dbss = vjp([dy_ref[:, LANE * j:LANE * (j + 1)] for j in range(4)])
        for j in range(4):
            sl = slice(LANE * j, LANE * (j + 1))
            dp_ref[:, LANE * j:LANE * (j + 1)] = dus[j]
            dp_ref[:, 512 + LANE * j:512 + LANE * (j + 1)] = dvs[j]
            dp_ref[:, 1024 + LANE * j:1024 + LANE * (j + 1)] = dzs[j]
            glg_ref[:, sl] += dlgs[j]
            glb_ref[:, sl] += dlbs[j]
            gws_ref[j] += dwss[j]
            gbs_ref[j] += jnp.broadcast_to(jnp.sum(dbss[j], axis=-1, keepdims=True), (SG, SG))

    return pl.pallas_call(
        body, name="c_bwd", grid=(n // SG,),
        in_specs=[pl.BlockSpec(memory_space=pl.ANY),
                  pl.BlockSpec((SG, 512), lambda i: (i, cb)), pl.BlockSpec((SG, 512), lambda i: (i, cb + 1)),
                  pl.BlockSpec((SG, 512), lambda i: (i, cb + 2)), pl.BlockSpec((SG, 512), lambda i: (i, 0)),
                  _full((1, 512)), _full((1, 512)), _full((4, SG, SG)), _full((4, SG, SG))],
        out_specs=[pl.BlockSpec((SG, 1536), lambda i: (i, C_C // 1536)),
                   _full((1, 512)), _full((1, 512)), _full((4, SG, SG)), _full((4, SG, SG))],
        out_shape=[SDS((n, NP), F32), SDS((1, 512), F32), SDS((1, 512), F32), SDS((4, SG, SG), F32), SDS((4, SG, SG), F32)],
        input_output_aliases={0: 0}, compiler_params=_cp(32),
    )(dproj, proj, proj, proj, dy, lg, lb, ws, bsb)


def _merge_fwd(x2d, ya, yb, yc, proj, ap, bp, cp, wo):
    n = x2d.shape[0]
    tm = _tile(n, 256)
    gb = C_G // D

    def body(x_ref, ya_ref, yb_ref, yc_ref, g0_ref, g1_ref, g2_ref, ap_ref, bp_ref, cp_ref, wo_ref, o_ref):
        merged = (jax.nn.sigmoid(g0_ref[...]) * _bdot(ya_ref[...], ap_ref[...])
                  + jax.nn.sigmoid(g1_ref[...]) * _bdot(yb_ref[...], bp_ref[...])
                  + jax.nn.sigmoid(g2_ref[...]) * _bdot(yc_ref[...], cp_ref[...]))
        o_ref[...] = x_ref[...] + _bdot(merged, wo_ref[...])

    def rows(w):
        return pl.BlockSpec((tm, w), lambda i: (i, 0))

    return pl.pallas_call(
        body, name="merge_fwd", grid=(n // tm,),
        in_specs=[rows(D), rows(512), rows(D), rows(512),
                  pl.BlockSpec((tm, D), lambda i: (i, gb)), pl.BlockSpec((tm, D), lambda i: (i, gb + 1)),
                  pl.BlockSpec((tm, D), lambda i: (i, gb + 2)),
                  _full((512, D)), _full((D, D)), _full((512, D)), _full((D, D))],
        out_specs=rows(D), out_shape=SDS((n, D), F32), compiler_params=_cp(48),
    )(x2d, ya, yb, yc, proj, proj, proj, ap, bp, cp, wo)


def _merge_bwd(dxo, ya, yb, yc, proj, ap, bp, cp, apt, bpt, cpt, wot):
    n = dxo.shape[0]
    tm = _tile(n, 128)
    gb = C_G // D

    def body(d_ref, ya_ref, yb_ref, yc_ref, g0_ref, g1_ref, g2_ref, ap_ref, bp_ref, cp_ref,
             apt_ref, bpt_ref, cpt_ref, wot_ref,
             dp_ref, dya_ref, dyb_ref, dyc_ref, dpa_ref, dpb_ref, dpc_ref, mg_ref):
        dm = _bdot(d_ref[...], wot_ref[...])
        merged = None
        for j, (g_ref, y_ref, w_ref, wt_ref, dy_ref, dpj_ref) in enumerate((
                (g0_ref, ya_ref, ap_ref, apt_ref, dya_ref, dpa_ref),
                (g1_ref, yb_ref, bp_ref, bpt_ref, dyb_ref, dpb_ref),
                (g2_ref, yc_ref, cp_ref, cpt_ref, dyc_ref, dpc_ref))):
            s = jax.nn.sigmoid(g_ref[...])
            pj = _bdot(y_ref[...], w_ref[...])
            merged = s * pj if merged is None else merged + s * pj
            dp_ref[:, D * j:D * (j + 1)] = dm * pj * s * (1.0 - s)
            dpj = (dm * s).astype(BF16)
            dpj_ref[...] = dpj
            dy_ref[...] = jnp.dot(dpj, wt_ref[...], preferred_element_type=F32)
        mg_ref[...] = merged

    def rows(w):
        return pl.BlockSpec((tm, w), lambda i: (i, 0))

    return pl.pallas_call(
        body, name="merge_bwd", grid=(n // tm,),
        in_specs=[rows(D), rows(512), rows(D), rows(512),
                  pl.BlockSpec((tm, D), lambda i: (i, gb)), pl.BlockSpec((tm, D), lambda i: (i, gb + 1)),
                  pl.BlockSpec((tm, D), lambda i: (i, gb + 2)),
                  _full((512, D)), _full((D, D)), _full((512, D)),
                  _full((D, 512)), _full((D, D)), _full((D, 512)), _full((D, D))],
        out_specs=[pl.BlockSpec((tm, 3 * D), lambda i: (i, C_G // (3 * D))), rows(512), rows(D), rows(512),
                   rows(D), rows(D), rows(D), rows(D)],
        out_shape=[SDS((n, NP), F32), SDS((n, 512), F32), SDS((n, D), F32), SDS((n, 512), F32),
                   SDS((n, D), BF16), SDS((n, D), BF16), SDS((n, D), BF16), SDS((n, D), F32)],
        compiler_params=_cp(56),
    )(dxo, ya, yb, yc, proj, proj, proj, ap, bp, cp, apt, bpt, cpt, wot)


def _reduce_adamw(parts, w, m, v, name):
    r, c = w.shape
    tr = _tile(r, 128)
    c1 = 1.0 - ADAM_B1 ** ADAM_STEP
    c2 = 1.0 - ADAM_B2 ** ADAM_STEP

    def body(p_ref, w_ref, m_ref, v_ref, g_ref, d_ref, nm_ref, nv_ref):
        g = p_ref[0]
        for s in range(1, NDEV):
            g = g + p_ref[s]
        nm = ADAM_B1 * m_ref[...] + (1.0 - ADAM_B1) * g
        nv = ADAM_B2 * v_ref[...] + (1.0 - ADAM_B2) * jnp.square(g)
        g_ref[...] = g
        nm_ref[...] = nm
        nv_ref[...] = nv
        d_ref[...] = -ADAM_LR * ((nm / c1) / (jnp.sqrt(nv / c2) + ADAM_EPS) + ADAM_WD * w_ref[...])

    blk = pl.BlockSpec((tr, c), lambda i: (i, 0))
    return pl.pallas_call(
        body, name=name, grid=(r // tr,),
        in_specs=[pl.BlockSpec((NDEV, tr, c), lambda i: (0, i, 0)), blk, blk, blk],
        out_specs=[blk, blk, blk, blk], out_shape=[SDS((r, c), F32)] * 4, compiler_params=_cp(48),
    )(parts, w, m, v)


def _leaf_rows(shape):
    size = 1
    for s in shape:
        size *= s
    return -(-size // (8 * PACK_W)) * 8, size


def _pack(arrs, mult, dtype, lead=()):
    parts = []
    for a in arrs:
        rows, size = _leaf_rows(a.shape[len(lead):])
        flat = a.reshape(lead + (size,)).astype(dtype)
        flat = jnp.pad(flat, [(0, 0)] * len(lead) + [(0, rows * PACK_W - size)])
        parts.append(flat.reshape(lead + (rows, PACK_W)))
    buf = jnp.concatenate(parts, axis=len(lead))
    rows = buf.shape[len(lead)]
    total = -(-rows // mult) * mult
    return jnp.pad(buf, [(0, 0)] * len(lead) + [(0, total - rows), (0, 0)])


def _unpack(buf, shapes, lead=()):
    out, off = [], 0
    for shp in shapes:
        rows, size = _leaf_rows(shp)
        part = buf[..., off:off + rows, :].reshape(lead + (rows * PACK_W,))
        out.append(part[..., :size].reshape(lead + tuple(shp)))
        off += rows
    return out


def _unshard(name, g):
    if name in ROW_SHARDED:
        g = jnp.moveaxis(g, 0, 1)
        return g.reshape(g.shape[0], g.shape[1] * g.shape[2], g.shape[3])
    g = jnp.moveaxis(g, 0, 2)
    return g.reshape(g.shape[0], g.shape[1], g.shape[2] * g.shape[3])


def _reshard(name, full):
    l, r, c = full.shape
    if name in ROW_SHARDED:
        return jnp.moveaxis(full.reshape(l, NDEV, r // NDEV, c), 1, 0)
    return jnp.moveaxis(full.reshape(l, r, NDEV, c // NDEV), 2, 0)


def _w_in_to_padded(slabs):
    pieces = []
    for lo, hi, _ in sorted(SEGMENTS, key=lambda s: s[2]):
        for d in range(NDEV):
            a, b = max(lo, d * W_SHARD), min(hi, (d + 1) * W_SHARD)
            if a < b:
                pieces.append(slabs[d, :, :, a - d * W_SHARD:b - d * W_SHARD])
    pieces.append(jnp.zeros(slabs.shape[1:3] + (NP - C_BA - 16,), slabs.dtype))
    return jnp.concatenate(pieces, axis=-1)


def _w_in_from_padded(g):
    slabs = []
    for d in range(NDEV):
        pieces = []
        for lo, hi, pstart in SEGMENTS:
            a, b = max(lo, d * W_SHARD), min(hi, (d + 1) * W_SHARD)
            if a < b:
                pieces.append(g[:, :, pstart + a - lo:pstart + b - lo])
        pieces.append(jnp.zeros(g.shape[:2] + (W_SHARD_PAD - W_SHARD,), g.dtype))
        slabs.append(jnp.concatenate(pieces, axis=-1))
    return jnp.stack(slabs)


def _pad_w_in(w):
    return jnp.pad(w, ((0, 0), (0, 0), (0, W_SHARD_PAD - W_SHARD))).reshape(-1, W_SHARD_PAD)


def _lane_row(vec8, offset):
    return jnp.pad(vec8, (offset, LANE - NH - offset))[None]


def kernel(x, norm_g, w_in, a_dw, a_dw_b, a_ln_g, a_ln_b, a_proj, b_conv, b_a_log, b_dt_bias, b_onorm_g, b_proj, c_ln_g, c_ln_b, c_ws, c_bs, c_proj, w_out, final_g, loss_target, m_norm_g, m_w_in, m_a_dw, m_a_dw_b, m_a_ln_g, m_a_ln_b, m_a_proj, m_b_conv, m_b_a_log, m_b_dt_bias, m_b_onorm_g, m_b_proj, m_c_ln_g, m_c_ln_b, m_c_ws, m_c_bs, m_c_proj, m_w_out, m_final_g, v_norm_g, v_w_in, v_a_dw, v_a_dw_b, v_a_ln_g, v_a_ln_b, v_a_proj, v_b_conv, v_b_a_log, v_b_dt_bias, v_b_onorm_g, v_b_proj, v_c_ln_g, v_c_ln_b, v_c_ws, v_c_bs, v_c_proj, v_w_out, v_final_g):
    wts = dict(norm_g=norm_g, w_in=w_in, a_dw=a_dw, a_dw_b=a_dw_b, a_ln_g=a_ln_g, a_ln_b=a_ln_b, a_proj=a_proj,
               b_conv=b_conv, b_a_log=b_a_log, b_dt_bias=b_dt_bias, b_onorm_g=b_onorm_g, b_proj=b_proj,
               c_ln_g=c_ln_g, c_ln_b=c_ln_b, c_ws=c_ws, c_bs=c_bs, c_proj=c_proj, w_out=w_out, final_g=final_g)
    mom = dict(norm_g=m_norm_g, w_in=m_w_in, a_dw=m_a_dw, a_dw_b=m_a_dw_b, a_ln_g=m_a_ln_g, a_ln_b=m_a_ln_b,
               a_proj=m_a_proj, b_conv=m_b_conv, b_a_log=m_b_a_log, b_dt_bias=m_b_dt_bias, b_onorm_g=m_b_onorm_g,
               b_proj=m_b_proj, c_ln_g=m_c_ln_g, c_ln_b=m_c_ln_b, c_ws=m_c_ws, c_bs=m_c_bs, c_proj=m_c_proj,
               w_out=m_w_out, final_g=m_final_g)
    vel = dict(norm_g=v_norm_g, w_in=v_w_in, a_dw=v_a_dw, a_dw_b=v_a_dw_b, a_ln_g=v_a_ln_g, a_ln_b=v_a_ln_b,
               a_proj=v_a_proj, b_conv=v_b_conv, b_a_log=v_b_a_log, b_dt_bias=v_b_dt_bias, b_onorm_g=v_b_onorm_g,
               b_proj=v_b_proj, c_ln_g=v_c_ln_g, c_ln_b=v_c_ln_b, c_ws=v_c_ws, c_bs=v_c_bs, c_proj=v_c_proj,
               w_out=v_w_out, final_g=v_final_g)

    bsz, t, _ = x.shape
    n = bsz * t
    depth = norm_g.shape[0]
    x2d = x.reshape(n, D)
    tgt = loss_target.reshape(n, D)

    w_in_all, big_all = _all_gather([_pad_w_in(w_in.astype(BF16)), _pack([wts[k] for k in BIG_REST], 16, BF16)],
                                    "gather_matmul_weights")
    (small_all,) = _all_gather([_pack([wts[k] for k in SMALL], 8, F32)], "gather_conv_weights")
    full = {}
    for k, g in zip(BIG_REST, _unpack(big_all, [wts[k].shape for k in BIG_REST], lead=(NDEV,))):
        full[k] = _unshard(k, g)
    for k, g in zip(SMALL, _unpack(small_all, [wts[k].shape for k in SMALL], lead=(NDEV,))):
        full[k] = _unshard(k, g)
    wp = _w_in_to_padded(w_in_all.reshape(NDEV, depth, D, W_SHARD_PAD))
    wpt = jnp.swapaxes(wp, 1, 2)
    a_dw32 = jnp.pad(full['a_dw'], ((0, 0), (0, 32 - A_K), (0, 0)))
    b_conv8 = jnp.pad(full['b_conv'], ((0, 0), (0, 8 - B_K), (0, 0)))
    bsb = jnp.broadcast_to(c_bs[..., None], c_bs.shape + (SG,))

    saved = []
    xl = x2d
    for l in range(depth):
        alog, dtb = _lane_row(b_a_log[l], NH), _lane_row(b_dt_bias[l], NH)
        proj, h = _inproj(xl, norm_g[l][None], wp[l])
        ya = _a_fwd(proj, a_dw32[l], a_dw_b[l][None], a_ln_g[l][None], a_ln_b[l][None], bsz, t)
        qkvn = _bprep_fwd(proj, b_conv8[l], bsz, t)
        yb, shist = _delta_fwd(qkvn, proj, alog, dtb, b_onorm_g[l][None], bsz, t)
        yc = _c_fwd(proj, c_ln_g[l][None], c_ln_b[l][None], c_ws[l], bsb[l], n)
        xn = _merge_fwd(xl, ya, yb, yc, proj, full['a_proj'][l], full['b_proj'][l], full['c_proj'][l], full['w_out'][l])
        saved.append((xl, proj, h, ya, yb, yc, qkvn, shist, alog, dtb))
        xl = xn

    dx, g_final, loss_blk = _loss_head(xl, final_g[None], tgt)
    loss = lax.psum(loss_blk[0, 0], ("x", "y", "c"))

    gfull = {k: [None] * depth for k in WEIGHTS if k != 'final_g'}
    for l in reversed(range(depth)):
        xl, proj, h, ya, yb, yc, qkvn, shist, alog, dtb = saved[l]
        ap, bp, cp, wo = full['a_proj'][l], full['b_proj'][l], full['c_proj'][l], full['w_out'][l]
        dproj, dya, dyb, dyc, dpa, dpb, dpc, merged = _merge_bwd(dx, ya, yb, yc, proj, ap, bp, cp, ap.T, bp.T, cp.T, wo.T)
        gfull['a_proj'][l] = _mm_nn(ya.T.astype(BF16), dpa, "grad_a_proj")
        gfull['b_proj'][l] = _mm_nn(yb.T.astype(BF16), dpb, "grad_b_proj")
        gfull['c_proj'][l] = _mm_nn(yc.T.astype(BF16), dpc, "grad_c_proj")
        gfull['w_out'][l] = _mm_nn(merged.T.astype(BF16), dx, "grad_w_out")
        dproj, g_clg, g_clb, g_cws, g_cbs = _c_bwd(dproj, proj, dyc, c_ln_g[l][None], c_ln_b[l][None], c_ws[l], bsb[l], n)
        dproj, g_adw, g_adb, g_alg, g_alb = _a_bwd(dproj, proj, dya, a_dw32[l], a_dw_b[l][None], a_ln_g[l][None],
                                                   a_ln_b[l][None], bsz, t)
        dproj, dqkvn, g_alog, g_dt, g_og = _delta_bwd(dproj, qkvn, proj, shist, dyb, alog, dtb, b_onorm_g[l][None], bsz, t)
        dproj, g_bconv = _bprep_bwd(dproj, proj, dqkvn, b_conv8[l], bsz, t)
        gfull['w_in'][l] = _mm_nn(h.T, dproj, "grad_w_in")
        dx, g_ng = _inproj_bwd(dproj, wpt[l], xl, norm_g[l][None], dx)
        gfull['norm_g'][l] = g_ng[0]
        gfull['a_dw'][l] = g_adw[:A_K]
        gfull['a_dw_b'][l], gfull['a_ln_g'][l], gfull['a_ln_b'][l] = g_adb[0], g_alg[0], g_alb[0]
        gfull['b_conv'][l] = g_bconv[:B_K]
        gfull['b_a_log'][l], gfull['b_dt_bias'][l] = g_alog[0, NH:2 * NH], g_dt[0, NH:2 * NH]
        gfull['b_onorm_g'][l] = g_og[0]
        gfull['c_ln_g'][l], gfull['c_ln_b'][l] = g_clg[0], g_clb[0]
        gfull['c_ws'][l], gfull['c_bs'][l] = g_cws, g_cbs[:, :, 0]
    grad_x = dx.reshape(bsz, t, D)
    gfull = {k: jnp.stack(vs) for k, vs in gfull.items()}
    gfull['final_g'] = g_final[0]

    send_w = _w_in_from_padded(gfull['w_in']).reshape(NDEV, depth * D, W_SHARD_PAD)
    send_r = _pack([_reshard(k, gfull[k]) for k in SHARDED_REST], 128, F32, lead=(NDEV,))
    recv_w, recv_r = _all_to_all([send_w, send_r], "exchange_weight_grads")
    outs_w = _reduce_adamw(recv_w, _pad_w_in(w_in), _pad_w_in(m_w_in), _pad_w_in(v_w_in), "adamw_w_in")
    outs_s = _reduce_adamw(recv_r, _pack([wts[k] for k in SHARDED_REST], 128, F32),
                           _pack([mom[k] for k in SHARDED_REST], 128, F32),
                           _pack([vel[k] for k in SHARDED_REST], 128, F32), "adamw_sharded")
    (parts_r,) = _all_gather([_pack([gfull[k] for k in REPL], 8, F32)], "gather_replicated_grads")
    outs_r = _reduce_adamw(parts_r, _pack([wts[k] for k in REPL], 8, F32), _pack([mom[k] for k in REPL], 8, F32),
                           _pack([vel[k] for k in REPL], 8, F32), "adamw_replicated")

    res = []
    for o_w, o_s, o_r in zip(outs_w, outs_s, outs_r):
        leaves = dict(zip(SHARDED_REST, _unpack(o_s, [wts[k].shape for k in SHARDED_REST])))
        leaves.update(zip(REPL, _unpack(o_r, [wts[k].shape for k in REPL])))
        leaves['w_in'] = o_w.reshape(depth, D, W_SHARD_PAD)[:, :, :W_SHARD]
        res.append([leaves[k] for k in WEIGHTS])
    grads, deltas, new_m, new_v = res
    return (loss, grad_x, *grads, *deltas, *new_m, *new_v)
```

```python
import jax
import jax.numpy as jnp
from jax import lax
from jax.experimental import pallas as pl
from jax.experimental.pallas import tpu as pltpu

F32 = jnp.float32
BF16 = jnp.bfloat16
SDS = jax.ShapeDtypeStruct
MESH = pl.DeviceIdType.MESH

NDEV = 8
D = 1024
EPS = 1e-6
LANE = 128
PACK_W = 1024

C_Q, C_K, C_V = 0, 1024, 2048
C_G = 3072
C_A = 6144
C_C = 7680
C_BZ = 9216
C_BA = 10240
NP = 10368
N_IN = 10256
SEGMENTS = ((0, 1536, C_A), (1536, 4608, C_Q), (4608, 5632, C_BZ), (5632, 5648, C_BA), (5648, 7184, C_C), (7184, 10256, C_G))
W_SHARD = N_IN // NDEV
W_SHARD_PAD = 1408

A_K, A_H = 31, 32
B_K, B_H = 4, 8
CH = 64
SG = 128
NH = 8

ADAM_LR, ADAM_B1, ADAM_B2, ADAM_EPS, ADAM_WD, ADAM_STEP = 0.001, 0.9, 0.999, 1e-08, 0.01, 10

WEIGHTS = ['norm_g', 'w_in', 'a_dw', 'a_dw_b', 'a_ln_g', 'a_ln_b', 'a_proj', 'b_conv', 'b_a_log', 'b_dt_bias',
           'b_onorm_g', 'b_proj', 'c_ln_g', 'c_ln_b', 'c_ws', 'c_bs', 'c_proj', 'w_out', 'final_g']
SHARDED_REST = ['a_dw', 'a_proj', 'b_conv', 'b_proj', 'c_proj', 'w_out']
REPL = [n for n in WEIGHTS if n != 'w_in' and n not in SHARDED_REST]
BIG_REST = ['a_proj', 'b_proj', 'c_proj', 'w_out']
SMALL = ['a_dw', 'b_conv']
ROW_SHARDED = ('b_proj', 'w_out')

NN = ((1,), (0,))
NT = ((1,), (1,))
TN = ((0,), (0,))


def _tile(n, pref):
    return pref if (n >= pref and n % pref == 0) else n


def _cp(vmem_mb):
    return pltpu.CompilerParams(vmem_limit_bytes=vmem_mb * 2 ** 20)


def _full(shape):
    nd = len(shape)
    return pl.BlockSpec(shape, lambda *_: (0,) * nd)


def _rms(x, g):
    return x * lax.rsqrt(jnp.mean(x * x, axis=-1, keepdims=True) + EPS) * g


def _softplus(x):
    return jnp.maximum(x, 0.0) + jnp.log1p(jnp.exp(-jnp.abs(x)))


def _bdot(a, b):
    return jnp.dot(a.astype(BF16), b.astype(BF16), preferred_element_type=F32)


def _mm(a, b, dims):
    return lax.dot_general(a, b, (dims, ((), ())), preferred_element_type=F32)


def _all_gather(xs, name):
    nops = len(xs)

    def body(*refs):
        x_refs, out_refs = refs[:nops], refs[nops:2 * nops]
        send_sems, recv_sems, local_sems = refs[2 * nops:]
        x, y, cc = lax.axis_index("x"), lax.axis_index("y"), lax.axis_index("c")
        me, sibling = (x, y, cc), (x, y, 1 - cc)
        chips = [(1 - x, y), (x, 1 - y), (1 - x, 1 - y)]

        def slot(t, px, py, pc):
            return out_refs[t].at[4 * px + 2 * py + pc]

        def copy(t, k, block, to, src=None):
            return pltpu.make_async_remote_copy(
                src_ref=slot(t, *block) if src is None else src, dst_ref=slot(t, *block),
                send_sem=send_sems.at[7 * t + k], recv_sem=recv_sems.at[7 * t + k], device_id=to, device_id_type=MESH)

        ops = range(nops)
        mine = [pltpu.make_async_copy(x_refs[t], slot(t, *me), local_sems.at[t]) for t in ops]
        for cp in mine:
            cp.start()
        first = [copy(t, 0, me, sibling, src=x_refs[t]) for t in ops]
        first += [copy(t, 1 + j, me, (*chip, cc), src=x_refs[t]) for j, chip in enumerate(chips) for t in ops]
        for cp in first:
            cp.start()
        passed = []
        for j, chip in enumerate(chips):
            for t in ops:
                copy(t, 1 + j, (*chip, cc), me).wait_recv()
                fwd = copy(t, 4 + j, (*chip, cc), sibling)
                fwd.start()
                passed.append(fwd)
        for t in ops:
            copy(t, 0, sibling, me).wait_recv()
        for j, chip in enumerate(chips):
            for t in ops:
                copy(t, 4 + j, (*chip, 1 - cc), me).wait_recv()
        for cp in first + passed:
            cp.wait_send()
        for cp in mine:
            cp.wait()

    hbm = pl.BlockSpec(memory_space=pltpu.HBM)
    return pl.pallas_call(
        body, name=name, out_shape=[SDS((NDEV,) + a.shape, a.dtype) for a in xs],
        in_specs=[hbm] * nops, out_specs=[hbm] * nops,
        scratch_shapes=[pltpu.SemaphoreType.DMA((7 * nops,)), pltpu.SemaphoreType.DMA((7 * nops,)),
                        pltpu.SemaphoreType.DMA((nops,))],
    )(*xs)


def _all_to_all(sends, name):
    nops = len(sends)

    def body(*refs):
        in_refs, out_refs = refs[:nops], refs[nops:2 * nops]
        send_sems, recv_sems, local_sems = refs[2 * nops:]
        x, y, cc = lax.axis_index("x"), lax.axis_index("y"), lax.axis_index("c")
        me = 4 * x + 2 * y + cc
        mine = [pltpu.make_async_copy(in_refs[t].at[me], out_refs[t].at[me], local_sems.at[t]) for t in range(nops)]
        for cp in mine:
            cp.start()
        out, landed = [], []
        for k in range(1, NDEV):
            px = 1 - x if k & 4 else x
            py = 1 - y if k & 2 else y
            pc = 1 - cc if k & 1 else cc
            peer = 4 * px + 2 * py + pc
            for t in range(nops):
                sem = 7 * t + k - 1
                out.append(pltpu.make_async_remote_copy(
                    src_ref=in_refs[t].at[peer], dst_ref=out_refs[t].at[me], send_sem=send_sems.at[sem],
                    recv_sem=recv_sems.at[sem], device_id=(px, py, pc), device_id_type=MESH))
                landed.append(pltpu.make_async_remote_copy(
                    src_ref=in_refs[t].at[me], dst_ref=out_refs[t].at[peer], send_sem=send_sems.at[sem],
                    recv_sem=recv_sems.at[sem], device_id=(px, py, pc), device_id_type=MESH))
        for cp in out:
            cp.start()
        for cp in landed:
            cp.wait_recv()
        for cp in out:
            cp.wait_send()
        for cp in mine:
            cp.wait()

    hbm = pl.BlockSpec(memory_space=pltpu.HBM)
    return pl.pallas_call(
        body, name=name, out_shape=[SDS(a.shape, a.dtype) for a in sends],
        in_specs=[hbm] * nops, out_specs=[hbm] * nops,
        scratch_shapes=[pltpu.SemaphoreType.DMA((7 * nops,)), pltpu.SemaphoreType.DMA((7 * nops,)),
                        pltpu.SemaphoreType.DMA((nops,))],
    )(*sends)


def _inproj(x2d, g_row, wp):
    n = x2d.shape[0]
    tm, tn = _tile(n, 1024), 1152

    def body(x_ref, g_ref, w_ref, proj_ref, h_ref, hs):
        @pl.when(pl.program_id(1) == 0)
        def _():
            h = _rms(x_ref[...], g_ref[...]).astype(BF16)
            hs[...] = h
            h_ref[...] = h

        proj_ref[...] = jnp.dot(hs[...], w_ref[...], preferred_element_type=F32)

    return pl.pallas_call(
        body, name="inproj", grid=(n // tm, NP // tn),
        in_specs=[pl.BlockSpec((tm, D), lambda i, j: (i, 0)), _full((1, D)), pl.BlockSpec((D, tn), lambda i, j: (0, j))],
        out_specs=[pl.BlockSpec((tm, tn), lambda i, j: (i, j)), pl.BlockSpec((tm, D), lambda i, j: (i, 0))],
        out_shape=[SDS((n, NP), F32), SDS((n, D), BF16)],
        scratch_shapes=[pltpu.VMEM((tm, D), BF16)], compiler_params=_cp(48),
    )(x2d, g_row, wp)


def _mm_nn(at, b, name):
    m, nn = at.shape
    k = b.shape[1]
    tk = 1152 if k % 1152 == 0 else _tile(k, 1024)
    tn = _tile(nn, 1024)

    def body(a_ref, b_ref, o_ref):
        p = jnp.dot(a_ref[...], b_ref[...].astype(BF16), preferred_element_type=F32)

        @pl.when(pl.program_id(1) == 0)
        def _():
            o_ref[...] = p

        @pl.when(pl.program_id(1) > 0)
        def _():
            o_ref[...] += p

    return pl.pallas_call(
        body, name=name, grid=(k // tk, nn // tn),
        in_specs=[pl.BlockSpec((m, tn), lambda j, t: (0, t)), pl.BlockSpec((tn, tk), lambda j, t: (t, j))],
        out_specs=pl.BlockSpec((m, tk), lambda j, t: (0, j)),
        out_shape=SDS((m, k), F32), compiler_params=_cp(48),
    )(at, b)


def _inproj_bwd(dproj, wpt, x2d, g_row, dxo):
    n = x2d.shape[0]
    tm, tk = _tile(n, 1024), 1152
    nk = NP // tk

    def body(dp_ref, w_ref, x_ref, g_ref, dxo_ref, dx_ref, dg_ref, acc):
        i, k = pl.program_id(0), pl.program_id(1)
        p = jnp.dot(dp_ref[...], w_ref[...], preferred_element_type=F32)

        @pl.when(k == 0)
        def _():
            acc[...] = p

        @pl.when(k > 0)
        def _():
            acc[...] += p

        @pl.when(k == nk - 1)
        def _():
            _, vjp = jax.vjp(_rms, x_ref[...], g_ref[...])
            dx, dg = vjp(acc[...])
            dx_ref[...] = dxo_ref[...] + dx

            @pl.when(i == 0)
            def _():
                dg_ref[...] = dg

            @pl.when(i > 0)
            def _():
                dg_ref[...] += dg

    return pl.pallas_call(
        body, name="inproj_bwd", grid=(n // tm, nk),
        in_specs=[pl.BlockSpec((tm, tk), lambda i, k: (i, k)), pl.BlockSpec((tk, D), lambda i, k: (k, 0)),
                  pl.BlockSpec((tm, D), lambda i, k: (i, 0)), _full((1, D)), pl.BlockSpec((tm, D), lambda i, k: (i, 0))],
        out_specs=[pl.BlockSpec((tm, D), lambda i, k: (i, 0)), _full((1, D))],
        out_shape=[SDS((n, D), F32), SDS((1, D), F32)],
        scratch_shapes=[pltpu.VMEM((tm, D), F32)], compiler_params=_cp(56),
    )(dproj, wpt, x2d, g_row, dxo)


def _loss_head(x2d, g_row, tgt):
    n = x2d.shape[0]
    tm = _tile(n, 512)

    def body(x_ref, g_ref, t_ref, dx_ref, dg_ref, loss_ref):
        i = pl.program_id(0)
        y, vjp = jax.vjp(_rms, x_ref[...], g_ref[...])
        err = y - t_ref[...]
        part = 0.5 * jnp.sum(jnp.mean(err * err, axis=-1, keepdims=True), axis=0, keepdims=True)
        dx, dg = vjp(err * (1.0 / D))
        dx_ref[...] = dx
        lb = jnp.broadcast_to(part, (8, LANE))

        @pl.when(i == 0)
        def _():
            dg_ref[...] = dg
            loss_ref[...] = lb

        @pl.when(i > 0)
        def _():
            dg_ref[...] += dg
            loss_ref[...] += lb

    return pl.pallas_call(
        body, name="loss_head", grid=(n // tm,),
        in_specs=[pl.BlockSpec((tm, D), lambda i: (i, 0)), _full((1, D)), pl.BlockSpec((tm, D), lambda i: (i, 0))],
        out_specs=[pl.BlockSpec((tm, D), lambda i: (i, 0)), _full((1, D)), _full((8, LANE))],
        out_shape=[SDS((n, D), F32), SDS((1, D), F32), SDS((8, LANE), F32)], compiler_params=_cp(40),
    )(x2d, g_row, tgt)


def _conv_fwd(ext_ref, w_ref, kw, halo, tt):
    acc = None
    for j in range(kw):
        term = w_ref[j:j + 1, :] * ext_ref[pl.ds(halo - (kw - 1) + j, tt), :]
        acc = term if acc is None else acc + term
    return acc


def _conv_bwd_x(dcp_ref, w_ref, kw, halo, tt):
    acc = None
    for j in range(kw):
        term = w_ref[j:j + 1, :] * dcp_ref[pl.ds(kw - 1 - j, tt + halo), :]
        acc = term if acc is None else acc + term
    return acc


def _conv_bwd_w(dc, ext_ref, gw_ref, kw, halo, tt):
    for j in range(kw):
        gw_ref[j:j + 1, :] += jnp.sum(dc * ext_ref[pl.ds(halo - (kw - 1) + j, tt), :], axis=0, keepdims=True)


SUB = 8


def _fill_shifts(sh_ref, rows):
    for s in range(1, SUB):
        sh_ref[s, 0:rows, :] = sh_ref[0, pl.ds(s, rows), :]


def _tap(sh_ref, off, rows):
    return sh_ref[off % SUB, pl.ds(off - off % SUB, rows), :]


def _conv_fwd_sh(ext_sh, w_ref, kw, halo, tt):
    acc = None
    for j in range(kw):
        term = w_ref[j:j + 1, :] * _tap(ext_sh, halo - (kw - 1) + j, tt)
        acc = term if acc is None else acc + term
    return acc


def _conv_bwd_x_sh(dcp_sh, w_ref, kw, halo, tt):
    acc = None
    for j in range(kw):
        term = w_ref[j:j + 1, :] * _tap(dcp_sh, kw - 1 - j, tt + halo)
        acc = term if acc is None else acc + term
    return acc


def _conv_bwd_w_sh(dc, ext_sh, gw_ref, kw, halo, tt):
    for j in range(kw):
        gw_ref[j:j + 1, :] += jnp.sum(dc * _tap(ext_sh, halo - (kw - 1) + j, tt), axis=0, keepdims=True)


def _a_post(c, z, g, b):
    mu = jnp.mean(c, axis=-1, keepdims=True)
    var = jnp.mean(jnp.square(c - mu), axis=-1, keepdims=True)
    a = (c - mu) * lax.rsqrt(var + EPS) * g + b
    return jax.nn.silu(a) * jax.nn.silu(z)


def _a_fwd(proj, dw, b_row, lg, lb, bsz, t):
    n = bsz * t
    tt = _tile(t, 256)
    nt = t // tt
    cb = C_A // 512

    def body(v_ref, g_ref, z_ref, vh_ref, gh_ref, w_ref, b_ref, lg_ref, lb_ref, y_ref, ext):
        i = pl.program_id(1)
        ext[0, 0:A_H, :] = jnp.where(i > 0, vh_ref[...] * jax.nn.sigmoid(gh_ref[...]), 0.0)
        ext[0, A_H:, :] = v_ref[...] * jax.nn.sigmoid(g_ref[...])
        _fill_shifts(ext, tt + A_H - SUB)
        c = _conv_fwd_sh(ext, w_ref, A_K, A_H, tt) + b_ref[...]
        y_ref[...] = _a_post(c, z_ref[...], lg_ref[...], lb_ref[...])

    def row(b, i):
        return b * nt + i

    def halo(b, i):
        return jnp.maximum((b * t + i * tt) // A_H - 1, 0)

    return pl.pallas_call(
        body, name="a_fwd", grid=(bsz, nt),
        in_specs=[pl.BlockSpec((tt, 512), lambda b, i: (row(b, i), cb)),
                  pl.BlockSpec((tt, 512), lambda b, i: (row(b, i), cb + 1)),
                  pl.BlockSpec((tt, 512), lambda b, i: (row(b, i), cb + 2)),
                  pl.BlockSpec((A_H, 512), lambda b, i: (halo(b, i), cb)),
                  pl.BlockSpec((A_H, 512), lambda b, i: (halo(b, i), cb + 1)),
                  _full((32, 512)), _full((1, 512)), _full((1, 512)), _full((1, 512))],
        out_specs=pl.BlockSpec((tt, 512), lambda b, i: (row(b, i), 0)),
        out_shape=SDS((n, 512), F32),
        scratch_shapes=[pltpu.VMEM((SUB, tt + A_H, 512), F32)], compiler_params=_cp(40),
    )(proj, proj, proj, proj, proj, dw, b_row, lg, lb)


def _a_bwd(dproj, proj, dy, dw, b_row, lg, lb, bsz, t):
    n = bsz * t
    tt = _tile(t, 256)
    nt = t // tt
    cb = C_A // 512

    def body(dp_any, v_ref, g_ref, z_ref, vh_ref, gh_ref, dy_ref, w_ref, b_ref, lg_ref, lb_ref,
             dp_ref, gw_ref, gb_ref, glg_ref, glb_ref, ext, dcp, dae, carry):
        b, i = pl.program_id(0), pl.program_id(1)
        ti = nt - 1 - i

        @pl.when((b == 0) & (i == 0))
        def _():
            gw_ref[...] = jnp.zeros_like(gw_ref)
            gb_ref[...] = jnp.zeros_like(gb_ref)
            glg_ref[...] = jnp.zeros_like(glg_ref)
            glb_ref[...] = jnp.zeros_like(glb_ref)

        @pl.when(i == 0)
        def _():
            carry[...] = jnp.zeros_like(carry)

        val, glu = v_ref[...], g_ref[...]
        sg = jax.nn.sigmoid(glu)
        ext[0, 0:A_H, :] = jnp.where(ti > 0, vh_ref[...] * jax.nn.sigmoid(gh_ref[...]), 0.0)
        ext[0, A_H:, :] = val * sg
        _fill_shifts(ext, tt + A_H - SUB)
        c = _conv_fwd_sh(ext, w_ref, A_K, A_H, tt) + b_ref[...]
        _, vjp = jax.vjp(_a_post, c, z_ref[...], lg_ref[...], lb_ref[...])
        dc, dz, dlg, dlb = vjp(dy_ref[...])
        gb_ref[...] += jnp.sum(dc, axis=0, keepdims=True)
        glg_ref[...] += dlg
        glb_ref[...] += dlb
        dcp[0, 0:A_H, :] = jnp.zeros((A_H, 512), F32)
        dcp[0, A_H:A_H + tt, :] = dc
        dcp[0, A_H + tt:, :] = jnp.zeros((A_H, 512), F32)
        _fill_shifts(dcp, tt + 2 * A_H - SUB)
        _conv_bwd_w_sh(dc, ext, gw_ref, A_K, A_H, tt)
        dae[...] = _conv_bwd_x_sh(dcp, w_ref, A_K, A_H, tt)
        dae[tt:tt + A_H, :] += carry[...]
        carry[...] = dae[0:A_H, :]
        da = dae[A_H:, :]
        dp_ref[:, 0:512] = (da * sg).astype(BF16)
        dp_ref[:, 512:1024] = (da * val * sg * (1.0 - sg)).astype(BF16)
        dp_ref[:, 1024:1536] = dz.astype(BF16)

    def row(b, i):
        return b * nt + (nt - 1 - i)

    def halo(b, i):
        return jnp.maximum((b * t + (nt - 1 - i) * tt) // A_H - 1, 0)

    outs = pl.pallas_call(
        body, name="a_bwd", grid=(bsz, nt),
        in_specs=[pl.BlockSpec(memory_space=pl.ANY),
                  pl.BlockSpec((tt, 512), lambda b, i: (row(b, i), cb)),
                  pl.BlockSpec((tt, 512), lambda b, i: (row(b, i), cb + 1)),
                  pl.BlockSpec((tt, 512), lambda b, i: (row(b, i), cb + 2)),
                  pl.BlockSpec((A_H, 512), lambda b, i: (halo(b, i), cb)),
                  pl.BlockSpec((A_H, 512), lambda b, i: (halo(b, i), cb + 1)),
                  pl.BlockSpec((tt, 512), lambda b, i: (row(b, i), 0)),
                  _full((32, 512)), _full((1, 512)), _full((1, 512)), _full((1, 512))],
        out_specs=[pl.BlockSpec((tt, 1536), lambda b, i: (row(b, i), C_A // 1536)),
                   _full((32, 512)), _full((1, 512)), _full((1, 512)), _full((1, 512))],
        out_shape=[SDS((n, NP), BF16), SDS((32, 512), F32), SDS((1, 512), F32), SDS((1, 512), F32), SDS((1, 512), F32)],
        input_output_aliases={0: 0},
        scratch_shapes=[pltpu.VMEM((SUB, tt + A_H, 512), F32), pltpu.VMEM((SUB, tt + 2 * A_H, 512), F32),
                        pltpu.VMEM((tt + A_H, 512), F32), pltpu.VMEM((A_H, 512), F32)],
        compiler_params=_cp(48),
    )(dproj, proj, proj, proj, proj, proj, dy, dw, b_row, lg, lb)
    return outs


def _b_post(blocks):
    out = []
    for idx, c in enumerate(blocks):
        s = jax.nn.silu(c)
        if idx < 2 * NH:
            s = s * lax.rsqrt(jnp.sum(s * s, axis=-1, keepdims=True) + EPS)
            if idx < NH:
                s = s * (LANE ** -0.5)
        out.append(s)
    return out


def _bprep_fwd(proj, wconv, bsz, t):
    n = bsz * t
    tt = _tile(t, 256)
    nt = t // tt

    def body(x_ref, xh_ref, w_ref, o_ref, ext):
        i = pl.program_id(1)
        ext[0:B_H, :] = jnp.where(i > 0, xh_ref[...], 0.0)
        ext[B_H:, :] = x_ref[...]
        c = _conv_fwd(ext, w_ref, B_K, B_H, tt)
        outs = _b_post([c[:, LANE * j:LANE * (j + 1)] for j in range(3 * NH)])
        for j, o in enumerate(outs):
            o_ref[:, LANE * j:LANE * (j + 1)] = o

    return pl.pallas_call(
        body, name="bprep_fwd", grid=(bsz, nt),
        in_specs=[pl.BlockSpec((tt, 3072), lambda b, i: (b * nt + i, 0)),
                  pl.BlockSpec((B_H, 3072), lambda b, i: (jnp.maximum((b * t + i * tt) // B_H - 1, 0), 0)),
                  _full((8, 3072))],
        out_specs=pl.BlockSpec((tt, 3072), lambda b, i: (b * nt + i, 0)),
        out_shape=SDS((n, 3072), F32),
        scratch_shapes=[pltpu.VMEM((tt + B_H, 3072), F32)], compiler_params=_cp(48),
    )(proj, proj, wconv)


def _bprep_bwd(dproj, proj, dqkvn, wconv, bsz, t):
    n = bsz * t
    tt = _tile(t, 256)
    nt = t // tt

    def body(dp_any, x_ref, xh_ref, dq_ref, w_ref, dp_ref, gw_ref, ext, dcp, dae, carry):
        b, i = pl.program_id(0), pl.program_id(1)
        ti = nt - 1 - i

        @pl.when((b == 0) & (i == 0))
        def _():
            gw_ref[...] = jnp.zeros_like(gw_ref)

        @pl.when(i == 0)
        def _():
            carry[...] = jnp.zeros_like(carry)

        ext[0:B_H, :] = jnp.where(ti > 0, xh_ref[...], 0.0)
        ext[B_H:, :] = x_ref[...]
        c = _conv_fwd(ext, w_ref, B_K, B_H, tt)
        _, vjp = jax.vjp(_b_post, [c[:, LANE * j:LANE * (j + 1)] for j in range(3 * NH)])
        (dcs,) = vjp([dq_ref[:, LANE * j:LANE * (j + 1)] for j in range(3 * NH)])
        dcp[0:B_H, :] = jnp.zeros((B_H, 3072), F32)
        for j, dcj in enumerate(dcs):
            dcp[B_H:B_H + tt, LANE * j:LANE * (j + 1)] = dcj
        dcp[B_H + tt:, :] = jnp.zeros((B_H, 3072), F32)
        _conv_bwd_w(dcp[B_H:B_H + tt, :], ext, gw_ref, B_K, B_H, tt)
        dae[...] = _conv_bwd_x(dcp, w_ref, B_K, B_H, tt)
        dae[tt:tt + B_H, :] += carry[...]
        carry[...] = dae[0:B_H, :]
        dp_ref[...] = dae[B_H:, :].astype(BF16)

    def row(b, i):
        return b * nt + (nt - 1 - i)

    return pl.pallas_call(
        body, name="bprep_bwd", grid=(bsz, nt),
        in_specs=[pl.BlockSpec(memory_space=pl.ANY),
                  pl.BlockSpec((tt, 3072), lambda b, i: (row(b, i), 0)),
                  pl.BlockSpec((B_H, 3072), lambda b, i: (jnp.maximum((b * t + (nt - 1 - i) * tt) // B_H - 1, 0), 0)),
                  pl.BlockSpec((tt, 3072), lambda b, i: (row(b, i), 0)),
                  _full((8, 3072))],
        out_specs=[pl.BlockSpec((tt, 3072), lambda b, i: (row(b, i), 0)), _full((8, 3072))],
        out_shape=[SDS((n, NP), BF16), SDS((8, 3072), F32)],
        input_output_aliases={0: 0},
        scratch_shapes=[pltpu.VMEM((tt + B_H, 3072), F32), pltpu.VMEM((tt + 2 * B_H, 3072), F32),
                        pltpu.VMEM((tt + B_H, 3072), F32), pltpu.VMEM((B_H, 3072), F32)],
        compiler_params=_cp(56),
    )(dproj, proj, proj, dqkvn, wconv)


def _split2(a):
    hi = a.astype(BF16)
    return hi, (a - hi.astype(F32)).astype(BF16)


def _split3(a):
    p1 = a.astype(BF16)
    r1 = a - p1.astype(F32)
    p2 = r1.astype(BF16)
    return p1, p2, (r1 - p2.astype(F32)).astype(BF16)


def _dot3_raw(a, b, dims):
    a1, a2 = _split2(a)
    b1, b2 = _split2(b)
    return _mm(a1, b1, dims) + (_mm(a1, b2, dims) + _mm(a2, b1, dims))


def _dot6(a, b, dims):
    a1, a2, a3 = _split3(a)
    b1, b2, b3 = _split3(b)
    return (_mm(a1, b1, dims) + (_mm(a1, b2, dims) + _mm(a2, b1, dims))
            + (_mm(a1, b3, dims) + _mm(a2, b2, dims) + _mm(a3, b1, dims)))


def _unit_lower_inverse_raw(lmats):
    r = lax.broadcasted_iota(jnp.int32, (CH, CH), 0)
    c = lax.broadcasted_iota(jnp.int32, (CH, CH), 1)
    eye = (r == c).astype(F32)
    blk = jnp.right_shift(r, 4) == jnp.right_shift(c, 4)
    dm = [jnp.where(blk, x, 0.0) for x in lmats]
    om = [a - b for a, b in zip(lmats, dm)]
    d2 = [_dot3_raw(x, x, NN) for x in dm]
    d4 = [_dot3_raw(x, x, NN) for x in d2]
    d8 = [_dot3_raw(x, x, NN) for x in d4]
    p = [_dot3_raw(eye - a, eye + b, NN) for a, b in zip(dm, d2)]
    p = [_dot3_raw(a, eye + b, NN) for a, b in zip(p, d4)]
    p = [_dot3_raw(a, eye + b, NN) for a, b in zip(p, d8)]
    m = [_dot3_raw(a, b, NN) for a, b in zip(p, om)]
    m2 = [_dot3_raw(x, x, NN) for x in m]
    t = [_dot3_raw(eye - a, eye + b, NN) for a, b in zip(m, m2)]
    return [_dot3_raw(a, b, NN) for a, b in zip(t, p)]


@jax.custom_vjp
def _unit_lower_inverse(lmats):
    return _unit_lower_inverse_raw(lmats)


def _unit_lower_inverse_fwd(lmats):
    tinv = _unit_lower_inverse_raw(lmats)
    return tinv, tinv


def _unit_lower_inverse_bwd(tinv, gs):
    x = [_dot6(t, g, TN) for t, g in zip(tinv, gs)]
    return ([-_dot6(a, t, NT) for a, t in zip(x, tinv)],)


_unit_lower_inverse.defvjp(_unit_lower_inverse_fwd, _unit_lower_inverse_bwd)


def _tri(lower):
    r = lax.broadcasted_iota(jnp.int32, (CH, CH), 0)
    c = lax.broadcasted_iota(jnp.int32, (CH, CH), 1)
    return ((r >= c) if lower else (r <= c)).astype(BF16)


def _tri_dot(x, lower, dims, tri_first):
    p1, p2, p3 = _split3(x)
    tri = _tri(lower)
    if tri_first:
        return _mm(tri, p1, dims) + (_mm(tri, p2, dims) + _mm(tri, p3, dims))
    return _mm(p1, tri, dims) + (_mm(p2, tri, dims) + _mm(p3, tri, dims))


@jax.custom_vjp
def _cumsum_rows(x):
    return _tri_dot(x, True, NN, True)


def _cumsum_rows_fwd(x):
    return _tri_dot(x, True, NN, True), None


def _cumsum_rows_bwd(_, g):
    return (_tri_dot(g, True, TN, True),)


_cumsum_rows.defvjp(_cumsum_rows_fwd, _cumsum_rows_bwd)


@jax.custom_vjp
def _cumsum_rows_t(x):
    return _tri_dot(x, False, TN, False)


def _cumsum_rows_t_fwd(x):
    return _tri_dot(x, False, TN, False), None


def _cumsum_rows_t_bwd(_, g):
    return (_tri_dot(g, False, NT, True),)


_cumsum_rows_t.defvjp(_cumsum_rows_t_fwd, _cumsum_rows_t_bwd)


def _delta_chunk(ss, qs, ks, vs, ba, zs, alog, dtb, og):
    heads = range(NH)
    lane = lax.broadcasted_iota(jnp.int32, (1, LANE), 1)
    r = lax.broadcasted_iota(jnp.int32, (CH, CH), 0)
    c = lax.broadcasted_iota(jnp.int32, (CH, CH), 1)
    ri = lax.broadcasted_iota(jnp.int32, (CH, 1), 0)
    incl, strict = r >= c, r > c

    def pick(x, h):
        return jnp.sum(jnp.where(lane == h, x, 0.0), axis=-1, keepdims=True)

    beta = [jax.nn.sigmoid(pick(ba, h)) for h in heads]
    g = [-jnp.exp(pick(alog, h + NH)) * _softplus(pick(ba, h + NH) + pick(dtb, h + NH)) for h in heads]
    gb = [jnp.broadcast_to(x, (CH, CH)) for x in g]
    gca = [_cumsum_rows(x) for x in gb]
    gcr = [_cumsum_rows_t(x) for x in gb]
    gc = [jnp.sum(jnp.where(c == 0, x, 0.0), axis=-1, keepdims=True) for x in gca]
    gl = [jnp.sum(jnp.where(ri == CH - 1, x, 0.0), axis=0, keepdims=True) for x in gc]
    diff = [a - b for a, b in zip(gca, gcr)]
    gam_s = [jnp.where(strict, jnp.exp(jnp.where(strict, x, 0.0)), 0.0) for x in diff]
    gam_i = [jnp.where(incl, jnp.exp(jnp.where(incl, x, 0.0)), 0.0) for x in diff]

    kk = [_mm(k, k, NT) for k in ks]
    tinv = _unit_lower_inverse([beta[h] * kk[h] * gam_s[h] for h in heads])

    eg = [jnp.exp(x) for x in gc]
    u = [_mm(tinv[h], vs[h] * beta[h], NN) for h in heads]
    w = [_mm(tinv[h], ks[h] * (beta[h] * eg[h]), NN) for h in heads]
    qk = [_mm(qs[h], ks[h], NT) * gam_i[h] for h in heads]
    vn = [u[h] - _mm(w[h], ss[h], NN) for h in heads]
    o = [_mm(qs[h] * eg[h], ss[h], NN) + _mm(qk[h], vn[h], NN) for h in heads]
    sn = [jnp.exp(gl[h]) * ss[h] + _mm(ks[h] * jnp.exp(gl[h] - gc[h]), vn[h], TN) for h in heads]
    y = [_rms(o[h], og) * jax.nn.silu(zs[h]) for h in heads]
    return sn, y


def _head_blocks(ref, base=0):
    return [ref[:, base + LANE * h:base + LANE * (h + 1)] for h in range(NH)]


def _delta_fwd(qkvn, proj, alog, dtb, og, bsz, t):
    n = bsz * t
    nc = t // CH

    def body(q_ref, k_ref, v_ref, ba_ref, z_ref, al_ref, dt_ref, og_ref, y_ref, sh_ref, s_scr):
        @pl.when(pl.program_id(1) == 0)
        def _():
            s_scr[...] = jnp.zeros_like(s_scr)

        ss = [s_scr[h] for h in range(NH)]
        for h in range(NH):
            sh_ref[h] = ss[h]
        sn, y = _delta_chunk(ss, _head_blocks(q_ref), _head_blocks(k_ref), _head_blocks(v_ref), ba_ref[...],
                             _head_blocks(z_ref), al_ref[...], dt_ref[...], og_ref[...])
        for h in range(NH):
            s_scr[h] = sn[h]
            y_ref[:, LANE * h:LANE * (h + 1)] = y[h]

    def blk(width, col):
        return pl.BlockSpec((CH, width), lambda b, ci: (b * nc + ci, col))

    return pl.pallas_call(
        body, name="delta_fwd", grid=(bsz, nc),
        in_specs=[blk(D, 0), blk(D, 1), blk(D, 2), blk(LANE, C_BA // LANE), blk(D, C_BZ // D),
                  _full((1, LANE)), _full((1, LANE)), _full((1, LANE))],
        out_specs=[blk(D, 0), pl.BlockSpec((None, None, NH, LANE, LANE), lambda b, ci: (b, ci, 0, 0, 0))],
        out_shape=[SDS((n, D), F32), SDS((bsz, nc, NH, LANE, LANE), F32)],
        scratch_shapes=[pltpu.VMEM((NH, LANE, LANE), F32)], compiler_params=_cp(40),
    )(qkvn, qkvn, qkvn, proj, proj, alog, dtb, og)


def _delta_bwd(dproj, qkvn, proj, shist, dyb, alog, dtb, og, bsz, t):
    n = bsz * t
    nc = t // CH

    def body(dp_any, q_ref, k_ref, v_ref, ba_ref, z_ref, sh_ref, dy_ref, al_ref, dt_ref, og_ref,
             dp_ref, dqkv_ref, gal_ref, gdt_ref, gog_ref, ds_scr):
        b, ci = pl.program_id(0), pl.program_id(1)

        @pl.when((b == 0) & (ci == 0))
        def _():
            gal_ref[...] = jnp.zeros_like(gal_ref)
            gdt_ref[...] = jnp.zeros_like(gdt_ref)
            gog_ref[...] = jnp.zeros_like(gog_ref)

        @pl.when(ci == 0)
        def _():
            ds_scr[...] = jnp.zeros_like(ds_scr)

        _, vjp = jax.vjp(_delta_chunk, [sh_ref[h] for h in range(NH)], _head_blocks(q_ref), _head_blocks(k_ref),
                         _head_blocks(v_ref), ba_ref[...], _head_blocks(z_ref), al_ref[...], dt_ref[...], og_ref[...])
        ds, dq, dk, dv, dba, dz, dal, ddt, dog = vjp(([ds_scr[h] for h in range(NH)], _head_blocks(dy_ref)))
        gal_ref[...] += dal
        gdt_ref[...] += ddt
        gog_ref[...] += dog
        dp_ref[:, D:D + LANE] = dba.astype(BF16)
        for h in range(NH):
            ds_scr[h] = ds[h]
            dp_ref[:, LANE * h:LANE * (h + 1)] = dz[h].astype(BF16)
            dqkv_ref[:, LANE * h:LANE * (h + 1)] = dq[h]
            dqkv_ref[:, D + LANE * h:D + LANE * (h + 1)] = dk[h]
            dqkv_ref[:, 2 * D + LANE * h:2 * D + LANE * (h + 1)] = dv[h]

    def blk(width, col):
        return pl.BlockSpec((CH, width), lambda b, ci: (b * nc + (nc - 1 - ci), col))

    return pl.pallas_call(
        body, name="delta_bwd", grid=(bsz, nc),
        in_specs=[pl.BlockSpec(memory_space=pl.ANY), blk(D, 0), blk(D, 1), blk(D, 2), blk(LANE, C_BA // LANE),
                  blk(D, C_BZ // D),
                  pl.BlockSpec((None, None, NH, LANE, LANE), lambda b, ci: (b, nc - 1 - ci, 0, 0, 0)),
                  blk(D, 0), _full((1, LANE)), _full((1, LANE)), _full((1, LANE))],
        out_specs=[blk(D + LANE, C_BZ // (D + LANE)), blk(3 * D, 0), _full((1, LANE)), _full((1, LANE)), _full((1, LANE))],
        out_shape=[SDS((n, NP), BF16), SDS((n, 3 * D), F32), SDS((1, LANE), F32), SDS((1, LANE), F32), SDS((1, LANE), F32)],
        input_output_aliases={0: 0},
        scratch_shapes=[pltpu.VMEM((NH, LANE, LANE), F32)], compiler_params=_cp(48),
    )(dproj, qkvn, qkvn, qkvn, proj, proj, shist, dyb, alog, dtb, og)


def _c_chunk(us, vs, zs, lgs, lbs, ws, bsb):
    gv = [jax.nn.gelu(v) for v in vs]
    width = LANE * len(gv)
    mu = sum(jnp.sum(x, axis=-1, keepdims=True) for x in gv) / width
    var = sum(jnp.sum(jnp.square(x - mu), axis=-1, keepdims=True) for x in gv) / width
    rstd = lax.rsqrt(var + EPS)
    r = lax.broadcasted_iota(jnp.int32, (SG, SG), 0)
    c = lax.broadcasted_iota(jnp.int32, (SG, SG), 1)
    out = []
    for j in range(len(gv)):
        nrm = (gv[j] - mu) * rstd * lgs[j] + lbs[j]
        mixed = jnp.dot(jnp.where(r >= c, ws[j], 0.0), nrm, preferred_element_type=F32) + bsb[j]
        out.append(jax.nn.gelu(us[j]) * mixed * jax.nn.silu(zs[j]))
    return out


def _c_args(u_ref, v_ref, z_ref, lg_ref, lb_ref, ws_ref, bs_ref):
    sl = [slice(LANE * j, LANE * (j + 1)) for j in range(4)]
    return ([u_ref[:, s] for s in sl], [v_ref[:, s] for s in sl], [z_ref[:, s] for s in sl],
            [lg_ref[:, s] for s in sl], [lb_ref[:, s] for s in sl],
            [ws_ref[j] for j in range(4)], [bs_ref[j] for j in range(4)])


def _c_fwd(proj, lg, lb, ws, bsb, n):
    cb = C_C // 512

    def body(u_ref, v_ref, z_ref, lg_ref, lb_ref, ws_ref, bs_ref, y_ref):
        outs = _c_chunk(*_c_args(u_ref, v_ref, z_ref, lg_ref, lb_ref, ws_ref, bs_ref))
        for j, o in enumerate(outs):
            y_ref[:, LANE * j:LANE * (j + 1)] = o

    return pl.pallas_call(
        body, name="c_fwd", grid=(n // SG,),
        in_specs=[pl.BlockSpec((SG, 512), lambda i: (i, cb)), pl.BlockSpec((SG, 512), lambda i: (i, cb + 1)),
                  pl.BlockSpec((SG, 512), lambda i: (i, cb + 2)), _full((1, 512)), _full((1, 512)),
                  _full((4, SG, SG)), _full((4, SG, SG))],
        out_specs=pl.BlockSpec((SG, 512), lambda i: (i, 0)),
        out_shape=SDS((n, 512), F32), compiler_params=_cp(32),
    )(proj, proj, proj, lg, lb, ws, bsb)


def _c_bwd(dproj, proj, dy, lg, lb, ws, bsb, n):
    cb = C_C // 512

    def body(dp_any, u_ref, v_ref, z_ref, dy_ref, lg_ref, lb_ref, ws_ref, bs_ref,
             dp_ref, glg_ref, glb_ref, gws_ref, gbs_ref):
        @pl.when(pl.program_id(0) == 0)
        def _():
            glg_ref[...] = jnp.zeros_like(glg_ref)
            glb_ref[...] = jnp.zeros_like(glb_ref)
            gws_ref[...] = jnp.zeros_like(gws_ref)
            gbs_ref[...] = jnp.zeros_like(gbs_ref)

        _, vjp = jax.vjp(_c_chunk, *_c_args(u_ref, v_ref, z_ref, lg_ref, lb_ref, ws_ref, bs_ref))
        dus, dvs, dzs, dlgs, dlbs, dwss, dbss = vjp([dy_ref[:, LANE * j:LANE * (j + 1)] for j in range(4)])
        for j in range(4):
            sl = slice(LANE * j, LANE * (j + 1))
            dp_ref[:, LANE * j:LANE * (j + 1)] = dus[j].astype(BF16)
            dp_ref[:, 512 + LANE * j:512 + LANE * (j + 1)] = dvs[j].astype(BF16)
            dp_ref[:, 1024 + LANE * j:1024 + LANE * (j + 1)] = dzs[j].astype(BF16)
            glg_ref[:, sl] += dlgs[j]
            glb_ref[:, sl] += dlbs[j]
            gws_ref[j] += dwss[j]
            gbs_ref[j] += jnp.broadcast_to(jnp.sum(dbss[j], axis=-1, keepdims=True), (SG, SG))

    return pl.pallas_call(
        body, name="c_bwd", grid=(n // SG,),
        in_specs=[pl.BlockSpec(memory_space=pl.ANY),
                  pl.BlockSpec((SG, 512), lambda i: (i, cb)), pl.BlockSpec((SG, 512), lambda i: (i, cb + 1)),
                  pl.BlockSpec((SG, 512), lambda i: (i, cb + 2)), pl.BlockSpec((SG, 512), lambda i: (i, 0)),
                  _full((1, 512)), _full((1, 512)), _full((4, SG, SG)), _full((4, SG, SG))],
        out_specs=[pl.BlockSpec((SG, 1536), lambda i: (i, C_C // 1536)),
                   _full((1, 512)), _full((1, 512)), _full((4, SG, SG)), _full((4, SG, SG))],
        out_shape=[SDS((n, NP), BF16), SDS((1, 512), F32), SDS((1, 512), F32), SDS((4, SG, SG), F32), SDS((4, SG, SG), F32)],
        input_output_aliases={0: 0}, compiler_params=_cp(32),
    )(dproj, proj, proj, proj, dy, lg, lb, ws, bsb)


def _merge_fwd(x2d, ya, yb, yc, proj, ap, bp, cp, wo):
    n = x2d.shape[0]
    tm = _tile(n, 256)
    gb = C_G // D

    def body(x_ref, ya_ref, yb_ref, yc_ref, g0_ref, g1_ref, g2_ref, ap_ref, bp_ref, cp_ref, wo_ref, o_ref):
        merged = (jax.nn.sigmoid(g0_ref[...]) * _bdot(ya_ref[...], ap_ref[...])
                  + jax.nn.sigmoid(g1_ref[...]) * _bdot(yb_ref[...], bp_ref[...])
                  + jax.nn.sigmoid(g2_ref[...]) * _bdot(yc_ref[...], cp_ref[...]))
        o_ref[...] = x_ref[...] + _bdot(merged, wo_ref[...])

    def rows(w):
        return pl.BlockSpec((tm, w), lambda i: (i, 0))

    return pl.pallas_call(
        body, name="merge_fwd", grid=(n // tm,),
        in_specs=[rows(D), rows(512), rows(D), rows(512),
                  pl.BlockSpec((tm, D), lambda i: (i, gb)), pl.BlockSpec((tm, D), lambda i: (i, gb + 1)),
                  pl.BlockSpec((tm, D), lambda i: (i, gb + 2)),
                  _full((512, D)), _full((D, D)), _full((512, D)), _full((D, D))],
        out_specs=rows(D), out_shape=SDS((n, D), F32), compiler_params=_cp(48),
    )(x2d, ya, yb, yc, proj, proj, proj, ap, bp, cp, wo)


def _merge_bwd(dxo, ya, yb, yc, proj, ap, bp, cp, apt, bpt, cpt, wot):
    n = dxo.shape[0]
    tm = _tile(n, 128)
    gb = C_G // D

    def body(d_ref, ya_ref, yb_ref, yc_ref, g0_ref, g1_ref, g2_ref, ap_ref, bp_ref, cp_ref,
             apt_ref, bpt_ref, cpt_ref, wot_ref,
             dp_ref, dya_ref, dyb_ref, dyc_ref, dpa_ref, dpb_ref, dpc_ref, mg_ref):
        dm = _bdot(d_ref[...], wot_ref[...])
        merged = None
        for j, (g_ref, y_ref, w_ref, wt_ref, dy_ref, dpj_ref) in enumerate((
                (g0_ref, ya_ref, ap_ref, apt_ref, dya_ref, dpa_ref),
                (g1_ref, yb_ref, bp_ref, bpt_ref, dyb_ref, dpb_ref),
                (g2_ref, yc_ref, cp_ref, cpt_ref, dyc_ref, dpc_ref))):
            s = jax.nn.sigmoid(g_ref[...])
            pj = _bdot(y_ref[...], w_ref[...])
            merged = s * pj if merged is None else merged + s * pj
            dp_ref[:, D * j:D * (j + 1)] = (dm * pj * s * (1.0 - s)).astype(BF16)
            dpj = (dm * s).astype(BF16)
            dpj_ref[...] = dpj
            dy_ref[...] = jnp.dot(dpj, wt_ref[...], preferred_element_type=F32)
        mg_ref[...] = merged

    def rows(w):
        return pl.BlockSpec((tm, w), lambda i: (i, 0))

    return pl.pallas_call(
        body, name="merge_bwd", grid=(n // tm,),
        in_specs=[rows(D), rows(512), rows(D), rows(512),
                  pl.BlockSpec((tm, D), lambda i: (i, gb)), pl.BlockSpec((tm, D), lambda i: (i, gb + 1)),
                  pl.BlockSpec((tm, D), lambda i: (i, gb + 2)),
                  _full((512, D)), _full((D, D)), _full((512, D)),
                  _full((D, 512)), _full((D, D)), _full((D, 512)), _full((D, D))],
        out_specs=[pl.BlockSpec((tm, 3 * D), lambda i: (i, C_G // (3 * D))), rows(512), rows(D), rows(512),
                   rows(D), rows(D), rows(D), rows(D)],
        out_shape=[SDS((n, NP), BF16), SDS((n, 512), F32), SDS((n, D), F32), SDS((n, 512), F32),
                   SDS((n, D), BF16), SDS((n, D), BF16), SDS((n, D), BF16), SDS((n, D), F32)],
        compiler_params=_cp(56),
    )(dxo, ya, yb, yc, proj, proj, proj, ap, bp, cp, apt, bpt, cpt, wot)


def _reduce_adamw(parts, w, m, v, name):
    r, c = w.shape
    tr = _tile(r, 128)
    c1 = 1.0 - ADAM_B1 ** ADAM_STEP
    c2 = 1.0 - ADAM_B2 ** ADAM_STEP

    def body(p_ref, w_ref, m_ref, v_ref, g_ref, d_ref, nm_ref, nv_ref):
        g = p_ref[0].astype(F32)
        for s in range(1, NDEV):
            g = g + p_ref[s].astype(F32)
        nm = ADAM_B1 * m_ref[...] + (1.0 - ADAM_B1) * g
        nv = ADAM_B2 * v_ref[...] + (1.0 - ADAM_B2) * jnp.square(g)
        g_ref[...] = g
        nm_ref[...] = nm
        nv_ref[...] = nv
        d_ref[...] = -ADAM_LR * ((nm / c1) / (jnp.sqrt(nv / c2) + ADAM_EPS) + ADAM_WD * w_ref[...])

    blk = pl.BlockSpec((tr, c), lambda i: (i, 0))
    return pl.pallas_call(
        body, name=name, grid=(r // tr,),
        in_specs=[pl.BlockSpec((NDEV, tr, c), lambda i: (0, i, 0)), blk, blk, blk],
        out_specs=[blk, blk, blk, blk], out_shape=[SDS((r, c), F32)] * 4, compiler_params=_cp(48),
    )(parts, w, m, v)


def _leaf_rows(shape):
    size = 1
    for s in shape:
        size *= s
    return -(-size // (8 * PACK_W)) * 8, size


def _pack(arrs, mult, dtype, lead=()):
    parts = []
    for a in arrs:
        rows, size = _leaf_rows(a.shape[len(lead):])
        flat = a.reshape(lead + (size,)).astype(dtype)
        flat = jnp.pad(flat, [(0, 0)] * len(lead) + [(0, rows * PACK_W - size)])
        parts.append(flat.reshape(lead + (rows, PACK_W)))
    buf = jnp.concatenate(parts, axis=len(lead))
    rows = buf.shape[len(lead)]
    total = -(-rows // mult) * mult
    return jnp.pad(buf, [(0, 0)] * len(lead) + [(0, total - rows), (0, 0)])


def _unpack(buf, shapes, lead=()):
    out, off = [], 0
    for shp in shapes:
        rows, size = _leaf_rows(shp)
        part = buf[..., off:off + rows, :].reshape(lead + (rows * PACK_W,))
        out.append(part[..., :size].reshape(lead + tuple(shp)))
        off += rows
    return out


def _unshard(name, g):
    if name in ROW_SHARDED:
        g = jnp.moveaxis(g, 0, 1)
        return g.reshape(g.shape[0], g.shape[1] * g.shape[2], g.shape[3])
    g = jnp.moveaxis(g, 0, 2)
    return g.reshape(g.shape[0], g.shape[1], g.shape[2] * g.shape[3])


def _reshard(name, full):
    l, r, c = full.shape
    if name in ROW_SHARDED:
        return jnp.moveaxis(full.reshape(l, NDEV, r // NDEV, c), 1, 0)
    return jnp.moveaxis(full.reshape(l, r, NDEV, c // NDEV), 2, 0)


def _w_in_to_padded(slabs):
    pieces = []
    for lo, hi, _ in sorted(SEGMENTS, key=lambda s: s[2]):
        for d in range(NDEV):
            a, b = max(lo, d * W_SHARD), min(hi, (d + 1) * W_SHARD)
            if a < b:
                pieces.append(slabs[d, :, :, a - d * W_SHARD:b - d * W_SHARD])
    pieces.append(jnp.zeros(slabs.shape[1:3] + (NP - C_BA - 16,), slabs.dtype))
    return jnp.concatenate(pieces, axis=-1)


def _w_in_from_padded(g):
    slabs = []
    for d in range(NDEV):
        pieces = []
        for lo, hi, pstart in SEGMENTS:
            a, b = max(lo, d * W_SHARD), min(hi, (d + 1) * W_SHARD)
            if a < b:
                pieces.append(g[:, :, pstart + a - lo:pstart + b - lo])
        pieces.append(jnp.zeros(g.shape[:2] + (W_SHARD_PAD - W_SHARD,), g.dtype))
        slabs.append(jnp.concatenate(pieces, axis=-1))
    return jnp.stack(slabs)


def _pad_w_in(w):
    return jnp.pad(w, ((0, 0), (0, 0), (0, W_SHARD_PAD - W_SHARD))).reshape(-1, W_SHARD_PAD)


def _lane_row(vec8, offset):
    return jnp.pad(vec8, (offset, LANE - NH - offset))[None]


def kernel(x, norm_g, w_in, a_dw, a_dw_b, a_ln_g, a_ln_b, a_proj, b_conv, b_a_log, b_dt_bias, b_onorm_g, b_proj, c_ln_g, c_ln_b, c_ws, c_bs, c_proj, w_out, final_g, loss_target, m_norm_g, m_w_in, m_a_dw, m_a_dw_b, m_a_ln_g, m_a_ln_b, m_a_proj, m_b_conv, m_b_a_log, m_b_dt_bias, m_b_onorm_g, m_b_proj, m_c_ln_g, m_c_ln_b, m_c_ws, m_c_bs, m_c_proj, m_w_out, m_final_g, v_norm_g, v_w_in, v_a_dw, v_a_dw_b, v_a_ln_g, v_a_ln_b, v_a_proj, v_b_conv, v_b_a_log, v_b_dt_bias, v_b_onorm_g, v_b_proj, v_c_ln_g, v_c_ln_b, v_c_ws, v_c_bs, v_c_proj, v_w_out, v_final_g):
    wts = dict(norm_g=norm_g, w_in=w_in, a_dw=a_dw, a_dw_b=a_dw_b, a_ln_g=a_ln_g, a_ln_b=a_ln_b, a_proj=a_proj,
               b_conv=b_conv, b_a_log=b_a_log, b_dt_bias=b_dt_bias, b_onorm_g=b_onorm_g, b_proj=b_proj,
               c_ln_g=c_ln_g, c_ln_b=c_ln_b, c_ws=c_ws, c_bs=c_bs, c_proj=c_proj, w_out=w_out, final_g=final_g)
    mom = dict(norm_g=m_norm_g, w_in=m_w_in, a_dw=m_a_dw, a_dw_b=m_a_dw_b, a_ln_g=m_a_ln_g, a_ln_b=m_a_ln_b,
               a_proj=m_a_proj, b_conv=m_b_conv, b_a_log=m_b_a_log, b_dt_bias=m_b_dt_bias, b_onorm_g=m_b_onorm_g,
               b_proj=m_b_proj, c_ln_g=m_c_ln_g, c_ln_b=m_c_ln_b, c_ws=m_c_ws, c_bs=m_c_bs, c_proj=m_c_proj,
               w_out=m_w_out, final_g=m_final_g)
    vel = dict(norm_g=v_norm_g, w_in=v_w_in, a_dw=v_a_dw, a_dw_b=v_a_dw_b, a_ln_g=v_a_ln_g, a_ln_b=v_a_ln_b,
               a_proj=v_a_proj, b_conv=v_b_conv, b_a_log=v_b_a_log, b_dt_bias=v_b_dt_bias, b_onorm_g=v_b_onorm_g,
               b_proj=v_b_proj, c_ln_g=v_c_ln_g, c_ln_b=v_c_ln_b, c_ws=v_c_ws, c_bs=v_c_bs, c_proj=v_c_proj,
               w_out=v_w_out, final_g=v_final_g)

    bsz, t, _ = x.shape
    n = bsz * t
    depth = norm_g.shape[0]
    x2d = x.reshape(n, D)
    tgt = loss_target.reshape(n, D)

    w_in_all, big_all = _all_gather([_pad_w_in(w_in.astype(BF16)), _pack([wts[k] for k in BIG_REST], 16, BF16)],
                                    "gather_matmul_weights")
    (small_all,) = _all_gather([_pack([wts[k] for k in SMALL], 8, F32)], "gather_conv_weights")
    full = {}
    for k, g in zip(BIG_REST, _unpack(big_all, [wts[k].shape for k in BIG_REST], lead=(NDEV,))):
        full[k] = _unshard(k, g)
    for k, g in zip(SMALL, _unpack(small_all, [wts[k].shape for k in SMALL], lead=(NDEV,))):
        full[k] = _unshard(k, g)
    wp = _w_in_to_padded(w_in_all.reshape(NDEV, depth, D, W_SHARD_PAD))
    wpt = jnp.swapaxes(wp, 1, 2)
    a_dw32 = jnp.pad(full['a_dw'], ((0, 0), (0, 32 - A_K), (0, 0)))
    b_conv8 = jnp.pad(full['b_conv'], ((0, 0), (0, 8 - B_K), (0, 0)))
    bsb = jnp.broadcast_to(c_bs[..., None], c_bs.shape + (SG,))

    saved = []
    xl = x2d
    for l in range(depth):
        alog, dtb = _lane_row(b_a_log[l], NH), _lane_row(b_dt_bias[l], NH)
        proj, h = _inproj(xl, norm_g[l][None], wp[l])
        ya = _a_fwd(proj, a_dw32[l], a_dw_b[l][None], a_ln_g[l][None], a_ln_b[l][None], bsz, t)
        qkvn = _bprep_fwd(proj, b_conv8[l], bsz, t)
        yb, shist = _delta_fwd(qkvn, proj, alog, dtb, b_onorm_g[l][None], bsz, t)
        yc = _c_fwd(proj, c_ln_g[l][None], c_ln_b[l][None], c_ws[l], bsb[l], n)
        xn = _merge_fwd(xl, ya, yb, yc, proj, full['a_proj'][l], full['b_proj'][l], full['c_proj'][l], full['w_out'][l])
        saved.append((xl, proj, h, ya, yb, yc, qkvn, shist, alog, dtb))
        xl = xn

    dx, g_final, loss_blk = _loss_head(xl, final_g[None], tgt)
    loss = lax.psum(loss_blk[0, 0], ("x", "y", "c"))

    gfull = {k: [None] * depth for k in WEIGHTS if k != 'final_g'}
    for l in reversed(range(depth)):
        xl, proj, h, ya, yb, yc, qkvn, shist, alog, dtb = saved[l]
        ap, bp, cp, wo = full['a_proj'][l], full['b_proj'][l], full['c_proj'][l], full['w_out'][l]
        dproj, dya, dyb, dyc, dpa, dpb, dpc, merged = _merge_bwd(dx, ya, yb, yc, proj, ap, bp, cp, ap.T, bp.T, cp.T, wo.T)
        gfull['a_proj'][l] = _mm_nn(ya.T.astype(BF16), dpa, "grad_a_proj")
        gfull['b_proj'][l] = _mm_nn(yb.T.astype(BF16), dpb, "grad_b_proj")
        gfull['c_proj'][l] = _mm_nn(yc.T.astype(BF16), dpc, "grad_c_proj")
        gfull['w_out'][l] = _mm_nn(merged.T.astype(BF16), dx, "grad_w_out")
        dproj, g_clg, g_clb, g_cws, g_cbs = _c_bwd(dproj, proj, dyc, c_ln_g[l][None], c_ln_b[l][None], c_ws[l], bsb[l], n)
        dproj, g_adw, g_adb, g_alg, g_alb = _a_bwd(dproj, proj, dya, a_dw32[l], a_dw_b[l][None], a_ln_g[l][None],
                                                   a_ln_b[l][None], bsz, t)
        dproj, dqkvn, g_alog, g_dt, g_og = _delta_bwd(dproj, qkvn, proj, shist, dyb, alog, dtb, b_onorm_g[l][None], bsz, t)
        dproj, g_bconv = _bprep_bwd(dproj, proj, dqkvn, b_conv8[l], bsz, t)
        gfull['w_in'][l] = _mm_nn(h.T, dproj, "grad_w_in")
        dx, g_ng = _inproj_bwd(dproj, wpt[l], xl, norm_g[l][None], dx)
        gfull['norm_g'][l] = g_ng[0]
        gfull['a_dw'][l] = g_adw[:A_K]
        gfull['a_dw_b'][l], gfull['a_ln_g'][l], gfull['a_ln_b'][l] = g_adb[0], g_alg[0], g_alb[0]
        gfull['b_conv'][l] = g_bconv[:B_K]
        gfull['b_a_log'][l], gfull['b_dt_bias'][l] = g_alog[0, NH:2 * NH], g_dt[0, NH:2 * NH]
        gfull['b_onorm_g'][l] = g_og[0]
        gfull['c_ln_g'][l], gfull['c_ln_b'][l] = g_clg[0], g_clb[0]
        gfull['c_ws'][l], gfull['c_bs'][l] = g_cws, g_cbs[:, :, 0]
    grad_x = dx.reshape(bsz, t, D)
    gfull = {k: jnp.stack(vs) for k, vs in gfull.items()}
    gfull['final_g'] = g_final[0]

    send_w = _w_in_from_padded(gfull['w_in']).astype(BF16).reshape(NDEV, depth * D, W_SHARD_PAD)
    send_r = _pack([_reshard(k, gfull[k]) for k in SHARDED_REST], 128, BF16, lead=(NDEV,))
    recv_w, recv_r = _all_to_all([send_w, send_r], "exchange_weight_grads")
    outs_w = _reduce_adamw(recv_w, _pad_w_in(w_in), _pad_w_in(m_w_in), _pad_w_in(v_w_in), "adamw_w_in")
    outs_s = _reduce_adamw(recv_r, _pack([wts[k] for k in SHARDED_REST], 128, F32),
                           _pack([mom[k] for k in SHARDED_REST], 128, F32),
                           _pack([vel[k] for k in SHARDED_REST], 128, F32), "adamw_sharded")
    (parts_r,) = _all_gather([_pack([gfull[k] for k in REPL], 8, F32)], "gather_replicated_grads")
    outs_r = _reduce_adamw(parts_r, _pack([wts[k] for k in REPL], 8, F32), _pack([mom[k] for k in REPL], 8, F32),
                           _pack([vel[k] for k in REPL], 8, F32), "adamw_replicated")

    res = []
    for o_w, o_s, o_r in zip(outs_w, outs_s, outs_r):
        leaves = dict(zip(SHARDED_REST, _unpack(o_s, [wts[k].shape for k in SHARDED_REST])))
        leaves.update(zip(REPL, _unpack(o_r, [wts[k].shape for k in REPL])))
        leaves['w_in'] = o_w.reshape(depth, D, W_SHARD_PAD)[:, :, :W_SHARD]
        res.append([leaves[k] for k in WEIGHTS])
    grads, deltas, new_m, new_v = res
    return (loss, grad_x, *grads, *deltas, *new_m, *new_v)
```

```python
import jax
import jax.numpy as jnp
from jax import lax
from jax.experimental import pallas as pl
from jax.experimental.pallas import tpu as pltpu

F32 = jnp.float32
BF16 = jnp.bfloat16
SDS = jax.ShapeDtypeStruct
MESH = pl.DeviceIdType.MESH

NDEV = 8
D = 1024
EPS = 1e-6
LANE = 128
PACK_W = 1024

C_Q, C_K, C_V = 0, 1024, 2048
C_G = 3072
C_A = 6144
C_C = 7680
C_BZ = 9216
C_BA = 10240
NP = 10368
N_IN = 10256
SEGMENTS = ((0, 1536, C_A), (1536, 4608, C_Q), (4608, 5632, C_BZ), (5632, 5648, C_BA), (5648, 7184, C_C), (7184, 10256, C_G))
W_SHARD = N_IN // NDEV
W_SHARD_PAD = 1408

A_K, A_H = 31, 32
B_K, B_H = 4, 8
CH = 64
SG = 128
NH = 8

ADAM_LR, ADAM_B1, ADAM_B2, ADAM_EPS, ADAM_WD, ADAM_STEP = 0.001, 0.9, 0.999, 1e-08, 0.01, 10

WEIGHTS = ['norm_g', 'w_in', 'a_dw', 'a_dw_b', 'a_ln_g', 'a_ln_b', 'a_proj', 'b_conv', 'b_a_log', 'b_dt_bias',
           'b_onorm_g', 'b_proj', 'c_ln_g', 'c_ln_b', 'c_ws', 'c_bs', 'c_proj', 'w_out', 'final_g']
SHARDED_REST = ['a_dw', 'a_proj', 'b_conv', 'b_proj', 'c_proj', 'w_out']
REPL = [n for n in WEIGHTS if n != 'w_in' and n not in SHARDED_REST]
BIG_REST = ['a_proj', 'b_proj', 'c_proj', 'w_out']
SMALL = ['a_dw', 'b_conv']
ROW_SHARDED = ('b_proj', 'w_out')

NN = ((1,), (0,))
NT = ((1,), (1,))
TN = ((0,), (0,))


def _tile(n, pref):
    return pref if (n >= pref and n % pref == 0) else n


def _cp(vmem_mb):
    return pltpu.CompilerParams(vmem_limit_bytes=vmem_mb * 2 ** 20)


def _full(shape):
    nd = len(shape)
    return pl.BlockSpec(shape, lambda *_: (0,) * nd)


def _rms(x, g):
    return x * lax.rsqrt(jnp.mean(x * x, axis=-1, keepdims=True) + EPS) * g


def _softplus(x):
    return jnp.maximum(x, 0.0) + jnp.log1p(jnp.exp(-jnp.abs(x)))


def _bdot(a, b):
    return jnp.dot(a.astype(BF16), b.astype(BF16), preferred_element_type=F32)


def _mm(a, b, dims):
    return lax.dot_general(a, b, (dims, ((), ())), preferred_element_type=F32)


def _all_gather(xs, name):
    nops = len(xs)

    def body(*refs):
        x_refs, out_refs = refs[:nops], refs[nops:2 * nops]
        send_sems, recv_sems, local_sems = refs[2 * nops:]
        x, y, cc = lax.axis_index("x"), lax.axis_index("y"), lax.axis_index("c")
        me, sibling = (x, y, cc), (x, y, 1 - cc)
        chips = [(1 - x, y), (x, 1 - y), (1 - x, 1 - y)]

        def slot(t, px, py, pc):
            return out_refs[t].at[4 * px + 2 * py + pc]

        def copy(t, k, block, to, src=None):
            return pltpu.make_async_remote_copy(
                src_ref=slot(t, *block) if src is None else src, dst_ref=slot(t, *block),
                send_sem=send_sems.at[7 * t + k], recv_sem=recv_sems.at[7 * t + k], device_id=to, device_id_type=MESH)

        ops = range(nops)
        mine = [pltpu.make_async_copy(x_refs[t], slot(t, *me), local_sems.at[t]) for t in ops]
        for cp in mine:
            cp.start()
        first = [copy(t, 0, me, sibling, src=x_refs[t]) for t in ops]
        first += [copy(t, 1 + j, me, (*chip, cc), src=x_refs[t]) for j, chip in enumerate(chips) for t in ops]
        for cp in first:
            cp.start()
        passed = []
        for j, chip in enumerate(chips):
            for t in ops:
                copy(t, 1 + j, (*chip, cc), me).wait_recv()
                fwd = copy(t, 4 + j, (*chip, cc), sibling)
                fwd.start()
                passed.append(fwd)
        for t in ops:
            copy(t, 0, sibling, me).wait_recv()
        for j, chip in enumerate(chips):
            for t in ops:
                copy(t, 4 + j, (*chip, 1 - cc), me).wait_recv()
        for cp in first + passed:
            cp.wait_send()
        for cp in mine:
            cp.wait()

    hbm = pl.BlockSpec(memory_space=pltpu.HBM)
    return pl.pallas_call(
        body, name=name, out_shape=[SDS((NDEV,) + a.shape, a.dtype) for a in xs],
        in_specs=[hbm] * nops, out_specs=[hbm] * nops,
        scratch_shapes=[pltpu.SemaphoreType.DMA((7 * nops,)), pltpu.SemaphoreType.DMA((7 * nops,)),
                        pltpu.SemaphoreType.DMA((nops,))],
    )(*xs)


def _all_to_all(sends, name):
    nops = len(sends)

    def body(*refs):
        ex = _Exchange(True, refs[:nops], refs[nops:2 * nops], *refs[2 * nops:])
        ex.start()
        ex.wait()

    hbm = pl.BlockSpec(memory_space=pltpu.HBM)
    return pl.pallas_call(
        body, name=name, out_shape=[SDS(a.shape, a.dtype) for a in sends],
        in_specs=[hbm] * nops, out_specs=[hbm] * nops, scratch_shapes=_exchange_sems(nops),
    )(*sends)


def _exchange_sems(nops):
    return [pltpu.SemaphoreType.DMA((7 * nops,)), pltpu.SemaphoreType.DMA((7 * nops,)), pltpu.SemaphoreType.DMA((nops,))]


class _Exchange:
    def __init__(self, scatter, in_refs, out_refs, send_sems, recv_sems, local_sems):
        x, y, cc = lax.axis_index("x"), lax.axis_index("y"), lax.axis_index("c")
        me = 4 * x + 2 * y + cc
        nops = len(in_refs)
        self.local = [pltpu.make_async_copy(in_refs[t].at[me] if scatter else in_refs[t], out_refs[t].at[me],
                                            local_sems.at[t]) for t in range(nops)]
        self.sends, self.recvs = [], []
        for k in range(1, NDEV):
            px = 1 - x if k & 4 else x
            py = 1 - y if k & 2 else y
            pc = 1 - cc if k & 1 else cc
            peer = 4 * px + 2 * py + pc
            for t in range(nops):
                sem = 7 * t + k - 1
                self.sends.append(pltpu.make_async_remote_copy(
                    src_ref=in_refs[t].at[peer] if scatter else in_refs[t], dst_ref=out_refs[t].at[me],
                    send_sem=send_sems.at[sem], recv_sem=recv_sems.at[sem], device_id=(px, py, pc), device_id_type=MESH))
                self.recvs.append(pltpu.make_async_remote_copy(
                    src_ref=in_refs[t].at[me] if scatter else in_refs[t], dst_ref=out_refs[t].at[peer],
                    send_sem=send_sems.at[sem], recv_sem=recv_sems.at[sem], device_id=(px, py, pc), device_id_type=MESH))

    def start(self):
        for cp in self.local + self.sends:
            cp.start()

    def wait(self):
        for cp in self.recvs:
            cp.wait_recv()
        for cp in self.sends:
            cp.wait_send()
        for cp in self.local:
            cp.wait()


def _inproj(x2d, g_row, wp):
    n = x2d.shape[0]
    tm, tn = _tile(n, 1024), 1152

    def body(x_ref, g_ref, w_ref, proj_ref, h_ref, hs):
        @pl.when(pl.program_id(1) == 0)
        def _():
            h = _rms(x_ref[...], g_ref[...]).astype(BF16)
            hs[...] = h
            h_ref[...] = h

        proj_ref[...] = jnp.dot(hs[...], w_ref[...], preferred_element_type=F32)

    return pl.pallas_call(
        body, name="inproj", grid=(n // tm, NP // tn),
        in_specs=[pl.BlockSpec((tm, D), lambda i, j: (i, 0)), _full((1, D)), pl.BlockSpec((D, tn), lambda i, j: (0, j))],
        out_specs=[pl.BlockSpec((tm, tn), lambda i, j: (i, j)), pl.BlockSpec((tm, D), lambda i, j: (i, 0))],
        out_shape=[SDS((n, NP), F32), SDS((n, D), BF16)],
        scratch_shapes=[pltpu.VMEM((tm, D), BF16)], compiler_params=_cp(48),
    )(x2d, g_row, wp)


def _mm_nn(at, b, name):
    m, nn = at.shape
    k = b.shape[1]
    tk = 1152 if k % 1152 == 0 else _tile(k, 1024)
    tn = _tile(nn, 1024)

    def body(a_ref, b_ref, o_ref):
        p = jnp.dot(a_ref[...], b_ref[...].astype(BF16), preferred_element_type=F32)

        @pl.when(pl.program_id(1) == 0)
        def _():
            o_ref[...] = p

        @pl.when(pl.program_id(1) > 0)
        def _():
            o_ref[...] += p

    return pl.pallas_call(
        body, name=name, grid=(k // tk, nn // tn),
        in_specs=[pl.BlockSpec((m, tn), lambda j, t: (0, t)), pl.BlockSpec((tn, tk), lambda j, t: (t, j))],
        out_specs=pl.BlockSpec((m, tk), lambda j, t: (0, j)),
        out_shape=SDS((m, k), F32), compiler_params=_cp(48),
    )(at, b)


def _inproj_bwd(dproj, wpt, x2d, g_row, dxo):
    n = x2d.shape[0]
    tm, tk = _tile(n, 1024), 1152
    nk = NP // tk

    def body(dp_ref, w_ref, x_ref, g_ref, dxo_ref, dx_ref, dg_ref, acc):
        i, k = pl.program_id(0), pl.program_id(1)
        p = jnp.dot(dp_ref[...], w_ref[...], preferred_element_type=F32)

        @pl.when(k == 0)
        def _():
            acc[...] = p

        @pl.when(k > 0)
        def _():
            acc[...] += p

        @pl.when(k == nk - 1)
        def _():
            _, vjp = jax.vjp(_rms, x_ref[...], g_ref[...])
            dx, dg = vjp(acc[...])
            dx_ref[...] = dxo_ref[...] + dx

            @pl.when(i == 0)
            def _():
                dg_ref[...] = dg

            @pl.when(i > 0)
            def _():
                dg_ref[...] += dg

    return pl.pallas_call(
        body, name="inproj_bwd", grid=(n // tm, nk),
        in_specs=[pl.BlockSpec((tm, tk), lambda i, k: (i, k)), pl.BlockSpec((tk, D), lambda i, k: (k, 0)),
                  pl.BlockSpec((tm, D), lambda i, k: (i, 0)), _full((1, D)), pl.BlockSpec((tm, D), lambda i, k: (i, 0))],
        out_specs=[pl.BlockSpec((tm, D), lambda i, k: (i, 0)), _full((1, D))],
        out_shape=[SDS((n, D), F32), SDS((1, D), F32)],
        scratch_shapes=[pltpu.VMEM((tm, D), F32)], compiler_params=_cp(56),
    )(dproj, wpt, x2d, g_row, dxo)


def _loss_head(x2d, g_row, tgt):
    n = x2d.shape[0]
    tm = _tile(n, 512)

    def body(x_ref, g_ref, t_ref, dx_ref, dg_ref, loss_ref):
        i = pl.program_id(0)
        y, vjp = jax.vjp(_rms, x_ref[...], g_ref[...])
        err = y - t_ref[...]
        part = 0.5 * jnp.sum(jnp.mean(err * err, axis=-1, keepdims=True), axis=0, keepdims=True)
        dx, dg = vjp(err * (1.0 / D))
        dx_ref[...] = dx
        lb = jnp.broadcast_to(part, (8, LANE))

        @pl.when(i == 0)
        def _():
            dg_ref[...] = dg
            loss_ref[...] = lb

        @pl.when(i > 0)
        def _():
            dg_ref[...] += dg
            loss_ref[...] += lb

    return pl.pallas_call(
        body, name="loss_head", grid=(n // tm,),
        in_specs=[pl.BlockSpec((tm, D), lambda i: (i, 0)), _full((1, D)), pl.BlockSpec((tm, D), lambda i: (i, 0))],
        out_specs=[pl.BlockSpec((tm, D), lambda i: (i, 0)), _full((1, D)), _full((8, LANE))],
        out_shape=[SDS((n, D), F32), SDS((1, D), F32), SDS((8, LANE), F32)], compiler_params=_cp(40),
    )(x2d, g_row, tgt)


def _conv_fwd(ext_ref, w_ref, kw, halo, tt):
    acc = None
    for j in range(kw):
        term = w_ref[j:j + 1, :] * ext_ref[pl.ds(halo - (kw - 1) + j, tt), :]
        acc = term if acc is None else acc + term
    return acc


def _conv_bwd_x(dcp_ref, w_ref, kw, halo, tt):
    acc = None
    for j in range(kw):
        term = w_ref[j:j + 1, :] * dcp_ref[pl.ds(kw - 1 - j, tt + halo), :]
        acc = term if acc is None else acc + term
    return acc


def _conv_bwd_w(dc, ext_ref, gw_ref, kw, halo, tt):
    for j in range(kw):
        gw_ref[j:j + 1, :] += jnp.sum(dc * ext_ref[pl.ds(halo - (kw - 1) + j, tt), :], axis=0, keepdims=True)


SUB = 8


def _fill_shifts(sh_ref, rows):
    for s in range(1, SUB):
        sh_ref[s, 0:rows, :] = sh_ref[0, pl.ds(s, rows), :]


def _tap(sh_ref, off, rows):
    return sh_ref[off % SUB, pl.ds(off - off % SUB, rows), :]


def _conv_fwd_sh(ext_sh, w_ref, kw, halo, tt):
    acc = None
    for j in range(kw):
        term = w_ref[j:j + 1, :] * _tap(ext_sh, halo - (kw - 1) + j, tt)
        acc = term if acc is None else acc + term
    return acc


def _conv_bwd_x_sh(dcp_sh, w_ref, kw, halo, tt):
    acc = None
    for j in range(kw):
        term = w_ref[j:j + 1, :] * _tap(dcp_sh, kw - 1 - j, tt + halo)
        acc = term if acc is None else acc + term
    return acc


def _conv_bwd_w_sh(dc, ext_sh, gw_ref, kw, halo, tt):
    for j in range(kw):
        gw_ref[j:j + 1, :] += jnp.sum(dc * _tap(ext_sh, halo - (kw - 1) + j, tt), axis=0, keepdims=True)


def _a_post(c, z, g, b):
    mu = jnp.mean(c, axis=-1, keepdims=True)
    var = jnp.mean(jnp.square(c - mu), axis=-1, keepdims=True)
    a = (c - mu) * lax.rsqrt(var + EPS) * g + b
    return jax.nn.silu(a) * jax.nn.silu(z)


def _a_fwd(proj, dw, b_row, lg, lb, bsz, t):
    n = bsz * t
    tt = _tile(t, 256)
    nt = t // tt
    cb = C_A // 512

    def body(v_ref, g_ref, z_ref, vh_ref, gh_ref, w_ref, b_ref, lg_ref, lb_ref, y_ref, ext):
        i = pl.program_id(1)
        ext[0, 0:A_H, :] = jnp.where(i > 0, vh_ref[...] * jax.nn.sigmoid(gh_ref[...]), 0.0)
        ext[0, A_H:, :] = v_ref[...] * jax.nn.sigmoid(g_ref[...])
        _fill_shifts(ext, tt + A_H - SUB)
        c = _conv_fwd_sh(ext, w_ref, A_K, A_H, tt) + b_ref[...]
        y_ref[...] = _a_post(c, z_ref[...], lg_ref[...], lb_ref[...])

    def row(b, i):
        return b * nt + i

    def halo(b, i):
        return jnp.maximum((b * t + i * tt) // A_H - 1, 0)

    return pl.pallas_call(
        body, name="a_fwd", grid=(bsz, nt),
        in_specs=[pl.BlockSpec((tt, 512), lambda b, i: (row(b, i), cb)),
                  pl.BlockSpec((tt, 512), lambda b, i: (row(b, i), cb + 1)),
                  pl.BlockSpec((tt, 512), lambda b, i: (row(b, i), cb + 2)),
                  pl.BlockSpec((A_H, 512), lambda b, i: (halo(b, i), cb)),
                  pl.BlockSpec((A_H, 512), lambda b, i: (halo(b, i), cb + 1)),
                  _full((32, 512)), _full((1, 512)), _full((1, 512)), _full((1, 512))],
        out_specs=pl.BlockSpec((tt, 512), lambda b, i: (row(b, i), 0)),
        out_shape=SDS((n, 512), F32),
        scratch_shapes=[pltpu.VMEM((SUB, tt + A_H, 512), F32)], compiler_params=_cp(40),
    )(proj, proj, proj, proj, proj, dw, b_row, lg, lb)


def _a_bwd(dproj, proj, dy, dw, b_row, lg, lb, bsz, t):
    n = bsz * t
    tt = _tile(t, 256)
    nt = t // tt
    cb = C_A // 512

    def body(dp_any, v_ref, g_ref, z_ref, vh_ref, gh_ref, dy_ref, w_ref, b_ref, lg_ref, lb_ref,
             dp_ref, gw_ref, gb_ref, glg_ref, glb_ref, ext, dcp, dae, carry):
        b, i = pl.program_id(0), pl.program_id(1)
        ti = nt - 1 - i

        @pl.when((b == 0) & (i == 0))
        def _():
            gw_ref[...] = jnp.zeros_like(gw_ref)
            gb_ref[...] = jnp.zeros_like(gb_ref)
            glg_ref[...] = jnp.zeros_like(glg_ref)
            glb_ref[...] = jnp.zeros_like(glb_ref)

        @pl.when(i == 0)
        def _():
            carry[...] = jnp.zeros_like(carry)

        val, glu = v_ref[...], g_ref[...]
        sg = jax.nn.sigmoid(glu)
        ext[0, 0:A_H, :] = jnp.where(ti > 0, vh_ref[...] * jax.nn.sigmoid(gh_ref[...]), 0.0)
        ext[0, A_H:, :] = val * sg
        _fill_shifts(ext, tt + A_H - SUB)
        c = _conv_fwd_sh(ext, w_ref, A_K, A_H, tt) + b_ref[...]
        _, vjp = jax.vjp(_a_post, c, z_ref[...], lg_ref[...], lb_ref[...])
        dc, dz, dlg, dlb = vjp(dy_ref[...])
        gb_ref[...] += jnp.sum(dc, axis=0, keepdims=True)
        glg_ref[...] += dlg
        glb_ref[...] += dlb
        dcp[0, 0:A_H, :] = jnp.zeros((A_H, 512), F32)
        dcp[0, A_H:A_H + tt, :] = dc
        dcp[0, A_H + tt:, :] = jnp.zeros((A_H, 512), F32)
        _fill_shifts(dcp, tt + 2 * A_H - SUB)
        _conv_bwd_w_sh(dc, ext, gw_ref, A_K, A_H, tt)
        dae[...] = _conv_bwd_x_sh(dcp, w_ref, A_K, A_H, tt)
        dae[tt:tt + A_H, :] += carry[...]
        carry[...] = dae[0:A_H, :]
        da = dae[A_H:, :]
        dp_ref[:, 0:512] = (da * sg).astype(BF16)
        dp_ref[:, 512:1024] = (da * val * sg * (1.0 - sg)).astype(BF16)
        dp_ref[:, 1024:1536] = dz.astype(BF16)

    def row(b, i):
        return b * nt + (nt - 1 - i)

    def halo(b, i):
        return jnp.maximum((b * t + (nt - 1 - i) * tt) // A_H - 1, 0)

    outs = pl.pallas_call(
        body, name="a_bwd", grid=(bsz, nt),
        in_specs=[pl.BlockSpec(memory_space=pl.ANY),
                  pl.BlockSpec((tt, 512), lambda b, i: (row(b, i), cb)),
                  pl.BlockSpec((tt, 512), lambda b, i: (row(b, i), cb + 1)),
                  pl.BlockSpec((tt, 512), lambda b, i: (row(b, i), cb + 2)),
                  pl.BlockSpec((A_H, 512), lambda b, i: (halo(b, i), cb)),
                  pl.BlockSpec((A_H, 512), lambda b, i: (halo(b, i), cb + 1)),
                  pl.BlockSpec((tt, 512), lambda b, i: (row(b, i), 0)),
                  _full((32, 512)), _full((1, 512)), _full((1, 512)), _full((1, 512))],
        out_specs=[pl.BlockSpec((tt, 1536), lambda b, i: (row(b, i), C_A // 1536)),
                   _full((32, 512)), _full((1, 512)), _full((1, 512)), _full((1, 512))],
        out_shape=[SDS((n, NP), BF16), SDS((32, 512), F32), SDS((1, 512), F32), SDS((1, 512), F32), SDS((1, 512), F32)],
        input_output_aliases={0: 0},
        scratch_shapes=[pltpu.VMEM((SUB, tt + A_H, 512), F32), pltpu.VMEM((SUB, tt + 2 * A_H, 512), F32),
                        pltpu.VMEM((tt + A_H, 512), F32), pltpu.VMEM((A_H, 512), F32)],
        compiler_params=_cp(48),
    )(dproj, proj, proj, proj, proj, proj, dy, dw, b_row, lg, lb)
    return outs


def _b_post(blocks):
    out = []
    for idx, c in enumerate(blocks):
        s = jax.nn.silu(c)
        if idx < 2 * NH:
            s = s * lax.rsqrt(jnp.sum(s * s, axis=-1, keepdims=True) + EPS)
            if idx < NH:
                s = s * (LANE ** -0.5)
        out.append(s)
    return out


def _bprep_fwd(proj, wconv, bsz, t):
    n = bsz * t
    tt = _tile(t, 256)
    nt = t // tt

    def body(x_ref, xh_ref, w_ref, o_ref, ext):
        i = pl.program_id(1)
        ext[0:B_H, :] = jnp.where(i > 0, xh_ref[...], 0.0)
        ext[B_H:, :] = x_ref[...]
        c = _conv_fwd(ext, w_ref, B_K, B_H, tt)
        outs = _b_post([c[:, LANE * j:LANE * (j + 1)] for j in range(3 * NH)])
        for j, o in enumerate(outs):
            o_ref[:, LANE * j:LANE * (j + 1)] = o

    return pl.pallas_call(
        body, name="bprep_fwd", grid=(bsz, nt),
        in_specs=[pl.BlockSpec((tt, 3072), lambda b, i: (b * nt + i, 0)),
                  pl.BlockSpec((B_H, 3072), lambda b, i: (jnp.maximum((b * t + i * tt) // B_H - 1, 0), 0)),
                  _full((8, 3072))],
        out_specs=pl.BlockSpec((tt, 3072), lambda b, i: (b * nt + i, 0)),
        out_shape=SDS((n, 3072), F32),
        scratch_shapes=[pltpu.VMEM((tt + B_H, 3072), F32)], compiler_params=_cp(48),
    )(proj, proj, wconv)


def _bprep_bwd(dproj, proj, dqkvn, wconv, bsz, t):
    n = bsz * t
    tt = _tile(t, 256)
    nt = t // tt

    def body(dp_any, x_ref, xh_ref, dq_ref, w_ref, dp_ref, gw_ref, ext, dcp, dae, carry):
        b, i = pl.program_id(0), pl.program_id(1)
        ti = nt - 1 - i

        @pl.when((b == 0) & (i == 0))
        def _():
            gw_ref[...] = jnp.zeros_like(gw_ref)

        @pl.when(i == 0)
        def _():
            carry[...] = jnp.zeros_like(carry)

        ext[0:B_H, :] = jnp.where(ti > 0, xh_ref[...], 0.0)
        ext[B_H:, :] = x_ref[...]
        c = _conv_fwd(ext, w_ref, B_K, B_H, tt)
        _, vjp = jax.vjp(_b_post, [c[:, LANE * j:LANE * (j + 1)] for j in range(3 * NH)])
        (dcs,) = vjp([dq_ref[:, LANE * j:LANE * (j + 1)] for j in range(3 * NH)])
        dcp[0:B_H, :] = jnp.zeros((B_H, 3072), F32)
        for j, dcj in enumerate(dcs):
            dcp[B_H:B_H + tt, LANE * j:LANE * (j + 1)] = dcj
        dcp[B_H + tt:, :] = jnp.zeros((B_H, 3072), F32)
        _conv_bwd_w(dcp[B_H:B_H + tt, :], ext, gw_ref, B_K, B_H, tt)
        dae[...] = _conv_bwd_x(dcp, w_ref, B_K, B_H, tt)
        dae[tt:tt + B_H, :] += carry[...]
        carry[...] = dae[0:B_H, :]
        dp_ref[...] = dae[B_H:, :].astype(BF16)

    def row(b, i):
        return b * nt + (nt - 1 - i)

    return pl.pallas_call(
        body, name="bprep_bwd", grid=(bsz, nt),
        in_specs=[pl.BlockSpec(memory_space=pl.ANY),
                  pl.BlockSpec((tt, 3072), lambda b, i: (row(b, i), 0)),
                  pl.BlockSpec((B_H, 3072), lambda b, i: (jnp.maximum((b * t + (nt - 1 - i) * tt) // B_H - 1, 0), 0)),
                  pl.BlockSpec((tt, 3072), lambda b, i: (row(b, i), 0)),
                  _full((8, 3072))],
        out_specs=[pl.BlockSpec((tt, 3072), lambda b, i: (row(b, i), 0)), _full((8, 3072))],
        out_shape=[SDS((n, NP), BF16), SDS((8, 3072), F32)],
        input_output_aliases={0: 0},
        scratch_shapes=[pltpu.VMEM((tt + B_H, 3072), F32), pltpu.VMEM((tt + 2 * B_H, 3072), F32),
                        pltpu.VMEM((tt + B_H, 3072), F32), pltpu.VMEM((B_H, 3072), F32)],
        compiler_params=_cp(56),
    )(dproj, proj, proj, dqkvn, wconv)


def _split2(a):
    hi = a.astype(BF16)
    return hi, (a - hi.astype(F32)).astype(BF16)


def _split3(a):
    p1 = a.astype(BF16)
    r1 = a - p1.astype(F32)
    p2 = r1.astype(BF16)
    return p1, p2, (r1 - p2.astype(F32)).astype(BF16)


def _dot3_raw(a, b, dims):
    a1, a2 = _split2(a)
    b1, b2 = _split2(b)
    return _mm(a1, b1, dims) + (_mm(a1, b2, dims) + _mm(a2, b1, dims))


def _dot6(a, b, dims):
    a1, a2, a3 = _split3(a)
    b1, b2, b3 = _split3(b)
    return (_mm(a1, b1, dims) + (_mm(a1, b2, dims) + _mm(a2, b1, dims))
            + (_mm(a1, b3, dims) + _mm(a2, b2, dims) + _mm(a3, b1, dims)))


def _unit_lower_inverse_raw(lmats):
    r = lax.broadcasted_iota(jnp.int32, (CH, CH), 0)
    c = lax.broadcasted_iota(jnp.int32, (CH, CH), 1)
    eye = (r == c).astype(F32)
    blk = jnp.right_shift(r, 4) == jnp.right_shift(c, 4)
    dm = [jnp.where(blk, x, 0.0) for x in lmats]
    om = [a - b for a, b in zip(lmats, dm)]
    d2 = [_dot3_raw(x, x, NN) for x in dm]
    d4 = [_dot3_raw(x, x, NN) for x in d2]
    d8 = [_dot3_raw(x, x, NN) for x in d4]
    p = [_dot3_raw(eye - a, eye + b, NN) for a, b in zip(dm, d2)]
    p = [_dot3_raw(a, eye + b, NN) for a, b in zip(p, d4)]
    p = [_dot3_raw(a, eye + b, NN) for a, b in zip(p, d8)]
    m = [_dot3_raw(a, b, NN) for a, b in zip(p, om)]
    m2 = [_dot3_raw(x, x, NN) for x in m]
    t = [_dot3_raw(eye - a, eye + b, NN) for a, b in zip(m, m2)]
    return [_dot3_raw(a, b, NN) for a, b in zip(t, p)]


@jax.custom_vjp
def _unit_lower_inverse(lmats):
    return _unit_lower_inverse_raw(lmats)


def _unit_lower_inverse_fwd(lmats):
    tinv = _unit_lower_inverse_raw(lmats)
    return tinv, tinv


def _unit_lower_inverse_bwd(tinv, gs):
    x = [_dot6(t, g, TN) for t, g in zip(tinv, gs)]
    return ([-_dot6(a, t, NT) for a, t in zip(x, tinv)],)


_unit_lower_inverse.defvjp(_unit_lower_inverse_fwd, _unit_lower_inverse_bwd)


def _tri(lower):
    r = lax.broadcasted_iota(jnp.int32, (CH, CH), 0)
    c = lax.broadcasted_iota(jnp.int32, (CH, CH), 1)
    return ((r >= c) if lower else (r <= c)).astype(BF16)


def _tri_dot(x, lower, dims, tri_first):
    p1, p2, p3 = _split3(x)
    tri = _tri(lower)
    if tri_first:
        return _mm(tri, p1, dims) + (_mm(tri, p2, dims) + _mm(tri, p3, dims))
    return _mm(p1, tri, dims) + (_mm(p2, tri, dims) + _mm(p3, tri, dims))


@jax.custom_vjp
def _cumsum_rows(x):
    return _tri_dot(x, True, NN, True)


def _cumsum_rows_fwd(x):
    return _tri_dot(x, True, NN, True), None


def _cumsum_rows_bwd(_, g):
    return (_tri_dot(g, True, TN, True),)


_cumsum_rows.defvjp(_cumsum_rows_fwd, _cumsum_rows_bwd)


@jax.custom_vjp
def _cumsum_rows_t(x):
    return _tri_dot(x, False, TN, False)


def _cumsum_rows_t_fwd(x):
    return _tri_dot(x, False, TN, False), None


def _cumsum_rows_t_bwd(_, g):
    return (_tri_dot(g, False, NT, True),)


_cumsum_rows_t.defvjp(_cumsum_rows_t_fwd, _cumsum_rows_t_bwd)


def _delta_chunk(ss, qs, ks, vs, ba, zs, alog, dtb, og):
    heads = range(NH)
    lane = lax.broadcasted_iota(jnp.int32, (1, LANE), 1)
    r = lax.broadcasted_iota(jnp.int32, (CH, CH), 0)
    c = lax.broadcasted_iota(jnp.int32, (CH, CH), 1)
    ri = lax.broadcasted_iota(jnp.int32, (CH, 1), 0)
    incl, strict = r >= c, r > c

    def pick(x, h):
        return jnp.sum(jnp.where(lane == h, x, 0.0), axis=-1, keepdims=True)

    beta = [jax.nn.sigmoid(pick(ba, h)) for h in heads]
    g = [-jnp.exp(pick(alog, h + NH)) * _softplus(pick(ba, h + NH) + pick(dtb, h + NH)) for h in heads]
    gb = [jnp.broadcast_to(x, (CH, CH)) for x in g]
    gca = [_cumsum_rows(x) for x in gb]
    gcr = [_cumsum_rows_t(x) for x in gb]
    gc = [jnp.sum(jnp.where(c == 0, x, 0.0), axis=-1, keepdims=True) for x in gca]
    gl = [jnp.sum(jnp.where(ri == CH - 1, x, 0.0), axis=0, keepdims=True) for x in gc]
    diff = [a - b for a, b in zip(gca, gcr)]
    gam_s = [jnp.where(strict, jnp.exp(jnp.where(strict, x, 0.0)), 0.0) for x in diff]
    gam_i = [jnp.where(incl, jnp.exp(jnp.where(incl, x, 0.0)), 0.0) for x in diff]

    kk = [_mm(k, k, NT) for k in ks]
    tinv = _unit_lower_inverse([beta[h] * kk[h] * gam_s[h] for h in heads])

    eg = [jnp.exp(x) for x in gc]
    u = [_mm(tinv[h], vs[h] * beta[h], NN) for h in heads]
    w = [_mm(tinv[h], ks[h] * (beta[h] * eg[h]), NN) for h in heads]
    qk = [_mm(qs[h], ks[h], NT) * gam_i[h] for h in heads]
    vn = [u[h] - _mm(w[h], ss[h], NN) for h in heads]
    o = [_mm(qs[h] * eg[h], ss[h], NN) + _mm(qk[h], vn[h], NN) for h in heads]
    sn = [jnp.exp(gl[h]) * ss[h] + _mm(ks[h] * jnp.exp(gl[h] - gc[h]), vn[h], TN) for h in heads]
    y = [_rms(o[h], og) * jax.nn.silu(zs[h]) for h in heads]
    return sn, y


def _head_blocks(ref, base=0):
    return [ref[:, base + LANE * h:base + LANE * (h + 1)] for h in range(NH)]


def _ride_along(ex, first, last):
    if ex is None:
        return

    @pl.when(first)
    def _():
        ex.start()

    @pl.when(last)
    def _():
        ex.wait()


def _delta_fwd(qkvn, proj, alog, dtb, og, bsz, t, gather=()):
    n = bsz * t
    nc = t // CH
    ng = len(gather)

    def body(q_ref, k_ref, v_ref, ba_ref, z_ref, al_ref, dt_ref, og_ref, *rest):
        g_in, (y_ref, sh_ref), g_out, (s_scr, *sems) = rest[:ng], rest[ng:ng + 2], rest[ng + 2:2 * ng + 2], rest[2 * ng + 2:]
        b, ci = pl.program_id(0), pl.program_id(1)
        _ride_along(_Exchange(False, g_in, g_out, *sems) if ng else None,
                    (b == 0) & (ci == 0), (b == bsz - 1) & (ci == nc - 1))

        @pl.when(ci == 0)
        def _():
            s_scr[...] = jnp.zeros_like(s_scr)

        ss = [s_scr[h] for h in range(NH)]
        for h in range(NH):
            sh_ref[h] = ss[h]
        sn, y = _delta_chunk(ss, _head_blocks(q_ref), _head_blocks(k_ref), _head_blocks(v_ref), ba_ref[...],
                             _head_blocks(z_ref), al_ref[...], dt_ref[...], og_ref[...])
        for h in range(NH):
            s_scr[h] = sn[h]
            y_ref[:, LANE * h:LANE * (h + 1)] = y[h]

    def blk(width, col):
        return pl.BlockSpec((CH, width), lambda b, ci: (b * nc + ci, col))

    hbm = pl.BlockSpec(memory_space=pltpu.HBM)
    return pl.pallas_call(
        body, name="delta_fwd_gather" if ng else "delta_fwd", grid=(bsz, nc),
        in_specs=[blk(D, 0), blk(D, 1), blk(D, 2), blk(LANE, C_BA // LANE), blk(D, C_BZ // D),
                  _full((1, LANE)), _full((1, LANE)), _full((1, LANE))] + [hbm] * ng,
        out_specs=[blk(D, 0), pl.BlockSpec((None, None, NH, LANE, LANE), lambda b, ci: (b, ci, 0, 0, 0))] + [hbm] * ng,
        out_shape=[SDS((n, D), F32), SDS((bsz, nc, NH, LANE, LANE), F32)] + [SDS((NDEV,) + a.shape, a.dtype) for a in gather],
        scratch_shapes=[pltpu.VMEM((NH, LANE, LANE), F32)] + (_exchange_sems(ng) if ng else []),
        compiler_params=_cp(40),
    )(qkvn, qkvn, qkvn, proj, proj, alog, dtb, og, *gather)


def _delta_bwd(dproj, qkvn, proj, shist, dyb, alog, dtb, og, bsz, t, scatter=()):
    n = bsz * t
    nc = t // CH
    ns = len(scatter)

    def body(dp_any, q_ref, k_ref, v_ref, ba_ref, z_ref, sh_ref, dy_ref, al_ref, dt_ref, og_ref, *rest):
        s_in, (dp_ref, dqkv_ref, gal_ref, gdt_ref, gog_ref) = rest[:ns], rest[ns:ns + 5]
        s_out, (ds_scr, *sems) = rest[ns + 5:2 * ns + 5], rest[2 * ns + 5:]
        b, ci = pl.program_id(0), pl.program_id(1)
        _ride_along(_Exchange(True, s_in, s_out, *sems) if ns else None,
                    (b == 0) & (ci == 0), (b == bsz - 1) & (ci == nc - 1))

        @pl.when((b == 0) & (ci == 0))
        def _():
            gal_ref[...] = jnp.zeros_like(gal_ref)
            gdt_ref[...] = jnp.zeros_like(gdt_ref)
            gog_ref[...] = jnp.zeros_like(gog_ref)

        @pl.when(ci == 0)
        def _():
            ds_scr[...] = jnp.zeros_like(ds_scr)

        _, vjp = jax.vjp(_delta_chunk, [sh_ref[h] for h in range(NH)], _head_blocks(q_ref), _head_blocks(k_ref),
                         _head_blocks(v_ref), ba_ref[...], _head_blocks(z_ref), al_ref[...], dt_ref[...], og_ref[...])
        ds, dq, dk, dv, dba, dz, dal, ddt, dog = vjp(([ds_scr[h] for h in range(NH)], _head_blocks(dy_ref)))
        gal_ref[...] += dal
        gdt_ref[...] += ddt
        gog_ref[...] += dog
        dp_ref[:, D:D + LANE] = dba.astype(BF16)
        for h in range(NH):
            ds_scr[h] = ds[h]
            dp_ref[:, LANE * h:LANE * (h + 1)] = dz[h].astype(BF16)
            dqkv_ref[:, LANE * h:LANE * (h + 1)] = dq[h]
            dqkv_ref[:, D + LANE * h:D + LANE * (h + 1)] = dk[h]
            dqkv_ref[:, 2 * D + LANE * h:2 * D + LANE * (h + 1)] = dv[h]

    def blk(width, col):
        return pl.BlockSpec((CH, width), lambda b, ci: (b * nc + (nc - 1 - ci), col))

    hbm = pl.BlockSpec(memory_space=pltpu.HBM)
    return pl.pallas_call(
        body, name="delta_bwd_exchange" if ns else "delta_bwd", grid=(bsz, nc),
        in_specs=[pl.BlockSpec(memory_space=pl.ANY), blk(D, 0), blk(D, 1), blk(D, 2), blk(LANE, C_BA // LANE),
                  blk(D, C_BZ // D),
                  pl.BlockSpec((None, None, NH, LANE, LANE), lambda b, ci: (b, nc - 1 - ci, 0, 0, 0)),
                  blk(D, 0), _full((1, LANE)), _full((1, LANE)), _full((1, LANE))] + [hbm] * ns,
        out_specs=[blk(D + LANE, C_BZ // (D + LANE)), blk(3 * D, 0), _full((1, LANE)), _full((1, LANE)),
                   _full((1, LANE))] + [hbm] * ns,
        out_shape=[SDS((n, NP), BF16), SDS((n, 3 * D), F32), SDS((1, LANE), F32), SDS((1, LANE), F32),
                   SDS((1, LANE), F32)] + [SDS(a.shape, a.dtype) for a in scatter],
        input_output_aliases={0: 0},
        scratch_shapes=[pltpu.VMEM((NH, LANE, LANE), F32)] + (_exchange_sems(ns) if ns else []),
        compiler_params=_cp(48),
    )(dproj, qkvn, qkvn, qkvn, proj, proj, shist, dyb, alog, dtb, og, *scatter)


def _c_chunk(us, vs, zs, lgs, lbs, ws, bsb):
    gv = [jax.nn.gelu(v) for v in vs]
    width = LANE * len(gv)
    mu = sum(jnp.sum(x, axis=-1, keepdims=True) for x in gv) / width
    var = sum(jnp.sum(jnp.square(x - mu), axis=-1, keepdims=True) for x in gv) / width
    rstd = lax.rsqrt(var + EPS)
    r = lax.broadcasted_iota(jnp.int32, (SG, SG), 0)
    c = lax.broadcasted_iota(jnp.int32, (SG, SG), 1)
    out = []
    for j in range(len(gv)):
        nrm = (gv[j] - mu) * rstd * lgs[j] + lbs[j]
        mixed = jnp.dot(jnp.where(r >= c, ws[j], 0.0), nrm, preferred_element_type=F32) + bsb[j]
        out.append(jax.nn.gelu(us[j]) * mixed * jax.nn.silu(zs[j]))
    return out


def _c_args(u_ref, v_ref, z_ref, lg_ref, lb_ref, ws_ref, bs_ref):
    sl = [slice(LANE * j, LANE * (j + 1)) for j in range(4)]
    return ([u_ref[:, s] for s in sl], [v_ref[:, s] for s in sl], [z_ref[:, s] for s in sl],
            [lg_ref[:, s] for s in sl], [lb_ref[:, s] for s in sl],
            [ws_ref[j] for j in range(4)], [bs_ref[j] for j in range(4)])


def _c_fwd(proj, lg, lb, ws, bsb, n):
    cb = C_C // 512

    def body(u_ref, v_ref, z_ref, lg_ref, lb_ref, ws_ref, bs_ref, y_ref):
        outs = _c_chunk(*_c_args(u_ref, v_ref, z_ref, lg_ref, lb_ref, ws_ref, bs_ref))
        for j, o in enumerate(outs):
            y_ref[:, LANE * j:LANE * (j + 1)] = o

    return pl.pallas_call(
        body, name="c_fwd", grid=(n // SG,),
        in_specs=[pl.BlockSpec((SG, 512), lambda i: (i, cb)), pl.BlockSpec((SG, 512), lambda i: (i, cb + 1)),
                  pl.BlockSpec((SG, 512), lambda i: (i, cb + 2)), _full((1, 512)), _full((1, 512)),
                  _full((4, SG, SG)), _full((4, SG, SG))],
        out_specs=pl.BlockSpec((SG, 512), lambda i: (i, 0)),
        out_shape=SDS((n, 512), F32), compiler_params=_cp(32),
    )(proj, proj, proj, lg, lb, ws, bsb)


def _c_bwd(dproj, proj, dy, lg, lb, ws, bsb, n):
    cb = C_C // 512

    def body(dp_any, u_ref, v_ref, z_ref, dy_ref, lg_ref, lb_ref, ws_ref, bs_ref,
             dp_ref, glg_ref, glb_ref, gws_ref, gbs_ref):
        @pl.when(pl.program_id(0) == 0)
        def _():
            glg_ref[...] = jnp.zeros_like(glg_ref)
            glb_ref[...] = jnp.zeros_like(glb_ref)
            gws_ref[...] = jnp.zeros_like(gws_ref)
            gbs_ref[...] = jnp.zeros_like(gbs_ref)

        _, vjp = jax.vjp(_c_chunk, *_c_args(u_ref, v_ref, z_ref, lg_ref, lb_ref, ws_ref, bs_ref))
        dus, dvs, dzs, dlgs, dlbs, dwss, dbss = vjp([dy_ref[:, LANE * j:LANE * (j + 1)] for j in range(4)])
        for j in range(4):
            sl = slice(LANE * j, LANE * (j + 1))
            dp_ref[:, LANE * j:LANE * (j + 1)] = dus[j].astype(BF16)
            dp_ref[:, 512 + LANE * j:512 + LANE * (j + 1)] = dvs[j].astype(BF16)
            dp_ref[:, 1024 + LANE * j:1024 + LANE * (j + 1)] = dzs[j].astype(BF16)
            glg_ref[:, sl] += dlgs[j]
            glb_ref[:, sl] += dlbs[j]
            gws_ref[j] += dwss[j]
            gbs_ref[j] += jnp.broadcast_to(jnp.sum(dbss[j], axis=-1, keepdims=True), (SG, SG))

    return pl.pallas_call(
        body, name="c_bwd", grid=(n // SG,),
        in_specs=[pl.BlockSpec(memory_space=pl.ANY),
                  pl.BlockSpec((SG, 512), lambda i: (i, cb)), pl.BlockSpec((SG, 512), lambda i: (i, cb + 1)),
                  pl.BlockSpec((SG, 512), lambda i: (i, cb + 2)), pl.BlockSpec((SG, 512), lambda i: (i, 0)),
                  _full((1, 512)), _full((1, 512)), _full((4, SG, SG)), _full((4, SG, SG))],
        out_specs=[pl.BlockSpec((SG, 1536), lambda i: (i, C_C // 1536)),
                   _full((1, 512)), _full((1, 512)), _full((4, SG, SG)), _full((4, SG, SG))],
        out_shape=[SDS((n, NP), BF16), SDS((1, 512), F32), SDS((1, 512), F32), SDS((4, SG, SG), F32), SDS((4, SG, SG), F32)],
        input_output_aliases={0: 0}, compiler_params=_cp(32),
    )(dproj, proj, proj, proj, dy, lg, lb, ws, bsb)


def _merge_fwd(x2d, ya, yb, yc, proj, ap, bp, cp, wo):
    n = x2d.shape[0]
    tm = _tile(n, 256)
    gb = C_G // D

    def body(x_ref, ya_ref, yb_ref, yc_ref, g0_ref, g1_ref, g2_ref, ap_ref, bp_ref, cp_ref, wo_ref, o_ref):
        merged = (jax.nn.sigmoid(g0_ref[...]) * _bdot(ya_ref[...], ap_ref[...])
                  + jax.nn.sigmoid(g1_ref[...]) * _bdot(yb_ref[...], bp_ref[...])
                  + jax.nn.sigmoid(g2_ref[...]) * _bdot(yc_ref[...], cp_ref[...]))
        o_ref[...] = x_ref[...] + _bdot(merged, wo_ref[...])

    def rows(w):
        return pl.BlockSpec((tm, w), lambda i: (i, 0))

    return pl.pallas_call(
        body, name="merge_fwd", grid=(n // tm,),
        in_specs=[rows(D), rows(512), rows(D), rows(512),
                  pl.BlockSpec((tm, D), lambda i: (i, gb)), pl.BlockSpec((tm, D), lambda i: (i, gb + 1)),
                  pl.BlockSpec((tm, D), lambda i: (i, gb + 2)),
                  _full((512, D)), _full((D, D)), _full((512, D)), _full((D, D))],
        out_specs=rows(D), out_shape=SDS((n, D), F32), compiler_params=_cp(48),
    )(x2d, ya, yb, yc, proj, proj, proj, ap, bp, cp, wo)


def _merge_bwd(dxo, ya, yb, yc, proj, ap, bp, cp, apt, bpt, cpt, wot):
    n = dxo.shape[0]
    tm = _tile(n, 128)
    gb = C_G // D

    def body(d_ref, ya_ref, yb_ref, yc_ref, g0_ref, g1_ref, g2_ref, ap_ref, bp_ref, cp_ref,
             apt_ref, bpt_ref, cpt_ref, wot_ref,
             dp_ref, dya_ref, dyb_ref, dyc_ref, dpa_ref, dpb_ref, dpc_ref, mg_ref):
        dm = _bdot(d_ref[...], wot_ref[...])
        merged = None
        for j, (g_ref, y_ref, w_ref, wt_ref, dy_ref, dpj_ref) in enumerate((
                (g0_ref, ya_ref, ap_ref, apt_ref, dya_ref, dpa_ref),
                (g1_ref, yb_ref, bp_ref, bpt_ref, dyb_ref, dpb_ref),
                (g2_ref, yc_ref, cp_ref, cpt_ref, dyc_ref, dpc_ref))):
            s = jax.nn.sigmoid(g_ref[...])
            pj = _bdot(y_ref[...], w_ref[...])
            merged = s * pj if merged is None else merged + s * pj
            dp_ref[:, D * j:D * (j + 1)] = (dm * pj * s * (1.0 - s)).astype(BF16)
            dpj = (dm * s).astype(BF16)
            dpj_ref[...] = dpj
            dy_ref[...] = jnp.dot(dpj, wt_ref[...], preferred_element_type=F32)
        mg_ref[...] = merged

    def rows(w):
        return pl.BlockSpec((tm, w), lambda i: (i, 0))

    return pl.pallas_call(
        body, name="merge_bwd", grid=(n // tm,),
        in_specs=[rows(D), rows(512), rows(D), rows(512),
                  pl.BlockSpec((tm, D), lambda i: (i, gb)), pl.BlockSpec((tm, D), lambda i: (i, gb + 1)),
                  pl.BlockSpec((tm, D), lambda i: (i, gb + 2)),
                  _full((512, D)), _full((D, D)), _full((512, D)),
                  _full((D, 512)), _full((D, D)), _full((D, 512)), _full((D, D))],
        out_specs=[pl.BlockSpec((tm, 3 * D), lambda i: (i, C_G // (3 * D))), rows(512), rows(D), rows(512),
                   rows(D), rows(D), rows(D), rows(D)],
        out_shape=[SDS((n, NP), BF16), SDS((n, 512), F32), SDS((n, D), F32), SDS((n, 512), F32),
                   SDS((n, D), BF16), SDS((n, D), BF16), SDS((n, D), BF16), SDS((n, D), F32)],
        compiler_params=_cp(56),
    )(dxo, ya, yb, yc, proj, proj, proj, ap, bp, cp, apt, bpt, cpt, wot)


def _reduce_adamw(parts, w, m, v, name):
    r, c = w.shape
    tr = _tile(r, 128)
    c1 = 1.0 - ADAM_B1 ** ADAM_STEP
    c2 = 1.0 - ADAM_B2 ** ADAM_STEP

    def body(p_ref, w_ref, m_ref, v_ref, g_ref, d_ref, nm_ref, nv_ref):
        g = p_ref[0].astype(F32)
        for s in range(1, NDEV):
            g = g + p_ref[s].astype(F32)
        nm = ADAM_B1 * m_ref[...] + (1.0 - ADAM_B1) * g
        nv = ADAM_B2 * v_ref[...] + (1.0 - ADAM_B2) * jnp.square(g)
        g_ref[...] = g
        nm_ref[...] = nm
        nv_ref[...] = nv
        d_ref[...] = -ADAM_LR * ((nm / c1) / (jnp.sqrt(nv / c2) + ADAM_EPS) + ADAM_WD * w_ref[...])

    blk = pl.BlockSpec((tr, c), lambda i: (i, 0))
    return pl.pallas_call(
        body, name=name, grid=(r // tr,),
        in_specs=[pl.BlockSpec((NDEV, tr, c), lambda i: (0, i, 0)), blk, blk, blk],
        out_specs=[blk, blk, blk, blk], out_shape=[SDS((r, c), F32)] * 4, compiler_params=_cp(48),
    )(parts, w, m, v)


def _leaf_rows(shape):
    size = 1
    for s in shape:
        size *= s
    return -(-size // (8 * PACK_W)) * 8, size


def _pack(arrs, mult, dtype, lead=()):
    parts = []
    for a in arrs:
        rows, size = _leaf_rows(a.shape[len(lead):])
        flat = a.reshape(lead + (size,)).astype(dtype)
        flat = jnp.pad(flat, [(0, 0)] * len(lead) + [(0, rows * PACK_W - size)])
        parts.append(flat.reshape(lead + (rows, PACK_W)))
    buf = jnp.concatenate(parts, axis=len(lead))
    rows = buf.shape[len(lead)]
    total = -(-rows // mult) * mult
    return jnp.pad(buf, [(0, 0)] * len(lead) + [(0, total - rows), (0, 0)])


def _unpack(buf, shapes, lead=()):
    out, off = [], 0
    for shp in shapes:
        rows, size = _leaf_rows(shp)
        part = buf[..., off:off + rows, :].reshape(lead + (rows * PACK_W,))
        out.append(part[..., :size].reshape(lead + tuple(shp)))
        off += rows
    return out


def _unshard(name, g):
    if name in ROW_SHARDED:
        return g.reshape(g.shape[0] * g.shape[1], g.shape[2])
    g = jnp.moveaxis(g, 0, 1)
    return g.reshape(g.shape[0], g.shape[1] * g.shape[2])


def _reshard(name, full):
    r, c = full.shape
    if name in ROW_SHARDED:
        return full.reshape(NDEV, r // NDEV, c)
    return jnp.moveaxis(full.reshape(r, NDEV, c // NDEV), 1, 0)


def _w_in_to_padded(slabs):
    pieces = []
    for lo, hi, _ in sorted(SEGMENTS, key=lambda s: s[2]):
        for d in range(NDEV):
            a, b = max(lo, d * W_SHARD), min(hi, (d + 1) * W_SHARD)
            if a < b:
                pieces.append(slabs[d, :, a - d * W_SHARD:b - d * W_SHARD])
    pieces.append(jnp.zeros(slabs.shape[1:2] + (NP - C_BA - 16,), slabs.dtype))
    return jnp.concatenate(pieces, axis=-1)


def _w_in_from_padded(g):
    slabs = []
    for d in range(NDEV):
        pieces = []
        for lo, hi, pstart in SEGMENTS:
            a, b = max(lo, d * W_SHARD), min(hi, (d + 1) * W_SHARD)
            if a < b:
                pieces.append(g[:, pstart + a - lo:pstart + b - lo])
        pieces.append(jnp.zeros(g.shape[:1] + (W_SHARD_PAD - W_SHARD,), g.dtype))
        slabs.append(jnp.concatenate(pieces, axis=-1))
    return jnp.stack(slabs)


def _pad_w_in(w):
    return jnp.pad(w, ((0, 0), (0, W_SHARD_PAD - W_SHARD)))


def _lane_row(vec8, offset):
    return jnp.pad(vec8, (offset, LANE - NH - offset))[None]


def kernel(x, norm_g, w_in, a_dw, a_dw_b, a_ln_g, a_ln_b, a_proj, b_conv, b_a_log, b_dt_bias, b_onorm_g, b_proj, c_ln_g, c_ln_b, c_ws, c_bs, c_proj, w_out, final_g, loss_target, m_norm_g, m_w_in, m_a_dw, m_a_dw_b, m_a_ln_g, m_a_ln_b, m_a_proj, m_b_conv, m_b_a_log, m_b_dt_bias, m_b_onorm_g, m_b_proj, m_c_ln_g, m_c_ln_b, m_c_ws, m_c_bs, m_c_proj, m_w_out, m_final_g, v_norm_g, v_w_in, v_a_dw, v_a_dw_b, v_a_ln_g, v_a_ln_b, v_a_proj, v_b_conv, v_b_a_log, v_b_dt_bias, v_b_onorm_g, v_b_proj, v_c_ln_g, v_c_ln_b, v_c_ws, v_c_bs, v_c_proj, v_w_out, v_final_g):
    wts = dict(norm_g=norm_g, w_in=w_in, a_dw=a_dw, a_dw_b=a_dw_b, a_ln_g=a_ln_g, a_ln_b=a_ln_b, a_proj=a_proj,
               b_conv=b_conv, b_a_log=b_a_log, b_dt_bias=b_dt_bias, b_onorm_g=b_onorm_g, b_proj=b_proj,
               c_ln_g=c_ln_g, c_ln_b=c_ln_b, c_ws=c_ws, c_bs=c_bs, c_proj=c_proj, w_out=w_out, final_g=final_g)
    mom = dict(norm_g=m_norm_g, w_in=m_w_in, a_dw=m_a_dw, a_dw_b=m_a_dw_b, a_ln_g=m_a_ln_g, a_ln_b=m_a_ln_b,
               a_proj=m_a_proj, b_conv=m_b_conv, b_a_log=m_b_a_log, b_dt_bias=m_b_dt_bias, b_onorm_g=m_b_onorm_g,
               b_proj=m_b_proj, c_ln_g=m_c_ln_g, c_ln_b=m_c_ln_b, c_ws=m_c_ws, c_bs=m_c_bs, c_proj=m_c_proj,
               w_out=m_w_out, final_g=m_final_g)
    vel = dict(norm_g=v_norm_g, w_in=v_w_in, a_dw=v_a_dw, a_dw_b=v_a_dw_b, a_ln_g=v_a_ln_g, a_ln_b=v_a_ln_b,
               a_proj=v_a_proj, b_conv=v_b_conv, b_a_log=v_b_a_log, b_dt_bias=v_b_dt_bias, b_onorm_g=v_b_onorm_g,
               b_proj=v_b_proj, c_ln_g=v_c_ln_g, c_ln_b=v_c_ln_b, c_ws=v_c_ws, c_bs=v_c_bs, c_proj=v_c_proj,
               w_out=v_w_out, final_g=v_final_g)

    bsz, t, _ = x.shape
    n = bsz * t
    depth = norm_g.shape[0]
    x2d = x.reshape(n, D)
    tgt = loss_target.reshape(n, D)

    def weight_blocks(l):
        return [_pad_w_in(w_in[l].astype(BF16)), _pack([wts[k][l] for k in BIG_REST], 16, BF16)]

    def matmul_weights(w_in_all, big_all, l):
        got = {k: _unshard(k, g) for k, g in
               zip(BIG_REST, _unpack(big_all, [wts[k][l].shape for k in BIG_REST], lead=(NDEV,)))}
        got['wp'] = _w_in_to_padded(w_in_all)
        return got

    (small_all,) = _all_gather([_pack([wts[k] for k in SMALL], 8, F32)], "gather_conv_weights")
    conv_w = {k: jnp.stack([_unshard(k, g[:, l]) for l in range(depth)]) for k, g in
              zip(SMALL, _unpack(small_all, [wts[k].shape for k in SMALL], lead=(NDEV,)))}
    a_dw32 = jnp.pad(conv_w['a_dw'], ((0, 0), (0, 32 - A_K), (0, 0)))
    b_conv8 = jnp.pad(conv_w['b_conv'], ((0, 0), (0, 8 - B_K), (0, 0)))
    bsb = jnp.broadcast_to(c_bs[..., None], c_bs.shape + (SG,))
    full = [matmul_weights(*_all_gather(weight_blocks(0), "gather_matmul_weights"), 0)]

    saved = []
    xl = x2d
    for l in range(depth):
        alog, dtb = _lane_row(b_a_log[l], NH), _lane_row(b_dt_bias[l], NH)
        proj, h = _inproj(xl, norm_g[l][None], full[l]['wp'])
        ya = _a_fwd(proj, a_dw32[l], a_dw_b[l][None], a_ln_g[l][None], a_ln_b[l][None], bsz, t)
        qkvn = _bprep_fwd(proj, b_conv8[l], bsz, t)
        yb, shist, *nxt = _delta_fwd(qkvn, proj, alog, dtb, b_onorm_g[l][None], bsz, t,
                                     gather=weight_blocks(l + 1) if l + 1 < depth else ())
        if nxt:
            full.append(matmul_weights(*nxt, l + 1))
        yc = _c_fwd(proj, c_ln_g[l][None], c_ln_b[l][None], c_ws[l], bsb[l], n)
        xn = _merge_fwd(xl, ya, yb, yc, proj, full[l]['a_proj'], full[l]['b_proj'], full[l]['c_proj'], full[l]['w_out'])
        saved.append((xl, proj, h, ya, yb, yc, qkvn, shist, alog, dtb))
        xl = xn

    dx, g_final, loss_blk = _loss_head(xl, final_g[None], tgt)
    loss = lax.psum(loss_blk[0, 0], ("x", "y", "c"))

    gfull = {k: [None] * depth for k in WEIGHTS if k != 'final_g'}
    recv = [None] * depth

    def grad_slabs(l):
        return [_w_in_from_padded(gfull['w_in'][l]).astype(BF16),
                _pack([_reshard(k, gfull[k][l]) for k in SHARDED_REST], 128, BF16, lead=(NDEV,))]

    for l in reversed(range(depth)):
        xl, proj, h, ya, yb, yc, qkvn, shist, alog, dtb = saved[l]
        ap, bp, cp, wo = full[l]['a_proj'], full[l]['b_proj'], full[l]['c_proj'], full[l]['w_out']
        dproj, dya, dyb, dyc, dpa, dpb, dpc, merged = _merge_bwd(dx, ya, yb, yc, proj, ap, bp, cp, ap.T, bp.T, cp.T, wo.T)
        gfull['a_proj'][l] = _mm_nn(ya.T.astype(BF16), dpa, "grad_a_proj")
        gfull['b_proj'][l] = _mm_nn(yb.T.astype(BF16), dpb, "grad_b_proj")
        gfull['c_proj'][l] = _mm_nn(yc.T.astype(BF16), dpc, "grad_c_proj")
        gfull['w_out'][l] = _mm_nn(merged.T.astype(BF16), dx, "grad_w_out")
        dproj, g_clg, g_clb, g_cws, g_cbs = _c_bwd(dproj, proj, dyc, c_ln_g[l][None], c_ln_b[l][None], c_ws[l], bsb[l], n)
        dproj, g_adw, g_adb, g_alg, g_alb = _a_bwd(dproj, proj, dya, a_dw32[l], a_dw_b[l][None], a_ln_g[l][None],
                                                   a_ln_b[l][None], bsz, t)
        dproj, dqkvn, g_alog, g_dt, g_og, *got = _delta_bwd(dproj, qkvn, proj, shist, dyb, alog, dtb, b_onorm_g[l][None],
                                                            bsz, t, scatter=grad_slabs(l + 1) if l + 1 < depth else ())
        if got:
            recv[l + 1] = got
        dproj, g_bconv = _bprep_bwd(dproj, proj, dqkvn, b_conv8[l], bsz, t)
        gfull['w_in'][l] = _mm_nn(h.T, dproj, "grad_w_in")
        dx, g_ng = _inproj_bwd(dproj, full[l]['wp'].T, xl, norm_g[l][None], dx)
        gfull['norm_g'][l] = g_ng[0]
        gfull['a_dw'][l] = g_adw[:A_K]
        gfull['a_dw_b'][l], gfull['a_ln_g'][l], gfull['a_ln_b'][l] = g_adb[0], g_alg[0], g_alb[0]
        gfull['b_conv'][l] = g_bconv[:B_K]
        gfull['b_a_log'][l], gfull['b_dt_bias'][l] = g_alog[0, NH:2 * NH], g_dt[0, NH:2 * NH]
        gfull['b_onorm_g'][l] = g_og[0]
        gfull['c_ln_g'][l], gfull['c_ln_b'][l] = g_clg[0], g_clb[0]
        gfull['c_ws'][l], gfull['c_bs'][l] = g_cws, g_cbs[:, :, 0]
    grad_x = dx.reshape(bsz, t, D)
    recv[0] = _all_to_all(grad_slabs(0), "exchange_weight_grads")

    outs_w, outs_s = [], []
    for l in range(depth):
        recv_w, recv_r = recv[l]
        outs_w.append(_reduce_adamw(recv_w, _pad_w_in(w_in[l]), _pad_w_in(m_w_in[l]), _pad_w_in(v_w_in[l]), "adamw_w_in"))
        outs_s.append(_reduce_adamw(recv_r, _pack([wts[k][l] for k in SHARDED_REST], 128, F32),
                                    _pack([mom[k][l] for k in SHARDED_REST], 128, F32),
                                    _pack([vel[k][l] for k in SHARDED_REST], 128, F32), "adamw_sharded"))
    grepl = {k: (jnp.stack(gfull[k]) if k != 'final_g' else g_final[0]) for k in REPL}
    (parts_r,) = _all_gather([_pack([grepl[k] for k in REPL], 8, F32)], "gather_replicated_grads")
    outs_r = _reduce_adamw(parts_r, _pack([wts[k] for k in REPL], 8, F32), _pack([mom[k] for k in REPL], 8, F32),
                           _pack([vel[k] for k in REPL], 8, F32), "adamw_replicated")

    res = []
    for j in range(4):
        leaves = dict(zip(REPL, _unpack(outs_r[j], [wts[k].shape for k in REPL])))
        per_layer = [_unpack(outs_s[l][j], [wts[k][l].shape for k in SHARDED_REST]) for l in range(depth)]
        for i, k in enumerate(SHARDED_REST):
            leaves[k] = jnp.stack([per_layer[l][i] for l in range(depth)])
        leaves['w_in'] = jnp.stack([outs_w[l][j][:, :W_SHARD] for l in range(depth)])
        res.append([leaves[k] for k in WEIGHTS])
    grads, deltas, new_m, new_v = res
    return (loss, grad_x, *grads, *deltas, *new_m, *new_v)
```

```python
import jax
import jax.numpy as jnp
from jax import lax
from jax.experimental import pallas as pl
from jax.experimental.pallas import tpu as pltpu

F32 = jnp.float32
BF16 = jnp.bfloat16
SDS = jax.ShapeDtypeStruct
MESH = pl.DeviceIdType.MESH

NDEV = 8
D = 1024
EPS = 1e-6
LANE = 128
PACK_W = 1024

C_Q, C_K, C_V = 0, 1024, 2048
C_G = 3072
C_A = 6144
C_C = 7680
C_BZ = 9216
C_BA = 10240
NP = 10368
N_IN = 10256
SEGMENTS = ((0, 1536, C_A), (1536, 4608, C_Q), (4608, 5632, C_BZ), (5632, 5648, C_BA), (5648, 7184, C_C), (7184, 10256, C_G))
W_SHARD = N_IN // NDEV
W_SHARD_PAD = 1408

A_K, A_H = 31, 32
B_K, B_H = 4, 8
CH = 64
SG = 128
NH = 8

ADAM_LR, ADAM_B1, ADAM_B2, ADAM_EPS, ADAM_WD, ADAM_STEP = 0.001, 0.9, 0.999, 1e-08, 0.01, 10

WEIGHTS = ['norm_g', 'w_in', 'a_dw', 'a_dw_b', 'a_ln_g', 'a_ln_b', 'a_proj', 'b_conv', 'b_a_log', 'b_dt_bias',
           'b_onorm_g', 'b_proj', 'c_ln_g', 'c_ln_b', 'c_ws', 'c_bs', 'c_proj', 'w_out', 'final_g']
SHARDED_REST = ['a_dw', 'a_proj', 'b_conv', 'b_proj', 'c_proj', 'w_out']
REPL = [n for n in WEIGHTS if n != 'w_in' and n not in SHARDED_REST]
BIG_REST = ['a_proj', 'b_proj', 'c_proj', 'w_out']
SMALL = ['a_dw', 'b_conv']
ROW_SHARDED = ('b_proj', 'w_out')

NN = ((1,), (0,))
NT = ((1,), (1,))
TN = ((0,), (0,))


def _tile(n, pref):
    return pref if (n >= pref and n % pref == 0) else n


def _cp(vmem_mb):
    return pltpu.CompilerParams(vmem_limit_bytes=vmem_mb * 2 ** 20)


def _full(shape):
    nd = len(shape)
    return pl.BlockSpec(shape, lambda *_: (0,) * nd)


def _rms(x, g):
    return x * lax.rsqrt(jnp.mean(x * x, axis=-1, keepdims=True) + EPS) * g


def _softplus(x):
    return jnp.maximum(x, 0.0) + jnp.log1p(jnp.exp(-jnp.abs(x)))


def _bdot(a, b):
    return jnp.dot(a.astype(BF16), b.astype(BF16), preferred_element_type=F32)


def _mm(a, b, dims):
    return lax.dot_general(a, b, (dims, ((), ())), preferred_element_type=F32)


def _all_gather(xs, name):
    nops = len(xs)

    def body(*refs):
        x_refs, out_refs = refs[:nops], refs[nops:2 * nops]
        send_sems, recv_sems, local_sems = refs[2 * nops:]
        x, y, cc = lax.axis_index("x"), lax.axis_index("y"), lax.axis_index("c")
        me, sibling = (x, y, cc), (x, y, 1 - cc)
        chips = [(1 - x, y), (x, 1 - y), (1 - x, 1 - y)]

        def slot(t, px, py, pc):
            return out_refs[t].at[4 * px + 2 * py + pc]

        def copy(t, k, block, to, src=None):
            return pltpu.make_async_remote_copy(
                src_ref=slot(t, *block) if src is None else src, dst_ref=slot(t, *block),
                send_sem=send_sems.at[7 * t + k], recv_sem=recv_sems.at[7 * t + k], device_id=to, device_id_type=MESH)

        ops = range(nops)
        mine = [pltpu.make_async_copy(x_refs[t], slot(t, *me), local_sems.at[t]) for t in ops]
        for cp in mine:
            cp.start()
        first = [copy(t, 0, me, sibling, src=x_refs[t]) for t in ops]
        first += [copy(t, 1 + j, me, (*chip, cc), src=x_refs[t]) for j, chip in enumerate(chips) for t in ops]
        for cp in first:
            cp.start()
        passed = []
        for j, chip in enumerate(chips):
            for t in ops:
                copy(t, 1 + j, (*chip, cc), me).wait_recv()
                fwd = copy(t, 4 + j, (*chip, cc), sibling)
                fwd.start()
                passed.append(fwd)
        for t in ops:
            copy(t, 0, sibling, me).wait_recv()
        for j, chip in enumerate(chips):
            for t in ops:
                copy(t, 4 + j, (*chip, 1 - cc), me).wait_recv()
        for cp in first + passed:
            cp.wait_send()
        for cp in mine:
            cp.wait()

    hbm = pl.BlockSpec(memory_space=pltpu.HBM)
    return pl.pallas_call(
        body, name=name, out_shape=[SDS((NDEV,) + a.shape, a.dtype) for a in xs],
        in_specs=[hbm] * nops, out_specs=[hbm] * nops,
        scratch_shapes=[pltpu.SemaphoreType.DMA((7 * nops,)), pltpu.SemaphoreType.DMA((7 * nops,)),
                        pltpu.SemaphoreType.DMA((nops,))],
    )(*xs)


def _exchange_sems(nops):
    return [pltpu.SemaphoreType.DMA((7 * nops,)), pltpu.SemaphoreType.DMA((7 * nops,)), pltpu.SemaphoreType.DMA((nops,))]


class _Exchange:
    def __init__(self, scatter, in_refs, out_refs, send_sems, recv_sems, local_sems):
        x, y, cc = lax.axis_index("x"), lax.axis_index("y"), lax.axis_index("c")
        me = 4 * x + 2 * y + cc
        nops = len(in_refs)
        self.local = [pltpu.make_async_copy(in_refs[t].at[me] if scatter else in_refs[t], out_refs[t].at[me],
                                            local_sems.at[t]) for t in range(nops)]
        self.sends, self.recvs = [], []
        for k in range(1, NDEV):
            px = 1 - x if k & 4 else x
            py = 1 - y if k & 2 else y
            pc = 1 - cc if k & 1 else cc
            peer = 4 * px + 2 * py + pc
            for t in range(nops):
                sem = 7 * t + k - 1
                self.sends.append(pltpu.make_async_remote_copy(
                    src_ref=in_refs[t].at[peer] if scatter else in_refs[t], dst_ref=out_refs[t].at[me],
                    send_sem=send_sems.at[sem], recv_sem=recv_sems.at[sem], device_id=(px, py, pc), device_id_type=MESH))
                self.recvs.append(pltpu.make_async_remote_copy(
                    src_ref=in_refs[t].at[me] if scatter else in_refs[t], dst_ref=out_refs[t].at[peer],
                    send_sem=send_sems.at[sem], recv_sem=recv_sems.at[sem], device_id=(px, py, pc), device_id_type=MESH))

    def start(self):
        for cp in self.local + self.sends:
            cp.start()

    def wait(self):
        for cp in self.recvs:
            cp.wait_recv()
        for cp in self.sends:
            cp.wait_send()
        for cp in self.local:
            cp.wait()


def _inproj(x2d, g_row, wp):
    n = x2d.shape[0]
    tm, tn = _tile(n, 1024), 1152

    def body(x_ref, g_ref, w_ref, proj_ref, h_ref, hs):
        @pl.when(pl.program_id(1) == 0)
        def _():
            h = _rms(x_ref[...], g_ref[...]).astype(BF16)
            hs[...] = h
            h_ref[...] = h

        proj_ref[...] = jnp.dot(hs[...], w_ref[...], preferred_element_type=F32)

    return pl.pallas_call(
        body, name="inproj", grid=(n // tm, NP // tn),
        in_specs=[pl.BlockSpec((tm, D), lambda i, j: (i, 0)), _full((1, D)), pl.BlockSpec((D, tn), lambda i, j: (0, j))],
        out_specs=[pl.BlockSpec((tm, tn), lambda i, j: (i, j)), pl.BlockSpec((tm, D), lambda i, j: (i, 0))],
        out_shape=[SDS((n, NP), F32), SDS((n, D), BF16)],
        scratch_shapes=[pltpu.VMEM((tm, D), BF16)], compiler_params=_cp(48),
    )(x2d, g_row, wp)


def _mm_tn(a, b, name):
    nn, m = a.shape
    k = b.shape[1]
    tk = 1152 if k % 1152 == 0 else _tile(k, 1024)
    tn = _tile(nn, 1024)

    def body(a_ref, b_ref, o_ref):
        p = _mm(a_ref[...].astype(BF16), b_ref[...].astype(BF16), TN)

        @pl.when(pl.program_id(1) == 0)
        def _():
            o_ref[...] = p

        @pl.when(pl.program_id(1) > 0)
        def _():
            o_ref[...] += p

    return pl.pallas_call(
        body, name=name, grid=(k // tk, nn // tn),
        in_specs=[pl.BlockSpec((tn, m), lambda j, t: (t, 0)), pl.BlockSpec((tn, tk), lambda j, t: (t, j))],
        out_specs=pl.BlockSpec((m, tk), lambda j, t: (0, j)),
        out_shape=SDS((m, k), F32), compiler_params=_cp(48),
    )(a, b)


def _inproj_bwd(dproj, wp, x2d, g_row, dxo, scatter=()):
    n = x2d.shape[0]
    tm, tk = _tile(n, 1024), 1152
    nk = NP // tk
    ns = len(scatter)

    def body(dp_ref, w_ref, x_ref, g_ref, dxo_ref, *rest):
        s_in, (dx_ref, dg_ref), s_out, (acc, *sems) = rest[:ns], rest[ns:ns + 2], rest[ns + 2:2 * ns + 2], rest[2 * ns + 2:]
        i, k = pl.program_id(0), pl.program_id(1)
        _ride_along(_Exchange(True, s_in, s_out, *sems) if ns else None,
                    (i == 0) & (k == 0), (i == n // tm - 1) & (k == nk - 1))
        p = _mm(dp_ref[...], w_ref[...], NT)

        @pl.when(k == 0)
        def _():
            acc[...] = p

        @pl.when(k > 0)
        def _():
            acc[...] += p

        @pl.when(k == nk - 1)
        def _():
            _, vjp = jax.vjp(_rms, x_ref[...], g_ref[...])
            dx, dg = vjp(acc[...])
            dx_ref[...] = dxo_ref[...] + dx

            @pl.when(i == 0)
            def _():
                dg_ref[...] = dg

            @pl.when(i > 0)
            def _():
                dg_ref[...] += dg

    hbm = pl.BlockSpec(memory_space=pltpu.HBM)
    return pl.pallas_call(
        body, name="inproj_bwd_exchange" if ns else "inproj_bwd", grid=(n // tm, nk),
        in_specs=[pl.BlockSpec((tm, tk), lambda i, k: (i, k)), pl.BlockSpec((D, tk), lambda i, k: (0, k)),
                  pl.BlockSpec((tm, D), lambda i, k: (i, 0)), _full((1, D)),
                  pl.BlockSpec((tm, D), lambda i, k: (i, 0))] + [hbm] * ns,
        out_specs=[pl.BlockSpec((tm, D), lambda i, k: (i, 0)), _full((1, D))] + [hbm] * ns,
        out_shape=[SDS((n, D), F32), SDS((1, D), F32)] + [SDS(a.shape, a.dtype) for a in scatter],
        scratch_shapes=[pltpu.VMEM((tm, D), F32)] + (_exchange_sems(ns) if ns else []), compiler_params=_cp(56),
    )(dproj, wp, x2d, g_row, dxo, *scatter)


def _loss_head(x2d, g_row, tgt):
    n = x2d.shape[0]
    tm = _tile(n, 512)

    def body(x_ref, g_ref, t_ref, dx_ref, dg_ref, loss_ref):
        i = pl.program_id(0)
        y, vjp = jax.vjp(_rms, x_ref[...], g_ref[...])
        err = y - t_ref[...]
        part = 0.5 * jnp.sum(jnp.mean(err * err, axis=-1, keepdims=True), axis=0, keepdims=True)
        dx, dg = vjp(err * (1.0 / D))
        dx_ref[...] = dx
        lb = jnp.broadcast_to(part, (8, LANE))

        @pl.when(i == 0)
        def _():
            dg_ref[...] = dg
            loss_ref[...] = lb

        @pl.when(i > 0)
        def _():
            dg_ref[...] += dg
            loss_ref[...] += lb

    return pl.pallas_call(
        body, name="loss_head", grid=(n // tm,),
        in_specs=[pl.BlockSpec((tm, D), lambda i: (i, 0)), _full((1, D)), pl.BlockSpec((tm, D), lambda i: (i, 0))],
        out_specs=[pl.BlockSpec((tm, D), lambda i: (i, 0)), _full((1, D)), _full((8, LANE))],
        out_shape=[SDS((n, D), F32), SDS((1, D), F32), SDS((8, LANE), F32)], compiler_params=_cp(40),
    )(x2d, g_row, tgt)


def _conv_fwd(ext_ref, w_ref, kw, halo, tt):
    acc = None
    for j in range(kw):
        term = w_ref[j:j + 1, :] * ext_ref[pl.ds(halo - (kw - 1) + j, tt), :]
        acc = term if acc is None else acc + term
    return acc


def _conv_bwd_x(dcp_ref, w_ref, kw, halo, tt):
    acc = None
    for j in range(kw):
        term = w_ref[j:j + 1, :] * dcp_ref[pl.ds(kw - 1 - j, tt + halo), :]
        acc = term if acc is None else acc + term
    return acc


def _conv_bwd_w(dc, ext_ref, gw_ref, kw, halo, tt):
    for j in range(kw):
        gw_ref[j:j + 1, :] += jnp.sum(dc * ext_ref[pl.ds(halo - (kw - 1) + j, tt), :], axis=0, keepdims=True)


SUB = 8


def _fill_shifts(sh_ref, rows):
    for s in range(1, SUB):
        sh_ref[s, 0:rows, :] = sh_ref[0, pl.ds(s, rows), :]


def _tap(sh_ref, off, rows):
    return sh_ref[off % SUB, pl.ds(off - off % SUB, rows), :]


def _conv_fwd_sh(ext_sh, w_ref, kw, halo, tt):
    acc = None
    for j in range(kw):
        term = w_ref[j:j + 1, :] * _tap(ext_sh, halo - (kw - 1) + j, tt)
        acc = term if acc is None else acc + term
    return acc


def _conv_bwd_x_sh(dcp_sh, w_ref, kw, halo, tt):
    acc = None
    for j in range(kw):
        term = w_ref[j:j + 1, :] * _tap(dcp_sh, kw - 1 - j, tt + halo)
        acc = term if acc is None else acc + term
    return acc


def _conv_bwd_w_sh(dc, ext_sh, gw_ref, kw, halo, tt):
    for j in range(kw):
        gw_ref[j:j + 1, :] += jnp.sum(dc * _tap(ext_sh, halo - (kw - 1) + j, tt), axis=0, keepdims=True)


def _a_post(c, z, g, b):
    mu = jnp.mean(c, axis=-1, keepdims=True)
    var = jnp.mean(jnp.square(c - mu), axis=-1, keepdims=True)
    a = (c - mu) * lax.rsqrt(var + EPS) * g + b
    return jax.nn.silu(a) * jax.nn.silu(z)


def _a_fwd(proj, dw, b_row, lg, lb, bsz, t):
    n = bsz * t
    tt = _tile(t, 256)
    nt = t // tt
    cb = C_A // 512

    def body(v_ref, g_ref, z_ref, vh_ref, gh_ref, w_ref, b_ref, lg_ref, lb_ref, y_ref, ext):
        i = pl.program_id(1)
        ext[0, 0:A_H, :] = jnp.where(i > 0, vh_ref[...] * jax.nn.sigmoid(gh_ref[...]), 0.0)
        ext[0, A_H:, :] = v_ref[...] * jax.nn.sigmoid(g_ref[...])
        _fill_shifts(ext, tt + A_H - SUB)
        c = _conv_fwd_sh(ext, w_ref, A_K, A_H, tt) + b_ref[...]
        y_ref[...] = _a_post(c, z_ref[...], lg_ref[...], lb_ref[...])

    def row(b, i):
        return b * nt + i

    def halo(b, i):
        return jnp.maximum((b * t + i * tt) // A_H - 1, 0)

    return pl.pallas_call(
        body, name="a_fwd", grid=(bsz, nt),
        in_specs=[pl.BlockSpec((tt, 512), lambda b, i: (row(b, i), cb)),
                  pl.BlockSpec((tt, 512), lambda b, i: (row(b, i), cb + 1)),
                  pl.BlockSpec((tt, 512), lambda b, i: (row(b, i), cb + 2)),
                  pl.BlockSpec((A_H, 512), lambda b, i: (halo(b, i), cb)),
                  pl.BlockSpec((A_H, 512), lambda b, i: (halo(b, i), cb + 1)),
                  _full((32, 512)), _full((1, 512)), _full((1, 512)), _full((1, 512))],
        out_specs=pl.BlockSpec((tt, 512), lambda b, i: (row(b, i), 0)),
        out_shape=SDS((n, 512), F32),
        scratch_shapes=[pltpu.VMEM((SUB, tt + A_H, 512), F32)], compiler_params=_cp(40),
    )(proj, proj, proj, proj, proj, dw, b_row, lg, lb)


def _a_bwd(dproj, proj, dy, dw, b_row, lg, lb, bsz, t):
    n = bsz * t
    tt = _tile(t, 256)
    nt = t // tt
    cb = C_A // 512

    def body(dp_any, v_ref, g_ref, z_ref, vh_ref, gh_ref, dy_ref, w_ref, b_ref, lg_ref, lb_ref,
             dp_ref, gw_ref, gb_ref, glg_ref, glb_ref, ext, dcp, dae, carry):
        b, i = pl.program_id(0), pl.program_id(1)
        ti = nt - 1 - i

        @pl.when((b == 0) & (i == 0))
        def _():
            gw_ref[...] = jnp.zeros_like(gw_ref)
            gb_ref[...] = jnp.zeros_like(gb_ref)
            glg_ref[...] = jnp.zeros_like(glg_ref)
            glb_ref[...] = jnp.zeros_like(glb_ref)

        @pl.when(i == 0)
        def _():
            carry[...] = jnp.zeros_like(carry)

        val, glu = v_ref[...], g_ref[...]
        sg = jax.nn.sigmoid(glu)
        ext[0, 0:A_H, :] = jnp.where(ti > 0, vh_ref[...] * jax.nn.sigmoid(gh_ref[...]), 0.0)
        ext[0, A_H:, :] = val * sg
        _fill_shifts(ext, tt + A_H - SUB)
        c = _conv_fwd_sh(ext, w_ref, A_K, A_H, tt) + b_ref[...]
        _, vjp = jax.vjp(_a_post, c, z_ref[...], lg_ref[...], lb_ref[...])
        dc, dz, dlg, dlb = vjp(dy_ref[...])
        gb_ref[...] += jnp.sum(dc, axis=0, keepdims=True)
        glg_ref[...] += dlg
        glb_ref[...] += dlb
        dcp[0, 0:A_H, :] = jnp.zeros((A_H, 512), F32)
        dcp[0, A_H:A_H + tt, :] = dc
        dcp[0, A_H + tt:, :] = jnp.zeros((A_H, 512), F32)
        _fill_shifts(dcp, tt + 2 * A_H - SUB)
        _conv_bwd_w_sh(dc, ext, gw_ref, A_K, A_H, tt)
        dae[...] = _conv_bwd_x_sh(dcp, w_ref, A_K, A_H, tt)
        dae[tt:tt + A_H, :] += carry[...]
        carry[...] = dae[0:A_H, :]
        da = dae[A_H:, :]
        dp_ref[:, 0:512] = (da * sg).astype(BF16)
        dp_ref[:, 512:1024] = (da * val * sg * (1.0 - sg)).astype(BF16)
        dp_ref[:, 1024:1536] = dz.astype(BF16)

    def row(b, i):
        return b * nt + (nt - 1 - i)

    def halo(b, i):
        return jnp.maximum((b * t + (nt - 1 - i) * tt) // A_H - 1, 0)

    outs = pl.pallas_call(
        body, name="a_bwd", grid=(bsz, nt),
        in_specs=[pl.BlockSpec(memory_space=pl.ANY),
                  pl.BlockSpec((tt, 512), lambda b, i: (row(b, i), cb)),
                  pl.BlockSpec((tt, 512), lambda b, i: (row(b, i), cb + 1)),
                  pl.BlockSpec((tt, 512), lambda b, i: (row(b, i), cb + 2)),
                  pl.BlockSpec((A_H, 512), lambda b, i: (halo(b, i), cb)),
                  pl.BlockSpec((A_H, 512), lambda b, i: (halo(b, i), cb + 1)),
                  pl.BlockSpec((tt, 512), lambda b, i: (row(b, i), 0)),
                  _full((32, 512)), _full((1, 512)), _full((1, 512)), _full((1, 512))],
        out_specs=[pl.BlockSpec((tt, 1536), lambda b, i: (row(b, i), C_A // 1536)),
                   _full((32, 512)), _full((1, 512)), _full((1, 512)), _full((1, 512))],
        out_shape=[SDS((n, NP), BF16), SDS((32, 512), F32), SDS((1, 512), F32), SDS((1, 512), F32), SDS((1, 512), F32)],
        input_output_aliases={0: 0},
        scratch_shapes=[pltpu.VMEM((SUB, tt + A_H, 512), F32), pltpu.VMEM((SUB, tt + 2 * A_H, 512), F32),
                        pltpu.VMEM((tt + A_H, 512), F32), pltpu.VMEM((A_H, 512), F32)],
        compiler_params=_cp(48),
    )(dproj, proj, proj, proj, proj, proj, dy, dw, b_row, lg, lb)
    return outs


def _b_post(blocks):
    out = []
    for idx, c in enumerate(blocks):
        s = jax.nn.silu(c)
        if idx < 2 * NH:
            s = s * lax.rsqrt(jnp.sum(s * s, axis=-1, keepdims=True) + EPS)
            if idx < NH:
                s = s * (LANE ** -0.5)
        out.append(s)
    return out


def _bprep_fwd(proj, wconv, bsz, t):
    n = bsz * t
    tt = _tile(t, 256)
    nt = t // tt

    def body(x_ref, xh_ref, w_ref, o_ref, ext):
        i = pl.program_id(1)
        ext[0:B_H, :] = jnp.where(i > 0, xh_ref[...], 0.0)
        ext[B_H:, :] = x_ref[...]
        c = _conv_fwd(ext, w_ref, B_K, B_H, tt)
        outs = _b_post([c[:, LANE * j:LANE * (j + 1)] for j in range(3 * NH)])
        for j, o in enumerate(outs):
            o_ref[:, LANE * j:LANE * (j + 1)] = o

    return pl.pallas_call(
        body, name="bprep_fwd", grid=(bsz, nt),
        in_specs=[pl.BlockSpec((tt, 3072), lambda b, i: (b * nt + i, 0)),
                  pl.BlockSpec((B_H, 3072), lambda b, i: (jnp.maximum((b * t + i * tt) // B_H - 1, 0), 0)),
                  _full((8, 3072))],
        out_specs=pl.BlockSpec((tt, 3072), lambda b, i: (b * nt + i, 0)),
        out_shape=SDS((n, 3072), F32),
        scratch_shapes=[pltpu.VMEM((tt + B_H, 3072), F32)], compiler_params=_cp(48),
    )(proj, proj, wconv)


def _bprep_bwd(dproj, proj, dqkvn, wconv, bsz, t):
    n = bsz * t
    tt = _tile(t, 256)
    nt = t // tt

    def body(dp_any, x_ref, xh_ref, dq_ref, w_ref, dp_ref, gw_ref, ext, dcp, dae, carry):
        b, i = pl.program_id(0), pl.program_id(1)
        ti = nt - 1 - i

        @pl.when((b == 0) & (i == 0))
        def _():
            gw_ref[...] = jnp.zeros_like(gw_ref)

        @pl.when(i == 0)
        def _():
            carry[...] = jnp.zeros_like(carry)

        ext[0:B_H, :] = jnp.where(ti > 0, xh_ref[...], 0.0)
        ext[B_H:, :] = x_ref[...]
        c = _conv_fwd(ext, w_ref, B_K, B_H, tt)
        _, vjp = jax.vjp(_b_post, [c[:, LANE * j:LANE * (j + 1)] for j in range(3 * NH)])
        (dcs,) = vjp([dq_ref[:, LANE * j:LANE * (j + 1)] for j in range(3 * NH)])
        dcp[0:B_H, :] = jnp.zeros((B_H, 3072), F32)
        for j, dcj in enumerate(dcs):
            dcp[B_H:B_H + tt, LANE * j:LANE * (j + 1)] = dcj
        dcp[B_H + tt:, :] = jnp.zeros((B_H, 3072), F32)
        _conv_bwd_w(dcp[B_H:B_H + tt, :], ext, gw_ref, B_K, B_H, tt)
        dae[...] = _conv_bwd_x(dcp, w_ref, B_K, B_H, tt)
        dae[tt:tt + B_H, :] += carry[...]
        carry[...] = dae[0:B_H, :]
        dp_ref[...] = dae[B_H:, :].astype(BF16)

    def row(b, i):
        return b * nt + (nt - 1 - i)

    return pl.pallas_call(
        body, name="bprep_bwd", grid=(bsz, nt),
        in_specs=[pl.BlockSpec(memory_space=pl.ANY),
                  pl.BlockSpec((tt, 3072), lambda b, i: (row(b, i), 0)),
                  pl.BlockSpec((B_H, 3072), lambda b, i: (jnp.maximum((b * t + (nt - 1 - i) * tt) // B_H - 1, 0), 0)),
                  pl.BlockSpec((tt, 3072), lambda b, i: (row(b, i), 0)),
                  _full((8, 3072))],
        out_specs=[pl.BlockSpec((tt, 3072), lambda b, i: (row(b, i), 0)), _full((8, 3072))],
        out_shape=[SDS((n, NP), BF16), SDS((8, 3072), F32)],
        input_output_aliases={0: 0},
        scratch_shapes=[pltpu.VMEM((tt + B_H, 3072), F32), pltpu.VMEM((tt + 2 * B_H, 3072), F32),
                        pltpu.VMEM((tt + B_H, 3072), F32), pltpu.VMEM((B_H, 3072), F32)],
        compiler_params=_cp(56),
    )(dproj, proj, proj, dqkvn, wconv)


def _split2(a):
    hi = a.astype(BF16)
    return hi, (a - hi.astype(F32)).astype(BF16)


def _split3(a):
    p1 = a.astype(BF16)
    r1 = a - p1.astype(F32)
    p2 = r1.astype(BF16)
    return p1, p2, (r1 - p2.astype(F32)).astype(BF16)


def _dot3_raw(a, b, dims):
    a1, a2 = _split2(a)
    b1, b2 = _split2(b)
    return _mm(a1, b1, dims) + (_mm(a1, b2, dims) + _mm(a2, b1, dims))


def _dot6(a, b, dims):
    a1, a2, a3 = _split3(a)
    b1, b2, b3 = _split3(b)
    return (_mm(a1, b1, dims) + (_mm(a1, b2, dims) + _mm(a2, b1, dims))
            + (_mm(a1, b3, dims) + _mm(a2, b2, dims) + _mm(a3, b1, dims)))


def _unit_lower_inverse_raw(lmats):
    r = lax.broadcasted_iota(jnp.int32, (CH, CH), 0)
    c = lax.broadcasted_iota(jnp.int32, (CH, CH), 1)
    eye = (r == c).astype(F32)
    blk = jnp.right_shift(r, 4) == jnp.right_shift(c, 4)
    dm = [jnp.where(blk, x, 0.0) for x in lmats]
    om = [a - b for a, b in zip(lmats, dm)]
    d2 = [_dot3_raw(x, x, NN) for x in dm]
    d4 = [_dot3_raw(x, x, NN) for x in d2]
    d8 = [_dot3_raw(x, x, NN) for x in d4]
    p = [_dot3_raw(eye - a, eye + b, NN) for a, b in zip(dm, d2)]
    p = [_dot3_raw(a, eye + b, NN) for a, b in zip(p, d4)]
    p = [_dot3_raw(a, eye + b, NN) for a, b in zip(p, d8)]
    m = [_dot3_raw(a, b, NN) for a, b in zip(p, om)]
    m2 = [_dot3_raw(x, x, NN) for x in m]
    t = [_dot3_raw(eye - a, eye + b, NN) for a, b in zip(m, m2)]
    return [_dot3_raw(a, b, NN) for a, b in zip(t, p)]


@jax.custom_vjp
def _unit_lower_inverse(lmats):
    return _unit_lower_inverse_raw(lmats)


def _unit_lower_inverse_fwd(lmats):
    tinv = _unit_lower_inverse_raw(lmats)
    return tinv, tinv


def _unit_lower_inverse_bwd(tinv, gs):
    x = [_dot6(t, g, TN) for t, g in zip(tinv, gs)]
    return ([-_dot6(a, t, NT) for a, t in zip(x, tinv)],)


_unit_lower_inverse.defvjp(_unit_lower_inverse_fwd, _unit_lower_inverse_bwd)


@jax.custom_vjp
def _saved_unit_lower_inverse(lmats, tinv):
    del lmats
    return tinv


def _saved_unit_lower_inverse_fwd(lmats, tinv):
    del lmats
    return tinv, tinv


def _saved_unit_lower_inverse_bwd(tinv, gs):
    (dl,) = _unit_lower_inverse_bwd(tinv, gs)
    return dl, [jnp.zeros_like(t) for t in tinv]


_saved_unit_lower_inverse.defvjp(_saved_unit_lower_inverse_fwd, _saved_unit_lower_inverse_bwd)


def _tri(lower):
    r = lax.broadcasted_iota(jnp.int32, (CH, CH), 0)
    c = lax.broadcasted_iota(jnp.int32, (CH, CH), 1)
    return ((r >= c) if lower else (r <= c)).astype(BF16)


def _tri_dot(x, lower, dims, tri_first):
    p1, p2, p3 = _split3(x)
    tri = _tri(lower)
    if tri_first:
        return _mm(tri, p1, dims) + (_mm(tri, p2, dims) + _mm(tri, p3, dims))
    return _mm(p1, tri, dims) + (_mm(p2, tri, dims) + _mm(p3, tri, dims))


@jax.custom_vjp
def _cumsum_rows(x):
    return _tri_dot(x, True, NN, True)


def _cumsum_rows_fwd(x):
    return _tri_dot(x, True, NN, True), None


def _cumsum_rows_bwd(_, g):
    return (_tri_dot(g, True, TN, True),)


_cumsum_rows.defvjp(_cumsum_rows_fwd, _cumsum_rows_bwd)


@jax.custom_vjp
def _cumsum_rows_t(x):
    return _tri_dot(x, False, TN, False)


def _cumsum_rows_t_fwd(x):
    return _tri_dot(x, False, TN, False), None


def _cumsum_rows_t_bwd(_, g):
    return (_tri_dot(g, False, NT, True),)


_cumsum_rows_t.defvjp(_cumsum_rows_t_fwd, _cumsum_rows_t_bwd)


def _delta_chunk(ss, qs, ks, vs, ba, zs, alog, dtb, og, tsaved=None):
    heads = range(NH)
    lane = lax.broadcasted_iota(jnp.int32, (1, LANE), 1)
    r = lax.broadcasted_iota(jnp.int32, (CH, CH), 0)
    c = lax.broadcasted_iota(jnp.int32, (CH, CH), 1)
    ri = lax.broadcasted_iota(jnp.int32, (CH, 1), 0)
    incl, strict = r >= c, r > c

    def pick(x, h):
        return jnp.sum(jnp.where(lane == h, x, 0.0), axis=-1, keepdims=True)

    beta = [jax.nn.sigmoid(pick(ba, h)) for h in heads]
    g = [-jnp.exp(pick(alog, h + NH)) * _softplus(pick(ba, h + NH) + pick(dtb, h + NH)) for h in heads]
    gb = [jnp.broadcast_to(x, (CH, CH)) for x in g]
    gca = [_cumsum_rows(x) for x in gb]
    gcr = [_cumsum_rows_t(x) for x in gb]
    gc = [jnp.sum(jnp.where(c == 0, x, 0.0), axis=-1, keepdims=True) for x in gca]
    gl = [jnp.sum(jnp.where(ri == CH - 1, x, 0.0), axis=0, keepdims=True) for x in gc]
    diff = [a - b for a, b in zip(gca, gcr)]
    gam_s = [jnp.where(strict, jnp.exp(jnp.where(strict, x, 0.0)), 0.0) for x in diff]
    gam_i = [jnp.where(incl, jnp.exp(jnp.where(incl, x, 0.0)), 0.0) for x in diff]

    kk = [_mm(k, k, NT) for k in ks]
    lmats = [beta[h] * kk[h] * gam_s[h] for h in heads]
    tinv = _unit_lower_inverse(lmats) if tsaved is None else _saved_unit_lower_inverse(lmats, tsaved)

    eg = [jnp.exp(x) for x in gc]
    u = [_mm(tinv[h], vs[h] * beta[h], NN) for h in heads]
    w = [_mm(tinv[h], ks[h] * (beta[h] * eg[h]), NN) for h in heads]
    qk = [_mm(qs[h], ks[h], NT) * gam_i[h] for h in heads]
    vn = [u[h] - _mm(w[h], ss[h], NN) for h in heads]
    o = [_mm(qs[h] * eg[h], ss[h], NN) + _mm(qk[h], vn[h], NN) for h in heads]
    sn = [jnp.exp(gl[h]) * ss[h] + _mm(ks[h] * jnp.exp(gl[h] - gc[h]), vn[h], TN) for h in heads]
    y = [_rms(o[h], og) * jax.nn.silu(zs[h]) for h in heads]
    return (sn, y), tinv


def _head_blocks(ref, base=0):
    return [ref[:, base + LANE * h:base + LANE * (h + 1)] for h in range(NH)]


def _ride_along(ex, first, last):
    if ex is None:
        return

    @pl.when(first)
    def _():
        ex.start()

    @pl.when(last)
    def _():
        ex.wait()


def _delta_fwd(qkvn, proj, alog, dtb, og, bsz, t, gather=()):
    n = bsz * t
    nc = t // CH
    ng = len(gather)

    def body(q_ref, k_ref, v_ref, ba_ref, z_ref, al_ref, dt_ref, og_ref, *rest):
        g_in, (y_ref, sh_ref, ti_ref) = rest[:ng], rest[ng:ng + 3]
        g_out, (s_scr, *sems) = rest[ng + 3:2 * ng + 3], rest[2 * ng + 3:]
        b, ci = pl.program_id(0), pl.program_id(1)
        _ride_along(_Exchange(False, g_in, g_out, *sems) if ng else None,
                    (b == 0) & (ci == 0), (b == bsz - 1) & (ci == nc - 1))

        @pl.when(ci == 0)
        def _():
            s_scr[...] = jnp.zeros_like(s_scr)

        ss = [s_scr[h] for h in range(NH)]
        for h in range(NH):
            sh_ref[h] = ss[h]
        (sn, y), tinv = _delta_chunk(ss, _head_blocks(q_ref), _head_blocks(k_ref), _head_blocks(v_ref), ba_ref[...],
                                     _head_blocks(z_ref), al_ref[...], dt_ref[...], og_ref[...])
        for h in range(NH):
            s_scr[h] = sn[h]
            ti_ref[h] = tinv[h]
            y_ref[:, LANE * h:LANE * (h + 1)] = y[h]

    def blk(width, col):
        return pl.BlockSpec((CH, width), lambda b, ci: (b * nc + ci, col))

    hbm = pl.BlockSpec(memory_space=pltpu.HBM)
    return pl.pallas_call(
        body, name="delta_fwd_gather" if ng else "delta_fwd", grid=(bsz, nc),
        in_specs=[blk(D, 0), blk(D, 1), blk(D, 2), blk(LANE, C_BA // LANE), blk(D, C_BZ // D),
                  _full((1, LANE)), _full((1, LANE)), _full((1, LANE))] + [hbm] * ng,
        out_specs=[blk(D, 0), pl.BlockSpec((None, None, NH, LANE, LANE), lambda b, ci: (b, ci, 0, 0, 0)),
                   pl.BlockSpec((None, None, NH, CH, CH), lambda b, ci: (b, ci, 0, 0, 0))] + [hbm] * ng,
        out_shape=[SDS((n, D), F32), SDS((bsz, nc, NH, LANE, LANE), F32), SDS((bsz, nc, NH, CH, CH), F32)]
        + [SDS((NDEV,) + a.shape, a.dtype) for a in gather],
        scratch_shapes=[pltpu.VMEM((NH, LANE, LANE), F32)] + (_exchange_sems(ng) if ng else []),
        compiler_params=_cp(40),
    )(qkvn, qkvn, qkvn, proj, proj, alog, dtb, og, *gather)


def _delta_bwd(dproj, qkvn, proj, shist, tsaved, dyb, alog, dtb, og, bsz, t, scatter=()):
    n = bsz * t
    nc = t // CH
    ns = len(scatter)

    def body(dp_any, q_ref, k_ref, v_ref, ba_ref, z_ref, sh_ref, ti_ref, dy_ref, al_ref, dt_ref, og_ref, *rest):
        s_in, (dp_ref, dqkv_ref, gal_ref, gdt_ref, gog_ref) = rest[:ns], rest[ns:ns + 5]
        s_out, (ds_scr, *sems) = rest[ns + 5:2 * ns + 5], rest[2 * ns + 5:]
        b, ci = pl.program_id(0), pl.program_id(1)
        _ride_along(_Exchange(True, s_in, s_out, *sems) if ns else None,
                    (b == 0) & (ci == 0), (b == bsz - 1) & (ci == nc - 1))

        @pl.when((b == 0) & (ci == 0))
        def _():
            gal_ref[...] = jnp.zeros_like(gal_ref)
            gdt_ref[...] = jnp.zeros_like(gdt_ref)
            gog_ref[...] = jnp.zeros_like(gog_ref)

        @pl.when(ci == 0)
        def _():
            ds_scr[...] = jnp.zeros_like(ds_scr)

        _, vjp, _ = jax.vjp(_delta_chunk, [sh_ref[h] for h in range(NH)], _head_blocks(q_ref), _head_blocks(k_ref),
                            _head_blocks(v_ref), ba_ref[...], _head_blocks(z_ref), al_ref[...], dt_ref[...], og_ref[...],
                            [ti_ref[h] for h in range(NH)], has_aux=True)
        ds, dq, dk, dv, dba, dz, dal, ddt, dog, _ = vjp(([ds_scr[h] for h in range(NH)], _head_blocks(dy_ref)))
        gal_ref[...] += dal
        gdt_ref[...] += ddt
        gog_ref[...] += dog
        dp_ref[:, D:D + LANE] = dba.astype(BF16)
        for h in range(NH):
            ds_scr[h] = ds[h]
            dp_ref[:, LANE * h:LANE * (h + 1)] = dz[h].astype(BF16)
            dqkv_ref[:, LANE * h:LANE * (h + 1)] = dq[h]
            dqkv_ref[:, D + LANE * h:D + LANE * (h + 1)] = dk[h]
            dqkv_ref[:, 2 * D + LANE * h:2 * D + LANE * (h + 1)] = dv[h]

    def blk(width, col):
        return pl.BlockSpec((CH, width), lambda b, ci: (b * nc + (nc - 1 - ci), col))

    hbm = pl.BlockSpec(memory_space=pltpu.HBM)
    return pl.pallas_call(
        body, name="delta_bwd_exchange" if ns else "delta_bwd", grid=(bsz, nc),
        in_specs=[pl.BlockSpec(memory_space=pl.ANY), blk(D, 0), blk(D, 1), blk(D, 2), blk(LANE, C_BA // LANE),
                  blk(D, C_BZ // D),
                  pl.BlockSpec((None, None, NH, LANE, LANE), lambda b, ci: (b, nc - 1 - ci, 0, 0, 0)),
                  pl.BlockSpec((None, None, NH, CH, CH), lambda b, ci: (b, nc - 1 - ci, 0, 0, 0)),
                  blk(D, 0), _full((1, LANE)), _full((1, LANE)), _full((1, LANE))] + [hbm] * ns,
        out_specs=[blk(D + LANE, C_BZ // (D + LANE)), blk(3 * D, 0), _full((1, LANE)), _full((1, LANE)),
                   _full((1, LANE))] + [hbm] * ns,
        out_shape=[SDS((n, NP), BF16), SDS((n, 3 * D), F32), SDS((1, LANE), F32), SDS((1, LANE), F32),
                   SDS((1, LANE), F32)] + [SDS(a.shape, a.dtype) for a in scatter],
        input_output_aliases={0: 0},
        scratch_shapes=[pltpu.VMEM((NH, LANE, LANE), F32)] + (_exchange_sems(ns) if ns else []),
        compiler_params=_cp(48),
    )(dproj, qkvn, qkvn, qkvn, proj, proj, shist, tsaved, dyb, alog, dtb, og, *scatter)


def _c_chunk(us, vs, zs, lgs, lbs, ws, bsb):
    gv = [jax.nn.gelu(v) for v in vs]
    width = LANE * len(gv)
    mu = sum(jnp.sum(x, axis=-1, keepdims=True) for x in gv) / width
    var = sum(jnp.sum(jnp.square(x - mu), axis=-1, keepdims=True) for x in gv) / width
    rstd = lax.rsqrt(var + EPS)
    r = lax.broadcasted_iota(jnp.int32, (SG, SG), 0)
    c = lax.broadcasted_iota(jnp.int32, (SG, SG), 1)
    out = []
    for j in range(len(gv)):
        nrm = (gv[j] - mu) * rstd * lgs[j] + lbs[j]
        mixed = jnp.dot(jnp.where(r >= c, ws[j], 0.0), nrm, preferred_element_type=F32) + bsb[j]
        out.append(jax.nn.gelu(us[j]) * mixed * jax.nn.silu(zs[j]))
    return out


def _c_args(u_ref, v_ref, z_ref, lg_ref, lb_ref, ws_ref, bs_ref):
    sl = [slice(LANE * j, LANE * (j + 1)) for j in range(4)]
    return ([u_ref[:, s] for s in sl], [v_ref[:, s] for s in sl], [z_ref[:, s] for s in sl],
            [lg_ref[:, s] for s in sl], [lb_ref[:, s] for s in sl],
            [ws_ref[j] for j in range(4)], [bs_ref[j] for j in range(4)])


def _c_fwd(proj, lg, lb, ws, bsb, n):
    cb = C_C // 512

    def body(u_ref, v_ref, z_ref, lg_ref, lb_ref, ws_ref, bs_ref, y_ref):
        outs = _c_chunk(*_c_args(u_ref, v_ref, z_ref, lg_ref, lb_ref, ws_ref, bs_ref))
        for j, o in enumerate(outs):
            y_ref[:, LANE * j:LANE * (j + 1)] = o

    return pl.pallas_call(
        body, name="c_fwd", grid=(n // SG,),
        in_specs=[pl.BlockSpec((SG, 512), lambda i: (i, cb)), pl.BlockSpec((SG, 512), lambda i: (i, cb + 1)),
                  pl.BlockSpec((SG, 512), lambda i: (i, cb + 2)), _full((1, 512)), _full((1, 512)),
                  _full((4, SG, SG)), _full((4, SG, SG))],
        out_specs=pl.BlockSpec((SG, 512), lambda i: (i, 0)),
        out_shape=SDS((n, 512), F32), compiler_params=_cp(32),
    )(proj, proj, proj, lg, lb, ws, bsb)


def _c_bwd(dproj, proj, dy, lg, lb, ws, bsb, n):
    cb = C_C // 512

    def body(dp_any, u_ref, v_ref, z_ref, dy_ref, lg_ref, lb_ref, ws_ref, bs_ref,
             dp_ref, glg_ref, glb_ref, gws_ref, gbs_ref):
        @pl.when(pl.program_id(0) == 0)
        def _():
            glg_ref[...] = jnp.zeros_like(glg_ref)
            glb_ref[...] = jnp.zeros_like(glb_ref)
            gws_ref[...] = jnp.zeros_like(gws_ref)
            gbs_ref[...] = jnp.zeros_like(gbs_ref)

        _, vjp = jax.vjp(_c_chunk, *_c_args(u_ref, v_ref, z_ref, lg_ref, lb_ref, ws_ref, bs_ref))
        dus, dvs, dzs, dlgs, dlbs, dwss, dbss = vjp([dy_ref[:, LANE * j:LANE * (j + 1)] for j in range(4)])
        for j in range(4):
            sl = slice(LANE * j, LANE * (j + 1))
            dp_ref[:, LANE * j:LANE * (j + 1)] = dus[j].astype(BF16)
            dp_ref[:, 512 + LANE * j:512 + LANE * (j + 1)] = dvs[j].astype(BF16)
            dp_ref[:, 1024 + LANE * j:1024 + LANE * (j + 1)] = dzs[j].astype(BF16)
            glg_ref[:, sl] += dlgs[j]
            glb_ref[:, sl] += dlbs[j]
            gws_ref[j] += dwss[j]
            gbs_ref[j] += jnp.broadcast_to(jnp.sum(dbss[j], axis=-1, keepdims=True), (SG, SG))

    return pl.pallas_call(
        body, name="c_bwd", grid=(n // SG,),
        in_specs=[pl.BlockSpec(memory_space=pl.ANY),
                  pl.BlockSpec((SG, 512), lambda i: (i, cb)), pl.BlockSpec((SG, 512), lambda i: (i, cb + 1)),
                  pl.BlockSpec((SG, 512), lambda i: (i, cb + 2)), pl.BlockSpec((SG, 512), lambda i: (i, 0)),
                  _full((1, 512)), _full((1, 512)), _full((4, SG, SG)), _full((4, SG, SG))],
        out_specs=[pl.BlockSpec((SG, 1536), lambda i: (i, C_C // 1536)),
                   _full((1, 512)), _full((1, 512)), _full((4, SG, SG)), _full((4, SG, SG))],
        out_shape=[SDS((n, NP), BF16), SDS((1, 512), F32), SDS((1, 512), F32), SDS((4, SG, SG), F32), SDS((4, SG, SG), F32)],
        input_output_aliases={0: 0}, compiler_params=_cp(32),
    )(dproj, proj, proj, proj, dy, lg, lb, ws, bsb)


def _merge_fwd(x2d, ya, yb, yc, proj, ap, bp, cp, wo):
    n = x2d.shape[0]
    tm = _tile(n, 256)
    gb = C_G // D

    def body(x_ref, ya_ref, yb_ref, yc_ref, g0_ref, g1_ref, g2_ref, ap_ref, bp_ref, cp_ref, wo_ref, o_ref):
        merged = (jax.nn.sigmoid(g0_ref[...]) * _bdot(ya_ref[...], ap_ref[...])
                  + jax.nn.sigmoid(g1_ref[...]) * _bdot(yb_ref[...], bp_ref[...])
                  + jax.nn.sigmoid(g2_ref[...]) * _bdot(yc_ref[...], cp_ref[...]))
        o_ref[...] = x_ref[...] + _bdot(merged, wo_ref[...])

    def rows(w):
        return pl.BlockSpec((tm, w), lambda i: (i, 0))

    return pl.pallas_call(
        body, name="merge_fwd", grid=(n // tm,),
        in_specs=[rows(D), rows(512), rows(D), rows(512),
                  pl.BlockSpec((tm, D), lambda i: (i, gb)), pl.BlockSpec((tm, D), lambda i: (i, gb + 1)),
                  pl.BlockSpec((tm, D), lambda i: (i, gb + 2)),
                  _full((512, D)), _full((D, D)), _full((512, D)), _full((D, D))],
        out_specs=rows(D), out_shape=SDS((n, D), F32), compiler_params=_cp(48),
    )(x2d, ya, yb, yc, proj, proj, proj, ap, bp, cp, wo)


def _merge_bwd(dxo, ya, yb, yc, proj, ap, bp, cp, apt, bpt, cpt, wot):
    n = dxo.shape[0]
    tm = _tile(n, 128)
    gb = C_G // D

    def body(d_ref, ya_ref, yb_ref, yc_ref, g0_ref, g1_ref, g2_ref, ap_ref, bp_ref, cp_ref,
             apt_ref, bpt_ref, cpt_ref, wot_ref,
             dp_ref, dya_ref, dyb_ref, dyc_ref, dpa_ref, dpb_ref, dpc_ref, mg_ref):
        dm = _bdot(d_ref[...], wot_ref[...])
        merged = None
        for j, (g_ref, y_ref, w_ref, wt_ref, dy_ref, dpj_ref) in enumerate((
                (g0_ref, ya_ref, ap_ref, apt_ref, dya_ref, dpa_ref),
                (g1_ref, yb_ref, bp_ref, bpt_ref, dyb_ref, dpb_ref),
                (g2_ref, yc_ref, cp_ref, cpt_ref, dyc_ref, dpc_ref))):
            s = jax.nn.sigmoid(g_ref[...])
            pj = _bdot(y_ref[...], w_ref[...])
            merged = s * pj if merged is None else merged + s * pj
            dp_ref[:, D * j:D * (j + 1)] = (dm * pj * s * (1.0 - s)).astype(BF16)
            dpj = (dm * s).astype(BF16)
            dpj_ref[...] = dpj
            dy_ref[...] = jnp.dot(dpj, wt_ref[...], preferred_element_type=F32)
        mg_ref[...] = merged

    def rows(w):
        return pl.BlockSpec((tm, w), lambda i: (i, 0))

    return pl.pallas_call(
        body, name="merge_bwd", grid=(n // tm,),
        in_specs=[rows(D), rows(512), rows(D), rows(512),
                  pl.BlockSpec((tm, D), lambda i: (i, gb)), pl.BlockSpec((tm, D), lambda i: (i, gb + 1)),
                  pl.BlockSpec((tm, D), lambda i: (i, gb + 2)),
                  _full((512, D)), _full((D, D)), _full((512, D)),
                  _full((D, 512)), _full((D, D)), _full((D, 512)), _full((D, D))],
        out_specs=[pl.BlockSpec((tm, 3 * D), lambda i: (i, C_G // (3 * D))), rows(512), rows(D), rows(512),
                   rows(D), rows(D), rows(D), rows(D)],
        out_shape=[SDS((n, NP), BF16), SDS((n, 512), F32), SDS((n, D), F32), SDS((n, 512), F32),
                   SDS((n, D), BF16), SDS((n, D), BF16), SDS((n, D), BF16), SDS((n, D), F32)],
        compiler_params=_cp(56),
    )(dxo, ya, yb, yc, proj, proj, proj, ap, bp, cp, apt, bpt, cpt, wot)


def _reduce_adamw(parts, w, m, v, name):
    r, c = w.shape
    tr = _tile(r, 128)
    c1 = 1.0 - ADAM_B1 ** ADAM_STEP
    c2 = 1.0 - ADAM_B2 ** ADAM_STEP

    def body(p_ref, w_ref, m_ref, v_ref, g_ref, d_ref, nm_ref, nv_ref):
        g = p_ref[0].astype(F32)
        for s in range(1, NDEV):
            g = g + p_ref[s].astype(F32)
        nm = ADAM_B1 * m_ref[...] + (1.0 - ADAM_B1) * g
        nv = ADAM_B2 * v_ref[...] + (1.0 - ADAM_B2) * jnp.square(g)
        g_ref[...] = g
        nm_ref[...] = nm
        nv_ref[...] = nv
        d_ref[...] = -ADAM_LR * ((nm / c1) / (jnp.sqrt(nv / c2) + ADAM_EPS) + ADAM_WD * w_ref[...])

    blk = pl.BlockSpec((tr, c), lambda i: (i, 0))
    return pl.pallas_call(
        body, name=name, grid=(r // tr,),
        in_specs=[pl.BlockSpec((NDEV, tr, c), lambda i: (0, i, 0)), blk, blk, blk],
        out_specs=[blk, blk, blk, blk], out_shape=[SDS((r, c), F32)] * 4, compiler_params=_cp(48),
    )(parts, w, m, v)


def _leaf_rows(shape):
    size = 1
    for s in shape:
        size *= s
    return -(-size // (8 * PACK_W)) * 8, size


def _pack(arrs, mult, dtype, lead=()):
    parts = []
    for a in arrs:
        rows, size = _leaf_rows(a.shape[len(lead):])
        flat = a.reshape(lead + (size,)).astype(dtype)
        flat = jnp.pad(flat, [(0, 0)] * len(lead) + [(0, rows * PACK_W - size)])
        parts.append(flat.reshape(lead + (rows, PACK_W)))
    buf = jnp.concatenate(parts, axis=len(lead))
    rows = buf.shape[len(lead)]
    total = -(-rows // mult) * mult
    return jnp.pad(buf, [(0, 0)] * len(lead) + [(0, total - rows), (0, 0)])


def _unpack(buf, shapes, lead=()):
    out, off = [], 0
    for shp in shapes:
        rows, size = _leaf_rows(shp)
        part = buf[..., off:off + rows, :].reshape(lead + (rows * PACK_W,))
        out.append(part[..., :size].reshape(lead + tuple(shp)))
        off += rows
    return out


def _unshard(name, g):
    if name in ROW_SHARDED:
        return g.reshape(g.shape[0] * g.shape[1], g.shape[2])
    g = jnp.moveaxis(g, 0, 1)
    return g.reshape(g.shape[0], g.shape[1] * g.shape[2])


def _reshard(name, full):
    r, c = full.shape
    if name in ROW_SHARDED:
        return full.reshape(NDEV, r // NDEV, c)
    return jnp.moveaxis(full.reshape(r, NDEV, c // NDEV), 1, 0)


def _w_in_to_padded(slabs):
    pieces = []
    for lo, hi, _ in sorted(SEGMENTS, key=lambda s: s[2]):
        for d in range(NDEV):
            a, b = max(lo, d * W_SHARD), min(hi, (d + 1) * W_SHARD)
            if a < b:
                pieces.append(slabs[d, :, a - d * W_SHARD:b - d * W_SHARD])
    pieces.append(jnp.zeros(slabs.shape[1:2] + (NP - C_BA - 16,), slabs.dtype))
    return jnp.concatenate(pieces, axis=-1)


def _w_in_from_padded(g):
    slabs = []
    for d in range(NDEV):
        pieces = []
        for lo, hi, pstart in SEGMENTS:
            a, b = max(lo, d * W_SHARD), min(hi, (d + 1) * W_SHARD)
            if a < b:
                pieces.append(g[:, pstart + a - lo:pstart + b - lo])
        pieces.append(jnp.zeros(g.shape[:1] + (W_SHARD_PAD - W_SHARD,), g.dtype))
        slabs.append(jnp.concatenate(pieces, axis=-1))
    return jnp.stack(slabs)


def _pad_w_in(w):
    return jnp.pad(w, ((0, 0), (0, W_SHARD_PAD - W_SHARD)))


def _lane_row(vec8, offset):
    return jnp.pad(vec8, (offset, LANE - NH - offset))[None]


def kernel(x, norm_g, w_in, a_dw, a_dw_b, a_ln_g, a_ln_b, a_proj, b_conv, b_a_log, b_dt_bias, b_onorm_g, b_proj, c_ln_g, c_ln_b, c_ws, c_bs, c_proj, w_out, final_g, loss_target, m_norm_g, m_w_in, m_a_dw, m_a_dw_b, m_a_ln_g, m_a_ln_b, m_a_proj, m_b_conv, m_b_a_log, m_b_dt_bias, m_b_onorm_g, m_b_proj, m_c_ln_g, m_c_ln_b, m_c_ws, m_c_bs, m_c_proj, m_w_out, m_final_g, v_norm_g, v_w_in, v_a_dw, v_a_dw_b, v_a_ln_g, v_a_ln_b, v_a_proj, v_b_conv, v_b_a_log, v_b_dt_bias, v_b_onorm_g, v_b_proj, v_c_ln_g, v_c_ln_b, v_c_ws, v_c_bs, v_c_proj, v_w_out, v_final_g):
    wts = dict(norm_g=norm_g, w_in=w_in, a_dw=a_dw, a_dw_b=a_dw_b, a_ln_g=a_ln_g, a_ln_b=a_ln_b, a_proj=a_proj,
               b_conv=b_conv, b_a_log=b_a_log, b_dt_bias=b_dt_bias, b_onorm_g=b_onorm_g, b_proj=b_proj,
               c_ln_g=c_ln_g, c_ln_b=c_ln_b, c_ws=c_ws, c_bs=c_bs, c_proj=c_proj, w_out=w_out, final_g=final_g)
    mom = dict(norm_g=m_norm_g, w_in=m_w_in, a_dw=m_a_dw, a_dw_b=m_a_dw_b, a_ln_g=m_a_ln_g, a_ln_b=m_a_ln_b,
               a_proj=m_a_proj, b_conv=m_b_conv, b_a_log=m_b_a_log, b_dt_bias=m_b_dt_bias, b_onorm_g=m_b_onorm_g,
               b_proj=m_b_proj, c_ln_g=m_c_ln_g, c_ln_b=m_c_ln_b, c_ws=m_c_ws, c_bs=m_c_bs, c_proj=m_c_proj,
               w_out=m_w_out, final_g=m_final_g)
    vel = dict(norm_g=v_norm_g, w_in=v_w_in, a_dw=v_a_dw, a_dw_b=v_a_dw_b, a_ln_g=v_a_ln_g, a_ln_b=v_a_ln_b,
               a_proj=v_a_proj, b_conv=v_b_conv, b_a_log=v_b_a_log, b_dt_bias=v_b_dt_bias, b_onorm_g=v_b_onorm_g,
               b_proj=v_b_proj, c_ln_g=v_c_ln_g, c_ln_b=v_c_ln_b, c_ws=v_c_ws, c_bs=v_c_bs, c_proj=v_c_proj,
               w_out=v_w_out, final_g=v_final_g)

    bsz, t, _ = x.shape
    n = bsz * t
    depth = norm_g.shape[0]
    x2d = x.reshape(n, D)
    tgt = loss_target.reshape(n, D)

    def weight_blocks(l):
        return [_pad_w_in(w_in[l].astype(BF16)), _pack([wts[k][l] for k in BIG_REST], 16, BF16)]

    def matmul_weights(w_in_all, big_all, l):
        got = {k: _unshard(k, g) for k, g in
               zip(BIG_REST, _unpack(big_all, [wts[k][l].shape for k in BIG_REST], lead=(NDEV,)))}
        got['wp'] = _w_in_to_padded(w_in_all)
        return got

    (small_all,) = _all_gather([_pack([wts[k] for k in SMALL], 8, F32)], "gather_conv_weights")
    conv_w = {k: jnp.stack([_unshard(k, g[:, l]) for l in range(depth)]) for k, g in
              zip(SMALL, _unpack(small_all, [wts[k].shape for k in SMALL], lead=(NDEV,)))}
    a_dw32 = jnp.pad(conv_w['a_dw'], ((0, 0), (0, 32 - A_K), (0, 0)))
    b_conv8 = jnp.pad(conv_w['b_conv'], ((0, 0), (0, 8 - B_K), (0, 0)))
    bsb = jnp.broadcast_to(c_bs[..., None], c_bs.shape + (SG,))
    full = [matmul_weights(*_all_gather(weight_blocks(0), "gather_matmul_weights"), 0)]

    saved = []
    xl = x2d
    for l in range(depth):
        alog, dtb = _lane_row(b_a_log[l], NH), _lane_row(b_dt_bias[l], NH)
        proj, h = _inproj(xl, norm_g[l][None], full[l]['wp'])
        ya = _a_fwd(proj, a_dw32[l], a_dw_b[l][None], a_ln_g[l][None], a_ln_b[l][None], bsz, t)
        qkvn = _bprep_fwd(proj, b_conv8[l], bsz, t)
        yb, shist, tsave, *nxt = _delta_fwd(qkvn, proj, alog, dtb, b_onorm_g[l][None], bsz, t,
                                            gather=weight_blocks(l + 1) if l + 1 < depth else ())
        if nxt:
            full.append(matmul_weights(*nxt, l + 1))
        yc = _c_fwd(proj, c_ln_g[l][None], c_ln_b[l][None], c_ws[l], bsb[l], n)
        xn = _merge_fwd(xl, ya, yb, yc, proj, full[l]['a_proj'], full[l]['b_proj'], full[l]['c_proj'], full[l]['w_out'])
        saved.append((xl, proj, h, ya, yb, yc, qkvn, shist, tsave, alog, dtb))
        xl = xn

    dx, g_final, loss_blk = _loss_head(xl, final_g[None], tgt)
    loss = lax.psum(loss_blk[0, 0], ("x", "y", "c"))

    gfull = {k: [None] * depth for k in WEIGHTS if k != 'final_g'}
    recv = [None] * depth

    def grad_slabs(l):
        return [_w_in_from_padded(gfull['w_in'][l]).astype(BF16),
                _pack([_reshard(k, gfull[k][l]) for k in SHARDED_REST], 128, BF16, lead=(NDEV,))]

    for l in reversed(range(depth)):
        xl, proj, h, ya, yb, yc, qkvn, shist, tsave, alog, dtb = saved[l]
        ap, bp, cp, wo = full[l]['a_proj'], full[l]['b_proj'], full[l]['c_proj'], full[l]['w_out']
        dproj, dya, dyb, dyc, dpa, dpb, dpc, merged = _merge_bwd(dx, ya, yb, yc, proj, ap, bp, cp, ap.T, bp.T, cp.T, wo.T)
        gfull['a_proj'][l] = _mm_tn(ya, dpa, "grad_a_proj")
        gfull['b_proj'][l] = _mm_tn(yb, dpb, "grad_b_proj")
        gfull['c_proj'][l] = _mm_tn(yc, dpc, "grad_c_proj")
        gfull['w_out'][l] = _mm_tn(merged, dx, "grad_w_out")
        dproj, g_clg, g_clb, g_cws, g_cbs = _c_bwd(dproj, proj, dyc, c_ln_g[l][None], c_ln_b[l][None], c_ws[l], bsb[l], n)
        dproj, g_adw, g_adb, g_alg, g_alb = _a_bwd(dproj, proj, dya, a_dw32[l], a_dw_b[l][None], a_ln_g[l][None],
                                                   a_ln_b[l][None], bsz, t)
        dproj, dqkvn, g_alog, g_dt, g_og, *got = _delta_bwd(dproj, qkvn, proj, shist, tsave, dyb, alog, dtb, b_onorm_g[l][None],
                                                            bsz, t, scatter=grad_slabs(l + 1) if l + 1 < depth else ())
        if got:
            recv[l + 1] = got
        dproj, g_bconv = _bprep_bwd(dproj, proj, dqkvn, b_conv8[l], bsz, t)
        gfull['w_in'][l] = _mm_tn(h, dproj, "grad_w_in")
        gfull['a_dw'][l] = g_adw[:A_K]
        gfull['b_conv'][l] = g_bconv[:B_K]
        dx, g_ng, *got = _inproj_bwd(dproj, full[l]['wp'], xl, norm_g[l][None], dx, scatter=grad_slabs(0) if l == 0 else ())
        if got:
            recv[0] = got
        gfull['norm_g'][l] = g_ng[0]
        gfull['a_dw_b'][l], gfull['a_ln_g'][l], gfull['a_ln_b'][l] = g_adb[0], g_alg[0], g_alb[0]
        gfull['b_a_log'][l], gfull['b_dt_bias'][l] = g_alog[0, NH:2 * NH], g_dt[0, NH:2 * NH]
        gfull['b_onorm_g'][l] = g_og[0]
        gfull['c_ln_g'][l], gfull['c_ln_b'][l] = g_clg[0], g_clb[0]
        gfull['c_ws'][l], gfull['c_bs'][l] = g_cws, g_cbs[:, :, 0]
    grad_x = dx.reshape(bsz, t, D)

    outs_w, outs_s = [], []
    for l in range(depth):
        recv_w, recv_r = recv[l]
        outs_w.append(_reduce_adamw(recv_w, _pad_w_in(w_in[l]), _pad_w_in(m_w_in[l]), _pad_w_in(v_w_in[l]), "adamw_w_in"))
        outs_s.append(_reduce_adamw(recv_r, _pack([wts[k][l] for k in SHARDED_REST], 128, F32),
                                    _pack([mom[k][l] for k in SHARDED_REST], 128, F32),
                                    _pack([vel[k][l] for k in SHARDED_REST], 128, F32), "adamw_sharded"))
    grepl = {k: (jnp.stack(gfull[k]) if k != 'final_g' else g_final[0]) for k in REPL}
    (parts_r,) = _all_gather([_pack([grepl[k] for k in REPL], 8, F32)], "gather_replicated_grads")
    outs_r = _reduce_adamw(parts_r, _pack([wts[k] for k in REPL], 8, F32), _pack([mom[k] for k in REPL], 8, F32),
                           _pack([vel[k] for k in REPL], 8, F32), "adamw_replicated")

    res = []
    for j in range(4):
        leaves = dict(zip(REPL, _unpack(outs_r[j], [wts[k].shape for k in REPL])))
        per_layer = [_unpack(outs_s[l][j], [wts[k][l].shape for k in SHARDED_REST]) for l in range(depth)]
        for i, k in enumerate(SHARDED_REST):
            leaves[k] = jnp.stack([per_layer[l][i] for l in range(depth)])
        leaves['w_in'] = jnp.stack([outs_w[l][j][:, :W_SHARD] for l in range(depth)])
        res.append([leaves[k] for k in WEIGHTS])
    grads, deltas, new_m, new_v = res
    return (loss, grad_x, *grads, *deltas, *new_m, *new_v)
```

```python
import jax
import jax.numpy as jnp
from jax import lax
from jax.experimental import pallas as pl
from jax.experimental.pallas import tpu as pltpu

F32 = jnp.float32
BF16 = jnp.bfloat16
SDS = jax.ShapeDtypeStruct
MESH = pl.DeviceIdType.MESH

NDEV = 8
D = 1024
EPS = 1e-6
LANE = 128

C_Q, C_K, C_V = 0, 1024, 2048
C_G = 3072
C_A = 6144
C_C = 7680
C_BZ = 9216
C_BA = 10240
NP = 10368
N_IN = 10256
SEGMENTS = ((0, 1536, C_A), (1536, 4608, C_Q), (4608, 5632, C_BZ), (5632, 5648, C_BA), (5648, 7184, C_C), (7184, 10256, C_G))
W_SHARD = N_IN // NDEV
W_SHARD_PAD = 1408

A_K, A_H = 31, 32
B_K, B_H = 4, 8
CH = 64
SG = 128
NH = 8

ADAM_LR, ADAM_B1, ADAM_B2, ADAM_EPS, ADAM_WD, ADAM_STEP = 0.001, 0.9, 0.999, 1e-08, 0.01, 10

WEIGHTS = ['norm_g', 'w_in', 'a_dw', 'a_dw_b', 'a_ln_g', 'a_ln_b', 'a_proj', 'b_conv', 'b_a_log', 'b_dt_bias',
           'b_onorm_g', 'b_proj', 'c_ln_g', 'c_ln_b', 'c_ws', 'c_bs', 'c_proj', 'w_out', 'final_g']
SHARDED_REST = ['a_dw', 'a_proj', 'b_conv', 'b_proj', 'c_proj', 'w_out']
REPL = [n for n in WEIGHTS if n != 'w_in' and n not in SHARDED_REST]
BIG_REST = ['a_proj', 'b_proj', 'c_proj', 'w_out']
SMALL = ['a_dw', 'b_conv']
ROW_SHARDED = ('b_proj', 'w_out')

NN = ((1,), (0,))
NT = ((1,), (1,))
TN = ((0,), (0,))


def _tile(n, pref):
    return pref if (n >= pref and n % pref == 0) else n


def _cp(vmem_mb):
    return pltpu.CompilerParams(vmem_limit_bytes=vmem_mb * 2 ** 20)


def _full(shape):
    nd = len(shape)
    return pl.BlockSpec(shape, lambda *_: (0,) * nd)


def _rms(x, g):
    return x * lax.rsqrt(jnp.mean(x * x, axis=-1, keepdims=True) + EPS) * g


def _softplus(x):
    return jnp.maximum(x, 0.0) + jnp.log1p(jnp.exp(-jnp.abs(x)))


def _bdot(a, b):
    return jnp.dot(a.astype(BF16), b.astype(BF16), preferred_element_type=F32)


def _mm(a, b, dims):
    return lax.dot_general(a, b, (dims, ((), ())), preferred_element_type=F32)


def _all_gather(xs, name):
    nops = len(xs)

    def body(*refs):
        x_refs, out_refs = refs[:nops], refs[nops:2 * nops]
        send_sems, recv_sems, local_sems = refs[2 * nops:]
        x, y, cc = lax.axis_index("x"), lax.axis_index("y"), lax.axis_index("c")
        me, sibling = (x, y, cc), (x, y, 1 - cc)
        chips = [(1 - x, y), (x, 1 - y), (1 - x, 1 - y)]

        def slot(t, px, py, pc):
            return out_refs[t].at[4 * px + 2 * py + pc]

        def copy(t, k, block, to, src=None):
            return pltpu.make_async_remote_copy(
                src_ref=slot(t, *block) if src is None else src, dst_ref=slot(t, *block),
                send_sem=send_sems.at[7 * t + k], recv_sem=recv_sems.at[7 * t + k], device_id=to, device_id_type=MESH)

        ops = range(nops)
        mine = [pltpu.make_async_copy(x_refs[t], slot(t, *me), local_sems.at[t]) for t in ops]
        for cp in mine:
            cp.start()
        first = [copy(t, 0, me, sibling, src=x_refs[t]) for t in ops]
        first += [copy(t, 1 + j, me, (*chip, cc), src=x_refs[t]) for j, chip in enumerate(chips) for t in ops]
        for cp in first:
            cp.start()
        passed = []
        for j, chip in enumerate(chips):
            for t in ops:
                copy(t, 1 + j, (*chip, cc), me).wait_recv()
                fwd = copy(t, 4 + j, (*chip, cc), sibling)
                fwd.start()
                passed.append(fwd)
        for t in ops:
            copy(t, 0, sibling, me).wait_recv()
        for j, chip in enumerate(chips):
            for t in ops:
                copy(t, 4 + j, (*chip, 1 - cc), me).wait_recv()
        for cp in first + passed:
            cp.wait_send()
        for cp in mine:
            cp.wait()

    hbm = pl.BlockSpec(memory_space=pltpu.HBM)
    return pl.pallas_call(
        body, name=name, out_shape=[SDS((NDEV,) + a.shape, a.dtype) for a in xs],
        in_specs=[hbm] * nops, out_specs=[hbm] * nops,
        scratch_shapes=[pltpu.SemaphoreType.DMA((7 * nops,)), pltpu.SemaphoreType.DMA((7 * nops,)),
                        pltpu.SemaphoreType.DMA((nops,))],
    )(*xs)


def _exchange_sems(nops):
    return [pltpu.SemaphoreType.DMA((7 * nops,)), pltpu.SemaphoreType.DMA((7 * nops,)), pltpu.SemaphoreType.DMA((nops,))]


class _Exchange:
    def __init__(self, scatter, in_refs, out_refs, send_sems, recv_sems, local_sems):
        x, y, cc = lax.axis_index("x"), lax.axis_index("y"), lax.axis_index("c")
        me = 4 * x + 2 * y + cc
        nops = len(in_refs)
        self.local = [pltpu.make_async_copy(in_refs[t].at[me] if scatter else in_refs[t], out_refs[t].at[me],
                                            local_sems.at[t]) for t in range(nops)]
        self.sends, self.recvs = [], []
        for k in range(1, NDEV):
            px = 1 - x if k & 4 else x
            py = 1 - y if k & 2 else y
            pc = 1 - cc if k & 1 else cc
            peer = 4 * px + 2 * py + pc
            for t in range(nops):
                sem = 7 * t + k - 1
                self.sends.append(pltpu.make_async_remote_copy(
                    src_ref=in_refs[t].at[peer] if scatter else in_refs[t], dst_ref=out_refs[t].at[me],
                    send_sem=send_sems.at[sem], recv_sem=recv_sems.at[sem], device_id=(px, py, pc), device_id_type=MESH))
                self.recvs.append(pltpu.make_async_remote_copy(
                    src_ref=in_refs[t].at[me] if scatter else in_refs[t], dst_ref=out_refs[t].at[peer],
                    send_sem=send_sems.at[sem], recv_sem=recv_sems.at[sem], device_id=(px, py, pc), device_id_type=MESH))

    def start(self):
        for cp in self.local + self.sends:
            cp.start()

    def wait(self):
        for cp in self.recvs:
            cp.wait_recv()
        for cp in self.sends:
            cp.wait_send()
        for cp in self.local:
            cp.wait()


def _inproj(x2d, g_row, wp):
    n = x2d.shape[0]
    tm, tn = _tile(n, 1024), 1152

    def body(x_ref, g_ref, w_ref, proj_ref, h_ref, hs):
        @pl.when(pl.program_id(1) == 0)
        def _():
            h = _rms(x_ref[...], g_ref[...]).astype(BF16)
            hs[...] = h
            h_ref[...] = h

        proj_ref[...] = jnp.dot(hs[...], w_ref[...], preferred_element_type=F32)

    return pl.pallas_call(
        body, name="inproj", grid=(n // tm, NP // tn),
        in_specs=[pl.BlockSpec((tm, D), lambda i, j: (i, 0)), _full((1, D)), pl.BlockSpec((D, tn), lambda i, j: (0, j))],
        out_specs=[pl.BlockSpec((tm, tn), lambda i, j: (i, j)), pl.BlockSpec((tm, D), lambda i, j: (i, 0))],
        out_shape=[SDS((n, NP), F32), SDS((n, D), BF16)],
        scratch_shapes=[pltpu.VMEM((tm, D), BF16)], compiler_params=_cp(48),
    )(x2d, g_row, wp)


def _mm_tn(a, b, name):
    nn, m = a.shape
    k = b.shape[1]
    tk = 1152 if k % 1152 == 0 else _tile(k, 1024)
    tn = _tile(nn, 1024)

    def body(a_ref, b_ref, o_ref):
        p = _mm(a_ref[...].astype(BF16), b_ref[...].astype(BF16), TN)

        @pl.when(pl.program_id(1) == 0)
        def _():
            o_ref[...] = p

        @pl.when(pl.program_id(1) > 0)
        def _():
            o_ref[...] += p

    return pl.pallas_call(
        body, name=name, grid=(k // tk, nn // tn),
        in_specs=[pl.BlockSpec((tn, m), lambda j, t: (t, 0)), pl.BlockSpec((tn, tk), lambda j, t: (t, j))],
        out_specs=pl.BlockSpec((m, tk), lambda j, t: (0, j)),
        out_shape=SDS((m, k), F32), compiler_params=_cp(48),
    )(a, b)


def _inproj_bwd(dproj, wp, x2d, g_row, dxo, scatter=()):
    n = x2d.shape[0]
    tm, tk = _tile(n, 1024), 1152
    nk = NP // tk
    ns = len(scatter)

    def body(dp_ref, w_ref, x_ref, g_ref, dxo_ref, *rest):
        s_in, (dx_ref, dg_ref), s_out, (acc, *sems) = rest[:ns], rest[ns:ns + 2], rest[ns + 2:2 * ns + 2], rest[2 * ns + 2:]
        i, k = pl.program_id(0), pl.program_id(1)
        _ride_along(_Exchange(True, s_in, s_out, *sems) if ns else None,
                    (i == 0) & (k == 0), (i == n // tm - 1) & (k == nk - 1))
        p = _mm(dp_ref[...], w_ref[...], NT)

        @pl.when(k == 0)
        def _():
            acc[...] = p

        @pl.when(k > 0)
        def _():
            acc[...] += p

        @pl.when(k == nk - 1)
        def _():
            _, vjp = jax.vjp(_rms, x_ref[...], g_ref[...])
            dx, dg = vjp(acc[...])
            dx_ref[...] = dxo_ref[...] + dx

            @pl.when(i == 0)
            def _():
                dg_ref[...] = dg

            @pl.when(i > 0)
            def _():
                dg_ref[...] += dg

    hbm = pl.BlockSpec(memory_space=pltpu.HBM)
    return pl.pallas_call(
        body, name="inproj_bwd_exchange" if ns else "inproj_bwd", grid=(n // tm, nk),
        in_specs=[pl.BlockSpec((tm, tk), lambda i, k: (i, k)), pl.BlockSpec((D, tk), lambda i, k: (0, k)),
                  pl.BlockSpec((tm, D), lambda i, k: (i, 0)), _full((1, D)),
                  pl.BlockSpec((tm, D), lambda i, k: (i, 0))] + [hbm] * ns,
        out_specs=[pl.BlockSpec((tm, D), lambda i, k: (i, 0)), _full((1, D))] + [hbm] * ns,
        out_shape=[SDS((n, D), F32), SDS((1, D), F32)] + [SDS(a.shape, a.dtype) for a in scatter],
        scratch_shapes=[pltpu.VMEM((tm, D), F32)] + (_exchange_sems(ns) if ns else []), compiler_params=_cp(56),
    )(dproj, wp, x2d, g_row, dxo, *scatter)


def _loss_head(x2d, g_row, tgt):
    n = x2d.shape[0]
    tm = _tile(n, 512)

    def body(x_ref, g_ref, t_ref, dx_ref, dg_ref, loss_ref):
        i = pl.program_id(0)
        y, vjp = jax.vjp(_rms, x_ref[...], g_ref[...])
        err = y - t_ref[...]
        part = 0.5 * jnp.sum(jnp.mean(err * err, axis=-1, keepdims=True), axis=0, keepdims=True)
        dx, dg = vjp(err * (1.0 / D))
        dx_ref[...] = dx
        lb = jnp.broadcast_to(part, (8, LANE))

        @pl.when(i == 0)
        def _():
            dg_ref[...] = dg
            loss_ref[...] = lb

        @pl.when(i > 0)
        def _():
            dg_ref[...] += dg
            loss_ref[...] += lb

    return pl.pallas_call(
        body, name="loss_head", grid=(n // tm,),
        in_specs=[pl.BlockSpec((tm, D), lambda i: (i, 0)), _full((1, D)), pl.BlockSpec((tm, D), lambda i: (i, 0))],
        out_specs=[pl.BlockSpec((tm, D), lambda i: (i, 0)), _full((1, D)), _full((8, LANE))],
        out_shape=[SDS((n, D), F32), SDS((1, D), F32), SDS((8, LANE), F32)], compiler_params=_cp(40),
    )(x2d, g_row, tgt)


def _conv_fwd(ext_ref, w_ref, kw, halo, tt):
    acc = None
    for j in range(kw):
        term = w_ref[j:j + 1, :] * ext_ref[pl.ds(halo - (kw - 1) + j, tt), :]
        acc = term if acc is None else acc + term
    return acc


def _conv_bwd_x(dcp_ref, w_ref, kw, halo, tt):
    acc = None
    for j in range(kw):
        term = w_ref[j:j + 1, :] * dcp_ref[pl.ds(kw - 1 - j, tt + halo), :]
        acc = term if acc is None else acc + term
    return acc


def _conv_bwd_w(dc, ext_ref, gw_ref, kw, halo, tt):
    for j in range(kw):
        gw_ref[j:j + 1, :] += jnp.sum(dc * ext_ref[pl.ds(halo - (kw - 1) + j, tt), :], axis=0, keepdims=True)


SUB = 8


def _fill_shifts(sh_ref, rows):
    for s in range(1, SUB):
        sh_ref[s, 0:rows, :] = sh_ref[0, pl.ds(s, rows), :]


def _tap(sh_ref, off, rows):
    return sh_ref[off % SUB, pl.ds(off - off % SUB, rows), :]


def _conv_fwd_sh(ext_sh, w_ref, kw, halo, tt):
    acc = None
    for j in range(kw):
        term = w_ref[j:j + 1, :] * _tap(ext_sh, halo - (kw - 1) + j, tt)
        acc = term if acc is None else acc + term
    return acc


def _conv_bwd_x_sh(dcp_sh, w_ref, kw, halo, tt):
    acc = None
    for j in range(kw):
        term = w_ref[j:j + 1, :] * _tap(dcp_sh, kw - 1 - j, tt + halo)
        acc = term if acc is None else acc + term
    return acc


def _conv_bwd_w_sh(dc, ext_sh, gw_ref, kw, halo, tt):
    for j in range(kw):
        gw_ref[j:j + 1, :] += jnp.sum(dc * _tap(ext_sh, halo - (kw - 1) + j, tt), axis=0, keepdims=True)


def _a_post(c, z, g, b):
    mu = jnp.mean(c, axis=-1, keepdims=True)
    var = jnp.mean(jnp.square(c - mu), axis=-1, keepdims=True)
    a = (c - mu) * lax.rsqrt(var + EPS) * g + b
    return jax.nn.silu(a) * jax.nn.silu(z)


def _a_fwd(proj, dw, b_row, lg, lb, bsz, t):
    n = bsz * t
    tt = _tile(t, 256)
    nt = t // tt
    cb = C_A // 512

    def body(v_ref, g_ref, z_ref, vh_ref, gh_ref, w_ref, b_ref, lg_ref, lb_ref, y_ref, ext):
        i = pl.program_id(1)
        ext[0, 0:A_H, :] = jnp.where(i > 0, vh_ref[...] * jax.nn.sigmoid(gh_ref[...]), 0.0)
        ext[0, A_H:, :] = v_ref[...] * jax.nn.sigmoid(g_ref[...])
        _fill_shifts(ext, tt + A_H - SUB)
        c = _conv_fwd_sh(ext, w_ref, A_K, A_H, tt) + b_ref[...]
        y_ref[...] = _a_post(c, z_ref[...], lg_ref[...], lb_ref[...])

    def row(b, i):
        return b * nt + i

    def halo(b, i):
        return jnp.maximum((b * t + i * tt) // A_H - 1, 0)

    return pl.pallas_call(
        body, name="a_fwd", grid=(bsz, nt),
        in_specs=[pl.BlockSpec((tt, 512), lambda b, i: (row(b, i), cb)),
                  pl.BlockSpec((tt, 512), lambda b, i: (row(b, i), cb + 1)),
                  pl.BlockSpec((tt, 512), lambda b, i: (row(b, i), cb + 2)),
                  pl.BlockSpec((A_H, 512), lambda b, i: (halo(b, i), cb)),
                  pl.BlockSpec((A_H, 512), lambda b, i: (halo(b, i), cb + 1)),
                  _full((32, 512)), _full((1, 512)), _full((1, 512)), _full((1, 512))],
        out_specs=pl.BlockSpec((tt, 512), lambda b, i: (row(b, i), 0)),
        out_shape=SDS((n, 512), F32),
        scratch_shapes=[pltpu.VMEM((SUB, tt + A_H, 512), F32)], compiler_params=_cp(40),
    )(proj, proj, proj, proj, proj, dw, b_row, lg, lb)


def _a_bwd(dproj, proj, dy, dw, b_row, lg, lb, bsz, t):
    n = bsz * t
    tt = _tile(t, 256)
    nt = t // tt
    cb = C_A // 512

    def body(dp_any, v_ref, g_ref, z_ref, vh_ref, gh_ref, dy_ref, w_ref, b_ref, lg_ref, lb_ref,
             dp_ref, gw_ref, gb_ref, glg_ref, glb_ref, ext, dcp, dae, carry):
        b, i = pl.program_id(0), pl.program_id(1)
        ti = nt - 1 - i

        @pl.when((b == 0) & (i == 0))
        def _():
            gw_ref[...] = jnp.zeros_like(gw_ref)
            gb_ref[...] = jnp.zeros_like(gb_ref)
            glg_ref[...] = jnp.zeros_like(glg_ref)
            glb_ref[...] = jnp.zeros_like(glb_ref)

        @pl.when(i == 0)
        def _():
            carry[...] = jnp.zeros_like(carry)

        val, glu = v_ref[...], g_ref[...]
        sg = jax.nn.sigmoid(glu)
        ext[0, 0:A_H, :] = jnp.where(ti > 0, vh_ref[...] * jax.nn.sigmoid(gh_ref[...]), 0.0)
        ext[0, A_H:, :] = val * sg
        _fill_shifts(ext, tt + A_H - SUB)
        c = _conv_fwd_sh(ext, w_ref, A_K, A_H, tt) + b_ref[...]
        _, vjp = jax.vjp(_a_post, c, z_ref[...], lg_ref[...], lb_ref[...])
        dc, dz, dlg, dlb = vjp(dy_ref[...])
        gb_ref[...] += jnp.sum(dc, axis=0, keepdims=True)
        glg_ref[...] += dlg
        glb_ref[...] += dlb
        dcp[0, 0:A_H, :] = jnp.zeros((A_H, 512), F32)
        dcp[0, A_H:A_H + tt, :] = dc
        dcp[0, A_H + tt:, :] = jnp.zeros((A_H, 512), F32)
        _fill_shifts(dcp, tt + 2 * A_H - SUB)
        _conv_bwd_w_sh(dc, ext, gw_ref, A_K, A_H, tt)
        dae[...] = _conv_bwd_x_sh(dcp, w_ref, A_K, A_H, tt)
        dae[tt:tt + A_H, :] += carry[...]
        carry[...] = dae[0:A_H, :]
        da = dae[A_H:, :]
        dp_ref[:, 0:512] = (da * sg).astype(BF16)
        dp_ref[:, 512:1024] = (da * val * sg * (1.0 - sg)).astype(BF16)
        dp_ref[:, 1024:1536] = dz.astype(BF16)

    def row(b, i):
        return b * nt + (nt - 1 - i)

    def halo(b, i):
        return jnp.maximum((b * t + (nt - 1 - i) * tt) // A_H - 1, 0)

    outs = pl.pallas_call(
        body, name="a_bwd", grid=(bsz, nt),
        in_specs=[pl.BlockSpec(memory_space=pl.ANY),
                  pl.BlockSpec((tt, 512), lambda b, i: (row(b, i), cb)),
                  pl.BlockSpec((tt, 512), lambda b, i: (row(b, i), cb + 1)),
                  pl.BlockSpec((tt, 512), lambda b, i: (row(b, i), cb + 2)),
                  pl.BlockSpec((A_H, 512), lambda b, i: (halo(b, i), cb)),
                  pl.BlockSpec((A_H, 512), lambda b, i: (halo(b, i), cb + 1)),
                  pl.BlockSpec((tt, 512), lambda b, i: (row(b, i), 0)),
                  _full((32, 512)), _full((1, 512)), _full((1, 512)), _full((1, 512))],
        out_specs=[pl.BlockSpec((tt, 1536), lambda b, i: (row(b, i), C_A // 1536)),
                   _full((32, 512)), _full((1, 512)), _full((1, 512)), _full((1, 512))],
        out_shape=[SDS((n, NP), BF16), SDS((32, 512), F32), SDS((1, 512), F32), SDS((1, 512), F32), SDS((1, 512), F32)],
        input_output_aliases={0: 0},
        scratch_shapes=[pltpu.VMEM((SUB, tt + A_H, 512), F32), pltpu.VMEM((SUB, tt + 2 * A_H, 512), F32),
                        pltpu.VMEM((tt + A_H, 512), F32), pltpu.VMEM((A_H, 512), F32)],
        compiler_params=_cp(48),
    )(dproj, proj, proj, proj, proj, proj, dy, dw, b_row, lg, lb)
    return outs


def _b_post(blocks):
    out = []
    for idx, c in enumerate(blocks):
        s = jax.nn.silu(c)
        if idx < 2 * NH:
            s = s * lax.rsqrt(jnp.sum(s * s, axis=-1, keepdims=True) + EPS)
            if idx < NH:
                s = s * (LANE ** -0.5)
        out.append(s)
    return out


def _bprep_fwd(proj, wconv, bsz, t):
    n = bsz * t
    tt = _tile(t, 256)
    nt = t // tt

    def body(x_ref, xh_ref, w_ref, o_ref, ext):
        i = pl.program_id(1)
        ext[0:B_H, :] = jnp.where(i > 0, xh_ref[...], 0.0)
        ext[B_H:, :] = x_ref[...]
        c = _conv_fwd(ext, w_ref, B_K, B_H, tt)
        outs = _b_post([c[:, LANE * j:LANE * (j + 1)] for j in range(3 * NH)])
        for j, o in enumerate(outs):
            o_ref[:, LANE * j:LANE * (j + 1)] = o

    return pl.pallas_call(
        body, name="bprep_fwd", grid=(bsz, nt),
        in_specs=[pl.BlockSpec((tt, 3072), lambda b, i: (b * nt + i, 0)),
                  pl.BlockSpec((B_H, 3072), lambda b, i: (jnp.maximum((b * t + i * tt) // B_H - 1, 0), 0)),
                  _full((8, 3072))],
        out_specs=pl.BlockSpec((tt, 3072), lambda b, i: (b * nt + i, 0)),
        out_shape=SDS((n, 3072), F32),
        scratch_shapes=[pltpu.VMEM((tt + B_H, 3072), F32)], compiler_params=_cp(48),
    )(proj, proj, wconv)


def _bprep_bwd(dproj, proj, dqkvn, wconv, bsz, t):
    n = bsz * t
    tt = _tile(t, 256)
    nt = t // tt

    def body(dp_any, x_ref, xh_ref, dq_ref, w_ref, dp_ref, gw_ref, ext, dcp, dae, carry):
        b, i = pl.program_id(0), pl.program_id(1)
        ti = nt - 1 - i

        @pl.when((b == 0) & (i == 0))
        def _():
            gw_ref[...] = jnp.zeros_like(gw_ref)

        @pl.when(i == 0)
        def _():
            carry[...] = jnp.zeros_like(carry)

        ext[0:B_H, :] = jnp.where(ti > 0, xh_ref[...], 0.0)
        ext[B_H:, :] = x_ref[...]
        c = _conv_fwd(ext, w_ref, B_K, B_H, tt)
        _, vjp = jax.vjp(_b_post, [c[:, LANE * j:LANE * (j + 1)] for j in range(3 * NH)])
        (dcs,) = vjp([dq_ref[:, LANE * j:LANE * (j + 1)] for j in range(3 * NH)])
        dcp[0:B_H, :] = jnp.zeros((B_H, 3072), F32)
        for j, dcj in enumerate(dcs):
            dcp[B_H:B_H + tt, LANE * j:LANE * (j + 1)] = dcj
        dcp[B_H + tt:, :] = jnp.zeros((B_H, 3072), F32)
        _conv_bwd_w(dcp[B_H:B_H + tt, :], ext, gw_ref, B_K, B_H, tt)
        dae[...] = _conv_bwd_x(dcp, w_ref, B_K, B_H, tt)
        dae[tt:tt + B_H, :] += carry[...]
        carry[...] = dae[0:B_H, :]
        dp_ref[...] = dae[B_H:, :].astype(BF16)

    def row(b, i):
        return b * nt + (nt - 1 - i)

    return pl.pallas_call(
        body, name="bprep_bwd", grid=(bsz, nt),
        in_specs=[pl.BlockSpec(memory_space=pl.ANY),
                  pl.BlockSpec((tt, 3072), lambda b, i: (row(b, i), 0)),
                  pl.BlockSpec((B_H, 3072), lambda b, i: (jnp.maximum((b * t + (nt - 1 - i) * tt) // B_H - 1, 0), 0)),
                  pl.BlockSpec((tt, 3072), lambda b, i: (row(b, i), 0)),
                  _full((8, 3072))],
        out_specs=[pl.BlockSpec((tt, 3072), lambda b, i: (row(b, i), 0)), _full((8, 3072))],
        out_shape=[SDS((n, NP), BF16), SDS((8, 3072), F32)],
        input_output_aliases={0: 0},
        scratch_shapes=[pltpu.VMEM((tt + B_H, 3072), F32), pltpu.VMEM((tt + 2 * B_H, 3072), F32),
                        pltpu.VMEM((tt + B_H, 3072), F32), pltpu.VMEM((B_H, 3072), F32)],
        compiler_params=_cp(56),
    )(dproj, proj, proj, dqkvn, wconv)


def _split2(a):
    hi = a.astype(BF16)
    return hi, (a - hi.astype(F32)).astype(BF16)


def _split3(a):
    p1 = a.astype(BF16)
    r1 = a - p1.astype(F32)
    p2 = r1.astype(BF16)
    return p1, p2, (r1 - p2.astype(F32)).astype(BF16)


def _dot3_raw(a, b, dims):
    a1, a2 = _split2(a)
    b1, b2 = _split2(b)
    return _mm(a1, b1, dims) + (_mm(a1, b2, dims) + _mm(a2, b1, dims))


def _dot6(a, b, dims):
    a1, a2, a3 = _split3(a)
    b1, b2, b3 = _split3(b)
    return (_mm(a1, b1, dims) + (_mm(a1, b2, dims) + _mm(a2, b1, dims))
            + (_mm(a1, b3, dims) + _mm(a2, b2, dims) + _mm(a3, b1, dims)))


def _unit_lower_inverse_raw(lmats):
    r = lax.broadcasted_iota(jnp.int32, (CH, CH), 0)
    c = lax.broadcasted_iota(jnp.int32, (CH, CH), 1)
    eye = (r == c).astype(F32)
    blk = jnp.right_shift(r, 4) == jnp.right_shift(c, 4)
    dm = [jnp.where(blk, x, 0.0) for x in lmats]
    om = [a - b for a, b in zip(lmats, dm)]
    d2 = [_dot3_raw(x, x, NN) for x in dm]
    d4 = [_dot3_raw(x, x, NN) for x in d2]
    d8 = [_dot3_raw(x, x, NN) for x in d4]
    p = [_dot3_raw(eye - a, eye + b, NN) for a, b in zip(dm, d2)]
    p = [_dot3_raw(a, eye + b, NN) for a, b in zip(p, d4)]
    p = [_dot3_raw(a, eye + b, NN) for a, b in zip(p, d8)]
    m = [_dot3_raw(a, b, NN) for a, b in zip(p, om)]
    m2 = [_dot3_raw(x, x, NN) for x in m]
    t = [_dot3_raw(eye - a, eye + b, NN) for a, b in zip(m, m2)]
    return [_dot3_raw(a, b, NN) for a, b in zip(t, p)]


@jax.custom_vjp
def _unit_lower_inverse(lmats):
    return _unit_lower_inverse_raw(lmats)


def _unit_lower_inverse_fwd(lmats):
    tinv = _unit_lower_inverse_raw(lmats)
    return tinv, tinv


def _unit_lower_inverse_bwd(tinv, gs):
    x = [_dot6(t, g, TN) for t, g in zip(tinv, gs)]
    return ([-_dot6(a, t, NT) for a, t in zip(x, tinv)],)


_unit_lower_inverse.defvjp(_unit_lower_inverse_fwd, _unit_lower_inverse_bwd)


@jax.custom_vjp
def _saved_unit_lower_inverse(lmats, tinv):
    del lmats
    return tinv


def _saved_unit_lower_inverse_fwd(lmats, tinv):
    del lmats
    return tinv, tinv


def _saved_unit_lower_inverse_bwd(tinv, gs):
    (dl,) = _unit_lower_inverse_bwd(tinv, gs)
    return dl, [jnp.zeros_like(t) for t in tinv]


_saved_unit_lower_inverse.defvjp(_saved_unit_lower_inverse_fwd, _saved_unit_lower_inverse_bwd)


def _tri(lower):
    r = lax.broadcasted_iota(jnp.int32, (CH, CH), 0)
    c = lax.broadcasted_iota(jnp.int32, (CH, CH), 1)
    return ((r >= c) if lower else (r <= c)).astype(BF16)


def _tri_dot(x, lower, dims, tri_first):
    p1, p2, p3 = _split3(x)
    tri = _tri(lower)
    if tri_first:
        return _mm(tri, p1, dims) + (_mm(tri, p2, dims) + _mm(tri, p3, dims))
    return _mm(p1, tri, dims) + (_mm(p2, tri, dims) + _mm(p3, tri, dims))


@jax.custom_vjp
def _cumsum_rows(x):
    return _tri_dot(x, True, NN, True)


def _cumsum_rows_fwd(x):
    return _tri_dot(x, True, NN, True), None


def _cumsum_rows_bwd(_, g):
    return (_tri_dot(g, True, TN, True),)


_cumsum_rows.defvjp(_cumsum_rows_fwd, _cumsum_rows_bwd)


@jax.custom_vjp
def _cumsum_rows_t(x):
    return _tri_dot(x, False, TN, False)


def _cumsum_rows_t_fwd(x):
    return _tri_dot(x, False, TN, False), None


def _cumsum_rows_t_bwd(_, g):
    return (_tri_dot(g, False, NT, True),)


_cumsum_rows_t.defvjp(_cumsum_rows_t_fwd, _cumsum_rows_t_bwd)


def _delta_chunk(ss, qs, ks, vs, ba, zs, alog, dtb, og, tsaved=None):
    heads = range(NH)
    lane = lax.broadcasted_iota(jnp.int32, (1, LANE), 1)
    r = lax.broadcasted_iota(jnp.int32, (CH, CH), 0)
    c = lax.broadcasted_iota(jnp.int32, (CH, CH), 1)
    ri = lax.broadcasted_iota(jnp.int32, (CH, 1), 0)
    incl, strict = r >= c, r > c

    def pick(x, h):
        return jnp.sum(jnp.where(lane == h, x, 0.0), axis=-1, keepdims=True)

    beta = [jax.nn.sigmoid(pick(ba, h)) for h in heads]
    g = [-jnp.exp(pick(alog, h + NH)) * _softplus(pick(ba, h + NH) + pick(dtb, h + NH)) for h in heads]
    gb = [jnp.broadcast_to(x, (CH, CH)) for x in g]
    gca = [_cumsum_rows(x) for x in gb]
    gcr = [_cumsum_rows_t(x) for x in gb]
    gc = [jnp.sum(jnp.where(c == 0, x, 0.0), axis=-1, keepdims=True) for x in gca]
    gl = [jnp.sum(jnp.where(ri == CH - 1, x, 0.0), axis=0, keepdims=True) for x in gc]
    diff = [a - b for a, b in zip(gca, gcr)]
    gam_s = [jnp.where(strict, jnp.exp(jnp.where(strict, x, 0.0)), 0.0) for x in diff]
    gam_i = [jnp.where(incl, jnp.exp(jnp.where(incl, x, 0.0)), 0.0) for x in diff]

    kk = [_mm(k, k, NT) for k in ks]
    lmats = [beta[h] * kk[h] * gam_s[h] for h in heads]
    tinv = _unit_lower_inverse(lmats) if tsaved is None else _saved_unit_lower_inverse(lmats, tsaved)

    eg = [jnp.exp(x) for x in gc]
    u = [_mm(tinv[h], vs[h] * beta[h], NN) for h in heads]
    w = [_mm(tinv[h], ks[h] * (beta[h] * eg[h]), NN) for h in heads]
    qk = [_mm(qs[h], ks[h], NT) * gam_i[h] for h in heads]
    vn = [u[h] - _mm(w[h], ss[h], NN) for h in heads]
    o = [_mm(qs[h] * eg[h], ss[h], NN) + _mm(qk[h], vn[h], NN) for h in heads]
    sn = [jnp.exp(gl[h]) * ss[h] + _mm(ks[h] * jnp.exp(gl[h] - gc[h]), vn[h], TN) for h in heads]
    y = [_rms(o[h], og) * jax.nn.silu(zs[h]) for h in heads]
    return (sn, y), tinv


def _head_blocks(ref, base=0):
    return [ref[:, base + LANE * h:base + LANE * (h + 1)] for h in range(NH)]


def _ride_along(ex, first, last):
    if ex is None:
        return

    @pl.when(first)
    def _():
        ex.start()

    @pl.when(last)
    def _():
        ex.wait()


def _delta_fwd(qkvn, proj, alog, dtb, og, bsz, t, gather=()):
    n = bsz * t
    nc = t // CH
    ng = len(gather)

    def body(q_ref, k_ref, v_ref, ba_ref, z_ref, al_ref, dt_ref, og_ref, *rest):
        g_in, (y_ref, sh_ref, ti_ref) = rest[:ng], rest[ng:ng + 3]
        g_out, (s_scr, *sems) = rest[ng + 3:2 * ng + 3], rest[2 * ng + 3:]
        b, ci = pl.program_id(0), pl.program_id(1)
        _ride_along(_Exchange(False, g_in, g_out, *sems) if ng else None,
                    (b == 0) & (ci == 0), (b == bsz - 1) & (ci == nc - 1))

        @pl.when(ci == 0)
        def _():
            s_scr[...] = jnp.zeros_like(s_scr)

        ss = [s_scr[h] for h in range(NH)]
        for h in range(NH):
            sh_ref[h] = ss[h]
        (sn, y), tinv = _delta_chunk(ss, _head_blocks(q_ref), _head_blocks(k_ref), _head_blocks(v_ref), ba_ref[...],
                                     _head_blocks(z_ref), al_ref[...], dt_ref[...], og_ref[...])
        for h in range(NH):
            s_scr[h] = sn[h]
            ti_ref[h] = tinv[h]
            y_ref[:, LANE * h:LANE * (h + 1)] = y[h]

    def blk(width, col):
        return pl.BlockSpec((CH, width), lambda b, ci: (b * nc + ci, col))

    hbm = pl.BlockSpec(memory_space=pltpu.HBM)
    return pl.pallas_call(
        body, name="delta_fwd_gather" if ng else "delta_fwd", grid=(bsz, nc),
        in_specs=[blk(D, 0), blk(D, 1), blk(D, 2), blk(LANE, C_BA // LANE), blk(D, C_BZ // D),
                  _full((1, LANE)), _full((1, LANE)), _full((1, LANE))] + [hbm] * ng,
        out_specs=[blk(D, 0), pl.BlockSpec((None, None, NH, LANE, LANE), lambda b, ci: (b, ci, 0, 0, 0)),
                   pl.BlockSpec((None, None, NH, CH, CH), lambda b, ci: (b, ci, 0, 0, 0))] + [hbm] * ng,
        out_shape=[SDS((n, D), F32), SDS((bsz, nc, NH, LANE, LANE), F32), SDS((bsz, nc, NH, CH, CH), F32)]
        + [SDS((NDEV,) + a.shape, a.dtype) for a in gather],
        scratch_shapes=[pltpu.VMEM((NH, LANE, LANE), F32)] + (_exchange_sems(ng) if ng else []),
        compiler_params=_cp(40),
    )(qkvn, qkvn, qkvn, proj, proj, alog, dtb, og, *gather)


def _delta_bwd(dproj, qkvn, proj, shist, tsaved, dyb, alog, dtb, og, bsz, t, scatter=()):
    n = bsz * t
    nc = t // CH
    ns = len(scatter)

    def body(dp_any, q_ref, k_ref, v_ref, ba_ref, z_ref, sh_ref, ti_ref, dy_ref, al_ref, dt_ref, og_ref, *rest):
        s_in, (dp_ref, dqkv_ref, gal_ref, gdt_ref, gog_ref) = rest[:ns], rest[ns:ns + 5]
        s_out, (ds_scr, *sems) = rest[ns + 5:2 * ns + 5], rest[2 * ns + 5:]
        b, ci = pl.program_id(0), pl.program_id(1)
        _ride_along(_Exchange(True, s_in, s_out, *sems) if ns else None,
                    (b == 0) & (ci == 0), (b == bsz - 1) & (ci == nc - 1))

        @pl.when((b == 0) & (ci == 0))
        def _():
            gal_ref[...] = jnp.zeros_like(gal_ref)
            gdt_ref[...] = jnp.zeros_like(gdt_ref)
            gog_ref[...] = jnp.zeros_like(gog_ref)

        @pl.when(ci == 0)
        def _():
            ds_scr[...] = jnp.zeros_like(ds_scr)

        _, vjp, _ = jax.vjp(_delta_chunk, [sh_ref[h] for h in range(NH)], _head_blocks(q_ref), _head_blocks(k_ref),
                            _head_blocks(v_ref), ba_ref[...], _head_blocks(z_ref), al_ref[...], dt_ref[...], og_ref[...],
                            [ti_ref[h] for h in range(NH)], has_aux=True)
        ds, dq, dk, dv, dba, dz, dal, ddt, dog, _ = vjp(([ds_scr[h] for h in range(NH)], _head_blocks(dy_ref)))
        gal_ref[...] += dal
        gdt_ref[...] += ddt
        gog_ref[...] += dog
        dp_ref[:, D:D + LANE] = dba.astype(BF16)
        for h in range(NH):
            ds_scr[h] = ds[h]
            dp_ref[:, LANE * h:LANE * (h + 1)] = dz[h].astype(BF16)
            dqkv_ref[:, LANE * h:LANE * (h + 1)] = dq[h]
            dqkv_ref[:, D + LANE * h:D + LANE * (h + 1)] = dk[h]
            dqkv_ref[:, 2 * D + LANE * h:2 * D + LANE * (h + 1)] = dv[h]

    def blk(width, col):
        return pl.BlockSpec((CH, width), lambda b, ci: (b * nc + (nc - 1 - ci), col))

    hbm = pl.BlockSpec(memory_space=pltpu.HBM)
    return pl.pallas_call(
        body, name="delta_bwd_exchange" if ns else "delta_bwd", grid=(bsz, nc),
        in_specs=[pl.BlockSpec(memory_space=pl.ANY), blk(D, 0), blk(D, 1), blk(D, 2), blk(LANE, C_BA // LANE),
                  blk(D, C_BZ // D),
                  pl.BlockSpec((None, None, NH, LANE, LANE), lambda b, ci: (b, nc - 1 - ci, 0, 0, 0)),
                  pl.BlockSpec((None, None, NH, CH, CH), lambda b, ci: (b, nc - 1 - ci, 0, 0, 0)),
                  blk(D, 0), _full((1, LANE)), _full((1, LANE)), _full((1, LANE))] + [hbm] * ns,
        out_specs=[blk(D + LANE, C_BZ // (D + LANE)), blk(3 * D, 0), _full((1, LANE)), _full((1, LANE)),
                   _full((1, LANE))] + [hbm] * ns,
        out_shape=[SDS((n, NP), BF16), SDS((n, 3 * D), F32), SDS((1, LANE), F32), SDS((1, LANE), F32),
                   SDS((1, LANE), F32)] + [SDS(a.shape, a.dtype) for a in scatter],
        input_output_aliases={0: 0},
        scratch_shapes=[pltpu.VMEM((NH, LANE, LANE), F32)] + (_exchange_sems(ns) if ns else []),
        compiler_params=_cp(48),
    )(dproj, qkvn, qkvn, qkvn, proj, proj, shist, tsaved, dyb, alog, dtb, og, *scatter)


def _c_chunk(us, vs, zs, lgs, lbs, ws, bsb):
    gv = [jax.nn.gelu(v) for v in vs]
    width = LANE * len(gv)
    mu = sum(jnp.sum(x, axis=-1, keepdims=True) for x in gv) / width
    var = sum(jnp.sum(jnp.square(x - mu), axis=-1, keepdims=True) for x in gv) / width
    rstd = lax.rsqrt(var + EPS)
    r = lax.broadcasted_iota(jnp.int32, (SG, SG), 0)
    c = lax.broadcasted_iota(jnp.int32, (SG, SG), 1)
    out = []
    for j in range(len(gv)):
        nrm = (gv[j] - mu) * rstd * lgs[j] + lbs[j]
        mixed = jnp.dot(jnp.where(r >= c, ws[j], 0.0), nrm, preferred_element_type=F32) + bsb[j]
        out.append(jax.nn.gelu(us[j]) * mixed * jax.nn.silu(zs[j]))
    return out


def _c_args(u_ref, v_ref, z_ref, lg_ref, lb_ref, ws_ref, bs_ref):
    sl = [slice(LANE * j, LANE * (j + 1)) for j in range(4)]
    return ([u_ref[:, s] for s in sl], [v_ref[:, s] for s in sl], [z_ref[:, s] for s in sl],
            [lg_ref[:, s] for s in sl], [lb_ref[:, s] for s in sl],
            [ws_ref[j] for j in range(4)], [bs_ref[j] for j in range(4)])


def _c_fwd(proj, lg, lb, ws, bsb, n):
    cb = C_C // 512

    def body(u_ref, v_ref, z_ref, lg_ref, lb_ref, ws_ref, bs_ref, y_ref):
        outs = _c_chunk(*_c_args(u_ref, v_ref, z_ref, lg_ref, lb_ref, ws_ref, bs_ref))
        for j, o in enumerate(outs):
            y_ref[:, LANE * j:LANE * (j + 1)] = o

    return pl.pallas_call(
        body, name="c_fwd", grid=(n // SG,),
        in_specs=[pl.BlockSpec((SG, 512), lambda i: (i, cb)), pl.BlockSpec((SG, 512), lambda i: (i, cb + 1)),
                  pl.BlockSpec((SG, 512), lambda i: (i, cb + 2)), _full((1, 512)), _full((1, 512)),
                  _full((4, SG, SG)), _full((4, SG, SG))],
        out_specs=pl.BlockSpec((SG, 512), lambda i: (i, 0)),
        out_shape=SDS((n, 512), F32), compiler_params=_cp(32),
    )(proj, proj, proj, lg, lb, ws, bsb)


def _c_bwd(dproj, proj, dy, lg, lb, ws, bsb, n):
    cb = C_C // 512

    def body(dp_any, u_ref, v_ref, z_ref, dy_ref, lg_ref, lb_ref, ws_ref, bs_ref,
             dp_ref, glg_ref, glb_ref, gws_ref, gbs_ref):
        @pl.when(pl.program_id(0) == 0)
        def _():
            glg_ref[...] = jnp.zeros_like(glg_ref)
            glb_ref[...] = jnp.zeros_like(glb_ref)
            gws_ref[...] = jnp.zeros_like(gws_ref)
            gbs_ref[...] = jnp.zeros_like(gbs_ref)

        _, vjp = jax.vjp(_c_chunk, *_c_args(u_ref, v_ref, z_ref, lg_ref, lb_ref, ws_ref, bs_ref))
        dus, dvs, dzs, dlgs, dlbs, dwss, dbss = vjp([dy_ref[:, LANE * j:LANE * (j + 1)] for j in range(4)])
        for j in range(4):
            sl = slice(LANE * j, LANE * (j + 1))
            dp_ref[:, LANE * j:LANE * (j + 1)] = dus[j].astype(BF16)
            dp_ref[:, 512 + LANE * j:512 + LANE * (j + 1)] = dvs[j].astype(BF16)
            dp_ref[:, 1024 + LANE * j:1024 + LANE * (j + 1)] = dzs[j].astype(BF16)
            glg_ref[:, sl] += dlgs[j]
            glb_ref[:, sl] += dlbs[j]
            gws_ref[j] += dwss[j]
            gbs_ref[j] += jnp.broadcast_to(jnp.sum(dbss[j], axis=-1, keepdims=True), (SG, SG))

    return pl.pallas_call(
        body, name="c_bwd", grid=(n // SG,),
        in_specs=[pl.BlockSpec(memory_space=pl.ANY),
                  pl.BlockSpec((SG, 512), lambda i: (i, cb)), pl.BlockSpec((SG, 512), lambda i: (i, cb + 1)),
                  pl.BlockSpec((SG, 512), lambda i: (i, cb + 2)), pl.BlockSpec((SG, 512), lambda i: (i, 0)),
                  _full((1, 512)), _full((1, 512)), _full((4, SG, SG)), _full((4, SG, SG))],
        out_specs=[pl.BlockSpec((SG, 1536), lambda i: (i, C_C // 1536)),
                   _full((1, 512)), _full((1, 512)), _full((4, SG, SG)), _full((4, SG, SG))],
        out_shape=[SDS((n, NP), BF16), SDS((1, 512), F32), SDS((1, 512), F32), SDS((4, SG, SG), F32), SDS((4, SG, SG), F32)],
        input_output_aliases={0: 0}, compiler_params=_cp(32),
    )(dproj, proj, proj, proj, dy, lg, lb, ws, bsb)


def _merge_fwd(x2d, ya, yb, yc, proj, ap, bp, cp, wo):
    n = x2d.shape[0]
    tm = _tile(n, 256)
    gb = C_G // D

    def body(x_ref, ya_ref, yb_ref, yc_ref, g0_ref, g1_ref, g2_ref, ap_ref, bp_ref, cp_ref, wo_ref, o_ref):
        merged = (jax.nn.sigmoid(g0_ref[...]) * _bdot(ya_ref[...], ap_ref[...])
                  + jax.nn.sigmoid(g1_ref[...]) * _bdot(yb_ref[...], bp_ref[...])
                  + jax.nn.sigmoid(g2_ref[...]) * _bdot(yc_ref[...], cp_ref[...]))
        o_ref[...] = x_ref[...] + _bdot(merged, wo_ref[...])

    def rows(w):
        return pl.BlockSpec((tm, w), lambda i: (i, 0))

    return pl.pallas_call(
        body, name="merge_fwd", grid=(n // tm,),
        in_specs=[rows(D), rows(512), rows(D), rows(512),
                  pl.BlockSpec((tm, D), lambda i: (i, gb)), pl.BlockSpec((tm, D), lambda i: (i, gb + 1)),
                  pl.BlockSpec((tm, D), lambda i: (i, gb + 2)),
                  _full((512, D)), _full((D, D)), _full((512, D)), _full((D, D))],
        out_specs=rows(D), out_shape=SDS((n, D), F32), compiler_params=_cp(48),
    )(x2d, ya, yb, yc, proj, proj, proj, ap, bp, cp, wo)


def _merge_bwd(dxo, ya, yb, yc, proj, ap, bp, cp, apt, bpt, cpt, wot):
    n = dxo.shape[0]
    tm = _tile(n, 128)
    gb = C_G // D

    def body(d_ref, ya_ref, yb_ref, yc_ref, g0_ref, g1_ref, g2_ref, ap_ref, bp_ref, cp_ref,
             apt_ref, bpt_ref, cpt_ref, wot_ref,
             dp_ref, dya_ref, dyb_ref, dyc_ref, dpa_ref, dpb_ref, dpc_ref, mg_ref):
        dm = _bdot(d_ref[...], wot_ref[...])
        merged = None
        for j, (g_ref, y_ref, w_ref, wt_ref, dy_ref, dpj_ref) in enumerate((
                (g0_ref, ya_ref, ap_ref, apt_ref, dya_ref, dpa_ref),
                (g1_ref, yb_ref, bp_ref, bpt_ref, dyb_ref, dpb_ref),
                (g2_ref, yc_ref, cp_ref, cpt_ref, dyc_ref, dpc_ref))):
            s = jax.nn.sigmoid(g_ref[...])
            pj = _bdot(y_ref[...], w_ref[...])
            merged = s * pj if merged is None else merged + s * pj
            dp_ref[:, D * j:D * (j + 1)] = (dm * pj * s * (1.0 - s)).astype(BF16)
            dpj = (dm * s).astype(BF16)
            dpj_ref[...] = dpj
            dy_ref[...] = jnp.dot(dpj, wt_ref[...], preferred_element_type=F32)
        mg_ref[...] = merged

    def rows(w):
        return pl.BlockSpec((tm, w), lambda i: (i, 0))

    return pl.pallas_call(
        body, name="merge_bwd", grid=(n // tm,),
        in_specs=[rows(D), rows(512), rows(D), rows(512),
                  pl.BlockSpec((tm, D), lambda i: (i, gb)), pl.BlockSpec((tm, D), lambda i: (i, gb + 1)),
                  pl.BlockSpec((tm, D), lambda i: (i, gb + 2)),
                  _full((512, D)), _full((D, D)), _full((512, D)),
                  _full((D, 512)), _full((D, D)), _full((D, 512)), _full((D, D))],
        out_specs=[pl.BlockSpec((tm, 3 * D), lambda i: (i, C_G // (3 * D))), rows(512), rows(D), rows(512),
                   rows(D), rows(D), rows(D), rows(D)],
        out_shape=[SDS((n, NP), BF16), SDS((n, 512), F32), SDS((n, D), F32), SDS((n, 512), F32),
                   SDS((n, D), BF16), SDS((n, D), BF16), SDS((n, D), BF16), SDS((n, D), F32)],
        compiler_params=_cp(56),
    )(dxo, ya, yb, yc, proj, proj, proj, ap, bp, cp, apt, bpt, cpt, wot)


def _sum_parts(p_ref):
    g = p_ref[0].astype(F32)
    for s in range(1, NDEV):
        g = g + p_ref[s].astype(F32)
    return g


def _adamw(g, w, m, v):
    nm = ADAM_B1 * m + (1.0 - ADAM_B1) * g
    nv = ADAM_B2 * v + (1.0 - ADAM_B2) * jnp.square(g)
    nm_hat = nm / (1.0 - ADAM_B1 ** ADAM_STEP)
    nv_hat = nv / (1.0 - ADAM_B2 ** ADAM_STEP)
    return -ADAM_LR * (nm_hat / (jnp.sqrt(nv_hat) + ADAM_EPS) + ADAM_WD * w), nm, nv


def _reduce_adamw(parts, w, m, v, name):
    r, c = w.shape
    tr = _tile(r, 128)

    def body(p_ref, w_ref, m_ref, v_ref, g_ref, d_ref, nm_ref, nv_ref):
        g = _sum_parts(p_ref)
        g_ref[...] = g
        d_ref[...], nm_ref[...], nv_ref[...] = _adamw(g, w_ref[...], m_ref[...], v_ref[...])

    blk = pl.BlockSpec((tr, c), lambda i: (i, 0))
    return pl.pallas_call(
        body, name=name, grid=(r // tr,),
        in_specs=[pl.BlockSpec((NDEV, tr, c), lambda i: (0, i, 0)), blk, blk, blk],
        out_specs=[blk, blk, blk, blk], out_shape=[SDS((r, c), F32)] * 4, compiler_params=_cp(48),
    )(parts, w, m, v)


def _reduce_adamw_leaves(parts, ws, ms, vs, name):
    nleaf = len(ws)
    counts = [len(p) if isinstance(p, (list, tuple)) else 0 for p in parts]
    flat = [a for p in parts for a in (p if isinstance(p, (list, tuple)) else [p])]

    def body(*refs):
        p_refs, rest = refs[:len(flat)], refs[len(flat):]
        w_refs, m_refs, v_refs = rest[:nleaf], rest[nleaf:2 * nleaf], rest[2 * nleaf:3 * nleaf]
        outs = rest[3 * nleaf:]
        at = 0
        for i in range(nleaf):
            g_ref, d_ref, nm_ref, nv_ref = outs[i], outs[nleaf + i], outs[2 * nleaf + i], outs[3 * nleaf + i]
            for idx in (range(counts[i]) if counts[i] else [Ellipsis]):
                g = _sum_parts(p_refs[at])
                at += 1
                g_ref[idx] = g
                d_ref[idx], nm_ref[idx], nv_ref[idx] = _adamw(g, w_refs[i][idx], m_refs[i][idx], v_refs[i][idx])

    vm = pl.BlockSpec(memory_space=pltpu.VMEM)
    outs = pl.pallas_call(
        body, name=name, in_specs=[vm] * (len(flat) + 3 * nleaf), out_specs=[vm] * (4 * nleaf),
        out_shape=[SDS(w.shape, F32) for w in ws] * 4, compiler_params=_cp(56),
    )(*flat, *ws, *ms, *vs)
    return [outs[j * nleaf:(j + 1) * nleaf] for j in range(4)]


def _unshard(name, g):
    if name in ROW_SHARDED:
        return g.reshape(g.shape[0] * g.shape[1], g.shape[2])
    g = jnp.moveaxis(g, 0, 1)
    return g.reshape(g.shape[0], g.shape[1] * g.shape[2])


def _reshard(name, full):
    r, c = full.shape
    if name in ROW_SHARDED:
        return full.reshape(NDEV, r // NDEV, c)
    return jnp.moveaxis(full.reshape(r, NDEV, c // NDEV), 1, 0)


def _w_in_to_padded(slabs):
    pieces = []
    for lo, hi, _ in sorted(SEGMENTS, key=lambda s: s[2]):
        for d in range(NDEV):
            a, b = max(lo, d * W_SHARD), min(hi, (d + 1) * W_SHARD)
            if a < b:
                pieces.append(slabs[d, :, a - d * W_SHARD:b - d * W_SHARD])
    pieces.append(jnp.zeros(slabs.shape[1:2] + (NP - C_BA - 16,), slabs.dtype))
    return jnp.concatenate(pieces, axis=-1)


def _w_in_from_padded(g):
    slabs = []
    for d in range(NDEV):
        pieces = []
        for lo, hi, pstart in SEGMENTS:
            a, b = max(lo, d * W_SHARD), min(hi, (d + 1) * W_SHARD)
            if a < b:
                pieces.append(g[:, pstart + a - lo:pstart + b - lo])
        pieces.append(jnp.zeros(g.shape[:1] + (W_SHARD_PAD - W_SHARD,), g.dtype))
        slabs.append(jnp.concatenate(pieces, axis=-1))
    return jnp.stack(slabs)


def _pad_w_in(w):
    return jnp.pad(w, ((0, 0), (0, W_SHARD_PAD - W_SHARD)))


def _lane_row(vec8, offset):
    return jnp.pad(vec8, (offset, LANE - NH - offset))[None]


def kernel(x, norm_g, w_in, a_dw, a_dw_b, a_ln_g, a_ln_b, a_proj, b_conv, b_a_log, b_dt_bias, b_onorm_g, b_proj, c_ln_g, c_ln_b, c_ws, c_bs, c_proj, w_out, final_g, loss_target, m_norm_g, m_w_in, m_a_dw, m_a_dw_b, m_a_ln_g, m_a_ln_b, m_a_proj, m_b_conv, m_b_a_log, m_b_dt_bias, m_b_onorm_g, m_b_proj, m_c_ln_g, m_c_ln_b, m_c_ws, m_c_bs, m_c_proj, m_w_out, m_final_g, v_norm_g, v_w_in, v_a_dw, v_a_dw_b, v_a_ln_g, v_a_ln_b, v_a_proj, v_b_conv, v_b_a_log, v_b_dt_bias, v_b_onorm_g, v_b_proj, v_c_ln_g, v_c_ln_b, v_c_ws, v_c_bs, v_c_proj, v_w_out, v_final_g):
    wts = dict(norm_g=norm_g, w_in=w_in, a_dw=a_dw, a_dw_b=a_dw_b, a_ln_g=a_ln_g, a_ln_b=a_ln_b, a_proj=a_proj,
               b_conv=b_conv, b_a_log=b_a_log, b_dt_bias=b_dt_bias, b_onorm_g=b_onorm_g, b_proj=b_proj,
               c_ln_g=c_ln_g, c_ln_b=c_ln_b, c_ws=c_ws, c_bs=c_bs, c_proj=c_proj, w_out=w_out, final_g=final_g)
    mom = dict(norm_g=m_norm_g, w_in=m_w_in, a_dw=m_a_dw, a_dw_b=m_a_dw_b, a_ln_g=m_a_ln_g, a_ln_b=m_a_ln_b,
               a_proj=m_a_proj, b_conv=m_b_conv, b_a_log=m_b_a_log, b_dt_bias=m_b_dt_bias, b_onorm_g=m_b_onorm_g,
               b_proj=m_b_proj, c_ln_g=m_c_ln_g, c_ln_b=m_c_ln_b, c_ws=m_c_ws, c_bs=m_c_bs, c_proj=m_c_proj,
               w_out=m_w_out, final_g=m_final_g)
    vel = dict(norm_g=v_norm_g, w_in=v_w_in, a_dw=v_a_dw, a_dw_b=v_a_dw_b, a_ln_g=v_a_ln_g, a_ln_b=v_a_ln_b,
               a_proj=v_a_proj, b_conv=v_b_conv, b_a_log=v_b_a_log, b_dt_bias=v_b_dt_bias, b_onorm_g=v_b_onorm_g,
               b_proj=v_b_proj, c_ln_g=v_c_ln_g, c_ln_b=v_c_ln_b, c_ws=v_c_ws, c_bs=v_c_bs, c_proj=v_c_proj,
               w_out=v_w_out, final_g=v_final_g)

    bsz, t, _ = x.shape
    n = bsz * t
    depth = norm_g.shape[0]
    x2d = x.reshape(n, D)
    tgt = loss_target.reshape(n, D)

    def weight_blocks(l):
        return [_pad_w_in(w_in[l].astype(BF16))] + [wts[k][l].astype(BF16) for k in BIG_REST]

    def matmul_weights(w_in_all, *rest):
        got = {k: _unshard(k, g) for k, g in zip(BIG_REST, rest)}
        got['wp'] = _w_in_to_padded(w_in_all)
        return got

    conv_all = _all_gather([wts[k] for k in SMALL], "gather_conv_weights")
    conv_w = {k: jnp.stack([_unshard(k, g[:, l]) for l in range(depth)]) for k, g in zip(SMALL, conv_all)}
    a_dw32 = jnp.pad(conv_w['a_dw'], ((0, 0), (0, 32 - A_K), (0, 0)))
    b_conv8 = jnp.pad(conv_w['b_conv'], ((0, 0), (0, 8 - B_K), (0, 0)))
    bsb = jnp.broadcast_to(c_bs[..., None], c_bs.shape + (SG,))
    full = [matmul_weights(*_all_gather(weight_blocks(0), "gather_matmul_weights"))]

    saved = []
    xl = x2d
    for l in range(depth):
        alog, dtb = _lane_row(b_a_log[l], NH), _lane_row(b_dt_bias[l], NH)
        proj, h = _inproj(xl, norm_g[l][None], full[l]['wp'])
        ya = _a_fwd(proj, a_dw32[l], a_dw_b[l][None], a_ln_g[l][None], a_ln_b[l][None], bsz, t)
        qkvn = _bprep_fwd(proj, b_conv8[l], bsz, t)
        yb, shist, tsave, *nxt = _delta_fwd(qkvn, proj, alog, dtb, b_onorm_g[l][None], bsz, t,
                                            gather=weight_blocks(l + 1) if l + 1 < depth else ())
        if nxt:
            full.append(matmul_weights(*nxt))
        yc = _c_fwd(proj, c_ln_g[l][None], c_ln_b[l][None], c_ws[l], bsb[l], n)
        xn = _merge_fwd(xl, ya, yb, yc, proj, full[l]['a_proj'], full[l]['b_proj'], full[l]['c_proj'], full[l]['w_out'])
        saved.append((xl, proj, h, ya, yb, yc, qkvn, shist, tsave, alog, dtb))
        xl = xn

    dx, g_final, loss_blk = _loss_head(xl, final_g[None], tgt)
    loss = lax.psum(loss_blk[0, 0], ("x", "y", "c"))

    gfull = {k: [None] * depth for k in WEIGHTS if k != 'final_g'}
    recv = [None] * depth

    def grad_slabs(l):
        return [_w_in_from_padded(gfull['w_in'][l]).astype(BF16)] + [
            _reshard(k, gfull[k][l]).astype(BF16) for k in SHARDED_REST]

    for l in reversed(range(depth)):
        xl, proj, h, ya, yb, yc, qkvn, shist, tsave, alog, dtb = saved[l]
        ap, bp, cp, wo = full[l]['a_proj'], full[l]['b_proj'], full[l]['c_proj'], full[l]['w_out']
        dproj, dya, dyb, dyc, dpa, dpb, dpc, merged = _merge_bwd(dx, ya, yb, yc, proj, ap, bp, cp, ap.T, bp.T, cp.T, wo.T)
        gfull['a_proj'][l] = _mm_tn(ya, dpa, "grad_a_proj")
        gfull['b_proj'][l] = _mm_tn(yb, dpb, "grad_b_proj")
        gfull['c_proj'][l] = _mm_tn(yc, dpc, "grad_c_proj")
        gfull['w_out'][l] = _mm_tn(merged, dx, "grad_w_out")
        dproj, g_clg, g_clb, g_cws, g_cbs = _c_bwd(dproj, proj, dyc, c_ln_g[l][None], c_ln_b[l][None], c_ws[l], bsb[l], n)
        dproj, g_adw, g_adb, g_alg, g_alb = _a_bwd(dproj, proj, dya, a_dw32[l], a_dw_b[l][None], a_ln_g[l][None],
                                                   a_ln_b[l][None], bsz, t)
        dproj, dqkvn, g_alog, g_dt, g_og, *got = _delta_bwd(dproj, qkvn, proj, shist, tsave, dyb, alog, dtb, b_onorm_g[l][None],
                                                            bsz, t, scatter=grad_slabs(l + 1) if l + 1 < depth else ())
        if got:
            recv[l + 1] = got
        dproj, g_bconv = _bprep_bwd(dproj, proj, dqkvn, b_conv8[l], bsz, t)
        gfull['w_in'][l] = _mm_tn(h, dproj, "grad_w_in")
        gfull['a_dw'][l] = g_adw[:A_K]
        gfull['b_conv'][l] = g_bconv[:B_K]
        dx, g_ng, *got = _inproj_bwd(dproj, full[l]['wp'], xl, norm_g[l][None], dx, scatter=grad_slabs(0) if l == 0 else ())
        if got:
            recv[0] = got
        gfull['norm_g'][l] = g_ng[0]
        gfull['a_dw_b'][l], gfull['a_ln_g'][l], gfull['a_ln_b'][l] = g_adb[0], g_alg[0], g_alb[0]
        gfull['b_a_log'][l], gfull['b_dt_bias'][l] = g_alog[0, NH:2 * NH], g_dt[0, NH:2 * NH]
        gfull['b_onorm_g'][l] = g_og[0]
        gfull['c_ln_g'][l], gfull['c_ln_b'][l] = g_clg[0], g_clb[0]
        gfull['c_ws'][l], gfull['c_bs'][l] = g_cws, g_cbs[:, :, 0]
    grad_x = dx.reshape(bsz, t, D)

    outs_w = [_reduce_adamw(recv[l][0], _pad_w_in(w_in[l]), _pad_w_in(m_w_in[l]), _pad_w_in(v_w_in[l]), "adamw_w_in")
              for l in range(depth)]
    outs_s = _reduce_adamw_leaves([[recv[l][1 + i] for l in range(depth)] for i in range(len(SHARDED_REST))],
                                  [wts[k] for k in SHARDED_REST], [mom[k] for k in SHARDED_REST],
                                  [vel[k] for k in SHARDED_REST], "adamw_sharded")

    def upto3d(a):
        return a[None] if a.ndim == 1 else a.reshape((-1,) + a.shape[-2:]) if a.ndim > 3 else a

    grepl = [upto3d(jnp.stack(gfull[k]) if k != 'final_g' else g_final[0]) for k in REPL]
    outs_r = _reduce_adamw_leaves(_all_gather(grepl, "gather_replicated_grads"), [upto3d(wts[k]) for k in REPL],
                                  [upto3d(mom[k]) for k in REPL], [upto3d(vel[k]) for k in REPL], "adamw_replicated")

    res = []
    for j in range(4):
        leaves = {k: outs_r[j][i].reshape(wts[k].shape) for i, k in enumerate(REPL)}
        leaves.update({k: outs_s[j][i] for i, k in enumerate(SHARDED_REST)})
        leaves['w_in'] = jnp.stack([outs_w[l][j][:, :W_SHARD] for l in range(depth)])
        res.append([leaves[k] for k in WEIGHTS])
    grads, deltas, new_m, new_v = res
    return (loss, grad_x, *grads, *deltas, *new_m, *new_v)
```

```python
import jax
import jax.numpy as jnp
from jax import lax
from jax.experimental import pallas as pl
from jax.experimental.pallas import tpu as pltpu

F32 = jnp.float32
BF16 = jnp.bfloat16
SDS = jax.ShapeDtypeStruct
MESH = pl.DeviceIdType.MESH

NDEV = 8
D = 1024
EPS = 1e-6
LANE = 128

C_Q, C_K, C_V = 0, 1024, 2048
C_G = 3072
C_A = 6144
C_C = 7680
C_BZ = 9216
C_BA = 10240
NP = 10368
N_IN = 10256
SEGMENTS = ((0, 1536, C_A), (1536, 4608, C_Q), (4608, 5632, C_BZ), (5632, 5648, C_BA), (5648, 7184, C_C), (7184, 10256, C_G))
W_SHARD = N_IN // NDEV
W_SHARD_PAD = 1408

A_K, A_H = 31, 32
B_K, B_H = 4, 8
CH = 64
SG = 128
NH = 8

ADAM_LR, ADAM_B1, ADAM_B2, ADAM_EPS, ADAM_WD, ADAM_STEP = 0.001, 0.9, 0.999, 1e-08, 0.01, 10

WEIGHTS = ['norm_g', 'w_in', 'a_dw', 'a_dw_b', 'a_ln_g', 'a_ln_b', 'a_proj', 'b_conv', 'b_a_log', 'b_dt_bias',
           'b_onorm_g', 'b_proj', 'c_ln_g', 'c_ln_b', 'c_ws', 'c_bs', 'c_proj', 'w_out', 'final_g']
SHARDED_REST = ['a_dw', 'a_proj', 'b_conv', 'b_proj', 'c_proj', 'w_out']
REPL = [n for n in WEIGHTS if n != 'w_in' and n not in SHARDED_REST]
BIG_REST = ['a_proj', 'b_proj', 'c_proj', 'w_out']
SMALL = ['a_dw', 'b_conv']
ROW_SHARDED = ('b_proj', 'w_out')

NN = ((1,), (0,))
NT = ((1,), (1,))
TN = ((0,), (0,))


def _tile(n, pref):
    return pref if (n >= pref and n % pref == 0) else n


def _cp(vmem_mb):
    return pltpu.CompilerParams(vmem_limit_bytes=vmem_mb * 2 ** 20)


def _full(shape):
    nd = len(shape)
    return pl.BlockSpec(shape, lambda *_: (0,) * nd)


def _resident(shape):
    nd = len(shape)
    return pl.BlockSpec(shape, lambda *_: (0,) * nd, pipeline_mode=pl.Buffered(1))


def _rms(x, g):
    return x * lax.rsqrt(jnp.mean(x * x, axis=-1, keepdims=True) + EPS) * g


def _softplus(x):
    return jnp.maximum(x, 0.0) + jnp.log1p(jnp.exp(-jnp.abs(x)))


def _bdot(a, b):
    return jnp.dot(a.astype(BF16), b.astype(BF16), preferred_element_type=F32)


def _mm(a, b, dims):
    return lax.dot_general(a, b, (dims, ((), ())), preferred_element_type=F32)


def _all_gather(xs, name):
    nops = len(xs)

    def body(*refs):
        x_refs, out_refs = refs[:nops], refs[nops:2 * nops]
        send_sems, recv_sems, local_sems = refs[2 * nops:]
        x, y, cc = lax.axis_index("x"), lax.axis_index("y"), lax.axis_index("c")
        me, sibling = (x, y, cc), (x, y, 1 - cc)
        chips = [(1 - x, y), (x, 1 - y), (1 - x, 1 - y)]

        def slot(t, px, py, pc):
            return out_refs[t].at[4 * px + 2 * py + pc]

        def copy(t, k, block, to, src=None):
            return pltpu.make_async_remote_copy(
                src_ref=slot(t, *block) if src is None else src, dst_ref=slot(t, *block),
                send_sem=send_sems.at[7 * t + k], recv_sem=recv_sems.at[7 * t + k], device_id=to, device_id_type=MESH)

        ops = range(nops)
        mine = [pltpu.make_async_copy(x_refs[t], slot(t, *me), local_sems.at[t]) for t in ops]
        for cp in mine:
            cp.start()
        first = [copy(t, 0, me, sibling, src=x_refs[t]) for t in ops]
        first += [copy(t, 1 + j, me, (*chip, cc), src=x_refs[t]) for j, chip in enumerate(chips) for t in ops]
        for cp in first:
            cp.start()
        passed = []
        for j, chip in enumerate(chips):
            for t in ops:
                copy(t, 1 + j, (*chip, cc), me).wait_recv()
                fwd = copy(t, 4 + j, (*chip, cc), sibling)
                fwd.start()
                passed.append(fwd)
        for t in ops:
            copy(t, 0, sibling, me).wait_recv()
        for j, chip in enumerate(chips):
            for t in ops:
                copy(t, 4 + j, (*chip, 1 - cc), me).wait_recv()
        for cp in first + passed:
            cp.wait_send()
        for cp in mine:
            cp.wait()

    hbm = pl.BlockSpec(memory_space=pltpu.HBM)
    return pl.pallas_call(
        body, name=name, out_shape=[SDS((NDEV,) + a.shape, a.dtype) for a in xs],
        in_specs=[hbm] * nops, out_specs=[hbm] * nops,
        scratch_shapes=[pltpu.SemaphoreType.DMA((7 * nops,)), pltpu.SemaphoreType.DMA((7 * nops,)),
                        pltpu.SemaphoreType.DMA((nops,))],
    )(*xs)


def _exchange_sems(nops):
    return [pltpu.SemaphoreType.DMA((7 * nops,)), pltpu.SemaphoreType.DMA((7 * nops,)), pltpu.SemaphoreType.DMA((nops,))]


class _Exchange:
    def __init__(self, scatter, in_refs, out_refs, send_sems, recv_sems, local_sems):
        x, y, cc = lax.axis_index("x"), lax.axis_index("y"), lax.axis_index("c")
        me = 4 * x + 2 * y + cc
        nops = len(in_refs)
        self.local = [pltpu.make_async_copy(in_refs[t].at[me] if scatter else in_refs[t], out_refs[t].at[me],
                                            local_sems.at[t]) for t in range(nops)]
        self.sends, self.recvs = [], []
        for k in range(1, NDEV):
            px = 1 - x if k & 4 else x
            py = 1 - y if k & 2 else y
            pc = 1 - cc if k & 1 else cc
            peer = 4 * px + 2 * py + pc
            for t in range(nops):
                sem = 7 * t + k - 1
                self.sends.append(pltpu.make_async_remote_copy(
                    src_ref=in_refs[t].at[peer] if scatter else in_refs[t], dst_ref=out_refs[t].at[me],
                    send_sem=send_sems.at[sem], recv_sem=recv_sems.at[sem], device_id=(px, py, pc), device_id_type=MESH))
                self.recvs.append(pltpu.make_async_remote_copy(
                    src_ref=in_refs[t].at[me] if scatter else in_refs[t], dst_ref=out_refs[t].at[peer],
                    send_sem=send_sems.at[sem], recv_sem=recv_sems.at[sem], device_id=(px, py, pc), device_id_type=MESH))

    def start(self):
        for cp in self.local + self.sends:
            cp.start()

    def wait(self):
        for cp in self.recvs:
            cp.wait_recv()
        for cp in self.sends:
            cp.wait_send()
        for cp in self.local:
            cp.wait()


def _inproj(x2d, g_row, wp):
    n = x2d.shape[0]
    tm, tn = _tile(n, 1024), 1152

    def body(x_ref, g_ref, w_ref, proj_ref, h_ref, hs):
        @pl.when(pl.program_id(1) == 0)
        def _():
            h = _rms(x_ref[...], g_ref[...]).astype(BF16)
            hs[...] = h
            h_ref[...] = h

        proj_ref[...] = jnp.dot(hs[...], w_ref[...], preferred_element_type=F32)

    return pl.pallas_call(
        body, name="inproj", grid=(n // tm, NP // tn),
        in_specs=[pl.BlockSpec((tm, D), lambda i, j: (i, 0)), _full((1, D)), pl.BlockSpec((D, tn), lambda i, j: (0, j))],
        out_specs=[pl.BlockSpec((tm, tn), lambda i, j: (i, j)), pl.BlockSpec((tm, D), lambda i, j: (i, 0))],
        out_shape=[SDS((n, NP), F32), SDS((n, D), BF16)],
        scratch_shapes=[pltpu.VMEM((tm, D), BF16)], compiler_params=_cp(48),
    )(x2d, g_row, wp)


def _mm_tn(a, b, name):
    nn, m = a.shape
    k = b.shape[1]
    tk = 1152 if k % 1152 == 0 else _tile(k, 1024)
    tn = _tile(nn, 1024)

    def body(a_ref, b_ref, o_ref):
        p = _mm(a_ref[...].astype(BF16), b_ref[...].astype(BF16), TN)

        @pl.when(pl.program_id(1) == 0)
        def _():
            o_ref[...] = p

        @pl.when(pl.program_id(1) > 0)
        def _():
            o_ref[...] += p

    return pl.pallas_call(
        body, name=name, grid=(k // tk, nn // tn),
        in_specs=[pl.BlockSpec((tn, m), lambda j, t: (t, 0)), pl.BlockSpec((tn, tk), lambda j, t: (t, j))],
        out_specs=pl.BlockSpec((m, tk), lambda j, t: (0, j)),
        out_shape=SDS((m, k), F32), compiler_params=_cp(48),
    )(a, b)


def _inproj_bwd(dproj, wp, x2d, g_row, dxo, scatter=()):
    n = x2d.shape[0]
    tm, tk = _tile(n, 1024), 1152
    nk = NP // tk
    ns = len(scatter)

    def body(dp_ref, w_ref, x_ref, g_ref, dxo_ref, *rest):
        s_in, (dx_ref, dg_ref), s_out, (acc, *sems) = rest[:ns], rest[ns:ns + 2], rest[ns + 2:2 * ns + 2], rest[2 * ns + 2:]
        i, k = pl.program_id(0), pl.program_id(1)
        _ride_along(_Exchange(True, s_in, s_out, *sems) if ns else None,
                    (i == 0) & (k == 0), (i == n // tm - 1) & (k == nk - 1))
        p = _mm(dp_ref[...], w_ref[...], NT)

        @pl.when(k == 0)
        def _():
            acc[...] = p

        @pl.when(k > 0)
        def _():
            acc[...] += p

        @pl.when(k == nk - 1)
        def _():
            _, vjp = jax.vjp(_rms, x_ref[...], g_ref[...])
            dx, dg = vjp(acc[...])
            dx_ref[...] = dxo_ref[...] + dx

            @pl.when(i == 0)
            def _():
                dg_ref[...] = dg

            @pl.when(i > 0)
            def _():
                dg_ref[...] += dg

    hbm = pl.BlockSpec(memory_space=pltpu.HBM)
    return pl.pallas_call(
        body, name="inproj_bwd_exchange" if ns else "inproj_bwd", grid=(n // tm, nk),
        in_specs=[pl.BlockSpec((tm, tk), lambda i, k: (i, k)), pl.BlockSpec((D, tk), lambda i, k: (0, k)),
                  pl.BlockSpec((tm, D), lambda i, k: (i, 0)), _full((1, D)),
                  pl.BlockSpec((tm, D), lambda i, k: (i, 0))] + [hbm] * ns,
        out_specs=[pl.BlockSpec((tm, D), lambda i, k: (i, 0)), _full((1, D))] + [hbm] * ns,
        out_shape=[SDS((n, D), F32), SDS((1, D), F32)] + [SDS(a.shape, a.dtype) for a in scatter],
        scratch_shapes=[pltpu.VMEM((tm, D), F32)] + (_exchange_sems(ns) if ns else []), compiler_params=_cp(56),
    )(dproj, wp, x2d, g_row, dxo, *scatter)


def _loss_head(x2d, g_row, tgt):
    n = x2d.shape[0]
    tm = _tile(n, 512)

    def body(x_ref, g_ref, t_ref, dx_ref, dg_ref, loss_ref):
        i = pl.program_id(0)
        y, vjp = jax.vjp(_rms, x_ref[...], g_ref[...])
        err = y - t_ref[...]
        part = 0.5 * jnp.sum(jnp.mean(err * err, axis=-1, keepdims=True), axis=0, keepdims=True)
        dx, dg = vjp(err * (1.0 / D))
        dx_ref[...] = dx
        lb = jnp.broadcast_to(part, (8, LANE))

        @pl.when(i == 0)
        def _():
            dg_ref[...] = dg
            loss_ref[...] = lb

        @pl.when(i > 0)
        def _():
            dg_ref[...] += dg
            loss_ref[...] += lb

    return pl.pallas_call(
        body, name="loss_head", grid=(n // tm,),
        in_specs=[pl.BlockSpec((tm, D), lambda i: (i, 0)), _full((1, D)), pl.BlockSpec((tm, D), lambda i: (i, 0))],
        out_specs=[pl.BlockSpec((tm, D), lambda i: (i, 0)), _full((1, D)), _full((8, LANE))],
        out_shape=[SDS((n, D), F32), SDS((1, D), F32), SDS((8, LANE), F32)], compiler_params=_cp(40),
    )(x2d, g_row, tgt)


def _conv_fwd(ext_ref, w_ref, kw, halo, tt):
    acc = None
    for j in range(kw):
        term = w_ref[j:j + 1, :] * ext_ref[pl.ds(halo - (kw - 1) + j, tt), :]
        acc = term if acc is None else acc + term
    return acc


def _conv_bwd_x(dcp_ref, w_ref, kw, halo, tt):
    acc = None
    for j in range(kw):
        term = w_ref[j:j + 1, :] * dcp_ref[pl.ds(kw - 1 - j, tt + halo), :]
        acc = term if acc is None else acc + term
    return acc


def _conv_bwd_w(dc, ext_ref, gw_ref, kw, halo, tt):
    for j in range(kw):
        gw_ref[j:j + 1, :] += jnp.sum(dc * ext_ref[pl.ds(halo - (kw - 1) + j, tt), :], axis=0, keepdims=True)


SUB = 8


def _fill_shifts(sh_ref, rows):
    for s in range(1, SUB):
        sh_ref[s, 0:rows, :] = sh_ref[0, pl.ds(s, rows), :]


def _tap(sh_ref, off, rows):
    return sh_ref[off % SUB, pl.ds(off - off % SUB, rows), :]


def _conv_fwd_sh(ext_sh, w_ref, kw, halo, tt):
    acc = None
    for j in range(kw):
        term = w_ref[j:j + 1, :] * _tap(ext_sh, halo - (kw - 1) + j, tt)
        acc = term if acc is None else acc + term
    return acc


def _conv_bwd_x_sh(dcp_sh, w_ref, kw, halo, tt):
    acc = None
    for j in range(kw):
        term = w_ref[j:j + 1, :] * _tap(dcp_sh, kw - 1 - j, tt + halo)
        acc = term if acc is None else acc + term
    return acc


def _conv_bwd_w_sh(dc, ext_sh, gw_ref, kw, halo, tt):
    for j in range(kw):
        gw_ref[j:j + 1, :] += jnp.sum(dc * _tap(ext_sh, halo - (kw - 1) + j, tt), axis=0, keepdims=True)


def _a_post(c, z, g, b):
    mu = jnp.mean(c, axis=-1, keepdims=True)
    var = jnp.mean(jnp.square(c - mu), axis=-1, keepdims=True)
    a = (c - mu) * lax.rsqrt(var + EPS) * g + b
    return jax.nn.silu(a) * jax.nn.silu(z)


def _a_fwd(proj, dw, b_row, lg, lb, bsz, t):
    n = bsz * t
    tt = _tile(t, 256)
    nt = t // tt
    cb = C_A // 512

    def body(v_ref, g_ref, z_ref, vh_ref, gh_ref, w_ref, b_ref, lg_ref, lb_ref, y_ref, c_ref, ext):
        i = pl.program_id(1)
        ext[0, 0:A_H, :] = jnp.where(i > 0, vh_ref[...] * jax.nn.sigmoid(gh_ref[...]), 0.0)
        ext[0, A_H:, :] = v_ref[...] * jax.nn.sigmoid(g_ref[...])
        _fill_shifts(ext, tt + A_H - SUB)
        c = _conv_fwd_sh(ext, w_ref, A_K, A_H, tt) + b_ref[...]
        c_ref[...] = c
        y_ref[...] = _a_post(c, z_ref[...], lg_ref[...], lb_ref[...])

    def row(b, i):
        return b * nt + i

    def halo(b, i):
        return jnp.maximum((b * t + i * tt) // A_H - 1, 0)

    return pl.pallas_call(
        body, name="a_fwd", grid=(bsz, nt),
        in_specs=[pl.BlockSpec((tt, 512), lambda b, i: (row(b, i), cb)),
                  pl.BlockSpec((tt, 512), lambda b, i: (row(b, i), cb + 1)),
                  pl.BlockSpec((tt, 512), lambda b, i: (row(b, i), cb + 2)),
                  pl.BlockSpec((A_H, 512), lambda b, i: (halo(b, i), cb)),
                  pl.BlockSpec((A_H, 512), lambda b, i: (halo(b, i), cb + 1)),
                  _full((32, 512)), _full((1, 512)), _full((1, 512)), _full((1, 512))],
        out_specs=[pl.BlockSpec((tt, 512), lambda b, i: (row(b, i), 0))] * 2,
        out_shape=[SDS((n, 512), F32)] * 2,
        scratch_shapes=[pltpu.VMEM((SUB, tt + A_H, 512), F32)], compiler_params=_cp(40),
    )(proj, proj, proj, proj, proj, dw, b_row, lg, lb)


def _a_bwd(dproj, proj, conv, dy, dw, lg, lb, bsz, t):
    n = bsz * t
    tt = _tile(t, 256)
    nt = t // tt
    cb = C_A // 512

    def body(dp_any, v_ref, g_ref, z_ref, vh_ref, gh_ref, c_ref, dy_ref, w_ref, lg_ref, lb_ref,
             dp_ref, gw_ref, gb_ref, glg_ref, glb_ref, ext, dcp, dae, carry):
        b, i = pl.program_id(0), pl.program_id(1)
        ti = nt - 1 - i

        @pl.when((b == 0) & (i == 0))
        def _():
            gw_ref[...] = jnp.zeros_like(gw_ref)
            gb_ref[...] = jnp.zeros_like(gb_ref)
            glg_ref[...] = jnp.zeros_like(glg_ref)
            glb_ref[...] = jnp.zeros_like(glb_ref)

        @pl.when(i == 0)
        def _():
            carry[...] = jnp.zeros_like(carry)

        val, glu = v_ref[...], g_ref[...]
        sg = jax.nn.sigmoid(glu)
        ext[0, 0:A_H, :] = jnp.where(ti > 0, vh_ref[...] * jax.nn.sigmoid(gh_ref[...]), 0.0)
        ext[0, A_H:, :] = val * sg
        _fill_shifts(ext, tt + A_H - SUB)
        _, vjp = jax.vjp(_a_post, c_ref[...], z_ref[...], lg_ref[...], lb_ref[...])
        dc, dz, dlg, dlb = vjp(dy_ref[...])
        gb_ref[...] += jnp.sum(dc, axis=0, keepdims=True)
        glg_ref[...] += dlg
        glb_ref[...] += dlb
        dcp[0, 0:A_H, :] = jnp.zeros((A_H, 512), F32)
        dcp[0, A_H:A_H + tt, :] = dc
        dcp[0, A_H + tt:, :] = jnp.zeros((A_H, 512), F32)
        _fill_shifts(dcp, tt + 2 * A_H - SUB)
        _conv_bwd_w_sh(dc, ext, gw_ref, A_K, A_H, tt)
        dae[...] = _conv_bwd_x_sh(dcp, w_ref, A_K, A_H, tt)
        dae[tt:tt + A_H, :] += carry[...]
        carry[...] = dae[0:A_H, :]
        da = dae[A_H:, :]
        dp_ref[:, 0:512] = (da * sg).astype(BF16)
        dp_ref[:, 512:1024] = (da * val * sg * (1.0 - sg)).astype(BF16)
        dp_ref[:, 1024:1536] = dz.astype(BF16)

    def row(b, i):
        return b * nt + (nt - 1 - i)

    def halo(b, i):
        return jnp.maximum((b * t + (nt - 1 - i) * tt) // A_H - 1, 0)

    outs = pl.pallas_call(
        body, name="a_bwd", grid=(bsz, nt),
        in_specs=[pl.BlockSpec(memory_space=pl.ANY),
                  pl.BlockSpec((tt, 512), lambda b, i: (row(b, i), cb)),
                  pl.BlockSpec((tt, 512), lambda b, i: (row(b, i), cb + 1)),
                  pl.BlockSpec((tt, 512), lambda b, i: (row(b, i), cb + 2)),
                  pl.BlockSpec((A_H, 512), lambda b, i: (halo(b, i), cb)),
                  pl.BlockSpec((A_H, 512), lambda b, i: (halo(b, i), cb + 1)),
                  pl.BlockSpec((tt, 512), lambda b, i: (row(b, i), 0)),
                  pl.BlockSpec((tt, 512), lambda b, i: (row(b, i), 0)),
                  _full((32, 512)), _full((1, 512)), _full((1, 512))],
        out_specs=[pl.BlockSpec((tt, 1536), lambda b, i: (row(b, i), C_A // 1536)),
                   _full((32, 512)), _full((1, 512)), _full((1, 512)), _full((1, 512))],
        out_shape=[SDS((n, NP), BF16), SDS((32, 512), F32), SDS((1, 512), F32), SDS((1, 512), F32), SDS((1, 512), F32)],
        input_output_aliases={0: 0},
        scratch_shapes=[pltpu.VMEM((SUB, tt + A_H, 512), F32), pltpu.VMEM((SUB, tt + 2 * A_H, 512), F32),
                        pltpu.VMEM((tt + A_H, 512), F32), pltpu.VMEM((A_H, 512), F32)],
        compiler_params=_cp(48),
    )(dproj, proj, proj, proj, proj, proj, conv, dy, dw, lg, lb)
    return outs


def _b_post(blocks):
    out = []
    for idx, c in enumerate(blocks):
        s = jax.nn.silu(c)
        if idx < 2 * NH:
            s = s * lax.rsqrt(jnp.sum(s * s, axis=-1, keepdims=True) + EPS)
            if idx < NH:
                s = s * (LANE ** -0.5)
        out.append(s)
    return out


def _bprep_fwd(proj, wconv, bsz, t):
    n = bsz * t
    tt = _tile(t, 256)
    nt = t // tt

    def body(x_ref, xh_ref, w_ref, o_ref, c_ref, ext):
        i = pl.program_id(1)
        ext[0:B_H, :] = jnp.where(i > 0, xh_ref[...], 0.0)
        ext[B_H:, :] = x_ref[...]
        c = _conv_fwd(ext, w_ref, B_K, B_H, tt)
        c_ref[...] = c
        outs = _b_post([c[:, LANE * j:LANE * (j + 1)] for j in range(3 * NH)])
        for j, o in enumerate(outs):
            o_ref[:, LANE * j:LANE * (j + 1)] = o

    return pl.pallas_call(
        body, name="bprep_fwd", grid=(bsz, nt),
        in_specs=[pl.BlockSpec((tt, 3072), lambda b, i: (b * nt + i, 0)),
                  pl.BlockSpec((B_H, 3072), lambda b, i: (jnp.maximum((b * t + i * tt) // B_H - 1, 0), 0)),
                  _full((8, 3072))],
        out_specs=[pl.BlockSpec((tt, 3072), lambda b, i: (b * nt + i, 0))] * 2,
        out_shape=[SDS((n, 3072), F32)] * 2,
        scratch_shapes=[pltpu.VMEM((tt + B_H, 3072), F32)], compiler_params=_cp(56),
    )(proj, proj, wconv)


def _bprep_bwd(dproj, proj, conv, dqkvn, wconv, bsz, t):
    n = bsz * t
    tt = _tile(t, 256)
    nt = t // tt

    def body(dp_any, x_ref, xh_ref, c_ref, dq_ref, w_ref, dp_ref, gw_ref, ext, dcp, dae, carry):
        b, i = pl.program_id(0), pl.program_id(1)
        ti = nt - 1 - i

        @pl.when((b == 0) & (i == 0))
        def _():
            gw_ref[...] = jnp.zeros_like(gw_ref)

        @pl.when(i == 0)
        def _():
            carry[...] = jnp.zeros_like(carry)

        ext[0:B_H, :] = jnp.where(ti > 0, xh_ref[...], 0.0)
        ext[B_H:, :] = x_ref[...]
        _, vjp = jax.vjp(_b_post, [c_ref[:, LANE * j:LANE * (j + 1)] for j in range(3 * NH)])
        (dcs,) = vjp([dq_ref[:, LANE * j:LANE * (j + 1)] for j in range(3 * NH)])
        dcp[0:B_H, :] = jnp.zeros((B_H, 3072), F32)
        for j, dcj in enumerate(dcs):
            dcp[B_H:B_H + tt, LANE * j:LANE * (j + 1)] = dcj
        dcp[B_H + tt:, :] = jnp.zeros((B_H, 3072), F32)
        _conv_bwd_w(dcp[B_H:B_H + tt, :], ext, gw_ref, B_K, B_H, tt)
        dae[...] = _conv_bwd_x(dcp, w_ref, B_K, B_H, tt)
        dae[tt:tt + B_H, :] += carry[...]
        carry[...] = dae[0:B_H, :]
        dp_ref[...] = dae[B_H:, :].astype(BF16)

    def row(b, i):
        return b * nt + (nt - 1 - i)

    return pl.pallas_call(
        body, name="bprep_bwd", grid=(bsz, nt),
        in_specs=[pl.BlockSpec(memory_space=pl.ANY),
                  pl.BlockSpec((tt, 3072), lambda b, i: (row(b, i), 0)),
                  pl.BlockSpec((B_H, 3072), lambda b, i: (jnp.maximum((b * t + (nt - 1 - i) * tt) // B_H - 1, 0), 0)),
                  pl.BlockSpec((tt, 3072), lambda b, i: (row(b, i), 0)),
                  pl.BlockSpec((tt, 3072), lambda b, i: (row(b, i), 0)),
                  _full((8, 3072))],
        out_specs=[pl.BlockSpec((tt, 3072), lambda b, i: (row(b, i), 0)), _full((8, 3072))],
        out_shape=[SDS((n, NP), BF16), SDS((8, 3072), F32)],
        input_output_aliases={0: 0},
        scratch_shapes=[pltpu.VMEM((tt + B_H, 3072), F32), pltpu.VMEM((tt + 2 * B_H, 3072), F32),
                        pltpu.VMEM((tt + B_H, 3072), F32), pltpu.VMEM((B_H, 3072), F32)],
        compiler_params=_cp(56),
    )(dproj, proj, proj, conv, dqkvn, wconv)


def _split2(a):
    hi = a.astype(BF16)
    return hi, (a - hi.astype(F32)).astype(BF16)


def _split3(a):
    p1 = a.astype(BF16)
    r1 = a - p1.astype(F32)
    p2 = r1.astype(BF16)
    return p1, p2, (r1 - p2.astype(F32)).astype(BF16)


def _dot3_raw(a, b, dims):
    a1, a2 = _split2(a)
    b1, b2 = _split2(b)
    return _mm(a1, b1, dims) + (_mm(a1, b2, dims) + _mm(a2, b1, dims))


def _dot6(a, b, dims):
    a1, a2, a3 = _split3(a)
    b1, b2, b3 = _split3(b)
    return (_mm(a1, b1, dims) + (_mm(a1, b2, dims) + _mm(a2, b1, dims))
            + (_mm(a1, b3, dims) + _mm(a2, b2, dims) + _mm(a3, b1, dims)))


def _unit_lower_inverse_raw(lmats):
    r = lax.broadcasted_iota(jnp.int32, (CH, CH), 0)
    c = lax.broadcasted_iota(jnp.int32, (CH, CH), 1)
    eye = (r == c).astype(F32)
    blk = jnp.right_shift(r, 4) == jnp.right_shift(c, 4)
    dm = [jnp.where(blk, x, 0.0) for x in lmats]
    om = [a - b for a, b in zip(lmats, dm)]
    d2 = [_dot3_raw(x, x, NN) for x in dm]
    d4 = [_dot3_raw(x, x, NN) for x in d2]
    d8 = [_dot3_raw(x, x, NN) for x in d4]
    p = [_dot3_raw(eye - a, eye + b, NN) for a, b in zip(dm, d2)]
    p = [_dot3_raw(a, eye + b, NN) for a, b in zip(p, d4)]
    p = [_dot3_raw(a, eye + b, NN) for a, b in zip(p, d8)]
    m = [_dot3_raw(a, b, NN) for a, b in zip(p, om)]
    m2 = [_dot3_raw(x, x, NN) for x in m]
    t = [_dot3_raw(eye - a, eye + b, NN) for a, b in zip(m, m2)]
    return [_dot3_raw(a, b, NN) for a, b in zip(t, p)]


@jax.custom_vjp
def _unit_lower_inverse(lmats):
    return _unit_lower_inverse_raw(lmats)


def _unit_lower_inverse_fwd(lmats):
    tinv = _unit_lower_inverse_raw(lmats)
    return tinv, tinv


def _unit_lower_inverse_bwd(tinv, gs):
    x = [_dot6(t, g, TN) for t, g in zip(tinv, gs)]
    return ([-_dot6(a, t, NT) for a, t in zip(x, tinv)],)


_unit_lower_inverse.defvjp(_unit_lower_inverse_fwd, _unit_lower_inverse_bwd)


@jax.custom_vjp
def _saved_unit_lower_inverse(lmats, tinv):
    del lmats
    return tinv


def _saved_unit_lower_inverse_fwd(lmats, tinv):
    del lmats
    return tinv, tinv


def _saved_unit_lower_inverse_bwd(tinv, gs):
    (dl,) = _unit_lower_inverse_bwd(tinv, gs)
    return dl, [jnp.zeros_like(t) for t in tinv]


_saved_unit_lower_inverse.defvjp(_saved_unit_lower_inverse_fwd, _saved_unit_lower_inverse_bwd)


def _tri(lower):
    r = lax.broadcasted_iota(jnp.int32, (CH, CH), 0)
    c = lax.broadcasted_iota(jnp.int32, (CH, CH), 1)
    return ((r >= c) if lower else (r <= c)).astype(BF16)


def _tri_dot(x, lower, dims, tri_first):
    p1, p2, p3 = _split3(x)
    tri = _tri(lower)
    if tri_first:
        return _mm(tri, p1, dims) + (_mm(tri, p2, dims) + _mm(tri, p3, dims))
    return _mm(p1, tri, dims) + (_mm(p2, tri, dims) + _mm(p3, tri, dims))


@jax.custom_vjp
def _cumsum_rows(x):
    return _tri_dot(x, True, NN, True)


def _cumsum_rows_fwd(x):
    return _tri_dot(x, True, NN, True), None


def _cumsum_rows_bwd(_, g):
    return (_tri_dot(g, True, TN, True),)


_cumsum_rows.defvjp(_cumsum_rows_fwd, _cumsum_rows_bwd)


@jax.custom_vjp
def _cumsum_rows_t(x):
    return _tri_dot(x, False, TN, False)


def _cumsum_rows_t_fwd(x):
    return _tri_dot(x, False, TN, False), None


def _cumsum_rows_t_bwd(_, g):
    return (_tri_dot(g, False, NT, True),)


_cumsum_rows_t.defvjp(_cumsum_rows_t_fwd, _cumsum_rows_t_bwd)


def _delta_chunk(ss, qs, ks, vs, ba, zs, alog, dtb, og, tsaved=None):
    heads = range(NH)
    lane = lax.broadcasted_iota(jnp.int32, (1, LANE), 1)
    r = lax.broadcasted_iota(jnp.int32, (CH, CH), 0)
    c = lax.broadcasted_iota(jnp.int32, (CH, CH), 1)
    ri = lax.broadcasted_iota(jnp.int32, (CH, 1), 0)
    incl, strict = r >= c, r > c

    def pick(x, h):
        return jnp.sum(jnp.where(lane == h, x, 0.0), axis=-1, keepdims=True)

    beta = [jax.nn.sigmoid(pick(ba, h)) for h in heads]
    g = [-jnp.exp(pick(alog, h + NH)) * _softplus(pick(ba, h + NH) + pick(dtb, h + NH)) for h in heads]
    gb = [jnp.broadcast_to(x, (CH, CH)) for x in g]
    gca = [_cumsum_rows(x) for x in gb]
    gcr = [_cumsum_rows_t(x) for x in gb]
    gc = [jnp.sum(jnp.where(c == 0, x, 0.0), axis=-1, keepdims=True) for x in gca]
    gl = [jnp.sum(jnp.where(ri == CH - 1, x, 0.0), axis=0, keepdims=True) for x in gc]
    diff = [a - b for a, b in zip(gca, gcr)]
    gam_s = [jnp.where(strict, jnp.exp(jnp.where(strict, x, 0.0)), 0.0) for x in diff]
    gam_i = [jnp.where(incl, jnp.exp(jnp.where(incl, x, 0.0)), 0.0) for x in diff]

    kk = [_mm(k, k, NT) for k in ks]
    lmats = [beta[h] * kk[h] * gam_s[h] for h in heads]
    tinv = _unit_lower_inverse(lmats) if tsaved is None else _saved_unit_lower_inverse(lmats, tsaved)

    eg = [jnp.exp(x) for x in gc]
    u = [_mm(tinv[h], vs[h] * beta[h], NN) for h in heads]
    w = [_mm(tinv[h], ks[h] * (beta[h] * eg[h]), NN) for h in heads]
    qk = [_mm(qs[h], ks[h], NT) * gam_i[h] for h in heads]
    vn = [u[h] - _mm(w[h], ss[h], NN) for h in heads]
    o = [_mm(qs[h] * eg[h], ss[h], NN) + _mm(qk[h], vn[h], NN) for h in heads]
    sn = [jnp.exp(gl[h]) * ss[h] + _mm(ks[h] * jnp.exp(gl[h] - gc[h]), vn[h], TN) for h in heads]
    y = [_rms(o[h], og) * jax.nn.silu(zs[h]) for h in heads]
    return (sn, y), tinv


def _head_blocks(ref, base=0):
    return [ref[:, base + LANE * h:base + LANE * (h + 1)] for h in range(NH)]


def _ride_along(ex, first, last):
    if ex is None:
        return

    @pl.when(first)
    def _():
        ex.start()

    @pl.when(last)
    def _():
        ex.wait()


def _delta_fwd(qkvn, proj, alog, dtb, og, bsz, t, gather=()):
    n = bsz * t
    nc = t // CH
    ng = len(gather)

    def body(q_ref, k_ref, v_ref, ba_ref, z_ref, al_ref, dt_ref, og_ref, *rest):
        g_in, (y_ref, sh_ref, ti_ref) = rest[:ng], rest[ng:ng + 3]
        g_out, (s_scr, *sems) = rest[ng + 3:2 * ng + 3], rest[2 * ng + 3:]
        b, ci = pl.program_id(0), pl.program_id(1)
        _ride_along(_Exchange(False, g_in, g_out, *sems) if ng else None,
                    (b == 0) & (ci == 0), (b == bsz - 1) & (ci == nc - 1))

        @pl.when(ci == 0)
        def _():
            s_scr[...] = jnp.zeros_like(s_scr)

        ss = [s_scr[h] for h in range(NH)]
        for h in range(NH):
            sh_ref[h] = ss[h]
        (sn, y), tinv = _delta_chunk(ss, _head_blocks(q_ref), _head_blocks(k_ref), _head_blocks(v_ref), ba_ref[...],
                                     _head_blocks(z_ref), al_ref[...], dt_ref[...], og_ref[...])
        for h in range(NH):
            s_scr[h] = sn[h]
            ti_ref[h] = tinv[h]
            y_ref[:, LANE * h:LANE * (h + 1)] = y[h]

    def blk(width, col):
        return pl.BlockSpec((CH, width), lambda b, ci: (b * nc + ci, col))

    hbm = pl.BlockSpec(memory_space=pltpu.HBM)
    return pl.pallas_call(
        body, name="delta_fwd_gather" if ng else "delta_fwd", grid=(bsz, nc),
        in_specs=[blk(D, 0), blk(D, 1), blk(D, 2), blk(LANE, C_BA // LANE), blk(D, C_BZ // D),
                  _full((1, LANE)), _full((1, LANE)), _full((1, LANE))] + [hbm] * ng,
        out_specs=[blk(D, 0), pl.BlockSpec((None, None, NH, LANE, LANE), lambda b, ci: (b, ci, 0, 0, 0)),
                   pl.BlockSpec((None, None, NH, CH, CH), lambda b, ci: (b, ci, 0, 0, 0))] + [hbm] * ng,
        out_shape=[SDS((n, D), F32), SDS((bsz, nc, NH, LANE, LANE), F32), SDS((bsz, nc, NH, CH, CH), F32)]
        + [SDS((NDEV,) + a.shape, a.dtype) for a in gather],
        scratch_shapes=[pltpu.VMEM((NH, LANE, LANE), F32)] + (_exchange_sems(ng) if ng else []),
        compiler_params=_cp(40),
    )(qkvn, qkvn, qkvn, proj, proj, alog, dtb, og, *gather)


def _delta_bwd(dproj, qkvn, proj, shist, tsaved, dyb, alog, dtb, og, bsz, t, scatter=()):
    n = bsz * t
    nc = t // CH
    ns = len(scatter)

    def body(dp_any, q_ref, k_ref, v_ref, ba_ref, z_ref, sh_ref, ti_ref, dy_ref, al_ref, dt_ref, og_ref, *rest):
        s_in, (dp_ref, dqkv_ref, gal_ref, gdt_ref, gog_ref) = rest[:ns], rest[ns:ns + 5]
        s_out, (ds_scr, *sems) = rest[ns + 5:2 * ns + 5], rest[2 * ns + 5:]
        b, ci = pl.program_id(0), pl.program_id(1)
        _ride_along(_Exchange(True, s_in, s_out, *sems) if ns else None,
                    (b == 0) & (ci == 0), (b == bsz - 1) & (ci == nc - 1))

        @pl.when((b == 0) & (ci == 0))
        def _():
            gal_ref[...] = jnp.zeros_like(gal_ref)
            gdt_ref[...] = jnp.zeros_like(gdt_ref)
            gog_ref[...] = jnp.zeros_like(gog_ref)

        @pl.when(ci == 0)
        def _():
            ds_scr[...] = jnp.zeros_like(ds_scr)

        _, vjp, _ = jax.vjp(_delta_chunk, [sh_ref[h] for h in range(NH)], _head_blocks(q_ref), _head_blocks(k_ref),
                            _head_blocks(v_ref), ba_ref[...], _head_blocks(z_ref), al_ref[...], dt_ref[...], og_ref[...],
                            [ti_ref[h] for h in range(NH)], has_aux=True)
        ds, dq, dk, dv, dba, dz, dal, ddt, dog, _ = vjp(([ds_scr[h] for h in range(NH)], _head_blocks(dy_ref)))
        gal_ref[...] += dal
        gdt_ref[...] += ddt
        gog_ref[...] += dog
        dp_ref[:, D:D + LANE] = dba.astype(BF16)
        for h in range(NH):
            ds_scr[h] = ds[h]
            dp_ref[:, LANE * h:LANE * (h + 1)] = dz[h].astype(BF16)
            dqkv_ref[:, LANE * h:LANE * (h + 1)] = dq[h]
            dqkv_ref[:, D + LANE * h:D + LANE * (h + 1)] = dk[h]
            dqkv_ref[:, 2 * D + LANE * h:2 * D + LANE * (h + 1)] = dv[h]

    def blk(width, col):
        return pl.BlockSpec((CH, width), lambda b, ci: (b * nc + (nc - 1 - ci), col))

    hbm = pl.BlockSpec(memory_space=pltpu.HBM)
    return pl.pallas_call(
        body, name="delta_bwd_exchange" if ns else "delta_bwd", grid=(bsz, nc),
        in_specs=[pl.BlockSpec(memory_space=pl.ANY), blk(D, 0), blk(D, 1), blk(D, 2), blk(LANE, C_BA // LANE),
                  blk(D, C_BZ // D),
                  pl.BlockSpec((None, None, NH, LANE, LANE), lambda b, ci: (b, nc - 1 - ci, 0, 0, 0)),
                  pl.BlockSpec((None, None, NH, CH, CH), lambda b, ci: (b, nc - 1 - ci, 0, 0, 0)),
                  blk(D, 0), _full((1, LANE)), _full((1, LANE)), _full((1, LANE))] + [hbm] * ns,
        out_specs=[blk(D + LANE, C_BZ // (D + LANE)), blk(3 * D, 0), _full((1, LANE)), _full((1, LANE)),
                   _full((1, LANE))] + [hbm] * ns,
        out_shape=[SDS((n, NP), BF16), SDS((n, 3 * D), F32), SDS((1, LANE), F32), SDS((1, LANE), F32),
                   SDS((1, LANE), F32)] + [SDS(a.shape, a.dtype) for a in scatter],
        input_output_aliases={0: 0},
        scratch_shapes=[pltpu.VMEM((NH, LANE, LANE), F32)] + (_exchange_sems(ns) if ns else []),
        compiler_params=_cp(48),
    )(dproj, qkvn, qkvn, qkvn, proj, proj, shist, tsaved, dyb, alog, dtb, og, *scatter)


def _c_chunk(us, vs, zs, lgs, lbs, ws, bsb):
    gv = [jax.nn.gelu(v) for v in vs]
    width = LANE * len(gv)
    mu = sum(jnp.sum(x, axis=-1, keepdims=True) for x in gv) / width
    var = sum(jnp.sum(jnp.square(x - mu), axis=-1, keepdims=True) for x in gv) / width
    rstd = lax.rsqrt(var + EPS)
    r = lax.broadcasted_iota(jnp.int32, (SG, SG), 0)
    c = lax.broadcasted_iota(jnp.int32, (SG, SG), 1)
    out = []
    for j in range(len(gv)):
        nrm = (gv[j] - mu) * rstd * lgs[j] + lbs[j]
        mixed = jnp.dot(jnp.where(r >= c, ws[j], 0.0), nrm, preferred_element_type=F32) + bsb[j]
        out.append(jax.nn.gelu(us[j]) * mixed * jax.nn.silu(zs[j]))
    return out


def _c_args(u_ref, v_ref, z_ref, lg_ref, lb_ref, ws_ref, bs_ref):
    sl = [slice(LANE * j, LANE * (j + 1)) for j in range(4)]
    return ([u_ref[:, s] for s in sl], [v_ref[:, s] for s in sl], [z_ref[:, s] for s in sl],
            [lg_ref[:, s] for s in sl], [lb_ref[:, s] for s in sl],
            [ws_ref[j] for j in range(4)], [bs_ref[j] for j in range(4)])


def _c_fwd(proj, lg, lb, ws, bsb, n):
    cb = C_C // 512

    def body(u_ref, v_ref, z_ref, lg_ref, lb_ref, ws_ref, bs_ref, y_ref):
        outs = _c_chunk(*_c_args(u_ref, v_ref, z_ref, lg_ref, lb_ref, ws_ref, bs_ref))
        for j, o in enumerate(outs):
            y_ref[:, LANE * j:LANE * (j + 1)] = o

    return pl.pallas_call(
        body, name="c_fwd", grid=(n // SG,),
        in_specs=[pl.BlockSpec((SG, 512), lambda i: (i, cb)), pl.BlockSpec((SG, 512), lambda i: (i, cb + 1)),
                  pl.BlockSpec((SG, 512), lambda i: (i, cb + 2)), _full((1, 512)), _full((1, 512)),
                  _full((4, SG, SG)), _full((4, SG, SG))],
        out_specs=pl.BlockSpec((SG, 512), lambda i: (i, 0)),
        out_shape=SDS((n, 512), F32), compiler_params=_cp(32),
    )(proj, proj, proj, lg, lb, ws, bsb)


def _c_bwd(dproj, proj, dy, lg, lb, ws, bsb, n):
    cb = C_C // 512

    def body(dp_any, u_ref, v_ref, z_ref, dy_ref, lg_ref, lb_ref, ws_ref, bs_ref,
             dp_ref, glg_ref, glb_ref, gws_ref, gbs_ref):
        @pl.when(pl.program_id(0) == 0)
        def _():
            glg_ref[...] = jnp.zeros_like(glg_ref)
            glb_ref[...] = jnp.zeros_like(glb_ref)
            gws_ref[...] = jnp.zeros_like(gws_ref)
            gbs_ref[...] = jnp.zeros_like(gbs_ref)

        _, vjp = jax.vjp(_c_chunk, *_c_args(u_ref, v_ref, z_ref, lg_ref, lb_ref, ws_ref, bs_ref))
        dus, dvs, dzs, dlgs, dlbs, dwss, dbss = vjp([dy_ref[:, LANE * j:LANE * (j + 1)] for j in range(4)])
        for j in range(4):
            sl = slice(LANE * j, LANE * (j + 1))
            dp_ref[:, LANE * j:LANE * (j + 1)] = dus[j].astype(BF16)
            dp_ref[:, 512 + LANE * j:512 + LANE * (j + 1)] = dvs[j].astype(BF16)
            dp_ref[:, 1024 + LANE * j:1024 + LANE * (j + 1)] = dzs[j].astype(BF16)
            glg_ref[:, sl] += dlgs[j]
            glb_ref[:, sl] += dlbs[j]
            gws_ref[j] += dwss[j]
            gbs_ref[j] += jnp.broadcast_to(jnp.sum(dbss[j], axis=-1, keepdims=True), (SG, SG))

    return pl.pallas_call(
        body, name="c_bwd", grid=(n // SG,),
        in_specs=[pl.BlockSpec(memory_space=pl.ANY),
                  pl.BlockSpec((SG, 512), lambda i: (i, cb)), pl.BlockSpec((SG, 512), lambda i: (i, cb + 1)),
                  pl.BlockSpec((SG, 512), lambda i: (i, cb + 2)), pl.BlockSpec((SG, 512), lambda i: (i, 0)),
                  _full((1, 512)), _full((1, 512)), _full((4, SG, SG)), _full((4, SG, SG))],
        out_specs=[pl.BlockSpec((SG, 1536), lambda i: (i, C_C // 1536)),
                   _full((1, 512)), _full((1, 512)), _full((4, SG, SG)), _full((4, SG, SG))],
        out_shape=[SDS((n, NP), BF16), SDS((1, 512), F32), SDS((1, 512), F32), SDS((4, SG, SG), F32), SDS((4, SG, SG), F32)],
        input_output_aliases={0: 0}, compiler_params=_cp(32),
    )(dproj, proj, proj, proj, dy, lg, lb, ws, bsb)


def _merge_fwd(x2d, ya, yb, yc, proj, ap, bp, cp, wo):
    n = x2d.shape[0]
    tm = _tile(n, 512)
    gb = C_G // D

    def body(x_ref, ya_ref, yb_ref, yc_ref, g0_ref, g1_ref, g2_ref, ap_ref, bp_ref, cp_ref, wo_ref, o_ref):
        merged = (jax.nn.sigmoid(g0_ref[...]) * _bdot(ya_ref[...], ap_ref[...])
                  + jax.nn.sigmoid(g1_ref[...]) * _bdot(yb_ref[...], bp_ref[...])
                  + jax.nn.sigmoid(g2_ref[...]) * _bdot(yc_ref[...], cp_ref[...]))
        o_ref[...] = x_ref[...] + _bdot(merged, wo_ref[...])

    def rows(w):
        return pl.BlockSpec((tm, w), lambda i: (i, 0))

    return pl.pallas_call(
        body, name="merge_fwd", grid=(n // tm,),
        in_specs=[rows(D), rows(512), rows(D), rows(512),
                  pl.BlockSpec((tm, D), lambda i: (i, gb)), pl.BlockSpec((tm, D), lambda i: (i, gb + 1)),
                  pl.BlockSpec((tm, D), lambda i: (i, gb + 2)),
                  _resident((512, D)), _resident((D, D)), _resident((512, D)), _resident((D, D))],
        out_specs=rows(D), out_shape=SDS((n, D), F32), compiler_params=_cp(48),
    )(x2d, ya, yb, yc, proj, proj, proj, ap, bp, cp, wo)


def _merge_bwd(dxo, ya, yb, yc, proj, ap, bp, cp, apt, bpt, cpt, wot):
    n = dxo.shape[0]
    tm = _tile(n, 256)
    gb = C_G // D

    def body(d_ref, ya_ref, yb_ref, yc_ref, g0_ref, g1_ref, g2_ref, ap_ref, bp_ref, cp_ref,
             apt_ref, bpt_ref, cpt_ref, wot_ref,
             dp_ref, dya_ref, dyb_ref, dyc_ref, dpa_ref, dpb_ref, dpc_ref, mg_ref):
        dm = _bdot(d_ref[...], wot_ref[...])
        merged = None
        for j, (g_ref, y_ref, w_ref, wt_ref, dy_ref, dpj_ref) in enumerate((
                (g0_ref, ya_ref, ap_ref, apt_ref, dya_ref, dpa_ref),
                (g1_ref, yb_ref, bp_ref, bpt_ref, dyb_ref, dpb_ref),
                (g2_ref, yc_ref, cp_ref, cpt_ref, dyc_ref, dpc_ref))):
            s = jax.nn.sigmoid(g_ref[...])
            pj = _bdot(y_ref[...], w_ref[...])
            merged = s * pj if merged is None else merged + s * pj
            dp_ref[:, D * j:D * (j + 1)] = (dm * pj * s * (1.0 - s)).astype(BF16)
            dpj = (dm * s).astype(BF16)
            dpj_ref[...] = dpj
            dy_ref[...] = jnp.dot(dpj, wt_ref[...], preferred_element_type=F32)
        mg_ref[...] = merged

    def rows(w):
        return pl.BlockSpec((tm, w), lambda i: (i, 0))

    return pl.pallas_call(
        body, name="merge_bwd", grid=(n // tm,),
        in_specs=[rows(D), rows(512), rows(D), rows(512),
                  pl.BlockSpec((tm, D), lambda i: (i, gb)), pl.BlockSpec((tm, D), lambda i: (i, gb + 1)),
                  pl.BlockSpec((tm, D), lambda i: (i, gb + 2)),
                  _resident((512, D)), _resident((D, D)), _resident((512, D)),
                  _resident((D, 512)), _resident((D, D)), _resident((D, 512)), _resident((D, D))],
        out_specs=[pl.BlockSpec((tm, 3 * D), lambda i: (i, C_G // (3 * D))), rows(512), rows(D), rows(512),
                   rows(D), rows(D), rows(D), rows(D)],
        out_shape=[SDS((n, NP), BF16), SDS((n, 512), F32), SDS((n, D), F32), SDS((n, 512), F32),
                   SDS((n, D), BF16), SDS((n, D), BF16), SDS((n, D), BF16), SDS((n, D), F32)],
        compiler_params=_cp(56),
    )(dxo, ya, yb, yc, proj, proj, proj, ap, bp, cp, apt, bpt, cpt, wot)


def _sum_parts(p_ref):
    g = p_ref[0].astype(F32)
    for s in range(1, NDEV):
        g = g + p_ref[s].astype(F32)
    return g


def _adamw(g, w, m, v):
    nm = ADAM_B1 * m + (1.0 - ADAM_B1) * g
    nv = ADAM_B2 * v + (1.0 - ADAM_B2) * jnp.square(g)
    nm_hat = nm / (1.0 - ADAM_B1 ** ADAM_STEP)
    nv_hat = nv / (1.0 - ADAM_B2 ** ADAM_STEP)
    return -ADAM_LR * (nm_hat / (jnp.sqrt(nv_hat) + ADAM_EPS) + ADAM_WD * w), nm, nv


def _reduce_adamw(parts, w, m, v, name):
    r, c = w.shape
    tr = _tile(r, 128)

    def body(p_ref, w_ref, m_ref, v_ref, g_ref, d_ref, nm_ref, nv_ref):
        g = _sum_parts(p_ref)
        g_ref[...] = g
        d_ref[...], nm_ref[...], nv_ref[...] = _adamw(g, w_ref[...], m_ref[...], v_ref[...])

    blk = pl.BlockSpec((tr, c), lambda i: (i, 0))
    return pl.pallas_call(
        body, name=name, grid=(r // tr,),
        in_specs=[pl.BlockSpec((NDEV, tr, c), lambda i: (0, i, 0)), blk, blk, blk],
        out_specs=[blk, blk, blk, blk], out_shape=[SDS((r, c), F32)] * 4, compiler_params=_cp(48),
    )(parts, w, m, v)


def _reduce_adamw_leaves(parts, ws, ms, vs, name):
    nleaf = len(ws)
    counts = [len(p) if isinstance(p, (list, tuple)) else 0 for p in parts]
    flat = [a for p in parts for a in (p if isinstance(p, (list, tuple)) else [p])]

    def body(*refs):
        p_refs, rest = refs[:len(flat)], refs[len(flat):]
        w_refs, m_refs, v_refs = rest[:nleaf], rest[nleaf:2 * nleaf], rest[2 * nleaf:3 * nleaf]
        outs = rest[3 * nleaf:]
        at = 0
        for i in range(nleaf):
            g_ref, d_ref, nm_ref, nv_ref = outs[i], outs[nleaf + i], outs[2 * nleaf + i], outs[3 * nleaf + i]
            for idx in (range(counts[i]) if counts[i] else [Ellipsis]):
                g = _sum_parts(p_refs[at])
                at += 1
                g_ref[idx] = g
                d_ref[idx], nm_ref[idx], nv_ref[idx] = _adamw(g, w_refs[i][idx], m_refs[i][idx], v_refs[i][idx])

    vm = pl.BlockSpec(memory_space=pltpu.VMEM)
    outs = pl.pallas_call(
        body, name=name, in_specs=[vm] * (len(flat) + 3 * nleaf), out_specs=[vm] * (4 * nleaf),
        out_shape=[SDS(w.shape, F32) for w in ws] * 4, compiler_params=_cp(56),
    )(*flat, *ws, *ms, *vs)
    return [outs[j * nleaf:(j + 1) * nleaf] for j in range(4)]


def _unshard(name, g):
    if name in ROW_SHARDED:
        return g.reshape(g.shape[0] * g.shape[1], g.shape[2])
    g = jnp.moveaxis(g, 0, 1)
    return g.reshape(g.shape[0], g.shape[1] * g.shape[2])


def _reshard(name, full):
    r, c = full.shape
    if name in ROW_SHARDED:
        return full.reshape(NDEV, r // NDEV, c)
    return jnp.moveaxis(full.reshape(r, NDEV, c // NDEV), 1, 0)


def _w_in_to_padded(slabs):
    pieces = []
    for lo, hi, _ in sorted(SEGMENTS, key=lambda s: s[2]):
        for d in range(NDEV):
            a, b = max(lo, d * W_SHARD), min(hi, (d + 1) * W_SHARD)
            if a < b:
                pieces.append(slabs[d, :, a - d * W_SHARD:b - d * W_SHARD])
    pieces.append(jnp.zeros(slabs.shape[1:2] + (NP - C_BA - 16,), slabs.dtype))
    return jnp.concatenate(pieces, axis=-1)


def _w_in_from_padded(g):
    slabs = []
    for d in range(NDEV):
        pieces = []
        for lo, hi, pstart in SEGMENTS:
            a, b = max(lo, d * W_SHARD), min(hi, (d + 1) * W_SHARD)
            if a < b:
                pieces.append(g[:, pstart + a - lo:pstart + b - lo])
        pieces.append(jnp.zeros(g.shape[:1] + (W_SHARD_PAD - W_SHARD,), g.dtype))
        slabs.append(jnp.concatenate(pieces, axis=-1))
    return jnp.stack(slabs)


def _pad_w_in(w):
    return jnp.pad(w, ((0, 0), (0, W_SHARD_PAD - W_SHARD)))


def _lane_row(vec8, offset):
    return jnp.pad(vec8, (offset, LANE - NH - offset))[None]


def kernel(x, norm_g, w_in, a_dw, a_dw_b, a_ln_g, a_ln_b, a_proj, b_conv, b_a_log, b_dt_bias, b_onorm_g, b_proj, c_ln_g, c_ln_b, c_ws, c_bs, c_proj, w_out, final_g, loss_target, m_norm_g, m_w_in, m_a_dw, m_a_dw_b, m_a_ln_g, m_a_ln_b, m_a_proj, m_b_conv, m_b_a_log, m_b_dt_bias, m_b_onorm_g, m_b_proj, m_c_ln_g, m_c_ln_b, m_c_ws, m_c_bs, m_c_proj, m_w_out, m_final_g, v_norm_g, v_w_in, v_a_dw, v_a_dw_b, v_a_ln_g, v_a_ln_b, v_a_proj, v_b_conv, v_b_a_log, v_b_dt_bias, v_b_onorm_g, v_b_proj, v_c_ln_g, v_c_ln_b, v_c_ws, v_c_bs, v_c_proj, v_w_out, v_final_g):
    wts = dict(norm_g=norm_g, w_in=w_in, a_dw=a_dw, a_dw_b=a_dw_b, a_ln_g=a_ln_g, a_ln_b=a_ln_b, a_proj=a_proj,
               b_conv=b_conv, b_a_log=b_a_log, b_dt_bias=b_dt_bias, b_onorm_g=b_onorm_g, b_proj=b_proj,
               c_ln_g=c_ln_g, c_ln_b=c_ln_b, c_ws=c_ws, c_bs=c_bs, c_proj=c_proj, w_out=w_out, final_g=final_g)
    mom = dict(norm_g=m_norm_g, w_in=m_w_in, a_dw=m_a_dw, a_dw_b=m_a_dw_b, a_ln_g=m_a_ln_g, a_ln_b=m_a_ln_b,
               a_proj=m_a_proj, b_conv=m_b_conv, b_a_log=m_b_a_log, b_dt_bias=m_b_dt_bias, b_onorm_g=m_b_onorm_g,
               b_proj=m_b_proj, c_ln_g=m_c_ln_g, c_ln_b=m_c_ln_b, c_ws=m_c_ws, c_bs=m_c_bs, c_proj=m_c_proj,
               w_out=m_w_out, final_g=m_final_g)
    vel = dict(norm_g=v_norm_g, w_in=v_w_in, a_dw=v_a_dw, a_dw_b=v_a_dw_b, a_ln_g=v_a_ln_g, a_ln_b=v_a_ln_b,
               a_proj=v_a_proj, b_conv=v_b_conv, b_a_log=v_b_a_log, b_dt_bias=v_b_dt_bias, b_onorm_g=v_b_onorm_g,
               b_proj=v_b_proj, c_ln_g=v_c_ln_g, c_ln_b=v_c_ln_b, c_ws=v_c_ws, c_bs=v_c_bs, c_proj=v_c_proj,
               w_out=v_w_out, final_g=v_final_g)

    bsz, t, _ = x.shape
    n = bsz * t
    depth = norm_g.shape[0]
    x2d = x.reshape(n, D)
    tgt = loss_target.reshape(n, D)

    def weight_blocks(l):
        return [_pad_w_in(w_in[l].astype(BF16))] + [wts[k][l].astype(BF16) for k in BIG_REST]

    def matmul_weights(w_in_all, *rest):
        got = {k: _unshard(k, g) for k, g in zip(BIG_REST, rest)}
        got['wp'] = _w_in_to_padded(w_in_all)
        return got

    conv_all = _all_gather([wts[k] for k in SMALL], "gather_conv_weights")
    conv_w = {k: jnp.stack([_unshard(k, g[:, l]) for l in range(depth)]) for k, g in zip(SMALL, conv_all)}
    a_dw32 = jnp.pad(conv_w['a_dw'], ((0, 0), (0, 32 - A_K), (0, 0)))
    b_conv8 = jnp.pad(conv_w['b_conv'], ((0, 0), (0, 8 - B_K), (0, 0)))
    bsb = jnp.broadcast_to(c_bs[..., None], c_bs.shape + (SG,))
    full = [matmul_weights(*_all_gather(weight_blocks(0), "gather_matmul_weights"))]

    saved = []
    xl = x2d
    for l in range(depth):
        alog, dtb = _lane_row(b_a_log[l], NH), _lane_row(b_dt_bias[l], NH)
        proj, h = _inproj(xl, norm_g[l][None], full[l]['wp'])
        ya, conv_a = _a_fwd(proj, a_dw32[l], a_dw_b[l][None], a_ln_g[l][None], a_ln_b[l][None], bsz, t)
        qkvn, conv_b = _bprep_fwd(proj, b_conv8[l], bsz, t)
        yb, shist, tsave, *nxt = _delta_fwd(qkvn, proj, alog, dtb, b_onorm_g[l][None], bsz, t,
                                            gather=weight_blocks(l + 1) if l + 1 < depth else ())
        if nxt:
            full.append(matmul_weights(*nxt))
        yc = _c_fwd(proj, c_ln_g[l][None], c_ln_b[l][None], c_ws[l], bsb[l], n)
        xn = _merge_fwd(xl, ya, yb, yc, proj, full[l]['a_proj'], full[l]['b_proj'], full[l]['c_proj'], full[l]['w_out'])
        saved.append((xl, proj, h, ya, yb, yc, qkvn, shist, tsave, alog, dtb, conv_a, conv_b))
        xl = xn

    dx, g_final, loss_blk = _loss_head(xl, final_g[None], tgt)
    loss = lax.psum(loss_blk[0, 0], ("x", "y", "c"))

    gfull = {k: [None] * depth for k in WEIGHTS if k != 'final_g'}
    recv = [None] * depth

    def grad_slabs(l):
        return [_w_in_from_padded(gfull['w_in'][l]).astype(BF16)] + [
            _reshard(k, gfull[k][l]).astype(BF16) for k in SHARDED_REST]

    for l in reversed(range(depth)):
        xl, proj, h, ya, yb, yc, qkvn, shist, tsave, alog, dtb, conv_a, conv_b = saved[l]
        ap, bp, cp, wo = full[l]['a_proj'], full[l]['b_proj'], full[l]['c_proj'], full[l]['w_out']
        dproj, dya, dyb, dyc, dpa, dpb, dpc, merged = _merge_bwd(dx, ya, yb, yc, proj, ap, bp, cp, ap.T, bp.T, cp.T, wo.T)
        gfull['a_proj'][l] = _mm_tn(ya, dpa, "grad_a_proj")
        gfull['b_proj'][l] = _mm_tn(yb, dpb, "grad_b_proj")
        gfull['c_proj'][l] = _mm_tn(yc, dpc, "grad_c_proj")
        gfull['w_out'][l] = _mm_tn(merged, dx, "grad_w_out")
        dproj, g_clg, g_clb, g_cws, g_cbs = _c_bwd(dproj, proj, dyc, c_ln_g[l][None], c_ln_b[l][None], c_ws[l], bsb[l], n)
        dproj, g_adw, g_adb, g_alg, g_alb = _a_bwd(dproj, proj, conv_a, dya, a_dw32[l], a_ln_g[l][None], a_ln_b[l][None],
                                                   bsz, t)
        dproj, dqkvn, g_alog, g_dt, g_og, *got = _delta_bwd(dproj, qkvn, proj, shist, tsave, dyb, alog, dtb, b_onorm_g[l][None],
                                                            bsz, t, scatter=grad_slabs(l + 1) if l + 1 < depth else ())
        if got:
            recv[l + 1] = got
        dproj, g_bconv = _bprep_bwd(dproj, proj, conv_b, dqkvn, b_conv8[l], bsz, t)
        gfull['w_in'][l] = _mm_tn(h, dproj, "grad_w_in")
        gfull['a_dw'][l] = g_adw[:A_K]
        gfull['b_conv'][l] = g_bconv[:B_K]
        dx, g_ng, *got = _inproj_bwd(dproj, full[l]['wp'], xl, norm_g[l][None], dx, scatter=grad_slabs(0) if l == 0 else ())
        if got:
            recv[0] = got
        gfull['norm_g'][l] = g_ng[0]
        gfull['a_dw_b'][l], gfull['a_ln_g'][l], gfull['a_ln_b'][l] = g_adb[0], g_alg[0], g_alb[0]
        gfull['b_a_log'][l], gfull['b_dt_bias'][l] = g_alog[0, NH:2 * NH], g_dt[0, NH:2 * NH]
        gfull['b_onorm_g'][l] = g_og[0]
        gfull['c_ln_g'][l], gfull['c_ln_b'][l] = g_clg[0], g_clb[0]
        gfull['c_ws'][l], gfull['c_bs'][l] = g_cws, g_cbs[:, :, 0]
    grad_x = dx.reshape(bsz, t, D)

    outs_w = [_reduce_adamw(recv[l][0], _pad_w_in(w_in[l]), _pad_w_in(m_w_in[l]), _pad_w_in(v_w_in[l]), "adamw_w_in")
              for l in range(depth)]
    outs_s = _reduce_adamw_leaves([[recv[l][1 + i] for l in range(depth)] for i in range(len(SHARDED_REST))],
                                  [wts[k] for k in SHARDED_REST], [mom[k] for k in SHARDED_REST],
                                  [vel[k] for k in SHARDED_REST], "adamw_sharded")

    def upto3d(a):
        return a[None] if a.ndim == 1 else a.reshape((-1,) + a.shape[-2:]) if a.ndim > 3 else a

    grepl = [upto3d(jnp.stack(gfull[k]) if k != 'final_g' else g_final[0]) for k in REPL]
    outs_r = _reduce_adamw_leaves(_all_gather(grepl, "gather_replicated_grads"), [upto3d(wts[k]) for k in REPL],
                                  [upto3d(mom[k]) for k in REPL], [upto3d(vel[k]) for k in REPL], "adamw_replicated")

    res = []
    for j in range(4):
        leaves = {k: outs_r[j][i].reshape(wts[k].shape) for i, k in enumerate(REPL)}
        leaves.update({k: outs_s[j][i] for i, k in enumerate(SHARDED_REST)})
        leaves['w_in'] = jnp.stack([outs_w[l][j][:, :W_SHARD] for l in range(depth)])
        res.append([leaves[k] for k in WEIGHTS])
    grads, deltas, new_m, new_v = res
    return (loss, grad_x, *grads, *deltas, *new_m, *new_v)
```

```python
import jax
import jax.numpy as jnp
from jax import lax
from jax.experimental import pallas as pl
from jax.experimental.pallas import tpu as pltpu

F32 = jnp.float32
BF16 = jnp.bfloat16
SDS = jax.ShapeDtypeStruct
MESH = pl.DeviceIdType.MESH

NDEV = 8
D = 1024
EPS = 1e-6
LANE = 128

C_Q, C_K, C_V = 0, 1024, 2048
C_G = 3072
C_A = 6144
C_C = 7680
C_BZ = 9216
C_BA = 10240
NP = 10368
N_IN = 10256
SEGMENTS = ((0, 1536, C_A), (1536, 4608, C_Q), (4608, 5632, C_BZ), (5632, 5648, C_BA), (5648, 7184, C_C), (7184, 10256, C_G))
W_SHARD = N_IN // NDEV
W_SHARD_PAD = 1408

A_K, A_H = 31, 32
B_K, B_H = 4, 8
CH = 64
SG = 128
NH = 8

ADAM_LR, ADAM_B1, ADAM_B2, ADAM_EPS, ADAM_WD, ADAM_STEP = 0.001, 0.9, 0.999, 1e-08, 0.01, 10

WEIGHTS = ['norm_g', 'w_in', 'a_dw', 'a_dw_b', 'a_ln_g', 'a_ln_b', 'a_proj', 'b_conv', 'b_a_log', 'b_dt_bias',
           'b_onorm_g', 'b_proj', 'c_ln_g', 'c_ln_b', 'c_ws', 'c_bs', 'c_proj', 'w_out', 'final_g']
SHARDED_REST = ['a_dw', 'a_proj', 'b_conv', 'b_proj', 'c_proj', 'w_out']
REPL = [n for n in WEIGHTS if n != 'w_in' and n not in SHARDED_REST]
BIG_REST = ['a_proj', 'b_proj', 'c_proj', 'w_out']
SMALL = ['a_dw', 'b_conv']
ROW_SHARDED = ('b_proj', 'w_out')

NN = ((1,), (0,))
NT = ((1,), (1,))
TN = ((0,), (0,))


def _tile(n, pref):
    return pref if (n >= pref and n % pref == 0) else n


def _cp(vmem_mb):
    return pltpu.CompilerParams(vmem_limit_bytes=vmem_mb * 2 ** 20)


def _full(shape):
    nd = len(shape)
    return pl.BlockSpec(shape, lambda *_: (0,) * nd)


def _resident(shape):
    nd = len(shape)
    return pl.BlockSpec(shape, lambda *_: (0,) * nd, pipeline_mode=pl.Buffered(1))


def _rms(x, g):
    return x * lax.rsqrt(jnp.mean(x * x, axis=-1, keepdims=True) + EPS) * g


def _softplus(x):
    return jnp.maximum(x, 0.0) + jnp.log1p(jnp.exp(-jnp.abs(x)))


def _bdot(a, b):
    return jnp.dot(a.astype(BF16), b.astype(BF16), preferred_element_type=F32)


def _mm(a, b, dims):
    return lax.dot_general(a, b, (dims, ((), ())), preferred_element_type=F32)


def _all_gather(xs, name):
    nops = len(xs)

    def body(*refs):
        x_refs, out_refs = refs[:nops], refs[nops:2 * nops]
        send_sems, recv_sems, local_sems = refs[2 * nops:]
        x, y, cc = lax.axis_index("x"), lax.axis_index("y"), lax.axis_index("c")
        me, sibling = (x, y, cc), (x, y, 1 - cc)
        chips = [(1 - x, y), (x, 1 - y), (1 - x, 1 - y)]

        def slot(t, px, py, pc):
            return out_refs[t].at[4 * px + 2 * py + pc]

        def copy(t, k, block, to, src=None):
            return pltpu.make_async_remote_copy(
                src_ref=slot(t, *block) if src is None else src, dst_ref=slot(t, *block),
                send_sem=send_sems.at[7 * t + k], recv_sem=recv_sems.at[7 * t + k], device_id=to, device_id_type=MESH)

        ops = range(nops)
        mine = [pltpu.make_async_copy(x_refs[t], slot(t, *me), local_sems.at[t]) for t in ops]
        for cp in mine:
            cp.start()
        first = [copy(t, 0, me, sibling, src=x_refs[t]) for t in ops]
        first += [copy(t, 1 + j, me, (*chip, cc), src=x_refs[t]) for j, chip in enumerate(chips) for t in ops]
        for cp in first:
            cp.start()
        passed = []
        for j, chip in enumerate(chips):
            for t in ops:
                copy(t, 1 + j, (*chip, cc), me).wait_recv()
                fwd = copy(t, 4 + j, (*chip, cc), sibling)
                fwd.start()
                passed.append(fwd)
        for t in ops:
            copy(t, 0, sibling, me).wait_recv()
        for j, chip in enumerate(chips):
            for t in ops:
                copy(t, 4 + j, (*chip, 1 - cc), me).wait_recv()
        for cp in first + passed:
            cp.wait_send()
        for cp in mine:
            cp.wait()

    hbm = pl.BlockSpec(memory_space=pltpu.HBM)
    return pl.pallas_call(
        body, name=name, out_shape=[SDS((NDEV,) + a.shape, a.dtype) for a in xs],
        in_specs=[hbm] * nops, out_specs=[hbm] * nops,
        scratch_shapes=[pltpu.SemaphoreType.DMA((7 * nops,)), pltpu.SemaphoreType.DMA((7 * nops,)),
                        pltpu.SemaphoreType.DMA((nops,))],
    )(*xs)


def _exchange_sems(nops):
    return [pltpu.SemaphoreType.DMA((7 * nops,)), pltpu.SemaphoreType.DMA((7 * nops,)), pltpu.SemaphoreType.DMA((nops,))]


class _Exchange:
    def __init__(self, scatter, in_refs, out_refs, send_sems, recv_sems, local_sems):
        x, y, cc = lax.axis_index("x"), lax.axis_index("y"), lax.axis_index("c")
        me = 4 * x + 2 * y + cc
        nops = len(in_refs)
        self.local = [pltpu.make_async_copy(in_refs[t].at[me] if scatter else in_refs[t], out_refs[t].at[me],
                                            local_sems.at[t]) for t in range(nops)]
        self.sends, self.recvs = [], []
        for k in range(1, NDEV):
            px = 1 - x if k & 4 else x
            py = 1 - y if k & 2 else y
            pc = 1 - cc if k & 1 else cc
            peer = 4 * px + 2 * py + pc
            for t in range(nops):
                sem = 7 * t + k - 1
                self.sends.append(pltpu.make_async_remote_copy(
                    src_ref=in_refs[t].at[peer] if scatter else in_refs[t], dst_ref=out_refs[t].at[me],
                    send_sem=send_sems.at[sem], recv_sem=recv_sems.at[sem], device_id=(px, py, pc), device_id_type=MESH))
                self.recvs.append(pltpu.make_async_remote_copy(
                    src_ref=in_refs[t].at[me] if scatter else in_refs[t], dst_ref=out_refs[t].at[peer],
                    send_sem=send_sems.at[sem], recv_sem=recv_sems.at[sem], device_id=(px, py, pc), device_id_type=MESH))

    def start(self):
        for cp in self.local + self.sends:
            cp.start()

    def wait(self):
        for cp in self.recvs:
            cp.wait_recv()
        for cp in self.sends:
            cp.wait_send()
        for cp in self.local:
            cp.wait()


def _inproj(x2d, g_row, wp):
    n = x2d.shape[0]
    tm, tn = _tile(n, 1024), 1152

    def body(x_ref, g_ref, w_ref, proj_ref, h_ref, hs):
        @pl.when(pl.program_id(1) == 0)
        def _():
            h = _rms(x_ref[...], g_ref[...]).astype(BF16)
            hs[...] = h
            h_ref[...] = h

        proj_ref[...] = jnp.dot(hs[...], w_ref[...], preferred_element_type=F32)

    return pl.pallas_call(
        body, name="inproj", grid=(n // tm, NP // tn),
        in_specs=[pl.BlockSpec((tm, D), lambda i, j: (i, 0)), _full((1, D)), pl.BlockSpec((D, tn), lambda i, j: (0, j))],
        out_specs=[pl.BlockSpec((tm, tn), lambda i, j: (i, j)), pl.BlockSpec((tm, D), lambda i, j: (i, 0))],
        out_shape=[SDS((n, NP), F32), SDS((n, D), BF16)],
        scratch_shapes=[pltpu.VMEM((tm, D), BF16)], compiler_params=_cp(48),
    )(x2d, g_row, wp)


def _mm_tn(a, b, name):
    nn, m = a.shape
    k = b.shape[1]
    tk = 1152 if k % 1152 == 0 else _tile(k, 1024)
    tn = _tile(nn, 1024)

    def body(a_ref, b_ref, o_ref):
        p = _mm(a_ref[...].astype(BF16), b_ref[...].astype(BF16), TN)

        @pl.when(pl.program_id(1) == 0)
        def _():
            o_ref[...] = p

        @pl.when(pl.program_id(1) > 0)
        def _():
            o_ref[...] += p

    return pl.pallas_call(
        body, name=name, grid=(k // tk, nn // tn),
        in_specs=[pl.BlockSpec((tn, m), lambda j, t: (t, 0)), pl.BlockSpec((tn, tk), lambda j, t: (t, j))],
        out_specs=pl.BlockSpec((m, tk), lambda j, t: (0, j)),
        out_shape=SDS((m, k), F32), compiler_params=_cp(48),
    )(a, b)


def _inproj_bwd(dproj, wp, x2d, g_row, dxo, scatter=()):
    n = x2d.shape[0]
    tm, tk = _tile(n, 1024), 1152
    nk = NP // tk
    ns = len(scatter)

    def body(dp_ref, w_ref, x_ref, g_ref, dxo_ref, *rest):
        s_in, (dx_ref, dg_ref), s_out, (acc, *sems) = rest[:ns], rest[ns:ns + 2], rest[ns + 2:2 * ns + 2], rest[2 * ns + 2:]
        i, k = pl.program_id(0), pl.program_id(1)
        _ride_along(_Exchange(True, s_in, s_out, *sems) if ns else None,
                    (i == 0) & (k == 0), (i == n // tm - 1) & (k == nk - 1))
        p = _mm(dp_ref[...], w_ref[...], NT)

        @pl.when(k == 0)
        def _():
            acc[...] = p

        @pl.when(k > 0)
        def _():
            acc[...] += p

        @pl.when(k == nk - 1)
        def _():
            _, vjp = jax.vjp(_rms, x_ref[...], g_ref[...])
            dx, dg = vjp(acc[...])
            dx_ref[...] = dxo_ref[...] + dx

            @pl.when(i == 0)
            def _():
                dg_ref[...] = dg

            @pl.when(i > 0)
            def _():
                dg_ref[...] += dg

    hbm = pl.BlockSpec(memory_space=pltpu.HBM)
    return pl.pallas_call(
        body, name="inproj_bwd_exchange" if ns else "inproj_bwd", grid=(n // tm, nk),
        in_specs=[pl.BlockSpec((tm, tk), lambda i, k: (i, k)), pl.BlockSpec((D, tk), lambda i, k: (0, k)),
                  pl.BlockSpec((tm, D), lambda i, k: (i, 0)), _full((1, D)),
                  pl.BlockSpec((tm, D), lambda i, k: (i, 0))] + [hbm] * ns,
        out_specs=[pl.BlockSpec((tm, D), lambda i, k: (i, 0)), _full((1, D))] + [hbm] * ns,
        out_shape=[SDS((n, D), F32), SDS((1, D), F32)] + [SDS(a.shape, a.dtype) for a in scatter],
        scratch_shapes=[pltpu.VMEM((tm, D), F32)] + (_exchange_sems(ns) if ns else []), compiler_params=_cp(56),
    )(dproj, wp, x2d, g_row, dxo, *scatter)


def _loss_head(x2d, g_row, tgt):
    n = x2d.shape[0]
    tm = _tile(n, 512)

    def body(x_ref, g_ref, t_ref, dx_ref, dg_ref, loss_ref):
        i = pl.program_id(0)
        y, vjp = jax.vjp(_rms, x_ref[...], g_ref[...])
        err = y - t_ref[...]
        part = 0.5 * jnp.sum(jnp.mean(err * err, axis=-1, keepdims=True), axis=0, keepdims=True)
        dx, dg = vjp(err * (1.0 / D))
        dx_ref[...] = dx
        lb = jnp.broadcast_to(part, (8, LANE))

        @pl.when(i == 0)
        def _():
            dg_ref[...] = dg
            loss_ref[...] = lb

        @pl.when(i > 0)
        def _():
            dg_ref[...] += dg
            loss_ref[...] += lb

    return pl.pallas_call(
        body, name="loss_head", grid=(n // tm,),
        in_specs=[pl.BlockSpec((tm, D), lambda i: (i, 0)), _full((1, D)), pl.BlockSpec((tm, D), lambda i: (i, 0))],
        out_specs=[pl.BlockSpec((tm, D), lambda i: (i, 0)), _full((1, D)), _full((8, LANE))],
        out_shape=[SDS((n, D), F32), SDS((1, D), F32), SDS((8, LANE), F32)], compiler_params=_cp(40),
    )(x2d, g_row, tgt)


def _conv_fwd(ext_ref, w_ref, kw, halo, tt):
    acc = None
    for j in range(kw):
        term = w_ref[j:j + 1, :] * ext_ref[pl.ds(halo - (kw - 1) + j, tt), :]
        acc = term if acc is None else acc + term
    return acc


def _conv_bwd_x(dcp_ref, w_ref, kw, halo, tt):
    acc = None
    for j in range(kw):
        term = w_ref[j:j + 1, :] * dcp_ref[pl.ds(kw - 1 - j, tt + halo), :]
        acc = term if acc is None else acc + term
    return acc


def _conv_bwd_w(dc, ext_ref, gw_ref, kw, halo, tt):
    for j in range(kw):
        gw_ref[j:j + 1, :] += jnp.sum(dc * ext_ref[pl.ds(halo - (kw - 1) + j, tt), :], axis=0, keepdims=True)


SUB = 8


def _fill_shifts(sh_ref, rows):
    for s in range(1, SUB):
        sh_ref[s, 0:rows, :] = sh_ref[0, pl.ds(s, rows), :]


def _tap(sh_ref, off, rows):
    return sh_ref[off % SUB, pl.ds(off - off % SUB, rows), :]


def _conv_fwd_sh(ext_sh, w_ref, kw, halo, tt):
    acc = None
    for j in range(kw):
        term = w_ref[j:j + 1, :] * _tap(ext_sh, halo - (kw - 1) + j, tt)
        acc = term if acc is None else acc + term
    return acc


def _conv_bwd_x_sh(dcp_sh, w_ref, kw, halo, tt):
    acc = None
    for j in range(kw):
        term = w_ref[j:j + 1, :] * _tap(dcp_sh, kw - 1 - j, tt + halo)
        acc = term if acc is None else acc + term
    return acc


def _conv_bwd_w_sh(dc, ext_sh, gw_ref, kw, halo, tt):
    for j in range(kw):
        gw_ref[j:j + 1, :] += jnp.sum(dc * _tap(ext_sh, halo - (kw - 1) + j, tt), axis=0, keepdims=True)


def _a_post(c, z, g, b):
    mu = jnp.mean(c, axis=-1, keepdims=True)
    var = jnp.mean(jnp.square(c - mu), axis=-1, keepdims=True)
    a = (c - mu) * lax.rsqrt(var + EPS) * g + b
    return jax.nn.silu(a) * jax.nn.silu(z)


def _a_fwd(proj, dw, b_row, lg, lb, bsz, t):
    n = bsz * t
    tt = _tile(t, 256)
    nt = t // tt
    cb = C_A // 512

    def body(v_ref, g_ref, z_ref, vh_ref, gh_ref, w_ref, b_ref, lg_ref, lb_ref, y_ref, c_ref, ext):
        i = pl.program_id(1)
        ext[0, 0:A_H, :] = jnp.where(i > 0, vh_ref[...] * jax.nn.sigmoid(gh_ref[...]), 0.0)
        ext[0, A_H:, :] = v_ref[...] * jax.nn.sigmoid(g_ref[...])
        _fill_shifts(ext, tt + A_H - SUB)
        c = _conv_fwd_sh(ext, w_ref, A_K, A_H, tt) + b_ref[...]
        c_ref[...] = c
        y_ref[...] = _a_post(c, z_ref[...], lg_ref[...], lb_ref[...])

    def row(b, i):
        return b * nt + i

    def halo(b, i):
        return jnp.maximum((b * t + i * tt) // A_H - 1, 0)

    return pl.pallas_call(
        body, name="a_fwd", grid=(bsz, nt),
        in_specs=[pl.BlockSpec((tt, 512), lambda b, i: (row(b, i), cb)),
                  pl.BlockSpec((tt, 512), lambda b, i: (row(b, i), cb + 1)),
                  pl.BlockSpec((tt, 512), lambda b, i: (row(b, i), cb + 2)),
                  pl.BlockSpec((A_H, 512), lambda b, i: (halo(b, i), cb)),
                  pl.BlockSpec((A_H, 512), lambda b, i: (halo(b, i), cb + 1)),
                  _full((32, 512)), _full((1, 512)), _full((1, 512)), _full((1, 512))],
        out_specs=[pl.BlockSpec((tt, 512), lambda b, i: (row(b, i), 0))] * 2,
        out_shape=[SDS((n, 512), F32)] * 2,
        scratch_shapes=[pltpu.VMEM((SUB, tt + A_H, 512), F32)], compiler_params=_cp(40),
    )(proj, proj, proj, proj, proj, dw, b_row, lg, lb)


def _a_bwd(dproj, proj, conv, dy, dw, lg, lb, bsz, t):
    n = bsz * t
    tt = _tile(t, 256)
    nt = t // tt
    cb = C_A // 512

    def body(dp_any, v_ref, g_ref, z_ref, vh_ref, gh_ref, c_ref, dy_ref, w_ref, lg_ref, lb_ref,
             dp_ref, gw_ref, gb_ref, glg_ref, glb_ref, ext, dcp, dae, carry):
        b, i = pl.program_id(0), pl.program_id(1)
        ti = nt - 1 - i

        @pl.when((b == 0) & (i == 0))
        def _():
            gw_ref[...] = jnp.zeros_like(gw_ref)
            gb_ref[...] = jnp.zeros_like(gb_ref)
            glg_ref[...] = jnp.zeros_like(glg_ref)
            glb_ref[...] = jnp.zeros_like(glb_ref)

        @pl.when(i == 0)
        def _():
            carry[...] = jnp.zeros_like(carry)

        val, glu = v_ref[...], g_ref[...]
        sg = jax.nn.sigmoid(glu)
        ext[0, 0:A_H, :] = jnp.where(ti > 0, vh_ref[...] * jax.nn.sigmoid(gh_ref[...]), 0.0)
        ext[0, A_H:, :] = val * sg
        _fill_shifts(ext, tt + A_H - SUB)
        _, vjp = jax.vjp(_a_post, c_ref[...], z_ref[...], lg_ref[...], lb_ref[...])
        dc, dz, dlg, dlb = vjp(dy_ref[...])
        gb_ref[...] += jnp.sum(dc, axis=0, keepdims=True)
        glg_ref[...] += dlg
        glb_ref[...] += dlb
        dcp[0, 0:A_H, :] = jnp.zeros((A_H, 512), F32)
        dcp[0, A_H:A_H + tt, :] = dc
        dcp[0, A_H + tt:, :] = jnp.zeros((A_H, 512), F32)
        _fill_shifts(dcp, tt + 2 * A_H - SUB)
        _conv_bwd_w_sh(dc, ext, gw_ref, A_K, A_H, tt)
        dae[...] = _conv_bwd_x_sh(dcp, w_ref, A_K, A_H, tt)
        dae[tt:tt + A_H, :] += carry[...]
        carry[...] = dae[0:A_H, :]
        da = dae[A_H:, :]
        dp_ref[:, 0:512] = (da * sg).astype(BF16)
        dp_ref[:, 512:1024] = (da * val * sg * (1.0 - sg)).astype(BF16)
        dp_ref[:, 1024:1536] = dz.astype(BF16)

    def row(b, i):
        return b * nt + (nt - 1 - i)

    def halo(b, i):
        return jnp.maximum((b * t + (nt - 1 - i) * tt) // A_H - 1, 0)

    outs = pl.pallas_call(
        body, name="a_bwd", grid=(bsz, nt),
        in_specs=[pl.BlockSpec(memory_space=pl.ANY),
                  pl.BlockSpec((tt, 512), lambda b, i: (row(b, i), cb)),
                  pl.BlockSpec((tt, 512), lambda b, i: (row(b, i), cb + 1)),
                  pl.BlockSpec((tt, 512), lambda b, i: (row(b, i), cb + 2)),
                  pl.BlockSpec((A_H, 512), lambda b, i: (halo(b, i), cb)),
                  pl.BlockSpec((A_H, 512), lambda b, i: (halo(b, i), cb + 1)),
                  pl.BlockSpec((tt, 512), lambda b, i: (row(b, i), 0)),
                  pl.BlockSpec((tt, 512), lambda b, i: (row(b, i), 0)),
                  _full((32, 512)), _full((1, 512)), _full((1, 512))],
        out_specs=[pl.BlockSpec((tt, 1536), lambda b, i: (row(b, i), C_A // 1536)),
                   _full((32, 512)), _full((1, 512)), _full((1, 512)), _full((1, 512))],
        out_shape=[SDS((n, NP), BF16), SDS((32, 512), F32), SDS((1, 512), F32), SDS((1, 512), F32), SDS((1, 512), F32)],
        input_output_aliases={0: 0},
        scratch_shapes=[pltpu.VMEM((SUB, tt + A_H, 512), F32), pltpu.VMEM((SUB, tt + 2 * A_H, 512), F32),
                        pltpu.VMEM((tt + A_H, 512), F32), pltpu.VMEM((A_H, 512), F32)],
        compiler_params=_cp(48),
    )(dproj, proj, proj, proj, proj, proj, conv, dy, dw, lg, lb)
    return outs


def _b_post(blocks):
    out = []
    for idx, c in enumerate(blocks):
        s = jax.nn.silu(c)
        if idx < 2 * NH:
            s = s * lax.rsqrt(jnp.sum(s * s, axis=-1, keepdims=True) + EPS)
            if idx < NH:
                s = s * (LANE ** -0.5)
        out.append(s)
    return out


def _bprep_fwd(proj, wconv, bsz, t):
    n = bsz * t
    tt = _tile(t, 256)
    nt = t // tt

    def body(x_ref, xh_ref, w_ref, o_ref, c_ref, ext):
        i = pl.program_id(1)
        ext[0:B_H, :] = jnp.where(i > 0, xh_ref[...], 0.0)
        ext[B_H:, :] = x_ref[...]
        c = _conv_fwd(ext, w_ref, B_K, B_H, tt)
        c_ref[...] = c
        outs = _b_post([c[:, LANE * j:LANE * (j + 1)] for j in range(3 * NH)])
        for j, o in enumerate(outs):
            o_ref[:, LANE * j:LANE * (j + 1)] = o

    return pl.pallas_call(
        body, name="bprep_fwd", grid=(bsz, nt),
        in_specs=[pl.BlockSpec((tt, 3072), lambda b, i: (b * nt + i, 0)),
                  pl.BlockSpec((B_H, 3072), lambda b, i: (jnp.maximum((b * t + i * tt) // B_H - 1, 0), 0)),
                  _full((8, 3072))],
        out_specs=[pl.BlockSpec((tt, 3072), lambda b, i: (b * nt + i, 0))] * 2,
        out_shape=[SDS((n, 3072), F32)] * 2,
        scratch_shapes=[pltpu.VMEM((tt + B_H, 3072), F32)], compiler_params=_cp(56),
    )(proj, proj, wconv)


def _bprep_bwd(dproj, proj, conv, dqkvn, wconv, bsz, t):
    n = bsz * t
    tt = _tile(t, 256)
    nt = t // tt

    def body(dp_any, x_ref, xh_ref, c_ref, dq_ref, w_ref, dp_ref, gw_ref, ext, dcp, dae, carry):
        b, i = pl.program_id(0), pl.program_id(1)
        ti = nt - 1 - i

        @pl.when((b == 0) & (i == 0))
        def _():
            gw_ref[...] = jnp.zeros_like(gw_ref)

        @pl.when(i == 0)
        def _():
            carry[...] = jnp.zeros_like(carry)

        ext[0:B_H, :] = jnp.where(ti > 0, xh_ref[...], 0.0)
        ext[B_H:, :] = x_ref[...]
        _, vjp = jax.vjp(_b_post, [c_ref[:, LANE * j:LANE * (j + 1)] for j in range(3 * NH)])
        (dcs,) = vjp([dq_ref[:, LANE * j:LANE * (j + 1)] for j in range(3 * NH)])
        dcp[0:B_H, :] = jnp.zeros((B_H, 3072), F32)
        for j, dcj in enumerate(dcs):
            dcp[B_H:B_H + tt, LANE * j:LANE * (j + 1)] = dcj
        dcp[B_H + tt:, :] = jnp.zeros((B_H, 3072), F32)
        _conv_bwd_w(dcp[B_H:B_H + tt, :], ext, gw_ref, B_K, B_H, tt)
        dae[...] = _conv_bwd_x(dcp, w_ref, B_K, B_H, tt)
        dae[tt:tt + B_H, :] += carry[...]
        carry[...] = dae[0:B_H, :]
        dp_ref[...] = dae[B_H:, :].astype(BF16)

    def row(b, i):
        return b * nt + (nt - 1 - i)

    return pl.pallas_call(
        body, name="bprep_bwd", grid=(bsz, nt),
        in_specs=[pl.BlockSpec(memory_space=pl.ANY),
                  pl.BlockSpec((tt, 3072), lambda b, i: (row(b, i), 0)),
                  pl.BlockSpec((B_H, 3072), lambda b, i: (jnp.maximum((b * t + (nt - 1 - i) * tt) // B_H - 1, 0), 0)),
                  pl.BlockSpec((tt, 3072), lambda b, i: (row(b, i), 0)),
                  pl.BlockSpec((tt, 3072), lambda b, i: (row(b, i), 0)),
                  _full((8, 3072))],
        out_specs=[pl.BlockSpec((tt, 3072), lambda b, i: (row(b, i), 0)), _full((8, 3072))],
        out_shape=[SDS((n, NP), BF16), SDS((8, 3072), F32)],
        input_output_aliases={0: 0},
        scratch_shapes=[pltpu.VMEM((tt + B_H, 3072), F32), pltpu.VMEM((tt + 2 * B_H, 3072), F32),
                        pltpu.VMEM((tt + B_H, 3072), F32), pltpu.VMEM((B_H, 3072), F32)],
        compiler_params=_cp(56),
    )(dproj, proj, proj, conv, dqkvn, wconv)


def _split2(a):
    hi = a.astype(BF16)
    return hi, (a - hi.astype(F32)).astype(BF16)


def _split3(a):
    p1 = a.astype(BF16)
    r1 = a - p1.astype(F32)
    p2 = r1.astype(BF16)
    return p1, p2, (r1 - p2.astype(F32)).astype(BF16)


def _dot3_raw(a, b, dims):
    a1, a2 = _split2(a)
    b1, b2 = _split2(b)
    return _mm(a1, b1, dims) + (_mm(a1, b2, dims) + _mm(a2, b1, dims))


def _dot6(a, b, dims):
    a1, a2, a3 = _split3(a)
    b1, b2, b3 = _split3(b)
    return (_mm(a1, b1, dims) + (_mm(a1, b2, dims) + _mm(a2, b1, dims))
            + (_mm(a1, b3, dims) + _mm(a2, b2, dims) + _mm(a3, b1, dims)))


def _unit_lower_inverse_raw(lmats):
    r = lax.broadcasted_iota(jnp.int32, (CH, CH), 0)
    c = lax.broadcasted_iota(jnp.int32, (CH, CH), 1)
    eye = (r == c).astype(F32)
    blk = jnp.right_shift(r, 4) == jnp.right_shift(c, 4)
    dm = [jnp.where(blk, x, 0.0) for x in lmats]
    om = [a - b for a, b in zip(lmats, dm)]
    d2 = [_dot3_raw(x, x, NN) for x in dm]
    d4 = [_dot3_raw(x, x, NN) for x in d2]
    d8 = [_dot3_raw(x, x, NN) for x in d4]
    p = [_dot3_raw(eye - a, eye + b, NN) for a, b in zip(dm, d2)]
    p = [_dot3_raw(a, eye + b, NN) for a, b in zip(p, d4)]
    p = [_dot3_raw(a, eye + b, NN) for a, b in zip(p, d8)]
    m = [_dot3_raw(a, b, NN) for a, b in zip(p, om)]
    m2 = [_dot3_raw(x, x, NN) for x in m]
    t = [_dot3_raw(eye - a, eye + b, NN) for a, b in zip(m, m2)]
    return [_dot3_raw(a, b, NN) for a, b in zip(t, p)]


@jax.custom_vjp
def _unit_lower_inverse(lmats):
    return _unit_lower_inverse_raw(lmats)


def _unit_lower_inverse_fwd(lmats):
    tinv = _unit_lower_inverse_raw(lmats)
    return tinv, tinv


def _unit_lower_inverse_bwd(tinv, gs):
    x = [_dot6(t, g, TN) for t, g in zip(tinv, gs)]
    return ([-_dot6(a, t, NT) for a, t in zip(x, tinv)],)


_unit_lower_inverse.defvjp(_unit_lower_inverse_fwd, _unit_lower_inverse_bwd)


@jax.custom_vjp
def _saved_unit_lower_inverse(lmats, tinv):
    del lmats
    return tinv


def _saved_unit_lower_inverse_fwd(lmats, tinv):
    del lmats
    return tinv, tinv


def _saved_unit_lower_inverse_bwd(tinv, gs):
    (dl,) = _unit_lower_inverse_bwd(tinv, gs)
    return dl, [jnp.zeros_like(t) for t in tinv]


_saved_unit_lower_inverse.defvjp(_saved_unit_lower_inverse_fwd, _saved_unit_lower_inverse_bwd)


def _tri(lower):
    r = lax.broadcasted_iota(jnp.int32, (CH, CH), 0)
    c = lax.broadcasted_iota(jnp.int32, (CH, CH), 1)
    return ((r >= c) if lower else (r <= c)).astype(BF16)


def _tri_dot(x, lower, dims, tri_first):
    p1, p2, p3 = _split3(x)
    tri = _tri(lower)
    if tri_first:
        return _mm(tri, p1, dims) + (_mm(tri, p2, dims) + _mm(tri, p3, dims))
    return _mm(p1, tri, dims) + (_mm(p2, tri, dims) + _mm(p3, tri, dims))


@jax.custom_vjp
def _cumsum_rows(x):
    return _tri_dot(x, True, NN, True)


def _cumsum_rows_fwd(x):
    return _tri_dot(x, True, NN, True), None


def _cumsum_rows_bwd(_, g):
    return (_tri_dot(g, True, TN, True),)


_cumsum_rows.defvjp(_cumsum_rows_fwd, _cumsum_rows_bwd)


@jax.custom_vjp
def _cumsum_rows_t(x):
    return _tri_dot(x, False, TN, False)


def _cumsum_rows_t_fwd(x):
    return _tri_dot(x, False, TN, False), None


def _cumsum_rows_t_bwd(_, g):
    return (_tri_dot(g, False, NT, True),)


_cumsum_rows_t.defvjp(_cumsum_rows_t_fwd, _cumsum_rows_t_bwd)


def _delta_chunk(ss, qs, ks, vs, bas, zs, alog, dtb, og, tsaved=None):
    heads = range(len(ss))
    lane = lax.broadcasted_iota(jnp.int32, (1, LANE), 1)
    r = lax.broadcasted_iota(jnp.int32, (CH, CH), 0)
    c = lax.broadcasted_iota(jnp.int32, (CH, CH), 1)
    ri = lax.broadcasted_iota(jnp.int32, (CH, 1), 0)
    incl, strict = r >= c, r > c

    def pick(x, h):
        return jnp.sum(jnp.where(lane == h, x, 0.0), axis=-1, keepdims=True)

    beta = [jax.nn.sigmoid(pick(bas[e // NH], e % NH)) for e in heads]
    g = [-jnp.exp(pick(alog, e % NH + NH)) * _softplus(pick(bas[e // NH], e % NH + NH) + pick(dtb, e % NH + NH))
         for e in heads]
    gb = [jnp.broadcast_to(x, (CH, CH)) for x in g]
    gca = [_cumsum_rows(x) for x in gb]
    gcr = [_cumsum_rows_t(x) for x in gb]
    gc = [jnp.sum(jnp.where(c == 0, x, 0.0), axis=-1, keepdims=True) for x in gca]
    gl = [jnp.sum(jnp.where(ri == CH - 1, x, 0.0), axis=0, keepdims=True) for x in gc]
    diff = [a - b for a, b in zip(gca, gcr)]
    gam_s = [jnp.where(strict, jnp.exp(jnp.where(strict, x, 0.0)), 0.0) for x in diff]
    gam_i = [jnp.where(incl, jnp.exp(jnp.where(incl, x, 0.0)), 0.0) for x in diff]

    kk = [_mm(k, k, NT) for k in ks]
    lmats = [beta[h] * kk[h] * gam_s[h] for h in heads]
    tinv = _unit_lower_inverse(lmats) if tsaved is None else _saved_unit_lower_inverse(lmats, tsaved)

    eg = [jnp.exp(x) for x in gc]
    u = [_mm(tinv[h], vs[h] * beta[h], NN) for h in heads]
    w = [_mm(tinv[h], ks[h] * (beta[h] * eg[h]), NN) for h in heads]
    qk = [_mm(qs[h], ks[h], NT) * gam_i[h] for h in heads]
    vn = [u[h] - _mm(w[h], ss[h], NN) for h in heads]
    o = [_mm(qs[h] * eg[h], ss[h], NN) + _mm(qk[h], vn[h], NN) for h in heads]
    sn = [jnp.exp(gl[h]) * ss[h] + _mm(ks[h] * jnp.exp(gl[h] - gc[h]), vn[h], TN) for h in heads]
    y = [_rms(o[h], og) * jax.nn.silu(zs[h]) for h in heads]
    return (sn, y), tinv


def _chain_blocks(ref):
    return [ref[b, :, LANE * h:LANE * (h + 1)] for b in range(ref.shape[0]) for h in range(NH)]


def _ride_along(ex, first, last):
    if ex is None:
        return

    @pl.when(first)
    def _():
        ex.start()

    @pl.when(last)
    def _():
        ex.wait()


def _delta_fwd(qkvn, proj, alog, dtb, og, bsz, t, gather=()):
    n = bsz * t
    nc = t // CH
    ng = len(gather)

    def body(q_ref, k_ref, v_ref, ba_ref, z_ref, al_ref, dt_ref, og_ref, *rest):
        g_in, (y_ref, sh_ref, ti_ref) = rest[:ng], rest[ng:ng + 3]
        g_out, (s_scr, *sems) = rest[ng + 3:2 * ng + 3], rest[2 * ng + 3:]
        ci = pl.program_id(0)
        _ride_along(_Exchange(False, g_in, g_out, *sems) if ng else None, ci == 0, ci == nc - 1)

        @pl.when(ci == 0)
        def _():
            s_scr[...] = jnp.zeros_like(s_scr)

        chains = range(bsz * NH)
        ss = [s_scr[e] for e in chains]
        for e in chains:
            sh_ref[e // NH, e % NH] = ss[e]
        (sn, y), tinv = _delta_chunk(ss, _chain_blocks(q_ref), _chain_blocks(k_ref), _chain_blocks(v_ref),
                                     [ba_ref[b] for b in range(bsz)], _chain_blocks(z_ref),
                                     al_ref[...], dt_ref[...], og_ref[...])
        for e in chains:
            s_scr[e] = sn[e]
            ti_ref[e // NH, e % NH] = tinv[e]
            y_ref[e // NH, :, LANE * (e % NH):LANE * (e % NH + 1)] = y[e]

    def blk(width, col):
        return pl.BlockSpec((bsz, CH, width), lambda ci: (0, ci, col))

    def per_chunk(rows):
        return pl.BlockSpec((bsz, None, NH, rows, rows), lambda ci: (0, ci, 0, 0, 0))

    hbm = pl.BlockSpec(memory_space=pltpu.HBM)
    qkv3, proj3 = qkvn.reshape(bsz, t, 3 * D), proj.reshape(bsz, t, NP)
    y, *rest = pl.pallas_call(
        body, name="delta_fwd_gather" if ng else "delta_fwd", grid=(nc,),
        in_specs=[blk(D, 0), blk(D, 1), blk(D, 2), blk(LANE, C_BA // LANE), blk(D, C_BZ // D),
                  _full((1, LANE)), _full((1, LANE)), _full((1, LANE))] + [hbm] * ng,
        out_specs=[blk(D, 0), per_chunk(LANE), per_chunk(CH)] + [hbm] * ng,
        out_shape=[SDS((bsz, t, D), F32), SDS((bsz, nc, NH, LANE, LANE), F32), SDS((bsz, nc, NH, CH, CH), F32)]
        + [SDS((NDEV,) + a.shape, a.dtype) for a in gather],
        scratch_shapes=[pltpu.VMEM((bsz * NH, LANE, LANE), F32)] + (_exchange_sems(ng) if ng else []),
        compiler_params=_cp(48),
    )(qkv3, qkv3, qkv3, proj3, proj3, alog, dtb, og, *gather)
    return (y.reshape(n, D), *rest)


def _delta_bwd(dproj, qkvn, proj, shist, tsaved, dyb, alog, dtb, og, bsz, t, scatter=()):
    n = bsz * t
    nc = t // CH
    ns = len(scatter)
    eg = 1

    def body(dp_any, q_ref, k_ref, v_ref, ba_ref, z_ref, sh_ref, ti_ref, dy_ref, al_ref, dt_ref, og_ref, *rest):
        s_in, (dp_ref, dqkv_ref, gal_ref, gdt_ref, gog_ref) = rest[:ns], rest[ns:ns + 5]
        s_out, (ds_scr, *sems) = rest[ns + 5:2 * ns + 5], rest[2 * ns + 5:]
        bi, ci = pl.program_id(0), pl.program_id(1)
        _ride_along(_Exchange(True, s_in, s_out, *sems) if ns else None,
                    (bi == 0) & (ci == 0), (bi == bsz // eg - 1) & (ci == nc - 1))

        @pl.when((bi == 0) & (ci == 0))
        def _():
            gal_ref[...] = jnp.zeros_like(gal_ref)
            gdt_ref[...] = jnp.zeros_like(gdt_ref)
            gog_ref[...] = jnp.zeros_like(gog_ref)

        @pl.when(ci == 0)
        def _():
            ds_scr[...] = jnp.zeros_like(ds_scr)

        chains = range(eg * NH)
        _, vjp, _ = jax.vjp(_delta_chunk, [sh_ref[e // NH, e % NH] for e in chains], _chain_blocks(q_ref),
                            _chain_blocks(k_ref), _chain_blocks(v_ref), [ba_ref[b] for b in range(eg)],
                            _chain_blocks(z_ref), al_ref[...], dt_ref[...], og_ref[...],
                            [ti_ref[e // NH, e % NH] for e in chains], has_aux=True)
        ds, dq, dk, dv, dba, dz, dal, ddt, dog, _ = vjp(([ds_scr[e] for e in chains], _chain_blocks(dy_ref)))
        gal_ref[...] += dal
        gdt_ref[...] += ddt
        gog_ref[...] += dog
        for b in range(eg):
            dp_ref[b, :, D:D + LANE] = dba[b].astype(BF16)
        for e in chains:
            b, lo = e // NH, LANE * (e % NH)
            ds_scr[e] = ds[e]
            dp_ref[b, :, lo:lo + LANE] = dz[e].astype(BF16)
            dqkv_ref[b, :, lo:lo + LANE] = dq[e]
            dqkv_ref[b, :, D + lo:D + lo + LANE] = dk[e]
            dqkv_ref[b, :, 2 * D + lo:2 * D + lo + LANE] = dv[e]

    def blk(width, col):
        return pl.BlockSpec((eg, CH, width), lambda bi, ci: (bi, nc - 1 - ci, col))

    def per_chunk(rows):
        return pl.BlockSpec((eg, None, NH, rows, rows), lambda bi, ci: (bi, nc - 1 - ci, 0, 0, 0))

    hbm = pl.BlockSpec(memory_space=pltpu.HBM)
    qkv3, proj3 = qkvn.reshape(bsz, t, 3 * D), proj.reshape(bsz, t, NP)
    dproj, dqkvn, *rest = pl.pallas_call(
        body, name="delta_bwd_exchange" if ns else "delta_bwd", grid=(bsz // eg, nc),
        in_specs=[pl.BlockSpec(memory_space=pl.ANY), blk(D, 0), blk(D, 1), blk(D, 2), blk(LANE, C_BA // LANE),
                  blk(D, C_BZ // D), per_chunk(LANE), per_chunk(CH),
                  blk(D, 0), _full((1, LANE)), _full((1, LANE)), _full((1, LANE))] + [hbm] * ns,
        out_specs=[blk(D + LANE, C_BZ // (D + LANE)), blk(3 * D, 0), _full((1, LANE)), _full((1, LANE)),
                   _full((1, LANE))] + [hbm] * ns,
        out_shape=[SDS((bsz, t, NP), BF16), SDS((bsz, t, 3 * D), F32), SDS((1, LANE), F32), SDS((1, LANE), F32),
                   SDS((1, LANE), F32)] + [SDS(a.shape, a.dtype) for a in scatter],
        input_output_aliases={0: 0},
        scratch_shapes=[pltpu.VMEM((eg * NH, LANE, LANE), F32)] + (_exchange_sems(ns) if ns else []),
        compiler_params=_cp(56),
    )(dproj.reshape(bsz, t, NP), qkv3, qkv3, qkv3, proj3, proj3, shist, tsaved, dyb.reshape(bsz, t, D), alog, dtb, og,
      *scatter)
    return (dproj.reshape(n, NP), dqkvn.reshape(n, 3 * D), *rest)


def _c_chunk(us, vs, zs, lgs, lbs, ws, bsb):
    gv = [jax.nn.gelu(v) for v in vs]
    width = LANE * len(gv)
    mu = sum(jnp.sum(x, axis=-1, keepdims=True) for x in gv) / width
    var = sum(jnp.sum(jnp.square(x - mu), axis=-1, keepdims=True) for x in gv) / width
    rstd = lax.rsqrt(var + EPS)
    r = lax.broadcasted_iota(jnp.int32, (SG, SG), 0)
    c = lax.broadcasted_iota(jnp.int32, (SG, SG), 1)
    out = []
    for j in range(len(gv)):
        nrm = (gv[j] - mu) * rstd * lgs[j] + lbs[j]
        mixed = jnp.dot(jnp.where(r >= c, ws[j], 0.0), nrm, preferred_element_type=F32) + bsb[j]
        out.append(jax.nn.gelu(us[j]) * mixed * jax.nn.silu(zs[j]))
    return out


def _c_args(u_ref, v_ref, z_ref, lg_ref, lb_ref, ws_ref, bs_ref):
    sl = [slice(LANE * j, LANE * (j + 1)) for j in range(4)]
    return ([u_ref[:, s] for s in sl], [v_ref[:, s] for s in sl], [z_ref[:, s] for s in sl],
            [lg_ref[:, s] for s in sl], [lb_ref[:, s] for s in sl],
            [ws_ref[j] for j in range(4)], [bs_ref[j] for j in range(4)])


def _c_fwd(proj, lg, lb, ws, bsb, n):
    cb = C_C // 512

    def body(u_ref, v_ref, z_ref, lg_ref, lb_ref, ws_ref, bs_ref, y_ref):
        outs = _c_chunk(*_c_args(u_ref, v_ref, z_ref, lg_ref, lb_ref, ws_ref, bs_ref))
        for j, o in enumerate(outs):
            y_ref[:, LANE * j:LANE * (j + 1)] = o

    return pl.pallas_call(
        body, name="c_fwd", grid=(n // SG,),
        in_specs=[pl.BlockSpec((SG, 512), lambda i: (i, cb)), pl.BlockSpec((SG, 512), lambda i: (i, cb + 1)),
                  pl.BlockSpec((SG, 512), lambda i: (i, cb + 2)), _full((1, 512)), _full((1, 512)),
                  _full((4, SG, SG)), _full((4, SG, SG))],
        out_specs=pl.BlockSpec((SG, 512), lambda i: (i, 0)),
        out_shape=SDS((n, 512), F32), compiler_params=_cp(32),
    )(proj, proj, proj, lg, lb, ws, bsb)


def _c_bwd(dproj, proj, dy, lg, lb, ws, bsb, n):
    cb = C_C // 512

    def body(dp_any, u_ref, v_ref, z_ref, dy_ref, lg_ref, lb_ref, ws_ref, bs_ref,
             dp_ref, glg_ref, glb_ref, gws_ref, gbs_ref):
        @pl.when(pl.program_id(0) == 0)
        def _():
            glg_ref[...] = jnp.zeros_like(glg_ref)
            glb_ref[...] = jnp.zeros_like(glb_ref)
            gws_ref[...] = jnp.zeros_like(gws_ref)
            gbs_ref[...] = jnp.zeros_like(gbs_ref)

        _, vjp = jax.vjp(_c_chunk, *_c_args(u_ref, v_ref, z_ref, lg_ref, lb_ref, ws_ref, bs_ref))
        dus, dvs, dzs, dlgs, dlbs, dwss, dbss = vjp([dy_ref[:, LANE * j:LANE * (j + 1)] for j in range(4)])
        for j in range(4):
            sl = slice(LANE * j, LANE * (j + 1))
            dp_ref[:, LANE * j:LANE * (j + 1)] = dus[j].astype(BF16)
            dp_ref[:, 512 + LANE * j:512 + LANE * (j + 1)] = dvs[j].astype(BF16)
            dp_ref[:, 1024 + LANE * j:1024 + LANE * (j + 1)] = dzs[j].astype(BF16)
            glg_ref[:, sl] += dlgs[j]
            glb_ref[:, sl] += dlbs[j]
            gws_ref[j] += dwss[j]
            gbs_ref[j] += jnp.broadcast_to(jnp.sum(dbss[j], axis=-1, keepdims=True), (SG, SG))

    return pl.pallas_call(
        body, name="c_bwd", grid=(n // SG,),
        in_specs=[pl.BlockSpec(memory_space=pl.ANY),
                  pl.BlockSpec((SG, 512), lambda i: (i, cb)), pl.BlockSpec((SG, 512), lambda i: (i, cb + 1)),
                  pl.BlockSpec((SG, 512), lambda i: (i, cb + 2)), pl.BlockSpec((SG, 512), lambda i: (i, 0)),
                  _full((1, 512)), _full((1, 512)), _full((4, SG, SG)), _full((4, SG, SG))],
        out_specs=[pl.BlockSpec((SG, 1536), lambda i: (i, C_C // 1536)),
                   _full((1, 512)), _full((1, 512)), _full((4, SG, SG)), _full((4, SG, SG))],
        out_shape=[SDS((n, NP), BF16), SDS((1, 512), F32), SDS((1, 512), F32), SDS((4, SG, SG), F32), SDS((4, SG, SG), F32)],
        input_output_aliases={0: 0}, compiler_params=_cp(32),
    )(dproj, proj, proj, proj, dy, lg, lb, ws, bsb)


def _merge_fwd(x2d, ya, yb, yc, proj, ap, bp, cp, wo):
    n = x2d.shape[0]
    tm = _tile(n, 512)
    gb = C_G // D

    def body(x_ref, ya_ref, yb_ref, yc_ref, g0_ref, g1_ref, g2_ref, ap_ref, bp_ref, cp_ref, wo_ref, o_ref):
        merged = (jax.nn.sigmoid(g0_ref[...]) * _bdot(ya_ref[...], ap_ref[...])
                  + jax.nn.sigmoid(g1_ref[...]) * _bdot(yb_ref[...], bp_ref[...])
                  + jax.nn.sigmoid(g2_ref[...]) * _bdot(yc_ref[...], cp_ref[...]))
        o_ref[...] = x_ref[...] + _bdot(merged, wo_ref[...])

    def rows(w):
        return pl.BlockSpec((tm, w), lambda i: (i, 0))

    return pl.pallas_call(
        body, name="merge_fwd", grid=(n // tm,),
        in_specs=[rows(D), rows(512), rows(D), rows(512),
                  pl.BlockSpec((tm, D), lambda i: (i, gb)), pl.BlockSpec((tm, D), lambda i: (i, gb + 1)),
                  pl.BlockSpec((tm, D), lambda i: (i, gb + 2)),
                  _resident((512, D)), _resident((D, D)), _resident((512, D)), _resident((D, D))],
        out_specs=rows(D), out_shape=SDS((n, D), F32), compiler_params=_cp(48),
    )(x2d, ya, yb, yc, proj, proj, proj, ap, bp, cp, wo)


def _merge_bwd(dxo, ya, yb, yc, proj, ap, bp, cp, apt, bpt, cpt, wot):
    n = dxo.shape[0]
    tm = _tile(n, 256)
    gb = C_G // D

    def body(d_ref, ya_ref, yb_ref, yc_ref, g0_ref, g1_ref, g2_ref, ap_ref, bp_ref, cp_ref,
             apt_ref, bpt_ref, cpt_ref, wot_ref,
             dp_ref, dya_ref, dyb_ref, dyc_ref, dpa_ref, dpb_ref, dpc_ref, mg_ref):
        dm = _bdot(d_ref[...], wot_ref[...])
        merged = None
        for j, (g_ref, y_ref, w_ref, wt_ref, dy_ref, dpj_ref) in enumerate((
                (g0_ref, ya_ref, ap_ref, apt_ref, dya_ref, dpa_ref),
                (g1_ref, yb_ref, bp_ref, bpt_ref, dyb_ref, dpb_ref),
                (g2_ref, yc_ref, cp_ref, cpt_ref, dyc_ref, dpc_ref))):
            s = jax.nn.sigmoid(g_ref[...])
            pj = _bdot(y_ref[...], w_ref[...])
            merged = s * pj if merged is None else merged + s * pj
            dp_ref[:, D * j:D * (j + 1)] = (dm * pj * s * (1.0 - s)).astype(BF16)
            dpj = (dm * s).astype(BF16)
            dpj_ref[...] = dpj
            dy_ref[...] = jnp.dot(dpj, wt_ref[...], preferred_element_type=F32)
        mg_ref[...] = merged

    def rows(w):
        return pl.BlockSpec((tm, w), lambda i: (i, 0))

    return pl.pallas_call(
        body, name="merge_bwd", grid=(n // tm,),
        in_specs=[rows(D), rows(512), rows(D), rows(512),
                  pl.BlockSpec((tm, D), lambda i: (i, gb)), pl.BlockSpec((tm, D), lambda i: (i, gb + 1)),
                  pl.BlockSpec((tm, D), lambda i: (i, gb + 2)),
                  _resident((512, D)), _resident((D, D)), _resident((512, D)),
                  _resident((D, 512)), _resident((D, D)), _resident((D, 512)), _resident((D, D))],
        out_specs=[pl.BlockSpec((tm, 3 * D), lambda i: (i, C_G // (3 * D))), rows(512), rows(D), rows(512),
                   rows(D), rows(D), rows(D), rows(D)],
        out_shape=[SDS((n, NP), BF16), SDS((n, 512), F32), SDS((n, D), F32), SDS((n, 512), F32),
                   SDS((n, D), BF16), SDS((n, D), BF16), SDS((n, D), BF16), SDS((n, D), F32)],
        compiler_params=_cp(56),
    )(dxo, ya, yb, yc, proj, proj, proj, ap, bp, cp, apt, bpt, cpt, wot)


def _sum_parts(p_ref):
    g = p_ref[0].astype(F32)
    for s in range(1, NDEV):
        g = g + p_ref[s].astype(F32)
    return g


def _adamw(g, w, m, v):
    nm = ADAM_B1 * m + (1.0 - ADAM_B1) * g
    nv = ADAM_B2 * v + (1.0 - ADAM_B2) * jnp.square(g)
    nm_hat = nm / (1.0 - ADAM_B1 ** ADAM_STEP)
    nv_hat = nv / (1.0 - ADAM_B2 ** ADAM_STEP)
    return -ADAM_LR * (nm_hat / (jnp.sqrt(nv_hat) + ADAM_EPS) + ADAM_WD * w), nm, nv


def _reduce_adamw(parts, w, m, v, name):
    r, c = w.shape
    tr = _tile(r, 128)

    def body(p_ref, w_ref, m_ref, v_ref, g_ref, d_ref, nm_ref, nv_ref):
        g = _sum_parts(p_ref)
        g_ref[...] = g
        d_ref[...], nm_ref[...], nv_ref[...] = _adamw(g, w_ref[...], m_ref[...], v_ref[...])

    blk = pl.BlockSpec((tr, c), lambda i: (i, 0))
    return pl.pallas_call(
        body, name=name, grid=(r // tr,),
        in_specs=[pl.BlockSpec((NDEV, tr, c), lambda i: (0, i, 0)), blk, blk, blk],
        out_specs=[blk, blk, blk, blk], out_shape=[SDS((r, c), F32)] * 4, compiler_params=_cp(48),
    )(parts, w, m, v)


def _reduce_adamw_leaves(parts, ws, ms, vs, name):
    nleaf = len(ws)
    counts = [len(p) if isinstance(p, (list, tuple)) else 0 for p in parts]
    flat = [a for p in parts for a in (p if isinstance(p, (list, tuple)) else [p])]

    def body(*refs):
        p_refs, rest = refs[:len(flat)], refs[len(flat):]
        w_refs, m_refs, v_refs = rest[:nleaf], rest[nleaf:2 * nleaf], rest[2 * nleaf:3 * nleaf]
        outs = rest[3 * nleaf:]
        at = 0
        for i in range(nleaf):
            g_ref, d_ref, nm_ref, nv_ref = outs[i], outs[nleaf + i], outs[2 * nleaf + i], outs[3 * nleaf + i]
            for idx in (range(counts[i]) if counts[i] else [Ellipsis]):
                g = _sum_parts(p_refs[at])
                at += 1
                g_ref[idx] = g
                d_ref[idx], nm_ref[idx], nv_ref[idx] = _adamw(g, w_refs[i][idx], m_refs[i][idx], v_refs[i][idx])

    vm = pl.BlockSpec(memory_space=pltpu.VMEM)
    outs = pl.pallas_call(
        body, name=name, in_specs=[vm] * (len(flat) + 3 * nleaf), out_specs=[vm] * (4 * nleaf),
        out_shape=[SDS(w.shape, F32) for w in ws] * 4, compiler_params=_cp(56),
    )(*flat, *ws, *ms, *vs)
    return [outs[j * nleaf:(j + 1) * nleaf] for j in range(4)]


def _unshard(name, g):
    if name in ROW_SHARDED:
        return g.reshape(g.shape[0] * g.shape[1], g.shape[2])
    g = jnp.moveaxis(g, 0, 1)
    return g.reshape(g.shape[0], g.shape[1] * g.shape[2])


def _reshard(name, full):
    r, c = full.shape
    if name in ROW_SHARDED:
        return full.reshape(NDEV, r // NDEV, c)
    return jnp.moveaxis(full.reshape(r, NDEV, c // NDEV), 1, 0)


def _w_in_to_padded(slabs):
    pieces = []
    for lo, hi, _ in sorted(SEGMENTS, key=lambda s: s[2]):
        for d in range(NDEV):
            a, b = max(lo, d * W_SHARD), min(hi, (d + 1) * W_SHARD)
            if a < b:
                pieces.append(slabs[d, :, a - d * W_SHARD:b - d * W_SHARD])
    pieces.append(jnp.zeros(slabs.shape[1:2] + (NP - C_BA - 16,), slabs.dtype))
    return jnp.concatenate(pieces, axis=-1)


def _w_in_from_padded(g):
    slabs = []
    for d in range(NDEV):
        pieces = []
        for lo, hi, pstart in SEGMENTS:
            a, b = max(lo, d * W_SHARD), min(hi, (d + 1) * W_SHARD)
            if a < b:
                pieces.append(g[:, pstart + a - lo:pstart + b - lo])
        pieces.append(jnp.zeros(g.shape[:1] + (W_SHARD_PAD - W_SHARD,), g.dtype))
        slabs.append(jnp.concatenate(pieces, axis=-1))
    return jnp.stack(slabs)


def _pad_w_in(w):
    return jnp.pad(w, ((0, 0), (0, W_SHARD_PAD - W_SHARD)))


def _lane_row(vec8, offset):
    return jnp.pad(vec8, (offset, LANE - NH - offset))[None]


def kernel(x, norm_g, w_in, a_dw, a_dw_b, a_ln_g, a_ln_b, a_proj, b_conv, b_a_log, b_dt_bias, b_onorm_g, b_proj, c_ln_g, c_ln_b, c_ws, c_bs, c_proj, w_out, final_g, loss_target, m_norm_g, m_w_in, m_a_dw, m_a_dw_b, m_a_ln_g, m_a_ln_b, m_a_proj, m_b_conv, m_b_a_log, m_b_dt_bias, m_b_onorm_g, m_b_proj, m_c_ln_g, m_c_ln_b, m_c_ws, m_c_bs, m_c_proj, m_w_out, m_final_g, v_norm_g, v_w_in, v_a_dw, v_a_dw_b, v_a_ln_g, v_a_ln_b, v_a_proj, v_b_conv, v_b_a_log, v_b_dt_bias, v_b_onorm_g, v_b_proj, v_c_ln_g, v_c_ln_b, v_c_ws, v_c_bs, v_c_proj, v_w_out, v_final_g):
    wts = dict(norm_g=norm_g, w_in=w_in, a_dw=a_dw, a_dw_b=a_dw_b, a_ln_g=a_ln_g, a_ln_b=a_ln_b, a_proj=a_proj,
               b_conv=b_conv, b_a_log=b_a_log, b_dt_bias=b_dt_bias, b_onorm_g=b_onorm_g, b_proj=b_proj,
               c_ln_g=c_ln_g, c_ln_b=c_ln_b, c_ws=c_ws, c_bs=c_bs, c_proj=c_proj, w_out=w_out, final_g=final_g)
    mom = dict(norm_g=m_norm_g, w_in=m_w_in, a_dw=m_a_dw, a_dw_b=m_a_dw_b, a_ln_g=m_a_ln_g, a_ln_b=m_a_ln_b,
               a_proj=m_a_proj, b_conv=m_b_conv, b_a_log=m_b_a_log, b_dt_bias=m_b_dt_bias, b_onorm_g=m_b_onorm_g,
               b_proj=m_b_proj, c_ln_g=m_c_ln_g, c_ln_b=m_c_ln_b, c_ws=m_c_ws, c_bs=m_c_bs, c_proj=m_c_proj,
               w_out=m_w_out, final_g=m_final_g)
    vel = dict(norm_g=v_norm_g, w_in=v_w_in, a_dw=v_a_dw, a_dw_b=v_a_dw_b, a_ln_g=v_a_ln_g, a_ln_b=v_a_ln_b,
               a_proj=v_a_proj, b_conv=v_b_conv, b_a_log=v_b_a_log, b_dt_bias=v_b_dt_bias, b_onorm_g=v_b_onorm_g,
               b_proj=v_b_proj, c_ln_g=v_c_ln_g, c_ln_b=v_c_ln_b, c_ws=v_c_ws, c_bs=v_c_bs, c_proj=v_c_proj,
               w_out=v_w_out, final_g=v_final_g)

    bsz, t, _ = x.shape
    n = bsz * t
    depth = norm_g.shape[0]
    x2d = x.reshape(n, D)
    tgt = loss_target.reshape(n, D)

    def weight_blocks(l):
        return [_pad_w_in(w_in[l].astype(BF16))] + [wts[k][l].astype(BF16) for k in BIG_REST]

    def matmul_weights(w_in_all, *rest):
        got = {k: _unshard(k, g) for k, g in zip(BIG_REST, rest)}
        got['wp'] = _w_in_to_padded(w_in_all)
        return got

    conv_all = _all_gather([wts[k] for k in SMALL], "gather_conv_weights")
    conv_w = {k: jnp.stack([_unshard(k, g[:, l]) for l in range(depth)]) for k, g in zip(SMALL, conv_all)}
    a_dw32 = jnp.pad(conv_w['a_dw'], ((0, 0), (0, 32 - A_K), (0, 0)))
    b_conv8 = jnp.pad(conv_w['b_conv'], ((0, 0), (0, 8 - B_K), (0, 0)))
    bsb = jnp.broadcast_to(c_bs[..., None], c_bs.shape + (SG,))
    full = [matmul_weights(*_all_gather(weight_blocks(0), "gather_matmul_weights"))]

    saved = []
    xl = x2d
    for l in range(depth):
        alog, dtb = _lane_row(b_a_log[l], NH), _lane_row(b_dt_bias[l], NH)
        proj, h = _inproj(xl, norm_g[l][None], full[l]['wp'])
        ya, conv_a = _a_fwd(proj, a_dw32[l], a_dw_b[l][None], a_ln_g[l][None], a_ln_b[l][None], bsz, t)
        qkvn, conv_b = _bprep_fwd(proj, b_conv8[l], bsz, t)
        yb, shist, tsave, *nxt = _delta_fwd(qkvn, proj, alog, dtb, b_onorm_g[l][None], bsz, t,
                                            gather=weight_blocks(l + 1) if l + 1 < depth else ())
        if nxt:
            full.append(matmul_weights(*nxt))
        yc = _c_fwd(proj, c_ln_g[l][None], c_ln_b[l][None], c_ws[l], bsb[l], n)
        xn = _merge_fwd(xl, ya, yb, yc, proj, full[l]['a_proj'], full[l]['b_proj'], full[l]['c_proj'], full[l]['w_out'])
        saved.append((xl, proj, h, ya, yb, yc, qkvn, shist, tsave, alog, dtb, conv_a, conv_b))
        xl = xn

    dx, g_final, loss_blk = _loss_head(xl, final_g[None], tgt)
    loss = lax.psum(loss_blk[0, 0], ("x", "y", "c"))

    gfull = {k: [None] * depth for k in WEIGHTS if k != 'final_g'}
    recv = [None] * depth

    def grad_slabs(l):
        return [_w_in_from_padded(gfull['w_in'][l]).astype(BF16)] + [
            _reshard(k, gfull[k][l]).astype(BF16) for k in SHARDED_REST]

    for l in reversed(range(depth)):
        xl, proj, h, ya, yb, yc, qkvn, shist, tsave, alog, dtb, conv_a, conv_b = saved[l]
        ap, bp, cp, wo = full[l]['a_proj'], full[l]['b_proj'], full[l]['c_proj'], full[l]['w_out']
        dproj, dya, dyb, dyc, dpa, dpb, dpc, merged = _merge_bwd(dx, ya, yb, yc, proj, ap, bp, cp, ap.T, bp.T, cp.T, wo.T)
        gfull['a_proj'][l] = _mm_tn(ya, dpa, "grad_a_proj")
        gfull['b_proj'][l] = _mm_tn(yb, dpb, "grad_b_proj")
        gfull['c_proj'][l] = _mm_tn(yc, dpc, "grad_c_proj")
        gfull['w_out'][l] = _mm_tn(merged, dx, "grad_w_out")
        dproj, g_clg, g_clb, g_cws, g_cbs = _c_bwd(dproj, proj, dyc, c_ln_g[l][None], c_ln_b[l][None], c_ws[l], bsb[l], n)
        dproj, g_adw, g_adb, g_alg, g_alb = _a_bwd(dproj, proj, conv_a, dya, a_dw32[l], a_ln_g[l][None], a_ln_b[l][None],
                                                   bsz, t)
        dproj, dqkvn, g_alog, g_dt, g_og, *got = _delta_bwd(dproj, qkvn, proj, shist, tsave, dyb, alog, dtb, b_onorm_g[l][None],
                                                            bsz, t, scatter=grad_slabs(l + 1) if l + 1 < depth else ())
        if got:
            recv[l + 1] = got
        dproj, g_bconv = _bprep_bwd(dproj, proj, conv_b, dqkvn, b_conv8[l], bsz, t)
        gfull['w_in'][l] = _mm_tn(h, dproj, "grad_w_in")
        gfull['a_dw'][l] = g_adw[:A_K]
        gfull['b_conv'][l] = g_bconv[:B_K]
        dx, g_ng, *got = _inproj_bwd(dproj, full[l]['wp'], xl, norm_g[l][None], dx, scatter=grad_slabs(0) if l == 0 else ())
        if got:
            recv[0] = got
        gfull['norm_g'][l] = g_ng[0]
        gfull['a_dw_b'][l], gfull['a_ln_g'][l], gfull['a_ln_b'][l] = g_adb[0], g_alg[0], g_alb[0]
        gfull['b_a_log'][l], gfull['b_dt_bias'][l] = g_alog[0, NH:2 * NH], g_dt[0, NH:2 * NH]
        gfull['b_onorm_g'][l] = g_og[0]
        gfull['c_ln_g'][l], gfull['c_ln_b'][l] = g_clg[0], g_clb[0]
        gfull['c_ws'][l], gfull['c_bs'][l] = g_cws, g_cbs[:, :, 0]
    grad_x = dx.reshape(bsz, t, D)

    outs_w = [_reduce_adamw(recv[l][0], _pad_w_in(w_in[l]), _pad_w_in(m_w_in[l]), _pad_w_in(v_w_in[l]), "adamw_w_in")
              for l in range(depth)]
    outs_s = _reduce_adamw_leaves([[recv[l][1 + i] for l in range(depth)] for i in range(len(SHARDED_REST))],
                                  [wts[k] for k in SHARDED_REST], [mom[k] for k in SHARDED_REST],
                                  [vel[k] for k in SHARDED_REST], "adamw_sharded")

    def upto3d(a):
        return a[None] if a.ndim == 1 else a.reshape((-1,) + a.shape[-2:]) if a.ndim > 3 else a

    grepl = [upto3d(jnp.stack(gfull[k]) if k != 'final_g' else g_final[0]) for k in REPL]
    outs_r = _reduce_adamw_leaves(_all_gather(grepl, "gather_replicated_grads"), [upto3d(wts[k]) for k in REPL],
                                  [upto3d(mom[k]) for k in REPL], [upto3d(vel[k]) for k in REPL], "adamw_replicated")

    res = []
    for j in range(4):
        leaves = {k: outs_r[j][i].reshape(wts[k].shape) for i, k in enumerate(REPL)}
        leaves.update({k: outs_s[j][i] for i, k in enumerate(SHARDED_REST)})
        leaves['w_in'] = jnp.stack([outs_w[l][j][:, :W_SHARD] for l in range(depth)])
        res.append([leaves[k] for k in WEIGHTS])
    grads, deltas, new_m, new_v = res
    return (loss, grad_x, *grads, *deltas, *new_m, *new_v)
```

```python
import jax
import jax.numpy as jnp
from jax import lax
from jax.experimental import pallas as pl
from jax.experimental.pallas import tpu as pltpu

F32 = jnp.float32
BF16 = jnp.bfloat16
SDS = jax.ShapeDtypeStruct
MESH = pl.DeviceIdType.MESH

NDEV = 8
D = 1024
EPS = 1e-6
LANE = 128

C_Q, C_K, C_V = 0, 1024, 2048
C_G = 3072
C_A = 6144
C_C = 7680
C_BZ = 9216
C_BA = 10240
NP = 10368
N_IN = 10256
SEGMENTS = ((0, 1536, C_A), (1536, 4608, C_Q), (4608, 5632, C_BZ), (5632, 5648, C_BA), (5648, 7184, C_C), (7184, 10256, C_G))
W_SHARD = N_IN // NDEV
W_SHARD_PAD = 1408

A_K, A_H, A_RB = 31, 32, 32
B_K, B_H, B_RB = 4, 8, 8
CH = 64
SG = 128
NH = 8

ADAM_LR, ADAM_B1, ADAM_B2, ADAM_EPS, ADAM_WD, ADAM_STEP = 0.001, 0.9, 0.999, 1e-08, 0.01, 10

WEIGHTS = ['norm_g', 'w_in', 'a_dw', 'a_dw_b', 'a_ln_g', 'a_ln_b', 'a_proj', 'b_conv', 'b_a_log', 'b_dt_bias',
           'b_onorm_g', 'b_proj', 'c_ln_g', 'c_ln_b', 'c_ws', 'c_bs', 'c_proj', 'w_out', 'final_g']
SHARDED_REST = ['a_dw', 'a_proj', 'b_conv', 'b_proj', 'c_proj', 'w_out']
REPL = [n for n in WEIGHTS if n != 'w_in' and n not in SHARDED_REST]
BIG_REST = ['a_proj', 'b_proj', 'c_proj', 'w_out']
SMALL = ['a_dw', 'b_conv']
ROW_SHARDED = ('b_proj', 'w_out')

NN = ((1,), (0,))
NT = ((1,), (1,))
TN = ((0,), (0,))


def _tile(n, pref):
    return pref if (n >= pref and n % pref == 0) else n


def _cp(vmem_mb):
    return pltpu.CompilerParams(vmem_limit_bytes=vmem_mb * 2 ** 20)


def _full(shape):
    nd = len(shape)
    return pl.BlockSpec(shape, lambda *_: (0,) * nd)


def _resident(shape):
    nd = len(shape)
    return pl.BlockSpec(shape, lambda *_: (0,) * nd, pipeline_mode=pl.Buffered(1))


def _rms(x, g):
    return x * lax.rsqrt(jnp.mean(x * x, axis=-1, keepdims=True) + EPS) * g


def _softplus(x):
    return jnp.maximum(x, 0.0) + jnp.log1p(jnp.exp(-jnp.abs(x)))


def _bdot(a, b):
    return jnp.dot(a.astype(BF16), b.astype(BF16), preferred_element_type=F32)


def _mm(a, b, dims):
    return lax.dot_general(a, b, (dims, ((), ())), preferred_element_type=F32)


def _all_gather(xs, name):
    nops = len(xs)

    def body(*refs):
        x_refs, out_refs = refs[:nops], refs[nops:2 * nops]
        send_sems, recv_sems, local_sems = refs[2 * nops:]
        x, y, cc = lax.axis_index("x"), lax.axis_index("y"), lax.axis_index("c")
        me, sibling = (x, y, cc), (x, y, 1 - cc)
        chips = [(1 - x, y), (x, 1 - y), (1 - x, 1 - y)]

        def slot(t, px, py, pc):
            return out_refs[t].at[4 * px + 2 * py + pc]

        def copy(t, k, block, to, src=None):
            return pltpu.make_async_remote_copy(
                src_ref=slot(t, *block) if src is None else src, dst_ref=slot(t, *block),
                send_sem=send_sems.at[7 * t + k], recv_sem=recv_sems.at[7 * t + k], device_id=to, device_id_type=MESH)

        ops = range(nops)
        mine = [pltpu.make_async_copy(x_refs[t], slot(t, *me), local_sems.at[t]) for t in ops]
        for cp in mine:
            cp.start()
        first = [copy(t, 0, me, sibling, src=x_refs[t]) for t in ops]
        first += [copy(t, 1 + j, me, (*chip, cc), src=x_refs[t]) for j, chip in enumerate(chips) for t in ops]
        for cp in first:
            cp.start()
        passed = []
        for j, chip in enumerate(chips):
            for t in ops:
                copy(t, 1 + j, (*chip, cc), me).wait_recv()
                fwd = copy(t, 4 + j, (*chip, cc), sibling)
                fwd.start()
                passed.append(fwd)
        for t in ops:
            copy(t, 0, sibling, me).wait_recv()
        for j, chip in enumerate(chips):
            for t in ops:
                copy(t, 4 + j, (*chip, 1 - cc), me).wait_recv()
        for cp in first + passed:
            cp.wait_send()
        for cp in mine:
            cp.wait()

    hbm = pl.BlockSpec(memory_space=pltpu.HBM)
    return pl.pallas_call(
        body, name=name, out_shape=[SDS((NDEV,) + a.shape, a.dtype) for a in xs],
        in_specs=[hbm] * nops, out_specs=[hbm] * nops,
        scratch_shapes=[pltpu.SemaphoreType.DMA((7 * nops,)), pltpu.SemaphoreType.DMA((7 * nops,)),
                        pltpu.SemaphoreType.DMA((nops,))],
    )(*xs)


def _exchange_sems(nops):
    return [pltpu.SemaphoreType.DMA((7 * nops,)), pltpu.SemaphoreType.DMA((7 * nops,)), pltpu.SemaphoreType.DMA((nops,))]


class _Exchange:
    def __init__(self, scatter, in_refs, out_refs, send_sems, recv_sems, local_sems):
        x, y, cc = lax.axis_index("x"), lax.axis_index("y"), lax.axis_index("c")
        me = 4 * x + 2 * y + cc
        nops = len(in_refs)
        self.local = [pltpu.make_async_copy(in_refs[t].at[me] if scatter else in_refs[t], out_refs[t].at[me],
                                            local_sems.at[t]) for t in range(nops)]
        self.sends, self.recvs = [], []
        for k in range(1, NDEV):
            px = 1 - x if k & 4 else x
            py = 1 - y if k & 2 else y
            pc = 1 - cc if k & 1 else cc
            peer = 4 * px + 2 * py + pc
            for t in range(nops):
                sem = 7 * t + k - 1
                self.sends.append(pltpu.make_async_remote_copy(
                    src_ref=in_refs[t].at[peer] if scatter else in_refs[t], dst_ref=out_refs[t].at[me],
                    send_sem=send_sems.at[sem], recv_sem=recv_sems.at[sem], device_id=(px, py, pc), device_id_type=MESH))
                self.recvs.append(pltpu.make_async_remote_copy(
                    src_ref=in_refs[t].at[me] if scatter else in_refs[t], dst_ref=out_refs[t].at[peer],
                    send_sem=send_sems.at[sem], recv_sem=recv_sems.at[sem], device_id=(px, py, pc), device_id_type=MESH))

    def start(self):
        for cp in self.local + self.sends:
            cp.start()

    def wait(self):
        for cp in self.recvs:
            cp.wait_recv()
        for cp in self.sends:
            cp.wait_send()
        for cp in self.local:
            cp.wait()


def _inproj(x2d, g_row, wp):
    n = x2d.shape[0]
    tm, tn = _tile(n, 1024), 1152

    def body(x_ref, g_ref, w_ref, proj_ref, h_ref, hs):
        @pl.when(pl.program_id(1) == 0)
        def _():
            h = _rms(x_ref[...], g_ref[...]).astype(BF16)
            hs[...] = h
            h_ref[...] = h

        proj_ref[...] = jnp.dot(hs[...], w_ref[...], preferred_element_type=F32)

    return pl.pallas_call(
        body, name="inproj", grid=(n // tm, NP // tn),
        in_specs=[pl.BlockSpec((tm, D), lambda i, j: (i, 0)), _full((1, D)), pl.BlockSpec((D, tn), lambda i, j: (0, j))],
        out_specs=[pl.BlockSpec((tm, tn), lambda i, j: (i, j)), pl.BlockSpec((tm, D), lambda i, j: (i, 0))],
        out_shape=[SDS((n, NP), F32), SDS((n, D), BF16)],
        scratch_shapes=[pltpu.VMEM((tm, D), BF16)], compiler_params=_cp(48),
    )(x2d, g_row, wp)


def _mm_tn(a, b, name):
    nn, m = a.shape
    k = b.shape[1]
    tk = 1152 if k % 1152 == 0 else _tile(k, 1024)
    tn = _tile(nn, 1024)

    def body(a_ref, b_ref, o_ref):
        p = _mm(a_ref[...].astype(BF16), b_ref[...].astype(BF16), TN)

        @pl.when(pl.program_id(1) == 0)
        def _():
            o_ref[...] = p

        @pl.when(pl.program_id(1) > 0)
        def _():
            o_ref[...] += p

    return pl.pallas_call(
        body, name=name, grid=(k // tk, nn // tn),
        in_specs=[pl.BlockSpec((tn, m), lambda j, t: (t, 0)), pl.BlockSpec((tn, tk), lambda j, t: (t, j))],
        out_specs=pl.BlockSpec((m, tk), lambda j, t: (0, j)),
        out_shape=SDS((m, k), F32), compiler_params=_cp(48),
    )(a, b)


def _inproj_bwd(dproj, wp, x2d, g_row, dxo, scatter=()):
    n = x2d.shape[0]
    tm, tk = _tile(n, 1024), 1152
    nk = NP // tk
    ns = len(scatter)

    def body(dp_ref, w_ref, x_ref, g_ref, dxo_ref, *rest):
        s_in, (dx_ref, dg_ref), s_out, (acc, *sems) = rest[:ns], rest[ns:ns + 2], rest[ns + 2:2 * ns + 2], rest[2 * ns + 2:]
        i, k = pl.program_id(0), pl.program_id(1)
        _ride_along(_Exchange(True, s_in, s_out, *sems) if ns else None,
                    (i == 0) & (k == 0), (i == n // tm - 1) & (k == nk - 1))
        p = _mm(dp_ref[...], w_ref[...], NT)

        @pl.when(k == 0)
        def _():
            acc[...] = p

        @pl.when(k > 0)
        def _():
            acc[...] += p

        @pl.when(k == nk - 1)
        def _():
            _, vjp = jax.vjp(_rms, x_ref[...], g_ref[...])
            dx, dg = vjp(acc[...])
            dx_ref[...] = dxo_ref[...] + dx

            @pl.when(i == 0)
            def _():
                dg_ref[...] = dg

            @pl.when(i > 0)
            def _():
                dg_ref[...] += dg

    hbm = pl.BlockSpec(memory_space=pltpu.HBM)
    return pl.pallas_call(
        body, name="inproj_bwd_exchange" if ns else "inproj_bwd", grid=(n // tm, nk),
        in_specs=[pl.BlockSpec((tm, tk), lambda i, k: (i, k)), pl.BlockSpec((D, tk), lambda i, k: (0, k)),
                  pl.BlockSpec((tm, D), lambda i, k: (i, 0)), _full((1, D)),
                  pl.BlockSpec((tm, D), lambda i, k: (i, 0))] + [hbm] * ns,
        out_specs=[pl.BlockSpec((tm, D), lambda i, k: (i, 0)), _full((1, D))] + [hbm] * ns,
        out_shape=[SDS((n, D), F32), SDS((1, D), F32)] + [SDS(a.shape, a.dtype) for a in scatter],
        scratch_shapes=[pltpu.VMEM((tm, D), F32)] + (_exchange_sems(ns) if ns else []), compiler_params=_cp(56),
    )(dproj, wp, x2d, g_row, dxo, *scatter)


def _loss_head(x2d, g_row, tgt):
    n = x2d.shape[0]
    tm = _tile(n, 512)

    def body(x_ref, g_ref, t_ref, dx_ref, dg_ref, loss_ref):
        i = pl.program_id(0)
        y, vjp = jax.vjp(_rms, x_ref[...], g_ref[...])
        err = y - t_ref[...]
        part = 0.5 * jnp.sum(jnp.mean(err * err, axis=-1, keepdims=True), axis=0, keepdims=True)
        dx, dg = vjp(err * (1.0 / D))
        dx_ref[...] = dx
        lb = jnp.broadcast_to(part, (8, LANE))

        @pl.when(i == 0)
        def _():
            dg_ref[...] = dg
            loss_ref[...] = lb

        @pl.when(i > 0)
        def _():
            dg_ref[...] += dg
            loss_ref[...] += lb

    return pl.pallas_call(
        body, name="loss_head", grid=(n // tm,),
        in_specs=[pl.BlockSpec((tm, D), lambda i: (i, 0)), _full((1, D)), pl.BlockSpec((tm, D), lambda i: (i, 0))],
        out_specs=[pl.BlockSpec((tm, D), lambda i: (i, 0)), _full((1, D)), _full((8, LANE))],
        out_shape=[SDS((n, D), F32), SDS((1, D), F32), SDS((8, LANE), F32)], compiler_params=_cp(40),
    )(x2d, g_row, tgt)


def _conv_rows(dst_ref, read, w_ref, offs, nrows, rb, bias=None):
    for r0 in range(0, nrows, rb):
        acc = bias
        for j, off in enumerate(offs):
            term = w_ref[j:j + 1, :] * read(off + r0, rb)
            acc = term if acc is None else acc + term
        dst_ref[r0:r0 + rb, :] = acc


def _fwd_offsets(kw, halo):
    return [halo - (kw - 1) + j for j in range(kw)]


def _bwd_offsets(kw):
    return [kw - 1 - j for j in range(kw)]


def _conv_bwd_w(dc, ext_ref, gw_ref, kw, halo, tt):
    for j in range(kw):
        gw_ref[j:j + 1, :] += jnp.sum(dc * ext_ref[pl.ds(halo - (kw - 1) + j, tt), :], axis=0, keepdims=True)


SUB = 8


def _fill_shifts(sh_ref, rows):
    for s in range(1, SUB):
        sh_ref[s, 0:rows, :] = sh_ref[0, pl.ds(s, rows), :]


def _tap(sh_ref, off, rows):
    return sh_ref[off % SUB, pl.ds(off - off % SUB, rows), :]


def _conv_bwd_w_sh(dc, ext_sh, gw_ref, kw, halo, tt):
    for j in range(kw):
        gw_ref[j:j + 1, :] += jnp.sum(dc * _tap(ext_sh, halo - (kw - 1) + j, tt), axis=0, keepdims=True)


def _a_post(c, z, g, b):
    mu = jnp.mean(c, axis=-1, keepdims=True)
    var = jnp.mean(jnp.square(c - mu), axis=-1, keepdims=True)
    a = (c - mu) * lax.rsqrt(var + EPS) * g + b
    return jax.nn.silu(a) * jax.nn.silu(z)


def _a_fwd(proj, dw, b_row, lg, lb, bsz, t):
    n = bsz * t
    tt = _tile(t, 256)
    nt = t // tt
    cb = C_A // 512

    def body(v_ref, g_ref, z_ref, vh_ref, gh_ref, w_ref, b_ref, lg_ref, lb_ref, y_ref, c_ref, ext):
        i = pl.program_id(1)
        ext[0, 0:A_H, :] = jnp.where(i > 0, vh_ref[...] * jax.nn.sigmoid(gh_ref[...]), 0.0)
        ext[0, A_H:, :] = v_ref[...] * jax.nn.sigmoid(g_ref[...])
        _fill_shifts(ext, tt + A_H - SUB)
        _conv_rows(c_ref, lambda off, rows: _tap(ext, off, rows), w_ref, _fwd_offsets(A_K, A_H), tt, A_RB, bias=b_ref[...])
        y_ref[...] = _a_post(c_ref[...], z_ref[...], lg_ref[...], lb_ref[...])

    def row(b, i):
        return b * nt + i

    def halo(b, i):
        return jnp.maximum((b * t + i * tt) // A_H - 1, 0)

    return pl.pallas_call(
        body, name="a_fwd", grid=(bsz, nt),
        in_specs=[pl.BlockSpec((tt, 512), lambda b, i: (row(b, i), cb)),
                  pl.BlockSpec((tt, 512), lambda b, i: (row(b, i), cb + 1)),
                  pl.BlockSpec((tt, 512), lambda b, i: (row(b, i), cb + 2)),
                  pl.BlockSpec((A_H, 512), lambda b, i: (halo(b, i), cb)),
                  pl.BlockSpec((A_H, 512), lambda b, i: (halo(b, i), cb + 1)),
                  _full((32, 512)), _full((1, 512)), _full((1, 512)), _full((1, 512))],
        out_specs=[pl.BlockSpec((tt, 512), lambda b, i: (row(b, i), 0))] * 2,
        out_shape=[SDS((n, 512), F32)] * 2,
        scratch_shapes=[pltpu.VMEM((SUB, tt + A_H, 512), F32)], compiler_params=_cp(40),
    )(proj, proj, proj, proj, proj, dw, b_row, lg, lb)


def _a_bwd(dproj, proj, conv, dy, dw, lg, lb, bsz, t):
    n = bsz * t
    tt = _tile(t, 256)
    nt = t // tt
    cb = C_A // 512

    def body(dp_any, v_ref, g_ref, z_ref, vh_ref, gh_ref, c_ref, dy_ref, w_ref, lg_ref, lb_ref,
             dp_ref, gw_ref, gb_ref, glg_ref, glb_ref, ext, dcp, dae, carry):
        b, i = pl.program_id(0), pl.program_id(1)
        ti = nt - 1 - i

        @pl.when((b == 0) & (i == 0))
        def _():
            gw_ref[...] = jnp.zeros_like(gw_ref)
            gb_ref[...] = jnp.zeros_like(gb_ref)
            glg_ref[...] = jnp.zeros_like(glg_ref)
            glb_ref[...] = jnp.zeros_like(glb_ref)

        @pl.when(i == 0)
        def _():
            carry[...] = jnp.zeros_like(carry)

        val, glu = v_ref[...], g_ref[...]
        sg = jax.nn.sigmoid(glu)
        ext[0, 0:A_H, :] = jnp.where(ti > 0, vh_ref[...] * jax.nn.sigmoid(gh_ref[...]), 0.0)
        ext[0, A_H:, :] = val * sg
        _fill_shifts(ext, tt + A_H - SUB)
        _, vjp = jax.vjp(_a_post, c_ref[...], z_ref[...], lg_ref[...], lb_ref[...])
        dc, dz, dlg, dlb = vjp(dy_ref[...])
        gb_ref[...] += jnp.sum(dc, axis=0, keepdims=True)
        glg_ref[...] += dlg
        glb_ref[...] += dlb
        dcp[0, 0:A_H, :] = jnp.zeros((A_H, 512), F32)
        dcp[0, A_H:A_H + tt, :] = dc
        dcp[0, A_H + tt:, :] = jnp.zeros((A_H, 512), F32)
        _fill_shifts(dcp, tt + 2 * A_H - SUB)
        _conv_bwd_w_sh(dc, ext, gw_ref, A_K, A_H, tt)
        _conv_rows(dae, lambda off, rows: _tap(dcp, off, rows), w_ref, _bwd_offsets(A_K), tt + A_H, A_RB)
        dae[tt:tt + A_H, :] += carry[...]
        carry[...] = dae[0:A_H, :]
        da = dae[A_H:, :]
        dp_ref[:, 0:512] = (da * sg).astype(BF16)
        dp_ref[:, 512:1024] = (da * val * sg * (1.0 - sg)).astype(BF16)
        dp_ref[:, 1024:1536] = dz.astype(BF16)

    def row(b, i):
        return b * nt + (nt - 1 - i)

    def halo(b, i):
        return jnp.maximum((b * t + (nt - 1 - i) * tt) // A_H - 1, 0)

    outs = pl.pallas_call(
        body, name="a_bwd", grid=(bsz, nt),
        in_specs=[pl.BlockSpec(memory_space=pl.ANY),
                  pl.BlockSpec((tt, 512), lambda b, i: (row(b, i), cb)),
                  pl.BlockSpec((tt, 512), lambda b, i: (row(b, i), cb + 1)),
                  pl.BlockSpec((tt, 512), lambda b, i: (row(b, i), cb + 2)),
                  pl.BlockSpec((A_H, 512), lambda b, i: (halo(b, i), cb)),
                  pl.BlockSpec((A_H, 512), lambda b, i: (halo(b, i), cb + 1)),
                  pl.BlockSpec((tt, 512), lambda b, i: (row(b, i), 0)),
                  pl.BlockSpec((tt, 512), lambda b, i: (row(b, i), 0)),
                  _full((32, 512)), _full((1, 512)), _full((1, 512))],
        out_specs=[pl.BlockSpec((tt, 1536), lambda b, i: (row(b, i), C_A // 1536)),
                   _full((32, 512)), _full((1, 512)), _full((1, 512)), _full((1, 512))],
        out_shape=[SDS((n, NP), BF16), SDS((32, 512), F32), SDS((1, 512), F32), SDS((1, 512), F32), SDS((1, 512), F32)],
        input_output_aliases={0: 0},
        scratch_shapes=[pltpu.VMEM((SUB, tt + A_H, 512), F32), pltpu.VMEM((SUB, tt + 2 * A_H, 512), F32),
                        pltpu.VMEM((tt + A_H, 512), F32), pltpu.VMEM((A_H, 512), F32)],
        compiler_params=_cp(48),
    )(dproj, proj, proj, proj, proj, proj, conv, dy, dw, lg, lb)
    return outs


def _b_post(blocks):
    out = []
    for idx, c in enumerate(blocks):
        s = jax.nn.silu(c)
        if idx < 2 * NH:
            s = s * lax.rsqrt(jnp.sum(s * s, axis=-1, keepdims=True) + EPS)
            if idx < NH:
                s = s * (LANE ** -0.5)
        out.append(s)
    return out


def _bprep_fwd(proj, wconv, bsz, t):
    n = bsz * t
    tt = _tile(t, 256)
    nt = t // tt

    def body(x_ref, xh_ref, w_ref, o_ref, c_ref, ext):
        i = pl.program_id(1)
        ext[0:B_H, :] = jnp.where(i > 0, xh_ref[...], 0.0)
        ext[B_H:, :] = x_ref[...]
        _conv_rows(c_ref, lambda off, rows: ext[pl.ds(off, rows), :], w_ref, _fwd_offsets(B_K, B_H), tt, B_RB)
        outs = _b_post([c_ref[:, LANE * j:LANE * (j + 1)] for j in range(3 * NH)])
        for j, o in enumerate(outs):
            o_ref[:, LANE * j:LANE * (j + 1)] = o

    return pl.pallas_call(
        body, name="bprep_fwd", grid=(bsz, nt),
        in_specs=[pl.BlockSpec((tt, 3072), lambda b, i: (b * nt + i, 0)),
                  pl.BlockSpec((B_H, 3072), lambda b, i: (jnp.maximum((b * t + i * tt) // B_H - 1, 0), 0)),
                  _full((8, 3072))],
        out_specs=[pl.BlockSpec((tt, 3072), lambda b, i: (b * nt + i, 0))] * 2,
        out_shape=[SDS((n, 3072), F32)] * 2,
        scratch_shapes=[pltpu.VMEM((tt + B_H, 3072), F32)], compiler_params=_cp(56),
    )(proj, proj, wconv)


def _bprep_bwd(dproj, proj, conv, dqkvn, wconv, bsz, t):
    n = bsz * t
    tt = _tile(t, 256)
    nt = t // tt

    def body(dp_any, x_ref, xh_ref, c_ref, dq_ref, w_ref, dp_ref, gw_ref, ext, dcp, dae, carry):
        b, i = pl.program_id(0), pl.program_id(1)
        ti = nt - 1 - i

        @pl.when((b == 0) & (i == 0))
        def _():
            gw_ref[...] = jnp.zeros_like(gw_ref)

        @pl.when(i == 0)
        def _():
            carry[...] = jnp.zeros_like(carry)

        ext[0:B_H, :] = jnp.where(ti > 0, xh_ref[...], 0.0)
        ext[B_H:, :] = x_ref[...]
        _, vjp = jax.vjp(_b_post, [c_ref[:, LANE * j:LANE * (j + 1)] for j in range(3 * NH)])
        (dcs,) = vjp([dq_ref[:, LANE * j:LANE * (j + 1)] for j in range(3 * NH)])
        dcp[0:B_H, :] = jnp.zeros((B_H, 3072), F32)
        for j, dcj in enumerate(dcs):
            dcp[B_H:B_H + tt, LANE * j:LANE * (j + 1)] = dcj
        dcp[B_H + tt:, :] = jnp.zeros((B_H, 3072), F32)
        _conv_bwd_w(dcp[B_H:B_H + tt, :], ext, gw_ref, B_K, B_H, tt)
        _conv_rows(dae, lambda off, rows: dcp[pl.ds(off, rows), :], w_ref, _bwd_offsets(B_K), tt + B_H, B_RB)
        dae[tt:tt + B_H, :] += carry[...]
        carry[...] = dae[0:B_H, :]
        dp_ref[...] = dae[B_H:, :].astype(BF16)

    def row(b, i):
        return b * nt + (nt - 1 - i)

    return pl.pallas_call(
        body, name="bprep_bwd", grid=(bsz, nt),
        in_specs=[pl.BlockSpec(memory_space=pl.ANY),
                  pl.BlockSpec((tt, 3072), lambda b, i: (row(b, i), 0)),
                  pl.BlockSpec((B_H, 3072), lambda b, i: (jnp.maximum((b * t + (nt - 1 - i) * tt) // B_H - 1, 0), 0)),
                  pl.BlockSpec((tt, 3072), lambda b, i: (row(b, i), 0)),
                  pl.BlockSpec((tt, 3072), lambda b, i: (row(b, i), 0)),
                  _full((8, 3072))],
        out_specs=[pl.BlockSpec((tt, 3072), lambda b, i: (row(b, i), 0)), _full((8, 3072))],
        out_shape=[SDS((n, NP), BF16), SDS((8, 3072), F32)],
        input_output_aliases={0: 0},
        scratch_shapes=[pltpu.VMEM((tt + B_H, 3072), F32), pltpu.VMEM((tt + 2 * B_H, 3072), F32),
                        pltpu.VMEM((tt + B_H, 3072), F32), pltpu.VMEM((B_H, 3072), F32)],
        compiler_params=_cp(56),
    )(dproj, proj, proj, conv, dqkvn, wconv)


def _split2(a):
    hi = a.astype(BF16)
    return hi, (a - hi.astype(F32)).astype(BF16)


def _split3(a):
    p1 = a.astype(BF16)
    r1 = a - p1.astype(F32)
    p2 = r1.astype(BF16)
    return p1, p2, (r1 - p2.astype(F32)).astype(BF16)


def _dot3_raw(a, b, dims):
    a1, a2 = _split2(a)
    b1, b2 = _split2(b)
    return _mm(a1, b1, dims) + (_mm(a1, b2, dims) + _mm(a2, b1, dims))


def _unit_lower_inverse_raw(lmats):
    r = lax.broadcasted_iota(jnp.int32, (CH, CH), 0)
    c = lax.broadcasted_iota(jnp.int32, (CH, CH), 1)
    eye = (r == c).astype(F32)
    blk = jnp.right_shift(r, 4) == jnp.right_shift(c, 4)
    dm = [jnp.where(blk, x, 0.0) for x in lmats]
    om = [a - b for a, b in zip(lmats, dm)]
    d2 = [_dot3_raw(x, x, NN) for x in dm]
    d4 = [_dot3_raw(x, x, NN) for x in d2]
    d8 = [_dot3_raw(x, x, NN) for x in d4]
    p = [_dot3_raw(eye - a, eye + b, NN) for a, b in zip(dm, d2)]
    p = [_dot3_raw(a, eye + b, NN) for a, b in zip(p, d4)]
    p = [_dot3_raw(a, eye + b, NN) for a, b in zip(p, d8)]
    m = [_dot3_raw(a, b, NN) for a, b in zip(p, om)]
    m2 = [_dot3_raw(x, x, NN) for x in m]
    t = [_dot3_raw(eye - a, eye + b, NN) for a, b in zip(m, m2)]
    return [_dot3_raw(a, b, NN) for a, b in zip(t, p)]


@jax.custom_vjp
def _unit_lower_inverse(lmats):
    return _unit_lower_inverse_raw(lmats)


def _unit_lower_inverse_fwd(lmats):
    tinv = _unit_lower_inverse_raw(lmats)
    return tinv, tinv


def _unit_lower_inverse_bwd(tinv, gs):
    x = [_dot3_raw(t, g, TN) for t, g in zip(tinv, gs)]
    return ([-_dot3_raw(a, t, NT) for a, t in zip(x, tinv)],)


_unit_lower_inverse.defvjp(_unit_lower_inverse_fwd, _unit_lower_inverse_bwd)


@jax.custom_vjp
def _saved_unit_lower_inverse(lmats, tinv):
    del lmats
    return tinv


def _saved_unit_lower_inverse_fwd(lmats, tinv):
    del lmats
    return tinv, tinv


def _saved_unit_lower_inverse_bwd(tinv, gs):
    (dl,) = _unit_lower_inverse_bwd(tinv, gs)
    return dl, [jnp.zeros_like(t) for t in tinv]


_saved_unit_lower_inverse.defvjp(_saved_unit_lower_inverse_fwd, _saved_unit_lower_inverse_bwd)


def _tri(lower):
    r = lax.broadcasted_iota(jnp.int32, (CH, CH), 0)
    c = lax.broadcasted_iota(jnp.int32, (CH, CH), 1)
    return ((r >= c) if lower else (r <= c)).astype(BF16)


def _tri_dot(x, lower, dims, tri_first):
    p1, p2, p3 = _split3(x)
    tri = _tri(lower)
    if tri_first:
        return _mm(tri, p1, dims) + (_mm(tri, p2, dims) + _mm(tri, p3, dims))
    return _mm(p1, tri, dims) + (_mm(p2, tri, dims) + _mm(p3, tri, dims))


@jax.custom_vjp
def _cumsum_rows(x):
    return _tri_dot(x, True, NN, True)


def _cumsum_rows_fwd(x):
    return _tri_dot(x, True, NN, True), None


def _cumsum_rows_bwd(_, g):
    return (_tri_dot(g, True, TN, True),)


_cumsum_rows.defvjp(_cumsum_rows_fwd, _cumsum_rows_bwd)


@jax.custom_vjp
def _cumsum_rows_t(x):
    return _tri_dot(x, False, TN, False)


def _cumsum_rows_t_fwd(x):
    return _tri_dot(x, False, TN, False), None


def _cumsum_rows_t_bwd(_, g):
    return (_tri_dot(g, False, NT, True),)


_cumsum_rows_t.defvjp(_cumsum_rows_t_fwd, _cumsum_rows_t_bwd)


def _delta_chunk(ss, qs, ks, vs, bas, zs, alog, dtb, og, tsaved=None):
    heads = range(len(ss))
    lane = lax.broadcasted_iota(jnp.int32, (1, LANE), 1)
    r = lax.broadcasted_iota(jnp.int32, (CH, CH), 0)
    c = lax.broadcasted_iota(jnp.int32, (CH, CH), 1)
    ri = lax.broadcasted_iota(jnp.int32, (CH, 1), 0)
    incl, strict = r >= c, r > c

    def pick(x, h):
        return jnp.sum(jnp.where(lane == h, x, 0.0), axis=-1, keepdims=True)

    beta = [jax.nn.sigmoid(pick(bas[e // NH], e % NH)) for e in heads]
    g = [-jnp.exp(pick(alog, e % NH + NH)) * _softplus(pick(bas[e // NH], e % NH + NH) + pick(dtb, e % NH + NH))
         for e in heads]
    gb = [jnp.broadcast_to(x, (CH, CH)) for x in g]
    gca = [_cumsum_rows(x) for x in gb]
    gcr = [_cumsum_rows_t(x) for x in gb]
    gc = [jnp.sum(jnp.where(c == 0, x, 0.0), axis=-1, keepdims=True) for x in gca]
    gl = [jnp.sum(jnp.where(ri == CH - 1, x, 0.0), axis=0, keepdims=True) for x in gc]
    diff = [a - b for a, b in zip(gca, gcr)]
    gam_s = [jnp.where(strict, jnp.exp(jnp.where(strict, x, 0.0)), 0.0) for x in diff]
    gam_i = [jnp.where(incl, jnp.exp(jnp.where(incl, x, 0.0)), 0.0) for x in diff]

    kk = [_mm(k, k, NT) for k in ks]
    lmats = [beta[h] * kk[h] * gam_s[h] for h in heads]
    tinv = _unit_lower_inverse(lmats) if tsaved is None else _saved_unit_lower_inverse(lmats, tsaved)

    eg = [jnp.exp(x) for x in gc]
    u = [_mm(tinv[h], vs[h] * beta[h], NN) for h in heads]
    w = [_mm(tinv[h], ks[h] * (beta[h] * eg[h]), NN) for h in heads]
    qk = [_mm(qs[h], ks[h], NT) * gam_i[h] for h in heads]
    vn = [u[h] - _mm(w[h], ss[h], NN) for h in heads]
    o = [_mm(qs[h] * eg[h], ss[h], NN) + _mm(qk[h], vn[h], NN) for h in heads]
    sn = [jnp.exp(gl[h]) * ss[h] + _mm(ks[h] * jnp.exp(gl[h] - gc[h]), vn[h], TN) for h in heads]
    y = [_rms(o[h], og) * jax.nn.silu(zs[h]) for h in heads]
    return (sn, y), tinv


def _chain_blocks(ref):
    return [ref[b, :, LANE * h:LANE * (h + 1)] for b in range(ref.shape[0]) for h in range(NH)]


def _ride_along(ex, first, last):
    if ex is None:
        return

    @pl.when(first)
    def _():
        ex.start()

    @pl.when(last)
    def _():
        ex.wait()


def _delta_fwd(qkvn, proj, alog, dtb, og, bsz, t, gather=()):
    n = bsz * t
    nc = t // CH
    ng = len(gather)

    def body(q_ref, k_ref, v_ref, ba_ref, z_ref, al_ref, dt_ref, og_ref, *rest):
        g_in, (y_ref, sh_ref, ti_ref) = rest[:ng], rest[ng:ng + 3]
        g_out, (s_scr, *sems) = rest[ng + 3:2 * ng + 3], rest[2 * ng + 3:]
        ci = pl.program_id(0)
        _ride_along(_Exchange(False, g_in, g_out, *sems) if ng else None, ci == 0, ci == nc - 1)

        @pl.when(ci == 0)
        def _():
            s_scr[...] = jnp.zeros_like(s_scr)

        chains = range(bsz * NH)
        ss = [s_scr[e] for e in chains]
        for e in chains:
            sh_ref[e // NH, e % NH] = ss[e]
        (sn, y), tinv = _delta_chunk(ss, _chain_blocks(q_ref), _chain_blocks(k_ref), _chain_blocks(v_ref),
                                     [ba_ref[b] for b in range(bsz)], _chain_blocks(z_ref),
                                     al_ref[...], dt_ref[...], og_ref[...])
        for e in chains:
            s_scr[e] = sn[e]
            ti_ref[e // NH, e % NH] = tinv[e]
            y_ref[e // NH, :, LANE * (e % NH):LANE * (e % NH + 1)] = y[e]

    def blk(width, col):
        return pl.BlockSpec((bsz, CH, width), lambda ci: (0, ci, col))

    def per_chunk(rows):
        return pl.BlockSpec((bsz, None, NH, rows, rows), lambda ci: (0, ci, 0, 0, 0))

    hbm = pl.BlockSpec(memory_space=pltpu.HBM)
    qkv3, proj3 = qkvn.reshape(bsz, t, 3 * D), proj.reshape(bsz, t, NP)
    y, *rest = pl.pallas_call(
        body, name="delta_fwd_gather" if ng else "delta_fwd", grid=(nc,),
        in_specs=[blk(D, 0), blk(D, 1), blk(D, 2), blk(LANE, C_BA // LANE), blk(D, C_BZ // D),
                  _full((1, LANE)), _full((1, LANE)), _full((1, LANE))] + [hbm] * ng,
        out_specs=[blk(D, 0), per_chunk(LANE), per_chunk(CH)] + [hbm] * ng,
        out_shape=[SDS((bsz, t, D), F32), SDS((bsz, nc, NH, LANE, LANE), F32), SDS((bsz, nc, NH, CH, CH), F32)]
        + [SDS((NDEV,) + a.shape, a.dtype) for a in gather],
        scratch_shapes=[pltpu.VMEM((bsz * NH, LANE, LANE), F32)] + (_exchange_sems(ng) if ng else []),
        compiler_params=_cp(48),
    )(qkv3, qkv3, qkv3, proj3, proj3, alog, dtb, og, *gather)
    return (y.reshape(n, D), *rest)


def _delta_bwd(dproj, qkvn, proj, shist, tsaved, dyb, alog, dtb, og, bsz, t, scatter=()):
    n = bsz * t
    nc = t // CH
    ns = len(scatter)
    eg = 1

    def body(dp_any, q_ref, k_ref, v_ref, ba_ref, z_ref, sh_ref, ti_ref, dy_ref, al_ref, dt_ref, og_ref, *rest):
        s_in, (dp_ref, dqkv_ref, gal_ref, gdt_ref, gog_ref) = rest[:ns], rest[ns:ns + 5]
        s_out, (ds_scr, *sems) = rest[ns + 5:2 * ns + 5], rest[2 * ns + 5:]
        bi, ci = pl.program_id(0), pl.program_id(1)
        _ride_along(_Exchange(True, s_in, s_out, *sems) if ns else None,
                    (bi == 0) & (ci == 0), (bi == bsz // eg - 1) & (ci == nc - 1))

        @pl.when((bi == 0) & (ci == 0))
        def _():
            gal_ref[...] = jnp.zeros_like(gal_ref)
            gdt_ref[...] = jnp.zeros_like(gdt_ref)
            gog_ref[...] = jnp.zeros_like(gog_ref)

        @pl.when(ci == 0)
        def _():
            ds_scr[...] = jnp.zeros_like(ds_scr)

        chains = range(eg * NH)
        _, vjp, _ = jax.vjp(_delta_chunk, [sh_ref[e // NH, e % NH] for e in chains], _chain_blocks(q_ref),
                            _chain_blocks(k_ref), _chain_blocks(v_ref), [ba_ref[b] for b in range(eg)],
                            _chain_blocks(z_ref), al_ref[...], dt_ref[...], og_ref[...],
                            [ti_ref[e // NH, e % NH] for e in chains], has_aux=True)
        ds, dq, dk, dv, dba, dz, dal, ddt, dog, _ = vjp(([ds_scr[e] for e in chains], _chain_blocks(dy_ref)))
        gal_ref[...] += dal
        gdt_ref[...] += ddt
        gog_ref[...] += dog
        for b in range(eg):
            dp_ref[b, :, D:D + LANE] = dba[b].astype(BF16)
        for e in chains:
            b, lo = e // NH, LANE * (e % NH)
            ds_scr[e] = ds[e]
            dp_ref[b, :, lo:lo + LANE] = dz[e].astype(BF16)
            dqkv_ref[b, :, lo:lo + LANE] = dq[e]
            dqkv_ref[b, :, D + lo:D + lo + LANE] = dk[e]
            dqkv_ref[b, :, 2 * D + lo:2 * D + lo + LANE] = dv[e]

    def blk(width, col):
        return pl.BlockSpec((eg, CH, width), lambda bi, ci: (bi, nc - 1 - ci, col))

    def per_chunk(rows):
        return pl.BlockSpec((eg, None, NH, rows, rows), lambda bi, ci: (bi, nc - 1 - ci, 0, 0, 0))

    hbm = pl.BlockSpec(memory_space=pltpu.HBM)
    qkv3, proj3 = qkvn.reshape(bsz, t, 3 * D), proj.reshape(bsz, t, NP)
    dproj, dqkvn, *rest = pl.pallas_call(
        body, name="delta_bwd_exchange" if ns else "delta_bwd", grid=(bsz // eg, nc),
        in_specs=[pl.BlockSpec(memory_space=pl.ANY), blk(D, 0), blk(D, 1), blk(D, 2), blk(LANE, C_BA // LANE),
                  blk(D, C_BZ // D), per_chunk(LANE), per_chunk(CH),
                  blk(D, 0), _full((1, LANE)), _full((1, LANE)), _full((1, LANE))] + [hbm] * ns,
        out_specs=[blk(D + LANE, C_BZ // (D + LANE)), blk(3 * D, 0), _full((1, LANE)), _full((1, LANE)),
                   _full((1, LANE))] + [hbm] * ns,
        out_shape=[SDS((bsz, t, NP), BF16), SDS((bsz, t, 3 * D), F32), SDS((1, LANE), F32), SDS((1, LANE), F32),
                   SDS((1, LANE), F32)] + [SDS(a.shape, a.dtype) for a in scatter],
        input_output_aliases={0: 0},
        scratch_shapes=[pltpu.VMEM((eg * NH, LANE, LANE), F32)] + (_exchange_sems(ns) if ns else []),
        compiler_params=_cp(56),
    )(dproj.reshape(bsz, t, NP), qkv3, qkv3, qkv3, proj3, proj3, shist, tsaved, dyb.reshape(bsz, t, D), alog, dtb, og,
      *scatter)
    return (dproj.reshape(n, NP), dqkvn.reshape(n, 3 * D), *rest)


def _c_chunk(us, vs, zs, lgs, lbs, ws, bsb):
    gv = [jax.nn.gelu(v) for v in vs]
    width = LANE * len(gv)
    mu = sum(jnp.sum(x, axis=-1, keepdims=True) for x in gv) / width
    var = sum(jnp.sum(jnp.square(x - mu), axis=-1, keepdims=True) for x in gv) / width
    rstd = lax.rsqrt(var + EPS)
    r = lax.broadcasted_iota(jnp.int32, (SG, SG), 0)
    c = lax.broadcasted_iota(jnp.int32, (SG, SG), 1)
    out = []
    for j in range(len(gv)):
        nrm = (gv[j] - mu) * rstd * lgs[j] + lbs[j]
        mixed = jnp.dot(jnp.where(r >= c, ws[j], 0.0), nrm, preferred_element_type=F32) + bsb[j]
        out.append(jax.nn.gelu(us[j]) * mixed * jax.nn.silu(zs[j]))
    return out


def _c_args(u_ref, v_ref, z_ref, lg_ref, lb_ref, ws_ref, bs_ref):
    sl = [slice(LANE * j, LANE * (j + 1)) for j in range(4)]
    return ([u_ref[:, s] for s in sl], [v_ref[:, s] for s in sl], [z_ref[:, s] for s in sl],
            [lg_ref[:, s] for s in sl], [lb_ref[:, s] for s in sl],
            [ws_ref[j] for j in range(4)], [bs_ref[j] for j in range(4)])


def _c_fwd(proj, lg, lb, ws, bsb, n):
    cb = C_C // 512

    def body(u_ref, v_ref, z_ref, lg_ref, lb_ref, ws_ref, bs_ref, y_ref):
        outs = _c_chunk(*_c_args(u_ref, v_ref, z_ref, lg_ref, lb_ref, ws_ref, bs_ref))
        for j, o in enumerate(outs):
            y_ref[:, LANE * j:LANE * (j + 1)] = o

    return pl.pallas_call(
        body, name="c_fwd", grid=(n // SG,),
        in_specs=[pl.BlockSpec((SG, 512), lambda i: (i, cb)), pl.BlockSpec((SG, 512), lambda i: (i, cb + 1)),
                  pl.BlockSpec((SG, 512), lambda i: (i, cb + 2)), _full((1, 512)), _full((1, 512)),
                  _full((4, SG, SG)), _full((4, SG, SG))],
        out_specs=pl.BlockSpec((SG, 512), lambda i: (i, 0)),
        out_shape=SDS((n, 512), F32), compiler_params=_cp(32),
    )(proj, proj, proj, lg, lb, ws, bsb)


def _c_bwd(dproj, proj, dy, lg, lb, ws, bsb, n):
    cb = C_C // 512

    def body(dp_any, u_ref, v_ref, z_ref, dy_ref, lg_ref, lb_ref, ws_ref, bs_ref,
             dp_ref, glg_ref, glb_ref, gws_ref, gbs_ref):
        @pl.when(pl.program_id(0) == 0)
        def _():
            glg_ref[...] = jnp.zeros_like(glg_ref)
            glb_ref[...] = jnp.zeros_like(glb_ref)
            gws_ref[...] = jnp.zeros_like(gws_ref)
            gbs_ref[...] = jnp.zeros_like(gbs_ref)

        _, vjp = jax.vjp(_c_chunk, *_c_args(u_ref, v_ref, z_ref, lg_ref, lb_ref, ws_ref, bs_ref))
        dus, dvs, dzs, dlgs, dlbs, dwss, dbss = vjp([dy_ref[:, LANE * j:LANE * (j + 1)] for j in range(4)])
        for j in range(4):
            sl = slice(LANE * j, LANE * (j + 1))
            dp_ref[:, LANE * j:LANE * (j + 1)] = dus[j].astype(BF16)
            dp_ref[:, 512 + LANE * j:512 + LANE * (j + 1)] = dvs[j].astype(BF16)
            dp_ref[:, 1024 + LANE * j:1024 + LANE * (j + 1)] = dzs[j].astype(BF16)
            glg_ref[:, sl] += dlgs[j]
            glb_ref[:, sl] += dlbs[j]
            gws_ref[j] += dwss[j]
            gbs_ref[j] += jnp.broadcast_to(jnp.sum(dbss[j], axis=-1, keepdims=True), (SG, SG))

    return pl.pallas_call(
        body, name="c_bwd", grid=(n // SG,),
        in_specs=[pl.BlockSpec(memory_space=pl.ANY),
                  pl.BlockSpec((SG, 512), lambda i: (i, cb)), pl.BlockSpec((SG, 512), lambda i: (i, cb + 1)),
                  pl.BlockSpec((SG, 512), lambda i: (i, cb + 2)), pl.BlockSpec((SG, 512), lambda i: (i, 0)),
                  _full((1, 512)), _full((1, 512)), _full((4, SG, SG)), _full((4, SG, SG))],
        out_specs=[pl.BlockSpec((SG, 1536), lambda i: (i, C_C // 1536)),
                   _full((1, 512)), _full((1, 512)), _full((4, SG, SG)), _full((4, SG, SG))],
        out_shape=[SDS((n, NP), BF16), SDS((1, 512), F32), SDS((1, 512), F32), SDS((4, SG, SG), F32), SDS((4, SG, SG), F32)],
        input_output_aliases={0: 0}, compiler_params=_cp(32),
    )(dproj, proj, proj, proj, dy, lg, lb, ws, bsb)


def _merge_fwd(x2d, ya, yb, yc, proj, ap, bp, cp, wo):
    n = x2d.shape[0]
    tm = _tile(n, 512)
    gb = C_G // D

    def body(x_ref, ya_ref, yb_ref, yc_ref, g0_ref, g1_ref, g2_ref, ap_ref, bp_ref, cp_ref, wo_ref, o_ref):
        merged = (jax.nn.sigmoid(g0_ref[...]) * _bdot(ya_ref[...], ap_ref[...])
                  + jax.nn.sigmoid(g1_ref[...]) * _bdot(yb_ref[...], bp_ref[...])
                  + jax.nn.sigmoid(g2_ref[...]) * _bdot(yc_ref[...], cp_ref[...]))
        o_ref[...] = x_ref[...] + _bdot(merged, wo_ref[...])

    def rows(w):
        return pl.BlockSpec((tm, w), lambda i: (i, 0))

    return pl.pallas_call(
        body, name="merge_fwd", grid=(n // tm,),
        in_specs=[rows(D), rows(512), rows(D), rows(512),
                  pl.BlockSpec((tm, D), lambda i: (i, gb)), pl.BlockSpec((tm, D), lambda i: (i, gb + 1)),
                  pl.BlockSpec((tm, D), lambda i: (i, gb + 2)),
                  _resident((512, D)), _resident((D, D)), _resident((512, D)), _resident((D, D))],
        out_specs=rows(D), out_shape=SDS((n, D), F32), compiler_params=_cp(48),
    )(x2d, ya, yb, yc, proj, proj, proj, ap, bp, cp, wo)


def _merge_bwd(dxo, ya, yb, yc, proj, ap, bp, cp, apt, bpt, cpt, wot):
    n = dxo.shape[0]
    tm = _tile(n, 256)
    gb = C_G // D

    def body(d_ref, ya_ref, yb_ref, yc_ref, g0_ref, g1_ref, g2_ref, ap_ref, bp_ref, cp_ref,
             apt_ref, bpt_ref, cpt_ref, wot_ref,
             dp_ref, dya_ref, dyb_ref, dyc_ref, dpa_ref, dpb_ref, dpc_ref, mg_ref):
        dm = _bdot(d_ref[...], wot_ref[...])
        merged = None
        for j, (g_ref, y_ref, w_ref, wt_ref, dy_ref, dpj_ref) in enumerate((
                (g0_ref, ya_ref, ap_ref, apt_ref, dya_ref, dpa_ref),
                (g1_ref, yb_ref, bp_ref, bpt_ref, dyb_ref, dpb_ref),
                (g2_ref, yc_ref, cp_ref, cpt_ref, dyc_ref, dpc_ref))):
            s = jax.nn.sigmoid(g_ref[...])
            pj = _bdot(y_ref[...], w_ref[...])
            merged = s * pj if merged is None else merged + s * pj
            dp_ref[:, D * j:D * (j + 1)] = (dm * pj * s * (1.0 - s)).astype(BF16)
            dpj = (dm * s).astype(BF16)
            dpj_ref[...] = dpj
            dy_ref[...] = jnp.dot(dpj, wt_ref[...], preferred_element_type=F32)
        mg_ref[...] = merged

    def rows(w):
        return pl.BlockSpec((tm, w), lambda i: (i, 0))

    return pl.pallas_call(
        body, name="merge_bwd", grid=(n // tm,),
        in_specs=[rows(D), rows(512), rows(D), rows(512),
                  pl.BlockSpec((tm, D), lambda i: (i, gb)), pl.BlockSpec((tm, D), lambda i: (i, gb + 1)),
                  pl.BlockSpec((tm, D), lambda i: (i, gb + 2)),
                  _resident((512, D)), _resident((D, D)), _resident((512, D)),
                  _resident((D, 512)), _resident((D, D)), _resident((D, 512)), _resident((D, D))],
        out_specs=[pl.BlockSpec((tm, 3 * D), lambda i: (i, C_G // (3 * D))), rows(512), rows(D), rows(512),
                   rows(D), rows(D), rows(D), rows(D)],
        out_shape=[SDS((n, NP), BF16), SDS((n, 512), F32), SDS((n, D), F32), SDS((n, 512), F32),
                   SDS((n, D), BF16), SDS((n, D), BF16), SDS((n, D), BF16), SDS((n, D), F32)],
        compiler_params=_cp(56),
    )(dxo, ya, yb, yc, proj, proj, proj, ap, bp, cp, apt, bpt, cpt, wot)


def _sum_parts(p_ref):
    g = p_ref[0].astype(F32)
    for s in range(1, NDEV):
        g = g + p_ref[s].astype(F32)
    return g


def _adamw(g, w, m, v):
    nm = ADAM_B1 * m + (1.0 - ADAM_B1) * g
    nv = ADAM_B2 * v + (1.0 - ADAM_B2) * jnp.square(g)
    nm_hat = nm / (1.0 - ADAM_B1 ** ADAM_STEP)
    nv_hat = nv / (1.0 - ADAM_B2 ** ADAM_STEP)
    return -ADAM_LR * (nm_hat / (jnp.sqrt(nv_hat) + ADAM_EPS) + ADAM_WD * w), nm, nv


def _reduce_adamw(parts, w, m, v, name):
    r, c = w.shape
    tr = _tile(r, 128)

    def body(p_ref, w_ref, m_ref, v_ref, g_ref, d_ref, nm_ref, nv_ref):
        g = _sum_parts(p_ref)
        g_ref[...] = g
        d_ref[...], nm_ref[...], nv_ref[...] = _adamw(g, w_ref[...], m_ref[...], v_ref[...])

    blk = pl.BlockSpec((tr, c), lambda i: (i, 0))
    return pl.pallas_call(
        body, name=name, grid=(r // tr,),
        in_specs=[pl.BlockSpec((NDEV, tr, c), lambda i: (0, i, 0)), blk, blk, blk],
        out_specs=[blk, blk, blk, blk], out_shape=[SDS((r, c), F32)] * 4, compiler_params=_cp(48),
    )(parts, w, m, v)


def _reduce_adamw_leaves(parts, ws, ms, vs, name):
    nleaf = len(ws)
    counts = [len(p) if isinstance(p, (list, tuple)) else 0 for p in parts]
    flat = [a for p in parts for a in (p if isinstance(p, (list, tuple)) else [p])]

    def body(*refs):
        p_refs, rest = refs[:len(flat)], refs[len(flat):]
        w_refs, m_refs, v_refs = rest[:nleaf], rest[nleaf:2 * nleaf], rest[2 * nleaf:3 * nleaf]
        outs = rest[3 * nleaf:]
        at = 0
        for i in range(nleaf):
            g_ref, d_ref, nm_ref, nv_ref = outs[i], outs[nleaf + i], outs[2 * nleaf + i], outs[3 * nleaf + i]
            for idx in (range(counts[i]) if counts[i] else [Ellipsis]):
                g = _sum_parts(p_refs[at])
                at += 1
                g_ref[idx] = g
                d_ref[idx], nm_ref[idx], nv_ref[idx] = _adamw(g, w_refs[i][idx], m_refs[i][idx], v_refs[i][idx])

    vm = pl.BlockSpec(memory_space=pltpu.VMEM)
    outs = pl.pallas_call(
        body, name=name, in_specs=[vm] * (len(flat) + 3 * nleaf), out_specs=[vm] * (4 * nleaf),
        out_shape=[SDS(w.shape, F32) for w in ws] * 4, compiler_params=_cp(56),
    )(*flat, *ws, *ms, *vs)
    return [outs[j * nleaf:(j + 1) * nleaf] for j in range(4)]


def _unshard(name, g):
    if name in ROW_SHARDED:
        return g.reshape(g.shape[0] * g.shape[1], g.shape[2])
    g = jnp.moveaxis(g, 0, 1)
    return g.reshape(g.shape[0], g.shape[1] * g.shape[2])


def _reshard(name, full):
    r, c = full.shape
    if name in ROW_SHARDED:
        return full.reshape(NDEV, r // NDEV, c)
    return jnp.moveaxis(full.reshape(r, NDEV, c // NDEV), 1, 0)


def _w_in_to_padded(slabs):
    pieces = []
    for lo, hi, _ in sorted(SEGMENTS, key=lambda s: s[2]):
        for d in range(NDEV):
            a, b = max(lo, d * W_SHARD), min(hi, (d + 1) * W_SHARD)
            if a < b:
                pieces.append(slabs[d, :, a - d * W_SHARD:b - d * W_SHARD])
    pieces.append(jnp.zeros(slabs.shape[1:2] + (NP - C_BA - 16,), slabs.dtype))
    return jnp.concatenate(pieces, axis=-1)


def _w_in_from_padded(g):
    slabs = []
    for d in range(NDEV):
        pieces = []
        for lo, hi, pstart in SEGMENTS:
            a, b = max(lo, d * W_SHARD), min(hi, (d + 1) * W_SHARD)
            if a < b:
                pieces.append(g[:, pstart + a - lo:pstart + b - lo])
        pieces.append(jnp.zeros(g.shape[:1] + (W_SHARD_PAD - W_SHARD,), g.dtype))
        slabs.append(jnp.concatenate(pieces, axis=-1))
    return jnp.stack(slabs)


def _pad_w_in(w):
    return jnp.pad(w, ((0, 0), (0, W_SHARD_PAD - W_SHARD)))


def _lane_row(vec8, offset):
    return jnp.pad(vec8, (offset, LANE - NH - offset))[None]


def kernel(x, norm_g, w_in, a_dw, a_dw_b, a_ln_g, a_ln_b, a_proj, b_conv, b_a_log, b_dt_bias, b_onorm_g, b_proj, c_ln_g, c_ln_b, c_ws, c_bs, c_proj, w_out, final_g, loss_target, m_norm_g, m_w_in, m_a_dw, m_a_dw_b, m_a_ln_g, m_a_ln_b, m_a_proj, m_b_conv, m_b_a_log, m_b_dt_bias, m_b_onorm_g, m_b_proj, m_c_ln_g, m_c_ln_b, m_c_ws, m_c_bs, m_c_proj, m_w_out, m_final_g, v_norm_g, v_w_in, v_a_dw, v_a_dw_b, v_a_ln_g, v_a_ln_b, v_a_proj, v_b_conv, v_b_a_log, v_b_dt_bias, v_b_onorm_g, v_b_proj, v_c_ln_g, v_c_ln_b, v_c_ws, v_c_bs, v_c_proj, v_w_out, v_final_g):
    wts = dict(norm_g=norm_g, w_in=w_in, a_dw=a_dw, a_dw_b=a_dw_b, a_ln_g=a_ln_g, a_ln_b=a_ln_b, a_proj=a_proj,
               b_conv=b_conv, b_a_log=b_a_log, b_dt_bias=b_dt_bias, b_onorm_g=b_onorm_g, b_proj=b_proj,
               c_ln_g=c_ln_g, c_ln_b=c_ln_b, c_ws=c_ws, c_bs=c_bs, c_proj=c_proj, w_out=w_out, final_g=final_g)
    mom = dict(norm_g=m_norm_g, w_in=m_w_in, a_dw=m_a_dw, a_dw_b=m_a_dw_b, a_ln_g=m_a_ln_g, a_ln_b=m_a_ln_b,
               a_proj=m_a_proj, b_conv=m_b_conv, b_a_log=m_b_a_log, b_dt_bias=m_b_dt_bias, b_onorm_g=m_b_onorm_g,
               b_proj=m_b_proj, c_ln_g=m_c_ln_g, c_ln_b=m_c_ln_b, c_ws=m_c_ws, c_bs=m_c_bs, c_proj=m_c_proj,
               w_out=m_w_out, final_g=m_final_g)
    vel = dict(norm_g=v_norm_g, w_in=v_w_in, a_dw=v_a_dw, a_dw_b=v_a_dw_b, a_ln_g=v_a_ln_g, a_ln_b=v_a_ln_b,
               a_proj=v_a_proj, b_conv=v_b_conv, b_a_log=v_b_a_log, b_dt_bias=v_b_dt_bias, b_onorm_g=v_b_onorm_g,
               b_proj=v_b_proj, c_ln_g=v_c_ln_g, c_ln_b=v_c_ln_b, c_ws=v_c_ws, c_bs=v_c_bs, c_proj=v_c_proj,
               w_out=v_w_out, final_g=v_final_g)

    bsz, t, _ = x.shape
    n = bsz * t
    depth = norm_g.shape[0]
    x2d = x.reshape(n, D)
    tgt = loss_target.reshape(n, D)

    def weight_blocks(l):
        return [_pad_w_in(w_in[l].astype(BF16))] + [wts[k][l].astype(BF16) for k in BIG_REST]

    def matmul_weights(w_in_all, *rest):
        got = {k: _unshard(k, g) for k, g in zip(BIG_REST, rest)}
        got['wp'] = _w_in_to_padded(w_in_all)
        return got

    conv_all = _all_gather([wts[k] for k in SMALL], "gather_conv_weights")
    conv_w = {k: jnp.stack([_unshard(k, g[:, l]) for l in range(depth)]) for k, g in zip(SMALL, conv_all)}
    a_dw32 = jnp.pad(conv_w['a_dw'], ((0, 0), (0, 32 - A_K), (0, 0)))
    b_conv8 = jnp.pad(conv_w['b_conv'], ((0, 0), (0, 8 - B_K), (0, 0)))
    bsb = jnp.broadcast_to(c_bs[..., None], c_bs.shape + (SG,))
    full = [matmul_weights(*_all_gather(weight_blocks(0), "gather_matmul_weights"))]

    saved = []
    xl = x2d
    for l in range(depth):
        alog, dtb = _lane_row(b_a_log[l], NH), _lane_row(b_dt_bias[l], NH)
        proj, h = _inproj(xl, norm_g[l][None], full[l]['wp'])
        ya, conv_a = _a_fwd(proj, a_dw32[l], a_dw_b[l][None], a_ln_g[l][None], a_ln_b[l][None], bsz, t)
        qkvn, conv_b = _bprep_fwd(proj, b_conv8[l], bsz, t)
        yb, shist, tsave, *nxt = _delta_fwd(qkvn, proj, alog, dtb, b_onorm_g[l][None], bsz, t,
                                            gather=weight_blocks(l + 1) if l + 1 < depth else ())
        if nxt:
            full.append(matmul_weights(*nxt))
        yc = _c_fwd(proj, c_ln_g[l][None], c_ln_b[l][None], c_ws[l], bsb[l], n)
        xn = _merge_fwd(xl, ya, yb, yc, proj, full[l]['a_proj'], full[l]['b_proj'], full[l]['c_proj'], full[l]['w_out'])
        saved.append((xl, proj, h, ya, yb, yc, qkvn, shist, tsave, alog, dtb, conv_a, conv_b))
        xl = xn

    dx, g_final, loss_blk = _loss_head(xl, final_g[None], tgt)
    loss = lax.psum(loss_blk[0, 0], ("x", "y", "c"))

    gfull = {k: [None] * depth for k in WEIGHTS if k != 'final_g'}
    recv = [None] * depth

    def grad_slabs(l):
        return [_w_in_from_padded(gfull['w_in'][l]).astype(BF16)] + [
            _reshard(k, gfull[k][l]).astype(BF16) for k in SHARDED_REST]

    for l in reversed(range(depth)):
        xl, proj, h, ya, yb, yc, qkvn, shist, tsave, alog, dtb, conv_a, conv_b = saved[l]
        ap, bp, cp, wo = full[l]['a_proj'], full[l]['b_proj'], full[l]['c_proj'], full[l]['w_out']
        dproj, dya, dyb, dyc, dpa, dpb, dpc, merged = _merge_bwd(dx, ya, yb, yc, proj, ap, bp, cp, ap.T, bp.T, cp.T, wo.T)
        gfull['a_proj'][l] = _mm_tn(ya, dpa, "grad_a_proj")
        gfull['b_proj'][l] = _mm_tn(yb, dpb, "grad_b_proj")
        gfull['c_proj'][l] = _mm_tn(yc, dpc, "grad_c_proj")
        gfull['w_out'][l] = _mm_tn(merged, dx, "grad_w_out")
        dproj, g_clg, g_clb, g_cws, g_cbs = _c_bwd(dproj, proj, dyc, c_ln_g[l][None], c_ln_b[l][None], c_ws[l], bsb[l], n)
        dproj, g_adw, g_adb, g_alg, g_alb = _a_bwd(dproj, proj, conv_a, dya, a_dw32[l], a_ln_g[l][None], a_ln_b[l][None],
                                                   bsz, t)
        dproj, dqkvn, g_alog, g_dt, g_og, *got = _delta_bwd(dproj, qkvn, proj, shist, tsave, dyb, alog, dtb, b_onorm_g[l][None],
                                                            bsz, t, scatter=grad_slabs(l + 1) if l + 1 < depth else ())
        if got:
            recv[l + 1] = got
        dproj, g_bconv = _bprep_bwd(dproj, proj, conv_b, dqkvn, b_conv8[l], bsz, t)
        gfull['w_in'][l] = _mm_tn(h, dproj, "grad_w_in")
        gfull['a_dw'][l] = g_adw[:A_K]
        gfull['b_conv'][l] = g_bconv[:B_K]
        dx, g_ng, *got = _inproj_bwd(dproj, full[l]['wp'], xl, norm_g[l][None], dx, scatter=grad_slabs(0) if l == 0 else ())
        if got:
            recv[0] = got
        gfull['norm_g'][l] = g_ng[0]
        gfull['a_dw_b'][l], gfull['a_ln_g'][l], gfull['a_ln_b'][l] = g_adb[0], g_alg[0], g_alb[0]
        gfull['b_a_log'][l], gfull['b_dt_bias'][l] = g_alog[0, NH:2 * NH], g_dt[0, NH:2 * NH]
        gfull['b_onorm_g'][l] = g_og[0]
        gfull['c_ln_g'][l], gfull['c_ln_b'][l] = g_clg[0], g_clb[0]
        gfull['c_ws'][l], gfull['c_bs'][l] = g_cws, g_cbs[:, :, 0]
    grad_x = dx.reshape(bsz, t, D)

    outs_w = [_reduce_adamw(recv[l][0], _pad_w_in(w_in[l]), _pad_w_in(m_w_in[l]), _pad_w_in(v_w_in[l]), "adamw_w_in")
              for l in range(depth)]
    outs_s = _reduce_adamw_leaves([[recv[l][1 + i] for l in range(depth)] for i in range(len(SHARDED_REST))],
                                  [wts[k] for k in SHARDED_REST], [mom[k] for k in SHARDED_REST],
                                  [vel[k] for k in SHARDED_REST], "adamw_sharded")

    def upto3d(a):
        return a[None] if a.ndim == 1 else a.reshape((-1,) + a.shape[-2:]) if a.ndim > 3 else a

    grepl = [upto3d(jnp.stack(gfull[k]) if k != 'final_g' else g_final[0]) for k in REPL]
    outs_r = _reduce_adamw_leaves(_all_gather(grepl, "gather_replicated_grads"), [upto3d(wts[k]) for k in REPL],
                                  [upto3d(mom[k]) for k in REPL], [upto3d(vel[k]) for k in REPL], "adamw_replicated")

    res = []
    for j in range(4):
        leaves = {k: outs_r[j][i].reshape(wts[k].shape) for i, k in enumerate(REPL)}
        leaves.update({k: outs_s[j][i] for i, k in enumerate(SHARDED_REST)})
        leaves['w_in'] = jnp.stack([outs_w[l][j][:, :W_SHARD] for l in range(depth)])
        res.append([leaves[k] for k in WEIGHTS])
    grads, deltas, new_m, new_v = res
    return (loss, grad_x, *grads, *deltas, *new_m, *new_v)
```

```python
import jax
import jax.numpy as jnp
from jax import lax
from jax.experimental import pallas as pl
from jax.experimental.pallas import tpu as pltpu

F32 = jnp.float32
BF16 = jnp.bfloat16
SDS = jax.ShapeDtypeStruct
MESH = pl.DeviceIdType.MESH

NDEV = 8
D = 1024
EPS = 1e-6
LANE = 128

C_Q, C_K, C_V = 0, 1024, 2048
C_G = 3072
C_A = 6144
C_C = 7680
C_BZ = 9216
C_BA = 10240
NP = 10368
N_IN = 10256
SEGMENTS = ((0, 1536, C_A), (1536, 4608, C_Q), (4608, 5632, C_BZ), (5632, 5648, C_BA), (5648, 7184, C_C), (7184, 10256, C_G))
W_SHARD = N_IN // NDEV
W_SHARD_PAD = 1408

A_K, A_H, A_RB = 31, 32, 32
B_K, B_H = 4, 8
CH = 64
SG = 128
NH = 8

ADAM_LR, ADAM_B1, ADAM_B2, ADAM_EPS, ADAM_WD, ADAM_STEP = 0.001, 0.9, 0.999, 1e-08, 0.01, 10

WEIGHTS = ['norm_g', 'w_in', 'a_dw', 'a_dw_b', 'a_ln_g', 'a_ln_b', 'a_proj', 'b_conv', 'b_a_log', 'b_dt_bias',
           'b_onorm_g', 'b_proj', 'c_ln_g', 'c_ln_b', 'c_ws', 'c_bs', 'c_proj', 'w_out', 'final_g']
SHARDED_REST = ['a_dw', 'a_proj', 'b_conv', 'b_proj', 'c_proj', 'w_out']
REPL = [n for n in WEIGHTS if n != 'w_in' and n not in SHARDED_REST]
BIG_REST = ['a_proj', 'b_proj', 'c_proj', 'w_out']
SMALL = ['a_dw', 'b_conv']
ROW_SHARDED = ('b_proj', 'w_out')

NN = ((1,), (0,))
NT = ((1,), (1,))
TN = ((0,), (0,))


def _tile(n, pref):
    return pref if (n >= pref and n % pref == 0) else n


def _cp(vmem_mb):
    return pltpu.CompilerParams(vmem_limit_bytes=vmem_mb * 2 ** 20)


def _full(shape):
    nd = len(shape)
    return pl.BlockSpec(shape, lambda *_: (0,) * nd)


def _resident(shape):
    nd = len(shape)
    return pl.BlockSpec(shape, lambda *_: (0,) * nd, pipeline_mode=pl.Buffered(1))


def _rms(x, g):
    return x * lax.rsqrt(jnp.mean(x * x, axis=-1, keepdims=True) + EPS) * g


def _softplus(x):
    return jnp.maximum(x, 0.0) + jnp.log1p(jnp.exp(-jnp.abs(x)))


def _bdot(a, b):
    return jnp.dot(a.astype(BF16), b.astype(BF16), preferred_element_type=F32)


def _mm(a, b, dims):
    return lax.dot_general(a, b, (dims, ((), ())), preferred_element_type=F32)


def _all_gather(xs, name):
    nops = len(xs)

    def body(*refs):
        x_refs, out_refs = refs[:nops], refs[nops:2 * nops]
        send_sems, recv_sems, local_sems = refs[2 * nops:]
        x, y, cc = lax.axis_index("x"), lax.axis_index("y"), lax.axis_index("c")
        me, sibling = (x, y, cc), (x, y, 1 - cc)
        chips = [(1 - x, y), (x, 1 - y), (1 - x, 1 - y)]

        def slot(t, px, py, pc):
            return out_refs[t].at[4 * px + 2 * py + pc]

        def copy(t, k, block, to, src=None):
            return pltpu.make_async_remote_copy(
                src_ref=slot(t, *block) if src is None else src, dst_ref=slot(t, *block),
                send_sem=send_sems.at[7 * t + k], recv_sem=recv_sems.at[7 * t + k], device_id=to, device_id_type=MESH)

        ops = range(nops)
        mine = [pltpu.make_async_copy(x_refs[t], slot(t, *me), local_sems.at[t]) for t in ops]
        for cp in mine:
            cp.start()
        first = [copy(t, 0, me, sibling, src=x_refs[t]) for t in ops]
        first += [copy(t, 1 + j, me, (*chip, cc), src=x_refs[t]) for j, chip in enumerate(chips) for t in ops]
        for cp in first:
            cp.start()
        passed = []
        for j, chip in enumerate(chips):
            for t in ops:
                copy(t, 1 + j, (*chip, cc), me).wait_recv()
                fwd = copy(t, 4 + j, (*chip, cc), sibling)
                fwd.start()
                passed.append(fwd)
        for t in ops:
            copy(t, 0, sibling, me).wait_recv()
        for j, chip in enumerate(chips):
            for t in ops:
                copy(t, 4 + j, (*chip, 1 - cc), me).wait_recv()
        for cp in first + passed:
            cp.wait_send()
        for cp in mine:
            cp.wait()

    hbm = pl.BlockSpec(memory_space=pltpu.HBM)
    return pl.pallas_call(
        body, name=name, out_shape=[SDS((NDEV,) + a.shape, a.dtype) for a in xs],
        in_specs=[hbm] * nops, out_specs=[hbm] * nops,
        scratch_shapes=[pltpu.SemaphoreType.DMA((7 * nops,)), pltpu.SemaphoreType.DMA((7 * nops,)),
                        pltpu.SemaphoreType.DMA((nops,))],
    )(*xs)


def _exchange_sems(nops):
    return [pltpu.SemaphoreType.DMA((7 * nops,)), pltpu.SemaphoreType.DMA((7 * nops,)), pltpu.SemaphoreType.DMA((nops,))]


class _Exchange:
    def __init__(self, scatter, in_refs, out_refs, send_sems, recv_sems, local_sems):
        x, y, cc = lax.axis_index("x"), lax.axis_index("y"), lax.axis_index("c")
        me = 4 * x + 2 * y + cc
        nops = len(in_refs)
        self.local = [pltpu.make_async_copy(in_refs[t].at[me] if scatter else in_refs[t], out_refs[t].at[me],
                                            local_sems.at[t]) for t in range(nops)]
        self.sends, self.recvs = [], []
        for k in range(1, NDEV):
            px = 1 - x if k & 4 else x
            py = 1 - y if k & 2 else y
            pc = 1 - cc if k & 1 else cc
            peer = 4 * px + 2 * py + pc
            for t in range(nops):
                sem = 7 * t + k - 1
                self.sends.append(pltpu.make_async_remote_copy(
                    src_ref=in_refs[t].at[peer] if scatter else in_refs[t], dst_ref=out_refs[t].at[me],
                    send_sem=send_sems.at[sem], recv_sem=recv_sems.at[sem], device_id=(px, py, pc), device_id_type=MESH))
                self.recvs.append(pltpu.make_async_remote_copy(
                    src_ref=in_refs[t].at[me] if scatter else in_refs[t], dst_ref=out_refs[t].at[peer],
                    send_sem=send_sems.at[sem], recv_sem=recv_sems.at[sem], device_id=(px, py, pc), device_id_type=MESH))

    def start(self):
        for cp in self.local + self.sends:
            cp.start()

    def wait(self):
        for cp in self.recvs:
            cp.wait_recv()
        for cp in self.sends:
            cp.wait_send()
        for cp in self.local:
            cp.wait()


def _inproj(x2d, g_row, wp):
    n = x2d.shape[0]
    tm, tn = _tile(n, 1024), 1152

    def body(x_ref, g_ref, w_ref, proj_ref, h_ref, hs):
        @pl.when(pl.program_id(1) == 0)
        def _():
            h = _rms(x_ref[...], g_ref[...]).astype(BF16)
            hs[...] = h
            h_ref[...] = h

        proj_ref[...] = jnp.dot(hs[...], w_ref[...], preferred_element_type=F32)

    return pl.pallas_call(
        body, name="inproj", grid=(n // tm, NP // tn),
        in_specs=[pl.BlockSpec((tm, D), lambda i, j: (i, 0)), _full((1, D)), pl.BlockSpec((D, tn), lambda i, j: (0, j))],
        out_specs=[pl.BlockSpec((tm, tn), lambda i, j: (i, j)), pl.BlockSpec((tm, D), lambda i, j: (i, 0))],
        out_shape=[SDS((n, NP), F32), SDS((n, D), BF16)],
        scratch_shapes=[pltpu.VMEM((tm, D), BF16)], compiler_params=_cp(48),
    )(x2d, g_row, wp)


def _mm_tn(a, b, name):
    nn, m = a.shape
    k = b.shape[1]
    tk = 1152 if k % 1152 == 0 else _tile(k, 1024)
    tn = _tile(nn, 1024)

    def body(a_ref, b_ref, o_ref):
        p = _mm(a_ref[...].astype(BF16), b_ref[...].astype(BF16), TN)

        @pl.when(pl.program_id(1) == 0)
        def _():
            o_ref[...] = p

        @pl.when(pl.program_id(1) > 0)
        def _():
            o_ref[...] += p

    return pl.pallas_call(
        body, name=name, grid=(k // tk, nn // tn),
        in_specs=[pl.BlockSpec((tn, m), lambda j, t: (t, 0)), pl.BlockSpec((tn, tk), lambda j, t: (t, j))],
        out_specs=pl.BlockSpec((m, tk), lambda j, t: (0, j)),
        out_shape=SDS((m, k), F32), compiler_params=_cp(48),
    )(a, b)


def _inproj_bwd(dproj, wp, x2d, g_row, dxo, scatter=()):
    n = x2d.shape[0]
    tm, tk = _tile(n, 1024), 1152
    nk = NP // tk
    ns = len(scatter)

    def body(dp_ref, w_ref, x_ref, g_ref, dxo_ref, *rest):
        s_in, (dx_ref, dg_ref), s_out, (acc, *sems) = rest[:ns], rest[ns:ns + 2], rest[ns + 2:2 * ns + 2], rest[2 * ns + 2:]
        i, k = pl.program_id(0), pl.program_id(1)
        _ride_along(_Exchange(True, s_in, s_out, *sems) if ns else None,
                    (i == 0) & (k == 0), (i == n // tm - 1) & (k == nk - 1))
        p = _mm(dp_ref[...], w_ref[...], NT)

        @pl.when(k == 0)
        def _():
            acc[...] = p

        @pl.when(k > 0)
        def _():
            acc[...] += p

        @pl.when(k == nk - 1)
        def _():
            _, vjp = jax.vjp(_rms, x_ref[...], g_ref[...])
            dx, dg = vjp(acc[...])
            dx_ref[...] = dxo_ref[...] + dx

            @pl.when(i == 0)
            def _():
                dg_ref[...] = dg

            @pl.when(i > 0)
            def _():
                dg_ref[...] += dg

    hbm = pl.BlockSpec(memory_space=pltpu.HBM)
    return pl.pallas_call(
        body, name="inproj_bwd_exchange" if ns else "inproj_bwd", grid=(n // tm, nk),
        in_specs=[pl.BlockSpec((tm, tk), lambda i, k: (i, k)), pl.BlockSpec((D, tk), lambda i, k: (0, k)),
                  pl.BlockSpec((tm, D), lambda i, k: (i, 0)), _full((1, D)),
                  pl.BlockSpec((tm, D), lambda i, k: (i, 0))] + [hbm] * ns,
        out_specs=[pl.BlockSpec((tm, D), lambda i, k: (i, 0)), _full((1, D))] + [hbm] * ns,
        out_shape=[SDS((n, D), F32), SDS((1, D), F32)] + [SDS(a.shape, a.dtype) for a in scatter],
        scratch_shapes=[pltpu.VMEM((tm, D), F32)] + (_exchange_sems(ns) if ns else []), compiler_params=_cp(56),
    )(dproj, wp, x2d, g_row, dxo, *scatter)


def _loss_head(x2d, g_row, tgt):
    n = x2d.shape[0]
    tm = _tile(n, 512)

    def body(x_ref, g_ref, t_ref, dx_ref, dg_ref, loss_ref):
        i = pl.program_id(0)
        y, vjp = jax.vjp(_rms, x_ref[...], g_ref[...])
        err = y - t_ref[...]
        part = 0.5 * jnp.sum(jnp.mean(err * err, axis=-1, keepdims=True), axis=0, keepdims=True)
        dx, dg = vjp(err * (1.0 / D))
        dx_ref[...] = dx
        lb = jnp.broadcast_to(part, (8, LANE))

        @pl.when(i == 0)
        def _():
            dg_ref[...] = dg
            loss_ref[...] = lb

        @pl.when(i > 0)
        def _():
            dg_ref[...] += dg
            loss_ref[...] += lb

    return pl.pallas_call(
        body, name="loss_head", grid=(n // tm,),
        in_specs=[pl.BlockSpec((tm, D), lambda i: (i, 0)), _full((1, D)), pl.BlockSpec((tm, D), lambda i: (i, 0))],
        out_specs=[pl.BlockSpec((tm, D), lambda i: (i, 0)), _full((1, D)), _full((8, LANE))],
        out_shape=[SDS((n, D), F32), SDS((1, D), F32), SDS((8, LANE), F32)], compiler_params=_cp(40),
    )(x2d, g_row, tgt)


def _conv_rows(dst_ref, read, w_ref, offs, nrows, rb, bias=None):
    for r0 in range(0, nrows, rb):
        acc = bias
        for j, off in enumerate(offs):
            term = w_ref[j:j + 1, :] * read(off + r0, rb)
            acc = term if acc is None else acc + term
        dst_ref[r0:r0 + rb, :] = acc


def _fwd_offsets(kw, halo):
    return [halo - (kw - 1) + j for j in range(kw)]


def _bwd_offsets(kw):
    return [kw - 1 - j for j in range(kw)]


def _conv_bwd_w(dc, ext_ref, gw_ref, kw, halo, tt):
    for j in range(kw):
        gw_ref[j:j + 1, :] += jnp.sum(dc * ext_ref[pl.ds(halo - (kw - 1) + j, tt), :], axis=0, keepdims=True)


SUB = 8


def _fill_shifts(sh_ref, rows):
    for s in range(1, SUB):
        sh_ref[s, 0:rows, :] = sh_ref[0, pl.ds(s, rows), :]


def _tap(sh_ref, off, rows):
    return sh_ref[off % SUB, pl.ds(off - off % SUB, rows), :]


def _conv_bwd_w_sh(dc, ext_sh, gw_ref, kw, halo, tt):
    for j in range(kw):
        gw_ref[j:j + 1, :] += jnp.sum(dc * _tap(ext_sh, halo - (kw - 1) + j, tt), axis=0, keepdims=True)


def _a_post(c, z, g, b):
    mu = jnp.mean(c, axis=-1, keepdims=True)
    var = jnp.mean(jnp.square(c - mu), axis=-1, keepdims=True)
    a = (c - mu) * lax.rsqrt(var + EPS) * g + b
    return jax.nn.silu(a) * jax.nn.silu(z)


def _a_fwd(proj, dw, b_row, lg, lb, bsz, t):
    n = bsz * t
    tt = _tile(t, 256)
    nt = t // tt
    cb = C_A // 512

    def body(v_ref, g_ref, z_ref, vh_ref, gh_ref, w_ref, b_ref, lg_ref, lb_ref, y_ref, c_ref, ext):
        i = pl.program_id(1)
        ext[0, 0:A_H, :] = jnp.where(i > 0, vh_ref[...] * jax.nn.sigmoid(gh_ref[...]), 0.0)
        ext[0, A_H:, :] = v_ref[...] * jax.nn.sigmoid(g_ref[...])
        _fill_shifts(ext, tt + A_H - SUB)
        _conv_rows(c_ref, lambda off, rows: _tap(ext, off, rows), w_ref, _fwd_offsets(A_K, A_H), tt, A_RB, bias=b_ref[...])
        y_ref[...] = _a_post(c_ref[...], z_ref[...], lg_ref[...], lb_ref[...])

    def row(b, i):
        return b * nt + i

    def halo(b, i):
        return jnp.maximum((b * t + i * tt) // A_H - 1, 0)

    return pl.pallas_call(
        body, name="a_fwd", grid=(bsz, nt),
        in_specs=[pl.BlockSpec((tt, 512), lambda b, i: (row(b, i), cb)),
                  pl.BlockSpec((tt, 512), lambda b, i: (row(b, i), cb + 1)),
                  pl.BlockSpec((tt, 512), lambda b, i: (row(b, i), cb + 2)),
                  pl.BlockSpec((A_H, 512), lambda b, i: (halo(b, i), cb)),
                  pl.BlockSpec((A_H, 512), lambda b, i: (halo(b, i), cb + 1)),
                  _full((32, 512)), _full((1, 512)), _full((1, 512)), _full((1, 512))],
        out_specs=[pl.BlockSpec((tt, 512), lambda b, i: (row(b, i), 0))] * 2,
        out_shape=[SDS((n, 512), F32)] * 2,
        scratch_shapes=[pltpu.VMEM((SUB, tt + A_H, 512), F32)], compiler_params=_cp(40),
    )(proj, proj, proj, proj, proj, dw, b_row, lg, lb)


def _a_bwd(dproj, proj, conv, dy, dw, lg, lb, bsz, t):
    n = bsz * t
    tt = _tile(t, 256)
    nt = t // tt
    cb = C_A // 512

    def body(dp_any, v_ref, g_ref, z_ref, vh_ref, gh_ref, c_ref, dy_ref, w_ref, lg_ref, lb_ref,
             dp_ref, gw_ref, gb_ref, glg_ref, glb_ref, ext, dcp, dae, carry):
        b, i = pl.program_id(0), pl.program_id(1)
        ti = nt - 1 - i

        @pl.when((b == 0) & (i == 0))
        def _():
            gw_ref[...] = jnp.zeros_like(gw_ref)
            gb_ref[...] = jnp.zeros_like(gb_ref)
            glg_ref[...] = jnp.zeros_like(glg_ref)
            glb_ref[...] = jnp.zeros_like(glb_ref)

        @pl.when(i == 0)
        def _():
            carry[...] = jnp.zeros_like(carry)

        val, glu = v_ref[...], g_ref[...]
        sg = jax.nn.sigmoid(glu)
        ext[0, 0:A_H, :] = jnp.where(ti > 0, vh_ref[...] * jax.nn.sigmoid(gh_ref[...]), 0.0)
        ext[0, A_H:, :] = val * sg
        _fill_shifts(ext, tt + A_H - SUB)
        _, vjp = jax.vjp(_a_post, c_ref[...], z_ref[...], lg_ref[...], lb_ref[...])
        dc, dz, dlg, dlb = vjp(dy_ref[...])
        gb_ref[...] += jnp.sum(dc, axis=0, keepdims=True)
        glg_ref[...] += dlg
        glb_ref[...] += dlb
        dcp[0, 0:A_H, :] = jnp.zeros((A_H, 512), F32)
        dcp[0, A_H:A_H + tt, :] = dc
        dcp[0, A_H + tt:, :] = jnp.zeros((A_H, 512), F32)
        _fill_shifts(dcp, tt + 2 * A_H - SUB)
        _conv_bwd_w_sh(dc, ext, gw_ref, A_K, A_H, tt)
        _conv_rows(dae, lambda off, rows: _tap(dcp, off, rows), w_ref, _bwd_offsets(A_K), tt + A_H, A_RB)
        dae[tt:tt + A_H, :] += carry[...]
        carry[...] = dae[0:A_H, :]
        da = dae[A_H:, :]
        dp_ref[:, 0:512] = (da * sg).astype(BF16)
        dp_ref[:, 512:1024] = (da * val * sg * (1.0 - sg)).astype(BF16)
        dp_ref[:, 1024:1536] = dz.astype(BF16)

    def row(b, i):
        return b * nt + (nt - 1 - i)

    def halo(b, i):
        return jnp.maximum((b * t + (nt - 1 - i) * tt) // A_H - 1, 0)

    outs = pl.pallas_call(
        body, name="a_bwd", grid=(bsz, nt),
        in_specs=[pl.BlockSpec(memory_space=pl.ANY),
                  pl.BlockSpec((tt, 512), lambda b, i: (row(b, i), cb)),
                  pl.BlockSpec((tt, 512), lambda b, i: (row(b, i), cb + 1)),
                  pl.BlockSpec((tt, 512), lambda b, i: (row(b, i), cb + 2)),
                  pl.BlockSpec((A_H, 512), lambda b, i: (halo(b, i), cb)),
                  pl.BlockSpec((A_H, 512), lambda b, i: (halo(b, i), cb + 1)),
                  pl.BlockSpec((tt, 512), lambda b, i: (row(b, i), 0)),
                  pl.BlockSpec((tt, 512), lambda b, i: (row(b, i), 0)),
                  _full((32, 512)), _full((1, 512)), _full((1, 512))],
        out_specs=[pl.BlockSpec((tt, 1536), lambda b, i: (row(b, i), C_A // 1536)),
                   _full((32, 512)), _full((1, 512)), _full((1, 512)), _full((1, 512))],
        out_shape=[SDS((n, NP), BF16), SDS((32, 512), F32), SDS((1, 512), F32), SDS((1, 512), F32), SDS((1, 512), F32)],
        input_output_aliases={0: 0},
        scratch_shapes=[pltpu.VMEM((SUB, tt + A_H, 512), F32), pltpu.VMEM((SUB, tt + 2 * A_H, 512), F32),
                        pltpu.VMEM((tt + A_H, 512), F32), pltpu.VMEM((A_H, 512), F32)],
        compiler_params=_cp(48),
    )(dproj, proj, proj, proj, proj, proj, conv, dy, dw, lg, lb)
    return outs


def _b_post(blocks):
    out = []
    for idx, c in enumerate(blocks):
        s = jax.nn.silu(c)
        if idx < 2 * NH:
            s = s * lax.rsqrt(jnp.sum(s * s, axis=-1, keepdims=True) + EPS)
            if idx < NH:
                s = s * (LANE ** -0.5)
        out.append(s)
    return out


def _bprep_fwd(proj, wconv, bsz, t):
    n = bsz * t
    tt = _tile(t, 256)
    nt = t // tt

    def body(x_ref, xh_ref, w_ref, o_ref, c_ref, ext):
        i = pl.program_id(1)
        ext[0:B_H, :] = jnp.where(i > 0, xh_ref[...], 0.0)
        ext[B_H:, :] = x_ref[...]
        _conv_rows(c_ref, lambda off, rows: ext[pl.ds(off, rows), :], w_ref, _fwd_offsets(B_K, B_H), tt, tt)
        outs = _b_post([c_ref[:, LANE * j:LANE * (j + 1)] for j in range(3 * NH)])
        for j, o in enumerate(outs):
            o_ref[:, LANE * j:LANE * (j + 1)] = o

    return pl.pallas_call(
        body, name="bprep_fwd", grid=(bsz, nt),
        in_specs=[pl.BlockSpec((tt, 3072), lambda b, i: (b * nt + i, 0)),
                  pl.BlockSpec((B_H, 3072), lambda b, i: (jnp.maximum((b * t + i * tt) // B_H - 1, 0), 0)),
                  _full((8, 3072))],
        out_specs=[pl.BlockSpec((tt, 3072), lambda b, i: (b * nt + i, 0))] * 2,
        out_shape=[SDS((n, 3072), F32)] * 2,
        scratch_shapes=[pltpu.VMEM((tt + B_H, 3072), F32)], compiler_params=_cp(56),
    )(proj, proj, wconv)


def _bprep_bwd(dproj, proj, conv, dqkvn, wconv, bsz, t):
    n = bsz * t
    tt = _tile(t, 256)
    nt = t // tt

    def body(dp_any, x_ref, xh_ref, c_ref, dq_ref, w_ref, dp_ref, gw_ref, ext, dcp, dae, carry):
        b, i = pl.program_id(0), pl.program_id(1)
        ti = nt - 1 - i

        @pl.when((b == 0) & (i == 0))
        def _():
            gw_ref[...] = jnp.zeros_like(gw_ref)

        @pl.when(i == 0)
        def _():
            carry[...] = jnp.zeros_like(carry)

        ext[0:B_H, :] = jnp.where(ti > 0, xh_ref[...], 0.0)
        ext[B_H:, :] = x_ref[...]
        _, vjp = jax.vjp(_b_post, [c_ref[:, LANE * j:LANE * (j + 1)] for j in range(3 * NH)])
        (dcs,) = vjp([dq_ref[:, LANE * j:LANE * (j + 1)] for j in range(3 * NH)])
        dcp[0:B_H, :] = jnp.zeros((B_H, 3072), F32)
        for j, dcj in enumerate(dcs):
            dcp[B_H:B_H + tt, LANE * j:LANE * (j + 1)] = dcj
        dcp[B_H + tt:, :] = jnp.zeros((B_H, 3072), F32)
        _conv_bwd_w(dcp[B_H:B_H + tt, :], ext, gw_ref, B_K, B_H, tt)
        _conv_rows(dae, lambda off, rows: dcp[pl.ds(off, rows), :], w_ref, _bwd_offsets(B_K), tt + B_H, tt + B_H)
        dae[tt:tt + B_H, :] += carry[...]
        carry[...] = dae[0:B_H, :]
        dp_ref[...] = dae[B_H:, :].astype(BF16)

    def row(b, i):
        return b * nt + (nt - 1 - i)

    return pl.pallas_call(
        body, name="bprep_bwd", grid=(bsz, nt),
        in_specs=[pl.BlockSpec(memory_space=pl.ANY),
                  pl.BlockSpec((tt, 3072), lambda b, i: (row(b, i), 0)),
                  pl.BlockSpec((B_H, 3072), lambda b, i: (jnp.maximum((b * t + (nt - 1 - i) * tt) // B_H - 1, 0), 0)),
                  pl.BlockSpec((tt, 3072), lambda b, i: (row(b, i), 0)),
                  pl.BlockSpec((tt, 3072), lambda b, i: (row(b, i), 0)),
                  _full((8, 3072))],
        out_specs=[pl.BlockSpec((tt, 3072), lambda b, i: (row(b, i), 0)), _full((8, 3072))],
        out_shape=[SDS((n, NP), BF16), SDS((8, 3072), F32)],
        input_output_aliases={0: 0},
        scratch_shapes=[pltpu.VMEM((tt + B_H, 3072), F32), pltpu.VMEM((tt + 2 * B_H, 3072), F32),
                        pltpu.VMEM((tt + B_H, 3072), F32), pltpu.VMEM((B_H, 3072), F32)],
        compiler_params=_cp(56),
    )(dproj, proj, proj, conv, dqkvn, wconv)


def _split2(a):
    hi = a.astype(BF16)
    return hi, (a - hi.astype(F32)).astype(BF16)


def _split3(a):
    p1 = a.astype(BF16)
    r1 = a - p1.astype(F32)
    p2 = r1.astype(BF16)
    return p1, p2, (r1 - p2.astype(F32)).astype(BF16)


def _dot3_raw(a, b, dims):
    a1, a2 = _split2(a)
    b1, b2 = _split2(b)
    return _mm(a1, b1, dims) + (_mm(a1, b2, dims) + _mm(a2, b1, dims))


def _dot6(a, b, dims):
    a1, a2, a3 = _split3(a)
    b1, b2, b3 = _split3(b)
    return (_mm(a1, b1, dims) + (_mm(a1, b2, dims) + _mm(a2, b1, dims))
            + (_mm(a1, b3, dims) + _mm(a2, b2, dims) + _mm(a3, b1, dims)))


def _unit_lower_inverse_raw(lmats):
    r = lax.broadcasted_iota(jnp.int32, (CH, CH), 0)
    c = lax.broadcasted_iota(jnp.int32, (CH, CH), 1)
    eye = (r == c).astype(F32)
    blk = jnp.right_shift(r, 4) == jnp.right_shift(c, 4)
    dm = [jnp.where(blk, x, 0.0) for x in lmats]
    om = [a - b for a, b in zip(lmats, dm)]
    d2 = [_dot3_raw(x, x, NN) for x in dm]
    d4 = [_dot3_raw(x, x, NN) for x in d2]
    d8 = [_dot3_raw(x, x, NN) for x in d4]
    p = [_dot3_raw(eye - a, eye + b, NN) for a, b in zip(dm, d2)]
    p = [_dot3_raw(a, eye + b, NN) for a, b in zip(p, d4)]
    p = [_dot3_raw(a, eye + b, NN) for a, b in zip(p, d8)]
    m = [_dot3_raw(a, b, NN) for a, b in zip(p, om)]
    m2 = [_dot3_raw(x, x, NN) for x in m]
    t = [_dot3_raw(eye - a, eye + b, NN) for a, b in zip(m, m2)]
    return [_dot3_raw(a, b, NN) for a, b in zip(t, p)]


@jax.custom_vjp
def _unit_lower_inverse(lmats):
    return _unit_lower_inverse_raw(lmats)


def _unit_lower_inverse_fwd(lmats):
    tinv = _unit_lower_inverse_raw(lmats)
    return tinv, tinv


def _unit_lower_inverse_bwd(tinv, gs):
    x = [_dot6(t, g, TN) for t, g in zip(tinv, gs)]
    return ([-_dot6(a, t, NT) for a, t in zip(x, tinv)],)


_unit_lower_inverse.defvjp(_unit_lower_inverse_fwd, _unit_lower_inverse_bwd)


@jax.custom_vjp
def _saved_unit_lower_inverse(lmats, tinv):
    del lmats
    return tinv


def _saved_unit_lower_inverse_fwd(lmats, tinv):
    del lmats
    return tinv, tinv


def _saved_unit_lower_inverse_bwd(tinv, gs):
    (dl,) = _unit_lower_inverse_bwd(tinv, gs)
    return dl, [jnp.zeros_like(t) for t in tinv]


_saved_unit_lower_inverse.defvjp(_saved_unit_lower_inverse_fwd, _saved_unit_lower_inverse_bwd)


def _tri(lower):
    r = lax.broadcasted_iota(jnp.int32, (CH, CH), 0)
    c = lax.broadcasted_iota(jnp.int32, (CH, CH), 1)
    return ((r >= c) if lower else (r <= c)).astype(BF16)


def _tri_dot(x, lower, dims, tri_first):
    p1, p2, p3 = _split3(x)
    tri = _tri(lower)
    if tri_first:
        return _mm(tri, p1, dims) + (_mm(tri, p2, dims) + _mm(tri, p3, dims))
    return _mm(p1, tri, dims) + (_mm(p2, tri, dims) + _mm(p3, tri, dims))


@jax.custom_vjp
def _cumsum_rows(x):
    return _tri_dot(x, True, NN, True)


def _cumsum_rows_fwd(x):
    return _tri_dot(x, True, NN, True), None


def _cumsum_rows_bwd(_, g):
    return (_tri_dot(g, True, TN, True),)


_cumsum_rows.defvjp(_cumsum_rows_fwd, _cumsum_rows_bwd)


@jax.custom_vjp
def _cumsum_rows_t(x):
    return _tri_dot(x, False, TN, False)


def _cumsum_rows_t_fwd(x):
    return _tri_dot(x, False, TN, False), None


def _cumsum_rows_t_bwd(_, g):
    return (_tri_dot(g, False, NT, True),)


_cumsum_rows_t.defvjp(_cumsum_rows_t_fwd, _cumsum_rows_t_bwd)


def _delta_chunk(ss, qs, ks, vs, bas, zs, alog, dtb, og, tsaved=None):
    heads = range(len(ss))
    lane = lax.broadcasted_iota(jnp.int32, (1, LANE), 1)
    r = lax.broadcasted_iota(jnp.int32, (CH, CH), 0)
    c = lax.broadcasted_iota(jnp.int32, (CH, CH), 1)
    ri = lax.broadcasted_iota(jnp.int32, (CH, 1), 0)
    incl, strict = r >= c, r > c

    def pick(x, h):
        return jnp.sum(jnp.where(lane == h, x, 0.0), axis=-1, keepdims=True)

    beta = [jax.nn.sigmoid(pick(bas[e // NH], e % NH)) for e in heads]
    g = [-jnp.exp(pick(alog, e % NH + NH)) * _softplus(pick(bas[e // NH], e % NH + NH) + pick(dtb, e % NH + NH))
         for e in heads]
    gb = [jnp.broadcast_to(x, (CH, CH)) for x in g]
    gca = [_cumsum_rows(x) for x in gb]
    gcr = [_cumsum_rows_t(x) for x in gb]
    gc = [jnp.sum(jnp.where(c == 0, x, 0.0), axis=-1, keepdims=True) for x in gca]
    gl = [jnp.sum(jnp.where(ri == CH - 1, x, 0.0), axis=0, keepdims=True) for x in gc]
    diff = [a - b for a, b in zip(gca, gcr)]
    gam_s = [jnp.where(strict, jnp.exp(jnp.where(strict, x, 0.0)), 0.0) for x in diff]
    gam_i = [jnp.where(incl, jnp.exp(jnp.where(incl, x, 0.0)), 0.0) for x in diff]

    kk = [_mm(k, k, NT) for k in ks]
    lmats = [beta[h] * kk[h] * gam_s[h] for h in heads]
    tinv = _unit_lower_inverse(lmats) if tsaved is None else _saved_unit_lower_inverse(lmats, tsaved)

    eg = [jnp.exp(x) for x in gc]
    u = [_mm(tinv[h], vs[h] * beta[h], NN) for h in heads]
    w = [_mm(tinv[h], ks[h] * (beta[h] * eg[h]), NN) for h in heads]
    qk = [_mm(qs[h], ks[h], NT) * gam_i[h] for h in heads]
    vn = [u[h] - _mm(w[h], ss[h], NN) for h in heads]
    o = [_mm(qs[h] * eg[h], ss[h], NN) + _mm(qk[h], vn[h], NN) for h in heads]
    sn = [jnp.exp(gl[h]) * ss[h] + _mm(ks[h] * jnp.exp(gl[h] - gc[h]), vn[h], TN) for h in heads]
    y = [_rms(o[h], og) * jax.nn.silu(zs[h]) for h in heads]
    return (sn, y), tinv


def _chain_blocks(ref):
    return [ref[b, :, LANE * h:LANE * (h + 1)] for b in range(ref.shape[0]) for h in range(NH)]


def _ride_along(ex, first, last):
    if ex is None:
        return

    @pl.when(first)
    def _():
        ex.start()

    @pl.when(last)
    def _():
        ex.wait()


def _delta_fwd(qkvn, proj, alog, dtb, og, bsz, t, gather=()):
    n = bsz * t
    nc = t // CH
    ng = len(gather)

    def body(q_ref, k_ref, v_ref, ba_ref, z_ref, al_ref, dt_ref, og_ref, *rest):
        g_in, (y_ref, sh_ref, ti_ref) = rest[:ng], rest[ng:ng + 3]
        g_out, (s_scr, *sems) = rest[ng + 3:2 * ng + 3], rest[2 * ng + 3:]
        ci = pl.program_id(0)
        _ride_along(_Exchange(False, g_in, g_out, *sems) if ng else None, ci == 0, ci == nc - 1)

        @pl.when(ci == 0)
        def _():
            s_scr[...] = jnp.zeros_like(s_scr)

        chains = range(bsz * NH)
        ss = [s_scr[e] for e in chains]
        for e in chains:
            sh_ref[e // NH, e % NH] = ss[e]
        (sn, y), tinv = _delta_chunk(ss, _chain_blocks(q_ref), _chain_blocks(k_ref), _chain_blocks(v_ref),
                                     [ba_ref[b] for b in range(bsz)], _chain_blocks(z_ref),
                                     al_ref[...], dt_ref[...], og_ref[...])
        for e in chains:
            s_scr[e] = sn[e]
            ti_ref[e // NH, e % NH] = tinv[e]
            y_ref[e // NH, :, LANE * (e % NH):LANE * (e % NH + 1)] = y[e]

    def blk(width, col):
        return pl.BlockSpec((bsz, CH, width), lambda ci: (0, ci, col))

    def per_chunk(rows):
        return pl.BlockSpec((bsz, None, NH, rows, rows), lambda ci: (0, ci, 0, 0, 0))

    hbm = pl.BlockSpec(memory_space=pltpu.HBM)
    qkv3, proj3 = qkvn.reshape(bsz, t, 3 * D), proj.reshape(bsz, t, NP)
    y, *rest = pl.pallas_call(
        body, name="delta_fwd_gather" if ng else "delta_fwd", grid=(nc,),
        in_specs=[blk(D, 0), blk(D, 1), blk(D, 2), blk(LANE, C_BA // LANE), blk(D, C_BZ // D),
                  _full((1, LANE)), _full((1, LANE)), _full((1, LANE))] + [hbm] * ng,
        out_specs=[blk(D, 0), per_chunk(LANE), per_chunk(CH)] + [hbm] * ng,
        out_shape=[SDS((bsz, t, D), F32), SDS((bsz, nc, NH, LANE, LANE), F32), SDS((bsz, nc, NH, CH, CH), F32)]
        + [SDS((NDEV,) + a.shape, a.dtype) for a in gather],
        scratch_shapes=[pltpu.VMEM((bsz * NH, LANE, LANE), F32)] + (_exchange_sems(ng) if ng else []),
        compiler_params=_cp(48),
    )(qkv3, qkv3, qkv3, proj3, proj3, alog, dtb, og, *gather)
    return (y.reshape(n, D), *rest)


def _delta_bwd(dproj, qkvn, proj, shist, tsaved, dyb, alog, dtb, og, bsz, t, scatter=()):
    n = bsz * t
    nc = t // CH
    ns = len(scatter)
    eg = 1

    def body(dp_any, q_ref, k_ref, v_ref, ba_ref, z_ref, sh_ref, ti_ref, dy_ref, al_ref, dt_ref, og_ref, *rest):
        s_in, (dp_ref, dqkv_ref, gal_ref, gdt_ref, gog_ref) = rest[:ns], rest[ns:ns + 5]
        s_out, (ds_scr, *sems) = rest[ns + 5:2 * ns + 5], rest[2 * ns + 5:]
        bi, ci = pl.program_id(0), pl.program_id(1)
        _ride_along(_Exchange(True, s_in, s_out, *sems) if ns else None,
                    (bi == 0) & (ci == 0), (bi == bsz // eg - 1) & (ci == nc - 1))

        @pl.when((bi == 0) & (ci == 0))
        def _():
            gal_ref[...] = jnp.zeros_like(gal_ref)
            gdt_ref[...] = jnp.zeros_like(gdt_ref)
            gog_ref[...] = jnp.zeros_like(gog_ref)

        @pl.when(ci == 0)
        def _():
            ds_scr[...] = jnp.zeros_like(ds_scr)

        chains = range(eg * NH)
        _, vjp, _ = jax.vjp(_delta_chunk, [sh_ref[e // NH, e % NH] for e in chains], _chain_blocks(q_ref),
                            _chain_blocks(k_ref), _chain_blocks(v_ref), [ba_ref[b] for b in range(eg)],
                            _chain_blocks(z_ref), al_ref[...], dt_ref[...], og_ref[...],
                            [ti_ref[e // NH, e % NH] for e in chains], has_aux=True)
        ds, dq, dk, dv, dba, dz, dal, ddt, dog, _ = vjp(([ds_scr[e] for e in chains], _chain_blocks(dy_ref)))
        gal_ref[...] += dal
        gdt_ref[...] += ddt
        gog_ref[...] += dog
        for b in range(eg):
            dp_ref[b, :, D:D + LANE] = dba[b].astype(BF16)
        for e in chains:
            b, lo = e // NH, LANE * (e % NH)
            ds_scr[e] = ds[e]
            dp_ref[b, :, lo:lo + LANE] = dz[e].astype(BF16)
            dqkv_ref[b, :, lo:lo + LANE] = dq[e]
            dqkv_ref[b, :, D + lo:D + lo + LANE] = dk[e]
            dqkv_ref[b, :, 2 * D + lo:2 * D + lo + LANE] = dv[e]

    def blk(width, col):
        return pl.BlockSpec((eg, CH, width), lambda bi, ci: (bi, nc - 1 - ci, col))

    def per_chunk(rows):
        return pl.BlockSpec((eg, None, NH, rows, rows), lambda bi, ci: (bi, nc - 1 - ci, 0, 0, 0))

    hbm = pl.BlockSpec(memory_space=pltpu.HBM)
    qkv3, proj3 = qkvn.reshape(bsz, t, 3 * D), proj.reshape(bsz, t, NP)
    dproj, dqkvn, *rest = pl.pallas_call(
        body, name="delta_bwd_exchange" if ns else "delta_bwd", grid=(bsz // eg, nc),
        in_specs=[pl.BlockSpec(memory_space=pl.ANY), blk(D, 0), blk(D, 1), blk(D, 2), blk(LANE, C_BA // LANE),
                  blk(D, C_BZ // D), per_chunk(LANE), per_chunk(CH),
                  blk(D, 0), _full((1, LANE)), _full((1, LANE)), _full((1, LANE))] + [hbm] * ns,
        out_specs=[blk(D + LANE, C_BZ // (D + LANE)), blk(3 * D, 0), _full((1, LANE)), _full((1, LANE)),
                   _full((1, LANE))] + [hbm] * ns,
        out_shape=[SDS((bsz, t, NP), BF16), SDS((bsz, t, 3 * D), F32), SDS((1, LANE), F32), SDS((1, LANE), F32),
                   SDS((1, LANE), F32)] + [SDS(a.shape, a.dtype) for a in scatter],
        input_output_aliases={0: 0},
        scratch_shapes=[pltpu.VMEM((eg * NH, LANE, LANE), F32)] + (_exchange_sems(ns) if ns else []),
        compiler_params=_cp(56),
    )(dproj.reshape(bsz, t, NP), qkv3, qkv3, qkv3, proj3, proj3, shist, tsaved, dyb.reshape(bsz, t, D), alog, dtb, og,
      *scatter)
    return (dproj.reshape(n, NP), dqkvn.reshape(n, 3 * D), *rest)


def _c_chunk(us, vs, zs, lgs, lbs, ws, bsb):
    gv = [jax.nn.gelu(v) for v in vs]
    width = LANE * len(gv)
    mu = sum(jnp.sum(x, axis=-1, keepdims=True) for x in gv) / width
    var = sum(jnp.sum(jnp.square(x - mu), axis=-1, keepdims=True) for x in gv) / width
    rstd = lax.rsqrt(var + EPS)
    r = lax.broadcasted_iota(jnp.int32, (SG, SG), 0)
    c = lax.broadcasted_iota(jnp.int32, (SG, SG), 1)
    out = []
    for j in range(len(gv)):
        nrm = (gv[j] - mu) * rstd * lgs[j] + lbs[j]
        mixed = jnp.dot(jnp.where(r >= c, ws[j], 0.0), nrm, preferred_element_type=F32) + bsb[j]
        out.append(jax.nn.gelu(us[j]) * mixed * jax.nn.silu(zs[j]))
    return out


def _c_args(u_ref, v_ref, z_ref, lg_ref, lb_ref, ws_ref, bs_ref):
    sl = [slice(LANE * j, LANE * (j + 1)) for j in range(4)]
    return ([u_ref[:, s] for s in sl], [v_ref[:, s] for s in sl], [z_ref[:, s] for s in sl],
            [lg_ref[:, s] for s in sl], [lb_ref[:, s] for s in sl],
            [ws_ref[j] for j in range(4)], [bs_ref[j] for j in range(4)])


def _c_fwd(proj, lg, lb, ws, bsb, n):
    cb = C_C // 512

    def body(u_ref, v_ref, z_ref, lg_ref, lb_ref, ws_ref, bs_ref, y_ref):
        outs = _c_chunk(*_c_args(u_ref, v_ref, z_ref, lg_ref, lb_ref, ws_ref, bs_ref))
        for j, o in enumerate(outs):
            y_ref[:, LANE * j:LANE * (j + 1)] = o

    return pl.pallas_call(
        body, name="c_fwd", grid=(n // SG,),
        in_specs=[pl.BlockSpec((SG, 512), lambda i: (i, cb)), pl.BlockSpec((SG, 512), lambda i: (i, cb + 1)),
                  pl.BlockSpec((SG, 512), lambda i: (i, cb + 2)), _full((1, 512)), _full((1, 512)),
                  _full((4, SG, SG)), _full((4, SG, SG))],
        out_specs=pl.BlockSpec((SG, 512), lambda i: (i, 0)),
        out_shape=SDS((n, 512), F32), compiler_params=_cp(32),
    )(proj, proj, proj, lg, lb, ws, bsb)


def _c_bwd(dproj, proj, dy, lg, lb, ws, bsb, n):
    cb = C_C // 512

    def body(dp_any, u_ref, v_ref, z_ref, dy_ref, lg_ref, lb_ref, ws_ref, bs_ref,
             dp_ref, glg_ref, glb_ref, gws_ref, gbs_ref):
        @pl.when(pl.program_id(0) == 0)
        def _():
            glg_ref[...] = jnp.zeros_like(glg_ref)
            glb_ref[...] = jnp.zeros_like(glb_ref)
            gws_ref[...] = jnp.zeros_like(gws_ref)
            gbs_ref[...] = jnp.zeros_like(gbs_ref)

        _, vjp = jax.vjp(_c_chunk, *_c_args(u_ref, v_ref, z_ref, lg_ref, lb_ref, ws_ref, bs_ref))
        dus, dvs, dzs, dlgs, dlbs, dwss, dbss = vjp([dy_ref[:, LANE * j:LANE * (j + 1)] for j in range(4)])
        for j in range(4):
            sl = slice(LANE * j, LANE * (j + 1))
            dp_ref[:, LANE * j:LANE * (j + 1)] = dus[j].astype(BF16)
            dp_ref[:, 512 + LANE * j:512 + LANE * (j + 1)] = dvs[j].astype(BF16)
            dp_ref[:, 1024 + LANE * j:1024 + LANE * (j + 1)] = dzs[j].astype(BF16)
            glg_ref[:, sl] += dlgs[j]
            glb_ref[:, sl] += dlbs[j]
            gws_ref[j] += dwss[j]
            gbs_ref[j] += jnp.broadcast_to(jnp.sum(dbss[j], axis=-1, keepdims=True), (SG, SG))

    return pl.pallas_call(
        body, name="c_bwd", grid=(n // SG,),
        in_specs=[pl.BlockSpec(memory_space=pl.ANY),
                  pl.BlockSpec((SG, 512), lambda i: (i, cb)), pl.BlockSpec((SG, 512), lambda i: (i, cb + 1)),
                  pl.BlockSpec((SG, 512), lambda i: (i, cb + 2)), pl.BlockSpec((SG, 512), lambda i: (i, 0)),
                  _full((1, 512)), _full((1, 512)), _full((4, SG, SG)), _full((4, SG, SG))],
        out_specs=[pl.BlockSpec((SG, 1536), lambda i: (i, C_C // 1536)),
                   _full((1, 512)), _full((1, 512)), _full((4, SG, SG)), _full((4, SG, SG))],
        out_shape=[SDS((n, NP), BF16), SDS((1, 512), F32), SDS((1, 512), F32), SDS((4, SG, SG), F32), SDS((4, SG, SG), F32)],
        input_output_aliases={0: 0}, compiler_params=_cp(32),
    )(dproj, proj, proj, proj, dy, lg, lb, ws, bsb)


def _merge_fwd(x2d, ya, yb, yc, proj, ap, bp, cp, wo):
    n = x2d.shape[0]
    tm = _tile(n, 512)
    gb = C_G // D

    def body(x_ref, ya_ref, yb_ref, yc_ref, g0_ref, g1_ref, g2_ref, ap_ref, bp_ref, cp_ref, wo_ref, o_ref):
        merged = (jax.nn.sigmoid(g0_ref[...]) * _bdot(ya_ref[...], ap_ref[...])
                  + jax.nn.sigmoid(g1_ref[...]) * _bdot(yb_ref[...], bp_ref[...])
                  + jax.nn.sigmoid(g2_ref[...]) * _bdot(yc_ref[...], cp_ref[...]))
        o_ref[...] = x_ref[...] + _bdot(merged, wo_ref[...])

    def rows(w):
        return pl.BlockSpec((tm, w), lambda i: (i, 0))

    return pl.pallas_call(
        body, name="merge_fwd", grid=(n // tm,),
        in_specs=[rows(D), rows(512), rows(D), rows(512),
                  pl.BlockSpec((tm, D), lambda i: (i, gb)), pl.BlockSpec((tm, D), lambda i: (i, gb + 1)),
                  pl.BlockSpec((tm, D), lambda i: (i, gb + 2)),
                  _resident((512, D)), _resident((D, D)), _resident((512, D)), _resident((D, D))],
        out_specs=rows(D), out_shape=SDS((n, D), F32), compiler_params=_cp(48),
    )(x2d, ya, yb, yc, proj, proj, proj, ap, bp, cp, wo)


def _merge_bwd(dxo, ya, yb, yc, proj, ap, bp, cp, apt, bpt, cpt, wot):
    n = dxo.shape[0]
    tm = _tile(n, 256)
    gb = C_G // D

    def body(d_ref, ya_ref, yb_ref, yc_ref, g0_ref, g1_ref, g2_ref, ap_ref, bp_ref, cp_ref,
             apt_ref, bpt_ref, cpt_ref, wot_ref,
             dp_ref, dya_ref, dyb_ref, dyc_ref, dpa_ref, dpb_ref, dpc_ref, mg_ref):
        dm = _bdot(d_ref[...], wot_ref[...])
        merged = None
        for j, (g_ref, y_ref, w_ref, wt_ref, dy_ref, dpj_ref) in enumerate((
                (g0_ref, ya_ref, ap_ref, apt_ref, dya_ref, dpa_ref),
                (g1_ref, yb_ref, bp_ref, bpt_ref, dyb_ref, dpb_ref),
                (g2_ref, yc_ref, cp_ref, cpt_ref, dyc_ref, dpc_ref))):
            s = jax.nn.sigmoid(g_ref[...])
            pj = _bdot(y_ref[...], w_ref[...])
            merged = s * pj if merged is None else merged + s * pj
            dp_ref[:, D * j:D * (j + 1)] = (dm * pj * s * (1.0 - s)).astype(BF16)
            dpj = (dm * s).astype(BF16)
            dpj_ref[...] = dpj
            dy_ref[...] = jnp.dot(dpj, wt_ref[...], preferred_element_type=F32)
        mg_ref[...] = merged

    def rows(w):
        return pl.BlockSpec((tm, w), lambda i: (i, 0))

    return pl.pallas_call(
        body, name="merge_bwd", grid=(n // tm,),
        in_specs=[rows(D), rows(512), rows(D), rows(512),
                  pl.BlockSpec((tm, D), lambda i: (i, gb)), pl.BlockSpec((tm, D), lambda i: (i, gb + 1)),
                  pl.BlockSpec((tm, D), lambda i: (i, gb + 2)),
                  _resident((512, D)), _resident((D, D)), _resident((512, D)),
                  _resident((D, 512)), _resident((D, D)), _resident((D, 512)), _resident((D, D))],
        out_specs=[pl.BlockSpec((tm, 3 * D), lambda i: (i, C_G // (3 * D))), rows(512), rows(D), rows(512),
                   rows(D), rows(D), rows(D), rows(D)],
        out_shape=[SDS((n, NP), BF16), SDS((n, 512), F32), SDS((n, D), F32), SDS((n, 512), F32),
                   SDS((n, D), BF16), SDS((n, D), BF16), SDS((n, D), BF16), SDS((n, D), F32)],
        compiler_params=_cp(56),
    )(dxo, ya, yb, yc, proj, proj, proj, ap, bp, cp, apt, bpt, cpt, wot)


def _sum_parts(p_ref):
    g = p_ref[0].astype(F32)
    for s in range(1, NDEV):
        g = g + p_ref[s].astype(F32)
    return g


def _adamw(g, w, m, v):
    nm = ADAM_B1 * m + (1.0 - ADAM_B1) * g
    nv = ADAM_B2 * v + (1.0 - ADAM_B2) * jnp.square(g)
    nm_hat = nm / (1.0 - ADAM_B1 ** ADAM_STEP)
    nv_hat = nv / (1.0 - ADAM_B2 ** ADAM_STEP)
    return -ADAM_LR * (nm_hat / (jnp.sqrt(nv_hat) + ADAM_EPS) + ADAM_WD * w), nm, nv


def _reduce_adamw(parts, w, m, v, name):
    r, c = w.shape
    tr = _tile(r, 128)

    def body(p_ref, w_ref, m_ref, v_ref, g_ref, d_ref, nm_ref, nv_ref):
        g = _sum_parts(p_ref)
        g_ref[...] = g
        d_ref[...], nm_ref[...], nv_ref[...] = _adamw(g, w_ref[...], m_ref[...], v_ref[...])

    blk = pl.BlockSpec((tr, c), lambda i: (i, 0))
    return pl.pallas_call(
        body, name=name, grid=(r // tr,),
        in_specs=[pl.BlockSpec((NDEV, tr, c), lambda i: (0, i, 0)), blk, blk, blk],
        out_specs=[blk, blk, blk, blk], out_shape=[SDS((r, c), F32)] * 4, compiler_params=_cp(48),
    )(parts, w, m, v)


def _reduce_adamw_leaves(parts, ws, ms, vs, name):
    nleaf = len(ws)
    counts = [len(p) if isinstance(p, (list, tuple)) else 0 for p in parts]
    flat = [a for p in parts for a in (p if isinstance(p, (list, tuple)) else [p])]

    def body(*refs):
        p_refs, rest = refs[:len(flat)], refs[len(flat):]
        w_refs, m_refs, v_refs = rest[:nleaf], rest[nleaf:2 * nleaf], rest[2 * nleaf:3 * nleaf]
        outs = rest[3 * nleaf:]
        at = 0
        for i in range(nleaf):
            g_ref, d_ref, nm_ref, nv_ref = outs[i], outs[nleaf + i], outs[2 * nleaf + i], outs[3 * nleaf + i]
            for idx in (range(counts[i]) if counts[i] else [Ellipsis]):
                g = _sum_parts(p_refs[at])
                at += 1
                g_ref[idx] = g
                d_ref[idx], nm_ref[idx], nv_ref[idx] = _adamw(g, w_refs[i][idx], m_refs[i][idx], v_refs[i][idx])

    vm = pl.BlockSpec(memory_space=pltpu.VMEM)
    outs = pl.pallas_call(
        body, name=name, in_specs=[vm] * (len(flat) + 3 * nleaf), out_specs=[vm] * (4 * nleaf),
        out_shape=[SDS(w.shape, F32) for w in ws] * 4, compiler_params=_cp(56),
    )(*flat, *ws, *ms, *vs)
    return [outs[j * nleaf:(j + 1) * nleaf] for j in range(4)]


def _unshard(name, g):
    if name in ROW_SHARDED:
        return g.reshape(g.shape[0] * g.shape[1], g.shape[2])
    g = jnp.moveaxis(g, 0, 1)
    return g.reshape(g.shape[0], g.shape[1] * g.shape[2])


def _reshard(name, full):
    r, c = full.shape
    if name in ROW_SHARDED:
        return full.reshape(NDEV, r // NDEV, c)
    return jnp.moveaxis(full.reshape(r, NDEV, c // NDEV), 1, 0)


def _w_in_to_padded(slabs):
    pieces = []
    for lo, hi, _ in sorted(SEGMENTS, key=lambda s: s[2]):
        for d in range(NDEV):
            a, b = max(lo, d * W_SHARD), min(hi, (d + 1) * W_SHARD)
            if a < b:
                pieces.append(slabs[d, :, a - d * W_SHARD:b - d * W_SHARD])
    pieces.append(jnp.zeros(slabs.shape[1:2] + (NP - C_BA - 16,), slabs.dtype))
    return jnp.concatenate(pieces, axis=-1)


def _w_in_from_padded(g):
    slabs = []
    for d in range(NDEV):
        pieces = []
        for lo, hi, pstart in SEGMENTS:
            a, b = max(lo, d * W_SHARD), min(hi, (d + 1) * W_SHARD)
            if a < b:
                pieces.append(g[:, pstart + a - lo:pstart + b - lo])
        pieces.append(jnp.zeros(g.shape[:1] + (W_SHARD_PAD - W_SHARD,), g.dtype))
        slabs.append(jnp.concatenate(pieces, axis=-1))
    return jnp.stack(slabs)


def _pad_w_in(w):
    return jnp.pad(w, ((0, 0), (0, W_SHARD_PAD - W_SHARD)))


def _lane_row(vec8, offset):
    return jnp.pad(vec8, (offset, LANE - NH - offset))[None]


def kernel(x, norm_g, w_in, a_dw, a_dw_b, a_ln_g, a_ln_b, a_proj, b_conv, b_a_log, b_dt_bias, b_onorm_g, b_proj, c_ln_g, c_ln_b, c_ws, c_bs, c_proj, w_out, final_g, loss_target, m_norm_g, m_w_in, m_a_dw, m_a_dw_b, m_a_ln_g, m_a_ln_b, m_a_proj, m_b_conv, m_b_a_log, m_b_dt_bias, m_b_onorm_g, m_b_proj, m_c_ln_g, m_c_ln_b, m_c_ws, m_c_bs, m_c_proj, m_w_out, m_final_g, v_norm_g, v_w_in, v_a_dw, v_a_dw_b, v_a_ln_g, v_a_ln_b, v_a_proj, v_b_conv, v_b_a_log, v_b_dt_bias, v_b_onorm_g, v_b_proj, v_c_ln_g, v_c_ln_b, v_c_ws, v_c_bs, v_c_proj, v_w_out, v_final_g):
    wts = dict(norm_g=norm_g, w_in=w_in, a_dw=a_dw, a_dw_b=a_dw_b, a_ln_g=a_ln_g, a_ln_b=a_ln_b, a_proj=a_proj,
               b_conv=b_conv, b_a_log=b_a_log, b_dt_bias=b_dt_bias, b_onorm_g=b_onorm_g, b_proj=b_proj,
               c_ln_g=c_ln_g, c_ln_b=c_ln_b, c_ws=c_ws, c_bs=c_bs, c_proj=c_proj, w_out=w_out, final_g=final_g)
    mom = dict(norm_g=m_norm_g, w_in=m_w_in, a_dw=m_a_dw, a_dw_b=m_a_dw_b, a_ln_g=m_a_ln_g, a_ln_b=m_a_ln_b,
               a_proj=m_a_proj, b_conv=m_b_conv, b_a_log=m_b_a_log, b_dt_bias=m_b_dt_bias, b_onorm_g=m_b_onorm_g,
               b_proj=m_b_proj, c_ln_g=m_c_ln_g, c_ln_b=m_c_ln_b, c_ws=m_c_ws, c_bs=m_c_bs, c_proj=m_c_proj,
               w_out=m_w_out, final_g=m_final_g)
    vel = dict(norm_g=v_norm_g, w_in=v_w_in, a_dw=v_a_dw, a_dw_b=v_a_dw_b, a_ln_g=v_a_ln_g, a_ln_b=v_a_ln_b,
               a_proj=v_a_proj, b_conv=v_b_conv, b_a_log=v_b_a_log, b_dt_bias=v_b_dt_bias, b_onorm_g=v_b_onorm_g,
               b_proj=v_b_proj, c_ln_g=v_c_ln_g, c_ln_b=v_c_ln_b, c_ws=v_c_ws, c_bs=v_c_bs, c_proj=v_c_proj,
               w_out=v_w_out, final_g=v_final_g)

    bsz, t, _ = x.shape
    n = bsz * t
    depth = norm_g.shape[0]
    x2d = x.reshape(n, D)
    tgt = loss_target.reshape(n, D)

    def weight_blocks(l):
        return [_pad_w_in(w_in[l].astype(BF16))] + [wts[k][l].astype(BF16) for k in BIG_REST]

    def matmul_weights(w_in_all, *rest):
        got = {k: _unshard(k, g) for k, g in zip(BIG_REST, rest)}
        got['wp'] = _w_in_to_padded(w_in_all)
        return got

    conv_all = _all_gather([wts[k] for k in SMALL], "gather_conv_weights")
    conv_w = {k: jnp.stack([_unshard(k, g[:, l]) for l in range(depth)]) for k, g in zip(SMALL, conv_all)}
    a_dw32 = jnp.pad(conv_w['a_dw'], ((0, 0), (0, 32 - A_K), (0, 0)))
    b_conv8 = jnp.pad(conv_w['b_conv'], ((0, 0), (0, 8 - B_K), (0, 0)))
    bsb = jnp.broadcast_to(c_bs[..., None], c_bs.shape + (SG,))
    full = [matmul_weights(*_all_gather(weight_blocks(0), "gather_matmul_weights"))]

    saved = []
    xl = x2d
    for l in range(depth):
        alog, dtb = _lane_row(b_a_log[l], NH), _lane_row(b_dt_bias[l], NH)
        proj, h = _inproj(xl, norm_g[l][None], full[l]['wp'])
        ya, conv_a = _a_fwd(proj, a_dw32[l], a_dw_b[l][None], a_ln_g[l][None], a_ln_b[l][None], bsz, t)
        qkvn, conv_b = _bprep_fwd(proj, b_conv8[l], bsz, t)
        yb, shist, tsave, *nxt = _delta_fwd(qkvn, proj, alog, dtb, b_onorm_g[l][None], bsz, t,
                                            gather=weight_blocks(l + 1) if l + 1 < depth else ())
        if nxt:
            full.append(matmul_weights(*nxt))
        yc = _c_fwd(proj, c_ln_g[l][None], c_ln_b[l][None], c_ws[l], bsb[l], n)
        xn = _merge_fwd(xl, ya, yb, yc, proj, full[l]['a_proj'], full[l]['b_proj'], full[l]['c_proj'], full[l]['w_out'])
        saved.append((xl, proj, h, ya, yb, yc, qkvn, shist, tsave, alog, dtb, conv_a, conv_b))
        xl = xn

    dx, g_final, loss_blk = _loss_head(xl, final_g[None], tgt)
    loss = lax.psum(loss_blk[0, 0], ("x", "y", "c"))

    gfull = {k: [None] * depth for k in WEIGHTS if k != 'final_g'}
    recv = [None] * depth

    def grad_slabs(l):
        return [_w_in_from_padded(gfull['w_in'][l]).astype(BF16)] + [
            _reshard(k, gfull[k][l]).astype(BF16) for k in SHARDED_REST]

    for l in reversed(range(depth)):
        xl, proj, h, ya, yb, yc, qkvn, shist, tsave, alog, dtb, conv_a, conv_b = saved[l]
        ap, bp, cp, wo = full[l]['a_proj'], full[l]['b_proj'], full[l]['c_proj'], full[l]['w_out']
        dproj, dya, dyb, dyc, dpa, dpb, dpc, merged = _merge_bwd(dx, ya, yb, yc, proj, ap, bp, cp, ap.T, bp.T, cp.T, wo.T)
        gfull['a_proj'][l] = _mm_tn(ya, dpa, "grad_a_proj")
        gfull['b_proj'][l] = _mm_tn(yb, dpb, "grad_b_proj")
        gfull['c_proj'][l] = _mm_tn(yc, dpc, "grad_c_proj")
        gfull['w_out'][l] = _mm_tn(merged, dx, "grad_w_out")
        dproj, g_clg, g_clb, g_cws, g_cbs = _c_bwd(dproj, proj, dyc, c_ln_g[l][None], c_ln_b[l][None], c_ws[l], bsb[l], n)
        dproj, g_adw, g_adb, g_alg, g_alb = _a_bwd(dproj, proj, conv_a, dya, a_dw32[l], a_ln_g[l][None], a_ln_b[l][None],
                                                   bsz, t)
        dproj, dqkvn, g_alog, g_dt, g_og, *got = _delta_bwd(dproj, qkvn, proj, shist, tsave, dyb, alog, dtb, b_onorm_g[l][None],
                                                            bsz, t, scatter=grad_slabs(l + 1) if l + 1 < depth else ())
        if got:
            recv[l + 1] = got
        dproj, g_bconv = _bprep_bwd(dproj, proj, conv_b, dqkvn, b_conv8[l], bsz, t)
        gfull['w_in'][l] = _mm_tn(h, dproj, "grad_w_in")
        gfull['a_dw'][l] = g_adw[:A_K]
        gfull['b_conv'][l] = g_bconv[:B_K]
        dx, g_ng, *got = _inproj_bwd(dproj, full[l]['wp'], xl, norm_g[l][None], dx, scatter=grad_slabs(0) if l == 0 else ())
        if got:
            recv[0] = got
        gfull['norm_g'][l] = g_ng[0]
        gfull['a_dw_b'][l], gfull['a_ln_g'][l], gfull['a_ln_b'][l] = g_adb[0], g_alg[0], g_alb[0]
        gfull['b_a_log'][l], gfull['b_dt_bias'][l] = g_alog[0, NH:2 * NH], g_dt[0, NH:2 * NH]
        gfull['b_onorm_g'][l] = g_og[0]
        gfull['c_ln_g'][l], gfull['c_ln_b'][l] = g_clg[0], g_clb[0]
        gfull['c_ws'][l], gfull['c_bs'][l] = g_cws, g_cbs[:, :, 0]
    grad_x = dx.reshape(bsz, t, D)

    outs_w = [_reduce_adamw(recv[l][0], _pad_w_in(w_in[l]), _pad_w_in(m_w_in[l]), _pad_w_in(v_w_in[l]), "adamw_w_in")
              for l in range(depth)]
    outs_s = _reduce_adamw_leaves([[recv[l][1 + i] for l in range(depth)] for i in range(len(SHARDED_REST))],
                                  [wts[k] for k in SHARDED_REST], [mom[k] for k in SHARDED_REST],
                                  [vel[k] for k in SHARDED_REST], "adamw_sharded")

    def upto3d(a):
        return a[None] if a.ndim == 1 else a.reshape((-1,) + a.shape[-2:]) if a.ndim > 3 else a

    grepl = [upto3d(jnp.stack(gfull[k]) if k != 'final_g' else g_final[0]) for k in REPL]
    outs_r = _reduce_adamw_leaves(_all_gather(grepl, "gather_replicated_grads"), [upto3d(wts[k]) for k in REPL],
                                  [upto3d(mom[k]) for k in REPL], [upto3d(vel[k]) for k in REPL], "adamw_replicated")

    res = []
    for j in range(4):
        leaves = {k: outs_r[j][i].reshape(wts[k].shape) for i, k in enumerate(REPL)}
        leaves.update({k: outs_s[j][i] for i, k in enumerate(SHARDED_REST)})
        leaves['w_in'] = jnp.stack([outs_w[l][j][:, :W_SHARD] for l in range(depth)])
        res.append([leaves[k] for k in WEIGHTS])
    grads, deltas, new_m, new_v = res
    return (loss, grad_x, *grads, *deltas, *new_m, *new_v)
```

```python
import jax
import jax.numpy as jnp
from jax import lax
from jax.experimental import pallas as pl
from jax.experimental.pallas import tpu as pltpu

F32 = jnp.float32
BF16 = jnp.bfloat16
SDS = jax.ShapeDtypeStruct
MESH = pl.DeviceIdType.MESH

NDEV = 8
D = 1024
EPS = 1e-6
LANE = 128

C_Q, C_K, C_V = 0, 1024, 2048
C_G = 3072
C_A = 6144
C_C = 7680
C_BZ = 9216
C_BA = 10240
NP = 10368
N_IN = 10256
SEGMENTS = ((0, 1536, C_A), (1536, 4608, C_Q), (4608, 5632, C_BZ), (5632, 5648, C_BA), (5648, 7184, C_C), (7184, 10256, C_G))
W_SHARD = N_IN // NDEV
W_SHARD_PAD = 1408

A_K, A_H, A_RB = 31, 32, 32
B_K, B_H, B_RB, B_CB = 4, 8, 64, 512
CH = 64
SG = 128
NH = 8

ADAM_LR, ADAM_B1, ADAM_B2, ADAM_EPS, ADAM_WD, ADAM_STEP = 0.001, 0.9, 0.999, 1e-08, 0.01, 10

WEIGHTS = ['norm_g', 'w_in', 'a_dw', 'a_dw_b', 'a_ln_g', 'a_ln_b', 'a_proj', 'b_conv', 'b_a_log', 'b_dt_bias',
           'b_onorm_g', 'b_proj', 'c_ln_g', 'c_ln_b', 'c_ws', 'c_bs', 'c_proj', 'w_out', 'final_g']
SHARDED_REST = ['a_dw', 'a_proj', 'b_conv', 'b_proj', 'c_proj', 'w_out']
REPL = [n for n in WEIGHTS if n != 'w_in' and n not in SHARDED_REST]
BIG_REST = ['a_proj', 'b_proj', 'c_proj', 'w_out']
SMALL = ['a_dw', 'b_conv']
ROW_SHARDED = ('b_proj', 'w_out')

NN = ((1,), (0,))
NT = ((1,), (1,))
TN = ((0,), (0,))


def _tile(n, pref):
    return pref if (n >= pref and n % pref == 0) else n


def _cp(vmem_mb):
    return pltpu.CompilerParams(vmem_limit_bytes=vmem_mb * 2 ** 20)


def _full(shape):
    nd = len(shape)
    return pl.BlockSpec(shape, lambda *_: (0,) * nd)


def _resident(shape):
    nd = len(shape)
    return pl.BlockSpec(shape, lambda *_: (0,) * nd, pipeline_mode=pl.Buffered(1))


def _rms(x, g):
    return x * lax.rsqrt(jnp.mean(x * x, axis=-1, keepdims=True) + EPS) * g


def _softplus(x):
    return jnp.maximum(x, 0.0) + jnp.log1p(jnp.exp(-jnp.abs(x)))


def _bdot(a, b):
    return jnp.dot(a.astype(BF16), b.astype(BF16), preferred_element_type=F32)


def _mm(a, b, dims):
    return lax.dot_general(a, b, (dims, ((), ())), preferred_element_type=F32)


def _all_gather(xs, name):
    nops = len(xs)

    def body(*refs):
        x_refs, out_refs = refs[:nops], refs[nops:2 * nops]
        send_sems, recv_sems, local_sems = refs[2 * nops:]
        x, y, cc = lax.axis_index("x"), lax.axis_index("y"), lax.axis_index("c")
        me, sibling = (x, y, cc), (x, y, 1 - cc)
        chips = [(1 - x, y), (x, 1 - y), (1 - x, 1 - y)]

        def slot(t, px, py, pc):
            return out_refs[t].at[4 * px + 2 * py + pc]

        def copy(t, k, block, to, src=None):
            return pltpu.make_async_remote_copy(
                src_ref=slot(t, *block) if src is None else src, dst_ref=slot(t, *block),
                send_sem=send_sems.at[7 * t + k], recv_sem=recv_sems.at[7 * t + k], device_id=to, device_id_type=MESH)

        ops = range(nops)
        mine = [pltpu.make_async_copy(x_refs[t], slot(t, *me), local_sems.at[t]) for t in ops]
        for cp in mine:
            cp.start()
        first = [copy(t, 0, me, sibling, src=x_refs[t]) for t in ops]
        first += [copy(t, 1 + j, me, (*chip, cc), src=x_refs[t]) for j, chip in enumerate(chips) for t in ops]
        for cp in first:
            cp.start()
        passed = []
        for j, chip in enumerate(chips):
            for t in ops:
                copy(t, 1 + j, (*chip, cc), me).wait_recv()
                fwd = copy(t, 4 + j, (*chip, cc), sibling)
                fwd.start()
                passed.append(fwd)
        for t in ops:
            copy(t, 0, sibling, me).wait_recv()
        for j, chip in enumerate(chips):
            for t in ops:
                copy(t, 4 + j, (*chip, 1 - cc), me).wait_recv()
        for cp in first + passed:
            cp.wait_send()
        for cp in mine:
            cp.wait()

    hbm = pl.BlockSpec(memory_space=pltpu.HBM)
    return pl.pallas_call(
        body, name=name, out_shape=[SDS((NDEV,) + a.shape, a.dtype) for a in xs],
        in_specs=[hbm] * nops, out_specs=[hbm] * nops,
        scratch_shapes=[pltpu.SemaphoreType.DMA((7 * nops,)), pltpu.SemaphoreType.DMA((7 * nops,)),
                        pltpu.SemaphoreType.DMA((nops,))],
    )(*xs)


def _exchange_sems(nops):
    return [pltpu.SemaphoreType.DMA((7 * nops,)), pltpu.SemaphoreType.DMA((7 * nops,)), pltpu.SemaphoreType.DMA((nops,))]


class _Exchange:
    def __init__(self, scatter, in_refs, out_refs, send_sems, recv_sems, local_sems):
        x, y, cc = lax.axis_index("x"), lax.axis_index("y"), lax.axis_index("c")
        me = 4 * x + 2 * y + cc
        nops = len(in_refs)
        self.local = [pltpu.make_async_copy(in_refs[t].at[me] if scatter else in_refs[t], out_refs[t].at[me],
                                            local_sems.at[t]) for t in range(nops)]
        self.sends, self.recvs = [], []
        for k in range(1, NDEV):
            px = 1 - x if k & 4 else x
            py = 1 - y if k & 2 else y
            pc = 1 - cc if k & 1 else cc
            peer = 4 * px + 2 * py + pc
            for t in range(nops):
                sem = 7 * t + k - 1
                self.sends.append(pltpu.make_async_remote_copy(
                    src_ref=in_refs[t].at[peer] if scatter else in_refs[t], dst_ref=out_refs[t].at[me],
                    send_sem=send_sems.at[sem], recv_sem=recv_sems.at[sem], device_id=(px, py, pc), device_id_type=MESH))
                self.recvs.append(pltpu.make_async_remote_copy(
                    src_ref=in_refs[t].at[me] if scatter else in_refs[t], dst_ref=out_refs[t].at[peer],
                    send_sem=send_sems.at[sem], recv_sem=recv_sems.at[sem], device_id=(px, py, pc), device_id_type=MESH))

    def start(self):
        for cp in self.local + self.sends:
            cp.start()

    def wait(self):
        for cp in self.recvs:
            cp.wait_recv()
        for cp in self.sends:
            cp.wait_send()
        for cp in self.local:
            cp.wait()


def _inproj(x2d, g_row, wp):
    n = x2d.shape[0]
    tm, tn = _tile(n, 1024), 1152

    def body(x_ref, g_ref, w_ref, proj_ref, h_ref, hs):
        @pl.when(pl.program_id(1) == 0)
        def _():
            h = _rms(x_ref[...], g_ref[...]).astype(BF16)
            hs[...] = h
            h_ref[...] = h

        proj_ref[...] = jnp.dot(hs[...], w_ref[...], preferred_element_type=F32)

    return pl.pallas_call(
        body, name="inproj", grid=(n // tm, NP // tn),
        in_specs=[pl.BlockSpec((tm, D), lambda i, j: (i, 0)), _full((1, D)), pl.BlockSpec((D, tn), lambda i, j: (0, j))],
        out_specs=[pl.BlockSpec((tm, tn), lambda i, j: (i, j)), pl.BlockSpec((tm, D), lambda i, j: (i, 0))],
        out_shape=[SDS((n, NP), F32), SDS((n, D), BF16)],
        scratch_shapes=[pltpu.VMEM((tm, D), BF16)], compiler_params=_cp(48),
    )(x2d, g_row, wp)


def _mm_tn(a, b, name):
    nn, m = a.shape
    k = b.shape[1]
    tk = 1152 if k % 1152 == 0 else _tile(k, 1024)
    tn = _tile(nn, 1024)

    def body(a_ref, b_ref, o_ref):
        p = _mm(a_ref[...].astype(BF16), b_ref[...].astype(BF16), TN)

        @pl.when(pl.program_id(1) == 0)
        def _():
            o_ref[...] = p

        @pl.when(pl.program_id(1) > 0)
        def _():
            o_ref[...] += p

    return pl.pallas_call(
        body, name=name, grid=(k // tk, nn // tn),
        in_specs=[pl.BlockSpec((tn, m), lambda j, t: (t, 0)), pl.BlockSpec((tn, tk), lambda j, t: (t, j))],
        out_specs=pl.BlockSpec((m, tk), lambda j, t: (0, j)),
        out_shape=SDS((m, k), F32), compiler_params=_cp(48),
    )(a, b)


def _inproj_bwd(dproj, wp, x2d, g_row, dxo, scatter=()):
    n = x2d.shape[0]
    tm, tk = _tile(n, 1024), 1152
    nk = NP // tk
    ns = len(scatter)

    def body(dp_ref, w_ref, x_ref, g_ref, dxo_ref, *rest):
        s_in, (dx_ref, dg_ref), s_out, (acc, *sems) = rest[:ns], rest[ns:ns + 2], rest[ns + 2:2 * ns + 2], rest[2 * ns + 2:]
        i, k = pl.program_id(0), pl.program_id(1)
        _ride_along(_Exchange(True, s_in, s_out, *sems) if ns else None,
                    (i == 0) & (k == 0), (i == n // tm - 1) & (k == nk - 1))
        p = _mm(dp_ref[...], w_ref[...], NT)

        @pl.when(k == 0)
        def _():
            acc[...] = p

        @pl.when(k > 0)
        def _():
            acc[...] += p

        @pl.when(k == nk - 1)
        def _():
            _, vjp = jax.vjp(_rms, x_ref[...], g_ref[...])
            dx, dg = vjp(acc[...])
            dx_ref[...] = dxo_ref[...] + dx

            @pl.when(i == 0)
            def _():
                dg_ref[...] = dg

            @pl.when(i > 0)
            def _():
                dg_ref[...] += dg

    hbm = pl.BlockSpec(memory_space=pltpu.HBM)
    return pl.pallas_call(
        body, name="inproj_bwd_exchange" if ns else "inproj_bwd", grid=(n // tm, nk),
        in_specs=[pl.BlockSpec((tm, tk), lambda i, k: (i, k)), pl.BlockSpec((D, tk), lambda i, k: (0, k)),
                  pl.BlockSpec((tm, D), lambda i, k: (i, 0)), _full((1, D)),
                  pl.BlockSpec((tm, D), lambda i, k: (i, 0))] + [hbm] * ns,
        out_specs=[pl.BlockSpec((tm, D), lambda i, k: (i, 0)), _full((1, D))] + [hbm] * ns,
        out_shape=[SDS((n, D), F32), SDS((1, D), F32)] + [SDS(a.shape, a.dtype) for a in scatter],
        scratch_shapes=[pltpu.VMEM((tm, D), F32)] + (_exchange_sems(ns) if ns else []), compiler_params=_cp(56),
    )(dproj, wp, x2d, g_row, dxo, *scatter)


def _loss_head(x2d, g_row, tgt):
    n = x2d.shape[0]
    tm = _tile(n, 512)

    def body(x_ref, g_ref, t_ref, dx_ref, dg_ref, loss_ref):
        i = pl.program_id(0)
        y, vjp = jax.vjp(_rms, x_ref[...], g_ref[...])
        err = y - t_ref[...]
        part = 0.5 * jnp.sum(jnp.mean(err * err, axis=-1, keepdims=True), axis=0, keepdims=True)
        dx, dg = vjp(err * (1.0 / D))
        dx_ref[...] = dx
        lb = jnp.broadcast_to(part, (8, LANE))

        @pl.when(i == 0)
        def _():
            dg_ref[...] = dg
            loss_ref[...] = lb

        @pl.when(i > 0)
        def _():
            dg_ref[...] += dg
            loss_ref[...] += lb

    return pl.pallas_call(
        body, name="loss_head", grid=(n // tm,),
        in_specs=[pl.BlockSpec((tm, D), lambda i: (i, 0)), _full((1, D)), pl.BlockSpec((tm, D), lambda i: (i, 0))],
        out_specs=[pl.BlockSpec((tm, D), lambda i: (i, 0)), _full((1, D)), _full((8, LANE))],
        out_shape=[SDS((n, D), F32), SDS((1, D), F32), SDS((8, LANE), F32)], compiler_params=_cp(40),
    )(x2d, g_row, tgt)


def _conv_rows(dst_ref, read, w_ref, offs, nrows, rb, cb, bias=None):
    ncols = dst_ref.shape[1]
    for c0 in range(0, ncols, cb):
        for r0 in range(0, nrows, rb):
            rows = min(rb, nrows - r0)
            acc = None if bias is None else bias[:, c0:c0 + cb]
            for j, off in enumerate(offs):
                term = w_ref[j:j + 1, c0:c0 + cb] * read(off + r0, rows, c0, cb)
                acc = term if acc is None else acc + term
            dst_ref[r0:r0 + rows, c0:c0 + cb] = acc


def _fwd_offsets(kw, halo):
    return [halo - (kw - 1) + j for j in range(kw)]


def _bwd_offsets(kw):
    return [kw - 1 - j for j in range(kw)]


def _conv_bwd_w(dc, ext_ref, gw_ref, kw, halo, tt):
    for j in range(kw):
        gw_ref[j:j + 1, :] += jnp.sum(dc * ext_ref[pl.ds(halo - (kw - 1) + j, tt), :], axis=0, keepdims=True)


SUB = 8


def _fill_shifts(sh_ref, rows):
    for s in range(1, SUB):
        sh_ref[s, 0:rows, :] = sh_ref[0, pl.ds(s, rows), :]


def _tap(sh_ref, off, rows):
    return sh_ref[off % SUB, pl.ds(off - off % SUB, rows), :]


def _conv_bwd_w_sh(dc, ext_sh, gw_ref, kw, halo, tt):
    for j in range(kw):
        gw_ref[j:j + 1, :] += jnp.sum(dc * _tap(ext_sh, halo - (kw - 1) + j, tt), axis=0, keepdims=True)


def _a_post(c, z, g, b):
    mu = jnp.mean(c, axis=-1, keepdims=True)
    var = jnp.mean(jnp.square(c - mu), axis=-1, keepdims=True)
    a = (c - mu) * lax.rsqrt(var + EPS) * g + b
    return jax.nn.silu(a) * jax.nn.silu(z)


def _a_fwd(proj, dw, b_row, lg, lb, bsz, t):
    n = bsz * t
    tt = _tile(t, 256)
    nt = t // tt
    cb = C_A // 512

    def body(v_ref, g_ref, z_ref, vh_ref, gh_ref, w_ref, b_ref, lg_ref, lb_ref, y_ref, c_ref, ext):
        i = pl.program_id(1)
        ext[0, 0:A_H, :] = jnp.where(i > 0, vh_ref[...] * jax.nn.sigmoid(gh_ref[...]), 0.0)
        ext[0, A_H:, :] = v_ref[...] * jax.nn.sigmoid(g_ref[...])
        _fill_shifts(ext, tt + A_H - SUB)
        _conv_rows(c_ref, lambda off, rows, c0, cols: _tap(ext, off, rows), w_ref, _fwd_offsets(A_K, A_H), tt, A_RB, 512,
                   bias=b_ref[...])
        y_ref[...] = _a_post(c_ref[...], z_ref[...], lg_ref[...], lb_ref[...])

    def row(b, i):
        return b * nt + i

    def halo(b, i):
        return jnp.maximum((b * t + i * tt) // A_H - 1, 0)

    return pl.pallas_call(
        body, name="a_fwd", grid=(bsz, nt),
        in_specs=[pl.BlockSpec((tt, 512), lambda b, i: (row(b, i), cb)),
                  pl.BlockSpec((tt, 512), lambda b, i: (row(b, i), cb + 1)),
                  pl.BlockSpec((tt, 512), lambda b, i: (row(b, i), cb + 2)),
                  pl.BlockSpec((A_H, 512), lambda b, i: (halo(b, i), cb)),
                  pl.BlockSpec((A_H, 512), lambda b, i: (halo(b, i), cb + 1)),
                  _full((32, 512)), _full((1, 512)), _full((1, 512)), _full((1, 512))],
        out_specs=[pl.BlockSpec((tt, 512), lambda b, i: (row(b, i), 0))] * 2,
        out_shape=[SDS((n, 512), F32)] * 2,
        scratch_shapes=[pltpu.VMEM((SUB, tt + A_H, 512), F32)], compiler_params=_cp(40),
    )(proj, proj, proj, proj, proj, dw, b_row, lg, lb)


def _a_bwd(dproj, proj, conv, dy, dw, lg, lb, bsz, t):
    n = bsz * t
    tt = _tile(t, 256)
    nt = t // tt
    cb = C_A // 512

    def body(dp_any, v_ref, g_ref, z_ref, vh_ref, gh_ref, c_ref, dy_ref, w_ref, lg_ref, lb_ref,
             dp_ref, gw_ref, gb_ref, glg_ref, glb_ref, ext, dcp, dae, carry):
        b, i = pl.program_id(0), pl.program_id(1)
        ti = nt - 1 - i

        @pl.when((b == 0) & (i == 0))
        def _():
            gw_ref[...] = jnp.zeros_like(gw_ref)
            gb_ref[...] = jnp.zeros_like(gb_ref)
            glg_ref[...] = jnp.zeros_like(glg_ref)
            glb_ref[...] = jnp.zeros_like(glb_ref)

        @pl.when(i == 0)
        def _():
            carry[...] = jnp.zeros_like(carry)

        val, glu = v_ref[...], g_ref[...]
        sg = jax.nn.sigmoid(glu)
        ext[0, 0:A_H, :] = jnp.where(ti > 0, vh_ref[...] * jax.nn.sigmoid(gh_ref[...]), 0.0)
        ext[0, A_H:, :] = val * sg
        _fill_shifts(ext, tt + A_H - SUB)
        _, vjp = jax.vjp(_a_post, c_ref[...], z_ref[...], lg_ref[...], lb_ref[...])
        dc, dz, dlg, dlb = vjp(dy_ref[...])
        gb_ref[...] += jnp.sum(dc, axis=0, keepdims=True)
        glg_ref[...] += dlg
        glb_ref[...] += dlb
        dcp[0, 0:A_H, :] = jnp.zeros((A_H, 512), F32)
        dcp[0, A_H:A_H + tt, :] = dc
        dcp[0, A_H + tt:, :] = jnp.zeros((A_H, 512), F32)
        _fill_shifts(dcp, tt + 2 * A_H - SUB)
        _conv_bwd_w_sh(dc, ext, gw_ref, A_K, A_H, tt)
        _conv_rows(dae, lambda off, rows, c0, cols: _tap(dcp, off, rows), w_ref, _bwd_offsets(A_K), tt + A_H, A_RB, 512)
        dae[tt:tt + A_H, :] += carry[...]
        carry[...] = dae[0:A_H, :]
        da = dae[A_H:, :]
        dp_ref[:, 0:512] = (da * sg).astype(BF16)
        dp_ref[:, 512:1024] = (da * val * sg * (1.0 - sg)).astype(BF16)
        dp_ref[:, 1024:1536] = dz.astype(BF16)

    def row(b, i):
        return b * nt + (nt - 1 - i)

    def halo(b, i):
        return jnp.maximum((b * t + (nt - 1 - i) * tt) // A_H - 1, 0)

    outs = pl.pallas_call(
        body, name="a_bwd", grid=(bsz, nt),
        in_specs=[pl.BlockSpec(memory_space=pl.ANY),
                  pl.BlockSpec((tt, 512), lambda b, i: (row(b, i), cb)),
                  pl.BlockSpec((tt, 512), lambda b, i: (row(b, i), cb + 1)),
                  pl.BlockSpec((tt, 512), lambda b, i: (row(b, i), cb + 2)),
                  pl.BlockSpec((A_H, 512), lambda b, i: (halo(b, i), cb)),
                  pl.BlockSpec((A_H, 512), lambda b, i: (halo(b, i), cb + 1)),
                  pl.BlockSpec((tt, 512), lambda b, i: (row(b, i), 0)),
                  pl.BlockSpec((tt, 512), lambda b, i: (row(b, i), 0)),
                  _full((32, 512)), _full((1, 512)), _full((1, 512))],
        out_specs=[pl.BlockSpec((tt, 1536), lambda b, i: (row(b, i), C_A // 1536)),
                   _full((32, 512)), _full((1, 512)), _full((1, 512)), _full((1, 512))],
        out_shape=[SDS((n, NP), BF16), SDS((32, 512), F32), SDS((1, 512), F32), SDS((1, 512), F32), SDS((1, 512), F32)],
        input_output_aliases={0: 0},
        scratch_shapes=[pltpu.VMEM((SUB, tt + A_H, 512), F32), pltpu.VMEM((SUB, tt + 2 * A_H, 512), F32),
                        pltpu.VMEM((tt + A_H, 512), F32), pltpu.VMEM((A_H, 512), F32)],
        compiler_params=_cp(48),
    )(dproj, proj, proj, proj, proj, proj, conv, dy, dw, lg, lb)
    return outs


def _b_post(blocks):
    out = []
    for idx, c in enumerate(blocks):
        s = jax.nn.silu(c)
        if idx < 2 * NH:
            s = s * lax.rsqrt(jnp.sum(s * s, axis=-1, keepdims=True) + EPS)
            if idx < NH:
                s = s * (LANE ** -0.5)
        out.append(s)
    return out


def _bprep_fwd(proj, wconv, bsz, t, gather=()):
    n = bsz * t
    tt = _tile(t, 256)
    nt = t // tt
    ng = len(gather)

    def body(x_ref, xh_ref, w_ref, *rest):
        g_in, (o_ref, c_ref), g_out, (ext, *sems) = rest[:ng], rest[ng:ng + 2], rest[ng + 2:2 * ng + 2], rest[2 * ng + 2:]
        b, i = pl.program_id(0), pl.program_id(1)
        _ride_along(_Exchange(False, g_in, g_out, *sems) if ng else None,
                    (b == 0) & (i == 0), (b == bsz - 1) & (i == nt - 1))
        ext[0:B_H, :] = jnp.where(i > 0, xh_ref[...], 0.0)
        ext[B_H:, :] = x_ref[...]
        _conv_rows(c_ref, lambda off, rows, c0, cols: ext[pl.ds(off, rows), c0:c0 + cols], w_ref,
                   _fwd_offsets(B_K, B_H), tt, B_RB, B_CB)
        outs = _b_post([c_ref[:, LANE * j:LANE * (j + 1)] for j in range(3 * NH)])
        for j, o in enumerate(outs):
            o_ref[:, LANE * j:LANE * (j + 1)] = o

    hbm = pl.BlockSpec(memory_space=pltpu.HBM)
    return pl.pallas_call(
        body, name="bprep_fwd_gather" if ng else "bprep_fwd", grid=(bsz, nt),
        in_specs=[pl.BlockSpec((tt, 3072), lambda b, i: (b * nt + i, 0)),
                  pl.BlockSpec((B_H, 3072), lambda b, i: (jnp.maximum((b * t + i * tt) // B_H - 1, 0), 0)),
                  _full((8, 3072))] + [hbm] * ng,
        out_specs=[pl.BlockSpec((tt, 3072), lambda b, i: (b * nt + i, 0))] * 2 + [hbm] * ng,
        out_shape=[SDS((n, 3072), F32)] * 2 + [SDS((NDEV,) + a.shape, a.dtype) for a in gather],
        scratch_shapes=[pltpu.VMEM((tt + B_H, 3072), F32)] + (_exchange_sems(ng) if ng else []),
        compiler_params=_cp(56),
    )(proj, proj, wconv, *gather)


def _bprep_bwd(dproj, proj, conv, dqkvn, wconv, bsz, t):
    n = bsz * t
    tt = _tile(t, 256)
    nt = t // tt

    def body(dp_any, x_ref, xh_ref, c_ref, dq_ref, w_ref, dp_ref, gw_ref, ext, dcp, dae, carry):
        b, i = pl.program_id(0), pl.program_id(1)
        ti = nt - 1 - i

        @pl.when((b == 0) & (i == 0))
        def _():
            gw_ref[...] = jnp.zeros_like(gw_ref)

        @pl.when(i == 0)
        def _():
            carry[...] = jnp.zeros_like(carry)

        ext[0:B_H, :] = jnp.where(ti > 0, xh_ref[...], 0.0)
        ext[B_H:, :] = x_ref[...]
        _, vjp = jax.vjp(_b_post, [c_ref[:, LANE * j:LANE * (j + 1)] for j in range(3 * NH)])
        (dcs,) = vjp([dq_ref[:, LANE * j:LANE * (j + 1)] for j in range(3 * NH)])
        dcp[0:B_H, :] = jnp.zeros((B_H, 3072), F32)
        for j, dcj in enumerate(dcs):
            dcp[B_H:B_H + tt, LANE * j:LANE * (j + 1)] = dcj
        dcp[B_H + tt:, :] = jnp.zeros((B_H, 3072), F32)
        _conv_bwd_w(dcp[B_H:B_H + tt, :], ext, gw_ref, B_K, B_H, tt)
        _conv_rows(dae, lambda off, rows, c0, cols: dcp[pl.ds(off, rows), c0:c0 + cols], w_ref,
                   _bwd_offsets(B_K), tt + B_H, B_RB, B_CB)
        dae[tt:tt + B_H, :] += carry[...]
        carry[...] = dae[0:B_H, :]
        dp_ref[...] = dae[B_H:, :].astype(BF16)

    def row(b, i):
        return b * nt + (nt - 1 - i)

    return pl.pallas_call(
        body, name="bprep_bwd", grid=(bsz, nt),
        in_specs=[pl.BlockSpec(memory_space=pl.ANY),
                  pl.BlockSpec((tt, 3072), lambda b, i: (row(b, i), 0)),
                  pl.BlockSpec((B_H, 3072), lambda b, i: (jnp.maximum((b * t + (nt - 1 - i) * tt) // B_H - 1, 0), 0)),
                  pl.BlockSpec((tt, 3072), lambda b, i: (row(b, i), 0)),
                  pl.BlockSpec((tt, 3072), lambda b, i: (row(b, i), 0)),
                  _full((8, 3072))],
        out_specs=[pl.BlockSpec((tt, 3072), lambda b, i: (row(b, i), 0)), _full((8, 3072))],
        out_shape=[SDS((n, NP), BF16), SDS((8, 3072), F32)],
        input_output_aliases={0: 0},
        scratch_shapes=[pltpu.VMEM((tt + B_H, 3072), F32), pltpu.VMEM((tt + 2 * B_H, 3072), F32),
                        pltpu.VMEM((tt + B_H, 3072), F32), pltpu.VMEM((B_H, 3072), F32)],
        compiler_params=_cp(56),
    )(dproj, proj, proj, conv, dqkvn, wconv)


def _split2(a):
    hi = a.astype(BF16)
    return hi, (a - hi.astype(F32)).astype(BF16)


def _split3(a):
    p1 = a.astype(BF16)
    r1 = a - p1.astype(F32)
    p2 = r1.astype(BF16)
    return p1, p2, (r1 - p2.astype(F32)).astype(BF16)


def _dot3_raw(a, b, dims):
    a1, a2 = _split2(a)
    b1, b2 = _split2(b)
    return _mm(a1, b1, dims) + (_mm(a1, b2, dims) + _mm(a2, b1, dims))


def _dot6(a, b, dims):
    a1, a2, a3 = _split3(a)
    b1, b2, b3 = _split3(b)
    return (_mm(a1, b1, dims) + (_mm(a1, b2, dims) + _mm(a2, b1, dims))
            + (_mm(a1, b3, dims) + _mm(a2, b2, dims) + _mm(a3, b1, dims)))


def _unit_lower_inverse_raw(lmats):
    r = lax.broadcasted_iota(jnp.int32, (CH, CH), 0)
    c = lax.broadcasted_iota(jnp.int32, (CH, CH), 1)
    eye = (r == c).astype(F32)
    blk = jnp.right_shift(r, 4) == jnp.right_shift(c, 4)
    dm = [jnp.where(blk, x, 0.0) for x in lmats]
    om = [a - b for a, b in zip(lmats, dm)]
    d2 = [_dot3_raw(x, x, NN) for x in dm]
    d4 = [_dot3_raw(x, x, NN) for x in d2]
    d8 = [_dot3_raw(x, x, NN) for x in d4]
    p = [_dot3_raw(eye - a, eye + b, NN) for a, b in zip(dm, d2)]
    p = [_dot3_raw(a, eye + b, NN) for a, b in zip(p, d4)]
    p = [_dot3_raw(a, eye + b, NN) for a, b in zip(p, d8)]
    m = [_dot3_raw(a, b, NN) for a, b in zip(p, om)]
    m2 = [_dot3_raw(x, x, NN) for x in m]
    t = [_dot3_raw(eye - a, eye + b, NN) for a, b in zip(m, m2)]
    return [_dot3_raw(a, b, NN) for a, b in zip(t, p)]


@jax.custom_vjp
def _unit_lower_inverse(lmats):
    return _unit_lower_inverse_raw(lmats)


def _unit_lower_inverse_fwd(lmats):
    tinv = _unit_lower_inverse_raw(lmats)
    return tinv, tinv


def _unit_lower_inverse_bwd(tinv, gs):
    x = [_dot6(t, g, TN) for t, g in zip(tinv, gs)]
    return ([-_dot6(a, t, NT) for a, t in zip(x, tinv)],)


_unit_lower_inverse.defvjp(_unit_lower_inverse_fwd, _unit_lower_inverse_bwd)


@jax.custom_vjp
def _saved_unit_lower_inverse(lmats, tinv):
    del lmats
    return tinv


def _saved_unit_lower_inverse_fwd(lmats, tinv):
    del lmats
    return tinv, tinv


def _saved_unit_lower_inverse_bwd(tinv, gs):
    (dl,) = _unit_lower_inverse_bwd(tinv, gs)
    return dl, [jnp.zeros_like(t) for t in tinv]


_saved_unit_lower_inverse.defvjp(_saved_unit_lower_inverse_fwd, _saved_unit_lower_inverse_bwd)


def _tri(lower):
    r = lax.broadcasted_iota(jnp.int32, (CH, CH), 0)
    c = lax.broadcasted_iota(jnp.int32, (CH, CH), 1)
    return ((r >= c) if lower else (r <= c)).astype(BF16)


def _tri_dot(x, lower, dims, tri_first):
    p1, p2, p3 = _split3(x)
    tri = _tri(lower)
    if tri_first:
        return _mm(tri, p1, dims) + (_mm(tri, p2, dims) + _mm(tri, p3, dims))
    return _mm(p1, tri, dims) + (_mm(p2, tri, dims) + _mm(p3, tri, dims))


@jax.custom_vjp
def _cumsum_rows(x):
    return _tri_dot(x, True, NN, True)


def _cumsum_rows_fwd(x):
    return _tri_dot(x, True, NN, True), None


def _cumsum_rows_bwd(_, g):
    return (_tri_dot(g, True, TN, True),)


_cumsum_rows.defvjp(_cumsum_rows_fwd, _cumsum_rows_bwd)


@jax.custom_vjp
def _cumsum_rows_t(x):
    return _tri_dot(x, False, TN, False)


def _cumsum_rows_t_fwd(x):
    return _tri_dot(x, False, TN, False), None


def _cumsum_rows_t_bwd(_, g):
    return (_tri_dot(g, False, NT, True),)


_cumsum_rows_t.defvjp(_cumsum_rows_t_fwd, _cumsum_rows_t_bwd)


def _delta_chunk(ss, qs, ks, vs, bas, zs, alog, dtb, og, tsaved=None):
    heads = range(len(ss))
    lane = lax.broadcasted_iota(jnp.int32, (1, LANE), 1)
    r = lax.broadcasted_iota(jnp.int32, (CH, CH), 0)
    c = lax.broadcasted_iota(jnp.int32, (CH, CH), 1)
    ri = lax.broadcasted_iota(jnp.int32, (CH, 1), 0)
    incl, strict = r >= c, r > c

    def pick(x, h):
        return jnp.sum(jnp.where(lane == h, x, 0.0), axis=-1, keepdims=True)

    beta = [jax.nn.sigmoid(pick(bas[e // NH], e % NH)) for e in heads]
    g = [-jnp.exp(pick(alog, e % NH + NH)) * _softplus(pick(bas[e // NH], e % NH + NH) + pick(dtb, e % NH + NH))
         for e in heads]
    gb = [jnp.broadcast_to(x, (CH, CH)) for x in g]
    gca = [_cumsum_rows(x) for x in gb]
    gcr = [_cumsum_rows_t(x) for x in gb]
    gc = [jnp.sum(jnp.where(c == 0, x, 0.0), axis=-1, keepdims=True) for x in gca]
    gl = [jnp.sum(jnp.where(ri == CH - 1, x, 0.0), axis=0, keepdims=True) for x in gc]
    diff = [a - b for a, b in zip(gca, gcr)]
    gam_s = [jnp.where(strict, jnp.exp(jnp.where(strict, x, 0.0)), 0.0) for x in diff]
    gam_i = [jnp.where(incl, jnp.exp(jnp.where(incl, x, 0.0)), 0.0) for x in diff]

    kk = [_mm(k, k, NT) for k in ks]
    lmats = [beta[h] * kk[h] * gam_s[h] for h in heads]
    tinv = _unit_lower_inverse(lmats) if tsaved is None else _saved_unit_lower_inverse(lmats, tsaved)

    eg = [jnp.exp(x) for x in gc]
    u = [_mm(tinv[h], vs[h] * beta[h], NN) for h in heads]
    w = [_mm(tinv[h], ks[h] * (beta[h] * eg[h]), NN) for h in heads]
    qk = [_mm(qs[h], ks[h], NT) * gam_i[h] for h in heads]
    vn = [u[h] - _mm(w[h], ss[h], NN) for h in heads]
    o = [_mm(qs[h] * eg[h], ss[h], NN) + _mm(qk[h], vn[h], NN) for h in heads]
    sn = [jnp.exp(gl[h]) * ss[h] + _mm(ks[h] * jnp.exp(gl[h] - gc[h]), vn[h], TN) for h in heads]
    y = [_rms(o[h], og) * jax.nn.silu(zs[h]) for h in heads]
    return (sn, y), tinv


def _chain_blocks(ref):
    return [ref[b, :, LANE * h:LANE * (h + 1)] for b in range(ref.shape[0]) for h in range(NH)]


def _ride_along(ex, first, last):
    if ex is None:
        return

    @pl.when(first)
    def _():
        ex.start()

    @pl.when(last)
    def _():
        ex.wait()


def _delta_fwd(qkvn, proj, alog, dtb, og, bsz, t, gather=()):
    n = bsz * t
    nc = t // CH
    ng = len(gather)

    def body(q_ref, k_ref, v_ref, ba_ref, z_ref, al_ref, dt_ref, og_ref, *rest):
        g_in, (y_ref, sh_ref, ti_ref) = rest[:ng], rest[ng:ng + 3]
        g_out, (s_scr, *sems) = rest[ng + 3:2 * ng + 3], rest[2 * ng + 3:]
        ci = pl.program_id(0)
        _ride_along(_Exchange(False, g_in, g_out, *sems) if ng else None, ci == 0, ci == nc - 1)

        @pl.when(ci == 0)
        def _():
            s_scr[...] = jnp.zeros_like(s_scr)

        chains = range(bsz * NH)
        ss = [s_scr[e] for e in chains]
        for e in chains:
            sh_ref[e // NH, e % NH] = ss[e]
        (sn, y), tinv = _delta_chunk(ss, _chain_blocks(q_ref), _chain_blocks(k_ref), _chain_blocks(v_ref),
                                     [ba_ref[b] for b in range(bsz)], _chain_blocks(z_ref),
                                     al_ref[...], dt_ref[...], og_ref[...])
        for e in chains:
            s_scr[e] = sn[e]
            ti_ref[e // NH, e % NH] = tinv[e]
            y_ref[e // NH, :, LANE * (e % NH):LANE * (e % NH + 1)] = y[e]

    def blk(width, col):
        return pl.BlockSpec((bsz, CH, width), lambda ci: (0, ci, col))

    def per_chunk(rows):
        return pl.BlockSpec((bsz, None, NH, rows, rows), lambda ci: (0, ci, 0, 0, 0))

    hbm = pl.BlockSpec(memory_space=pltpu.HBM)
    qkv3, proj3 = qkvn.reshape(bsz, t, 3 * D), proj.reshape(bsz, t, NP)
    y, *rest = pl.pallas_call(
        body, name="delta_fwd_gather" if ng else "delta_fwd", grid=(nc,),
        in_specs=[blk(D, 0), blk(D, 1), blk(D, 2), blk(LANE, C_BA // LANE), blk(D, C_BZ // D),
                  _full((1, LANE)), _full((1, LANE)), _full((1, LANE))] + [hbm] * ng,
        out_specs=[blk(D, 0), per_chunk(LANE), per_chunk(CH)] + [hbm] * ng,
        out_shape=[SDS((bsz, t, D), F32), SDS((bsz, nc, NH, LANE, LANE), F32), SDS((bsz, nc, NH, CH, CH), F32)]
        + [SDS((NDEV,) + a.shape, a.dtype) for a in gather],
        scratch_shapes=[pltpu.VMEM((bsz * NH, LANE, LANE), F32)] + (_exchange_sems(ng) if ng else []),
        compiler_params=_cp(48),
    )(qkv3, qkv3, qkv3, proj3, proj3, alog, dtb, og, *gather)
    return (y.reshape(n, D), *rest)


def _delta_bwd(dproj, qkvn, proj, shist, tsaved, dyb, alog, dtb, og, bsz, t, scatter=()):
    n = bsz * t
    nc = t // CH
    ns = len(scatter)
    eg = 1

    def body(dp_any, q_ref, k_ref, v_ref, ba_ref, z_ref, sh_ref, ti_ref, dy_ref, al_ref, dt_ref, og_ref, *rest):
        s_in, (dp_ref, dqkv_ref, gal_ref, gdt_ref, gog_ref) = rest[:ns], rest[ns:ns + 5]
        s_out, (ds_scr, *sems) = rest[ns + 5:2 * ns + 5], rest[2 * ns + 5:]
        bi, ci = pl.program_id(0), pl.program_id(1)
        _ride_along(_Exchange(True, s_in, s_out, *sems) if ns else None,
                    (bi == 0) & (ci == 0), (bi == bsz // eg - 1) & (ci == nc - 1))

        @pl.when((bi == 0) & (ci == 0))
        def _():
            gal_ref[...] = jnp.zeros_like(gal_ref)
            gdt_ref[...] = jnp.zeros_like(gdt_ref)
            gog_ref[...] = jnp.zeros_like(gog_ref)

        @pl.when(ci == 0)
        def _():
            ds_scr[...] = jnp.zeros_like(ds_scr)

        chains = range(eg * NH)
        _, vjp, _ = jax.vjp(_delta_chunk, [sh_ref[e // NH, e % NH] for e in chains], _chain_blocks(q_ref),
                            _chain_blocks(k_ref), _chain_blocks(v_ref), [ba_ref[b] for b in range(eg)],
                            _chain_blocks(z_ref), al_ref[...], dt_ref[...], og_ref[...],
                            [ti_ref[e // NH, e % NH] for e in chains], has_aux=True)
        ds, dq, dk, dv, dba, dz, dal, ddt, dog, _ = vjp(([ds_scr[e] for e in chains], _chain_blocks(dy_ref)))
        gal_ref[...] += dal
        gdt_ref[...] += ddt
        gog_ref[...] += dog
        for b in range(eg):
            dp_ref[b, :, D:D + LANE] = dba[b].astype(BF16)
        for e in chains:
            b, lo = e // NH, LANE * (e % NH)
            ds_scr[e] = ds[e]
            dp_ref[b, :, lo:lo + LANE] = dz[e].astype(BF16)
            dqkv_ref[b, :, lo:lo + LANE] = dq[e]
            dqkv_ref[b, :, D + lo:D + lo + LANE] = dk[e]
            dqkv_ref[b, :, 2 * D + lo:2 * D + lo + LANE] = dv[e]

    def blk(width, col):
        return pl.BlockSpec((eg, CH, width), lambda bi, ci: (bi, nc - 1 - ci, col))

    def per_chunk(rows):
        return pl.BlockSpec((eg, None, NH, rows, rows), lambda bi, ci: (bi, nc - 1 - ci, 0, 0, 0))

    hbm = pl.BlockSpec(memory_space=pltpu.HBM)
    qkv3, proj3 = qkvn.reshape(bsz, t, 3 * D), proj.reshape(bsz, t, NP)
    dproj, dqkvn, *rest = pl.pallas_call(
        body, name="delta_bwd_exchange" if ns else "delta_bwd", grid=(bsz // eg, nc),
        in_specs=[pl.BlockSpec(memory_space=pl.ANY), blk(D, 0), blk(D, 1), blk(D, 2), blk(LANE, C_BA // LANE),
                  blk(D, C_BZ // D), per_chunk(LANE), per_chunk(CH),
                  blk(D, 0), _full((1, LANE)), _full((1, LANE)), _full((1, LANE))] + [hbm] * ns,
        out_specs=[blk(D + LANE, C_BZ // (D + LANE)), blk(3 * D, 0), _full((1, LANE)), _full((1, LANE)),
                   _full((1, LANE))] + [hbm] * ns,
        out_shape=[SDS((bsz, t, NP), BF16), SDS((bsz, t, 3 * D), F32), SDS((1, LANE), F32), SDS((1, LANE), F32),
                   SDS((1, LANE), F32)] + [SDS(a.shape, a.dtype) for a in scatter],
        input_output_aliases={0: 0},
        scratch_shapes=[pltpu.VMEM((eg * NH, LANE, LANE), F32)] + (_exchange_sems(ns) if ns else []),
        compiler_params=_cp(56),
    )(dproj.reshape(bsz, t, NP), qkv3, qkv3, qkv3, proj3, proj3, shist, tsaved, dyb.reshape(bsz, t, D), alog, dtb, og,
      *scatter)
    return (dproj.reshape(n, NP), dqkvn.reshape(n, 3 * D), *rest)


def _c_chunk(us, vs, zs, lgs, lbs, ws, bsb):
    gv = [jax.nn.gelu(v) for v in vs]
    width = LANE * len(gv)
    mu = sum(jnp.sum(x, axis=-1, keepdims=True) for x in gv) / width
    var = sum(jnp.sum(jnp.square(x - mu), axis=-1, keepdims=True) for x in gv) / width
    rstd = lax.rsqrt(var + EPS)
    r = lax.broadcasted_iota(jnp.int32, (SG, SG), 0)
    c = lax.broadcasted_iota(jnp.int32, (SG, SG), 1)
    out = []
    for j in range(len(gv)):
        nrm = (gv[j] - mu) * rstd * lgs[j] + lbs[j]
        mixed = jnp.dot(jnp.where(r >= c, ws[j], 0.0), nrm, preferred_element_type=F32) + bsb[j]
        out.append(jax.nn.gelu(us[j]) * mixed * jax.nn.silu(zs[j]))
    return out


def _c_args(u_ref, v_ref, z_ref, lg_ref, lb_ref, ws_ref, bs_ref):
    sl = [slice(LANE * j, LANE * (j + 1)) for j in range(4)]
    return ([u_ref[:, s] for s in sl], [v_ref[:, s] for s in sl], [z_ref[:, s] for s in sl],
            [lg_ref[:, s] for s in sl], [lb_ref[:, s] for s in sl],
            [ws_ref[j] for j in range(4)], [bs_ref[j] for j in range(4)])


def _c_fwd(proj, lg, lb, ws, bsb, n):
    cb = C_C // 512

    def body(u_ref, v_ref, z_ref, lg_ref, lb_ref, ws_ref, bs_ref, y_ref):
        outs = _c_chunk(*_c_args(u_ref, v_ref, z_ref, lg_ref, lb_ref, ws_ref, bs_ref))
        for j, o in enumerate(outs):
            y_ref[:, LANE * j:LANE * (j + 1)] = o

    return pl.pallas_call(
        body, name="c_fwd", grid=(n // SG,),
        in_specs=[pl.BlockSpec((SG, 512), lambda i: (i, cb)), pl.BlockSpec((SG, 512), lambda i: (i, cb + 1)),
                  pl.BlockSpec((SG, 512), lambda i: (i, cb + 2)), _full((1, 512)), _full((1, 512)),
                  _full((4, SG, SG)), _full((4, SG, SG))],
        out_specs=pl.BlockSpec((SG, 512), lambda i: (i, 0)),
        out_shape=SDS((n, 512), F32), compiler_params=_cp(32),
    )(proj, proj, proj, lg, lb, ws, bsb)


def _c_bwd(dproj, proj, dy, lg, lb, ws, bsb, n):
    cb = C_C // 512

    def body(dp_any, u_ref, v_ref, z_ref, dy_ref, lg_ref, lb_ref, ws_ref, bs_ref,
             dp_ref, glg_ref, glb_ref, gws_ref, gbs_ref):
        @pl.when(pl.program_id(0) == 0)
        def _():
            glg_ref[...] = jnp.zeros_like(glg_ref)
            glb_ref[...] = jnp.zeros_like(glb_ref)
            gws_ref[...] = jnp.zeros_like(gws_ref)
            gbs_ref[...] = jnp.zeros_like(gbs_ref)

        _, vjp = jax.vjp(_c_chunk, *_c_args(u_ref, v_ref, z_ref, lg_ref, lb_ref, ws_ref, bs_ref))
        dus, dvs, dzs, dlgs, dlbs, dwss, dbss = vjp([dy_ref[:, LANE * j:LANE * (j + 1)] for j in range(4)])
        for j in range(4):
            sl = slice(LANE * j, LANE * (j + 1))
            dp_ref[:, LANE * j:LANE * (j + 1)] = dus[j].astype(BF16)
            dp_ref[:, 512 + LANE * j:512 + LANE * (j + 1)] = dvs[j].astype(BF16)
            dp_ref[:, 1024 + LANE * j:1024 + LANE * (j + 1)] = dzs[j].astype(BF16)
            glg_ref[:, sl] += dlgs[j]
            glb_ref[:, sl] += dlbs[j]
            gws_ref[j] += dwss[j]
            gbs_ref[j] += jnp.broadcast_to(jnp.sum(dbss[j], axis=-1, keepdims=True), (SG, SG))

    return pl.pallas_call(
        body, name="c_bwd", grid=(n // SG,),
        in_specs=[pl.BlockSpec(memory_space=pl.ANY),
                  pl.BlockSpec((SG, 512), lambda i: (i, cb)), pl.BlockSpec((SG, 512), lambda i: (i, cb + 1)),
                  pl.BlockSpec((SG, 512), lambda i: (i, cb + 2)), pl.BlockSpec((SG, 512), lambda i: (i, 0)),
                  _full((1, 512)), _full((1, 512)), _full((4, SG, SG)), _full((4, SG, SG))],
        out_specs=[pl.BlockSpec((SG, 1536), lambda i: (i, C_C // 1536)),
                   _full((1, 512)), _full((1, 512)), _full((4, SG, SG)), _full((4, SG, SG))],
        out_shape=[SDS((n, NP), BF16), SDS((1, 512), F32), SDS((1, 512), F32), SDS((4, SG, SG), F32), SDS((4, SG, SG), F32)],
        input_output_aliases={0: 0}, compiler_params=_cp(32),
    )(dproj, proj, proj, proj, dy, lg, lb, ws, bsb)


def _merge_fwd(x2d, ya, yb, yc, proj, ap, bp, cp, wo):
    n = x2d.shape[0]
    tm = _tile(n, 512)
    gb = C_G // D

    def body(x_ref, ya_ref, yb_ref, yc_ref, g0_ref, g1_ref, g2_ref, ap_ref, bp_ref, cp_ref, wo_ref, o_ref):
        merged = (jax.nn.sigmoid(g0_ref[...]) * _bdot(ya_ref[...], ap_ref[...])
                  + jax.nn.sigmoid(g1_ref[...]) * _bdot(yb_ref[...], bp_ref[...])
                  + jax.nn.sigmoid(g2_ref[...]) * _bdot(yc_ref[...], cp_ref[...]))
        o_ref[...] = x_ref[...] + _bdot(merged, wo_ref[...])

    def rows(w):
        return pl.BlockSpec((tm, w), lambda i: (i, 0))

    return pl.pallas_call(
        body, name="merge_fwd", grid=(n // tm,),
        in_specs=[rows(D), rows(512), rows(D), rows(512),
                  pl.BlockSpec((tm, D), lambda i: (i, gb)), pl.BlockSpec((tm, D), lambda i: (i, gb + 1)),
                  pl.BlockSpec((tm, D), lambda i: (i, gb + 2)),
                  _resident((512, D)), _resident((D, D)), _resident((512, D)), _resident((D, D))],
        out_specs=rows(D), out_shape=SDS((n, D), F32), compiler_params=_cp(48),
    )(x2d, ya, yb, yc, proj, proj, proj, ap, bp, cp, wo)


def _merge_bwd(dxo, ya, yb, yc, proj, ap, bp, cp, apt, bpt, cpt, wot):
    n = dxo.shape[0]
    tm = _tile(n, 256)
    gb = C_G // D

    def body(d_ref, ya_ref, yb_ref, yc_ref, g0_ref, g1_ref, g2_ref, ap_ref, bp_ref, cp_ref,
             apt_ref, bpt_ref, cpt_ref, wot_ref,
             dp_ref, dya_ref, dyb_ref, dyc_ref, dpa_ref, dpb_ref, dpc_ref, mg_ref):
        dm = _bdot(d_ref[...], wot_ref[...])
        merged = None
        for j, (g_ref, y_ref, w_ref, wt_ref, dy_ref, dpj_ref) in enumerate((
                (g0_ref, ya_ref, ap_ref, apt_ref, dya_ref, dpa_ref),
                (g1_ref, yb_ref, bp_ref, bpt_ref, dyb_ref, dpb_ref),
                (g2_ref, yc_ref, cp_ref, cpt_ref, dyc_ref, dpc_ref))):
            s = jax.nn.sigmoid(g_ref[...])
            pj = _bdot(y_ref[...], w_ref[...])
            merged = s * pj if merged is None else merged + s * pj
            dp_ref[:, D * j:D * (j + 1)] = (dm * pj * s * (1.0 - s)).astype(BF16)
            dpj = (dm * s).astype(BF16)
            dpj_ref[...] = dpj
            dy_ref[...] = jnp.dot(dpj, wt_ref[...], preferred_element_type=F32)
        mg_ref[...] = merged

    def rows(w):
        return pl.BlockSpec((tm, w), lambda i: (i, 0))

    return pl.pallas_call(
        body, name="merge_bwd", grid=(n // tm,),
        in_specs=[rows(D), rows(512), rows(D), rows(512),
                  pl.BlockSpec((tm, D), lambda i: (i, gb)), pl.BlockSpec((tm, D), lambda i: (i, gb + 1)),
                  pl.BlockSpec((tm, D), lambda i: (i, gb + 2)),
                  _resident((512, D)), _resident((D, D)), _resident((512, D)),
                  _resident((D, 512)), _resident((D, D)), _resident((D, 512)), _resident((D, D))],
        out_specs=[pl.BlockSpec((tm, 3 * D), lambda i: (i, C_G // (3 * D))), rows(512), rows(D), rows(512),
                   rows(D), rows(D), rows(D), rows(D)],
        out_shape=[SDS((n, NP), BF16), SDS((n, 512), F32), SDS((n, D), F32), SDS((n, 512), F32),
                   SDS((n, D), BF16), SDS((n, D), BF16), SDS((n, D), BF16), SDS((n, D), F32)],
        compiler_params=_cp(56),
    )(dxo, ya, yb, yc, proj, proj, proj, ap, bp, cp, apt, bpt, cpt, wot)


def _sum_parts(p_ref):
    g = p_ref[0].astype(F32)
    for s in range(1, NDEV):
        g = g + p_ref[s].astype(F32)
    return g


def _adamw(g, w, m, v):
    nm = ADAM_B1 * m + (1.0 - ADAM_B1) * g
    nv = ADAM_B2 * v + (1.0 - ADAM_B2) * jnp.square(g)
    nm_hat = nm / (1.0 - ADAM_B1 ** ADAM_STEP)
    nv_hat = nv / (1.0 - ADAM_B2 ** ADAM_STEP)
    return -ADAM_LR * (nm_hat / (jnp.sqrt(nv_hat) + ADAM_EPS) + ADAM_WD * w), nm, nv


def _reduce_adamw(parts, w, m, v, name):
    r, c = w.shape
    tr = _tile(r, 128)

    def body(p_ref, w_ref, m_ref, v_ref, g_ref, d_ref, nm_ref, nv_ref):
        g = _sum_parts(p_ref)
        g_ref[...] = g
        d_ref[...], nm_ref[...], nv_ref[...] = _adamw(g, w_ref[...], m_ref[...], v_ref[...])

    blk = pl.BlockSpec((tr, c), lambda i: (i, 0))
    return pl.pallas_call(
        body, name=name, grid=(r // tr,),
        in_specs=[pl.BlockSpec((NDEV, tr, c), lambda i: (0, i, 0)), blk, blk, blk],
        out_specs=[blk, blk, blk, blk], out_shape=[SDS((r, c), F32)] * 4, compiler_params=_cp(48),
    )(parts, w, m, v)


def _reduce_adamw_leaves(parts, ws, ms, vs, name):
    nleaf = len(ws)
    counts = [len(p) if isinstance(p, (list, tuple)) else 0 for p in parts]
    flat = [a for p in parts for a in (p if isinstance(p, (list, tuple)) else [p])]

    def body(*refs):
        p_refs, rest = refs[:len(flat)], refs[len(flat):]
        w_refs, m_refs, v_refs = rest[:nleaf], rest[nleaf:2 * nleaf], rest[2 * nleaf:3 * nleaf]
        outs = rest[3 * nleaf:]
        at = 0
        for i in range(nleaf):
            g_ref, d_ref, nm_ref, nv_ref = outs[i], outs[nleaf + i], outs[2 * nleaf + i], outs[3 * nleaf + i]
            for idx in (range(counts[i]) if counts[i] else [Ellipsis]):
                g = _sum_parts(p_refs[at])
                at += 1
                g_ref[idx] = g
                d_ref[idx], nm_ref[idx], nv_ref[idx] = _adamw(g, w_refs[i][idx], m_refs[i][idx], v_refs[i][idx])

    vm = pl.BlockSpec(memory_space=pltpu.VMEM)
    outs = pl.pallas_call(
        body, name=name, in_specs=[vm] * (len(flat) + 3 * nleaf), out_specs=[vm] * (4 * nleaf),
        out_shape=[SDS(w.shape, F32) for w in ws] * 4, compiler_params=_cp(56),
    )(*flat, *ws, *ms, *vs)
    return [outs[j * nleaf:(j + 1) * nleaf] for j in range(4)]


def _unshard(name, g):
    if name in ROW_SHARDED:
        return g.reshape(g.shape[0] * g.shape[1], g.shape[2])
    g = jnp.moveaxis(g, 0, 1)
    return g.reshape(g.shape[0], g.shape[1] * g.shape[2])


def _reshard(name, full):
    r, c = full.shape
    if name in ROW_SHARDED:
        return full.reshape(NDEV, r // NDEV, c)
    return jnp.moveaxis(full.reshape(r, NDEV, c // NDEV), 1, 0)


def _w_in_to_padded(slabs):
    pieces = []
    for lo, hi, _ in sorted(SEGMENTS, key=lambda s: s[2]):
        for d in range(NDEV):
            a, b = max(lo, d * W_SHARD), min(hi, (d + 1) * W_SHARD)
            if a < b:
                pieces.append(slabs[d, :, a - d * W_SHARD:b - d * W_SHARD])
    pieces.append(jnp.zeros(slabs.shape[1:2] + (NP - C_BA - 16,), slabs.dtype))
    return jnp.concatenate(pieces, axis=-1)


def _w_in_from_padded(g):
    slabs = []
    for d in range(NDEV):
        pieces = []
        for lo, hi, pstart in SEGMENTS:
            a, b = max(lo, d * W_SHARD), min(hi, (d + 1) * W_SHARD)
            if a < b:
                pieces.append(g[:, pstart + a - lo:pstart + b - lo])
        pieces.append(jnp.zeros(g.shape[:1] + (W_SHARD_PAD - W_SHARD,), g.dtype))
        slabs.append(jnp.concatenate(pieces, axis=-1))
    return jnp.stack(slabs)


def _pad_w_in(w):
    return jnp.pad(w, ((0, 0), (0, W_SHARD_PAD - W_SHARD)))


def _lane_row(vec8, offset):
    return jnp.pad(vec8, (offset, LANE - NH - offset))[None]


def kernel(x, norm_g, w_in, a_dw, a_dw_b, a_ln_g, a_ln_b, a_proj, b_conv, b_a_log, b_dt_bias, b_onorm_g, b_proj, c_ln_g, c_ln_b, c_ws, c_bs, c_proj, w_out, final_g, loss_target, m_norm_g, m_w_in, m_a_dw, m_a_dw_b, m_a_ln_g, m_a_ln_b, m_a_proj, m_b_conv, m_b_a_log, m_b_dt_bias, m_b_onorm_g, m_b_proj, m_c_ln_g, m_c_ln_b, m_c_ws, m_c_bs, m_c_proj, m_w_out, m_final_g, v_norm_g, v_w_in, v_a_dw, v_a_dw_b, v_a_ln_g, v_a_ln_b, v_a_proj, v_b_conv, v_b_a_log, v_b_dt_bias, v_b_onorm_g, v_b_proj, v_c_ln_g, v_c_ln_b, v_c_ws, v_c_bs, v_c_proj, v_w_out, v_final_g):
    wts = dict(norm_g=norm_g, w_in=w_in, a_dw=a_dw, a_dw_b=a_dw_b, a_ln_g=a_ln_g, a_ln_b=a_ln_b, a_proj=a_proj,
               b_conv=b_conv, b_a_log=b_a_log, b_dt_bias=b_dt_bias, b_onorm_g=b_onorm_g, b_proj=b_proj,
               c_ln_g=c_ln_g, c_ln_b=c_ln_b, c_ws=c_ws, c_bs=c_bs, c_proj=c_proj, w_out=w_out, final_g=final_g)
    mom = dict(norm_g=m_norm_g, w_in=m_w_in, a_dw=m_a_dw, a_dw_b=m_a_dw_b, a_ln_g=m_a_ln_g, a_ln_b=m_a_ln_b,
               a_proj=m_a_proj, b_conv=m_b_conv, b_a_log=m_b_a_log, b_dt_bias=m_b_dt_bias, b_onorm_g=m_b_onorm_g,
               b_proj=m_b_proj, c_ln_g=m_c_ln_g, c_ln_b=m_c_ln_b, c_ws=m_c_ws, c_bs=m_c_bs, c_proj=m_c_proj,
               w_out=m_w_out, final_g=m_final_g)
    vel = dict(norm_g=v_norm_g, w_in=v_w_in, a_dw=v_a_dw, a_dw_b=v_a_dw_b, a_ln_g=v_a_ln_g, a_ln_b=v_a_ln_b,
               a_proj=v_a_proj, b_conv=v_b_conv, b_a_log=v_b_a_log, b_dt_bias=v_b_dt_bias, b_onorm_g=v_b_onorm_g,
               b_proj=v_b_proj, c_ln_g=v_c_ln_g, c_ln_b=v_c_ln_b, c_ws=v_c_ws, c_bs=v_c_bs, c_proj=v_c_proj,
               w_out=v_w_out, final_g=v_final_g)

    bsz, t, _ = x.shape
    n = bsz * t
    depth = norm_g.shape[0]
    x2d = x.reshape(n, D)
    tgt = loss_target.reshape(n, D)

    def weight_blocks(l):
        return [_pad_w_in(w_in[l].astype(BF16))] + [wts[k][l].astype(BF16) for k in BIG_REST]

    def matmul_weights(w_in_all, *rest):
        got = {k: _unshard(k, g) for k, g in zip(BIG_REST, rest)}
        got['wp'] = _w_in_to_padded(w_in_all)
        return got

    conv_all = _all_gather([wts[k] for k in SMALL], "gather_conv_weights")
    conv_w = {k: jnp.stack([_unshard(k, g[:, l]) for l in range(depth)]) for k, g in zip(SMALL, conv_all)}
    a_dw32 = jnp.pad(conv_w['a_dw'], ((0, 0), (0, 32 - A_K), (0, 0)))
    b_conv8 = jnp.pad(conv_w['b_conv'], ((0, 0), (0, 8 - B_K), (0, 0)))
    bsb = jnp.broadcast_to(c_bs[..., None], c_bs.shape + (SG,))
    full = [matmul_weights(*_all_gather(weight_blocks(0), "gather_matmul_weights"))]

    saved = []
    xl = x2d
    for l in range(depth):
        alog, dtb = _lane_row(b_a_log[l], NH), _lane_row(b_dt_bias[l], NH)
        proj, h = _inproj(xl, norm_g[l][None], full[l]['wp'])
        ya, conv_a = _a_fwd(proj, a_dw32[l], a_dw_b[l][None], a_ln_g[l][None], a_ln_b[l][None], bsz, t)
        ahead = weight_blocks(l + 1) if l + 1 < depth else []
        qkvn, conv_b, *nxt_small = _bprep_fwd(proj, b_conv8[l], bsz, t, gather=ahead[1:])
        yb, shist, tsave, *nxt_w_in = _delta_fwd(qkvn, proj, alog, dtb, b_onorm_g[l][None], bsz, t, gather=ahead[:1])
        if ahead:
            full.append(matmul_weights(*nxt_w_in, *nxt_small))
        yc = _c_fwd(proj, c_ln_g[l][None], c_ln_b[l][None], c_ws[l], bsb[l], n)
        xn = _merge_fwd(xl, ya, yb, yc, proj, full[l]['a_proj'], full[l]['b_proj'], full[l]['c_proj'], full[l]['w_out'])
        saved.append((xl, proj, h, ya, yb, yc, qkvn, shist, tsave, alog, dtb, conv_a, conv_b))
        xl = xn

    dx, g_final, loss_blk = _loss_head(xl, final_g[None], tgt)
    loss = lax.psum(loss_blk[0, 0], ("x", "y", "c"))

    gfull = {k: [None] * depth for k in WEIGHTS if k != 'final_g'}
    recv = [{} for _ in range(depth)]
    sharded = ['w_in'] + SHARDED_REST
    late = ['w_in', 'b_conv']
    early = [k for k in sharded if k not in late]

    def grad_slabs(l, names):
        return [(_w_in_from_padded(gfull[k][l]) if k == 'w_in' else _reshard(k, gfull[k][l])).astype(BF16) for k in names]

    for l in reversed(range(depth)):
        xl, proj, h, ya, yb, yc, qkvn, shist, tsave, alog, dtb, conv_a, conv_b = saved[l]
        ap, bp, cp, wo = full[l]['a_proj'], full[l]['b_proj'], full[l]['c_proj'], full[l]['w_out']
        dproj, dya, dyb, dyc, dpa, dpb, dpc, merged = _merge_bwd(dx, ya, yb, yc, proj, ap, bp, cp, ap.T, bp.T, cp.T, wo.T)
        gfull['a_proj'][l] = _mm_tn(ya, dpa, "grad_a_proj")
        gfull['b_proj'][l] = _mm_tn(yb, dpb, "grad_b_proj")
        gfull['c_proj'][l] = _mm_tn(yc, dpc, "grad_c_proj")
        gfull['w_out'][l] = _mm_tn(merged, dx, "grad_w_out")
        dproj, g_clg, g_clb, g_cws, g_cbs = _c_bwd(dproj, proj, dyc, c_ln_g[l][None], c_ln_b[l][None], c_ws[l], bsb[l], n)
        dproj, g_adw, g_adb, g_alg, g_alb = _a_bwd(dproj, proj, conv_a, dya, a_dw32[l], a_ln_g[l][None], a_ln_b[l][None],
                                                   bsz, t)
        gfull['a_dw'][l] = g_adw[:A_K]
        riders = ([(l + 1, sharded)] if l + 1 < depth else []) + ([(0, early)] if l == 0 else [])
        dproj, dqkvn, g_alog, g_dt, g_og, *got = _delta_bwd(
            dproj, qkvn, proj, shist, tsave, dyb, alog, dtb, b_onorm_g[l][None], bsz, t,
            scatter=[s for ll, names in riders for s in grad_slabs(ll, names)])
        for ll, names in riders:
            for k in names:
                recv[ll][k] = got.pop(0)
        dproj, g_bconv = _bprep_bwd(dproj, proj, conv_b, dqkvn, b_conv8[l], bsz, t)
        gfull['w_in'][l] = _mm_tn(h, dproj, "grad_w_in")
        gfull['b_conv'][l] = g_bconv[:B_K]
        dx, g_ng, *got = _inproj_bwd(dproj, full[l]['wp'], xl, norm_g[l][None], dx,
                                     scatter=grad_slabs(0, late) if l == 0 else ())
        if got:
            recv[0].update(zip(late, got))
        gfull['norm_g'][l] = g_ng[0]
        gfull['a_dw_b'][l], gfull['a_ln_g'][l], gfull['a_ln_b'][l] = g_adb[0], g_alg[0], g_alb[0]
        gfull['b_a_log'][l], gfull['b_dt_bias'][l] = g_alog[0, NH:2 * NH], g_dt[0, NH:2 * NH]
        gfull['b_onorm_g'][l] = g_og[0]
        gfull['c_ln_g'][l], gfull['c_ln_b'][l] = g_clg[0], g_clb[0]
        gfull['c_ws'][l], gfull['c_bs'][l] = g_cws, g_cbs[:, :, 0]
    grad_x = dx.reshape(bsz, t, D)

    outs_w = [_reduce_adamw(recv[l]['w_in'], _pad_w_in(w_in[l]), _pad_w_in(m_w_in[l]), _pad_w_in(v_w_in[l]), "adamw_w_in")
              for l in range(depth)]
    outs_s = _reduce_adamw_leaves([[recv[l][k] for l in range(depth)] for k in SHARDED_REST],
                                  [wts[k] for k in SHARDED_REST], [mom[k] for k in SHARDED_REST],
                                  [vel[k] for k in SHARDED_REST], "adamw_sharded")

    def upto3d(a):
        return a[None] if a.ndim == 1 else a.reshape((-1,) + a.shape[-2:]) if a.ndim > 3 else a

    grepl = [upto3d(jnp.stack(gfull[k]) if k != 'final_g' else g_final[0]) for k in REPL]
    outs_r = _reduce_adamw_leaves(_all_gather(grepl, "gather_replicated_grads"), [upto3d(wts[k]) for k in REPL],
                                  [upto3d(mom[k]) for k in REPL], [upto3d(vel[k]) for k in REPL], "adamw_replicated")

    res = []
    for j in range(4):
        leaves = {k: outs_r[j][i].reshape(wts[k].shape) for i, k in enumerate(REPL)}
        leaves.update({k: outs_s[j][i] for i, k in enumerate(SHARDED_REST)})
        leaves['w_in'] = jnp.stack([outs_w[l][j][:, :W_SHARD] for l in range(depth)])
        res.append([leaves[k] for k in WEIGHTS])
    grads, deltas, new_m, new_v = res
    return (loss, grad_x, *grads, *deltas, *new_m, *new_v)
```

```python
import jax
import jax.numpy as jnp
from jax import lax
from jax.experimental import pallas as pl
from jax.experimental.pallas import tpu as pltpu

F32 = jnp.float32
BF16 = jnp.bfloat16
SDS = jax.ShapeDtypeStruct
MESH = pl.DeviceIdType.MESH

NDEV = 8
D = 1024
EPS = 1e-6
LANE = 128

C_Q, C_K, C_V = 0, 1024, 2048
C_G = 3072
C_A = 6144
C_C = 7680
C_BZ = 9216
C_BA = 10240
NP = 10368
N_IN = 10256
SEGMENTS = ((0, 1536, C_A), (1536, 4608, C_Q), (4608, 5632, C_BZ), (5632, 5648, C_BA), (5648, 7184, C_C), (7184, 10256, C_G))
W_SHARD = N_IN // NDEV
W_SHARD_PAD = 1408

A_K, A_H, A_RB = 31, 32, 32
B_K, B_H, B_RB, B_CB = 4, 8, 64, 512
CH = 64
SG = 128
NH = 8

ADAM_LR, ADAM_B1, ADAM_B2, ADAM_EPS, ADAM_WD, ADAM_STEP = 0.001, 0.9, 0.999, 1e-08, 0.01, 10

WEIGHTS = ['norm_g', 'w_in', 'a_dw', 'a_dw_b', 'a_ln_g', 'a_ln_b', 'a_proj', 'b_conv', 'b_a_log', 'b_dt_bias',
           'b_onorm_g', 'b_proj', 'c_ln_g', 'c_ln_b', 'c_ws', 'c_bs', 'c_proj', 'w_out', 'final_g']
SHARDED_REST = ['a_dw', 'a_proj', 'b_conv', 'b_proj', 'c_proj', 'w_out']
REPL = [n for n in WEIGHTS if n != 'w_in' and n not in SHARDED_REST]
BIG_REST = ['a_proj', 'b_proj', 'c_proj', 'w_out']
SMALL = ['a_dw', 'b_conv']
ROW_SHARDED = ('b_proj', 'w_out')

NN = ((1,), (0,))
NT = ((1,), (1,))
TN = ((0,), (0,))


def _tile(n, pref):
    return pref if (n >= pref and n % pref == 0) else n


def _cp(vmem_mb):
    return pltpu.CompilerParams(vmem_limit_bytes=vmem_mb * 2 ** 20)


def _full(shape):
    nd = len(shape)
    return pl.BlockSpec(shape, lambda *_: (0,) * nd)


def _resident(shape):
    nd = len(shape)
    return pl.BlockSpec(shape, lambda *_: (0,) * nd, pipeline_mode=pl.Buffered(1))


def _rms(x, g):
    return x * lax.rsqrt(jnp.mean(x * x, axis=-1, keepdims=True) + EPS) * g


def _softplus(x):
    return jnp.maximum(x, 0.0) + jnp.log1p(jnp.exp(-jnp.abs(x)))


def _bdot(a, b):
    return jnp.dot(a.astype(BF16), b.astype(BF16), preferred_element_type=F32)


def _mm(a, b, dims):
    return lax.dot_general(a, b, (dims, ((), ())), preferred_element_type=F32)


def _all_gather(xs, name):
    nops = len(xs)

    def body(*refs):
        x_refs, out_refs = refs[:nops], refs[nops:2 * nops]
        send_sems, recv_sems, local_sems = refs[2 * nops:]
        x, y, cc = lax.axis_index("x"), lax.axis_index("y"), lax.axis_index("c")
        me, sibling = (x, y, cc), (x, y, 1 - cc)
        chips = [(1 - x, y), (x, 1 - y), (1 - x, 1 - y)]

        def slot(t, px, py, pc):
            return out_refs[t].at[4 * px + 2 * py + pc]

        def copy(t, k, block, to, src=None):
            return pltpu.make_async_remote_copy(
                src_ref=slot(t, *block) if src is None else src, dst_ref=slot(t, *block),
                send_sem=send_sems.at[7 * t + k], recv_sem=recv_sems.at[7 * t + k], device_id=to, device_id_type=MESH)

        ops = range(nops)
        mine = [pltpu.make_async_copy(x_refs[t], slot(t, *me), local_sems.at[t]) for t in ops]
        for cp in mine:
            cp.start()
        first = [copy(t, 0, me, sibling, src=x_refs[t]) for t in ops]
        first += [copy(t, 1 + j, me, (*chip, cc), src=x_refs[t]) for j, chip in enumerate(chips) for t in ops]
        for cp in first:
            cp.start()
        passed = []
        for j, chip in enumerate(chips):
            for t in ops:
                copy(t, 1 + j, (*chip, cc), me).wait_recv()
                fwd = copy(t, 4 + j, (*chip, cc), sibling)
                fwd.start()
                passed.append(fwd)
        for t in ops:
            copy(t, 0, sibling, me).wait_recv()
        for j, chip in enumerate(chips):
            for t in ops:
                copy(t, 4 + j, (*chip, 1 - cc), me).wait_recv()
        for cp in first + passed:
            cp.wait_send()
        for cp in mine:
            cp.wait()

    hbm = pl.BlockSpec(memory_space=pltpu.HBM)
    return pl.pallas_call(
        body, name=name, out_shape=[SDS((NDEV,) + a.shape, a.dtype) for a in xs],
        in_specs=[hbm] * nops, out_specs=[hbm] * nops,
        scratch_shapes=[pltpu.SemaphoreType.DMA((7 * nops,)), pltpu.SemaphoreType.DMA((7 * nops,)),
                        pltpu.SemaphoreType.DMA((nops,))],
    )(*xs)


def _exchange_sems(nops):
    return [pltpu.SemaphoreType.DMA((7 * nops,)), pltpu.SemaphoreType.DMA((7 * nops,)), pltpu.SemaphoreType.DMA((nops,))]


class _Exchange:
    def __init__(self, scatter, in_refs, out_refs, send_sems, recv_sems, local_sems):
        x, y, cc = lax.axis_index("x"), lax.axis_index("y"), lax.axis_index("c")
        me = 4 * x + 2 * y + cc
        nops = len(in_refs)
        self.local = [pltpu.make_async_copy(in_refs[t].at[me] if scatter else in_refs[t], out_refs[t].at[me],
                                            local_sems.at[t]) for t in range(nops)]
        self.sends, self.recvs = [], []
        for k in range(1, NDEV):
            px = 1 - x if k & 4 else x
            py = 1 - y if k & 2 else y
            pc = 1 - cc if k & 1 else cc
            peer = 4 * px + 2 * py + pc
            for t in range(nops):
                sem = 7 * t + k - 1
                self.sends.append(pltpu.make_async_remote_copy(
                    src_ref=in_refs[t].at[peer] if scatter else in_refs[t], dst_ref=out_refs[t].at[me],
                    send_sem=send_sems.at[sem], recv_sem=recv_sems.at[sem], device_id=(px, py, pc), device_id_type=MESH))
                self.recvs.append(pltpu.make_async_remote_copy(
                    src_ref=in_refs[t].at[me] if scatter else in_refs[t], dst_ref=out_refs[t].at[peer],
                    send_sem=send_sems.at[sem], recv_sem=recv_sems.at[sem], device_id=(px, py, pc), device_id_type=MESH))

    def start(self):
        for cp in self.local + self.sends:
            cp.start()

    def wait(self):
        for cp in self.recvs:
            cp.wait_recv()
        for cp in self.sends:
            cp.wait_send()
        for cp in self.local:
            cp.wait()


def _inproj(x2d, g_row, wp):
    n = x2d.shape[0]
    tm, tn = _tile(n, 1024), 1152

    def body(x_ref, g_ref, w_ref, proj_ref, h_ref, hs):
        @pl.when(pl.program_id(1) == 0)
        def _():
            h = _rms(x_ref[...], g_ref[...]).astype(BF16)
            hs[...] = h
            h_ref[...] = h

        proj_ref[...] = jnp.dot(hs[...], w_ref[...], preferred_element_type=F32)

    return pl.pallas_call(
        body, name="inproj", grid=(n // tm, NP // tn),
        in_specs=[pl.BlockSpec((tm, D), lambda i, j: (i, 0)), _full((1, D)), pl.BlockSpec((D, tn), lambda i, j: (0, j))],
        out_specs=[pl.BlockSpec((tm, tn), lambda i, j: (i, j)), pl.BlockSpec((tm, D), lambda i, j: (i, 0))],
        out_shape=[SDS((n, NP), F32), SDS((n, D), BF16)],
        scratch_shapes=[pltpu.VMEM((tm, D), BF16)], compiler_params=_cp(48),
    )(x2d, g_row, wp)


def _mm_tn(a, b, name):
    nn, m = a.shape
    k = b.shape[1]
    tk = 1152 if k % 1152 == 0 else _tile(k, 1024)
    tn = _tile(nn, 1024)

    def body(a_ref, b_ref, o_ref):
        p = _mm(a_ref[...].astype(BF16), b_ref[...].astype(BF16), TN)

        @pl.when(pl.program_id(1) == 0)
        def _():
            o_ref[...] = p

        @pl.when(pl.program_id(1) > 0)
        def _():
            o_ref[...] += p

    return pl.pallas_call(
        body, name=name, grid=(k // tk, nn // tn),
        in_specs=[pl.BlockSpec((tn, m), lambda j, t: (t, 0)), pl.BlockSpec((tn, tk), lambda j, t: (t, j))],
        out_specs=pl.BlockSpec((m, tk), lambda j, t: (0, j)),
        out_shape=SDS((m, k), F32), compiler_params=_cp(48),
    )(a, b)


def _inproj_bwd(dproj, wp, x2d, g_row, dxo, scatter=()):
    n = x2d.shape[0]
    tm, tk = _tile(n, 1024), 1152
    nk = NP // tk
    ns = len(scatter)

    def body(dp_ref, w_ref, x_ref, g_ref, dxo_ref, *rest):
        s_in, (dx_ref, dg_ref), s_out, (acc, *sems) = rest[:ns], rest[ns:ns + 2], rest[ns + 2:2 * ns + 2], rest[2 * ns + 2:]
        i, k = pl.program_id(0), pl.program_id(1)
        _ride_along(_Exchange(True, s_in, s_out, *sems) if ns else None,
                    (i == 0) & (k == 0), (i == n // tm - 1) & (k == nk - 1))
        p = _mm(dp_ref[...], w_ref[...], NT)

        @pl.when(k == 0)
        def _():
            acc[...] = p

        @pl.when(k > 0)
        def _():
            acc[...] += p

        @pl.when(k == nk - 1)
        def _():
            _, vjp = jax.vjp(_rms, x_ref[...], g_ref[...])
            dx, dg = vjp(acc[...])
            dx_ref[...] = dxo_ref[...] + dx

            @pl.when(i == 0)
            def _():
                dg_ref[...] = dg

            @pl.when(i > 0)
            def _():
                dg_ref[...] += dg

    hbm = pl.BlockSpec(memory_space=pltpu.HBM)
    return pl.pallas_call(
        body, name="inproj_bwd_exchange" if ns else "inproj_bwd", grid=(n // tm, nk),
        in_specs=[pl.BlockSpec((tm, tk), lambda i, k: (i, k)), pl.BlockSpec((D, tk), lambda i, k: (0, k)),
                  pl.BlockSpec((tm, D), lambda i, k: (i, 0)), _full((1, D)),
                  pl.BlockSpec((tm, D), lambda i, k: (i, 0))] + [hbm] * ns,
        out_specs=[pl.BlockSpec((tm, D), lambda i, k: (i, 0)), _full((1, D))] + [hbm] * ns,
        out_shape=[SDS((n, D), F32), SDS((1, D), F32)] + [SDS(a.shape, a.dtype) for a in scatter],
        scratch_shapes=[pltpu.VMEM((tm, D), F32)] + (_exchange_sems(ns) if ns else []), compiler_params=_cp(56),
    )(dproj, wp, x2d, g_row, dxo, *scatter)


def _loss_head(x2d, g_row, tgt):
    n = x2d.shape[0]
    tm = _tile(n, 512)

    def body(x_ref, g_ref, t_ref, dx_ref, dg_ref, loss_ref):
        i = pl.program_id(0)
        y, vjp = jax.vjp(_rms, x_ref[...], g_ref[...])
        err = y - t_ref[...]
        part = 0.5 * jnp.sum(jnp.mean(err * err, axis=-1, keepdims=True), axis=0, keepdims=True)
        dx, dg = vjp(err * (1.0 / D))
        dx_ref[...] = dx
        lb = jnp.broadcast_to(part, (8, LANE))

        @pl.when(i == 0)
        def _():
            dg_ref[...] = dg
            loss_ref[...] = lb

        @pl.when(i > 0)
        def _():
            dg_ref[...] += dg
            loss_ref[...] += lb

    return pl.pallas_call(
        body, name="loss_head", grid=(n // tm,),
        in_specs=[pl.BlockSpec((tm, D), lambda i: (i, 0)), _full((1, D)), pl.BlockSpec((tm, D), lambda i: (i, 0))],
        out_specs=[pl.BlockSpec((tm, D), lambda i: (i, 0)), _full((1, D)), _full((8, LANE))],
        out_shape=[SDS((n, D), F32), SDS((1, D), F32), SDS((8, LANE), F32)], compiler_params=_cp(40),
    )(x2d, g_row, tgt)


def _conv_rows(dst_ref, read, w_ref, offs, nrows, rb, cb, bias=None):
    ncols = dst_ref.shape[1]
    for c0 in range(0, ncols, cb):
        for r0 in range(0, nrows, rb):
            rows = min(rb, nrows - r0)
            acc = None if bias is None else bias[:, c0:c0 + cb]
            for j, off in enumerate(offs):
                term = w_ref[j:j + 1, c0:c0 + cb] * read(off + r0, rows, c0, cb)
                acc = term if acc is None else acc + term
            dst_ref[r0:r0 + rows, c0:c0 + cb] = acc


def _fwd_offsets(kw, halo):
    return [halo - (kw - 1) + j for j in range(kw)]


def _bwd_offsets(kw):
    return [kw - 1 - j for j in range(kw)]


def _conv_bwd_w(dc, ext_ref, gw_ref, kw, halo, tt):
    for j in range(kw):
        gw_ref[j:j + 1, :] += jnp.sum(dc * ext_ref[pl.ds(halo - (kw - 1) + j, tt), :], axis=0, keepdims=True)


SUB = 8


def _fill_shifts(sh_ref, rows):
    for s in range(1, SUB):
        sh_ref[s, 0:rows, :] = sh_ref[0, pl.ds(s, rows), :]


def _tap(sh_ref, off, rows):
    return sh_ref[off % SUB, pl.ds(off - off % SUB, rows), :]


def _conv_bwd_w_sh(dc, ext_sh, gw_ref, kw, halo, tt):
    for j in range(kw):
        gw_ref[j:j + 1, :] += jnp.sum(dc * _tap(ext_sh, halo - (kw - 1) + j, tt), axis=0, keepdims=True)


def _a_post(c, z, g, b):
    mu = jnp.mean(c, axis=-1, keepdims=True)
    var = jnp.mean(jnp.square(c - mu), axis=-1, keepdims=True)
    a = (c - mu) * lax.rsqrt(var + EPS) * g + b
    return jax.nn.silu(a) * jax.nn.silu(z)


def _a_fwd(proj, dw, b_row, lg, lb, bsz, t, gather=()):
    n = bsz * t
    tt = _tile(t, 256)
    nt = t // tt
    cb = C_A // 512
    ng = len(gather)

    def body(v_ref, g_ref, z_ref, vh_ref, gh_ref, w_ref, b_ref, lg_ref, lb_ref, *rest):
        g_in, (y_ref, c_ref), g_out, (ext, *sems) = rest[:ng], rest[ng:ng + 2], rest[ng + 2:2 * ng + 2], rest[2 * ng + 2:]
        b, i = pl.program_id(0), pl.program_id(1)
        _ride_along(_Exchange(False, g_in, g_out, *sems) if ng else None,
                    (b == 0) & (i == 0), (b == bsz - 1) & (i == nt - 1))
        ext[0, 0:A_H, :] = jnp.where(i > 0, vh_ref[...] * jax.nn.sigmoid(gh_ref[...]), 0.0)
        ext[0, A_H:, :] = v_ref[...] * jax.nn.sigmoid(g_ref[...])
        _fill_shifts(ext, tt + A_H - SUB)
        _conv_rows(c_ref, lambda off, rows, c0, cols: _tap(ext, off, rows), w_ref, _fwd_offsets(A_K, A_H), tt, A_RB, 512,
                   bias=b_ref[...])
        y_ref[...] = _a_post(c_ref[...], z_ref[...], lg_ref[...], lb_ref[...])

    def row(b, i):
        return b * nt + i

    def halo(b, i):
        return jnp.maximum((b * t + i * tt) // A_H - 1, 0)

    hbm = pl.BlockSpec(memory_space=pltpu.HBM)
    return pl.pallas_call(
        body, name="a_fwd_gather" if ng else "a_fwd", grid=(bsz, nt),
        in_specs=[pl.BlockSpec((tt, 512), lambda b, i: (row(b, i), cb)),
                  pl.BlockSpec((tt, 512), lambda b, i: (row(b, i), cb + 1)),
                  pl.BlockSpec((tt, 512), lambda b, i: (row(b, i), cb + 2)),
                  pl.BlockSpec((A_H, 512), lambda b, i: (halo(b, i), cb)),
                  pl.BlockSpec((A_H, 512), lambda b, i: (halo(b, i), cb + 1)),
                  _full((32, 512)), _full((1, 512)), _full((1, 512)), _full((1, 512))] + [hbm] * ng,
        out_specs=[pl.BlockSpec((tt, 512), lambda b, i: (row(b, i), 0))] * 2 + [hbm] * ng,
        out_shape=[SDS((n, 512), F32)] * 2 + [SDS((NDEV,) + a.shape, a.dtype) for a in gather],
        scratch_shapes=[pltpu.VMEM((SUB, tt + A_H, 512), F32)] + (_exchange_sems(ng) if ng else []),
        compiler_params=_cp(40),
    )(proj, proj, proj, proj, proj, dw, b_row, lg, lb, *gather)


def _a_bwd(dproj, proj, conv, dy, dw, lg, lb, bsz, t):
    n = bsz * t
    tt = _tile(t, 256)
    nt = t // tt
    cb = C_A // 512

    def body(dp_any, v_ref, g_ref, z_ref, vh_ref, gh_ref, c_ref, dy_ref, w_ref, lg_ref, lb_ref,
             dp_ref, gw_ref, gb_ref, glg_ref, glb_ref, ext, dcp, dae, carry):
        b, i = pl.program_id(0), pl.program_id(1)
        ti = nt - 1 - i

        @pl.when((b == 0) & (i == 0))
        def _():
            gw_ref[...] = jnp.zeros_like(gw_ref)
            gb_ref[...] = jnp.zeros_like(gb_ref)
            glg_ref[...] = jnp.zeros_like(glg_ref)
            glb_ref[...] = jnp.zeros_like(glb_ref)

        @pl.when(i == 0)
        def _():
            carry[...] = jnp.zeros_like(carry)

        val, glu = v_ref[...], g_ref[...]
        sg = jax.nn.sigmoid(glu)
        ext[0, 0:A_H, :] = jnp.where(ti > 0, vh_ref[...] * jax.nn.sigmoid(gh_ref[...]), 0.0)
        ext[0, A_H:, :] = val * sg
        _fill_shifts(ext, tt + A_H - SUB)
        _, vjp = jax.vjp(_a_post, c_ref[...], z_ref[...], lg_ref[...], lb_ref[...])
        dc, dz, dlg, dlb = vjp(dy_ref[...])
        gb_ref[...] += jnp.sum(dc, axis=0, keepdims=True)
        glg_ref[...] += dlg
        glb_ref[...] += dlb
        dcp[0, 0:A_H, :] = jnp.zeros((A_H, 512), F32)
        dcp[0, A_H:A_H + tt, :] = dc
        dcp[0, A_H + tt:, :] = jnp.zeros((A_H, 512), F32)
        _fill_shifts(dcp, tt + 2 * A_H - SUB)
        _conv_bwd_w_sh(dc, ext, gw_ref, A_K, A_H, tt)
        _conv_rows(dae, lambda off, rows, c0, cols: _tap(dcp, off, rows), w_ref, _bwd_offsets(A_K), tt + A_H, A_RB, 512)
        dae[tt:tt + A_H, :] += carry[...]
        carry[...] = dae[0:A_H, :]
        da = dae[A_H:, :]
        dp_ref[:, 0:512] = (da * sg).astype(BF16)
        dp_ref[:, 512:1024] = (da * val * sg * (1.0 - sg)).astype(BF16)
        dp_ref[:, 1024:1536] = dz.astype(BF16)

    def row(b, i):
        return b * nt + (nt - 1 - i)

    def halo(b, i):
        return jnp.maximum((b * t + (nt - 1 - i) * tt) // A_H - 1, 0)

    outs = pl.pallas_call(
        body, name="a_bwd", grid=(bsz, nt),
        in_specs=[pl.BlockSpec(memory_space=pl.ANY),
                  pl.BlockSpec((tt, 512), lambda b, i: (row(b, i), cb)),
                  pl.BlockSpec((tt, 512), lambda b, i: (row(b, i), cb + 1)),
                  pl.BlockSpec((tt, 512), lambda b, i: (row(b, i), cb + 2)),
                  pl.BlockSpec((A_H, 512), lambda b, i: (halo(b, i), cb)),
                  pl.BlockSpec((A_H, 512), lambda b, i: (halo(b, i), cb + 1)),
                  pl.BlockSpec((tt, 512), lambda b, i: (row(b, i), 0)),
                  pl.BlockSpec((tt, 512), lambda b, i: (row(b, i), 0)),
                  _full((32, 512)), _full((1, 512)), _full((1, 512))],
        out_specs=[pl.BlockSpec((tt, 1536), lambda b, i: (row(b, i), C_A // 1536)),
                   _full((32, 512)), _full((1, 512)), _full((1, 512)), _full((1, 512))],
        out_shape=[SDS((n, NP), BF16), SDS((32, 512), F32), SDS((1, 512), F32), SDS((1, 512), F32), SDS((1, 512), F32)],
        input_output_aliases={0: 0},
        scratch_shapes=[pltpu.VMEM((SUB, tt + A_H, 512), F32), pltpu.VMEM((SUB, tt + 2 * A_H, 512), F32),
                        pltpu.VMEM((tt + A_H, 512), F32), pltpu.VMEM((A_H, 512), F32)],
        compiler_params=_cp(48),
    )(dproj, proj, proj, proj, proj, proj, conv, dy, dw, lg, lb)
    return outs


def _b_post(blocks):
    out = []
    for idx, c in enumerate(blocks):
        s = jax.nn.silu(c)
        if idx < 2 * NH:
            s = s * lax.rsqrt(jnp.sum(s * s, axis=-1, keepdims=True) + EPS)
            if idx < NH:
                s = s * (LANE ** -0.5)
        out.append(s)
    return out


def _bprep_fwd(proj, wconv, bsz, t, gather=()):
    n = bsz * t
    tt = _tile(t, 256)
    nt = t // tt
    ng = len(gather)

    def body(x_ref, xh_ref, w_ref, *rest):
        g_in, (o_ref, c_ref), g_out, (ext, *sems) = rest[:ng], rest[ng:ng + 2], rest[ng + 2:2 * ng + 2], rest[2 * ng + 2:]
        b, i = pl.program_id(0), pl.program_id(1)
        _ride_along(_Exchange(False, g_in, g_out, *sems) if ng else None,
                    (b == 0) & (i == 0), (b == bsz - 1) & (i == nt - 1))
        ext[0:B_H, :] = jnp.where(i > 0, xh_ref[...], 0.0)
        ext[B_H:, :] = x_ref[...]
        _conv_rows(c_ref, lambda off, rows, c0, cols: ext[pl.ds(off, rows), c0:c0 + cols], w_ref,
                   _fwd_offsets(B_K, B_H), tt, B_RB, B_CB)
        outs = _b_post([c_ref[:, LANE * j:LANE * (j + 1)] for j in range(3 * NH)])
        for j, o in enumerate(outs):
            o_ref[:, LANE * j:LANE * (j + 1)] = o

    hbm = pl.BlockSpec(memory_space=pltpu.HBM)
    return pl.pallas_call(
        body, name="bprep_fwd_gather" if ng else "bprep_fwd", grid=(bsz, nt),
        in_specs=[pl.BlockSpec((tt, 3072), lambda b, i: (b * nt + i, 0)),
                  pl.BlockSpec((B_H, 3072), lambda b, i: (jnp.maximum((b * t + i * tt) // B_H - 1, 0), 0)),
                  _full((8, 3072))] + [hbm] * ng,
        out_specs=[pl.BlockSpec((tt, 3072), lambda b, i: (b * nt + i, 0))] * 2 + [hbm] * ng,
        out_shape=[SDS((n, 3072), F32)] * 2 + [SDS((NDEV,) + a.shape, a.dtype) for a in gather],
        scratch_shapes=[pltpu.VMEM((tt + B_H, 3072), F32)] + (_exchange_sems(ng) if ng else []),
        compiler_params=_cp(56),
    )(proj, proj, wconv, *gather)


def _bprep_bwd(dproj, proj, conv, dqkvn, wconv, bsz, t):
    n = bsz * t
    tt = _tile(t, 256)
    nt = t // tt

    def body(dp_any, x_ref, xh_ref, c_ref, dq_ref, w_ref, dp_ref, gw_ref, ext, dcp, dae, carry):
        b, i = pl.program_id(0), pl.program_id(1)
        ti = nt - 1 - i

        @pl.when((b == 0) & (i == 0))
        def _():
            gw_ref[...] = jnp.zeros_like(gw_ref)

        @pl.when(i == 0)
        def _():
            carry[...] = jnp.zeros_like(carry)

        ext[0:B_H, :] = jnp.where(ti > 0, xh_ref[...], 0.0)
        ext[B_H:, :] = x_ref[...]
        _, vjp = jax.vjp(_b_post, [c_ref[:, LANE * j:LANE * (j + 1)] for j in range(3 * NH)])
        (dcs,) = vjp([dq_ref[:, LANE * j:LANE * (j + 1)] for j in range(3 * NH)])
        dcp[0:B_H, :] = jnp.zeros((B_H, 3072), F32)
        for j, dcj in enumerate(dcs):
            dcp[B_H:B_H + tt, LANE * j:LANE * (j + 1)] = dcj
        dcp[B_H + tt:, :] = jnp.zeros((B_H, 3072), F32)
        _conv_bwd_w(dcp[B_H:B_H + tt, :], ext, gw_ref, B_K, B_H, tt)
        _conv_rows(dae, lambda off, rows, c0, cols: dcp[pl.ds(off, rows), c0:c0 + cols], w_ref,
                   _bwd_offsets(B_K), tt + B_H, B_RB, B_CB)
        dae[tt:tt + B_H, :] += carry[...]
        carry[...] = dae[0:B_H, :]
        dp_ref[...] = dae[B_H:, :].astype(BF16)

    def row(b, i):
        return b * nt + (nt - 1 - i)

    return pl.pallas_call(
        body, name="bprep_bwd", grid=(bsz, nt),
        in_specs=[pl.BlockSpec(memory_space=pl.ANY),
                  pl.BlockSpec((tt, 3072), lambda b, i: (row(b, i), 0)),
                  pl.BlockSpec((B_H, 3072), lambda b, i: (jnp.maximum((b * t + (nt - 1 - i) * tt) // B_H - 1, 0), 0)),
                  pl.BlockSpec((tt, 3072), lambda b, i: (row(b, i), 0)),
                  pl.BlockSpec((tt, 3072), lambda b, i: (row(b, i), 0)),
                  _full((8, 3072))],
        out_specs=[pl.BlockSpec((tt, 3072), lambda b, i: (row(b, i), 0)), _full((8, 3072))],
        out_shape=[SDS((n, NP), BF16), SDS((8, 3072), F32)],
        input_output_aliases={0: 0},
        scratch_shapes=[pltpu.VMEM((tt + B_H, 3072), F32), pltpu.VMEM((tt + 2 * B_H, 3072), F32),
                        pltpu.VMEM((tt + B_H, 3072), F32), pltpu.VMEM((B_H, 3072), F32)],
        compiler_params=_cp(56),
    )(dproj, proj, proj, conv, dqkvn, wconv)


def _split2(a):
    hi = a.astype(BF16)
    return hi, (a - hi.astype(F32)).astype(BF16)


def _split3(a):
    p1 = a.astype(BF16)
    r1 = a - p1.astype(F32)
    p2 = r1.astype(BF16)
    return p1, p2, (r1 - p2.astype(F32)).astype(BF16)


def _dot3_raw(a, b, dims):
    a1, a2 = _split2(a)
    b1, b2 = _split2(b)
    return _mm(a1, b1, dims) + (_mm(a1, b2, dims) + _mm(a2, b1, dims))


def _dot6(a, b, dims):
    a1, a2, a3 = _split3(a)
    b1, b2, b3 = _split3(b)
    return (_mm(a1, b1, dims) + (_mm(a1, b2, dims) + _mm(a2, b1, dims))
            + (_mm(a1, b3, dims) + _mm(a2, b2, dims) + _mm(a3, b1, dims)))


def _unit_lower_inverse_raw(lmats):
    r = lax.broadcasted_iota(jnp.int32, (CH, CH), 0)
    c = lax.broadcasted_iota(jnp.int32, (CH, CH), 1)
    eye = (r == c).astype(F32)
    blk = jnp.right_shift(r, 4) == jnp.right_shift(c, 4)
    dm = [jnp.where(blk, x, 0.0) for x in lmats]
    om = [a - b for a, b in zip(lmats, dm)]
    d2 = [_dot3_raw(x, x, NN) for x in dm]
    d4 = [_dot3_raw(x, x, NN) for x in d2]
    d8 = [_dot3_raw(x, x, NN) for x in d4]
    p = [_dot3_raw(eye - a, eye + b, NN) for a, b in zip(dm, d2)]
    p = [_dot3_raw(a, eye + b, NN) for a, b in zip(p, d4)]
    p = [_dot3_raw(a, eye + b, NN) for a, b in zip(p, d8)]
    m = [_dot3_raw(a, b, NN) for a, b in zip(p, om)]
    m2 = [_dot3_raw(x, x, NN) for x in m]
    t = [_dot3_raw(eye - a, eye + b, NN) for a, b in zip(m, m2)]
    return [_dot3_raw(a, b, NN) for a, b in zip(t, p)]


@jax.custom_vjp
def _unit_lower_inverse(lmats):
    return _unit_lower_inverse_raw(lmats)


def _unit_lower_inverse_fwd(lmats):
    tinv = _unit_lower_inverse_raw(lmats)
    return tinv, tinv


def _unit_lower_inverse_bwd(tinv, gs):
    x = [_dot6(t, g, TN) for t, g in zip(tinv, gs)]
    return ([-_dot6(a, t, NT) for a, t in zip(x, tinv)],)


_unit_lower_inverse.defvjp(_unit_lower_inverse_fwd, _unit_lower_inverse_bwd)


@jax.custom_vjp
def _saved_unit_lower_inverse(lmats, tinv):
    del lmats
    return tinv


def _saved_unit_lower_inverse_fwd(lmats, tinv):
    del lmats
    return tinv, tinv


def _saved_unit_lower_inverse_bwd(tinv, gs):
    (dl,) = _unit_lower_inverse_bwd(tinv, gs)
    return dl, [jnp.zeros_like(t) for t in tinv]


_saved_unit_lower_inverse.defvjp(_saved_unit_lower_inverse_fwd, _saved_unit_lower_inverse_bwd)


def _tri(lower):
    r = lax.broadcasted_iota(jnp.int32, (CH, CH), 0)
    c = lax.broadcasted_iota(jnp.int32, (CH, CH), 1)
    return ((r >= c) if lower else (r <= c)).astype(BF16)


def _tri_dot(x, lower, dims, tri_first):
    p1, p2, p3 = _split3(x)
    tri = _tri(lower)
    if tri_first:
        return _mm(tri, p1, dims) + (_mm(tri, p2, dims) + _mm(tri, p3, dims))
    return _mm(p1, tri, dims) + (_mm(p2, tri, dims) + _mm(p3, tri, dims))


@jax.custom_vjp
def _cumsum_rows(x):
    return _tri_dot(x, True, NN, True)


def _cumsum_rows_fwd(x):
    return _tri_dot(x, True, NN, True), None


def _cumsum_rows_bwd(_, g):
    return (_tri_dot(g, True, TN, True),)


_cumsum_rows.defvjp(_cumsum_rows_fwd, _cumsum_rows_bwd)


@jax.custom_vjp
def _cumsum_rows_t(x):
    return _tri_dot(x, False, TN, False)


def _cumsum_rows_t_fwd(x):
    return _tri_dot(x, False, TN, False), None


def _cumsum_rows_t_bwd(_, g):
    return (_tri_dot(g, False, NT, True),)


_cumsum_rows_t.defvjp(_cumsum_rows_t_fwd, _cumsum_rows_t_bwd)


def _delta_chunk(ss, qs, ks, vs, bas, zs, alog, dtb, og, tsaved=None):
    heads = range(len(ss))
    lane = lax.broadcasted_iota(jnp.int32, (1, LANE), 1)
    r = lax.broadcasted_iota(jnp.int32, (CH, CH), 0)
    c = lax.broadcasted_iota(jnp.int32, (CH, CH), 1)
    ri = lax.broadcasted_iota(jnp.int32, (CH, 1), 0)
    incl, strict = r >= c, r > c

    def pick(x, h):
        return jnp.sum(jnp.where(lane == h, x, 0.0), axis=-1, keepdims=True)

    beta = [jax.nn.sigmoid(pick(bas[e // NH], e % NH)) for e in heads]
    g = [-jnp.exp(pick(alog, e % NH + NH)) * _softplus(pick(bas[e // NH], e % NH + NH) + pick(dtb, e % NH + NH))
         for e in heads]
    gb = [jnp.broadcast_to(x, (CH, CH)) for x in g]
    gca = [_cumsum_rows(x) for x in gb]
    gcr = [_cumsum_rows_t(x) for x in gb]
    gc = [jnp.sum(jnp.where(c == 0, x, 0.0), axis=-1, keepdims=True) for x in gca]
    gl = [jnp.sum(jnp.where(ri == CH - 1, x, 0.0), axis=0, keepdims=True) for x in gc]
    diff = [a - b for a, b in zip(gca, gcr)]
    gam_s = [jnp.where(strict, jnp.exp(jnp.where(strict, x, 0.0)), 0.0) for x in diff]
    gam_i = [jnp.where(incl, jnp.exp(jnp.where(incl, x, 0.0)), 0.0) for x in diff]

    kk = [_mm(k, k, NT) for k in ks]
    lmats = [beta[h] * kk[h] * gam_s[h] for h in heads]
    tinv = _unit_lower_inverse(lmats) if tsaved is None else _saved_unit_lower_inverse(lmats, tsaved)

    eg = [jnp.exp(x) for x in gc]
    u = [_mm(tinv[h], vs[h] * beta[h], NN) for h in heads]
    w = [_mm(tinv[h], ks[h] * (beta[h] * eg[h]), NN) for h in heads]
    qk = [_mm(qs[h], ks[h], NT) * gam_i[h] for h in heads]
    vn = [u[h] - _mm(w[h], ss[h], NN) for h in heads]
    o = [_mm(qs[h] * eg[h], ss[h], NN) + _mm(qk[h], vn[h], NN) for h in heads]
    sn = [jnp.exp(gl[h]) * ss[h] + _mm(ks[h] * jnp.exp(gl[h] - gc[h]), vn[h], TN) for h in heads]
    y = [_rms(o[h], og) * jax.nn.silu(zs[h]) for h in heads]
    return (sn, y), tinv


def _chain_blocks(ref):
    return [ref[b, :, LANE * h:LANE * (h + 1)] for b in range(ref.shape[0]) for h in range(NH)]


def _ride_along(ex, first, last):
    if ex is None:
        return

    @pl.when(first)
    def _():
        ex.start()

    @pl.when(last)
    def _():
        ex.wait()


def _delta_fwd(qkvn, proj, alog, dtb, og, bsz, t, gather=()):
    n = bsz * t
    nc = t // CH
    ng = len(gather)

    def body(q_ref, k_ref, v_ref, ba_ref, z_ref, al_ref, dt_ref, og_ref, *rest):
        g_in, (y_ref, sh_ref, ti_ref) = rest[:ng], rest[ng:ng + 3]
        g_out, (s_scr, *sems) = rest[ng + 3:2 * ng + 3], rest[2 * ng + 3:]
        ci = pl.program_id(0)
        _ride_along(_Exchange(False, g_in, g_out, *sems) if ng else None, ci == 0, ci == nc - 1)

        @pl.when(ci == 0)
        def _():
            s_scr[...] = jnp.zeros_like(s_scr)

        chains = range(bsz * NH)
        ss = [s_scr[e] for e in chains]
        for e in chains:
            sh_ref[e // NH, e % NH] = ss[e]
        (sn, y), tinv = _delta_chunk(ss, _chain_blocks(q_ref), _chain_blocks(k_ref), _chain_blocks(v_ref),
                                     [ba_ref[b] for b in range(bsz)], _chain_blocks(z_ref),
                                     al_ref[...], dt_ref[...], og_ref[...])
        for e in chains:
            s_scr[e] = sn[e]
            ti_ref[e // NH, e % NH] = tinv[e]
            y_ref[e // NH, :, LANE * (e % NH):LANE * (e % NH + 1)] = y[e]

    def blk(width, col):
        return pl.BlockSpec((bsz, CH, width), lambda ci: (0, ci, col))

    def per_chunk(rows):
        return pl.BlockSpec((bsz, None, NH, rows, rows), lambda ci: (0, ci, 0, 0, 0))

    hbm = pl.BlockSpec(memory_space=pltpu.HBM)
    qkv3, proj3 = qkvn.reshape(bsz, t, 3 * D), proj.reshape(bsz, t, NP)
    y, *rest = pl.pallas_call(
        body, name="delta_fwd_gather" if ng else "delta_fwd", grid=(nc,),
        in_specs=[blk(D, 0), blk(D, 1), blk(D, 2), blk(LANE, C_BA // LANE), blk(D, C_BZ // D),
                  _full((1, LANE)), _full((1, LANE)), _full((1, LANE))] + [hbm] * ng,
        out_specs=[blk(D, 0), per_chunk(LANE), per_chunk(CH)] + [hbm] * ng,
        out_shape=[SDS((bsz, t, D), F32), SDS((bsz, nc, NH, LANE, LANE), F32), SDS((bsz, nc, NH, CH, CH), F32)]
        + [SDS((NDEV,) + a.shape, a.dtype) for a in gather],
        scratch_shapes=[pltpu.VMEM((bsz * NH, LANE, LANE), F32)] + (_exchange_sems(ng) if ng else []),
        compiler_params=_cp(48),
    )(qkv3, qkv3, qkv3, proj3, proj3, alog, dtb, og, *gather)
    return (y.reshape(n, D), *rest)


def _delta_bwd(dproj, qkvn, proj, shist, tsaved, dyb, alog, dtb, og, bsz, t, scatter=()):
    n = bsz * t
    nc = t // CH
    ns = len(scatter)
    eg = 1

    def body(dp_any, q_ref, k_ref, v_ref, ba_ref, z_ref, sh_ref, ti_ref, dy_ref, al_ref, dt_ref, og_ref, *rest):
        s_in, (dp_ref, dqkv_ref, gal_ref, gdt_ref, gog_ref) = rest[:ns], rest[ns:ns + 5]
        s_out, (ds_scr, *sems) = rest[ns + 5:2 * ns + 5], rest[2 * ns + 5:]
        bi, ci = pl.program_id(0), pl.program_id(1)
        _ride_along(_Exchange(True, s_in, s_out, *sems) if ns else None,
                    (bi == 0) & (ci == 0), (bi == bsz // eg - 1) & (ci == nc - 1))

        @pl.when((bi == 0) & (ci == 0))
        def _():
            gal_ref[...] = jnp.zeros_like(gal_ref)
            gdt_ref[...] = jnp.zeros_like(gdt_ref)
            gog_ref[...] = jnp.zeros_like(gog_ref)

        @pl.when(ci == 0)
        def _():
            ds_scr[...] = jnp.zeros_like(ds_scr)

        chains = range(eg * NH)
        _, vjp, _ = jax.vjp(_delta_chunk, [sh_ref[e // NH, e % NH] for e in chains], _chain_blocks(q_ref),
                            _chain_blocks(k_ref), _chain_blocks(v_ref), [ba_ref[b] for b in range(eg)],
                            _chain_blocks(z_ref), al_ref[...], dt_ref[...], og_ref[...],
                            [ti_ref[e // NH, e % NH] for e in chains], has_aux=True)
        ds, dq, dk, dv, dba, dz, dal, ddt, dog, _ = vjp(([ds_scr[e] for e in chains], _chain_blocks(dy_ref)))
        gal_ref[...] += dal
        gdt_ref[...] += ddt
        gog_ref[...] += dog
        for b in range(eg):
            dp_ref[b, :, D:D + LANE] = dba[b].astype(BF16)
        for e in chains:
            b, lo = e // NH, LANE * (e % NH)
            ds_scr[e] = ds[e]
            dp_ref[b, :, lo:lo + LANE] = dz[e].astype(BF16)
            dqkv_ref[b, :, lo:lo + LANE] = dq[e]
            dqkv_ref[b, :, D + lo:D + lo + LANE] = dk[e]
            dqkv_ref[b, :, 2 * D + lo:2 * D + lo + LANE] = dv[e]

    def blk(width, col):
        return pl.BlockSpec((eg, CH, width), lambda bi, ci: (bi, nc - 1 - ci, col))

    def per_chunk(rows):
        return pl.BlockSpec((eg, None, NH, rows, rows), lambda bi, ci: (bi, nc - 1 - ci, 0, 0, 0))

    hbm = pl.BlockSpec(memory_space=pltpu.HBM)
    qkv3, proj3 = qkvn.reshape(bsz, t, 3 * D), proj.reshape(bsz, t, NP)
    dproj, dqkvn, *rest = pl.pallas_call(
        body, name="delta_bwd_exchange" if ns else "delta_bwd", grid=(bsz // eg, nc),
        in_specs=[pl.BlockSpec(memory_space=pl.ANY), blk(D, 0), blk(D, 1), blk(D, 2), blk(LANE, C_BA // LANE),
                  blk(D, C_BZ // D), per_chunk(LANE), per_chunk(CH),
                  blk(D, 0), _full((1, LANE)), _full((1, LANE)), _full((1, LANE))] + [hbm] * ns,
        out_specs=[blk(D + LANE, C_BZ // (D + LANE)), blk(3 * D, 0), _full((1, LANE)), _full((1, LANE)),
                   _full((1, LANE))] + [hbm] * ns,
        out_shape=[SDS((bsz, t, NP), BF16), SDS((bsz, t, 3 * D), F32), SDS((1, LANE), F32), SDS((1, LANE), F32),
                   SDS((1, LANE), F32)] + [SDS(a.shape, a.dtype) for a in scatter],
        input_output_aliases={0: 0},
        scratch_shapes=[pltpu.VMEM((eg * NH, LANE, LANE), F32)] + (_exchange_sems(ns) if ns else []),
        compiler_params=_cp(56),
    )(dproj.reshape(bsz, t, NP), qkv3, qkv3, qkv3, proj3, proj3, shist, tsaved, dyb.reshape(bsz, t, D), alog, dtb, og,
      *scatter)
    return (dproj.reshape(n, NP), dqkvn.reshape(n, 3 * D), *rest)


def _c_chunk(us, vs, zs, lgs, lbs, ws, bsb):
    gv = [jax.nn.gelu(v) for v in vs]
    width = LANE * len(gv)
    mu = sum(jnp.sum(x, axis=-1, keepdims=True) for x in gv) / width
    var = sum(jnp.sum(jnp.square(x - mu), axis=-1, keepdims=True) for x in gv) / width
    rstd = lax.rsqrt(var + EPS)
    r = lax.broadcasted_iota(jnp.int32, (SG, SG), 0)
    c = lax.broadcasted_iota(jnp.int32, (SG, SG), 1)
    out = []
    for j in range(len(gv)):
        nrm = (gv[j] - mu) * rstd * lgs[j] + lbs[j]
        mixed = jnp.dot(jnp.where(r >= c, ws[j], 0.0), nrm, preferred_element_type=F32) + bsb[j]
        out.append(jax.nn.gelu(us[j]) * mixed * jax.nn.silu(zs[j]))
    return out


def _c_args(u_ref, v_ref, z_ref, lg_ref, lb_ref, ws_ref, bs_ref):
    sl = [slice(LANE * j, LANE * (j + 1)) for j in range(4)]
    return ([u_ref[:, s] for s in sl], [v_ref[:, s] for s in sl], [z_ref[:, s] for s in sl],
            [lg_ref[:, s] for s in sl], [lb_ref[:, s] for s in sl],
            [ws_ref[j] for j in range(4)], [bs_ref[j] for j in range(4)])


def _c_fwd(proj, lg, lb, ws, bsb, n):
    cb = C_C // 512

    def body(u_ref, v_ref, z_ref, lg_ref, lb_ref, ws_ref, bs_ref, y_ref):
        outs = _c_chunk(*_c_args(u_ref, v_ref, z_ref, lg_ref, lb_ref, ws_ref, bs_ref))
        for j, o in enumerate(outs):
            y_ref[:, LANE * j:LANE * (j + 1)] = o

    return pl.pallas_call(
        body, name="c_fwd", grid=(n // SG,),
        in_specs=[pl.BlockSpec((SG, 512), lambda i: (i, cb)), pl.BlockSpec((SG, 512), lambda i: (i, cb + 1)),
                  pl.BlockSpec((SG, 512), lambda i: (i, cb + 2)), _full((1, 512)), _full((1, 512)),
                  _full((4, SG, SG)), _full((4, SG, SG))],
        out_specs=pl.BlockSpec((SG, 512), lambda i: (i, 0)),
        out_shape=SDS((n, 512), F32), compiler_params=_cp(32),
    )(proj, proj, proj, lg, lb, ws, bsb)


def _c_bwd(dproj, proj, dy, lg, lb, ws, bsb, n):
    cb = C_C // 512

    def body(dp_any, u_ref, v_ref, z_ref, dy_ref, lg_ref, lb_ref, ws_ref, bs_ref,
             dp_ref, glg_ref, glb_ref, gws_ref, gbs_ref):
        @pl.when(pl.program_id(0) == 0)
        def _():
            glg_ref[...] = jnp.zeros_like(glg_ref)
            glb_ref[...] = jnp.zeros_like(glb_ref)
            gws_ref[...] = jnp.zeros_like(gws_ref)
            gbs_ref[...] = jnp.zeros_like(gbs_ref)

        _, vjp = jax.vjp(_c_chunk, *_c_args(u_ref, v_ref, z_ref, lg_ref, lb_ref, ws_ref, bs_ref))
        dus, dvs, dzs, dlgs, dlbs, dwss, dbss = vjp([dy_ref[:, LANE * j:LANE * (j + 1)] for j in range(4)])
        for j in range(4):
            sl = slice(LANE * j, LANE * (j + 1))
            dp_ref[:, LANE * j:LANE * (j + 1)] = dus[j].astype(BF16)
            dp_ref[:, 512 + LANE * j:512 + LANE * (j + 1)] = dvs[j].astype(BF16)
            dp_ref[:, 1024 + LANE * j:1024 + LANE * (j + 1)] = dzs[j].astype(BF16)
            glg_ref[:, sl] += dlgs[j]
            glb_ref[:, sl] += dlbs[j]
            gws_ref[j] += dwss[j]
            gbs_ref[j] += jnp.broadcast_to(jnp.sum(dbss[j], axis=-1, keepdims=True), (SG, SG))

    return pl.pallas_call(
        body, name="c_bwd", grid=(n // SG,),
        in_specs=[pl.BlockSpec(memory_space=pl.ANY),
                  pl.BlockSpec((SG, 512), lambda i: (i, cb)), pl.BlockSpec((SG, 512), lambda i: (i, cb + 1)),
                  pl.BlockSpec((SG, 512), lambda i: (i, cb + 2)), pl.BlockSpec((SG, 512), lambda i: (i, 0)),
                  _full((1, 512)), _full((1, 512)), _full((4, SG, SG)), _full((4, SG, SG))],
        out_specs=[pl.BlockSpec((SG, 1536), lambda i: (i, C_C // 1536)),
                   _full((1, 512)), _full((1, 512)), _full((4, SG, SG)), _full((4, SG, SG))],
        out_shape=[SDS((n, NP), BF16), SDS((1, 512), F32), SDS((1, 512), F32), SDS((4, SG, SG), F32), SDS((4, SG, SG), F32)],
        input_output_aliases={0: 0}, compiler_params=_cp(32),
    )(dproj, proj, proj, proj, dy, lg, lb, ws, bsb)


def _merge_fwd(x2d, ya, yb, yc, proj, ap, bp, cp, wo):
    n = x2d.shape[0]
    tm = _tile(n, 512)
    gb = C_G // D

    def body(x_ref, ya_ref, yb_ref, yc_ref, g0_ref, g1_ref, g2_ref, ap_ref, bp_ref, cp_ref, wo_ref, o_ref):
        merged = (jax.nn.sigmoid(g0_ref[...]) * _bdot(ya_ref[...], ap_ref[...])
                  + jax.nn.sigmoid(g1_ref[...]) * _bdot(yb_ref[...], bp_ref[...])
                  + jax.nn.sigmoid(g2_ref[...]) * _bdot(yc_ref[...], cp_ref[...]))
        o_ref[...] = x_ref[...] + _bdot(merged, wo_ref[...])

    def rows(w):
        return pl.BlockSpec((tm, w), lambda i: (i, 0))

    return pl.pallas_call(
        body, name="merge_fwd", grid=(n // tm,),
        in_specs=[rows(D), rows(512), rows(D), rows(512),
                  pl.BlockSpec((tm, D), lambda i: (i, gb)), pl.BlockSpec((tm, D), lambda i: (i, gb + 1)),
                  pl.BlockSpec((tm, D), lambda i: (i, gb + 2)),
                  _resident((512, D)), _resident((D, D)), _resident((512, D)), _resident((D, D))],
        out_specs=rows(D), out_shape=SDS((n, D), F32), compiler_params=_cp(48),
    )(x2d, ya, yb, yc, proj, proj, proj, ap, bp, cp, wo)


def _merge_bwd(dxo, ya, yb, yc, proj, ap, bp, cp, apt, bpt, cpt, wot):
    n = dxo.shape[0]
    tm = _tile(n, 256)
    gb = C_G // D

    def body(d_ref, ya_ref, yb_ref, yc_ref, g0_ref, g1_ref, g2_ref, ap_ref, bp_ref, cp_ref,
             apt_ref, bpt_ref, cpt_ref, wot_ref,
             dp_ref, dya_ref, dyb_ref, dyc_ref, dpa_ref, dpb_ref, dpc_ref, mg_ref):
        dm = _bdot(d_ref[...], wot_ref[...])
        merged = None
        for j, (g_ref, y_ref, w_ref, wt_ref, dy_ref, dpj_ref) in enumerate((
                (g0_ref, ya_ref, ap_ref, apt_ref, dya_ref, dpa_ref),
                (g1_ref, yb_ref, bp_ref, bpt_ref, dyb_ref, dpb_ref),
                (g2_ref, yc_ref, cp_ref, cpt_ref, dyc_ref, dpc_ref))):
            s = jax.nn.sigmoid(g_ref[...])
            pj = _bdot(y_ref[...], w_ref[...])
            merged = s * pj if merged is None else merged + s * pj
            dp_ref[:, D * j:D * (j + 1)] = (dm * pj * s * (1.0 - s)).astype(BF16)
            dpj = (dm * s).astype(BF16)
            dpj_ref[...] = dpj
            dy_ref[...] = jnp.dot(dpj, wt_ref[...], preferred_element_type=F32)
        mg_ref[...] = merged

    def rows(w):
        return pl.BlockSpec((tm, w), lambda i: (i, 0))

    return pl.pallas_call(
        body, name="merge_bwd", grid=(n // tm,),
        in_specs=[rows(D), rows(512), rows(D), rows(512),
                  pl.BlockSpec((tm, D), lambda i: (i, gb)), pl.BlockSpec((tm, D), lambda i: (i, gb + 1)),
                  pl.BlockSpec((tm, D), lambda i: (i, gb + 2)),
                  _resident((512, D)), _resident((D, D)), _resident((512, D)),
                  _resident((D, 512)), _resident((D, D)), _resident((D, 512)), _resident((D, D))],
        out_specs=[pl.BlockSpec((tm, 3 * D), lambda i: (i, C_G // (3 * D))), rows(512), rows(D), rows(512),
                   rows(D), rows(D), rows(D), rows(D)],
        out_shape=[SDS((n, NP), BF16), SDS((n, 512), F32), SDS((n, D), F32), SDS((n, 512), F32),
                   SDS((n, D), BF16), SDS((n, D), BF16), SDS((n, D), BF16), SDS((n, D), F32)],
        compiler_params=_cp(56),
    )(dxo, ya, yb, yc, proj, proj, proj, ap, bp, cp, apt, bpt, cpt, wot)


def _sum_parts(p_ref):
    g = p_ref[0].astype(F32)
    for s in range(1, NDEV):
        g = g + p_ref[s].astype(F32)
    return g


def _adamw(g, w, m, v):
    nm = ADAM_B1 * m + (1.0 - ADAM_B1) * g
    nv = ADAM_B2 * v + (1.0 - ADAM_B2) * jnp.square(g)
    nm_hat = nm / (1.0 - ADAM_B1 ** ADAM_STEP)
    nv_hat = nv / (1.0 - ADAM_B2 ** ADAM_STEP)
    return -ADAM_LR * (nm_hat / (jnp.sqrt(nv_hat) + ADAM_EPS) + ADAM_WD * w), nm, nv


def _reduce_adamw(parts, w, m, v, name):
    r, c = w.shape
    tr = _tile(r, 128)

    def body(p_ref, w_ref, m_ref, v_ref, g_ref, d_ref, nm_ref, nv_ref):
        g = _sum_parts(p_ref)
        g_ref[...] = g
        d_ref[...], nm_ref[...], nv_ref[...] = _adamw(g, w_ref[...], m_ref[...], v_ref[...])

    blk = pl.BlockSpec((tr, c), lambda i: (i, 0))
    return pl.pallas_call(
        body, name=name, grid=(r // tr,),
        in_specs=[pl.BlockSpec((NDEV, tr, c), lambda i: (0, i, 0)), blk, blk, blk],
        out_specs=[blk, blk, blk, blk], out_shape=[SDS((r, c), F32)] * 4, compiler_params=_cp(48),
    )(parts, w, m, v)


def _reduce_adamw_leaves(parts, ws, ms, vs, name):
    nleaf = len(ws)
    counts = [len(p) if isinstance(p, (list, tuple)) else 0 for p in parts]
    flat = [a for p in parts for a in (p if isinstance(p, (list, tuple)) else [p])]

    def body(*refs):
        p_refs, rest = refs[:len(flat)], refs[len(flat):]
        w_refs, m_refs, v_refs = rest[:nleaf], rest[nleaf:2 * nleaf], rest[2 * nleaf:3 * nleaf]
        outs = rest[3 * nleaf:]
        at = 0
        for i in range(nleaf):
            g_ref, d_ref, nm_ref, nv_ref = outs[i], outs[nleaf + i], outs[2 * nleaf + i], outs[3 * nleaf + i]
            for idx in (range(counts[i]) if counts[i] else [Ellipsis]):
                g = _sum_parts(p_refs[at])
                at += 1
                g_ref[idx] = g
                d_ref[idx], nm_ref[idx], nv_ref[idx] = _adamw(g, w_refs[i][idx], m_refs[i][idx], v_refs[i][idx])

    vm = pl.BlockSpec(memory_space=pltpu.VMEM)
    outs = pl.pallas_call(
        body, name=name, in_specs=[vm] * (len(flat) + 3 * nleaf), out_specs=[vm] * (4 * nleaf),
        out_shape=[SDS(w.shape, F32) for w in ws] * 4, compiler_params=_cp(56),
    )(*flat, *ws, *ms, *vs)
    return [outs[j * nleaf:(j + 1) * nleaf] for j in range(4)]


def _unshard(name, g):
    if name in ROW_SHARDED:
        return g.reshape(g.shape[0] * g.shape[1], g.shape[2])
    g = jnp.moveaxis(g, 0, 1)
    return g.reshape(g.shape[0], g.shape[1] * g.shape[2])


def _reshard(name, full):
    r, c = full.shape
    if name in ROW_SHARDED:
        return full.reshape(NDEV, r // NDEV, c)
    return jnp.moveaxis(full.reshape(r, NDEV, c // NDEV), 1, 0)


def _w_in_to_padded(slabs):
    pieces = []
    for lo, hi, _ in sorted(SEGMENTS, key=lambda s: s[2]):
        for d in range(NDEV):
            a, b = max(lo, d * W_SHARD), min(hi, (d + 1) * W_SHARD)
            if a < b:
                pieces.append(slabs[d, :, a - d * W_SHARD:b - d * W_SHARD])
    pieces.append(jnp.zeros(slabs.shape[1:2] + (NP - C_BA - 16,), slabs.dtype))
    return jnp.concatenate(pieces, axis=-1)


def _w_in_from_padded(g):
    slabs = []
    for d in range(NDEV):
        pieces = []
        for lo, hi, pstart in SEGMENTS:
            a, b = max(lo, d * W_SHARD), min(hi, (d + 1) * W_SHARD)
            if a < b:
                pieces.append(g[:, pstart + a - lo:pstart + b - lo])
        pieces.append(jnp.zeros(g.shape[:1] + (W_SHARD_PAD - W_SHARD,), g.dtype))
        slabs.append(jnp.concatenate(pieces, axis=-1))
    return jnp.stack(slabs)


def _pad_w_in(w):
    return jnp.pad(w, ((0, 0), (0, W_SHARD_PAD - W_SHARD)))


def _lane_row(vec8, offset):
    return jnp.pad(vec8, (offset, LANE - NH - offset))[None]


def kernel(x, norm_g, w_in, a_dw, a_dw_b, a_ln_g, a_ln_b, a_proj, b_conv, b_a_log, b_dt_bias, b_onorm_g, b_proj, c_ln_g, c_ln_b, c_ws, c_bs, c_proj, w_out, final_g, loss_target, m_norm_g, m_w_in, m_a_dw, m_a_dw_b, m_a_ln_g, m_a_ln_b, m_a_proj, m_b_conv, m_b_a_log, m_b_dt_bias, m_b_onorm_g, m_b_proj, m_c_ln_g, m_c_ln_b, m_c_ws, m_c_bs, m_c_proj, m_w_out, m_final_g, v_norm_g, v_w_in, v_a_dw, v_a_dw_b, v_a_ln_g, v_a_ln_b, v_a_proj, v_b_conv, v_b_a_log, v_b_dt_bias, v_b_onorm_g, v_b_proj, v_c_ln_g, v_c_ln_b, v_c_ws, v_c_bs, v_c_proj, v_w_out, v_final_g):
    wts = dict(norm_g=norm_g, w_in=w_in, a_dw=a_dw, a_dw_b=a_dw_b, a_ln_g=a_ln_g, a_ln_b=a_ln_b, a_proj=a_proj,
               b_conv=b_conv, b_a_log=b_a_log, b_dt_bias=b_dt_bias, b_onorm_g=b_onorm_g, b_proj=b_proj,
               c_ln_g=c_ln_g, c_ln_b=c_ln_b, c_ws=c_ws, c_bs=c_bs, c_proj=c_proj, w_out=w_out, final_g=final_g)
    mom = dict(norm_g=m_norm_g, w_in=m_w_in, a_dw=m_a_dw, a_dw_b=m_a_dw_b, a_ln_g=m_a_ln_g, a_ln_b=m_a_ln_b,
               a_proj=m_a_proj, b_conv=m_b_conv, b_a_log=m_b_a_log, b_dt_bias=m_b_dt_bias, b_onorm_g=m_b_onorm_g,
               b_proj=m_b_proj, c_ln_g=m_c_ln_g, c_ln_b=m_c_ln_b, c_ws=m_c_ws, c_bs=m_c_bs, c_proj=m_c_proj,
               w_out=m_w_out, final_g=m_final_g)
    vel = dict(norm_g=v_norm_g, w_in=v_w_in, a_dw=v_a_dw, a_dw_b=v_a_dw_b, a_ln_g=v_a_ln_g, a_ln_b=v_a_ln_b,
               a_proj=v_a_proj, b_conv=v_b_conv, b_a_log=v_b_a_log, b_dt_bias=v_b_dt_bias, b_onorm_g=v_b_onorm_g,
               b_proj=v_b_proj, c_ln_g=v_c_ln_g, c_ln_b=v_c_ln_b, c_ws=v_c_ws, c_bs=v_c_bs, c_proj=v_c_proj,
               w_out=v_w_out, final_g=v_final_g)

    bsz, t, _ = x.shape
    n = bsz * t
    depth = norm_g.shape[0]
    x2d = x.reshape(n, D)
    tgt = loss_target.reshape(n, D)

    def weight_blocks(l):
        return [_pad_w_in(w_in[l].astype(BF16))] + [wts[k][l].astype(BF16) for k in BIG_REST]

    def matmul_weights(w_in_all, *rest):
        got = {k: _unshard(k, g) for k, g in zip(BIG_REST, rest)}
        got['wp'] = _w_in_to_padded(w_in_all)
        return got

    w_in_first, *conv_all = _all_gather(weight_blocks(0)[:1] + [wts[k] for k in SMALL], "gather_first_weights")
    conv_w = {k: jnp.stack([_unshard(k, g[:, l]) for l in range(depth)]) for k, g in zip(SMALL, conv_all)}
    a_dw32 = jnp.pad(conv_w['a_dw'], ((0, 0), (0, 32 - A_K), (0, 0)))
    b_conv8 = jnp.pad(conv_w['b_conv'], ((0, 0), (0, 8 - B_K), (0, 0)))
    bsb = jnp.broadcast_to(c_bs[..., None], c_bs.shape + (SG,))
    full = [{'wp': _w_in_to_padded(w_in_first)}]

    saved = []
    xl = x2d
    for l in range(depth):
        alog, dtb = _lane_row(b_a_log[l], NH), _lane_row(b_dt_bias[l], NH)
        proj, h = _inproj(xl, norm_g[l][None], full[l]['wp'])
        ya, conv_a, *own = _a_fwd(proj, a_dw32[l], a_dw_b[l][None], a_ln_g[l][None], a_ln_b[l][None], bsz, t,
                                  gather=weight_blocks(0)[1:] if l == 0 else ())
        if own:
            full[0].update({k: _unshard(k, g) for k, g in zip(BIG_REST, own)})
        ahead = weight_blocks(l + 1) if l + 1 < depth else []
        qkvn, conv_b, *nxt_small = _bprep_fwd(proj, b_conv8[l], bsz, t, gather=ahead[1:])
        yb, shist, tsave, *nxt_w_in = _delta_fwd(qkvn, proj, alog, dtb, b_onorm_g[l][None], bsz, t, gather=ahead[:1])
        if ahead:
            full.append(matmul_weights(*nxt_w_in, *nxt_small))
        yc = _c_fwd(proj, c_ln_g[l][None], c_ln_b[l][None], c_ws[l], bsb[l], n)
        xn = _merge_fwd(xl, ya, yb, yc, proj, full[l]['a_proj'], full[l]['b_proj'], full[l]['c_proj'], full[l]['w_out'])
        saved.append((xl, proj, h, ya, yb, yc, qkvn, shist, tsave, alog, dtb, conv_a, conv_b))
        xl = xn

    dx, g_final, loss_blk = _loss_head(xl, final_g[None], tgt)
    loss = lax.psum(loss_blk[0, 0], ("x", "y", "c"))

    gfull = {k: [None] * depth for k in WEIGHTS if k != 'final_g'}
    recv = [{} for _ in range(depth)]
    sharded = ['w_in'] + SHARDED_REST
    late = ['w_in', 'b_conv']
    early = [k for k in sharded if k not in late]

    def grad_slabs(l, names):
        return [(_w_in_from_padded(gfull[k][l]) if k == 'w_in' else _reshard(k, gfull[k][l])).astype(BF16) for k in names]

    for l in reversed(range(depth)):
        xl, proj, h, ya, yb, yc, qkvn, shist, tsave, alog, dtb, conv_a, conv_b = saved[l]
        ap, bp, cp, wo = full[l]['a_proj'], full[l]['b_proj'], full[l]['c_proj'], full[l]['w_out']
        dproj, dya, dyb, dyc, dpa, dpb, dpc, merged = _merge_bwd(dx, ya, yb, yc, proj, ap, bp, cp, ap.T, bp.T, cp.T, wo.T)
        gfull['a_proj'][l] = _mm_tn(ya, dpa, "grad_a_proj")
        gfull['b_proj'][l] = _mm_tn(yb, dpb, "grad_b_proj")
        gfull['c_proj'][l] = _mm_tn(yc, dpc, "grad_c_proj")
        gfull['w_out'][l] = _mm_tn(merged, dx, "grad_w_out")
        dproj, g_clg, g_clb, g_cws, g_cbs = _c_bwd(dproj, proj, dyc, c_ln_g[l][None], c_ln_b[l][None], c_ws[l], bsb[l], n)
        dproj, g_adw, g_adb, g_alg, g_alb = _a_bwd(dproj, proj, conv_a, dya, a_dw32[l], a_ln_g[l][None], a_ln_b[l][None],
                                                   bsz, t)
        gfull['a_dw'][l] = g_adw[:A_K]
        riders = ([(l + 1, sharded)] if l + 1 < depth else []) + ([(0, early)] if l == 0 else [])
        dproj, dqkvn, g_alog, g_dt, g_og, *got = _delta_bwd(
            dproj, qkvn, proj, shist, tsave, dyb, alog, dtb, b_onorm_g[l][None], bsz, t,
            scatter=[s for ll, names in riders for s in grad_slabs(ll, names)])
        for ll, names in riders:
            for k in names:
                recv[ll][k] = got.pop(0)
        dproj, g_bconv = _bprep_bwd(dproj, proj, conv_b, dqkvn, b_conv8[l], bsz, t)
        gfull['w_in'][l] = _mm_tn(h, dproj, "grad_w_in")
        gfull['b_conv'][l] = g_bconv[:B_K]
        dx, g_ng, *got = _inproj_bwd(dproj, full[l]['wp'], xl, norm_g[l][None], dx,
                                     scatter=grad_slabs(0, late) if l == 0 else ())
        if got:
            recv[0].update(zip(late, got))
        gfull['norm_g'][l] = g_ng[0]
        gfull['a_dw_b'][l], gfull['a_ln_g'][l], gfull['a_ln_b'][l] = g_adb[0], g_alg[0], g_alb[0]
        gfull['b_a_log'][l], gfull['b_dt_bias'][l] = g_alog[0, NH:2 * NH], g_dt[0, NH:2 * NH]
        gfull['b_onorm_g'][l] = g_og[0]
        gfull['c_ln_g'][l], gfull['c_ln_b'][l] = g_clg[0], g_clb[0]
        gfull['c_ws'][l], gfull['c_bs'][l] = g_cws, g_cbs[:, :, 0]
    grad_x = dx.reshape(bsz, t, D)

    outs_w = [_reduce_adamw(recv[l]['w_in'], _pad_w_in(w_in[l]), _pad_w_in(m_w_in[l]), _pad_w_in(v_w_in[l]), "adamw_w_in")
              for l in range(depth)]
    outs_s = _reduce_adamw_leaves([[recv[l][k] for l in range(depth)] for k in SHARDED_REST],
                                  [wts[k] for k in SHARDED_REST], [mom[k] for k in SHARDED_REST],
                                  [vel[k] for k in SHARDED_REST], "adamw_sharded")

    def upto3d(a):
        return a[None] if a.ndim == 1 else a.reshape((-1,) + a.shape[-2:]) if a.ndim > 3 else a

    grepl = [upto3d(jnp.stack(gfull[k]) if k != 'final_g' else g_final[0]) for k in REPL]
    outs_r = _reduce_adamw_leaves(_all_gather(grepl, "gather_replicated_grads"), [upto3d(wts[k]) for k in REPL],
                                  [upto3d(mom[k]) for k in REPL], [upto3d(vel[k]) for k in REPL], "adamw_replicated")

    res = []
    for j in range(4):
        leaves = {k: outs_r[j][i].reshape(wts[k].shape) for i, k in enumerate(REPL)}
        leaves.update({k: outs_s[j][i] for i, k in enumerate(SHARDED_REST)})
        leaves['w_in'] = jnp.stack([outs_w[l][j][:, :W_SHARD] for l in range(depth)])
        res.append([leaves[k] for k in WEIGHTS])
    grads, deltas, new_m, new_v = res
    return (loss, grad_x, *grads, *deltas, *new_m, *new_v)
```

```python
import jax
import jax.numpy as jnp
from jax import lax
from jax.experimental import pallas as pl
from jax.experimental.pallas import tpu as pltpu

F32 = jnp.float32
BF16 = jnp.bfloat16
SDS = jax.ShapeDtypeStruct
MESH = pl.DeviceIdType.MESH

NDEV = 8
D = 1024
EPS = 1e-6
LANE = 128

C_Q, C_K, C_V = 0, 1024, 2048
C_G = 3072
C_A = 6144
C_C = 7680
C_BZ = 9216
C_BA = 10240
NP = 10368
N_IN = 10256
SEGMENTS = ((0, 1536, C_A), (1536, 4608, C_Q), (4608, 5632, C_BZ), (5632, 5648, C_BA), (5648, 7184, C_C), (7184, 10256, C_G))
W_SHARD = N_IN // NDEV
W_SHARD_PAD = 1408

A_K, A_H, A_RB = 31, 32, 32
B_K, B_H, B_RB, B_CB = 4, 8, 64, 512
CH = 64
SG = 128
NH = 8

ADAM_LR, ADAM_B1, ADAM_B2, ADAM_EPS, ADAM_WD, ADAM_STEP = 0.001, 0.9, 0.999, 1e-08, 0.01, 10

WEIGHTS = ['norm_g', 'w_in', 'a_dw', 'a_dw_b', 'a_ln_g', 'a_ln_b', 'a_proj', 'b_conv', 'b_a_log', 'b_dt_bias',
           'b_onorm_g', 'b_proj', 'c_ln_g', 'c_ln_b', 'c_ws', 'c_bs', 'c_proj', 'w_out', 'final_g']
SHARDED_REST = ['a_dw', 'a_proj', 'b_conv', 'b_proj', 'c_proj', 'w_out']
REPL = [n for n in WEIGHTS if n != 'w_in' and n not in SHARDED_REST]
BIG_REST = ['a_proj', 'b_proj', 'c_proj', 'w_out']
SMALL = ['a_dw', 'b_conv']
ROW_SHARDED = ('b_proj', 'w_out')

NN = ((1,), (0,))
NT = ((1,), (1,))
TN = ((0,), (0,))


def _tile(n, pref):
    return pref if (n >= pref and n % pref == 0) else n


def _cp(vmem_mb):
    return pltpu.CompilerParams(vmem_limit_bytes=vmem_mb * 2 ** 20)


def _full(shape):
    nd = len(shape)
    return pl.BlockSpec(shape, lambda *_: (0,) * nd)


def _resident(shape):
    nd = len(shape)
    return pl.BlockSpec(shape, lambda *_: (0,) * nd, pipeline_mode=pl.Buffered(1))


def _rms(x, g):
    return x * lax.rsqrt(jnp.mean(x * x, axis=-1, keepdims=True) + EPS) * g


def _softplus(x):
    return jnp.maximum(x, 0.0) + jnp.log1p(jnp.exp(-jnp.abs(x)))


def _bdot(a, b):
    return jnp.dot(a.astype(BF16), b.astype(BF16), preferred_element_type=F32)


def _mm(a, b, dims):
    return lax.dot_general(a, b, (dims, ((), ())), preferred_element_type=F32)


def _all_gather(xs, name):
    nops = len(xs)

    def body(*refs):
        x_refs, out_refs = refs[:nops], refs[nops:2 * nops]
        send_sems, recv_sems, local_sems = refs[2 * nops:]
        x, y, cc = lax.axis_index("x"), lax.axis_index("y"), lax.axis_index("c")
        me, sibling = (x, y, cc), (x, y, 1 - cc)
        chips = [(1 - x, y), (x, 1 - y), (1 - x, 1 - y)]

        def slot(t, px, py, pc):
            return out_refs[t].at[4 * px + 2 * py + pc]

        def copy(t, k, block, to, src=None):
            return pltpu.make_async_remote_copy(
                src_ref=slot(t, *block) if src is None else src, dst_ref=slot(t, *block),
                send_sem=send_sems.at[7 * t + k], recv_sem=recv_sems.at[7 * t + k], device_id=to, device_id_type=MESH)

        ops = range(nops)
        mine = [pltpu.make_async_copy(x_refs[t], slot(t, *me), local_sems.at[t]) for t in ops]
        for cp in mine:
            cp.start()
        first = [copy(t, 0, me, sibling, src=x_refs[t]) for t in ops]
        first += [copy(t, 1 + j, me, (*chip, cc), src=x_refs[t]) for j, chip in enumerate(chips) for t in ops]
        for cp in first:
            cp.start()
        passed = []
        for j, chip in enumerate(chips):
            for t in ops:
                copy(t, 1 + j, (*chip, cc), me).wait_recv()
                fwd = copy(t, 4 + j, (*chip, cc), sibling)
                fwd.start()
                passed.append(fwd)
        for t in ops:
            copy(t, 0, sibling, me).wait_recv()
        for j, chip in enumerate(chips):
            for t in ops:
                copy(t, 4 + j, (*chip, 1 - cc), me).wait_recv()
        for cp in first + passed:
            cp.wait_send()
        for cp in mine:
            cp.wait()

    hbm = pl.BlockSpec(memory_space=pltpu.HBM)
    return pl.pallas_call(
        body, name=name, out_shape=[SDS((NDEV,) + a.shape, a.dtype) for a in xs],
        in_specs=[hbm] * nops, out_specs=[hbm] * nops,
        scratch_shapes=[pltpu.SemaphoreType.DMA((7 * nops,)), pltpu.SemaphoreType.DMA((7 * nops,)),
                        pltpu.SemaphoreType.DMA((nops,))],
    )(*xs)


def _exchange_sems(nops):
    return [pltpu.SemaphoreType.DMA((7 * nops,)), pltpu.SemaphoreType.DMA((7 * nops,)), pltpu.SemaphoreType.DMA((nops,))]


class _Exchange:
    def __init__(self, scatter, in_refs, out_refs, send_sems, recv_sems, local_sems):
        x, y, cc = lax.axis_index("x"), lax.axis_index("y"), lax.axis_index("c")
        me = 4 * x + 2 * y + cc
        nops = len(in_refs)
        self.local = [pltpu.make_async_copy(in_refs[t].at[me] if scatter else in_refs[t], out_refs[t].at[me],
                                            local_sems.at[t]) for t in range(nops)]
        self.sends, self.recvs = [], []
        for k in range(1, NDEV):
            px = 1 - x if k & 4 else x
            py = 1 - y if k & 2 else y
            pc = 1 - cc if k & 1 else cc
            peer = 4 * px + 2 * py + pc
            for t in range(nops):
                sem = 7 * t + k - 1
                self.sends.append(pltpu.make_async_remote_copy(
                    src_ref=in_refs[t].at[peer] if scatter else in_refs[t], dst_ref=out_refs[t].at[me],
                    send_sem=send_sems.at[sem], recv_sem=recv_sems.at[sem], device_id=(px, py, pc), device_id_type=MESH))
                self.recvs.append(pltpu.make_async_remote_copy(
                    src_ref=in_refs[t].at[me] if scatter else in_refs[t], dst_ref=out_refs[t].at[peer],
                    send_sem=send_sems.at[sem], recv_sem=recv_sems.at[sem], device_id=(px, py, pc), device_id_type=MESH))

    def start(self):
        for cp in self.local + self.sends:
            cp.start()

    def wait(self):
        for cp in self.recvs:
            cp.wait_recv()
        for cp in self.sends:
            cp.wait_send()
        for cp in self.local:
            cp.wait()


def _inproj(x2d, g_row, wp):
    n = x2d.shape[0]
    tm, tn = _tile(n, 1024), 1152

    def body(x_ref, g_ref, w_ref, proj_ref, h_ref, hs):
        @pl.when(pl.program_id(1) == 0)
        def _():
            h = _rms(x_ref[...], g_ref[...]).astype(BF16)
            hs[...] = h
            h_ref[...] = h

        proj_ref[...] = jnp.dot(hs[...], w_ref[...], preferred_element_type=F32)

    return pl.pallas_call(
        body, name="inproj", grid=(n // tm, NP // tn),
        in_specs=[pl.BlockSpec((tm, D), lambda i, j: (i, 0)), _full((1, D)), pl.BlockSpec((D, tn), lambda i, j: (0, j))],
        out_specs=[pl.BlockSpec((tm, tn), lambda i, j: (i, j)), pl.BlockSpec((tm, D), lambda i, j: (i, 0))],
        out_shape=[SDS((n, NP), F32), SDS((n, D), BF16)],
        scratch_shapes=[pltpu.VMEM((tm, D), BF16)], compiler_params=_cp(48),
    )(x2d, g_row, wp)


def _mm_tn(a, b, name):
    nn, m = a.shape
    k = b.shape[1]
    tk = 1152 if k % 1152 == 0 else _tile(k, 1024)
    tn = _tile(nn, 1024)

    def body(a_ref, b_ref, o_ref):
        p = _mm(a_ref[...].astype(BF16), b_ref[...].astype(BF16), TN)

        @pl.when(pl.program_id(1) == 0)
        def _():
            o_ref[...] = p

        @pl.when(pl.program_id(1) > 0)
        def _():
            o_ref[...] += p

    return pl.pallas_call(
        body, name=name, grid=(k // tk, nn // tn),
        in_specs=[pl.BlockSpec((tn, m), lambda j, t: (t, 0)), pl.BlockSpec((tn, tk), lambda j, t: (t, j))],
        out_specs=pl.BlockSpec((m, tk), lambda j, t: (0, j)),
        out_shape=SDS((m, k), F32), compiler_params=_cp(48),
    )(a, b)


def _inproj_bwd(dproj, wp, x2d, g_row, dxo, scatter=()):
    n = x2d.shape[0]
    tm, tk = _tile(n, 1024), 1152
    nk = NP // tk
    ns = len(scatter)

    def body(dp_ref, w_ref, x_ref, g_ref, dxo_ref, *rest):
        s_in, (dx_ref, dg_ref), s_out, (acc, *sems) = rest[:ns], rest[ns:ns + 2], rest[ns + 2:2 * ns + 2], rest[2 * ns + 2:]
        i, k = pl.program_id(0), pl.program_id(1)
        _ride_along(_Exchange(True, s_in, s_out, *sems) if ns else None,
                    (i == 0) & (k == 0), (i == n // tm - 1) & (k == nk - 1))
        p = _mm(dp_ref[...], w_ref[...], NT)

        @pl.when(k == 0)
        def _():
            acc[...] = p

        @pl.when(k > 0)
        def _():
            acc[...] += p

        @pl.when(k == nk - 1)
        def _():
            _, vjp = jax.vjp(_rms, x_ref[...], g_ref[...])
            dx, dg = vjp(acc[...])
            dx_ref[...] = dxo_ref[...] + dx

            @pl.when(i == 0)
            def _():
                dg_ref[...] = dg

            @pl.when(i > 0)
            def _():
                dg_ref[...] += dg

    hbm = pl.BlockSpec(memory_space=pltpu.HBM)
    return pl.pallas_call(
        body, name="inproj_bwd_exchange" if ns else "inproj_bwd", grid=(n // tm, nk),
        in_specs=[pl.BlockSpec((tm, tk), lambda i, k: (i, k)), pl.BlockSpec((D, tk), lambda i, k: (0, k)),
                  pl.BlockSpec((tm, D), lambda i, k: (i, 0)), _full((1, D)),
                  pl.BlockSpec((tm, D), lambda i, k: (i, 0))] + [hbm] * ns,
        out_specs=[pl.BlockSpec((tm, D), lambda i, k: (i, 0)), _full((1, D))] + [hbm] * ns,
        out_shape=[SDS((n, D), F32), SDS((1, D), F32)] + [SDS(a.shape, a.dtype) for a in scatter],
        scratch_shapes=[pltpu.VMEM((tm, D), F32)] + (_exchange_sems(ns) if ns else []), compiler_params=_cp(56),
    )(dproj, wp, x2d, g_row, dxo, *scatter)


def _loss_head(x2d, g_row, tgt):
    n = x2d.shape[0]
    tm = _tile(n, 512)

    def body(x_ref, g_ref, t_ref, dx_ref, dg_ref, loss_ref):
        i = pl.program_id(0)
        y, vjp = jax.vjp(_rms, x_ref[...], g_ref[...])
        err = y - t_ref[...]
        part = 0.5 * jnp.sum(jnp.mean(err * err, axis=-1, keepdims=True), axis=0, keepdims=True)
        dx, dg = vjp(err * (1.0 / D))
        dx_ref[...] = dx
        lb = jnp.broadcast_to(part, (8, LANE))

        @pl.when(i == 0)
        def _():
            dg_ref[...] = dg
            loss_ref[...] = lb

        @pl.when(i > 0)
        def _():
            dg_ref[...] += dg
            loss_ref[...] += lb

    return pl.pallas_call(
        body, name="loss_head", grid=(n // tm,),
        in_specs=[pl.BlockSpec((tm, D), lambda i: (i, 0)), _full((1, D)), pl.BlockSpec((tm, D), lambda i: (i, 0))],
        out_specs=[pl.BlockSpec((tm, D), lambda i: (i, 0)), _full((1, D)), _full((8, LANE))],
        out_shape=[SDS((n, D), F32), SDS((1, D), F32), SDS((8, LANE), F32)], compiler_params=_cp(40),
    )(x2d, g_row, tgt)


def _conv_rows(dst_ref, read, w_ref, offs, nrows, rb, cb, bias=None):
    ncols = dst_ref.shape[1]
    for c0 in range(0, ncols, cb):
        for r0 in range(0, nrows, rb):
            rows = min(rb, nrows - r0)
            acc = None if bias is None else bias[:, c0:c0 + cb]
            for j, off in enumerate(offs):
                term = w_ref[j:j + 1, c0:c0 + cb] * read(off + r0, rows, c0, cb)
                acc = term if acc is None else acc + term
            dst_ref[r0:r0 + rows, c0:c0 + cb] = acc


def _fwd_offsets(kw, halo):
    return [halo - (kw - 1) + j for j in range(kw)]


def _bwd_offsets(kw):
    return [kw - 1 - j for j in range(kw)]


def _conv_bwd_w(dc, ext_ref, gw_ref, kw, halo, tt):
    for j in range(kw):
        gw_ref[j:j + 1, :] += jnp.sum(dc * ext_ref[pl.ds(halo - (kw - 1) + j, tt), :], axis=0, keepdims=True)


SUB = 8


def _fill_shifts(sh_ref, rows):
    for s in range(1, SUB):
        sh_ref[s, 0:rows, :] = sh_ref[0, pl.ds(s, rows), :]


def _tap(sh_ref, off, rows):
    return sh_ref[off % SUB, pl.ds(off - off % SUB, rows), :]


def _conv_bwd_w_sh(dc, ext_sh, gw_ref, kw, halo, tt):
    for j in range(kw):
        gw_ref[j:j + 1, :] += jnp.sum(dc * _tap(ext_sh, halo - (kw - 1) + j, tt), axis=0, keepdims=True)


def _a_post(c, z, g, b):
    mu = jnp.mean(c, axis=-1, keepdims=True)
    var = jnp.mean(jnp.square(c - mu), axis=-1, keepdims=True)
    a = (c - mu) * lax.rsqrt(var + EPS) * g + b
    return jax.nn.silu(a) * jax.nn.silu(z)


def _a_fwd(proj, dw, b_row, lg, lb, bsz, t, gather=()):
    n = bsz * t
    tt = _tile(t, 256)
    nt = t // tt
    cb = C_A // 512
    ng = len(gather)

    def body(v_ref, g_ref, z_ref, vh_ref, gh_ref, w_ref, b_ref, lg_ref, lb_ref, *rest):
        g_in, (y_ref, c_ref), g_out, (ext, *sems) = rest[:ng], rest[ng:ng + 2], rest[ng + 2:2 * ng + 2], rest[2 * ng + 2:]
        b, i = pl.program_id(0), pl.program_id(1)
        _ride_along(_Exchange(False, g_in, g_out, *sems) if ng else None,
                    (b == 0) & (i == 0), (b == bsz - 1) & (i == nt - 1))
        ext[0, 0:A_H, :] = jnp.where(i > 0, vh_ref[...] * jax.nn.sigmoid(gh_ref[...]), 0.0)
        ext[0, A_H:, :] = v_ref[...] * jax.nn.sigmoid(g_ref[...])
        _fill_shifts(ext, tt + A_H - SUB)
        _conv_rows(c_ref, lambda off, rows, c0, cols: _tap(ext, off, rows), w_ref, _fwd_offsets(A_K, A_H), tt, A_RB, 512,
                   bias=b_ref[...])
        y_ref[...] = _a_post(c_ref[...], z_ref[...], lg_ref[...], lb_ref[...])

    def row(b, i):
        return b * nt + i

    def halo(b, i):
        return jnp.maximum((b * t + i * tt) // A_H - 1, 0)

    hbm = pl.BlockSpec(memory_space=pltpu.HBM)
    return pl.pallas_call(
        body, name="a_fwd_gather" if ng else "a_fwd", grid=(bsz, nt),
        in_specs=[pl.BlockSpec((tt, 512), lambda b, i: (row(b, i), cb)),
                  pl.BlockSpec((tt, 512), lambda b, i: (row(b, i), cb + 1)),
                  pl.BlockSpec((tt, 512), lambda b, i: (row(b, i), cb + 2)),
                  pl.BlockSpec((A_H, 512), lambda b, i: (halo(b, i), cb)),
                  pl.BlockSpec((A_H, 512), lambda b, i: (halo(b, i), cb + 1)),
                  _full((32, 512)), _full((1, 512)), _full((1, 512)), _full((1, 512))] + [hbm] * ng,
        out_specs=[pl.BlockSpec((tt, 512), lambda b, i: (row(b, i), 0))] * 2 + [hbm] * ng,
        out_shape=[SDS((n, 512), F32)] * 2 + [SDS((NDEV,) + a.shape, a.dtype) for a in gather],
        scratch_shapes=[pltpu.VMEM((SUB, tt + A_H, 512), F32)] + (_exchange_sems(ng) if ng else []),
        compiler_params=_cp(40),
    )(proj, proj, proj, proj, proj, dw, b_row, lg, lb, *gather)


def _a_bwd(dproj, proj, conv, dy, dw, lg, lb, bsz, t):
    n = bsz * t
    tt = _tile(t, 256)
    nt = t // tt
    cb = C_A // 512

    def body(dp_any, v_ref, g_ref, z_ref, vh_ref, gh_ref, c_ref, dy_ref, w_ref, lg_ref, lb_ref,
             dp_ref, gw_ref, gb_ref, glg_ref, glb_ref, ext, dcp, dae, carry):
        b, i = pl.program_id(0), pl.program_id(1)
        ti = nt - 1 - i

        @pl.when((b == 0) & (i == 0))
        def _():
            gw_ref[...] = jnp.zeros_like(gw_ref)
            gb_ref[...] = jnp.zeros_like(gb_ref)
            glg_ref[...] = jnp.zeros_like(glg_ref)
            glb_ref[...] = jnp.zeros_like(glb_ref)

        @pl.when(i == 0)
        def _():
            carry[...] = jnp.zeros_like(carry)

        val, glu = v_ref[...], g_ref[...]
        sg = jax.nn.sigmoid(glu)
        ext[0, 0:A_H, :] = jnp.where(ti > 0, vh_ref[...] * jax.nn.sigmoid(gh_ref[...]), 0.0)
        ext[0, A_H:, :] = val * sg
        _fill_shifts(ext, tt + A_H - SUB)
        _, vjp = jax.vjp(_a_post, c_ref[...], z_ref[...], lg_ref[...], lb_ref[...])
        dc, dz, dlg, dlb = vjp(dy_ref[...])
        gb_ref[...] += jnp.sum(dc, axis=0, keepdims=True)
        glg_ref[...] += dlg
        glb_ref[...] += dlb
        dcp[0, 0:A_H, :] = jnp.zeros((A_H, 512), F32)
        dcp[0, A_H:A_H + tt, :] = dc
        dcp[0, A_H + tt:, :] = jnp.zeros((A_H, 512), F32)
        _fill_shifts(dcp, tt + 2 * A_H - SUB)
        _conv_bwd_w_sh(dc, ext, gw_ref, A_K, A_H, tt)
        _conv_rows(dae, lambda off, rows, c0, cols: _tap(dcp, off, rows), w_ref, _bwd_offsets(A_K), tt + A_H, A_RB, 512)
        dae[tt:tt + A_H, :] += carry[...]
        carry[...] = dae[0:A_H, :]
        da = dae[A_H:, :]
        dp_ref[:, 0:512] = (da * sg).astype(BF16)
        dp_ref[:, 512:1024] = (da * val * sg * (1.0 - sg)).astype(BF16)
        dp_ref[:, 1024:1536] = dz.astype(BF16)

    def row(b, i):
        return b * nt + (nt - 1 - i)

    def halo(b, i):
        return jnp.maximum((b * t + (nt - 1 - i) * tt) // A_H - 1, 0)

    outs = pl.pallas_call(
        body, name="a_bwd", grid=(bsz, nt),
        in_specs=[pl.BlockSpec(memory_space=pl.ANY),
                  pl.BlockSpec((tt, 512), lambda b, i: (row(b, i), cb)),
                  pl.BlockSpec((tt, 512), lambda b, i: (row(b, i), cb + 1)),
                  pl.BlockSpec((tt, 512), lambda b, i: (row(b, i), cb + 2)),
                  pl.BlockSpec((A_H, 512), lambda b, i: (halo(b, i), cb)),
                  pl.BlockSpec((A_H, 512), lambda b, i: (halo(b, i), cb + 1)),
                  pl.BlockSpec((tt, 512), lambda b, i: (row(b, i), 0)),
                  pl.BlockSpec((tt, 512), lambda b, i: (row(b, i), 0)),
                  _full((32, 512)), _full((1, 512)), _full((1, 512))],
        out_specs=[pl.BlockSpec((tt, 1536), lambda b, i: (row(b, i), C_A // 1536)),
                   _full((32, 512)), _full((1, 512)), _full((1, 512)), _full((1, 512))],
        out_shape=[SDS((n, NP), BF16), SDS((32, 512), F32), SDS((1, 512), F32), SDS((1, 512), F32), SDS((1, 512), F32)],
        input_output_aliases={0: 0},
        scratch_shapes=[pltpu.VMEM((SUB, tt + A_H, 512), F32), pltpu.VMEM((SUB, tt + 2 * A_H, 512), F32),
                        pltpu.VMEM((tt + A_H, 512), F32), pltpu.VMEM((A_H, 512), F32)],
        compiler_params=_cp(48),
    )(dproj, proj, proj, proj, proj, proj, conv, dy, dw, lg, lb)
    return outs


def _b_post(blocks):
    out = []
    for idx, c in enumerate(blocks):
        s = jax.nn.silu(c)
        if idx < 2 * NH:
            s = s * lax.rsqrt(jnp.sum(s * s, axis=-1, keepdims=True) + EPS)
            if idx < NH:
                s = s * (LANE ** -0.5)
        out.append(s)
    return out


def _bprep_fwd(proj, wconv, bsz, t, gather=()):
    n = bsz * t
    tt = _tile(t, 256)
    nt = t // tt
    ng = len(gather)

    def body(x_ref, xh_ref, w_ref, *rest):
        g_in, (o_ref, c_ref), g_out, (ext, *sems) = rest[:ng], rest[ng:ng + 2], rest[ng + 2:2 * ng + 2], rest[2 * ng + 2:]
        b, i = pl.program_id(0), pl.program_id(1)
        _ride_along(_Exchange(False, g_in, g_out, *sems) if ng else None,
                    (b == 0) & (i == 0), (b == bsz - 1) & (i == nt - 1))
        ext[0:B_H, :] = jnp.where(i > 0, xh_ref[...], 0.0)
        ext[B_H:, :] = x_ref[...]
        _conv_rows(c_ref, lambda off, rows, c0, cols: ext[pl.ds(off, rows), c0:c0 + cols], w_ref,
                   _fwd_offsets(B_K, B_H), tt, B_RB, B_CB)
        outs = _b_post([c_ref[:, LANE * j:LANE * (j + 1)] for j in range(3 * NH)])
        for j, o in enumerate(outs):
            o_ref[:, LANE * j:LANE * (j + 1)] = o

    hbm = pl.BlockSpec(memory_space=pltpu.HBM)
    return pl.pallas_call(
        body, name="bprep_fwd_gather" if ng else "bprep_fwd", grid=(bsz, nt),
        in_specs=[pl.BlockSpec((tt, 3072), lambda b, i: (b * nt + i, 0)),
                  pl.BlockSpec((B_H, 3072), lambda b, i: (jnp.maximum((b * t + i * tt) // B_H - 1, 0), 0)),
                  _full((8, 3072))] + [hbm] * ng,
        out_specs=[pl.BlockSpec((tt, 3072), lambda b, i: (b * nt + i, 0))] * 2 + [hbm] * ng,
        out_shape=[SDS((n, 3072), F32)] * 2 + [SDS((NDEV,) + a.shape, a.dtype) for a in gather],
        scratch_shapes=[pltpu.VMEM((tt + B_H, 3072), F32)] + (_exchange_sems(ng) if ng else []),
        compiler_params=_cp(56),
    )(proj, proj, wconv, *gather)


def _bprep_bwd(dproj, proj, conv, dqkvn, wconv, bsz, t):
    n = bsz * t
    tt = _tile(t, 256)
    nt = t // tt

    def body(dp_any, x_ref, xh_ref, c_ref, dq_ref, w_ref, dp_ref, gw_ref, ext, dcp, dae, carry):
        b, i = pl.program_id(0), pl.program_id(1)
        ti = nt - 1 - i

        @pl.when((b == 0) & (i == 0))
        def _():
            gw_ref[...] = jnp.zeros_like(gw_ref)

        @pl.when(i == 0)
        def _():
            carry[...] = jnp.zeros_like(carry)

        ext[0:B_H, :] = jnp.where(ti > 0, xh_ref[...], 0.0)
        ext[B_H:, :] = x_ref[...]
        _, vjp = jax.vjp(_b_post, [c_ref[:, LANE * j:LANE * (j + 1)] for j in range(3 * NH)])
        (dcs,) = vjp([dq_ref[:, LANE * j:LANE * (j + 1)] for j in range(3 * NH)])
        dcp[0:B_H, :] = jnp.zeros((B_H, 3072), F32)
        for j, dcj in enumerate(dcs):
            dcp[B_H:B_H + tt, LANE * j:LANE * (j + 1)] = dcj
        dcp[B_H + tt:, :] = jnp.zeros((B_H, 3072), F32)
        _conv_bwd_w(dcp[B_H:B_H + tt, :], ext, gw_ref, B_K, B_H, tt)
        _conv_rows(dae, lambda off, rows, c0, cols: dcp[pl.ds(off, rows), c0:c0 + cols], w_ref,
                   _bwd_offsets(B_K), tt + B_H, B_RB, B_CB)
        dae[tt:tt + B_H, :] += carry[...]
        carry[...] = dae[0:B_H, :]
        dp_ref[...] = dae[B_H:, :].astype(BF16)

    def row(b, i):
        return b * nt + (nt - 1 - i)

    return pl.pallas_call(
        body, name="bprep_bwd", grid=(bsz, nt),
        in_specs=[pl.BlockSpec(memory_space=pl.ANY),
                  pl.BlockSpec((tt, 3072), lambda b, i: (row(b, i), 0)),
                  pl.BlockSpec((B_H, 3072), lambda b, i: (jnp.maximum((b * t + (nt - 1 - i) * tt) // B_H - 1, 0), 0)),
                  pl.BlockSpec((tt, 3072), lambda b, i: (row(b, i), 0)),
                  pl.BlockSpec((tt, 3072), lambda b, i: (row(b, i), 0)),
                  _full((8, 3072))],
        out_specs=[pl.BlockSpec((tt, 3072), lambda b, i: (row(b, i), 0)), _full((8, 3072))],
        out_shape=[SDS((n, NP), BF16), SDS((8, 3072), F32)],
        input_output_aliases={0: 0},
        scratch_shapes=[pltpu.VMEM((tt + B_H, 3072), F32), pltpu.VMEM((tt + 2 * B_H, 3072), F32),
                        pltpu.VMEM((tt + B_H, 3072), F32), pltpu.VMEM((B_H, 3072), F32)],
        compiler_params=_cp(56),
    )(dproj, proj, proj, conv, dqkvn, wconv)


def _split2(a):
    hi = a.astype(BF16)
    return hi, (a - hi.astype(F32)).astype(BF16)


def _split3(a):
    p1 = a.astype(BF16)
    r1 = a - p1.astype(F32)
    p2 = r1.astype(BF16)
    return p1, p2, (r1 - p2.astype(F32)).astype(BF16)


def _dot3_raw(a, b, dims):
    a1, a2 = _split2(a)
    b1, b2 = _split2(b)
    return _mm(a1, b1, dims) + (_mm(a1, b2, dims) + _mm(a2, b1, dims))


def _dot6(a, b, dims):
    a1, a2, a3 = _split3(a)
    b1, b2, b3 = _split3(b)
    return (_mm(a1, b1, dims) + (_mm(a1, b2, dims) + _mm(a2, b1, dims))
            + (_mm(a1, b3, dims) + _mm(a2, b2, dims) + _mm(a3, b1, dims)))


def _unit_lower_inverse_raw(lmats):
    r = lax.broadcasted_iota(jnp.int32, (CH, CH), 0)
    c = lax.broadcasted_iota(jnp.int32, (CH, CH), 1)
    eye = (r == c).astype(F32)
    blk = jnp.right_shift(r, 4) == jnp.right_shift(c, 4)
    dm = [jnp.where(blk, x, 0.0) for x in lmats]
    om = [a - b for a, b in zip(lmats, dm)]
    d2 = [_dot3_raw(x, x, NN) for x in dm]
    d4 = [_dot3_raw(x, x, NN) for x in d2]
    d8 = [_dot3_raw(x, x, NN) for x in d4]
    p = [_dot3_raw(eye - a, eye + b, NN) for a, b in zip(dm, d2)]
    p = [_dot3_raw(a, eye + b, NN) for a, b in zip(p, d4)]
    p = [_dot3_raw(a, eye + b, NN) for a, b in zip(p, d8)]
    m = [_dot3_raw(a, b, NN) for a, b in zip(p, om)]
    m2 = [_dot3_raw(x, x, NN) for x in m]
    t = [_dot3_raw(eye - a, eye + b, NN) for a, b in zip(m, m2)]
    return [_dot3_raw(a, b, NN) for a, b in zip(t, p)]


@jax.custom_vjp
def _unit_lower_inverse(lmats):
    return _unit_lower_inverse_raw(lmats)


def _unit_lower_inverse_fwd(lmats):
    tinv = _unit_lower_inverse_raw(lmats)
    return tinv, tinv


def _unit_lower_inverse_bwd(tinv, gs):
    x = [_dot6(t, g, TN) for t, g in zip(tinv, gs)]
    return ([-_dot6(a, t, NT) for a, t in zip(x, tinv)],)


_unit_lower_inverse.defvjp(_unit_lower_inverse_fwd, _unit_lower_inverse_bwd)


@jax.custom_vjp
def _saved_unit_lower_inverse(lmats, tinv):
    del lmats
    return tinv


def _saved_unit_lower_inverse_fwd(lmats, tinv):
    del lmats
    return tinv, tinv


def _saved_unit_lower_inverse_bwd(tinv, gs):
    (dl,) = _unit_lower_inverse_bwd(tinv, gs)
    return dl, [jnp.zeros_like(t) for t in tinv]


_saved_unit_lower_inverse.defvjp(_saved_unit_lower_inverse_fwd, _saved_unit_lower_inverse_bwd)


def _tri(lower):
    r = lax.broadcasted_iota(jnp.int32, (CH, CH), 0)
    c = lax.broadcasted_iota(jnp.int32, (CH, CH), 1)
    return ((r >= c) if lower else (r <= c)).astype(BF16)


def _tri_dot(x, lower, dims, tri_first):
    p1, p2, p3 = _split3(x)
    tri = _tri(lower)
    if tri_first:
        return _mm(tri, p1, dims) + (_mm(tri, p2, dims) + _mm(tri, p3, dims))
    return _mm(p1, tri, dims) + (_mm(p2, tri, dims) + _mm(p3, tri, dims))


@jax.custom_vjp
def _cumsum_rows(x):
    return _tri_dot(x, True, NN, True)


def _cumsum_rows_fwd(x):
    return _tri_dot(x, True, NN, True), None


def _cumsum_rows_bwd(_, g):
    return (_tri_dot(g, True, TN, True),)


_cumsum_rows.defvjp(_cumsum_rows_fwd, _cumsum_rows_bwd)


@jax.custom_vjp
def _cumsum_rows_t(x):
    return _tri_dot(x, False, TN, False)


def _cumsum_rows_t_fwd(x):
    return _tri_dot(x, False, TN, False), None


def _cumsum_rows_t_bwd(_, g):
    return (_tri_dot(g, False, NT, True),)


_cumsum_rows_t.defvjp(_cumsum_rows_t_fwd, _cumsum_rows_t_bwd)


def _delta_chunk(ss, qs, ks, vs, bas, zs, alog, dtb, og, tsaved=None):
    heads = range(len(ss))
    lane = lax.broadcasted_iota(jnp.int32, (1, LANE), 1)
    r = lax.broadcasted_iota(jnp.int32, (CH, CH), 0)
    c = lax.broadcasted_iota(jnp.int32, (CH, CH), 1)
    ri = lax.broadcasted_iota(jnp.int32, (CH, 1), 0)
    incl, strict = r >= c, r > c

    def pick(x, h):
        return jnp.sum(jnp.where(lane == h, x, 0.0), axis=-1, keepdims=True)

    beta = [jax.nn.sigmoid(pick(bas[e // NH], e % NH)) for e in heads]
    g = [-jnp.exp(pick(alog, e % NH + NH)) * _softplus(pick(bas[e // NH], e % NH + NH) + pick(dtb, e % NH + NH))
         for e in heads]
    gb = [jnp.broadcast_to(x, (CH, CH)) for x in g]
    gca = [_cumsum_rows(x) for x in gb]
    gcr = [_cumsum_rows_t(x) for x in gb]
    gc = [jnp.sum(jnp.where(c == 0, x, 0.0), axis=-1, keepdims=True) for x in gca]
    gl = [jnp.sum(jnp.where(ri == CH - 1, x, 0.0), axis=0, keepdims=True) for x in gc]
    diff = [a - b for a, b in zip(gca, gcr)]
    gam_s = [jnp.where(strict, jnp.exp(jnp.where(strict, x, 0.0)), 0.0) for x in diff]
    gam_i = [jnp.where(incl, jnp.exp(jnp.where(incl, x, 0.0)), 0.0) for x in diff]

    kk = [_mm(k, k, NT) for k in ks]
    lmats = [beta[h] * kk[h] * gam_s[h] for h in heads]
    tinv = _unit_lower_inverse(lmats) if tsaved is None else _saved_unit_lower_inverse(lmats, tsaved)

    eg = [jnp.exp(x) for x in gc]
    u = [_mm(tinv[h], vs[h] * beta[h], NN) for h in heads]
    w = [_mm(tinv[h], ks[h] * (beta[h] * eg[h]), NN) for h in heads]
    qk = [_mm(qs[h], ks[h], NT) * gam_i[h] for h in heads]
    vn = [u[h] - _mm(w[h], ss[h], NN) for h in heads]
    o = [_mm(qs[h] * eg[h], ss[h], NN) + _mm(qk[h], vn[h], NN) for h in heads]
    sn = [jnp.exp(gl[h]) * ss[h] + _mm(ks[h] * jnp.exp(gl[h] - gc[h]), vn[h], TN) for h in heads]
    y = [_rms(o[h], og) * jax.nn.silu(zs[h]) for h in heads]
    return (sn, y), tinv


def _chain_blocks(ref):
    return [ref[b, :, LANE * h:LANE * (h + 1)] for b in range(ref.shape[0]) for h in range(NH)]


def _ride_along(ex, first, last):
    if ex is None:
        return

    @pl.when(first)
    def _():
        ex.start()

    @pl.when(last)
    def _():
        ex.wait()


def _delta_fwd(qkvn, proj, alog, dtb, og, bsz, t, gather=()):
    n = bsz * t
    nc = t // CH
    ng = len(gather)

    def body(q_ref, k_ref, v_ref, ba_ref, z_ref, al_ref, dt_ref, og_ref, *rest):
        g_in, (y_ref, sh_ref, ti_ref) = rest[:ng], rest[ng:ng + 3]
        g_out, (s_scr, *sems) = rest[ng + 3:2 * ng + 3], rest[2 * ng + 3:]
        ci = pl.program_id(0)
        _ride_along(_Exchange(False, g_in, g_out, *sems) if ng else None, ci == 0, ci == nc - 1)

        @pl.when(ci == 0)
        def _():
            s_scr[...] = jnp.zeros_like(s_scr)

        chains = range(bsz * NH)
        ss = [s_scr[e] for e in chains]
        for e in chains:
            sh_ref[e // NH, e % NH] = ss[e]
        (sn, y), tinv = _delta_chunk(ss, _chain_blocks(q_ref), _chain_blocks(k_ref), _chain_blocks(v_ref),
                                     [ba_ref[b] for b in range(bsz)], _chain_blocks(z_ref),
                                     al_ref[...], dt_ref[...], og_ref[...])
        for e in chains:
            s_scr[e] = sn[e]
            ti_ref[e // NH, e % NH] = tinv[e]
            y_ref[e // NH, :, LANE * (e % NH):LANE * (e % NH + 1)] = y[e]

    def blk(width, col):
        return pl.BlockSpec((bsz, CH, width), lambda ci: (0, ci, col))

    def per_chunk(rows):
        return pl.BlockSpec((bsz, None, NH, rows, rows), lambda ci: (0, ci, 0, 0, 0))

    hbm = pl.BlockSpec(memory_space=pltpu.HBM)
    qkv3, proj3 = qkvn.reshape(bsz, t, 3 * D), proj.reshape(bsz, t, NP)
    y, *rest = pl.pallas_call(
        body, name="delta_fwd_gather" if ng else "delta_fwd", grid=(nc,),
        in_specs=[blk(D, 0), blk(D, 1), blk(D, 2), blk(LANE, C_BA // LANE), blk(D, C_BZ // D),
                  _full((1, LANE)), _full((1, LANE)), _full((1, LANE))] + [hbm] * ng,
        out_specs=[blk(D, 0), per_chunk(LANE), per_chunk(CH)] + [hbm] * ng,
        out_shape=[SDS((bsz, t, D), F32), SDS((bsz, nc, NH, LANE, LANE), F32), SDS((bsz, nc, NH, CH, CH), F32)]
        + [SDS((NDEV,) + a.shape, a.dtype) for a in gather],
        scratch_shapes=[pltpu.VMEM((bsz * NH, LANE, LANE), F32)] + (_exchange_sems(ng) if ng else []),
        compiler_params=_cp(48),
    )(qkv3, qkv3, qkv3, proj3, proj3, alog, dtb, og, *gather)
    return (y.reshape(n, D), *rest)


def _delta_bwd(dproj, qkvn, proj, shist, tsaved, dyb, alog, dtb, og, bsz, t, scatter=()):
    n = bsz * t
    nc = t // CH
    ns = len(scatter)
    eg = 1

    def body(dp_any, q_ref, k_ref, v_ref, ba_ref, z_ref, sh_ref, ti_ref, dy_ref, al_ref, dt_ref, og_ref, *rest):
        s_in, (dp_ref, dqkv_ref, gal_ref, gdt_ref, gog_ref) = rest[:ns], rest[ns:ns + 5]
        s_out, (ds_scr, *sems) = rest[ns + 5:2 * ns + 5], rest[2 * ns + 5:]
        bi, ci = pl.program_id(0), pl.program_id(1)
        _ride_along(_Exchange(True, s_in, s_out, *sems) if ns else None,
                    (bi == 0) & (ci == 0), (bi == bsz // eg - 1) & (ci == nc - 1))

        @pl.when((bi == 0) & (ci == 0))
        def _():
            gal_ref[...] = jnp.zeros_like(gal_ref)
            gdt_ref[...] = jnp.zeros_like(gdt_ref)
            gog_ref[...] = jnp.zeros_like(gog_ref)

        @pl.when(ci == 0)
        def _():
            ds_scr[...] = jnp.zeros_like(ds_scr)

        chains = range(eg * NH)
        _, vjp, _ = jax.vjp(_delta_chunk, [sh_ref[e // NH, e % NH] for e in chains], _chain_blocks(q_ref),
                            _chain_blocks(k_ref), _chain_blocks(v_ref), [ba_ref[b] for b in range(eg)],
                            _chain_blocks(z_ref), al_ref[...], dt_ref[...], og_ref[...],
                            [ti_ref[e // NH, e % NH] for e in chains], has_aux=True)
        ds, dq, dk, dv, dba, dz, dal, ddt, dog, _ = vjp(([ds_scr[e] for e in chains], _chain_blocks(dy_ref)))
        gal_ref[...] += dal
        gdt_ref[...] += ddt
        gog_ref[...] += dog
        for b in range(eg):
            dp_ref[b, :, D:D + LANE] = dba[b].astype(BF16)
        for e in chains:
            b, lo = e // NH, LANE * (e % NH)
            ds_scr[e] = ds[e]
            dp_ref[b, :, lo:lo + LANE] = dz[e].astype(BF16)
            dqkv_ref[b, :, lo:lo + LANE] = dq[e]
            dqkv_ref[b, :, D + lo:D + lo + LANE] = dk[e]
            dqkv_ref[b, :, 2 * D + lo:2 * D + lo + LANE] = dv[e]

    def blk(width, col):
        return pl.BlockSpec((eg, CH, width), lambda bi, ci: (bi, nc - 1 - ci, col))

    def per_chunk(rows):
        return pl.BlockSpec((eg, None, NH, rows, rows), lambda bi, ci: (bi, nc - 1 - ci, 0, 0, 0))

    hbm = pl.BlockSpec(memory_space=pltpu.HBM)
    qkv3, proj3 = qkvn.reshape(bsz, t, 3 * D), proj.reshape(bsz, t, NP)
    dproj, dqkvn, *rest = pl.pallas_call(
        body, name="delta_bwd_exchange" if ns else "delta_bwd", grid=(bsz // eg, nc),
        in_specs=[pl.BlockSpec(memory_space=pl.ANY), blk(D, 0), blk(D, 1), blk(D, 2), blk(LANE, C_BA // LANE),
                  blk(D, C_BZ // D), per_chunk(LANE), per_chunk(CH),
                  blk(D, 0), _full((1, LANE)), _full((1, LANE)), _full((1, LANE))] + [hbm] * ns,
        out_specs=[blk(D + LANE, C_BZ // (D + LANE)), blk(3 * D, 0), _full((1, LANE)), _full((1, LANE)),
                   _full((1, LANE))] + [hbm] * ns,
        out_shape=[SDS((bsz, t, NP), BF16), SDS((bsz, t, 3 * D), F32), SDS((1, LANE), F32), SDS((1, LANE), F32),
                   SDS((1, LANE), F32)] + [SDS(a.shape, a.dtype) for a in scatter],
        input_output_aliases={0: 0},
        scratch_shapes=[pltpu.VMEM((eg * NH, LANE, LANE), F32)] + (_exchange_sems(ns) if ns else []),
        compiler_params=_cp(56),
    )(dproj.reshape(bsz, t, NP), qkv3, qkv3, qkv3, proj3, proj3, shist, tsaved, dyb.reshape(bsz, t, D), alog, dtb, og,
      *scatter)
    return (dproj.reshape(n, NP), dqkvn.reshape(n, 3 * D), *rest)


def _c_chunk(us, vs, zs, lgs, lbs, ws, bsb):
    gv = [jax.nn.gelu(v) for v in vs]
    width = LANE * len(gv)
    mu = sum(jnp.sum(x, axis=-1, keepdims=True) for x in gv) / width
    var = sum(jnp.sum(jnp.square(x - mu), axis=-1, keepdims=True) for x in gv) / width
    rstd = lax.rsqrt(var + EPS)
    r = lax.broadcasted_iota(jnp.int32, (SG, SG), 0)
    c = lax.broadcasted_iota(jnp.int32, (SG, SG), 1)
    out = []
    for j in range(len(gv)):
        nrm = (gv[j] - mu) * rstd * lgs[j] + lbs[j]
        mixed = jnp.dot(jnp.where(r >= c, ws[j], 0.0), nrm, preferred_element_type=F32) + bsb[j]
        out.append(jax.nn.gelu(us[j]) * mixed * jax.nn.silu(zs[j]))
    return out


C_CHUNKS = 2


def _c_args(r, u_ref, v_ref, z_ref, lg_ref, lb_ref, ws_ref, bs_ref):
    sl = [slice(LANE * j, LANE * (j + 1)) for j in range(4)]
    rows = slice(SG * r, SG * (r + 1))
    return ([u_ref[rows, s] for s in sl], [v_ref[rows, s] for s in sl], [z_ref[rows, s] for s in sl],
            [lg_ref[:, s] for s in sl], [lb_ref[:, s] for s in sl],
            [ws_ref[j] for j in range(4)], [bs_ref[j] for j in range(4)])


def _c_fwd(proj, lg, lb, ws, bsb, n):
    cb = C_C // 512
    tr = SG * C_CHUNKS

    def body(u_ref, v_ref, z_ref, lg_ref, lb_ref, ws_ref, bs_ref, y_ref):
        for r in range(C_CHUNKS):
            outs = _c_chunk(*_c_args(r, u_ref, v_ref, z_ref, lg_ref, lb_ref, ws_ref, bs_ref))
            for j, o in enumerate(outs):
                y_ref[SG * r:SG * (r + 1), LANE * j:LANE * (j + 1)] = o

    return pl.pallas_call(
        body, name="c_fwd", grid=(n // tr,),
        in_specs=[pl.BlockSpec((tr, 512), lambda i: (i, cb)), pl.BlockSpec((tr, 512), lambda i: (i, cb + 1)),
                  pl.BlockSpec((tr, 512), lambda i: (i, cb + 2)), _full((1, 512)), _full((1, 512)),
                  _full((4, SG, SG)), _full((4, SG, SG))],
        out_specs=pl.BlockSpec((tr, 512), lambda i: (i, 0)),
        out_shape=SDS((n, 512), F32), compiler_params=_cp(32),
    )(proj, proj, proj, lg, lb, ws, bsb)


def _c_bwd(dproj, proj, dy, lg, lb, ws, bsb, n):
    cb = C_C // 512
    tr = SG * C_CHUNKS

    def body(dp_any, u_ref, v_ref, z_ref, dy_ref, lg_ref, lb_ref, ws_ref, bs_ref,
             dp_ref, glg_ref, glb_ref, gws_ref, gbs_ref):
        @pl.when(pl.program_id(0) == 0)
        def _():
            glg_ref[...] = jnp.zeros_like(glg_ref)
            glb_ref[...] = jnp.zeros_like(glb_ref)
            gws_ref[...] = jnp.zeros_like(gws_ref)
            gbs_ref[...] = jnp.zeros_like(gbs_ref)

        for r in range(C_CHUNKS):
            rows = slice(SG * r, SG * (r + 1))
            _, vjp = jax.vjp(_c_chunk, *_c_args(r, u_ref, v_ref, z_ref, lg_ref, lb_ref, ws_ref, bs_ref))
            dus, dvs, dzs, dlgs, dlbs, dwss, dbss = vjp([dy_ref[rows, LANE * j:LANE * (j + 1)] for j in range(4)])
            for j in range(4):
                sl = slice(LANE * j, LANE * (j + 1))
                dp_ref[rows, LANE * j:LANE * (j + 1)] = dus[j].astype(BF16)
                dp_ref[rows, 512 + LANE * j:512 + LANE * (j + 1)] = dvs[j].astype(BF16)
                dp_ref[rows, 1024 + LANE * j:1024 + LANE * (j + 1)] = dzs[j].astype(BF16)
                glg_ref[:, sl] += dlgs[j]
                glb_ref[:, sl] += dlbs[j]
                gws_ref[j] += dwss[j]
                gbs_ref[j] += jnp.broadcast_to(jnp.sum(dbss[j], axis=-1, keepdims=True), (SG, SG))

    return pl.pallas_call(
        body, name="c_bwd", grid=(n // tr,),
        in_specs=[pl.BlockSpec(memory_space=pl.ANY),
                  pl.BlockSpec((tr, 512), lambda i: (i, cb)), pl.BlockSpec((tr, 512), lambda i: (i, cb + 1)),
                  pl.BlockSpec((tr, 512), lambda i: (i, cb + 2)), pl.BlockSpec((tr, 512), lambda i: (i, 0)),
                  _full((1, 512)), _full((1, 512)), _full((4, SG, SG)), _full((4, SG, SG))],
        out_specs=[pl.BlockSpec((tr, 1536), lambda i: (i, C_C // 1536)),
                   _full((1, 512)), _full((1, 512)), _full((4, SG, SG)), _full((4, SG, SG))],
        out_shape=[SDS((n, NP), BF16), SDS((1, 512), F32), SDS((1, 512), F32), SDS((4, SG, SG), F32), SDS((4, SG, SG), F32)],
        input_output_aliases={0: 0}, compiler_params=_cp(32),
    )(dproj, proj, proj, proj, dy, lg, lb, ws, bsb)


def _merge_fwd(x2d, ya, yb, yc, proj, ap, bp, cp, wo):
    n = x2d.shape[0]
    tm = _tile(n, 512)
    gb = C_G // D

    def body(x_ref, ya_ref, yb_ref, yc_ref, g0_ref, g1_ref, g2_ref, ap_ref, bp_ref, cp_ref, wo_ref, o_ref):
        merged = (jax.nn.sigmoid(g0_ref[...]) * _bdot(ya_ref[...], ap_ref[...])
                  + jax.nn.sigmoid(g1_ref[...]) * _bdot(yb_ref[...], bp_ref[...])
                  + jax.nn.sigmoid(g2_ref[...]) * _bdot(yc_ref[...], cp_ref[...]))
        o_ref[...] = x_ref[...] + _bdot(merged, wo_ref[...])

    def rows(w):
        return pl.BlockSpec((tm, w), lambda i: (i, 0))

    return pl.pallas_call(
        body, name="merge_fwd", grid=(n // tm,),
        in_specs=[rows(D), rows(512), rows(D), rows(512),
                  pl.BlockSpec((tm, D), lambda i: (i, gb)), pl.BlockSpec((tm, D), lambda i: (i, gb + 1)),
                  pl.BlockSpec((tm, D), lambda i: (i, gb + 2)),
                  _resident((512, D)), _resident((D, D)), _resident((512, D)), _resident((D, D))],
        out_specs=rows(D), out_shape=SDS((n, D), F32), compiler_params=_cp(48),
    )(x2d, ya, yb, yc, proj, proj, proj, ap, bp, cp, wo)


def _merge_bwd(dxo, ya, yb, yc, proj, ap, bp, cp, apt, bpt, cpt, wot):
    n = dxo.shape[0]
    tm = _tile(n, 256)
    gb = C_G // D

    def body(d_ref, ya_ref, yb_ref, yc_ref, g0_ref, g1_ref, g2_ref, ap_ref, bp_ref, cp_ref,
             apt_ref, bpt_ref, cpt_ref, wot_ref,
             dp_ref, dya_ref, dyb_ref, dyc_ref, dpa_ref, dpb_ref, dpc_ref, mg_ref):
        dm = _bdot(d_ref[...], wot_ref[...])
        merged = None
        for j, (g_ref, y_ref, w_ref, wt_ref, dy_ref, dpj_ref) in enumerate((
                (g0_ref, ya_ref, ap_ref, apt_ref, dya_ref, dpa_ref),
                (g1_ref, yb_ref, bp_ref, bpt_ref, dyb_ref, dpb_ref),
                (g2_ref, yc_ref, cp_ref, cpt_ref, dyc_ref, dpc_ref))):
            s = jax.nn.sigmoid(g_ref[...])
            pj = _bdot(y_ref[...], w_ref[...])
            merged = s * pj if merged is None else merged + s * pj
            dp_ref[:, D * j:D * (j + 1)] = (dm * pj * s * (1.0 - s)).astype(BF16)
            dpj = (dm * s).astype(BF16)
            dpj_ref[...] = dpj
            dy_ref[...] = jnp.dot(dpj, wt_ref[...], preferred_element_type=F32)
        mg_ref[...] = merged

    def rows(w):
        return pl.BlockSpec((tm, w), lambda i: (i, 0))

    return pl.pallas_call(
        body, name="merge_bwd", grid=(n // tm,),
        in_specs=[rows(D), rows(512), rows(D), rows(512),
                  pl.BlockSpec((tm, D), lambda i: (i, gb)), pl.BlockSpec((tm, D), lambda i: (i, gb + 1)),
                  pl.BlockSpec((tm, D), lambda i: (i, gb + 2)),
                  _resident((512, D)), _resident((D, D)), _resident((512, D)),
                  _resident((D, 512)), _resident((D, D)), _resident((D, 512)), _resident((D, D))],
        out_specs=[pl.BlockSpec((tm, 3 * D), lambda i: (i, C_G // (3 * D))), rows(512), rows(D), rows(512),
                   rows(D), rows(D), rows(D), rows(D)],
        out_shape=[SDS((n, NP), BF16), SDS((n, 512), F32), SDS((n, D), F32), SDS((n, 512), F32),
                   SDS((n, D), BF16), SDS((n, D), BF16), SDS((n, D), BF16), SDS((n, D), F32)],
        compiler_params=_cp(56),
    )(dxo, ya, yb, yc, proj, proj, proj, ap, bp, cp, apt, bpt, cpt, wot)


def _sum_parts(p_ref):
    g = p_ref[0].astype(F32)
    for s in range(1, NDEV):
        g = g + p_ref[s].astype(F32)
    return g


def _adamw(g, w, m, v):
    nm = ADAM_B1 * m + (1.0 - ADAM_B1) * g
    nv = ADAM_B2 * v + (1.0 - ADAM_B2) * jnp.square(g)
    nm_hat = nm / (1.0 - ADAM_B1 ** ADAM_STEP)
    nv_hat = nv / (1.0 - ADAM_B2 ** ADAM_STEP)
    return -ADAM_LR * (nm_hat / (jnp.sqrt(nv_hat) + ADAM_EPS) + ADAM_WD * w), nm, nv


def _reduce_adamw(parts, w, m, v, name):
    r, c = w.shape
    tr = _tile(r, 128)

    def body(p_ref, w_ref, m_ref, v_ref, g_ref, d_ref, nm_ref, nv_ref):
        g = _sum_parts(p_ref)
        g_ref[...] = g
        d_ref[...], nm_ref[...], nv_ref[...] = _adamw(g, w_ref[...], m_ref[...], v_ref[...])

    blk = pl.BlockSpec((tr, c), lambda i: (i, 0))
    return pl.pallas_call(
        body, name=name, grid=(r // tr,),
        in_specs=[pl.BlockSpec((NDEV, tr, c), lambda i: (0, i, 0)), blk, blk, blk],
        out_specs=[blk, blk, blk, blk], out_shape=[SDS((r, c), F32)] * 4, compiler_params=_cp(48),
    )(parts, w, m, v)


def _reduce_adamw_leaves(parts, ws, ms, vs, name):
    nleaf = len(ws)
    counts = [len(p) if isinstance(p, (list, tuple)) else 0 for p in parts]
    flat = [a for p in parts for a in (p if isinstance(p, (list, tuple)) else [p])]

    def body(*refs):
        p_refs, rest = refs[:len(flat)], refs[len(flat):]
        w_refs, m_refs, v_refs = rest[:nleaf], rest[nleaf:2 * nleaf], rest[2 * nleaf:3 * nleaf]
        outs = rest[3 * nleaf:]
        at = 0
        for i in range(nleaf):
            g_ref, d_ref, nm_ref, nv_ref = outs[i], outs[nleaf + i], outs[2 * nleaf + i], outs[3 * nleaf + i]
            for idx in (range(counts[i]) if counts[i] else [Ellipsis]):
                g = _sum_parts(p_refs[at])
                at += 1
                g_ref[idx] = g
                d_ref[idx], nm_ref[idx], nv_ref[idx] = _adamw(g, w_refs[i][idx], m_refs[i][idx], v_refs[i][idx])

    vm = pl.BlockSpec(memory_space=pltpu.VMEM)
    outs = pl.pallas_call(
        body, name=name, in_specs=[vm] * (len(flat) + 3 * nleaf), out_specs=[vm] * (4 * nleaf),
        out_shape=[SDS(w.shape, F32) for w in ws] * 4, compiler_params=_cp(56),
    )(*flat, *ws, *ms, *vs)
    return [outs[j * nleaf:(j + 1) * nleaf] for j in range(4)]


def _unshard(name, g):
    if name in ROW_SHARDED:
        return g.reshape(g.shape[0] * g.shape[1], g.shape[2])
    g = jnp.moveaxis(g, 0, 1)
    return g.reshape(g.shape[0], g.shape[1] * g.shape[2])


def _reshard(name, full):
    r, c = full.shape
    if name in ROW_SHARDED:
        return full.reshape(NDEV, r // NDEV, c)
    return jnp.moveaxis(full.reshape(r, NDEV, c // NDEV), 1, 0)


def _w_in_to_padded(slabs):
    pieces = []
    for lo, hi, _ in sorted(SEGMENTS, key=lambda s: s[2]):
        for d in range(NDEV):
            a, b = max(lo, d * W_SHARD), min(hi, (d + 1) * W_SHARD)
            if a < b:
                pieces.append(slabs[d, :, a - d * W_SHARD:b - d * W_SHARD])
    pieces.append(jnp.zeros(slabs.shape[1:2] + (NP - C_BA - 16,), slabs.dtype))
    return jnp.concatenate(pieces, axis=-1)


def _w_in_from_padded(g):
    slabs = []
    for d in range(NDEV):
        pieces = []
        for lo, hi, pstart in SEGMENTS:
            a, b = max(lo, d * W_SHARD), min(hi, (d + 1) * W_SHARD)
            if a < b:
                pieces.append(g[:, pstart + a - lo:pstart + b - lo])
        pieces.append(jnp.zeros(g.shape[:1] + (W_SHARD_PAD - W_SHARD,), g.dtype))
        slabs.append(jnp.concatenate(pieces, axis=-1))
    return jnp.stack(slabs)


def _pad_w_in(w):
    return jnp.pad(w, ((0, 0), (0, W_SHARD_PAD - W_SHARD)))


def _lane_row(vec8, offset):
    return jnp.pad(vec8, (offset, LANE - NH - offset))[None]


def kernel(x, norm_g, w_in, a_dw, a_dw_b, a_ln_g, a_ln_b, a_proj, b_conv, b_a_log, b_dt_bias, b_onorm_g, b_proj, c_ln_g, c_ln_b, c_ws, c_bs, c_proj, w_out, final_g, loss_target, m_norm_g, m_w_in, m_a_dw, m_a_dw_b, m_a_ln_g, m_a_ln_b, m_a_proj, m_b_conv, m_b_a_log, m_b_dt_bias, m_b_onorm_g, m_b_proj, m_c_ln_g, m_c_ln_b, m_c_ws, m_c_bs, m_c_proj, m_w_out, m_final_g, v_norm_g, v_w_in, v_a_dw, v_a_dw_b, v_a_ln_g, v_a_ln_b, v_a_proj, v_b_conv, v_b_a_log, v_b_dt_bias, v_b_onorm_g, v_b_proj, v_c_ln_g, v_c_ln_b, v_c_ws, v_c_bs, v_c_proj, v_w_out, v_final_g):
    wts = dict(norm_g=norm_g, w_in=w_in, a_dw=a_dw, a_dw_b=a_dw_b, a_ln_g=a_ln_g, a_ln_b=a_ln_b, a_proj=a_proj,
               b_conv=b_conv, b_a_log=b_a_log, b_dt_bias=b_dt_bias, b_onorm_g=b_onorm_g, b_proj=b_proj,
               c_ln_g=c_ln_g, c_ln_b=c_ln_b, c_ws=c_ws, c_bs=c_bs, c_proj=c_proj, w_out=w_out, final_g=final_g)
    mom = dict(norm_g=m_norm_g, w_in=m_w_in, a_dw=m_a_dw, a_dw_b=m_a_dw_b, a_ln_g=m_a_ln_g, a_ln_b=m_a_ln_b,
               a_proj=m_a_proj, b_conv=m_b_conv, b_a_log=m_b_a_log, b_dt_bias=m_b_dt_bias, b_onorm_g=m_b_onorm_g,
               b_proj=m_b_proj, c_ln_g=m_c_ln_g, c_ln_b=m_c_ln_b, c_ws=m_c_ws, c_bs=m_c_bs, c_proj=m_c_proj,
               w_out=m_w_out, final_g=m_final_g)
    vel = dict(norm_g=v_norm_g, w_in=v_w_in, a_dw=v_a_dw, a_dw_b=v_a_dw_b, a_ln_g=v_a_ln_g, a_ln_b=v_a_ln_b,
               a_proj=v_a_proj, b_conv=v_b_conv, b_a_log=v_b_a_log, b_dt_bias=v_b_dt_bias, b_onorm_g=v_b_onorm_g,
               b_proj=v_b_proj, c_ln_g=v_c_ln_g, c_ln_b=v_c_ln_b, c_ws=v_c_ws, c_bs=v_c_bs, c_proj=v_c_proj,
               w_out=v_w_out, final_g=v_final_g)

    bsz, t, _ = x.shape
    n = bsz * t
    depth = norm_g.shape[0]
    x2d = x.reshape(n, D)
    tgt = loss_target.reshape(n, D)

    def weight_blocks(l):
        return [_pad_w_in(w_in[l].astype(BF16))] + [wts[k][l].astype(BF16) for k in BIG_REST]

    def matmul_weights(w_in_all, *rest):
        got = {k: _unshard(k, g) for k, g in zip(BIG_REST, rest)}
        got['wp'] = _w_in_to_padded(w_in_all)
        return got

    w_in_first, *conv_all = _all_gather(weight_blocks(0)[:1] + [wts[k] for k in SMALL], "gather_first_weights")
    conv_w = {k: jnp.stack([_unshard(k, g[:, l]) for l in range(depth)]) for k, g in zip(SMALL, conv_all)}
    a_dw32 = jnp.pad(conv_w['a_dw'], ((0, 0), (0, 32 - A_K), (0, 0)))
    b_conv8 = jnp.pad(conv_w['b_conv'], ((0, 0), (0, 8 - B_K), (0, 0)))
    bsb = jnp.broadcast_to(c_bs[..., None], c_bs.shape + (SG,))
    full = [{'wp': _w_in_to_padded(w_in_first)}]

    saved = []
    xl = x2d
    for l in range(depth):
        alog, dtb = _lane_row(b_a_log[l], NH), _lane_row(b_dt_bias[l], NH)
        proj, h = _inproj(xl, norm_g[l][None], full[l]['wp'])
        ya, conv_a, *own = _a_fwd(proj, a_dw32[l], a_dw_b[l][None], a_ln_g[l][None], a_ln_b[l][None], bsz, t,
                                  gather=weight_blocks(0)[1:] if l == 0 else ())
        if own:
            full[0].update({k: _unshard(k, g) for k, g in zip(BIG_REST, own)})
        ahead = weight_blocks(l + 1) if l + 1 < depth else []
        qkvn, conv_b, *nxt_small = _bprep_fwd(proj, b_conv8[l], bsz, t, gather=ahead[1:])
        yb, shist, tsave, *nxt_w_in = _delta_fwd(qkvn, proj, alog, dtb, b_onorm_g[l][None], bsz, t, gather=ahead[:1])
        if ahead:
            full.append(matmul_weights(*nxt_w_in, *nxt_small))
        yc = _c_fwd(proj, c_ln_g[l][None], c_ln_b[l][None], c_ws[l], bsb[l], n)
        xn = _merge_fwd(xl, ya, yb, yc, proj, full[l]['a_proj'], full[l]['b_proj'], full[l]['c_proj'], full[l]['w_out'])
        saved.append((xl, proj, h, ya, yb, yc, qkvn, shist, tsave, alog, dtb, conv_a, conv_b))
        xl = xn

    dx, g_final, loss_blk = _loss_head(xl, final_g[None], tgt)
    loss = lax.psum(loss_blk[0, 0], ("x", "y", "c"))

    gfull = {k: [None] * depth for k in WEIGHTS if k != 'final_g'}
    recv = [{} for _ in range(depth)]
    sharded = ['w_in'] + SHARDED_REST
    late = ['w_in', 'b_conv']
    early = [k for k in sharded if k not in late]

    def grad_slabs(l, names):
        return [(_w_in_from_padded(gfull[k][l]) if k == 'w_in' else _reshard(k, gfull[k][l])).astype(BF16) for k in names]

    for l in reversed(range(depth)):
        xl, proj, h, ya, yb, yc, qkvn, shist, tsave, alog, dtb, conv_a, conv_b = saved[l]
        ap, bp, cp, wo = full[l]['a_proj'], full[l]['b_proj'], full[l]['c_proj'], full[l]['w_out']
        dproj, dya, dyb, dyc, dpa, dpb, dpc, merged = _merge_bwd(dx, ya, yb, yc, proj, ap, bp, cp, ap.T, bp.T, cp.T, wo.T)
        gfull['a_proj'][l] = _mm_tn(ya, dpa, "grad_a_proj")
        gfull['b_proj'][l] = _mm_tn(yb, dpb, "grad_b_proj")
        gfull['c_proj'][l] = _mm_tn(yc, dpc, "grad_c_proj")
        gfull['w_out'][l] = _mm_tn(merged, dx, "grad_w_out")
        dproj, g_clg, g_clb, g_cws, g_cbs = _c_bwd(dproj, proj, dyc, c_ln_g[l][None], c_ln_b[l][None], c_ws[l], bsb[l], n)
        dproj, g_adw, g_adb, g_alg, g_alb = _a_bwd(dproj, proj, conv_a, dya, a_dw32[l], a_ln_g[l][None], a_ln_b[l][None],
                                                   bsz, t)
        gfull['a_dw'][l] = g_adw[:A_K]
        riders = ([(l + 1, sharded)] if l + 1 < depth else []) + ([(0, early)] if l == 0 else [])
        dproj, dqkvn, g_alog, g_dt, g_og, *got = _delta_bwd(
            dproj, qkvn, proj, shist, tsave, dyb, alog, dtb, b_onorm_g[l][None], bsz, t,
            scatter=[s for ll, names in riders for s in grad_slabs(ll, names)])
        for ll, names in riders:
            for k in names:
                recv[ll][k] = got.pop(0)
        dproj, g_bconv = _bprep_bwd(dproj, proj, conv_b, dqkvn, b_conv8[l], bsz, t)
        gfull['w_in'][l] = _mm_tn(h, dproj, "grad_w_in")
        gfull['b_conv'][l] = g_bconv[:B_K]
        dx, g_ng, *got = _inproj_bwd(dproj, full[l]['wp'], xl, norm_g[l][None], dx,
                                     scatter=grad_slabs(0, late) if l == 0 else ())
        if got:
            recv[0].update(zip(late, got))
        gfull['norm_g'][l] = g_ng[0]
        gfull['a_dw_b'][l], gfull['a_ln_g'][l], gfull['a_ln_b'][l] = g_adb[0], g_alg[0], g_alb[0]
        gfull['b_a_log'][l], gfull['b_dt_bias'][l] = g_alog[0, NH:2 * NH], g_dt[0, NH:2 * NH]
        gfull['b_onorm_g'][l] = g_og[0]
        gfull['c_ln_g'][l], gfull['c_ln_b'][l] = g_clg[0], g_clb[0]
        gfull['c_ws'][l], gfull['c_bs'][l] = g_cws, g_cbs[:, :, 0]
    grad_x = dx.reshape(bsz, t, D)

    outs_w = [_reduce_adamw(recv[l]['w_in'], _pad_w_in(w_in[l]), _pad_w_in(m_w_in[l]), _pad_w_in(v_w_in[l]), "adamw_w_in")
              for l in range(depth)]
    outs_s = _reduce_adamw_leaves([[recv[l][k] for l in range(depth)] for k in SHARDED_REST],
                                  [wts[k] for k in SHARDED_REST], [mom[k] for k in SHARDED_REST],
                                  [vel[k] for k in SHARDED_REST], "adamw_sharded")

    def upto3d(a):
        return a[None] if a.ndim == 1 else a.reshape((-1,) + a.shape[-2:]) if a.ndim > 3 else a

    grepl = [upto3d(jnp.stack(gfull[k]) if k != 'final_g' else g_final[0]) for k in REPL]
    outs_r = _reduce_adamw_leaves(_all_gather(grepl, "gather_replicated_grads"), [upto3d(wts[k]) for k in REPL],
                                  [upto3d(mom[k]) for k in REPL], [upto3d(vel[k]) for k in REPL], "adamw_replicated")

    res = []
    for j in range(4):
        leaves = {k: outs_r[j][i].reshape(wts[k].shape) for i, k in enumerate(REPL)}
        leaves.update({k: outs_s[j][i] for i, k in enumerate(SHARDED_REST)})
        leaves['w_in'] = jnp.stack([outs_w[l][j][:, :W_SHARD] for l in range(depth)])
        res.append([leaves[k] for k in WEIGHTS])
    grads, deltas, new_m, new_v = res
    return (loss, grad_x, *grads, *deltas, *new_m, *new_v)
```

```python
import jax
import jax.numpy as jnp
from jax import lax
from jax.experimental import pallas as pl
from jax.experimental.pallas import tpu as pltpu

F32 = jnp.float32
BF16 = jnp.bfloat16
SDS = jax.ShapeDtypeStruct
MESH = pl.DeviceIdType.MESH

NDEV = 8
D = 1024
EPS = 1e-6
LANE = 128

C_Q, C_K, C_V = 0, 1024, 2048
C_G = 3072
C_A = 6144
C_C = 7680
C_BZ = 9216
C_BA = 10240
NP = 10368
N_IN = 10256
SEGMENTS = ((0, 1536, C_A), (1536, 4608, C_Q), (4608, 5632, C_BZ), (5632, 5648, C_BA), (5648, 7184, C_C), (7184, 10256, C_G))
W_SHARD = N_IN // NDEV
W_SHARD_PAD = 1408

A_K, A_H, A_RB = 31, 32, 32
B_K, B_H, B_RB, B_CB = 4, 8, 64, 512
CH = 64
SG = 128
NH = 8

ADAM_LR, ADAM_B1, ADAM_B2, ADAM_EPS, ADAM_WD, ADAM_STEP = 0.001, 0.9, 0.999, 1e-08, 0.01, 10

WEIGHTS = ['norm_g', 'w_in', 'a_dw', 'a_dw_b', 'a_ln_g', 'a_ln_b', 'a_proj', 'b_conv', 'b_a_log', 'b_dt_bias',
           'b_onorm_g', 'b_proj', 'c_ln_g', 'c_ln_b', 'c_ws', 'c_bs', 'c_proj', 'w_out', 'final_g']
SHARDED_REST = ['a_dw', 'a_proj', 'b_conv', 'b_proj', 'c_proj', 'w_out']
REPL = [n for n in WEIGHTS if n != 'w_in' and n not in SHARDED_REST]
BIG_REST = ['a_proj', 'b_proj', 'c_proj', 'w_out']
SMALL = ['a_dw', 'b_conv']
ROW_SHARDED = ('b_proj', 'w_out')

NN = ((1,), (0,))
NT = ((1,), (1,))
TN = ((0,), (0,))


def _tile(n, pref):
    return pref if (n >= pref and n % pref == 0) else n


def _cp(vmem_mb):
    return pltpu.CompilerParams(vmem_limit_bytes=vmem_mb * 2 ** 20)


def _full(shape):
    nd = len(shape)
    return pl.BlockSpec(shape, lambda *_: (0,) * nd)


def _resident(shape):
    nd = len(shape)
    return pl.BlockSpec(shape, lambda *_: (0,) * nd, pipeline_mode=pl.Buffered(1))


def _rms(x, g):
    return x * lax.rsqrt(jnp.mean(x * x, axis=-1, keepdims=True) + EPS) * g


def _softplus(x):
    return jnp.maximum(x, 0.0) + jnp.log1p(jnp.exp(-jnp.abs(x)))


def _bdot(a, b):
    return jnp.dot(a.astype(BF16), b.astype(BF16), preferred_element_type=F32)


def _mm(a, b, dims):
    return lax.dot_general(a, b, (dims, ((), ())), preferred_element_type=F32)


def _all_gather(xs, name):
    nops = len(xs)

    def body(*refs):
        x_refs, out_refs = refs[:nops], refs[nops:2 * nops]
        send_sems, recv_sems, local_sems = refs[2 * nops:]
        x, y, cc = lax.axis_index("x"), lax.axis_index("y"), lax.axis_index("c")
        me, sibling = (x, y, cc), (x, y, 1 - cc)
        chips = [(1 - x, y), (x, 1 - y), (1 - x, 1 - y)]

        def slot(t, px, py, pc):
            return out_refs[t].at[4 * px + 2 * py + pc]

        def copy(t, k, block, to, src=None):
            return pltpu.make_async_remote_copy(
                src_ref=slot(t, *block) if src is None else src, dst_ref=slot(t, *block),
                send_sem=send_sems.at[7 * t + k], recv_sem=recv_sems.at[7 * t + k], device_id=to, device_id_type=MESH)

        ops = range(nops)
        mine = [pltpu.make_async_copy(x_refs[t], slot(t, *me), local_sems.at[t]) for t in ops]
        for cp in mine:
            cp.start()
        first = [copy(t, 0, me, sibling, src=x_refs[t]) for t in ops]
        first += [copy(t, 1 + j, me, (*chip, cc), src=x_refs[t]) for j, chip in enumerate(chips) for t in ops]
        for cp in first:
            cp.start()
        passed = []
        for j, chip in enumerate(chips):
            for t in ops:
                copy(t, 1 + j, (*chip, cc), me).wait_recv()
                fwd = copy(t, 4 + j, (*chip, cc), sibling)
                fwd.start()
                passed.append(fwd)
        for t in ops:
            copy(t, 0, sibling, me).wait_recv()
        for j, chip in enumerate(chips):
            for t in ops:
                copy(t, 4 + j, (*chip, 1 - cc), me).wait_recv()
        for cp in first + passed:
            cp.wait_send()
        for cp in mine:
            cp.wait()

    hbm = pl.BlockSpec(memory_space=pltpu.HBM)
    return pl.pallas_call(
        body, name=name, out_shape=[SDS((NDEV,) + a.shape, a.dtype) for a in xs],
        in_specs=[hbm] * nops, out_specs=[hbm] * nops,
        scratch_shapes=[pltpu.SemaphoreType.DMA((7 * nops,)), pltpu.SemaphoreType.DMA((7 * nops,)),
                        pltpu.SemaphoreType.DMA((nops,))],
    )(*xs)


def _exchange_sems(nops):
    return [pltpu.SemaphoreType.DMA((7 * nops,)), pltpu.SemaphoreType.DMA((7 * nops,)), pltpu.SemaphoreType.DMA((nops,))]


class _Exchange:
    def __init__(self, scatter, in_refs, out_refs, send_sems, recv_sems, local_sems):
        x, y, cc = lax.axis_index("x"), lax.axis_index("y"), lax.axis_index("c")
        me = 4 * x + 2 * y + cc
        nops = len(in_refs)
        self.local = [pltpu.make_async_copy(in_refs[t].at[me] if scatter else in_refs[t], out_refs[t].at[me],
                                            local_sems.at[t]) for t in range(nops)]
        self.sends, self.recvs = [], []
        for k in range(1, NDEV):
            px = 1 - x if k & 4 else x
            py = 1 - y if k & 2 else y
            pc = 1 - cc if k & 1 else cc
            peer = 4 * px + 2 * py + pc
            for t in range(nops):
                sem = 7 * t + k - 1
                self.sends.append(pltpu.make_async_remote_copy(
                    src_ref=in_refs[t].at[peer] if scatter else in_refs[t], dst_ref=out_refs[t].at[me],
                    send_sem=send_sems.at[sem], recv_sem=recv_sems.at[sem], device_id=(px, py, pc), device_id_type=MESH))
                self.recvs.append(pltpu.make_async_remote_copy(
                    src_ref=in_refs[t].at[me] if scatter else in_refs[t], dst_ref=out_refs[t].at[peer],
                    send_sem=send_sems.at[sem], recv_sem=recv_sems.at[sem], device_id=(px, py, pc), device_id_type=MESH))

    def start(self):
        for cp in self.local + self.sends:
            cp.start()

    def wait(self):
        for cp in self.recvs:
            cp.wait_recv()
        for cp in self.sends:
            cp.wait_send()
        for cp in self.local:
            cp.wait()


def _inproj(x2d, g_row, wp):
    n = x2d.shape[0]
    tm, tn = _tile(n, 1024), 1152

    def body(x_ref, g_ref, w_ref, proj_ref, h_ref, hs):
        @pl.when(pl.program_id(1) == 0)
        def _():
            h = _rms(x_ref[...], g_ref[...]).astype(BF16)
            hs[...] = h
            h_ref[...] = h

        proj_ref[...] = jnp.dot(hs[...], w_ref[...], preferred_element_type=F32)

    return pl.pallas_call(
        body, name="inproj", grid=(n // tm, NP // tn),
        in_specs=[pl.BlockSpec((tm, D), lambda i, j: (i, 0)), _full((1, D)), pl.BlockSpec((D, tn), lambda i, j: (0, j))],
        out_specs=[pl.BlockSpec((tm, tn), lambda i, j: (i, j)), pl.BlockSpec((tm, D), lambda i, j: (i, 0))],
        out_shape=[SDS((n, NP), F32), SDS((n, D), BF16)],
        scratch_shapes=[pltpu.VMEM((tm, D), BF16)], compiler_params=_cp(48),
    )(x2d, g_row, wp)


def _mm_tn(a, b, name):
    nn, m = a.shape
    k = b.shape[1]
    tk = 1152 if k % 1152 == 0 else _tile(k, 1024)
    tn = _tile(nn, 1024)

    def body(a_ref, b_ref, o_ref):
        p = _mm(a_ref[...].astype(BF16), b_ref[...].astype(BF16), TN)

        @pl.when(pl.program_id(1) == 0)
        def _():
            o_ref[...] = p

        @pl.when(pl.program_id(1) > 0)
        def _():
            o_ref[...] += p

    return pl.pallas_call(
        body, name=name, grid=(k // tk, nn // tn),
        in_specs=[pl.BlockSpec((tn, m), lambda j, t: (t, 0)), pl.BlockSpec((tn, tk), lambda j, t: (t, j))],
        out_specs=pl.BlockSpec((m, tk), lambda j, t: (0, j)),
        out_shape=SDS((m, k), F32), compiler_params=_cp(48),
    )(a, b)


def _inproj_bwd(dproj, wp, x2d, g_row, dxo, scatter=()):
    n = x2d.shape[0]
    tm, tk = _tile(n, 1024), 1152
    nk = NP // tk
    ns = len(scatter)

    def body(dp_ref, w_ref, x_ref, g_ref, dxo_ref, *rest):
        s_in, (dx_ref, dg_ref), s_out, (acc, *sems) = rest[:ns], rest[ns:ns + 2], rest[ns + 2:2 * ns + 2], rest[2 * ns + 2:]
        i, k = pl.program_id(0), pl.program_id(1)
        _ride_along(_Exchange(True, s_in, s_out, *sems) if ns else None,
                    (i == 0) & (k == 0), (i == n // tm - 1) & (k == nk - 1))
        p = _mm(dp_ref[...], w_ref[...], NT)

        @pl.when(k == 0)
        def _():
            acc[...] = p

        @pl.when(k > 0)
        def _():
            acc[...] += p

        @pl.when(k == nk - 1)
        def _():
            _, vjp = jax.vjp(_rms, x_ref[...], g_ref[...])
            dx, dg = vjp(acc[...])
            dx_ref[...] = dxo_ref[...] + dx

            @pl.when(i == 0)
            def _():
                dg_ref[...] = dg

            @pl.when(i > 0)
            def _():
                dg_ref[...] += dg

    hbm = pl.BlockSpec(memory_space=pltpu.HBM)
    return pl.pallas_call(
        body, name="inproj_bwd_exchange" if ns else "inproj_bwd", grid=(n // tm, nk),
        in_specs=[pl.BlockSpec((tm, tk), lambda i, k: (i, k)), pl.BlockSpec((D, tk), lambda i, k: (0, k)),
                  pl.BlockSpec((tm, D), lambda i, k: (i, 0)), _full((1, D)),
                  pl.BlockSpec((tm, D), lambda i, k: (i, 0))] + [hbm] * ns,
        out_specs=[pl.BlockSpec((tm, D), lambda i, k: (i, 0)), _full((1, D))] + [hbm] * ns,
        out_shape=[SDS((n, D), F32), SDS((1, D), F32)] + [SDS(a.shape, a.dtype) for a in scatter],
        scratch_shapes=[pltpu.VMEM((tm, D), F32)] + (_exchange_sems(ns) if ns else []), compiler_params=_cp(56),
    )(dproj, wp, x2d, g_row, dxo, *scatter)


def _loss_head(x2d, g_row, tgt):
    n = x2d.shape[0]
    tm = _tile(n, 512)

    def body(x_ref, g_ref, t_ref, dx_ref, dg_ref, loss_ref):
        i = pl.program_id(0)
        y, vjp = jax.vjp(_rms, x_ref[...], g_ref[...])
        err = y - t_ref[...]
        part = 0.5 * jnp.sum(jnp.mean(err * err, axis=-1, keepdims=True), axis=0, keepdims=True)
        dx, dg = vjp(err * (1.0 / D))
        dx_ref[...] = dx
        lb = jnp.broadcast_to(part, (8, LANE))

        @pl.when(i == 0)
        def _():
            dg_ref[...] = dg
            loss_ref[...] = lb

        @pl.when(i > 0)
        def _():
            dg_ref[...] += dg
            loss_ref[...] += lb

    return pl.pallas_call(
        body, name="loss_head", grid=(n // tm,),
        in_specs=[pl.BlockSpec((tm, D), lambda i: (i, 0)), _full((1, D)), pl.BlockSpec((tm, D), lambda i: (i, 0))],
        out_specs=[pl.BlockSpec((tm, D), lambda i: (i, 0)), _full((1, D)), _full((8, LANE))],
        out_shape=[SDS((n, D), F32), SDS((1, D), F32), SDS((8, LANE), F32)], compiler_params=_cp(40),
    )(x2d, g_row, tgt)


def _conv_rows(dst_ref, read, w_ref, offs, nrows, rb, cb, bias=None):
    ncols = dst_ref.shape[1]
    for c0 in range(0, ncols, cb):
        for r0 in range(0, nrows, rb):
            rows = min(rb, nrows - r0)
            acc = None if bias is None else bias[:, c0:c0 + cb]
            for j, off in enumerate(offs):
                term = w_ref[j:j + 1, c0:c0 + cb] * read(off + r0, rows, c0, cb)
                acc = term if acc is None else acc + term
            dst_ref[r0:r0 + rows, c0:c0 + cb] = acc


def _fwd_offsets(kw, halo):
    return [halo - (kw - 1) + j for j in range(kw)]


def _bwd_offsets(kw):
    return [kw - 1 - j for j in range(kw)]


def _conv_bwd_w(dc, ext_ref, gw_ref, kw, halo, tt):
    for j in range(kw):
        gw_ref[j:j + 1, :] += jnp.sum(dc * ext_ref[pl.ds(halo - (kw - 1) + j, tt), :], axis=0, keepdims=True)


SUB = 8


def _fill_shifts(sh_ref, rows):
    for s in range(1, SUB):
        sh_ref[s, 0:rows, :] = sh_ref[0, pl.ds(s, rows), :]


def _tap(sh_ref, off, rows):
    return sh_ref[off % SUB, pl.ds(off - off % SUB, rows), :]


def _conv_bwd_w_sh(dc, ext_sh, gw_ref, kw, halo, tt):
    for j in range(kw):
        gw_ref[j:j + 1, :] += jnp.sum(dc * _tap(ext_sh, halo - (kw - 1) + j, tt), axis=0, keepdims=True)


def _a_post(c, z, g, b):
    mu = jnp.mean(c, axis=-1, keepdims=True)
    var = jnp.mean(jnp.square(c - mu), axis=-1, keepdims=True)
    a = (c - mu) * lax.rsqrt(var + EPS) * g + b
    return jax.nn.silu(a) * jax.nn.silu(z)


def _a_fwd(proj, dw, b_row, lg, lb, bsz, t, gather=()):
    n = bsz * t
    tt = _tile(t, 256)
    nt = t // tt
    cb = C_A // 512
    ng = len(gather)

    def body(v_ref, g_ref, z_ref, vh_ref, gh_ref, w_ref, b_ref, lg_ref, lb_ref, *rest):
        g_in, (y_ref, c_ref), g_out, (ext, *sems) = rest[:ng], rest[ng:ng + 2], rest[ng + 2:2 * ng + 2], rest[2 * ng + 2:]
        b, i = pl.program_id(0), pl.program_id(1)
        _ride_along(_Exchange(False, g_in, g_out, *sems) if ng else None,
                    (b == 0) & (i == 0), (b == bsz - 1) & (i == nt - 1))
        ext[0, 0:A_H, :] = jnp.where(i > 0, vh_ref[...] * jax.nn.sigmoid(gh_ref[...]), 0.0)
        ext[0, A_H:, :] = v_ref[...] * jax.nn.sigmoid(g_ref[...])
        _fill_shifts(ext, tt + A_H - SUB)
        _conv_rows(c_ref, lambda off, rows, c0, cols: _tap(ext, off, rows), w_ref, _fwd_offsets(A_K, A_H), tt, A_RB, 512,
                   bias=b_ref[...])
        y_ref[...] = _a_post(c_ref[...], z_ref[...], lg_ref[...], lb_ref[...]).astype(BF16)

    def row(b, i):
        return b * nt + i

    def halo(b, i):
        return jnp.maximum((b * t + i * tt) // A_H - 1, 0)

    hbm = pl.BlockSpec(memory_space=pltpu.HBM)
    return pl.pallas_call(
        body, name="a_fwd_gather" if ng else "a_fwd", grid=(bsz, nt),
        in_specs=[pl.BlockSpec((tt, 512), lambda b, i: (row(b, i), cb)),
                  pl.BlockSpec((tt, 512), lambda b, i: (row(b, i), cb + 1)),
                  pl.BlockSpec((tt, 512), lambda b, i: (row(b, i), cb + 2)),
                  pl.BlockSpec((A_H, 512), lambda b, i: (halo(b, i), cb)),
                  pl.BlockSpec((A_H, 512), lambda b, i: (halo(b, i), cb + 1)),
                  _full((32, 512)), _full((1, 512)), _full((1, 512)), _full((1, 512))] + [hbm] * ng,
        out_specs=[pl.BlockSpec((tt, 512), lambda b, i: (row(b, i), 0))] * 2 + [hbm] * ng,
        out_shape=[SDS((n, 512), BF16), SDS((n, 512), F32)] + [SDS((NDEV,) + a.shape, a.dtype) for a in gather],
        scratch_shapes=[pltpu.VMEM((SUB, tt + A_H, 512), F32)] + (_exchange_sems(ng) if ng else []),
        compiler_params=_cp(40),
    )(proj, proj, proj, proj, proj, dw, b_row, lg, lb, *gather)


def _a_bwd(dproj, proj, conv, dy, dw, lg, lb, bsz, t):
    n = bsz * t
    tt = _tile(t, 256)
    nt = t // tt
    cb = C_A // 512

    def body(dp_any, v_ref, g_ref, z_ref, vh_ref, gh_ref, c_ref, dy_ref, w_ref, lg_ref, lb_ref,
             dp_ref, gw_ref, gb_ref, glg_ref, glb_ref, ext, dcp, dae, carry):
        b, i = pl.program_id(0), pl.program_id(1)
        ti = nt - 1 - i

        @pl.when((b == 0) & (i == 0))
        def _():
            gw_ref[...] = jnp.zeros_like(gw_ref)
            gb_ref[...] = jnp.zeros_like(gb_ref)
            glg_ref[...] = jnp.zeros_like(glg_ref)
            glb_ref[...] = jnp.zeros_like(glb_ref)

        @pl.when(i == 0)
        def _():
            carry[...] = jnp.zeros_like(carry)

        val, glu = v_ref[...], g_ref[...]
        sg = jax.nn.sigmoid(glu)
        ext[0, 0:A_H, :] = jnp.where(ti > 0, vh_ref[...] * jax.nn.sigmoid(gh_ref[...]), 0.0)
        ext[0, A_H:, :] = val * sg
        _fill_shifts(ext, tt + A_H - SUB)
        _, vjp = jax.vjp(_a_post, c_ref[...], z_ref[...], lg_ref[...], lb_ref[...])
        dc, dz, dlg, dlb = vjp(dy_ref[...])
        gb_ref[...] += jnp.sum(dc, axis=0, keepdims=True)
        glg_ref[...] += dlg
        glb_ref[...] += dlb
        dcp[0, 0:A_H, :] = jnp.zeros((A_H, 512), F32)
        dcp[0, A_H:A_H + tt, :] = dc
        dcp[0, A_H + tt:, :] = jnp.zeros((A_H, 512), F32)
        _fill_shifts(dcp, tt + 2 * A_H - SUB)
        _conv_bwd_w_sh(dc, ext, gw_ref, A_K, A_H, tt)
        _conv_rows(dae, lambda off, rows, c0, cols: _tap(dcp, off, rows), w_ref, _bwd_offsets(A_K), tt + A_H, A_RB, 512)
        dae[tt:tt + A_H, :] += carry[...]
        carry[...] = dae[0:A_H, :]
        da = dae[A_H:, :]
        dp_ref[:, 0:512] = (da * sg).astype(BF16)
        dp_ref[:, 512:1024] = (da * val * sg * (1.0 - sg)).astype(BF16)
        dp_ref[:, 1024:1536] = dz.astype(BF16)

    def row(b, i):
        return b * nt + (nt - 1 - i)

    def halo(b, i):
        return jnp.maximum((b * t + (nt - 1 - i) * tt) // A_H - 1, 0)

    outs = pl.pallas_call(
        body, name="a_bwd", grid=(bsz, nt),
        in_specs=[pl.BlockSpec(memory_space=pl.ANY),
                  pl.BlockSpec((tt, 512), lambda b, i: (row(b, i), cb)),
                  pl.BlockSpec((tt, 512), lambda b, i: (row(b, i), cb + 1)),
                  pl.BlockSpec((tt, 512), lambda b, i: (row(b, i), cb + 2)),
                  pl.BlockSpec((A_H, 512), lambda b, i: (halo(b, i), cb)),
                  pl.BlockSpec((A_H, 512), lambda b, i: (halo(b, i), cb + 1)),
                  pl.BlockSpec((tt, 512), lambda b, i: (row(b, i), 0)),
                  pl.BlockSpec((tt, 512), lambda b, i: (row(b, i), 0)),
                  _full((32, 512)), _full((1, 512)), _full((1, 512))],
        out_specs=[pl.BlockSpec((tt, 1536), lambda b, i: (row(b, i), C_A // 1536)),
                   _full((32, 512)), _full((1, 512)), _full((1, 512)), _full((1, 512))],
        out_shape=[SDS((n, NP), BF16), SDS((32, 512), F32), SDS((1, 512), F32), SDS((1, 512), F32), SDS((1, 512), F32)],
        input_output_aliases={0: 0},
        scratch_shapes=[pltpu.VMEM((SUB, tt + A_H, 512), F32), pltpu.VMEM((SUB, tt + 2 * A_H, 512), F32),
                        pltpu.VMEM((tt + A_H, 512), F32), pltpu.VMEM((A_H, 512), F32)],
        compiler_params=_cp(48),
    )(dproj, proj, proj, proj, proj, proj, conv, dy, dw, lg, lb)
    return outs


def _b_post(blocks):
    out = []
    for idx, c in enumerate(blocks):
        s = jax.nn.silu(c)
        if idx < 2 * NH:
            s = s * lax.rsqrt(jnp.sum(s * s, axis=-1, keepdims=True) + EPS)
            if idx < NH:
                s = s * (LANE ** -0.5)
        out.append(s)
    return out


def _bprep_fwd(proj, wconv, bsz, t, gather=()):
    n = bsz * t
    tt = _tile(t, 256)
    nt = t // tt
    ng = len(gather)

    def body(x_ref, xh_ref, w_ref, *rest):
        g_in, (o_ref, c_ref), g_out, (ext, *sems) = rest[:ng], rest[ng:ng + 2], rest[ng + 2:2 * ng + 2], rest[2 * ng + 2:]
        b, i = pl.program_id(0), pl.program_id(1)
        _ride_along(_Exchange(False, g_in, g_out, *sems) if ng else None,
                    (b == 0) & (i == 0), (b == bsz - 1) & (i == nt - 1))
        ext[0:B_H, :] = jnp.where(i > 0, xh_ref[...], 0.0)
        ext[B_H:, :] = x_ref[...]
        _conv_rows(c_ref, lambda off, rows, c0, cols: ext[pl.ds(off, rows), c0:c0 + cols], w_ref,
                   _fwd_offsets(B_K, B_H), tt, B_RB, B_CB)
        outs = _b_post([c_ref[:, LANE * j:LANE * (j + 1)] for j in range(3 * NH)])
        for j, o in enumerate(outs):
            o_ref[:, LANE * j:LANE * (j + 1)] = o

    hbm = pl.BlockSpec(memory_space=pltpu.HBM)
    return pl.pallas_call(
        body, name="bprep_fwd_gather" if ng else "bprep_fwd", grid=(bsz, nt),
        in_specs=[pl.BlockSpec((tt, 3072), lambda b, i: (b * nt + i, 0)),
                  pl.BlockSpec((B_H, 3072), lambda b, i: (jnp.maximum((b * t + i * tt) // B_H - 1, 0), 0)),
                  _full((8, 3072))] + [hbm] * ng,
        out_specs=[pl.BlockSpec((tt, 3072), lambda b, i: (b * nt + i, 0))] * 2 + [hbm] * ng,
        out_shape=[SDS((n, 3072), F32)] * 2 + [SDS((NDEV,) + a.shape, a.dtype) for a in gather],
        scratch_shapes=[pltpu.VMEM((tt + B_H, 3072), F32)] + (_exchange_sems(ng) if ng else []),
        compiler_params=_cp(56),
    )(proj, proj, wconv, *gather)


def _bprep_bwd(dproj, proj, conv, dqkvn, wconv, bsz, t):
    n = bsz * t
    tt = _tile(t, 256)
    nt = t // tt

    def body(dp_any, x_ref, xh_ref, c_ref, dq_ref, w_ref, dp_ref, gw_ref, ext, dcp, dae, carry):
        b, i = pl.program_id(0), pl.program_id(1)
        ti = nt - 1 - i

        @pl.when((b == 0) & (i == 0))
        def _():
            gw_ref[...] = jnp.zeros_like(gw_ref)

        @pl.when(i == 0)
        def _():
            carry[...] = jnp.zeros_like(carry)

        ext[0:B_H, :] = jnp.where(ti > 0, xh_ref[...], 0.0)
        ext[B_H:, :] = x_ref[...]
        _, vjp = jax.vjp(_b_post, [c_ref[:, LANE * j:LANE * (j + 1)] for j in range(3 * NH)])
        (dcs,) = vjp([dq_ref[:, LANE * j:LANE * (j + 1)] for j in range(3 * NH)])
        dcp[0:B_H, :] = jnp.zeros((B_H, 3072), F32)
        for j, dcj in enumerate(dcs):
            dcp[B_H:B_H + tt, LANE * j:LANE * (j + 1)] = dcj
        dcp[B_H + tt:, :] = jnp.zeros((B_H, 3072), F32)
        _conv_bwd_w(dcp[B_H:B_H + tt, :], ext, gw_ref, B_K, B_H, tt)
        _conv_rows(dae, lambda off, rows, c0, cols: dcp[pl.ds(off, rows), c0:c0 + cols], w_ref,
                   _bwd_offsets(B_K), tt + B_H, B_RB, B_CB)
        dae[tt:tt + B_H, :] += carry[...]
        carry[...] = dae[0:B_H, :]
        dp_ref[...] = dae[B_H:, :].astype(BF16)

    def row(b, i):
        return b * nt + (nt - 1 - i)

    return pl.pallas_call(
        body, name="bprep_bwd", grid=(bsz, nt),
        in_specs=[pl.BlockSpec(memory_space=pl.ANY),
                  pl.BlockSpec((tt, 3072), lambda b, i: (row(b, i), 0)),
                  pl.BlockSpec((B_H, 3072), lambda b, i: (jnp.maximum((b * t + (nt - 1 - i) * tt) // B_H - 1, 0), 0)),
                  pl.BlockSpec((tt, 3072), lambda b, i: (row(b, i), 0)),
                  pl.BlockSpec((tt, 3072), lambda b, i: (row(b, i), 0)),
                  _full((8, 3072))],
        out_specs=[pl.BlockSpec((tt, 3072), lambda b, i: (row(b, i), 0)), _full((8, 3072))],
        out_shape=[SDS((n, NP), BF16), SDS((8, 3072), F32)],
        input_output_aliases={0: 0},
        scratch_shapes=[pltpu.VMEM((tt + B_H, 3072), F32), pltpu.VMEM((tt + 2 * B_H, 3072), F32),
                        pltpu.VMEM((tt + B_H, 3072), F32), pltpu.VMEM((B_H, 3072), F32)],
        compiler_params=_cp(56),
    )(dproj, proj, proj, conv, dqkvn, wconv)


def _split2(a):
    hi = a.astype(BF16)
    return hi, (a - hi.astype(F32)).astype(BF16)


def _split3(a):
    p1 = a.astype(BF16)
    r1 = a - p1.astype(F32)
    p2 = r1.astype(BF16)
    return p1, p2, (r1 - p2.astype(F32)).astype(BF16)


def _dot3_raw(a, b, dims):
    a1, a2 = _split2(a)
    b1, b2 = _split2(b)
    return _mm(a1, b1, dims) + (_mm(a1, b2, dims) + _mm(a2, b1, dims))


def _dot6(a, b, dims):
    a1, a2, a3 = _split3(a)
    b1, b2, b3 = _split3(b)
    return (_mm(a1, b1, dims) + (_mm(a1, b2, dims) + _mm(a2, b1, dims))
            + (_mm(a1, b3, dims) + _mm(a2, b2, dims) + _mm(a3, b1, dims)))


def _unit_lower_inverse_raw(lmats):
    r = lax.broadcasted_iota(jnp.int32, (CH, CH), 0)
    c = lax.broadcasted_iota(jnp.int32, (CH, CH), 1)
    eye = (r == c).astype(F32)
    blk = jnp.right_shift(r, 4) == jnp.right_shift(c, 4)
    dm = [jnp.where(blk, x, 0.0) for x in lmats]
    om = [a - b for a, b in zip(lmats, dm)]
    d2 = [_dot3_raw(x, x, NN) for x in dm]
    d4 = [_dot3_raw(x, x, NN) for x in d2]
    d8 = [_dot3_raw(x, x, NN) for x in d4]
    p = [_dot3_raw(eye - a, eye + b, NN) for a, b in zip(dm, d2)]
    p = [_dot3_raw(a, eye + b, NN) for a, b in zip(p, d4)]
    p = [_dot3_raw(a, eye + b, NN) for a, b in zip(p, d8)]
    m = [_dot3_raw(a, b, NN) for a, b in zip(p, om)]
    m2 = [_dot3_raw(x, x, NN) for x in m]
    t = [_dot3_raw(eye - a, eye + b, NN) for a, b in zip(m, m2)]
    return [_dot3_raw(a, b, NN) for a, b in zip(t, p)]


@jax.custom_vjp
def _unit_lower_inverse(lmats):
    return _unit_lower_inverse_raw(lmats)


def _unit_lower_inverse_fwd(lmats):
    tinv = _unit_lower_inverse_raw(lmats)
    return tinv, tinv


def _unit_lower_inverse_bwd(tinv, gs):
    x = [_dot6(t, g, TN) for t, g in zip(tinv, gs)]
    return ([-_dot6(a, t, NT) for a, t in zip(x, tinv)],)


_unit_lower_inverse.defvjp(_unit_lower_inverse_fwd, _unit_lower_inverse_bwd)


@jax.custom_vjp
def _saved_unit_lower_inverse(lmats, tinv):
    del lmats
    return tinv


def _saved_unit_lower_inverse_fwd(lmats, tinv):
    del lmats
    return tinv, tinv


def _saved_unit_lower_inverse_bwd(tinv, gs):
    (dl,) = _unit_lower_inverse_bwd(tinv, gs)
    return dl, [jnp.zeros_like(t) for t in tinv]


_saved_unit_lower_inverse.defvjp(_saved_unit_lower_inverse_fwd, _saved_unit_lower_inverse_bwd)


def _tri(lower):
    r = lax.broadcasted_iota(jnp.int32, (CH, CH), 0)
    c = lax.broadcasted_iota(jnp.int32, (CH, CH), 1)
    return ((r >= c) if lower else (r <= c)).astype(BF16)


def _tri_dot(x, lower, dims, tri_first):
    p1, p2, p3 = _split3(x)
    tri = _tri(lower)
    if tri_first:
        return _mm(tri, p1, dims) + (_mm(tri, p2, dims) + _mm(tri, p3, dims))
    return _mm(p1, tri, dims) + (_mm(p2, tri, dims) + _mm(p3, tri, dims))


@jax.custom_vjp
def _cumsum_rows(x):
    return _tri_dot(x, True, NN, True)


def _cumsum_rows_fwd(x):
    return _tri_dot(x, True, NN, True), None


def _cumsum_rows_bwd(_, g):
    return (_tri_dot(g, True, TN, True),)


_cumsum_rows.defvjp(_cumsum_rows_fwd, _cumsum_rows_bwd)


@jax.custom_vjp
def _cumsum_rows_t(x):
    return _tri_dot(x, False, TN, False)


def _cumsum_rows_t_fwd(x):
    return _tri_dot(x, False, TN, False), None


def _cumsum_rows_t_bwd(_, g):
    return (_tri_dot(g, False, NT, True),)


_cumsum_rows_t.defvjp(_cumsum_rows_t_fwd, _cumsum_rows_t_bwd)


def _delta_chunk(ss, qs, ks, vs, bas, zs, alog, dtb, og, tsaved=None):
    heads = range(len(ss))
    lane = lax.broadcasted_iota(jnp.int32, (1, LANE), 1)
    r = lax.broadcasted_iota(jnp.int32, (CH, CH), 0)
    c = lax.broadcasted_iota(jnp.int32, (CH, CH), 1)
    ri = lax.broadcasted_iota(jnp.int32, (CH, 1), 0)
    incl, strict = r >= c, r > c

    def pick(x, h):
        return jnp.sum(jnp.where(lane == h, x, 0.0), axis=-1, keepdims=True)

    beta = [jax.nn.sigmoid(pick(bas[e // NH], e % NH)) for e in heads]
    g = [-jnp.exp(pick(alog, e % NH + NH)) * _softplus(pick(bas[e // NH], e % NH + NH) + pick(dtb, e % NH + NH))
         for e in heads]
    gb = [jnp.broadcast_to(x, (CH, CH)) for x in g]
    gca = [_cumsum_rows(x) for x in gb]
    gcr = [_cumsum_rows_t(x) for x in gb]
    gc = [jnp.sum(jnp.where(c == 0, x, 0.0), axis=-1, keepdims=True) for x in gca]
    gl = [jnp.sum(jnp.where(ri == CH - 1, x, 0.0), axis=0, keepdims=True) for x in gc]
    diff = [a - b for a, b in zip(gca, gcr)]
    gam_s = [jnp.where(strict, jnp.exp(jnp.where(strict, x, 0.0)), 0.0) for x in diff]
    gam_i = [jnp.where(incl, jnp.exp(jnp.where(incl, x, 0.0)), 0.0) for x in diff]

    kk = [_mm(k, k, NT) for k in ks]
    lmats = [beta[h] * kk[h] * gam_s[h] for h in heads]
    tinv = _unit_lower_inverse(lmats) if tsaved is None else _saved_unit_lower_inverse(lmats, tsaved)

    eg = [jnp.exp(x) for x in gc]
    u = [_mm(tinv[h], vs[h] * beta[h], NN) for h in heads]
    w = [_mm(tinv[h], ks[h] * (beta[h] * eg[h]), NN) for h in heads]
    qk = [_mm(qs[h], ks[h], NT) * gam_i[h] for h in heads]
    vn = [u[h] - _mm(w[h], ss[h], NN) for h in heads]
    o = [_mm(qs[h] * eg[h], ss[h], NN) + _mm(qk[h], vn[h], NN) for h in heads]
    sn = [jnp.exp(gl[h]) * ss[h] + _mm(ks[h] * jnp.exp(gl[h] - gc[h]), vn[h], TN) for h in heads]
    y = [_rms(o[h], og) * jax.nn.silu(zs[h]) for h in heads]
    return (sn, y), tinv


def _chain_blocks(ref):
    return [ref[b, :, LANE * h:LANE * (h + 1)] for b in range(ref.shape[0]) for h in range(NH)]


def _ride_along(ex, first, last):
    if ex is None:
        return

    @pl.when(first)
    def _():
        ex.start()

    @pl.when(last)
    def _():
        ex.wait()


def _delta_fwd(qkvn, proj, alog, dtb, og, bsz, t, gather=()):
    n = bsz * t
    nc = t // CH
    ng = len(gather)

    def body(q_ref, k_ref, v_ref, ba_ref, z_ref, al_ref, dt_ref, og_ref, *rest):
        g_in, (y_ref, sh_ref, ti_ref) = rest[:ng], rest[ng:ng + 3]
        g_out, (s_scr, *sems) = rest[ng + 3:2 * ng + 3], rest[2 * ng + 3:]
        ci = pl.program_id(0)
        _ride_along(_Exchange(False, g_in, g_out, *sems) if ng else None, ci == 0, ci == nc - 1)

        @pl.when(ci == 0)
        def _():
            s_scr[...] = jnp.zeros_like(s_scr)

        chains = range(bsz * NH)
        ss = [s_scr[e] for e in chains]
        for e in chains:
            sh_ref[e // NH, e % NH] = ss[e]
        (sn, y), tinv = _delta_chunk(ss, _chain_blocks(q_ref), _chain_blocks(k_ref), _chain_blocks(v_ref),
                                     [ba_ref[b] for b in range(bsz)], _chain_blocks(z_ref),
                                     al_ref[...], dt_ref[...], og_ref[...])
        for e in chains:
            s_scr[e] = sn[e]
            ti_ref[e // NH, e % NH] = tinv[e]
            y_ref[e // NH, :, LANE * (e % NH):LANE * (e % NH + 1)] = y[e].astype(BF16)

    def blk(width, col):
        return pl.BlockSpec((bsz, CH, width), lambda ci: (0, ci, col))

    def per_chunk(rows):
        return pl.BlockSpec((bsz, None, NH, rows, rows), lambda ci: (0, ci, 0, 0, 0))

    hbm = pl.BlockSpec(memory_space=pltpu.HBM)
    qkv3, proj3 = qkvn.reshape(bsz, t, 3 * D), proj.reshape(bsz, t, NP)
    y, *rest = pl.pallas_call(
        body, name="delta_fwd_gather" if ng else "delta_fwd", grid=(nc,),
        in_specs=[blk(D, 0), blk(D, 1), blk(D, 2), blk(LANE, C_BA // LANE), blk(D, C_BZ // D),
                  _full((1, LANE)), _full((1, LANE)), _full((1, LANE))] + [hbm] * ng,
        out_specs=[blk(D, 0), per_chunk(LANE), per_chunk(CH)] + [hbm] * ng,
        out_shape=[SDS((bsz, t, D), BF16), SDS((bsz, nc, NH, LANE, LANE), F32), SDS((bsz, nc, NH, CH, CH), F32)]
        + [SDS((NDEV,) + a.shape, a.dtype) for a in gather],
        scratch_shapes=[pltpu.VMEM((bsz * NH, LANE, LANE), F32)] + (_exchange_sems(ng) if ng else []),
        compiler_params=_cp(48),
    )(qkv3, qkv3, qkv3, proj3, proj3, alog, dtb, og, *gather)
    return (y.reshape(n, D), *rest)


def _delta_bwd(dproj, qkvn, proj, shist, tsaved, dyb, alog, dtb, og, bsz, t, scatter=()):
    n = bsz * t
    nc = t // CH
    ns = len(scatter)
    eg = 1

    def body(dp_any, q_ref, k_ref, v_ref, ba_ref, z_ref, sh_ref, ti_ref, dy_ref, al_ref, dt_ref, og_ref, *rest):
        s_in, (dp_ref, dqkv_ref, gal_ref, gdt_ref, gog_ref) = rest[:ns], rest[ns:ns + 5]
        s_out, (ds_scr, *sems) = rest[ns + 5:2 * ns + 5], rest[2 * ns + 5:]
        bi, ci = pl.program_id(0), pl.program_id(1)
        _ride_along(_Exchange(True, s_in, s_out, *sems) if ns else None,
                    (bi == 0) & (ci == 0), (bi == bsz // eg - 1) & (ci == nc - 1))

        @pl.when((bi == 0) & (ci == 0))
        def _():
            gal_ref[...] = jnp.zeros_like(gal_ref)
            gdt_ref[...] = jnp.zeros_like(gdt_ref)
            gog_ref[...] = jnp.zeros_like(gog_ref)

        @pl.when(ci == 0)
        def _():
            ds_scr[...] = jnp.zeros_like(ds_scr)

        chains = range(eg * NH)
        _, vjp, _ = jax.vjp(_delta_chunk, [sh_ref[e // NH, e % NH] for e in chains], _chain_blocks(q_ref),
                            _chain_blocks(k_ref), _chain_blocks(v_ref), [ba_ref[b] for b in range(eg)],
                            _chain_blocks(z_ref), al_ref[...], dt_ref[...], og_ref[...],
                            [ti_ref[e // NH, e % NH] for e in chains], has_aux=True)
        ds, dq, dk, dv, dba, dz, dal, ddt, dog, _ = vjp(([ds_scr[e] for e in chains], _chain_blocks(dy_ref)))
        gal_ref[...] += dal
        gdt_ref[...] += ddt
        gog_ref[...] += dog
        for b in range(eg):
            dp_ref[b, :, D:D + LANE] = dba[b].astype(BF16)
        for e in chains:
            b, lo = e // NH, LANE * (e % NH)
            ds_scr[e] = ds[e]
            dp_ref[b, :, lo:lo + LANE] = dz[e].astype(BF16)
            dqkv_ref[b, :, lo:lo + LANE] = dq[e]
            dqkv_ref[b, :, D + lo:D + lo + LANE] = dk[e]
            dqkv_ref[b, :, 2 * D + lo:2 * D + lo + LANE] = dv[e]

    def blk(width, col):
        return pl.BlockSpec((eg, CH, width), lambda bi, ci: (bi, nc - 1 - ci, col))

    def per_chunk(rows):
        return pl.BlockSpec((eg, None, NH, rows, rows), lambda bi, ci: (bi, nc - 1 - ci, 0, 0, 0))

    hbm = pl.BlockSpec(memory_space=pltpu.HBM)
    qkv3, proj3 = qkvn.reshape(bsz, t, 3 * D), proj.reshape(bsz, t, NP)
    dproj, dqkvn, *rest = pl.pallas_call(
        body, name="delta_bwd_exchange" if ns else "delta_bwd", grid=(bsz // eg, nc),
        in_specs=[pl.BlockSpec(memory_space=pl.ANY), blk(D, 0), blk(D, 1), blk(D, 2), blk(LANE, C_BA // LANE),
                  blk(D, C_BZ // D), per_chunk(LANE), per_chunk(CH),
                  blk(D, 0), _full((1, LANE)), _full((1, LANE)), _full((1, LANE))] + [hbm] * ns,
        out_specs=[blk(D + LANE, C_BZ // (D + LANE)), blk(3 * D, 0), _full((1, LANE)), _full((1, LANE)),
                   _full((1, LANE))] + [hbm] * ns,
        out_shape=[SDS((bsz, t, NP), BF16), SDS((bsz, t, 3 * D), F32), SDS((1, LANE), F32), SDS((1, LANE), F32),
                   SDS((1, LANE), F32)] + [SDS(a.shape, a.dtype) for a in scatter],
        input_output_aliases={0: 0},
        scratch_shapes=[pltpu.VMEM((eg * NH, LANE, LANE), F32)] + (_exchange_sems(ns) if ns else []),
        compiler_params=_cp(56),
    )(dproj.reshape(bsz, t, NP), qkv3, qkv3, qkv3, proj3, proj3, shist, tsaved, dyb.reshape(bsz, t, D), alog, dtb, og,
      *scatter)
    return (dproj.reshape(n, NP), dqkvn.reshape(n, 3 * D), *rest)


def _c_chunk(us, vs, zs, lgs, lbs, ws, bsb):
    gv = [jax.nn.gelu(v) for v in vs]
    width = LANE * len(gv)
    mu = sum(jnp.sum(x, axis=-1, keepdims=True) for x in gv) / width
    var = sum(jnp.sum(jnp.square(x - mu), axis=-1, keepdims=True) for x in gv) / width
    rstd = lax.rsqrt(var + EPS)
    r = lax.broadcasted_iota(jnp.int32, (SG, SG), 0)
    c = lax.broadcasted_iota(jnp.int32, (SG, SG), 1)
    out = []
    for j in range(len(gv)):
        nrm = (gv[j] - mu) * rstd * lgs[j] + lbs[j]
        mixed = jnp.dot(jnp.where(r >= c, ws[j], 0.0), nrm, preferred_element_type=F32) + bsb[j]
        out.append(jax.nn.gelu(us[j]) * mixed * jax.nn.silu(zs[j]))
    return out


C_CHUNKS = 2


def _c_args(r, u_ref, v_ref, z_ref, lg_ref, lb_ref, ws_ref, bs_ref):
    sl = [slice(LANE * j, LANE * (j + 1)) for j in range(4)]
    rows = slice(SG * r, SG * (r + 1))
    return ([u_ref[rows, s] for s in sl], [v_ref[rows, s] for s in sl], [z_ref[rows, s] for s in sl],
            [lg_ref[:, s] for s in sl], [lb_ref[:, s] for s in sl],
            [ws_ref[j] for j in range(4)], [bs_ref[j] for j in range(4)])


def _c_fwd(proj, lg, lb, ws, bsb, n):
    cb = C_C // 512
    tr = SG * C_CHUNKS

    def body(u_ref, v_ref, z_ref, lg_ref, lb_ref, ws_ref, bs_ref, y_ref):
        for r in range(C_CHUNKS):
            outs = _c_chunk(*_c_args(r, u_ref, v_ref, z_ref, lg_ref, lb_ref, ws_ref, bs_ref))
            for j, o in enumerate(outs):
                y_ref[SG * r:SG * (r + 1), LANE * j:LANE * (j + 1)] = o.astype(BF16)

    return pl.pallas_call(
        body, name="c_fwd", grid=(n // tr,),
        in_specs=[pl.BlockSpec((tr, 512), lambda i: (i, cb)), pl.BlockSpec((tr, 512), lambda i: (i, cb + 1)),
                  pl.BlockSpec((tr, 512), lambda i: (i, cb + 2)), _full((1, 512)), _full((1, 512)),
                  _full((4, SG, SG)), _full((4, SG, SG))],
        out_specs=pl.BlockSpec((tr, 512), lambda i: (i, 0)),
        out_shape=SDS((n, 512), BF16), compiler_params=_cp(32),
    )(proj, proj, proj, lg, lb, ws, bsb)


def _c_bwd(dproj, proj, dy, lg, lb, ws, bsb, n):
    cb = C_C // 512
    tr = SG * C_CHUNKS

    def body(dp_any, u_ref, v_ref, z_ref, dy_ref, lg_ref, lb_ref, ws_ref, bs_ref,
             dp_ref, glg_ref, glb_ref, gws_ref, gbs_ref):
        @pl.when(pl.program_id(0) == 0)
        def _():
            glg_ref[...] = jnp.zeros_like(glg_ref)
            glb_ref[...] = jnp.zeros_like(glb_ref)
            gws_ref[...] = jnp.zeros_like(gws_ref)
            gbs_ref[...] = jnp.zeros_like(gbs_ref)

        for r in range(C_CHUNKS):
            rows = slice(SG * r, SG * (r + 1))
            _, vjp = jax.vjp(_c_chunk, *_c_args(r, u_ref, v_ref, z_ref, lg_ref, lb_ref, ws_ref, bs_ref))
            dus, dvs, dzs, dlgs, dlbs, dwss, dbss = vjp([dy_ref[rows, LANE * j:LANE * (j + 1)] for j in range(4)])
            for j in range(4):
                sl = slice(LANE * j, LANE * (j + 1))
                dp_ref[rows, LANE * j:LANE * (j + 1)] = dus[j].astype(BF16)
                dp_ref[rows, 512 + LANE * j:512 + LANE * (j + 1)] = dvs[j].astype(BF16)
                dp_ref[rows, 1024 + LANE * j:1024 + LANE * (j + 1)] = dzs[j].astype(BF16)
                glg_ref[:, sl] += dlgs[j]
                glb_ref[:, sl] += dlbs[j]
                gws_ref[j] += dwss[j]
                gbs_ref[j] += jnp.broadcast_to(jnp.sum(dbss[j], axis=-1, keepdims=True), (SG, SG))

    return pl.pallas_call(
        body, name="c_bwd", grid=(n // tr,),
        in_specs=[pl.BlockSpec(memory_space=pl.ANY),
                  pl.BlockSpec((tr, 512), lambda i: (i, cb)), pl.BlockSpec((tr, 512), lambda i: (i, cb + 1)),
                  pl.BlockSpec((tr, 512), lambda i: (i, cb + 2)), pl.BlockSpec((tr, 512), lambda i: (i, 0)),
                  _full((1, 512)), _full((1, 512)), _full((4, SG, SG)), _full((4, SG, SG))],
        out_specs=[pl.BlockSpec((tr, 1536), lambda i: (i, C_C // 1536)),
                   _full((1, 512)), _full((1, 512)), _full((4, SG, SG)), _full((4, SG, SG))],
        out_shape=[SDS((n, NP), BF16), SDS((1, 512), F32), SDS((1, 512), F32), SDS((4, SG, SG), F32), SDS((4, SG, SG), F32)],
        input_output_aliases={0: 0}, compiler_params=_cp(32),
    )(dproj, proj, proj, proj, dy, lg, lb, ws, bsb)


def _merge_fwd(x2d, ya, yb, yc, proj, ap, bp, cp, wo):
    n = x2d.shape[0]
    tm = _tile(n, 512)
    gb = C_G // D

    def body(x_ref, ya_ref, yb_ref, yc_ref, g0_ref, g1_ref, g2_ref, ap_ref, bp_ref, cp_ref, wo_ref, o_ref):
        merged = (jax.nn.sigmoid(g0_ref[...]) * _bdot(ya_ref[...], ap_ref[...])
                  + jax.nn.sigmoid(g1_ref[...]) * _bdot(yb_ref[...], bp_ref[...])
                  + jax.nn.sigmoid(g2_ref[...]) * _bdot(yc_ref[...], cp_ref[...]))
        o_ref[...] = x_ref[...] + _bdot(merged, wo_ref[...])

    def rows(w):
        return pl.BlockSpec((tm, w), lambda i: (i, 0))

    return pl.pallas_call(
        body, name="merge_fwd", grid=(n // tm,),
        in_specs=[rows(D), rows(512), rows(D), rows(512),
                  pl.BlockSpec((tm, D), lambda i: (i, gb)), pl.BlockSpec((tm, D), lambda i: (i, gb + 1)),
                  pl.BlockSpec((tm, D), lambda i: (i, gb + 2)),
                  _resident((512, D)), _resident((D, D)), _resident((512, D)), _resident((D, D))],
        out_specs=rows(D), out_shape=SDS((n, D), F32), compiler_params=_cp(48),
    )(x2d, ya, yb, yc, proj, proj, proj, ap, bp, cp, wo)


def _merge_bwd(dxo, ya, yb, yc, proj, ap, bp, cp, apt, bpt, cpt, wot):
    n = dxo.shape[0]
    tm = _tile(n, 256)
    gb = C_G // D

    def body(d_ref, ya_ref, yb_ref, yc_ref, g0_ref, g1_ref, g2_ref, ap_ref, bp_ref, cp_ref,
             apt_ref, bpt_ref, cpt_ref, wot_ref,
             dp_ref, dya_ref, dyb_ref, dyc_ref, dpa_ref, dpb_ref, dpc_ref, mg_ref):
        dm = _bdot(d_ref[...], wot_ref[...])
        merged = None
        for j, (g_ref, y_ref, w_ref, wt_ref, dy_ref, dpj_ref) in enumerate((
                (g0_ref, ya_ref, ap_ref, apt_ref, dya_ref, dpa_ref),
                (g1_ref, yb_ref, bp_ref, bpt_ref, dyb_ref, dpb_ref),
                (g2_ref, yc_ref, cp_ref, cpt_ref, dyc_ref, dpc_ref))):
            s = jax.nn.sigmoid(g_ref[...])
            pj = _bdot(y_ref[...], w_ref[...])
            merged = s * pj if merged is None else merged + s * pj
            dp_ref[:, D * j:D * (j + 1)] = (dm * pj * s * (1.0 - s)).astype(BF16)
            dpj = (dm * s).astype(BF16)
            dpj_ref[...] = dpj
            dy_ref[...] = jnp.dot(dpj, wt_ref[...], preferred_element_type=F32)
        mg_ref[...] = merged

    def rows(w):
        return pl.BlockSpec((tm, w), lambda i: (i, 0))

    return pl.pallas_call(
        body, name="merge_bwd", grid=(n // tm,),
        in_specs=[rows(D), rows(512), rows(D), rows(512),
                  pl.BlockSpec((tm, D), lambda i: (i, gb)), pl.BlockSpec((tm, D), lambda i: (i, gb + 1)),
                  pl.BlockSpec((tm, D), lambda i: (i, gb + 2)),
                  _resident((512, D)), _resident((D, D)), _resident((512, D)),
                  _resident((D, 512)), _resident((D, D)), _resident((D, 512)), _resident((D, D))],
        out_specs=[pl.BlockSpec((tm, 3 * D), lambda i: (i, C_G // (3 * D))), rows(512), rows(D), rows(512),
                   rows(D), rows(D), rows(D), rows(D)],
        out_shape=[SDS((n, NP), BF16), SDS((n, 512), F32), SDS((n, D), F32), SDS((n, 512), F32),
                   SDS((n, D), BF16), SDS((n, D), BF16), SDS((n, D), BF16), SDS((n, D), F32)],
        compiler_params=_cp(56),
    )(dxo, ya, yb, yc, proj, proj, proj, ap, bp, cp, apt, bpt, cpt, wot)


def _sum_parts(p_ref):
    g = p_ref[0].astype(F32)
    for s in range(1, NDEV):
        g = g + p_ref[s].astype(F32)
    return g


def _adamw(g, w, m, v):
    nm = ADAM_B1 * m + (1.0 - ADAM_B1) * g
    nv = ADAM_B2 * v + (1.0 - ADAM_B2) * jnp.square(g)
    nm_hat = nm / (1.0 - ADAM_B1 ** ADAM_STEP)
    nv_hat = nv / (1.0 - ADAM_B2 ** ADAM_STEP)
    return -ADAM_LR * (nm_hat / (jnp.sqrt(nv_hat) + ADAM_EPS) + ADAM_WD * w), nm, nv


def _reduce_adamw(parts, w, m, v, name):
    r, c = w.shape
    tr = _tile(r, 128)

    def body(p_ref, w_ref, m_ref, v_ref, g_ref, d_ref, nm_ref, nv_ref):
        g = _sum_parts(p_ref)
        g_ref[...] = g
        d_ref[...], nm_ref[...], nv_ref[...] = _adamw(g, w_ref[...], m_ref[...], v_ref[...])

    blk = pl.BlockSpec((tr, c), lambda i: (i, 0))
    return pl.pallas_call(
        body, name=name, grid=(r // tr,),
        in_specs=[pl.BlockSpec((NDEV, tr, c), lambda i: (0, i, 0)), blk, blk, blk],
        out_specs=[blk, blk, blk, blk], out_shape=[SDS((r, c), F32)] * 4, compiler_params=_cp(48),
    )(parts, w, m, v)


def _reduce_adamw_leaves(parts, ws, ms, vs, name):
    nleaf = len(ws)
    counts = [len(p) if isinstance(p, (list, tuple)) else 0 for p in parts]
    flat = [a for p in parts for a in (p if isinstance(p, (list, tuple)) else [p])]

    def body(*refs):
        p_refs, rest = refs[:len(flat)], refs[len(flat):]
        w_refs, m_refs, v_refs = rest[:nleaf], rest[nleaf:2 * nleaf], rest[2 * nleaf:3 * nleaf]
        outs = rest[3 * nleaf:]
        at = 0
        for i in range(nleaf):
            g_ref, d_ref, nm_ref, nv_ref = outs[i], outs[nleaf + i], outs[2 * nleaf + i], outs[3 * nleaf + i]
            for idx in (range(counts[i]) if counts[i] else [Ellipsis]):
                g = _sum_parts(p_refs[at])
                at += 1
                g_ref[idx] = g
                d_ref[idx], nm_ref[idx], nv_ref[idx] = _adamw(g, w_refs[i][idx], m_refs[i][idx], v_refs[i][idx])

    vm = pl.BlockSpec(memory_space=pltpu.VMEM)
    outs = pl.pallas_call(
        body, name=name, in_specs=[vm] * (len(flat) + 3 * nleaf), out_specs=[vm] * (4 * nleaf),
        out_shape=[SDS(w.shape, F32) for w in ws] * 4, compiler_params=_cp(56),
    )(*flat, *ws, *ms, *vs)
    return [outs[j * nleaf:(j + 1) * nleaf] for j in range(4)]


def _unshard(name, g):
    if name in ROW_SHARDED:
        return g.reshape(g.shape[0] * g.shape[1], g.shape[2])
    g = jnp.moveaxis(g, 0, 1)
    return g.reshape(g.shape[0], g.shape[1] * g.shape[2])


def _reshard(name, full):
    r, c = full.shape
    if name in ROW_SHARDED:
        return full.reshape(NDEV, r // NDEV, c)
    return jnp.moveaxis(full.reshape(r, NDEV, c // NDEV), 1, 0)


def _w_in_to_padded(slabs):
    pieces = []
    for lo, hi, _ in sorted(SEGMENTS, key=lambda s: s[2]):
        for d in range(NDEV):
            a, b = max(lo, d * W_SHARD), min(hi, (d + 1) * W_SHARD)
            if a < b:
                pieces.append(slabs[d, :, a - d * W_SHARD:b - d * W_SHARD])
    pieces.append(jnp.zeros(slabs.shape[1:2] + (NP - C_BA - 16,), slabs.dtype))
    return jnp.concatenate(pieces, axis=-1)


def _w_in_from_padded(g):
    slabs = []
    for d in range(NDEV):
        pieces = []
        for lo, hi, pstart in SEGMENTS:
            a, b = max(lo, d * W_SHARD), min(hi, (d + 1) * W_SHARD)
            if a < b:
                pieces.append(g[:, pstart + a - lo:pstart + b - lo])
        pieces.append(jnp.zeros(g.shape[:1] + (W_SHARD_PAD - W_SHARD,), g.dtype))
        slabs.append(jnp.concatenate(pieces, axis=-1))
    return jnp.stack(slabs)


def _pad_w_in(w):
    return jnp.pad(w, ((0, 0), (0, W_SHARD_PAD - W_SHARD)))


def _lane_row(vec8, offset):
    return jnp.pad(vec8, (offset, LANE - NH - offset))[None]


def kernel(x, norm_g, w_in, a_dw, a_dw_b, a_ln_g, a_ln_b, a_proj, b_conv, b_a_log, b_dt_bias, b_onorm_g, b_proj, c_ln_g, c_ln_b, c_ws, c_bs, c_proj, w_out, final_g, loss_target, m_norm_g, m_w_in, m_a_dw, m_a_dw_b, m_a_ln_g, m_a_ln_b, m_a_proj, m_b_conv, m_b_a_log, m_b_dt_bias, m_b_onorm_g, m_b_proj, m_c_ln_g, m_c_ln_b, m_c_ws, m_c_bs, m_c_proj, m_w_out, m_final_g, v_norm_g, v_w_in, v_a_dw, v_a_dw_b, v_a_ln_g, v_a_ln_b, v_a_proj, v_b_conv, v_b_a_log, v_b_dt_bias, v_b_onorm_g, v_b_proj, v_c_ln_g, v_c_ln_b, v_c_ws, v_c_bs, v_c_proj, v_w_out, v_final_g):
    wts = dict(norm_g=norm_g, w_in=w_in, a_dw=a_dw, a_dw_b=a_dw_b, a_ln_g=a_ln_g, a_ln_b=a_ln_b, a_proj=a_proj,
               b_conv=b_conv, b_a_log=b_a_log, b_dt_bias=b_dt_bias, b_onorm_g=b_onorm_g, b_proj=b_proj,
               c_ln_g=c_ln_g, c_ln_b=c_ln_b, c_ws=c_ws, c_bs=c_bs, c_proj=c_proj, w_out=w_out, final_g=final_g)
    mom = dict(norm_g=m_norm_g, w_in=m_w_in, a_dw=m_a_dw, a_dw_b=m_a_dw_b, a_ln_g=m_a_ln_g, a_ln_b=m_a_ln_b,
               a_proj=m_a_proj, b_conv=m_b_conv, b_a_log=m_b_a_log, b_dt_bias=m_b_dt_bias, b_onorm_g=m_b_onorm_g,
               b_proj=m_b_proj, c_ln_g=m_c_ln_g, c_ln_b=m_c_ln_b, c_ws=m_c_ws, c_bs=m_c_bs, c_proj=m_c_proj,
               w_out=m_w_out, final_g=m_final_g)
    vel = dict(norm_g=v_norm_g, w_in=v_w_in, a_dw=v_a_dw, a_dw_b=v_a_dw_b, a_ln_g=v_a_ln_g, a_ln_b=v_a_ln_b,
               a_proj=v_a_proj, b_conv=v_b_conv, b_a_log=v_b_a_log, b_dt_bias=v_b_dt_bias, b_onorm_g=v_b_onorm_g,
               b_proj=v_b_proj, c_ln_g=v_c_ln_g, c_ln_b=v_c_ln_b, c_ws=v_c_ws, c_bs=v_c_bs, c_proj=v_c_proj,
               w_out=v_w_out, final_g=v_final_g)

    bsz, t, _ = x.shape
    n = bsz * t
    depth = norm_g.shape[0]
    x2d = x.reshape(n, D)
    tgt = loss_target.reshape(n, D)

    def weight_blocks(l):
        return [_pad_w_in(w_in[l].astype(BF16))] + [wts[k][l].astype(BF16) for k in BIG_REST]

    def matmul_weights(w_in_all, *rest):
        got = {k: _unshard(k, g) for k, g in zip(BIG_REST, rest)}
        got['wp'] = _w_in_to_padded(w_in_all)
        return got

    w_in_first, *conv_all = _all_gather(weight_blocks(0)[:1] + [wts[k] for k in SMALL], "gather_first_weights")
    conv_w = {k: jnp.stack([_unshard(k, g[:, l]) for l in range(depth)]) for k, g in zip(SMALL, conv_all)}
    a_dw32 = jnp.pad(conv_w['a_dw'], ((0, 0), (0, 32 - A_K), (0, 0)))
    b_conv8 = jnp.pad(conv_w['b_conv'], ((0, 0), (0, 8 - B_K), (0, 0)))
    bsb = jnp.broadcast_to(c_bs[..., None], c_bs.shape + (SG,))
    full = [{'wp': _w_in_to_padded(w_in_first)}]

    saved = []
    xl = x2d
    for l in range(depth):
        alog, dtb = _lane_row(b_a_log[l], NH), _lane_row(b_dt_bias[l], NH)
        proj, h = _inproj(xl, norm_g[l][None], full[l]['wp'])
        ya, conv_a, *own = _a_fwd(proj, a_dw32[l], a_dw_b[l][None], a_ln_g[l][None], a_ln_b[l][None], bsz, t,
                                  gather=weight_blocks(0)[1:] if l == 0 else ())
        if own:
            full[0].update({k: _unshard(k, g) for k, g in zip(BIG_REST, own)})
        ahead = weight_blocks(l + 1) if l + 1 < depth else []
        qkvn, conv_b, *nxt_small = _bprep_fwd(proj, b_conv8[l], bsz, t, gather=ahead[1:])
        yb, shist, tsave, *nxt_w_in = _delta_fwd(qkvn, proj, alog, dtb, b_onorm_g[l][None], bsz, t, gather=ahead[:1])
        if ahead:
            full.append(matmul_weights(*nxt_w_in, *nxt_small))
        yc = _c_fwd(proj, c_ln_g[l][None], c_ln_b[l][None], c_ws[l], bsb[l], n)
        xn = _merge_fwd(xl, ya, yb, yc, proj, full[l]['a_proj'], full[l]['b_proj'], full[l]['c_proj'], full[l]['w_out'])
        saved.append((xl, proj, h, ya, yb, yc, qkvn, shist, tsave, alog, dtb, conv_a, conv_b))
        xl = xn

    dx, g_final, loss_blk = _loss_head(xl, final_g[None], tgt)
    loss = lax.psum(loss_blk[0, 0], ("x", "y", "c"))

    gfull = {k: [None] * depth for k in WEIGHTS if k != 'final_g'}
    recv = [{} for _ in range(depth)]
    sharded = ['w_in'] + SHARDED_REST
    late = ['w_in', 'b_conv']
    early = [k for k in sharded if k not in late]

    def grad_slabs(l, names):
        return [(_w_in_from_padded(gfull[k][l]) if k == 'w_in' else _reshard(k, gfull[k][l])).astype(BF16) for k in names]

    for l in reversed(range(depth)):
        xl, proj, h, ya, yb, yc, qkvn, shist, tsave, alog, dtb, conv_a, conv_b = saved[l]
        ap, bp, cp, wo = full[l]['a_proj'], full[l]['b_proj'], full[l]['c_proj'], full[l]['w_out']
        dproj, dya, dyb, dyc, dpa, dpb, dpc, merged = _merge_bwd(dx, ya, yb, yc, proj, ap, bp, cp, ap.T, bp.T, cp.T, wo.T)
        gfull['a_proj'][l] = _mm_tn(ya, dpa, "grad_a_proj")
        gfull['b_proj'][l] = _mm_tn(yb, dpb, "grad_b_proj")
        gfull['c_proj'][l] = _mm_tn(yc, dpc, "grad_c_proj")
        gfull['w_out'][l] = _mm_tn(merged, dx, "grad_w_out")
        dproj, g_clg, g_clb, g_cws, g_cbs = _c_bwd(dproj, proj, dyc, c_ln_g[l][None], c_ln_b[l][None], c_ws[l], bsb[l], n)
        dproj, g_adw, g_adb, g_alg, g_alb = _a_bwd(dproj, proj, conv_a, dya, a_dw32[l], a_ln_g[l][None], a_ln_b[l][None],
                                                   bsz, t)
        gfull['a_dw'][l] = g_adw[:A_K]
        riders = ([(l + 1, sharded)] if l + 1 < depth else []) + ([(0, early)] if l == 0 else [])
        dproj, dqkvn, g_alog, g_dt, g_og, *got = _delta_bwd(
            dproj, qkvn, proj, shist, tsave, dyb, alog, dtb, b_onorm_g[l][None], bsz, t,
            scatter=[s for ll, names in riders for s in grad_slabs(ll, names)])
        for ll, names in riders:
            for k in names:
                recv[ll][k] = got.pop(0)
        dproj, g_bconv = _bprep_bwd(dproj, proj, conv_b, dqkvn, b_conv8[l], bsz, t)
        gfull['w_in'][l] = _mm_tn(h, dproj, "grad_w_in")
        gfull['b_conv'][l] = g_bconv[:B_K]
        dx, g_ng, *got = _inproj_bwd(dproj, full[l]['wp'], xl, norm_g[l][None], dx,
                                     scatter=grad_slabs(0, late) if l == 0 else ())
        if got:
            recv[0].update(zip(late, got))
        gfull['norm_g'][l] = g_ng[0]
        gfull['a_dw_b'][l], gfull['a_ln_g'][l], gfull['a_ln_b'][l] = g_adb[0], g_alg[0], g_alb[0]
        gfull['b_a_log'][l], gfull['b_dt_bias'][l] = g_alog[0, NH:2 * NH], g_dt[0, NH:2 * NH]
        gfull['b_onorm_g'][l] = g_og[0]
        gfull['c_ln_g'][l], gfull['c_ln_b'][l] = g_clg[0], g_clb[0]
        gfull['c_ws'][l], gfull['c_bs'][l] = g_cws, g_cbs[:, :, 0]
    grad_x = dx.reshape(bsz, t, D)

    outs_w = [_reduce_adamw(recv[l]['w_in'], _pad_w_in(w_in[l]), _pad_w_in(m_w_in[l]), _pad_w_in(v_w_in[l]), "adamw_w_in")
              for l in range(depth)]
    outs_s = _reduce_adamw_leaves([[recv[l][k] for l in range(depth)] for k in SHARDED_REST],
                                  [wts[k] for k in SHARDED_REST], [mom[k] for k in SHARDED_REST],
                                  [vel[k] for k in SHARDED_REST], "adamw_sharded")

    def upto3d(a):
        return a[None] if a.ndim == 1 else a.reshape((-1,) + a.shape[-2:]) if a.ndim > 3 else a

    grepl = [upto3d(jnp.stack(gfull[k]) if k != 'final_g' else g_final[0]) for k in REPL]
    outs_r = _reduce_adamw_leaves(_all_gather(grepl, "gather_replicated_grads"), [upto3d(wts[k]) for k in REPL],
                                  [upto3d(mom[k]) for k in REPL], [upto3d(vel[k]) for k in REPL], "adamw_replicated")

    res = []
    for j in range(4):
        leaves = {k: outs_r[j][i].reshape(wts[k].shape) for i, k in enumerate(REPL)}
        leaves.update({k: outs_s[j][i] for i, k in enumerate(SHARDED_REST)})
        leaves['w_in'] = jnp.stack([outs_w[l][j][:, :W_SHARD] for l in range(depth)])
        res.append([leaves[k] for k in WEIGHTS])
    grads, deltas, new_m, new_v = res
    return (loss, grad_x, *grads, *deltas, *new_m, *new_v)
```
